```python
import math
import jax, jax.numpy as jnp
from jax import lax
import numpy as np

D_MODEL = 1024
BATCH = 8
SEQ = 4096
DEPTH = 2

N_MIXERS = 2
N_POOL_LAYERS = (DEPTH + 1) // 2
N_SSM_LAYERS = DEPTH // 2
RMS_EPS = 1e-5

POOL_WINDOWS = (2, 4, 8, 16)
N_POOL_GROUPS = len(POOL_WINDOWS)
POOL_GROUP = D_MODEL // N_POOL_GROUPS

SSM_EXPAND = 2
D_INNER = SSM_EXPAND * D_MODEL
HEAD_DIM = 64
N_HEADS = D_INNER // HEAD_DIM
N_GROUPS = 4
HEADS_PER_GROUP = N_HEADS // N_GROUPS
D_STATE = 128
CONV_K = 4
CHUNK = 128
CONV_DIM = D_INNER + 2 * N_GROUPS * D_STATE
IN_PROJ_DIM = D_INNER + CONV_DIM + N_HEADS

D_FF = 4 * D_MODEL

kernel_name = "pool_ssd_interleaved_hybrid"


def rms_norm(x, g):
    xf = x.astype(jnp.float32)
    y = xf * lax.rsqrt(jnp.mean(xf * xf, axis=-1, keepdims=True) + RMS_EPS)
    return (y * g.astype(jnp.float32)).astype(x.dtype)


def pool_mixer(h, w_pool, b_pool, scale):
    b, s, d = h.shape
    hf = h.astype(jnp.float32)
    cs = jnp.pad(jnp.cumsum(hf, axis=1), ((0, 0), (1, 0), (0, 0)))
    t = jnp.arange(s)
    hg = hf.reshape(b, s, N_POOL_GROUPS, POOL_GROUP)
    csg = cs.reshape(b, s + 1, N_POOL_GROUPS, POOL_GROUP)
    pooled = []
    for gi, w in enumerate(POOL_WINDOWS):
        c = csg[:, :, gi]
        lower = c[:, jnp.maximum(t + 1 - w, 0)]
        count = jnp.minimum(t + 1, w).astype(jnp.float32)
        pooled.append((c[:, 1:] - lower) / count[None, :, None])
    mixed = jnp.stack(pooled, axis=2) - hg
    out = jnp.einsum("bsgi,gio->bsgo", mixed, w_pool.astype(jnp.float32)).reshape(b, s, d)
    out = (out + b_pool.astype(jnp.float32)) * scale.astype(jnp.float32)
    return out.astype(h.dtype)


def causal_dwconv(u, w, bias):
    s = u.shape[1]
    up = jnp.pad(u, ((0, 0), (CONV_K - 1, 0), (0, 0)))
    out = up[:, 0:s] * w[0]
    for k in range(1, CONV_K):
        out = out + up[:, k:k + s] * w[k]
    return out + bias


def ssd_scan(xs, dt, a, bm, cm):
    b, s, g, r, p = xs.shape
    n = bm.shape[-1]
    nc = s // CHUNK
    x = xs.astype(jnp.float32).reshape(b, nc, CHUNK, g, r, p)
    dtc = dt.astype(jnp.float32).reshape(b, nc, CHUNK, g, r)
    bc = bm.astype(jnp.float32).reshape(b, nc, CHUNK, g, n)
    cc = cm.astype(jnp.float32).reshape(b, nc, CHUNK, g, n)
    xdt = x * dtc[..., None]
    adt = jnp.moveaxis(dtc * a.astype(jnp.float32), 2, -1)
    a_cs = jnp.cumsum(adt, axis=-1)
    causal = jnp.tril(jnp.ones((CHUNK, CHUNK), dtype=bool))
    seg = a_cs[..., :, None] - a_cs[..., None, :]
    decay = jnp.exp(jnp.where(causal, seg, -jnp.inf))
    cb = jnp.einsum("bclgn,bcsgn->bcgls", cc, bc)
    y_diag = jnp.einsum("bcgls,bcgrls,bcsgrp->bclgrp", cb, decay, xdt)
    decay_to_end = jnp.exp(a_cs[..., -1:] - a_cs)
    states = jnp.einsum("bclgn,bcgrl,bclgrp->bcgrpn", bc, decay_to_end, xdt)
    chunk_decay = jnp.exp(a_cs[..., -1])

    def step(hstate, inp):
        st, dec = inp
        return hstate * dec[..., None, None] + st, hstate

    h0 = jnp.zeros((b, g, r, p, n), jnp.float32)
    _, prev = lax.scan(step, h0, (jnp.moveaxis(states, 1, 0), jnp.moveaxis(chunk_decay, 1, 0)))
    prev = jnp.moveaxis(prev, 0, 1)
    y_off = jnp.einsum("bclgn,bcgrpn,bcgrl->bclgrp", cc, prev, jnp.exp(a_cs))
    return (y_diag + y_off).reshape(b, s, g, r, p)


def ssd_mixer(h, w_in, conv_w, conv_b, dt_bias, a_log, d_skip, norm_g, w_out):
    b, s, _ = h.shape
    zxbcdt = h @ w_in
    z = zxbcdt[..., :D_INNER]
    xbc = zxbcdt[..., D_INNER:D_INNER + CONV_DIM]
    dt_raw = zxbcdt[..., D_INNER + CONV_DIM:]
    xbc = jax.nn.silu(causal_dwconv(xbc, conv_w, conv_b))
    xs = xbc[..., :D_INNER].reshape(b, s, N_GROUPS, HEADS_PER_GROUP, HEAD_DIM)
    bm = xbc[..., D_INNER:D_INNER + N_GROUPS * D_STATE].reshape(b, s, N_GROUPS, D_STATE)
    cm = xbc[..., D_INNER + N_GROUPS * D_STATE:].reshape(b, s, N_GROUPS, D_STATE)
    dt = jax.nn.softplus(dt_raw.astype(jnp.float32) + dt_bias.astype(jnp.float32))
    dt = dt.reshape(b, s, N_GROUPS, HEADS_PER_GROUP)
    a = -jnp.exp(a_log.astype(jnp.float32)).reshape(N_GROUPS, HEADS_PER_GROUP)
    y = ssd_scan(xs, dt, a, bm, cm)
    y = y + d_skip.astype(jnp.float32).reshape(N_GROUPS, HEADS_PER_GROUP)[..., None] * xs.astype(jnp.float32)
    y = y.reshape(b, s, D_INNER) * jax.nn.silu(z.astype(jnp.float32))
    yg = y.reshape(b, s, N_GROUPS, D_INNER // N_GROUPS)
    yg = yg * lax.rsqrt(jnp.mean(yg * yg, axis=-1, keepdims=True) + RMS_EPS)
    y = (yg.reshape(b, s, D_INNER) * norm_g.astype(jnp.float32)).astype(h.dtype)
    return y @ w_out


def sq_relu_mlp(h, w1, w2):
    u = jax.nn.relu(h @ w1)
    return (u * u) @ w2


def _fwd_setup_inputs(seed: int = 0) -> dict:
    key = jax.random.key(seed)
    ks = jax.random.split(key, 20)
    f32 = jnp.float32
    nrm = lambda k, shape, scale: jax.random.normal(k, shape, f32) * scale
    x = jax.random.normal(ks[0], (BATCH, SEQ, D_MODEL), f32)
    norm_mix_g = 1.0 + nrm(ks[1], (DEPTH, D_MODEL), 0.05)
    norm_mlp_g = 1.0 + nrm(ks[2], (DEPTH, D_MODEL), 0.05)
    pool_w = nrm(ks[3], (N_POOL_LAYERS, N_POOL_GROUPS, POOL_GROUP, POOL_GROUP), POOL_GROUP ** -0.5)
    pool_b = nrm(ks[4], (N_POOL_LAYERS, D_MODEL), 0.02)
    pool_scale = 0.5 + nrm(ks[5], (N_POOL_LAYERS, D_MODEL), 0.05)
    ssm_w_in = nrm(ks[6], (N_SSM_LAYERS, D_MODEL, IN_PROJ_DIM), D_MODEL ** -0.5)
    ssm_conv_w = nrm(ks[7], (N_SSM_LAYERS, CONV_K, CONV_DIM), CONV_K ** -0.5)
    ssm_conv_b = nrm(ks[8], (N_SSM_LAYERS, CONV_DIM), 0.02)
    dt0 = jnp.exp(jax.random.uniform(ks[9], (N_SSM_LAYERS, N_HEADS), f32)
                  * (math.log(0.1) - math.log(0.001)) + math.log(0.001))
    ssm_dt_bias = dt0 + jnp.log(-jnp.expm1(-dt0))
    ssm_a_log = jnp.log(jax.random.uniform(ks[10], (N_SSM_LAYERS, N_HEADS), f32, 1.0, 16.0))
    ssm_d = 1.0 + nrm(ks[11], (N_SSM_LAYERS, N_HEADS), 0.1)
    ssm_norm_g = 1.0 + nrm(ks[12], (N_SSM_LAYERS, D_INNER), 0.05)
    ssm_w_out = nrm(ks[13], (N_SSM_LAYERS, D_INNER, D_MODEL), D_INNER ** -0.5)
    mlp_w1 = nrm(ks[14], (DEPTH, D_MODEL, D_FF), D_MODEL ** -0.5)
    mlp_w2 = nrm(ks[15], (DEPTH, D_FF, D_MODEL), D_FF ** -0.5)
    final_g = 1.0 + nrm(ks[16], (D_MODEL,), 0.05)
    return {"x": x, "norm_mix_g": norm_mix_g, "norm_mlp_g": norm_mlp_g,
            "pool_w": pool_w, "pool_b": pool_b, "pool_scale": pool_scale,
            "ssm_w_in": ssm_w_in, "ssm_conv_w": ssm_conv_w, "ssm_conv_b": ssm_conv_b,
            "ssm_dt_bias": ssm_dt_bias, "ssm_a_log": ssm_a_log, "ssm_d": ssm_d,
            "ssm_norm_g": ssm_norm_g, "ssm_w_out": ssm_w_out,
            "mlp_w1": mlp_w1, "mlp_w2": mlp_w2, "final_g": final_g}


def _fwd_reference(x, norm_mix_g, norm_mlp_g, pool_w, pool_b, pool_scale,
              ssm_w_in, ssm_conv_w, ssm_conv_b, ssm_dt_bias, ssm_a_log, ssm_d,
              ssm_norm_g, ssm_w_out, mlp_w1, mlp_w2, final_g):
    h = x
    for i in range(DEPTH):
        hn = rms_norm(h, norm_mix_g[i])
        if i % N_MIXERS == 0:
            j = i // N_MIXERS
            mix = pool_mixer(hn, pool_w[j], pool_b[j], pool_scale[j])
        else:
            j = i // N_MIXERS
            mix = ssd_mixer(hn, ssm_w_in[j], ssm_conv_w[j], ssm_conv_b[j], ssm_dt_bias[j],
                            ssm_a_log[j], ssm_d[j], ssm_norm_g[j], ssm_w_out[j])
        h = h + mix
        h = h + sq_relu_mlp(rms_norm(h, norm_mlp_g[i]), mlp_w1[i], mlp_w2[i])
    return rms_norm(h, final_g)


import jax as _jax
import jax.numpy as _jnp

TWIN_FORMAT = 'train_step'
FWD_PARAMS = ['x', 'norm_mix_g', 'norm_mlp_g', 'pool_w', 'pool_b', 'pool_scale', 'ssm_w_in', 'ssm_conv_w', 'ssm_conv_b', 'ssm_dt_bias', 'ssm_a_log', 'ssm_d', 'ssm_norm_g', 'ssm_w_out', 'mlp_w1', 'mlp_w2', 'final_g']
TWIN_WEIGHTS = ['norm_mix_g', 'norm_mlp_g', 'pool_w', 'pool_b', 'pool_scale', 'ssm_w_in', 'ssm_conv_w', 'ssm_conv_b', 'ssm_dt_bias', 'ssm_a_log', 'ssm_d', 'ssm_norm_g', 'ssm_w_out', 'mlp_w1', 'mlp_w2', 'final_g']
TWIN_DIFF_INPUT = 'x'
TWIN_INPUTS = ['x', 'norm_mix_g', 'norm_mlp_g', 'pool_w', 'pool_b', 'pool_scale', 'ssm_w_in', 'ssm_conv_w', 'ssm_conv_b', 'ssm_dt_bias', 'ssm_a_log', 'ssm_d', 'ssm_norm_g', 'ssm_w_out', 'mlp_w1', 'mlp_w2', 'final_g', 'loss_target', 'm_norm_mix_g', 'm_norm_mlp_g', 'm_pool_w', 'm_pool_b', 'm_pool_scale', 'm_ssm_w_in', 'm_ssm_conv_w', 'm_ssm_conv_b', 'm_ssm_dt_bias', 'm_ssm_a_log', 'm_ssm_d', 'm_ssm_norm_g', 'm_ssm_w_out', 'm_mlp_w1', 'm_mlp_w2', 'm_final_g', 'v_norm_mix_g', 'v_norm_mlp_g', 'v_pool_w', 'v_pool_b', 'v_pool_scale', 'v_ssm_w_in', 'v_ssm_conv_w', 'v_ssm_conv_b', 'v_ssm_dt_bias', 'v_ssm_a_log', 'v_ssm_d', 'v_ssm_norm_g', 'v_ssm_w_out', 'v_mlp_w1', 'v_mlp_w2', 'v_final_g']
TWIN_OUTPUTS = ['loss', 'grad_x', 'grad_norm_mix_g', 'grad_norm_mlp_g', 'grad_pool_w', 'grad_pool_b', 'grad_pool_scale', 'grad_ssm_w_in', 'grad_ssm_conv_w', 'grad_ssm_conv_b', 'grad_ssm_dt_bias', 'grad_ssm_a_log', 'grad_ssm_d', 'grad_ssm_norm_g', 'grad_ssm_w_out', 'grad_mlp_w1', 'grad_mlp_w2', 'grad_final_g', 'delta_norm_mix_g', 'delta_norm_mlp_g', 'delta_pool_w', 'delta_pool_b', 'delta_pool_scale', 'delta_ssm_w_in', 'delta_ssm_conv_w', 'delta_ssm_conv_b', 'delta_ssm_dt_bias', 'delta_ssm_a_log', 'delta_ssm_d', 'delta_ssm_norm_g', 'delta_ssm_w_out', 'delta_mlp_w1', 'delta_mlp_w2', 'delta_final_g', 'new_m_norm_mix_g', 'new_m_norm_mlp_g', 'new_m_pool_w', 'new_m_pool_b', 'new_m_pool_scale', 'new_m_ssm_w_in', 'new_m_ssm_conv_w', 'new_m_ssm_conv_b', 'new_m_ssm_dt_bias', 'new_m_ssm_a_log', 'new_m_ssm_d', 'new_m_ssm_norm_g', 'new_m_ssm_w_out', 'new_m_mlp_w1', 'new_m_mlp_w2', 'new_m_final_g', 'new_v_norm_mix_g', 'new_v_norm_mlp_g', 'new_v_pool_w', 'new_v_pool_b', 'new_v_pool_scale', 'new_v_ssm_w_in', 'new_v_ssm_conv_w', 'new_v_ssm_conv_b', 'new_v_ssm_dt_bias', 'new_v_ssm_a_log', 'new_v_ssm_d', 'new_v_ssm_norm_g', 'new_v_ssm_w_out', 'new_v_mlp_w1', 'new_v_mlp_w2', 'new_v_final_g']
TWIN_LEAF_KINDS = {'loss': 'loss', 'grad_x': 'grad_x', 'grad_norm_mix_g': 'grad_w', 'grad_norm_mlp_g': 'grad_w', 'grad_pool_w': 'grad_w', 'grad_pool_b': 'grad_w', 'grad_pool_scale': 'grad_w', 'grad_ssm_w_in': 'grad_w', 'grad_ssm_conv_w': 'grad_w', 'grad_ssm_conv_b': 'grad_w', 'grad_ssm_dt_bias': 'grad_w', 'grad_ssm_a_log': 'grad_w', 'grad_ssm_d': 'grad_w', 'grad_ssm_norm_g': 'grad_w', 'grad_ssm_w_out': 'grad_w', 'grad_mlp_w1': 'grad_w', 'grad_mlp_w2': 'grad_w', 'grad_final_g': 'grad_w', 'delta_norm_mix_g': 'delta_w', 'delta_norm_mlp_g': 'delta_w', 'delta_pool_w': 'delta_w', 'delta_pool_b': 'delta_w', 'delta_pool_scale': 'delta_w', 'delta_ssm_w_in': 'delta_w', 'delta_ssm_conv_w': 'delta_w', 'delta_ssm_conv_b': 'delta_w', 'delta_ssm_dt_bias': 'delta_w', 'delta_ssm_a_log': 'delta_w', 'delta_ssm_d': 'delta_w', 'delta_ssm_norm_g': 'delta_w', 'delta_ssm_w_out': 'delta_w', 'delta_mlp_w1': 'delta_w', 'delta_mlp_w2': 'delta_w', 'delta_final_g': 'delta_w', 'new_m_norm_mix_g': 'new_m', 'new_m_norm_mlp_g': 'new_m', 'new_m_pool_w': 'new_m', 'new_m_pool_b': 'new_m', 'new_m_pool_scale': 'new_m', 'new_m_ssm_w_in': 'new_m', 'new_m_ssm_conv_w': 'new_m', 'new_m_ssm_conv_b': 'new_m', 'new_m_ssm_dt_bias': 'new_m', 'new_m_ssm_a_log': 'new_m', 'new_m_ssm_d': 'new_m', 'new_m_ssm_norm_g': 'new_m', 'new_m_ssm_w_out': 'new_m', 'new_m_mlp_w1': 'new_m', 'new_m_mlp_w2': 'new_m', 'new_m_final_g': 'new_m', 'new_v_norm_mix_g': 'new_v', 'new_v_norm_mlp_g': 'new_v', 'new_v_pool_w': 'new_v', 'new_v_pool_b': 'new_v', 'new_v_pool_scale': 'new_v', 'new_v_ssm_w_in': 'new_v', 'new_v_ssm_conv_w': 'new_v', 'new_v_ssm_conv_b': 'new_v', 'new_v_ssm_dt_bias': 'new_v', 'new_v_ssm_a_log': 'new_v', 'new_v_ssm_d': 'new_v', 'new_v_ssm_norm_g': 'new_v', 'new_v_ssm_w_out': 'new_v', 'new_v_mlp_w1': 'new_v', 'new_v_mlp_w2': 'new_v', 'new_v_final_g': 'new_v'}


def _forward(args):
    return _fwd_reference(*[args[k] for k in FWD_PARAMS])


def _output_shape():
    def fwd():
        inp = _fwd_setup_inputs(0)
        return _fwd_reference(*[inp[k] for k in FWD_PARAMS])
    out = _jax.eval_shape(fwd)
    return out.shape, out.dtype

N_MICROBATCH = 1
ADAM_LR = 0.001
ADAM_B1 = 0.9
ADAM_B2 = 0.999
ADAM_EPS = 1e-08
ADAM_WD = 0.01
ADAM_STEP = 10
PER_EXAMPLE_BATCH_AXIS = {'x': 0, 'loss_target': 0}
SHARED_INPUTS = []
_WEIGHT_DTYPES = {'norm_mix_g': _jnp.float32, 'norm_mlp_g': _jnp.float32, 'pool_w': _jnp.float32, 'pool_b': _jnp.float32, 'pool_scale': _jnp.float32, 'ssm_w_in': _jnp.float32, 'ssm_conv_w': _jnp.float32, 'ssm_conv_b': _jnp.float32, 'ssm_dt_bias': _jnp.float32, 'ssm_a_log': _jnp.float32, 'ssm_d': _jnp.float32, 'ssm_norm_g': _jnp.float32, 'ssm_w_out': _jnp.float32, 'mlp_w1': _jnp.float32, 'mlp_w2': _jnp.float32, 'final_g': _jnp.float32}
MOMENT_SCALE = {'norm_mix_g': 1.345133e-01, 'norm_mlp_g': 1.740716e-01, 'pool_w': 9.527643e-02, 'pool_b': 3.161681e-01, 'pool_scale': 2.686039e-01, 'ssm_w_in': 7.547216e-02, 'ssm_conv_w': 8.082230e-02, 'ssm_conv_b': 1.531471e-01, 'ssm_dt_bias': 1.375748e-01, 'ssm_a_log': 3.582930e-01, 'ssm_d': 3.953531e-01, 'ssm_norm_g': 9.426417e-02, 'ssm_w_out': 1.445772e-01, 'mlp_w1': 8.273635e-02, 'mlp_w2': 2.539186e-01, 'final_g': 3.261925e+01}


def _to_microbatches(a, axis):
    t = _jnp.moveaxis(a, axis, 0)
    t = t.reshape((N_MICROBATCH, t.shape[0] // N_MICROBATCH) + t.shape[1:])
    return _jnp.moveaxis(t, 1, axis + 1)


def setup_inputs(seed: int = 0) -> dict:
    inp = _fwd_setup_inputs(seed)
    key = _jax.random.fold_in(_jax.random.key(seed), 7919)
    shape, _ = _output_shape()
    out = dict(inp)
    out["loss_target"] = _jax.random.normal(_jax.random.fold_in(key, 0), shape, _jnp.float32)
    for i, name in enumerate(TWIN_WEIGHTS):
        w = inp[name].astype(_jnp.float32)
        if MOMENT_SCALE is None:
            s = _jnp.sqrt(_jnp.mean(_jnp.square(w)) + 1e-30)
        else:
            s = MOMENT_SCALE[name]
        km, kv = _jax.random.split(_jax.random.fold_in(key, i + 1))
        out[name] = w
        out["m_" + name] = s * _jax.random.normal(km, w.shape, _jnp.float32)
        out["v_" + name] = (s * s) * _jax.random.uniform(kv, w.shape, _jnp.float32, 0.5, 1.5)
    if N_MICROBATCH > 1:
        for name, axis in PER_EXAMPLE_BATCH_AXIS.items():
            out[name] = _to_microbatches(out[name], axis)
    return {'x': out['x'], 'norm_mix_g': out['norm_mix_g'], 'norm_mlp_g': out['norm_mlp_g'], 'pool_w': out['pool_w'], 'pool_b': out['pool_b'], 'pool_scale': out['pool_scale'], 'ssm_w_in': out['ssm_w_in'], 'ssm_conv_w': out['ssm_conv_w'], 'ssm_conv_b': out['ssm_conv_b'], 'ssm_dt_bias': out['ssm_dt_bias'], 'ssm_a_log': out['ssm_a_log'], 'ssm_d': out['ssm_d'], 'ssm_norm_g': out['ssm_norm_g'], 'ssm_w_out': out['ssm_w_out'], 'mlp_w1': out['mlp_w1'], 'mlp_w2': out['mlp_w2'], 'final_g': out['final_g'], 'loss_target': out['loss_target'], 'm_norm_mix_g': out['m_norm_mix_g'], 'm_norm_mlp_g': out['m_norm_mlp_g'], 'm_pool_w': out['m_pool_w'], 'm_pool_b': out['m_pool_b'], 'm_pool_scale': out['m_pool_scale'], 'm_ssm_w_in': out['m_ssm_w_in'], 'm_ssm_conv_w': out['m_ssm_conv_w'], 'm_ssm_conv_b': out['m_ssm_conv_b'], 'm_ssm_dt_bias': out['m_ssm_dt_bias'], 'm_ssm_a_log': out['m_ssm_a_log'], 'm_ssm_d': out['m_ssm_d'], 'm_ssm_norm_g': out['m_ssm_norm_g'], 'm_ssm_w_out': out['m_ssm_w_out'], 'm_mlp_w1': out['m_mlp_w1'], 'm_mlp_w2': out['m_mlp_w2'], 'm_final_g': out['m_final_g'], 'v_norm_mix_g': out['v_norm_mix_g'], 'v_norm_mlp_g': out['v_norm_mlp_g'], 'v_pool_w': out['v_pool_w'], 'v_pool_b': out['v_pool_b'], 'v_pool_scale': out['v_pool_scale'], 'v_ssm_w_in': out['v_ssm_w_in'], 'v_ssm_conv_w': out['v_ssm_conv_w'], 'v_ssm_conv_b': out['v_ssm_conv_b'], 'v_ssm_dt_bias': out['v_ssm_dt_bias'], 'v_ssm_a_log': out['v_ssm_a_log'], 'v_ssm_d': out['v_ssm_d'], 'v_ssm_norm_g': out['v_ssm_norm_g'], 'v_ssm_w_out': out['v_ssm_w_out'], 'v_mlp_w1': out['v_mlp_w1'], 'v_mlp_w2': out['v_mlp_w2'], 'v_final_g': out['v_final_g']}


def _loss(weights, diff, rest, loss_target):
    with _jax.named_scope("forward"):
        args = {**rest, TWIN_DIFF_INPUT: diff, **{k: w.astype(_WEIGHT_DTYPES[k]) for k, w in weights.items()}}
        y = _forward(args)
    with _jax.named_scope("loss_head"):
        err = _jnp.square(y.astype(_jnp.float32) - loss_target)
        return 0.5 * _jnp.sum(_jnp.mean(err, axis=-1)) if err.ndim else 0.5 * err


def _adamw(w, g, m, v):
    m = ADAM_B1 * m + (1.0 - ADAM_B1) * g
    v = ADAM_B2 * v + (1.0 - ADAM_B2) * _jnp.square(g)
    m_hat = m / (1.0 - ADAM_B1 ** ADAM_STEP)
    v_hat = v / (1.0 - ADAM_B2 ** ADAM_STEP)
    delta = -ADAM_LR * (m_hat / (_jnp.sqrt(v_hat) + ADAM_EPS) + ADAM_WD * w)
    return delta, m, v


def reference(x, norm_mix_g, norm_mlp_g, pool_w, pool_b, pool_scale, ssm_w_in, ssm_conv_w, ssm_conv_b, ssm_dt_bias, ssm_a_log, ssm_d, ssm_norm_g, ssm_w_out, mlp_w1, mlp_w2, final_g, loss_target, m_norm_mix_g, m_norm_mlp_g, m_pool_w, m_pool_b, m_pool_scale, m_ssm_w_in, m_ssm_conv_w, m_ssm_conv_b, m_ssm_dt_bias, m_ssm_a_log, m_ssm_d, m_ssm_norm_g, m_ssm_w_out, m_mlp_w1, m_mlp_w2, m_final_g, v_norm_mix_g, v_norm_mlp_g, v_pool_w, v_pool_b, v_pool_scale, v_ssm_w_in, v_ssm_conv_w, v_ssm_conv_b, v_ssm_dt_bias, v_ssm_a_log, v_ssm_d, v_ssm_norm_g, v_ssm_w_out, v_mlp_w1, v_mlp_w2, v_final_g):
    given = dict(x=x, norm_mix_g=norm_mix_g, norm_mlp_g=norm_mlp_g, pool_w=pool_w, pool_b=pool_b, pool_scale=pool_scale, ssm_w_in=ssm_w_in, ssm_conv_w=ssm_conv_w, ssm_conv_b=ssm_conv_b, ssm_dt_bias=ssm_dt_bias, ssm_a_log=ssm_a_log, ssm_d=ssm_d, ssm_norm_g=ssm_norm_g, ssm_w_out=ssm_w_out, mlp_w1=mlp_w1, mlp_w2=mlp_w2, final_g=final_g, loss_target=loss_target, m_norm_mix_g=m_norm_mix_g, m_norm_mlp_g=m_norm_mlp_g, m_pool_w=m_pool_w, m_pool_b=m_pool_b, m_pool_scale=m_pool_scale, m_ssm_w_in=m_ssm_w_in, m_ssm_conv_w=m_ssm_conv_w, m_ssm_conv_b=m_ssm_conv_b, m_ssm_dt_bias=m_ssm_dt_bias, m_ssm_a_log=m_ssm_a_log, m_ssm_d=m_ssm_d, m_ssm_norm_g=m_ssm_norm_g, m_ssm_w_out=m_ssm_w_out, m_mlp_w1=m_mlp_w1, m_mlp_w2=m_mlp_w2, m_final_g=m_final_g, v_norm_mix_g=v_norm_mix_g, v_norm_mlp_g=v_norm_mlp_g, v_pool_w=v_pool_w, v_pool_b=v_pool_b, v_pool_scale=v_pool_scale, v_ssm_w_in=v_ssm_w_in, v_ssm_conv_w=v_ssm_conv_w, v_ssm_conv_b=v_ssm_conv_b, v_ssm_dt_bias=v_ssm_dt_bias, v_ssm_a_log=v_ssm_a_log, v_ssm_d=v_ssm_d, v_ssm_norm_g=v_ssm_norm_g, v_ssm_w_out=v_ssm_w_out, v_mlp_w1=v_mlp_w1, v_mlp_w2=v_mlp_w2, v_final_g=v_final_g)
    weights = {n: given[n] for n in TWIN_WEIGHTS}
    shared = {n: given[n] for n in SHARED_INPUTS}
    per_example = {n: given[n] for n in ['x']}
    grad_fn = _jax.value_and_grad(_loss, argnums=(0, 1))

    def one_microbatch(ex, loss_target):
        ex = dict(ex)
        diff = ex.pop(TWIN_DIFF_INPUT)
        return grad_fn(weights, diff, {**shared, **ex}, loss_target)

    if N_MICROBATCH == 1:
        loss, (grad_w, grad_x) = one_microbatch(per_example, given["loss_target"])
    else:
        def body(carry, xs):
            loss_sum, grad_sum = carry
            l_k, (gw_k, gx_k) = one_microbatch(xs[0], xs[1])
            with _jax.named_scope("update"):
                return (loss_sum + l_k, _jax.tree.map(_jnp.add, grad_sum, gw_k)), gx_k

        init = (_jnp.zeros((), _jnp.float32), _jax.tree.map(_jnp.zeros_like, weights))
        (loss, grad_w), grad_x = _jax.lax.scan(body, init, (per_example, given["loss_target"]))
    with _jax.named_scope("update"):
        delta_w, new_m, new_v = {}, {}, {}
        for n in TWIN_WEIGHTS:
            delta_w[n], new_m[n], new_v[n] = _adamw(weights[n], grad_w[n], given["m_" + n], given["v_" + n])
    return (loss, grad_x, *[grad_w[n] for n in TWIN_WEIGHTS], *[delta_w[n] for n in TWIN_WEIGHTS],
            *[new_m[n] for n in TWIN_WEIGHTS], *[new_v[n] for n in TWIN_WEIGHTS])
```

```python
import functools
import math

import jax
import jax.numpy as jnp
from jax import lax
from jax.experimental import pallas as pl
from jax.experimental.pallas import tpu as pltpu

F32 = jnp.float32
BF16 = jnp.bfloat16
MESH = pl.DeviceIdType.MESH

D_MODEL = 1024
RMS_EPS = 1e-5
POOL_WINDOWS = (2, 4, 8, 16)
POOL_GROUP = 256
POOL_HALO = 16
D_INNER = 2048
HEAD_DIM = 64
N_HEADS = 32
N_GROUPS = 4
HEADS_PER_GROUP = 8
D_STATE = 128
CHUNK = 128
CONV_DIM = 3072
IN_PROJ_DIM = 5152
D_FF = 4096
N_DEV = 8
GROUP_X = HEADS_PER_GROUP * HEAD_DIM
ZX_COLS = D_INNER + CONV_DIM + N_GROUPS * 128
COL_BLK = 512

ADAM_LR = 0.001
ADAM_B1 = 0.9
ADAM_B2 = 0.999
ADAM_EPS = 1e-08
ADAM_WD = 0.01
ADAM_STEP = 10

VMEM_LIMIT_V7X = 48 * 1024 * 1024

ROWS_POOL, ROWS_WIN, ROWS_WIN_PAD, ROWS_WOUT, ROWS_W1, ROWS_W2, ROWS_SMALL = 32, 644, 656, 256, 1024, 1024, 16
OFF_POOL = 0
OFF_WIN = OFF_POOL + ROWS_POOL
OFF_WOUT = OFF_WIN + ROWS_WIN_PAD
OFF_W1 = OFF_WOUT + ROWS_WOUT
OFF_W2 = OFF_W1 + ROWS_W1
OFF_SMALL = OFF_W2 + ROWS_W2
PACK_ROWS = OFF_SMALL + ROWS_SMALL
PACK_ROW_BLK = PACK_ROWS // 4
N_SMALL = 4 * 384 + 384 + 256

_NN = (((1,), (0,)), ((), ()))
_NT = (((1,), (1,)), ((), ()))
_TN = (((0,), (0,)), ((), ()))


def _cp(sem):
    return pltpu.CompilerParams(dimension_semantics=sem, vmem_limit_bytes=VMEM_LIMIT_V7X)


def _dg(a, b, dn):
    return lax.dot_general(a.astype(BF16), b.astype(BF16), dn, preferred_element_type=F32)


@jax.custom_vjp
def mm_nn(a, b):
    return _dg(a, b, _NN)


@jax.custom_vjp
def mm_nt(a, b):
    return _dg(a, b, _NT)


@jax.custom_vjp
def mm_tn(a, b):
    return _dg(a, b, _TN)


mm_nn.defvjp(lambda a, b: (_dg(a, b, _NN), (a, b)), lambda r, ct: (mm_nt(ct, r[1]), mm_tn(r[0], ct)))
mm_nt.defvjp(lambda a, b: (_dg(a, b, _NT), (a, b)), lambda r, ct: (mm_nn(ct, r[1]), mm_tn(ct, r[0])))
mm_tn.defvjp(lambda a, b: (_dg(a, b, _TN), (a, b)), lambda r, ct: (mm_nt(r[1], ct), mm_nn(r[0], ct)))


def _split3(x):
    p1 = x.astype(BF16)
    r1 = x - p1.astype(F32)
    p2 = r1.astype(BF16)
    r2 = r1 - p2.astype(F32)
    return p1, p2, r2.astype(BF16)


def _exact01(x, c, dn, const_left):
    acc = None
    for p in reversed(_split3(x)):
        t = (lax.dot_general(c, p, dn, preferred_element_type=F32) if const_left
             else lax.dot_general(p, c, dn, preferred_element_type=F32))
        acc = t if acc is None else acc + t
    return acc


def _make_cmm(dn, const_left, bwd_name):
    @jax.custom_vjp
    def f(x, c):
        return _exact01(x, c, dn, const_left)

    def fwd(x, c):
        return _exact01(x, c, dn, const_left), c

    def bwd(c, ct):
        return _CMM[bwd_name](ct, c), jnp.zeros_like(c)

    f.defvjp(fwd, bwd)
    return f


_CMM = {}
_CMM["xc"] = _make_cmm(_NN, False, "xct")
_CMM["xct"] = _make_cmm(_NT, False, "xc")
_CMM["cx"] = _make_cmm(_NN, True, "ctx")
_CMM["ctx"] = _make_cmm(_TN, True, "cx")


@jax.custom_vjp
def _silu(x):
    return x / (1.0 + jnp.exp(-x))


def _silu_fwd(x):
    return _silu(x), x


def _silu_bwd(x, ct):
    s = 1.0 / (1.0 + jnp.exp(-x))
    return (ct * (s * (1.0 + x * (1.0 - s))),)


_silu.defvjp(_silu_fwd, _silu_bwd)


def _log1p_pos(e):
    u = 1.0 + e
    d = u - 1.0
    return jnp.where(d == 0.0, e, jnp.log(u) * (e / jnp.where(d == 0.0, 1.0, d)))


@jax.custom_vjp
def _softplus(x):
    return jnp.maximum(x, 0.0) + _log1p_pos(jnp.exp(-jnp.abs(x)))


def _softplus_fwd(x):
    return _softplus(x), x


def _softplus_bwd(x, ct):
    return (ct / (1.0 + jnp.exp(-x)),)


_softplus.defvjp(_softplus_fwd, _softplus_bwd)


def _make_shift(j):
    @jax.custom_vjp
    def f(ext):
        return pltpu.roll(ext, j, 0)[CHUNK:, :]

    def fwd(ext):
        return f(ext), None

    def bwd(_, ct):
        pad = jnp.concatenate([jnp.zeros_like(ct), ct], axis=0)
        return (pltpu.roll(pad, 2 * CHUNK - j, 0),)

    f.defvjp(fwd, bwd)
    return f


_SHIFT = {j: _make_shift(j) for j in (1, 2, 3)}


def _rms_fwd(x, g):
    r = lax.rsqrt(jnp.mean(x * x, axis=-1, keepdims=True) + RMS_EPS)
    n = x * r
    return n * g, n, r


def _rms_bwd(dy, n, r, g):
    dn = dy * g
    dx = r * (dn - n * jnp.mean(dn * n, axis=-1, keepdims=True))
    dg = jnp.sum(dy * n, axis=0, keepdims=True)
    return dx, dg


def _one(cond):
    return jnp.where(cond, 1.0, 0.0)


def _pool_tile(xe, g, ws, b, scale, tile, tt):
    r = lax.rsqrt(jnp.mean(xe * xe, axis=-1, keepdims=True) + RMS_EPS)
    hn = xe * r * g
    row_e = lax.broadcasted_iota(jnp.int32, (tt + POOL_HALO, POOL_GROUP), 0)
    keep = _one(jnp.logical_or(row_e >= POOL_HALO, tile > 0))
    rr = lax.broadcasted_iota(jnp.int32, (tt, tt + POOL_HALO), 0)
    qq = lax.broadcasted_iota(jnp.int32, (tt, tt + POOL_HALO), 1)
    dd = qq - rr
    tpos = tile * tt + lax.broadcasted_iota(jnp.int32, (tt, POOL_GROUP), 0)
    outs = []
    for gi, w in enumerate(POOL_WINDOWS):
        hg = hn[:, gi * POOL_GROUP:(gi + 1) * POOL_GROUP] * keep
        band = _one(jnp.logical_and(dd >= POOL_HALO - w + 1, dd <= POOL_HALO)).astype(BF16)
        cnt = jnp.minimum(tpos + 1, w).astype(F32)
        pooled = _CMM["cx"](hg, band) / cnt
        mixed = pooled - hg[POOL_HALO:, :]
        outs.append(mm_nn(mixed, ws[gi]))
    out = (jnp.concatenate(outs, axis=1) + b) * scale
    return xe[POOL_HALO:, :] + out


def _pool_specs(tt, nt, rev):
    per = tt // POOL_HALO
    t_of = (lambda i: nt - 1 - i) if rev else (lambda i: i)
    main = pl.BlockSpec((tt, D_MODEL), lambda i: (t_of(i), 0))
    halo = pl.BlockSpec((POOL_HALO, D_MODEL), lambda i: (jnp.maximum(t_of(i) * per - 1, 0), 0))
    vec = pl.BlockSpec((1, D_MODEL), lambda i: (0, 0))
    wsp = pl.BlockSpec((4, POOL_GROUP, POOL_GROUP), lambda i: (0, 0, 0))
    return main, halo, vec, wsp


def _pool_fwd(x, g, w, b, scale):
    t = x.shape[0]
    tt = min(t, 256)
    nt = t // tt
    main, halo, vec, wsp = _pool_specs(tt, nt, False)

    def body(xm_ref, xh_ref, g_ref, w_ref, b_ref, s_ref, o_ref):
        i = pl.program_id(0)
        xe = jnp.concatenate([xh_ref[...], xm_ref[...]], axis=0)
        ws = tuple(w_ref[k].astype(F32) for k in range(4))
        o_ref[...] = _pool_tile(xe, g_ref[...], ws, b_ref[...], s_ref[...], i, tt)

    return pl.pallas_call(
        body, name="pool_fwd", grid=(nt,),
        in_specs=[main, halo, vec, wsp, vec, vec], out_specs=main,
        out_shape=jax.ShapeDtypeStruct((t, D_MODEL), F32),
        compiler_params=_cp(("arbitrary",)),
    )(x, x, g, w, b, scale)


def _pool_bwd(x, dh, g, w, b, scale):
    t = x.shape[0]
    tt = min(t, 256)
    nt = t // tt
    main, halo, vec, wsp = _pool_specs(tt, nt, True)

    def body(xm_ref, xh_ref, dh_ref, g_ref, w_ref, b_ref, s_ref,
             dx_ref, dw_ref, db_ref, ds_ref, dg_ref, carry):
        i = pl.program_id(0)
        tile = nt - 1 - i

        @pl.when(i == 0)
        def _():
            carry[...] = jnp.zeros_like(carry)
            dw_ref[...] = jnp.zeros_like(dw_ref)
            db_ref[...] = jnp.zeros_like(db_ref)
            ds_ref[...] = jnp.zeros_like(ds_ref)
            dg_ref[...] = jnp.zeros_like(dg_ref)

        xe = jnp.concatenate([xh_ref[...], xm_ref[...]], axis=0)
        ws = tuple(w_ref[k].astype(F32) for k in range(4))
        _, vjp = jax.vjp(lambda a, gg, ww, bb, ss: _pool_tile(a, gg, ww, bb, ss, tile, tt),
                         xe, g_ref[...], ws, b_ref[...], s_ref[...])
        dxe, dgv, dws, dbv, dsv = vjp(dh_ref[...])
        dx_ref[...] = dxe[POOL_HALO:, :]
        dx_ref[tt - POOL_HALO:tt, :] += carry[...]
        carry[...] = dxe[:POOL_HALO, :]
        for k in range(4):
            dw_ref[k] += dws[k]
        db_ref[...] += dbv
        ds_ref[...] += dsv
        dg_ref[...] += dgv

    return pl.pallas_call(
        body, name="pool_bwd", grid=(nt,),
        in_specs=[main, halo, main, vec, wsp, vec, vec],
        out_specs=[main, wsp, vec, vec, vec],
        out_shape=[jax.ShapeDtypeStruct((t, D_MODEL), F32),
                   jax.ShapeDtypeStruct((4, POOL_GROUP, POOL_GROUP), F32),
                   jax.ShapeDtypeStruct((1, D_MODEL), F32),
                   jax.ShapeDtypeStruct((1, D_MODEL), F32),
                   jax.ShapeDtypeStruct((1, D_MODEL), F32)],
        scratch_shapes=[pltpu.VMEM((POOL_HALO, D_MODEL), F32)],
        compiler_params=_cp(("arbitrary",)),
    )(x, x, dh, g, w, b, scale)


def _mlp_fwd(h, g, w1b, w2, name):
    t = h.shape[0]
    tt = min(t, 512)
    nk, _, fb = w1b.shape

    def body(h_ref, g_ref, w1_ref, w2_ref, o_ref, u_ref, hm_ref, hm_s, acc_s):
        k = pl.program_id(1)

        @pl.when(k == 0)
        def _():
            xv = h_ref[...]
            y, _, _ = _rms_fwd(xv, g_ref[...])
            hb = y.astype(BF16)
            hm_s[...] = hb
            hm_ref[...] = hb
            acc_s[...] = xv

        a = jnp.dot(hm_s[...], w1_ref[...], preferred_element_type=F32)
        u = jnp.maximum(a, 0.0)
        u_ref[...] = u.astype(BF16)
        acc_s[...] += jnp.dot((u * u).astype(BF16), w2_ref[...], preferred_element_type=F32)

        @pl.when(k == nk - 1)
        def _():
            o_ref[...] = acc_s[...]

    return pl.pallas_call(
        body, name=name, grid=(t // tt, nk),
        in_specs=[pl.BlockSpec((tt, D_MODEL), lambda i, k: (i, 0)),
                  pl.BlockSpec((1, D_MODEL), lambda i, k: (0, 0)),
                  pl.BlockSpec((None, D_MODEL, fb), lambda i, k: (k, 0, 0)),
                  pl.BlockSpec((fb, D_MODEL), lambda i, k: (k, 0))],
        out_specs=[pl.BlockSpec((tt, D_MODEL), lambda i, k: (i, 0)),
                   pl.BlockSpec((tt, fb), lambda i, k: (i, k)),
                   pl.BlockSpec((tt, D_MODEL), lambda i, k: (i, 0))],
        out_shape=[jax.ShapeDtypeStruct((t, D_MODEL), F32),
                   jax.ShapeDtypeStruct((t, nk * fb), BF16),
                   jax.ShapeDtypeStruct((t, D_MODEL), BF16)],
        scratch_shapes=[pltpu.VMEM((tt, D_MODEL), BF16), pltpu.VMEM((tt, D_MODEL), F32)],
        compiler_params=_cp(("arbitrary", "arbitrary")),
    )(h, g, w1b, w2)


def _mlp_bwd(dh, h, g, u, w1b, w2, name):
    t = h.shape[0]
    tt = min(t, 512)
    nk, _, fb = w1b.shape

    def body(dh_ref, h_ref, g_ref, u_ref, w1_ref, w2_ref,
             dhin_ref, da_ref, dhb_ref, dg_ref, dhb_s, acc_s):
        i = pl.program_id(0)
        k = pl.program_id(1)

        @pl.when(jnp.logical_and(i == 0, k == 0))
        def _():
            dg_ref[...] = jnp.zeros_like(dg_ref)

        @pl.when(k == 0)
        def _():
            db = dh_ref[...].astype(BF16)
            dhb_s[...] = db
            dhb_ref[...] = db
            acc_s[...] = jnp.zeros_like(acc_s)

        dv = lax.dot_general(dhb_s[...], w2_ref[...], _NT, preferred_element_type=F32)
        dab = (dv * (2.0 * u_ref[...].astype(F32))).astype(BF16)
        da_ref[...] = dab
        acc_s[...] += lax.dot_general(dab, w1_ref[...], _NT, preferred_element_type=F32)

        @pl.when(k == nk - 1)
        def _():
            gv = g_ref[...]
            _, n, r = _rms_fwd(h_ref[...], gv)
            dx, dg = _rms_bwd(acc_s[...], n, r, gv)
            dhin_ref[...] = dh_ref[...] + dx
            dg_ref[...] += dg

    return pl.pallas_call(
        body, name=name, grid=(t // tt, nk),
        in_specs=[pl.BlockSpec((tt, D_MODEL), lambda i, k: (i, 0)),
                  pl.BlockSpec((tt, D_MODEL), lambda i, k: (i, 0)),
                  pl.BlockSpec((1, D_MODEL), lambda i, k: (0, 0)),
                  pl.BlockSpec((tt, fb), lambda i, k: (i, k)),
                  pl.BlockSpec((None, D_MODEL, fb), lambda i, k: (k, 0, 0)),
                  pl.BlockSpec((fb, D_MODEL), lambda i, k: (k, 0))],
        out_specs=[pl.BlockSpec((tt, D_MODEL), lambda i, k: (i, 0)),
                   pl.BlockSpec((tt, fb), lambda i, k: (i, k)),
                   pl.BlockSpec((tt, D_MODEL), lambda i, k: (i, 0)),
                   pl.BlockSpec((1, D_MODEL), lambda i, k: (0, 0))],
        out_shape=[jax.ShapeDtypeStruct((t, D_MODEL), F32),
                   jax.ShapeDtypeStruct((t, nk * fb), BF16),
                   jax.ShapeDtypeStruct((t, D_MODEL), BF16),
                   jax.ShapeDtypeStruct((1, D_MODEL), F32)],
        scratch_shapes=[pltpu.VMEM((tt, D_MODEL), BF16), pltpu.VMEM((tt, D_MODEL), F32)],
        compiler_params=_cp(("arbitrary", "arbitrary")),
    )(dh, h, g, u, w1b, w2)


def _matmul_tn(a, b, name, square_a=False, col_blocked=False):
    t, k1 = a.shape
    k2 = b.shape[1]
    b1 = min(k1, 1024)
    tt = min(t, 512)
    nt = t // tt

    def body(a_ref, b_ref, o_ref, acc):
        s = pl.program_id(2)

        @pl.when(s == 0)
        def _():
            acc[...] = jnp.zeros_like(acc)

        av = a_ref[...]
        if square_a:
            af = av.astype(F32)
            av = (af * af).astype(BF16)
        acc[...] += lax.dot_general(av, b_ref[...], _TN, preferred_element_type=F32)

        @pl.when(s == nt - 1)
        def _():
            o_ref[...] = acc[...].astype(o_ref.dtype)

    if col_blocked:
        out_shape = jax.ShapeDtypeStruct((k2 // COL_BLK, k1, COL_BLK), BF16)
        out_spec = pl.BlockSpec((None, b1, COL_BLK), lambda i, j, s: (j, i, 0))
    else:
        out_shape = jax.ShapeDtypeStruct((k1, k2), BF16)
        out_spec = pl.BlockSpec((b1, COL_BLK), lambda i, j, s: (i, j))
    return pl.pallas_call(
        body, name=name, grid=(k1 // b1, k2 // COL_BLK, nt),
        in_specs=[pl.BlockSpec((tt, b1), lambda i, j, s: (s, i)),
                  pl.BlockSpec((tt, COL_BLK), lambda i, j, s: (s, j))],
        out_specs=out_spec, out_shape=out_shape,
        scratch_shapes=[pltpu.VMEM((b1, COL_BLK), F32)],
        compiler_params=_cp(("arbitrary", "arbitrary", "arbitrary")),
    )(a, b)


def _norm_matmul(h, g, w):
    t = h.shape[0]
    tt = min(t, 512)
    n = w.shape[1]

    def body(h_ref, g_ref, w_ref, o_ref, hn_ref, hn_s):
        @pl.when(pl.program_id(1) == 0)
        def _():
            y, _, _ = _rms_fwd(h_ref[...], g_ref[...])
            hb = y.astype(BF16)
            hn_s[...] = hb
            hn_ref[...] = hb

        o_ref[...] = jnp.dot(hn_s[...], w_ref[...], preferred_element_type=F32)

    return pl.pallas_call(
        body, name="ssm_in_proj", grid=(t // tt, n // COL_BLK),
        in_specs=[pl.BlockSpec((tt, D_MODEL), lambda i, j: (i, 0)),
                  pl.BlockSpec((1, D_MODEL), lambda i, j: (0, 0)),
                  pl.BlockSpec((D_MODEL, COL_BLK), lambda i, j: (0, j))],
        out_specs=[pl.BlockSpec((tt, COL_BLK), lambda i, j: (i, j)),
                   pl.BlockSpec((tt, D_MODEL), lambda i, j: (i, 0))],
        out_shape=[jax.ShapeDtypeStruct((t, n), F32), jax.ShapeDtypeStruct((t, D_MODEL), BF16)],
        scratch_shapes=[pltpu.VMEM((tt, D_MODEL), BF16)],
        compiler_params=_cp(("arbitrary", "arbitrary")),
    )(h, g, w)


def _in_proj_bwd(dzx, w, h, g, dh_next):
    t = h.shape[0]
    tt = min(t, 512)
    n = w.shape[1]
    nj = n // COL_BLK

    def body(dz_ref, w_ref, h_ref, g_ref, dn_ref, dh_ref, dg_ref, acc):
        i = pl.program_id(0)
        j = pl.program_id(1)

        @pl.when(jnp.logical_and(i == 0, j == 0))
        def _():
            dg_ref[...] = jnp.zeros_like(dg_ref)

        @pl.when(j == 0)
        def _():
            acc[...] = jnp.zeros_like(acc)

        acc[...] += lax.dot_general(dz_ref[...], w_ref[...], _NT, preferred_element_type=F32)

        @pl.when(j == nj - 1)
        def _():
            gv = g_ref[...]
            _, nn, r = _rms_fwd(h_ref[...], gv)
            dx, dg = _rms_bwd(acc[...], nn, r, gv)
            dh_ref[...] = dn_ref[...] + dx
            dg_ref[...] += dg

    return pl.pallas_call(
        body, name="ssm_in_proj_bwd", grid=(t // tt, nj),
        in_specs=[pl.BlockSpec((tt, COL_BLK), lambda i, j: (i, j)),
                  pl.BlockSpec((D_MODEL, COL_BLK), lambda i, j: (0, j)),
                  pl.BlockSpec((tt, D_MODEL), lambda i, j: (i, 0)),
                  pl.BlockSpec((1, D_MODEL), lambda i, j: (0, 0)),
                  pl.BlockSpec((tt, D_MODEL), lambda i, j: (i, 0))],
        out_specs=[pl.BlockSpec((tt, D_MODEL), lambda i, j: (i, 0)),
                   pl.BlockSpec((1, D_MODEL), lambda i, j: (0, 0))],
        out_shape=[jax.ShapeDtypeStruct((t, D_MODEL), F32), jax.ShapeDtypeStruct((1, D_MODEL), F32)],
        scratch_shapes=[pltpu.VMEM((tt, D_MODEL), F32)],
        compiler_params=_cp(("arbitrary", "arbitrary")),
    )(dzx, w, h, g, dh_next)


def _ssd_consts():
    lane = lax.broadcasted_iota(jnp.int32, (CHUNK, CHUNK), 1)
    row = lax.broadcasted_iota(jnp.int32, (CHUNK, CHUNK), 0)
    causal = lane <= row
    tri = _one(causal).astype(BF16)
    er = lax.broadcasted_iota(jnp.int32, (CHUNK, GROUP_X), 0)
    ec = lax.broadcasted_iota(jnp.int32, (CHUNK, GROUP_X), 1)
    expand = _one(jnp.right_shift(ec, 6) == er).astype(BF16)
    e2r = lax.broadcasted_iota(jnp.int32, (CHUNK, HEADS_PER_GROUP * CHUNK), 0)
    e2c = lax.broadcasted_iota(jnp.int32, (CHUNK, HEADS_PER_GROUP * CHUNK), 1)
    expand2 = _one(jnp.right_shift(e2c, 7) == e2r).astype(BF16)
    return dict(causal=causal, tri=tri, expand=expand, expand2=expand2, lo=lane < HEAD_DIM)


def _conv_silu(cur, prev, w, b):
    ext = jnp.concatenate([prev, cur], axis=0)
    acc = cur * w[3] + b
    for j in (1, 2, 3):
        acc = acc + _SHIFT[j](ext) * w[3 - j]
    return _silu(acc)


def _ssd_chunk(xr, xrp, br, brp, cr, crp, dtr, ht, cwx, cbx, cwb, cbb, cwc, cbc, dtb, alog, dsk, k):
    xs = _conv_silu(xr, xrp, cwx, cbx)
    bm = _conv_silu(br, brp, cwb, cbb)
    cm = _conv_silu(cr, crp, cwc, cbc)
    dt = _softplus(dtr + dtb)
    a = -jnp.exp(alog)
    adt = dt * a
    xc = _CMM["xc"]

    def lanes(rowv):
        return jnp.sum(xc(jnp.broadcast_to(rowv, (16, CHUNK)), k["expand"]), axis=0, keepdims=True) * (1.0 / 16.0)

    dt_e = xc(dt, k["expand"])
    adt_e = dt_e * lanes(a)
    acs_e = _CMM["cx"](adt_e, k["tri"])
    acs = _CMM["cx"](adt, k["tri"])
    tot_e = jnp.sum(adt_e, axis=0, keepdims=True)
    cb_all = xc(acs, k["expand2"])
    gmat = mm_nt(cm, bm)
    xdt = xs * dt_e
    ys = []
    for j in range(HEADS_PER_GROUP // 2):
        ms = []
        for hh in (2 * j, 2 * j + 1):
            cb = cb_all[:, hh * CHUNK:(hh + 1) * CHUNK]
            seg = cb - cb.T
            ms.append(gmat * jnp.exp(jnp.where(k["causal"], seg, -jnp.inf)))
        xp = xdt[:, j * CHUNK:(j + 1) * CHUNK]
        rhs = jnp.concatenate([jnp.where(k["lo"], xp, 0.0), jnp.where(k["lo"], 0.0, xp)], axis=0)
        ys.append(mm_nn(jnp.concatenate(ms, axis=1), rhs))
    y_diag = jnp.concatenate(ys, axis=1)
    y_off = jnp.exp(acs_e) * mm_nn(cm, ht)
    h_new = jnp.exp(tot_e) * ht + mm_tn(bm, xdt * jnp.exp(tot_e - acs_e))
    return y_diag + y_off + lanes(dsk) * xs, h_new


def _ssd_in_specs(nc, rev):
    c_of = (lambda c: nc - 1 - c) if rev else (lambda c: c)
    xoff, boff, coff, doff = D_INNER // GROUP_X, (2 * D_INNER) // 128, (2 * D_INNER + 512) // 128, (D_INNER + CONV_DIM) // 128

    def cur(w, off):
        return pl.BlockSpec((CHUNK, w), lambda g, c: (c_of(c), off + g))

    def prev(w, off):
        return pl.BlockSpec((CHUNK, w), lambda g, c: (jnp.maximum(c_of(c) - 1, 0), off + g))

    zx = [cur(GROUP_X, xoff), prev(GROUP_X, xoff), cur(128, boff), prev(128, boff),
          cur(128, coff), prev(128, coff), cur(128, doff)]

    def par(rows, w, off):
        return pl.BlockSpec((rows, w), lambda g, c: (0, off + g))

    conv = [par(4, GROUP_X, 0), par(1, GROUP_X, 0), par(4, 128, 16), par(1, 128, 16), par(4, 128, 20), par(1, 128, 20)]
    head = [pl.BlockSpec((None, 1, 128), lambda g, c: (g, 0, 0))] * 3
    return zx, conv, head, c_of


def _load_chunk_args(refs, has_prev):
    (xr, xrp, br, brp, cr, crp, dtr, cwx, cbx, cwb, cbb, cwc, cbc, dtb, alog, dsk) = refs
    rows = lambda ref: tuple(ref[pl.ds(i, 1), :] for i in range(4))
    return (xr[...], xrp[...] * has_prev, br[...], brp[...] * has_prev, cr[...], crp[...] * has_prev, dtr[...],
            rows(cwx), cbx[...], rows(cwb), cbb[...], rows(cwc), cbc[...], dtb[...], alog[...], dsk[...])


def _ssd_fwd(zx, conv_w, conv_b, dtb, alog, dsk):
    t = zx.shape[0]
    nc = t // CHUNK
    zx_specs, conv_specs, head_specs, _ = _ssd_in_specs(nc, False)

    def body(*refs):
        ins, (y_ref, hs_ref, ht) = refs[:16], refs[16:]
        c = pl.program_id(1)

        @pl.when(c == 0)
        def _():
            ht[...] = jnp.zeros_like(ht)

        a = _load_chunk_args(ins, _one(c > 0))
        h_in = ht[...]
        y, h_new = _ssd_chunk(*a[:7], h_in, *a[7:], _ssd_consts())
        y_ref[...] = y
        hs_ref[...] = h_in
        ht[...] = h_new

    return pl.pallas_call(
        body, name="ssd_fwd", grid=(N_GROUPS, nc),
        in_specs=zx_specs + conv_specs + head_specs,
        out_specs=[pl.BlockSpec((CHUNK, GROUP_X), lambda g, c: (c, g)),
                   pl.BlockSpec((None, None, D_STATE, GROUP_X), lambda g, c: (g, c, 0, 0))],
        out_shape=[jax.ShapeDtypeStruct((t, D_INNER), F32),
                   jax.ShapeDtypeStruct((N_GROUPS, nc, D_STATE, GROUP_X), F32)],
        scratch_shapes=[pltpu.VMEM((D_STATE, GROUP_X), F32)],
        compiler_params=_cp(("arbitrary", "arbitrary")),
    )(*([zx] * 7), conv_w, conv_b, conv_w, conv_b, conv_w, conv_b, dtb, alog, dsk)


def _ssd_bwd(zx, conv_w, conv_b, dtb, alog, dsk, hs, dy):
    t = zx.shape[0]
    nc = t // CHUNK
    zx_specs, conv_specs, head_specs, c_of = _ssd_in_specs(nc, True)
    n_in = 18

    def body(*refs):
        ins, hs_ref, dy_ref = refs[:16], refs[16], refs[17]
        (dx_ref, db_ref, dc_ref, ddt_ref, dcwx, dcbx, dcwb, dcbb, dcwc, dcbc, ddtb, dalog, ddsk,
         dht, car_x, car_b, car_c) = refs[n_in:]
        cc = pl.program_id(1)
        accs = (dcwx, dcbx, dcwb, dcbb, dcwc, dcbc, ddtb, dalog, ddsk)

        @pl.when(cc == 0)
        def _():
            for r in (dht, car_x, car_b, car_c) + accs:
                r[...] = jnp.zeros_like(r)

        has_prev = _one(c_of(cc) > 0)
        a = _load_chunk_args(ins, has_prev)
        k = _ssd_consts()
        fn = lambda *args: _ssd_chunk(*args, k)
        _, vjp = jax.vjp(fn, *a[:7], hs_ref[...], *a[7:])
        (gx, gxp, gb, gbp, gc, gcp, gdt, ght, gcwx, gcbx, gcwb, gcbb, gcwc, gcbc, gdtb, galog, gdsk) = vjp(
            (dy_ref[...], dht[...]))
        dx_ref[...] = (gx + car_x[...]).astype(BF16)
        db_ref[...] = (gb + car_b[...]).astype(BF16)
        dc_ref[...] = (gc + car_c[...]).astype(BF16)
        ddt_ref[...] = gdt.astype(BF16)
        car_x[...] = gxp * has_prev
        car_b[...] = gbp * has_prev
        car_c[...] = gcp * has_prev
        dht[...] = ght
        for ref, rows in ((dcwx, gcwx), (dcwb, gcwb), (dcwc, gcwc)):
            for i in range(4):
                ref[pl.ds(i, 1), :] += rows[i]
        for ref, val in ((dcbx, gcbx), (dcbb, gcbb), (dcbc, gcbc), (ddtb, gdtb), (dalog, galog), (ddsk, gdsk)):
            ref[...] += val

    def acc(rows, w):
        return pl.BlockSpec((rows, w), lambda g, c: (0, g))

    head_out = pl.BlockSpec((None, 1, 128), lambda g, c: (g, 0, 0))
    sds = jax.ShapeDtypeStruct
    return pl.pallas_call(
        body, name="ssd_bwd", grid=(N_GROUPS, nc),
        in_specs=zx_specs + conv_specs + head_specs + [
            pl.BlockSpec((None, None, D_STATE, GROUP_X), lambda g, c: (g, c_of(c), 0, 0)),
            pl.BlockSpec((CHUNK, GROUP_X), lambda g, c: (c_of(c), g))],
        out_specs=[pl.BlockSpec((CHUNK, GROUP_X), lambda g, c: (c_of(c), g)),
                   pl.BlockSpec((CHUNK, 128), lambda g, c: (c_of(c), g)),
                   pl.BlockSpec((CHUNK, 128), lambda g, c: (c_of(c), g)),
                   pl.BlockSpec((CHUNK, 128), lambda g, c: (c_of(c), g)),
                   acc(4, GROUP_X), acc(1, GROUP_X), acc(4, 128), acc(1, 128), acc(4, 128), acc(1, 128),
                   head_out, head_out, head_out],
        out_shape=[sds((t, D_INNER), BF16), sds((t, 512), BF16), sds((t, 512), BF16), sds((t, 512), BF16),
                   sds((4, D_INNER), F32), sds((1, D_INNER), F32), sds((4, 512), F32), sds((1, 512), F32),
                   sds((4, 512), F32), sds((1, 512), F32),
                   sds((N_GROUPS, 1, 128), F32), sds((N_GROUPS, 1, 128), F32), sds((N_GROUPS, 1, 128), F32)],
        scratch_shapes=[pltpu.VMEM((D_STATE, GROUP_X), F32), pltpu.VMEM((CHUNK, GROUP_X), F32),
                        pltpu.VMEM((CHUNK, 128), F32), pltpu.VMEM((CHUNK, 128), F32)],
        compiler_params=_cp(("arbitrary", "arbitrary")),
    )(*([zx] * 7), conv_w, conv_b, conv_w, conv_b, conv_w, conv_b, dtb, alog, dsk, hs, dy)


def _gate_norm(y, z, ng):
    yz = y * _silu(z)
    outs = []
    for k in range(N_GROUPS):
        s = yz[:, k * GROUP_X:(k + 1) * GROUP_X]
        outs.append(s * lax.rsqrt(jnp.mean(s * s, axis=-1, keepdims=True) + RMS_EPS))
    return jnp.concatenate(outs, axis=1) * ng


def _ssm_out_fwd(y, zx, ng, w_out, h):
    t = h.shape[0]
    tt = min(t, 256)

    def body(y_ref, z_ref, ng_ref, w_ref, h_ref, o_ref):
        yn = _gate_norm(y_ref[...], z_ref[...], ng_ref[...])
        o_ref[...] = h_ref[...] + jnp.dot(yn.astype(BF16), w_ref[...], preferred_element_type=F32)

    return pl.pallas_call(
        body, name="ssm_out_fwd", grid=(t // tt,),
        in_specs=[pl.BlockSpec((tt, D_INNER), lambda i: (i, 0)),
                  pl.BlockSpec((tt, D_INNER), lambda i: (i, 0)),
                  pl.BlockSpec((1, D_INNER), lambda i: (0, 0)),
                  pl.BlockSpec((D_INNER, D_MODEL), lambda i: (0, 0)),
                  pl.BlockSpec((tt, D_MODEL), lambda i: (i, 0))],
        out_specs=pl.BlockSpec((tt, D_MODEL), lambda i: (i, 0)),
        out_shape=jax.ShapeDtypeStruct((t, D_MODEL), F32),
        compiler_params=_cp(("arbitrary",)),
    )(y, zx, ng, w_out, h)


def _ssm_out_bwd(dh, y, zx, ng, w_out):
    t = dh.shape[0]
    tt = min(t, 256)

    def body(dh_ref, y_ref, z_ref, ng_ref, w_ref, dy_ref, dz_ref, yn_ref, dhb_ref, dng_ref):
        @pl.when(pl.program_id(0) == 0)
        def _():
            dng_ref[...] = jnp.zeros_like(dng_ref)

        dhb = dh_ref[...].astype(BF16)
        dhb_ref[...] = dhb
        dyn = lax.dot_general(dhb, w_ref[...], _NT, preferred_element_type=F32)
        yn, vjp = jax.vjp(_gate_norm, y_ref[...], z_ref[...], ng_ref[...])
        dy, dz, dng = vjp(dyn)
        dy_ref[...] = dy
        dz_ref[...] = dz.astype(BF16)
        yn_ref[...] = yn.astype(BF16)
        dng_ref[...] += dng

    wide = pl.BlockSpec((tt, D_INNER), lambda i: (i, 0))
    narrow = pl.BlockSpec((tt, D_MODEL), lambda i: (i, 0))
    gain = pl.BlockSpec((1, D_INNER), lambda i: (0, 0))
    return pl.pallas_call(
        body, name="ssm_out_bwd", grid=(t // tt,),
        in_specs=[narrow, wide, wide, gain, pl.BlockSpec((D_INNER, D_MODEL), lambda i: (0, 0))],
        out_specs=[wide, wide, wide, narrow, gain],
        out_shape=[jax.ShapeDtypeStruct((t, D_INNER), F32), jax.ShapeDtypeStruct((t, D_INNER), BF16),
                   jax.ShapeDtypeStruct((t, D_INNER), BF16), jax.ShapeDtypeStruct((t, D_MODEL), BF16),
                   jax.ShapeDtypeStruct((1, D_INNER), F32)],
        compiler_params=_cp(("arbitrary",)),
    )(dh, y, zx, ng, w_out)


def _final(h, g, tgt):
    t = h.shape[0]
    tt = min(t, 512)
    nt = t // tt

    def body(h_ref, g_ref, t_ref, dh_ref, loss_ref, dg_ref, lacc):
        i = pl.program_id(0)

        @pl.when(i == 0)
        def _():
            dg_ref[...] = jnp.zeros_like(dg_ref)
            lacc[...] = jnp.zeros_like(lacc)

        gv = g_ref[...]
        y, n, r = _rms_fwd(h_ref[...], gv)
        err = y - t_ref[...]
        lacc[...] += jnp.sum(err * err, axis=0, keepdims=True)
        dx, dg = _rms_bwd(err * (1.0 / D_MODEL), n, r, gv)
        dh_ref[...] = dx
        dg_ref[...] += dg

        @pl.when(i == nt - 1)
        def _():
            loss_ref[...] = jnp.zeros_like(loss_ref) + (0.5 / D_MODEL) * jnp.sum(lacc[...])

    tile = pl.BlockSpec((tt, D_MODEL), lambda i: (i, 0))
    vec = pl.BlockSpec((1, D_MODEL), lambda i: (0, 0))
    return pl.pallas_call(
        body, name="final_loss", grid=(nt,),
        in_specs=[tile, vec, tile],
        out_specs=[tile, pl.BlockSpec((1, 128), lambda i: (0, 0)), vec],
        out_shape=[jax.ShapeDtypeStruct((t, D_MODEL), F32), jax.ShapeDtypeStruct((1, 128), F32),
                   jax.ShapeDtypeStruct((1, D_MODEL), F32)],
        scratch_shapes=[pltpu.VMEM((1, D_MODEL), F32)],
        compiler_params=_cp(("arbitrary",)),
    )(h, g, tgt)


_ANY = pl.BlockSpec(memory_space=pl.ANY)


def _place():
    return lax.axis_index("x"), lax.axis_index("y"), lax.axis_index("c")


def _all_gather_rows(wp):
    r, n = wp.shape

    def body(x_ref, out_ref, send_sems, recv_sems, local_sem):
        x, y, c = _place()
        me, sibling = (x, y, c), (x, y, 1 - c)
        chips = [(1 - x, y), (x, 1 - y), (1 - x, 1 - y)]

        def rows(px, py, pc):
            return out_ref.at[4 * px + 2 * py + pc]

        def copy(k, block, to, src=None):
            return pltpu.make_async_remote_copy(
                src_ref=rows(*block) if src is None else src, dst_ref=rows(*block),
                send_sem=send_sems.at[k], recv_sem=recv_sems.at[k], device_id=to, device_id_type=MESH)

        mine = pltpu.make_async_copy(x_ref, rows(*me), local_sem)
        mine.start()
        first = [copy(0, me, sibling, src=x_ref)]
        first += [copy(1 + j, me, (*chip, c), src=x_ref) for j, chip in enumerate(chips)]
        for cp in first:
            cp.start()
        passed = [copy(4 + j, (*chip, c), sibling) for j, chip in enumerate(chips)]
        for j, chip in enumerate(chips):
            copy(1 + j, (*chip, c), me).wait_recv()
            passed[j].start()
        copy(0, sibling, me).wait_recv()
        for j, chip in enumerate(chips):
            copy(4 + j, (*chip, 1 - c), me).wait_recv()
        for cp in first + passed:
            cp.wait_send()
        mine.wait()

    return pl.pallas_call(
        body, name="gather_weights",
        out_shape=jax.ShapeDtypeStruct((N_DEV, r, n), wp.dtype),
        in_specs=[_ANY], out_specs=_ANY,
        scratch_shapes=[pltpu.SemaphoreType.DMA((7,)), pltpu.SemaphoreType.DMA((7,)), pltpu.SemaphoreType.DMA(())],
    )(wp)


def _exchange_cores(gp):
    _, r, n = gp.shape

    def body(g_ref, land_ref, send_sems, recv_sems):
        x, y, c = _place()
        sibling = (x, y, 1 - c)

        def copy(q):
            return pltpu.make_async_remote_copy(
                src_ref=g_ref.at[2 * q + (1 - c)], dst_ref=land_ref.at[q],
                send_sem=send_sems.at[q], recv_sem=recv_sems.at[q], device_id=sibling, device_id_type=MESH)

        cps = [copy(q) for q in range(4)]
        for cp in cps:
            cp.start()
        for cp in cps:
            cp.wait_recv()
        for cp in cps:
            cp.wait_send()

    return pl.pallas_call(
        body, name="reduce_exchange_cores",
        out_shape=jax.ShapeDtypeStruct((4, r, n), gp.dtype),
        in_specs=[_ANY], out_specs=_ANY,
        scratch_shapes=[pltpu.SemaphoreType.DMA((4,)), pltpu.SemaphoreType.DMA((4,))],
    )(gp)


def _add_core_pair(cidx, gp, land):
    _, r, n = gp.shape
    gp4 = gp.reshape(4, 2, r, n)

    def body(c_ref, a_ref, b_ref, o_ref):
        o_ref[...] = (a_ref[...].astype(F32) + b_ref[...].astype(F32)).astype(o_ref.dtype)

    return pl.pallas_call(
        body, name="reduce_add_core_pair",
        grid_spec=pltpu.PrefetchScalarGridSpec(
            num_scalar_prefetch=1, grid=(4, r // PACK_ROW_BLK),
            in_specs=[pl.BlockSpec((None, None, PACK_ROW_BLK, n), lambda q, i, cref: (q, cref[0], i, 0)),
                      pl.BlockSpec((None, PACK_ROW_BLK, n), lambda q, i, cref: (q, i, 0))],
            out_specs=pl.BlockSpec((None, PACK_ROW_BLK, n), lambda q, i, cref: (q, i, 0))),
        out_shape=jax.ShapeDtypeStruct((4, r, n), gp.dtype),
        compiler_params=_cp(("arbitrary", "arbitrary")),
    )(cidx, gp4, land)


def _exchange_chips(part):
    _, r, n = part.shape

    def body(p_ref, land_ref, send_sems, recv_sems, local_sem):
        x, y, c = _place()
        qme = 2 * x + y
        chips = [(1 - x, y), (x, 1 - y), (1 - x, 1 - y)]

        def copy(k, chip):
            q = 2 * chip[0] + chip[1]
            return pltpu.make_async_remote_copy(
                src_ref=p_ref.at[q], dst_ref=land_ref.at[qme],
                send_sem=send_sems.at[k], recv_sem=recv_sems.at[k], device_id=(*chip, c), device_id_type=MESH)

        def arrival(k, chip):
            q = 2 * chip[0] + chip[1]
            return pltpu.make_async_remote_copy(
                src_ref=p_ref.at[q], dst_ref=land_ref.at[q],
                send_sem=send_sems.at[k], recv_sem=recv_sems.at[k], device_id=(*chip, c), device_id_type=MESH)

        mine = pltpu.make_async_copy(p_ref.at[qme], land_ref.at[qme], local_sem)
        mine.start()
        cps = [copy(k, chip) for k, chip in enumerate(chips)]
        for cp in cps:
            cp.start()
        for k, chip in enumerate(chips):
            arrival(k, chip).wait_recv()
        for cp in cps:
            cp.wait_send()
        mine.wait()

    return pl.pallas_call(
        body, name="reduce_exchange_chips",
        out_shape=jax.ShapeDtypeStruct((4, r, n), part.dtype),
        in_specs=[_ANY], out_specs=_ANY,
        scratch_shapes=[pltpu.SemaphoreType.DMA((3,)), pltpu.SemaphoreType.DMA((3,)), pltpu.SemaphoreType.DMA(())],
    )(part)


def _add_chips(land):
    _, r, n = land.shape

    def body(a_ref, o_ref):
        acc = a_ref[0].astype(F32)
        for q in range(1, 4):
            acc = acc + a_ref[q].astype(F32)
        o_ref[...] = acc

    return pl.pallas_call(
        body, name="reduce_add_chips", grid=(r // PACK_ROW_BLK,),
        in_specs=[pl.BlockSpec((4, PACK_ROW_BLK, n), lambda i: (0, i, 0))],
        out_specs=pl.BlockSpec((PACK_ROW_BLK, n), lambda i: (i, 0)),
        out_shape=jax.ShapeDtypeStruct((r, n), F32),
        compiler_params=_cp(("arbitrary",)),
    )(land)


def _all_reduce_small(sp):
    rows, n = sp.shape

    def body(x_ref, o_ref, land, send_sems, recv_sems):
        x, y, c = _place()
        me = 4 * x + 2 * y + c
        land[me] = x_ref[...]
        cps = []
        for rel in range(1, N_DEV):
            dx, dy, dc = (rel >> 2) & 1, (rel >> 1) & 1, rel & 1
            px = x + dx - 2 * x * dx
            py = y + dy - 2 * y * dy
            pc = c + dc - 2 * c * dc
            peer = 4 * px + 2 * py + pc
            cps.append((pltpu.make_async_remote_copy(
                src_ref=x_ref, dst_ref=land.at[me], send_sem=send_sems.at[rel - 1], recv_sem=recv_sems.at[rel - 1],
                device_id=(px, py, pc), device_id_type=MESH),
                pltpu.make_async_remote_copy(
                src_ref=x_ref, dst_ref=land.at[peer], send_sem=send_sems.at[rel - 1], recv_sem=recv_sems.at[rel - 1],
                device_id=(px, py, pc), device_id_type=MESH)))
        for cp, _ in cps:
            cp.start()
        for _, arr in cps:
            arr.wait_recv()
        for cp, _ in cps:
            cp.wait_send()
        acc = land[0]
        for k in range(1, N_DEV):
            acc = acc + land[k]
        o_ref[...] = acc

    vm = pl.BlockSpec(memory_space=pltpu.VMEM)
    return pl.pallas_call(
        body, name="all_reduce_small",
        out_shape=jax.ShapeDtypeStruct((rows, n), F32),
        in_specs=[vm], out_specs=vm,
        scratch_shapes=[pltpu.VMEM((N_DEV, rows, n), F32),
                        pltpu.SemaphoreType.DMA((N_DEV - 1,)), pltpu.SemaphoreType.DMA((N_DEV - 1,))],
    )(sp)


def _adamw(w, g, m, v, name):
    rows, cols = w.shape
    br = rows if rows <= 256 else 256

    def body(w_ref, g_ref, m_ref, v_ref, d_ref, m2_ref, v2_ref):
        gv = g_ref[...]
        m2 = ADAM_B1 * m_ref[...] + (1.0 - ADAM_B1) * gv
        v2 = ADAM_B2 * v_ref[...] + (1.0 - ADAM_B2) * (gv * gv)
        m_hat = m2 / (1.0 - ADAM_B1 ** ADAM_STEP)
        v_hat = v2 / (1.0 - ADAM_B2 ** ADAM_STEP)
        d_ref[...] = -ADAM_LR * (m_hat / (jnp.sqrt(v_hat) + ADAM_EPS) + ADAM_WD * w_ref[...])
        m2_ref[...] = m2
        v2_ref[...] = v2

    spec = pl.BlockSpec((br, cols), lambda i: (i, 0))
    out = jax.ShapeDtypeStruct((rows, cols), F32)
    return pl.pallas_call(
        body, name=name, grid=(rows // br,),
        in_specs=[spec] * 4, out_specs=[spec] * 3, out_shape=[out] * 3,
        compiler_params=_cp(("arbitrary",)),
    )(w, g, m, v)


def _pad_rows(a, rows):
    return jnp.pad(a, ((0, rows - a.shape[0]), (0, 0)))


def _pack_weight_shards(pool_w, w_in, conv_w, conv_b, norm_g, w_out, w1, w2):
    small = jnp.concatenate([conv_w.reshape(-1), conv_b.reshape(-1), norm_g.reshape(-1)])
    small = jnp.pad(small, (0, ROWS_SMALL * 512 - N_SMALL))
    small = lax.bitcast_convert_type(small, BF16).reshape(ROWS_SMALL, 1024)
    parts = [pool_w.reshape(ROWS_POOL, 1024), _pad_rows(w_in.reshape(ROWS_WIN, 1024), ROWS_WIN_PAD),
             w_out.reshape(ROWS_WOUT, 1024), w1.reshape(ROWS_W1, 1024), w2.reshape(ROWS_W2, 1024)]
    return jnp.concatenate([p.astype(BF16) for p in parts] + [small], axis=0)


def _spread_dt(cols):
    lead = cols.shape[:-1]
    c4 = cols.reshape(lead + (N_GROUPS, HEADS_PER_GROUP))
    return jnp.pad(c4, [(0, 0)] * (len(lead) + 1) + [(0, 128 - HEADS_PER_GROUP)]).reshape(lead + (N_GROUPS * 128,))


def _gather_dt(cols):
    lead = cols.shape[:-1]
    return cols.reshape(lead + (N_GROUPS, 128))[..., :HEADS_PER_GROUP].reshape(lead + (N_HEADS,))


def _unpack_weights(wg):
    pool = wg[:, OFF_POOL:OFF_POOL + ROWS_POOL].reshape(N_DEV, 4, 32, POOL_GROUP)
    pool = pool.transpose(1, 0, 2, 3).reshape(4, POOL_GROUP, POOL_GROUP)
    w_in = wg[:, OFF_WIN:OFF_WIN + ROWS_WIN].reshape(N_DEV, D_MODEL, IN_PROJ_DIM // N_DEV)
    w_in = w_in.transpose(1, 0, 2).reshape(D_MODEL, IN_PROJ_DIM)
    split = D_INNER + CONV_DIM
    w_zx = jnp.concatenate([w_in[:, :split], _spread_dt(w_in[:, split:])], axis=1)
    w_out = wg[:, OFF_WOUT:OFF_WOUT + ROWS_WOUT].reshape(D_INNER, D_MODEL)
    w1 = wg[:, OFF_W1:OFF_W1 + ROWS_W1].reshape(N_DEV, 2, D_MODEL, D_FF // N_DEV)
    w2 = wg[:, OFF_W2:OFF_W2 + ROWS_W2].reshape(N_DEV, 2, D_FF // N_DEV, D_MODEL)
    w1 = [w1[:, l] for l in range(2)]
    w2 = [w2[:, l].reshape(D_FF, D_MODEL) for l in range(2)]
    small = wg[:, OFF_SMALL:OFF_SMALL + ROWS_SMALL].reshape(N_DEV, ROWS_SMALL * 512, 2)
    small = lax.bitcast_convert_type(small, F32)
    conv_w = small[:, :1536].reshape(N_DEV, 4, 384).transpose(1, 0, 2).reshape(4, CONV_DIM)
    conv_b = small[:, 1536:1920].reshape(1, CONV_DIM)
    norm_g = small[:, 1920:N_SMALL].reshape(1, D_INNER)
    return pool, w_zx, conv_w, conv_b, norm_g, w_out, w1, w2


def _pack_grads(d_pool, d_wzx, d_conv_w, d_conv_b, d_norm_g, d_wout, d_w1, d_w2):
    pool = d_pool.reshape(4, N_DEV, 32, POOL_GROUP).transpose(1, 0, 2, 3).reshape(N_DEV, ROWS_POOL, 1024)
    split = D_INNER + CONV_DIM
    w_in = jnp.concatenate([d_wzx[:, :split], _gather_dt(d_wzx[:, split:])], axis=1)
    w_in = w_in.reshape(D_MODEL, N_DEV, IN_PROJ_DIM // N_DEV).transpose(1, 0, 2).reshape(N_DEV, ROWS_WIN, 1024)
    w_in = jnp.pad(w_in, ((0, 0), (0, ROWS_WIN_PAD - ROWS_WIN), (0, 0)))
    w_out = d_wout.reshape(N_DEV, ROWS_WOUT, 1024)
    w1 = jnp.stack(d_w1, axis=1).reshape(N_DEV, ROWS_W1, 1024)
    w2 = jnp.stack([w.reshape(N_DEV, D_FF // N_DEV, D_MODEL) for w in d_w2], axis=1).reshape(N_DEV, ROWS_W2, 1024)
    small = jnp.concatenate([d_conv_w.reshape(4, N_DEV, 384).transpose(1, 0, 2).reshape(N_DEV, 1536),
                             d_conv_b.reshape(N_DEV, 384), d_norm_g.reshape(N_DEV, 256)], axis=1)
    small = jnp.pad(small, ((0, 0), (0, ROWS_SMALL * 1024 - N_SMALL))).reshape(N_DEV, ROWS_SMALL, 1024)
    return jnp.concatenate([p.astype(BF16) for p in (pool, w_in, w_out, w1, w2, small)], axis=1)


def _head_params(p):
    return jnp.pad(p.reshape(N_GROUPS, 1, HEADS_PER_GROUP), ((0, 0), (0, 0), (0, 128 - HEADS_PER_GROUP)))


def _update(w, g, m, v, name):
    shp = w.shape
    to2 = lambda a: a.reshape(-1, shp[-1])
    d, m2, v2 = _adamw(to2(w), to2(g), to2(m), to2(v), name)
    return d.reshape(shp), m2.reshape(shp), v2.reshape(shp)


def kernel(x, norm_mix_g, norm_mlp_g, pool_w, pool_b, pool_scale, ssm_w_in, ssm_conv_w, ssm_conv_b, ssm_dt_bias, ssm_a_log, ssm_d, ssm_norm_g, ssm_w_out, mlp_w1, mlp_w2, final_g, loss_target, m_norm_mix_g, m_norm_mlp_g, m_pool_w, m_pool_b, m_pool_scale, m_ssm_w_in, m_ssm_conv_w, m_ssm_conv_b, m_ssm_dt_bias, m_ssm_a_log, m_ssm_d, m_ssm_norm_g, m_ssm_w_out, m_mlp_w1, m_mlp_w2, m_final_g, v_norm_mix_g, v_norm_mlp_g, v_pool_w, v_pool_b, v_pool_scale, v_ssm_w_in, v_ssm_conv_w, v_ssm_conv_b, v_ssm_dt_bias, v_ssm_a_log, v_ssm_d, v_ssm_norm_g, v_ssm_w_out, v_mlp_w1, v_mlp_w2, v_final_g):
    x2 = x[0]
    tgt = loss_target[0]
    gm0, gm1 = norm_mix_g[0:1], norm_mix_g[1:2]
    gl0, gl1 = norm_mlp_g[0:1], norm_mlp_g[1:2]
    gfin = final_g.reshape(1, D_MODEL)

    wp = _pack_weight_shards(pool_w, ssm_w_in, ssm_conv_w, ssm_conv_b, ssm_norm_g, ssm_w_out, mlp_w1, mlp_w2)
    wg = _all_gather_rows(wp)
    w_pool, w_zx, conv_w, conv_b, ssm_ng, w_out, w1, w2 = _unpack_weights(wg)
    dtb, alog, dsk = _head_params(ssm_dt_bias), _head_params(ssm_a_log), _head_params(ssm_d)

    h1 = _pool_fwd(x2, gm0, w_pool, pool_b, pool_scale)
    h2, u0, hm0 = _mlp_fwd(h1, gl0, w1[0], w2[0], "mlp0_fwd")
    zx, hn1 = _norm_matmul(h2, gm1, w_zx)
    y_ssd, states = _ssd_fwd(zx, conv_w, conv_b, dtb, alog, dsk)
    h3 = _ssm_out_fwd(y_ssd, zx, ssm_ng, w_out, h2)
    h4, u1, hm1 = _mlp_fwd(h3, gl1, w1[1], w2[1], "mlp1_fwd")
    dh4, loss_row, d_gfin = _final(h4, gfin, tgt)

    dh3, da1, dh4b, d_gl1 = _mlp_bwd(dh4, h3, gl1, u1, w1[1], w2[1], "mlp1_bwd")
    d_w1_1 = _matmul_tn(hm1, da1, "mlp1_dw1", col_blocked=True)
    d_w2_1 = _matmul_tn(u1, dh4b, "mlp1_dw2", square_a=True)
    dy_ssd, dz, yn, dh3b, d_ng = _ssm_out_bwd(dh3, y_ssd, zx, ssm_ng, w_out)
    d_wout = _matmul_tn(yn, dh3b, "ssm_dw_out")
    (dxr, dbr, dcr, ddt, dcwx, dcbx, dcwb, dcbb, dcwc, dcbc, d_dtb, d_alog, d_dsk) = _ssd_bwd(
        zx, conv_w, conv_b, dtb, alog, dsk, states, dy_ssd)
    dzx = jnp.concatenate([dz, dxr, dbr, dcr, ddt], axis=1)
    dh2, d_gm1 = _in_proj_bwd(dzx, w_zx, h2, gm1, dh3)
    d_wzx = _matmul_tn(hn1, dzx, "ssm_dw_in")
    dh1, da0, dh2b, d_gl0 = _mlp_bwd(dh2, h1, gl0, u0, w1[0], w2[0], "mlp0_bwd")
    d_w1_0 = _matmul_tn(hm0, da0, "mlp0_dw1", col_blocked=True)
    d_w2_0 = _matmul_tn(u0, dh2b, "mlp0_dw2", square_a=True)
    dx, d_pool, d_pb, d_ps, d_gm0 = _pool_bwd(x2, dh1, gm0, w_pool, pool_b, pool_scale)

    d_conv_w = jnp.concatenate([dcwx, dcwb, dcwc], axis=1)
    d_conv_b = jnp.concatenate([dcbx, dcbb, dcbc], axis=1)
    gp = _pack_grads(d_pool, d_wzx, d_conv_w, d_conv_b, d_ng, d_wout, [d_w1_0, d_w1_1], [d_w2_0, d_w2_1])
    cidx = lax.axis_index("c").astype(jnp.int32).reshape(1)
    part = _add_core_pair(cidx, gp, _exchange_cores(gp))
    gsh = _add_chips(_exchange_chips(part))

    heads = jnp.concatenate([_gather_dt(a.reshape(1, N_GROUPS * 128)) for a in (d_dtb, d_alog, d_dsk)], axis=1)
    sp = jnp.concatenate([d_gm0, d_gm1, d_gl0, d_gl1, d_pb, d_ps, d_gfin,
                          jnp.pad(heads, ((0, 0), (0, D_MODEL - 3 * N_HEADS)))], axis=0)
    sg = _all_reduce_small(sp)

    g_norm_mix = sg[0:2]
    g_norm_mlp = sg[2:4]
    g_pool_b, g_pool_scale = sg[4:5], sg[5:6]
    g_final = sg[6]
    g_dtb, g_alog, g_dsk = sg[7:8, 0:32], sg[7:8, 32:64], sg[7:8, 64:96]

    g_pool_w = gsh[OFF_POOL:OFF_POOL + ROWS_POOL].reshape(pool_w.shape)
    g_w_in = gsh[OFF_WIN:OFF_WIN + ROWS_WIN].reshape(ssm_w_in.shape)
    g_w_out = gsh[OFF_WOUT:OFF_WOUT + ROWS_WOUT].reshape(ssm_w_out.shape)
    g_w1 = gsh[OFF_W1:OFF_W1 + ROWS_W1].reshape(mlp_w1.shape)
    g_w2 = gsh[OFF_W2:OFF_W2 + ROWS_W2].reshape(mlp_w2.shape)
    g_small = gsh[OFF_SMALL:OFF_SMALL + ROWS_SMALL].reshape(-1)
    g_conv_w = g_small[:1536].reshape(ssm_conv_w.shape)
    g_conv_b = g_small[1536:1920].reshape(ssm_conv_b.shape)
    g_norm_g = g_small[1920:N_SMALL].reshape(ssm_norm_g.shape)

    def rep_pack(nm, nl, pb, ps, fg, db, al, dk):
        hd = jnp.pad(jnp.concatenate([db, al, dk], axis=1), ((0, 0), (0, D_MODEL - 3 * N_HEADS)))
        return jnp.concatenate([nm, nl, pb, ps, fg.reshape(1, D_MODEL), hd], axis=0)

    rep = [rep_pack(*t) for t in (
        (norm_mix_g, norm_mlp_g, pool_b, pool_scale, final_g, ssm_dt_bias, ssm_a_log, ssm_d),
        (g_norm_mix, g_norm_mlp, g_pool_b, g_pool_scale, g_final, g_dtb, g_alog, g_dsk),
        (m_norm_mix_g, m_norm_mlp_g, m_pool_b, m_pool_scale, m_final_g, m_ssm_dt_bias, m_ssm_a_log, m_ssm_d),
        (v_norm_mix_g, v_norm_mlp_g, v_pool_b, v_pool_scale, v_final_g, v_ssm_dt_bias, v_ssm_a_log, v_ssm_d))]
    rep_out = _adamw(*rep, "adamw_replicated")

    def rep_unpack(a):
        return (a[0:2], a[2:4], a[4:5], a[5:6], a[6], a[7:8, 0:32], a[7:8, 32:64], a[7:8, 64:96])

    def small_pack(cw, cb, ng):
        flat = jnp.concatenate([cw.reshape(-1), cb.reshape(-1), ng.reshape(-1)])
        return jnp.pad(flat, (0, 3 * 1024 - N_SMALL)).reshape(3, 1024)

    sm = [small_pack(*t) for t in ((ssm_conv_w, ssm_conv_b, ssm_norm_g), (g_conv_w, g_conv_b, g_norm_g),
                                   (m_ssm_conv_w, m_ssm_conv_b, m_ssm_norm_g),
                                   (v_ssm_conv_w, v_ssm_conv_b, v_ssm_norm_g))]
    sm_out = _adamw(*sm, "adamw_small_shards")

    def small_unpack(a):
        f = a.reshape(-1)
        return (f[:1536].reshape(ssm_conv_w.shape), f[1536:1920].reshape(ssm_conv_b.shape),
                f[1920:N_SMALL].reshape(ssm_norm_g.shape))

    big = {
        "pool_w": _update(pool_w, g_pool_w, m_pool_w, v_pool_w, "adamw_pool_w"),
        "ssm_w_in": _update(ssm_w_in, g_w_in, m_ssm_w_in, v_ssm_w_in, "adamw_w_in"),
        "ssm_w_out": _update(ssm_w_out, g_w_out, m_ssm_w_out, v_ssm_w_out, "adamw_w_out"),
        "mlp_w1": _update(mlp_w1, g_w1, m_mlp_w1, v_mlp_w1, "adamw_w1"),
        "mlp_w2": _update(mlp_w2, g_w2, m_mlp_w2, v_mlp_w2, "adamw_w2"),
    }

    def ordered(kind):
        nm, nl, pb, ps, fg, db, al, dk = rep_unpack(rep_out[kind])
        cw, cb, ng = small_unpack(sm_out[kind])
        return [nm, nl, big["pool_w"][kind], pb, ps, big["ssm_w_in"][kind], cw, cb, db, al, dk, ng,
                big["ssm_w_out"][kind], big["mlp_w1"][kind], big["mlp_w2"][kind], fg]

    grads = [g_norm_mix, g_norm_mlp, g_pool_w, g_pool_b, g_pool_scale, g_w_in, g_conv_w, g_conv_b,
             g_dtb, g_alog, g_dsk, g_norm_g, g_w_out, g_w1, g_w2, g_final]
    loss = lax.psum(loss_row[0, 0], ("x", "y", "c"))
    return (loss, dx[None], *grads, *ordered(0), *ordered(1), *ordered(2))
```

```python
import functools

import jax
import jax.numpy as jnp
from jax import lax
from jax.experimental import pallas as pl
from jax.experimental.pallas import tpu as pltpu

F32 = jnp.float32
BF16 = jnp.bfloat16
MESH = pl.DeviceIdType.MESH

D_MODEL = 1024
RMS_EPS = 1e-5
POOL_WINDOWS = (2, 4, 8, 16)
POOL_GROUP = 256
POOL_HALO = 16
POOL_SHARD = POOL_GROUP // 8
D_INNER = 2048
HEAD_DIM = 64
N_HEADS = 32
N_GROUPS = 4
HEADS_PER_GROUP = 8
D_STATE = 128
CHUNK = 128
CONV_DIM = 3072
IN_PROJ_DIM = 5152
D_FF = 4096
N_DEV = 8
GROUP_X = HEADS_PER_GROUP * HEAD_DIM
GROUP_CONV = GROUP_X + 2 * D_STATE
GROUP_COLS = GROUP_CONV + 128
Z_OFF = N_GROUPS * GROUP_COLS
ZX_COLS = Z_OFF + D_INNER
COL_BLK = 512
W_IN_SHARD = IN_PROJ_DIM // N_DEV

ADAM_LR = 0.001
ADAM_B1 = 0.9
ADAM_B2 = 0.999
ADAM_EPS = 1e-08
ADAM_WD = 0.01
ADAM_STEP = 10

VMEM_LIMIT_V7X = 48 * 1024 * 1024

CONV_SHARD = CONV_DIM // N_DEV
SMALL_ROWS = 8

_NN = (((1,), (0,)), ((), ()))
_NT = (((1,), (1,)), ((), ()))
_TN = (((0,), (0,)), ((), ()))


def _cp(sem):
    return pltpu.CompilerParams(dimension_semantics=sem, vmem_limit_bytes=VMEM_LIMIT_V7X)


def _dg(a, b, dn):
    return lax.dot_general(a.astype(BF16), b.astype(BF16), dn, preferred_element_type=F32)


@jax.custom_vjp
def mm_nn(a, b):
    return _dg(a, b, _NN)


@jax.custom_vjp
def mm_nt(a, b):
    return _dg(a, b, _NT)


@jax.custom_vjp
def mm_tn(a, b):
    return _dg(a, b, _TN)


mm_nn.defvjp(lambda a, b: (_dg(a, b, _NN), (a, b)), lambda r, ct: (mm_nt(ct, r[1]), mm_tn(r[0], ct)))
mm_nt.defvjp(lambda a, b: (_dg(a, b, _NT), (a, b)), lambda r, ct: (mm_nn(ct, r[1]), mm_tn(ct, r[0])))
mm_tn.defvjp(lambda a, b: (_dg(a, b, _TN), (a, b)), lambda r, ct: (mm_nt(r[1], ct), mm_nn(r[0], ct)))


def _split3(x):
    p1 = x.astype(BF16)
    r1 = x - p1.astype(F32)
    p2 = r1.astype(BF16)
    r2 = r1 - p2.astype(F32)
    return p1, p2, r2.astype(BF16)


def _exact01(x, c, dn, const_left):
    acc = None
    for p in reversed(_split3(x)):
        t = (lax.dot_general(c, p, dn, preferred_element_type=F32) if const_left
             else lax.dot_general(p, c, dn, preferred_element_type=F32))
        acc = t if acc is None else acc + t
    return acc


def _make_cmm(dn, const_left, bwd_name):
    @jax.custom_vjp
    def f(x, c):
        return _exact01(x, c, dn, const_left)

    def fwd(x, c):
        return _exact01(x, c, dn, const_left), c

    def bwd(c, ct):
        return _CMM[bwd_name](ct, c), jnp.zeros_like(c)

    f.defvjp(fwd, bwd)
    return f


_CMM = {}
_CMM["xc"] = _make_cmm(_NN, False, "xct")
_CMM["xct"] = _make_cmm(_NT, False, "xc")
_CMM["cx"] = _make_cmm(_NN, True, "ctx")
_CMM["ctx"] = _make_cmm(_TN, True, "cx")


@jax.custom_vjp
def _silu(x):
    return x / (1.0 + jnp.exp(-x))


def _silu_fwd(x):
    return _silu(x), x


def _silu_bwd(x, ct):
    s = 1.0 / (1.0 + jnp.exp(-x))
    return (ct * (s * (1.0 + x * (1.0 - s))),)


_silu.defvjp(_silu_fwd, _silu_bwd)


def _log1p_pos(e):
    u = 1.0 + e
    d = u - 1.0
    return jnp.where(d == 0.0, e, jnp.log(u) * (e / jnp.where(d == 0.0, 1.0, d)))


@jax.custom_vjp
def _softplus(x):
    return jnp.maximum(x, 0.0) + _log1p_pos(jnp.exp(-jnp.abs(x)))


def _softplus_fwd(x):
    return _softplus(x), x


def _softplus_bwd(x, ct):
    return (ct / (1.0 + jnp.exp(-x)),)


_softplus.defvjp(_softplus_fwd, _softplus_bwd)


def _make_shift(j):
    @jax.custom_vjp
    def f(ext):
        return pltpu.roll(ext, j, 0)[CHUNK:, :]

    def fwd(ext):
        return f(ext), None

    def bwd(_, ct):
        pad = jnp.concatenate([jnp.zeros_like(ct), ct], axis=0)
        return (pltpu.roll(pad, 2 * CHUNK - j, 0),)

    f.defvjp(fwd, bwd)
    return f


_SHIFT = {j: _make_shift(j) for j in (1, 2, 3)}


def _rms_fwd(x, g):
    r = lax.rsqrt(jnp.mean(x * x, axis=-1, keepdims=True) + RMS_EPS)
    n = x * r
    return n * g, n, r


def _rms_bwd(dy, n, r, g):
    dn = dy * g
    dx = r * (dn - n * jnp.mean(dn * n, axis=-1, keepdims=True))
    dg = jnp.sum(dy * n, axis=0, keepdims=True)
    return dx, dg


def _one(cond):
    return jnp.where(cond, 1.0, 0.0)


def _pool_tile(xe, g, ws, b, scale, tile, tt):
    r = lax.rsqrt(jnp.mean(xe * xe, axis=-1, keepdims=True) + RMS_EPS)
    hn = xe * r * g
    row_e = lax.broadcasted_iota(jnp.int32, (tt + POOL_HALO, POOL_GROUP), 0)
    keep = _one(jnp.logical_or(row_e >= POOL_HALO, tile > 0))
    rr = lax.broadcasted_iota(jnp.int32, (tt, tt + POOL_HALO), 0)
    qq = lax.broadcasted_iota(jnp.int32, (tt, tt + POOL_HALO), 1)
    dd = qq - rr
    tpos = tile * tt + lax.broadcasted_iota(jnp.int32, (tt, POOL_GROUP), 0)
    outs = []
    for gi, w in enumerate(POOL_WINDOWS):
        hg = hn[:, gi * POOL_GROUP:(gi + 1) * POOL_GROUP] * keep
        band = _one(jnp.logical_and(dd >= POOL_HALO - w + 1, dd <= POOL_HALO)).astype(BF16)
        cnt = jnp.minimum(tpos + 1, w).astype(F32)
        pooled = _CMM["cx"](hg, band) / cnt
        mixed = pooled - hg[POOL_HALO:, :]
        outs.append(mm_nn(mixed, ws[gi]))
    out = (jnp.concatenate(outs, axis=1) + b) * scale
    return xe[POOL_HALO:, :] + out


def _pool_specs(tt, nt, rev):
    per = tt // POOL_HALO
    t_of = (lambda i: nt - 1 - i) if rev else (lambda i: i)
    main = pl.BlockSpec((tt, D_MODEL), lambda i: (t_of(i), 0))
    halo = pl.BlockSpec((POOL_HALO, D_MODEL), lambda i: (jnp.maximum(t_of(i) * per - 1, 0), 0))
    vec = pl.BlockSpec((1, D_MODEL), lambda i: (0, 0))
    wsp = pl.BlockSpec((N_DEV, 4 * POOL_SHARD, POOL_GROUP), lambda i: (0, 0, 0))
    return main, halo, vec, wsp


def _pool_weights(w_ref):
    return tuple(
        jnp.concatenate([w_ref[k, gi * POOL_SHARD:(gi + 1) * POOL_SHARD, :] for k in range(N_DEV)], axis=0).astype(F32)
        for gi in range(4))


def _pool_fwd(x, g, w, b, scale):
    t = x.shape[0]
    tt = min(t, 256)
    nt = t // tt
    main, halo, vec, wsp = _pool_specs(tt, nt, False)

    def body(xm_ref, xh_ref, g_ref, w_ref, b_ref, s_ref, o_ref):
        i = pl.program_id(0)
        xe = jnp.concatenate([xh_ref[...], xm_ref[...]], axis=0)
        o_ref[...] = _pool_tile(xe, g_ref[...], _pool_weights(w_ref), b_ref[...], s_ref[...], i, tt)

    return pl.pallas_call(
        body, name="pool_fwd", grid=(nt,),
        in_specs=[main, halo, vec, wsp, vec, vec], out_specs=main,
        out_shape=jax.ShapeDtypeStruct((t, D_MODEL), F32),
        compiler_params=_cp(("arbitrary",)),
    )(x, x, g, w, b, scale)


def _pool_bwd(x, dh, g, w, b, scale):
    t = x.shape[0]
    tt = min(t, 256)
    nt = t // tt
    main, halo, vec, wsp = _pool_specs(tt, nt, True)

    def body(xm_ref, xh_ref, dh_ref, g_ref, w_ref, b_ref, s_ref,
             dx_ref, dw_ref, db_ref, ds_ref, dg_ref, carry, dw_acc):
        i = pl.program_id(0)
        tile = nt - 1 - i

        @pl.when(i == 0)
        def _():
            carry[...] = jnp.zeros_like(carry)
            dw_acc[...] = jnp.zeros_like(dw_acc)
            db_ref[...] = jnp.zeros_like(db_ref)
            ds_ref[...] = jnp.zeros_like(ds_ref)
            dg_ref[...] = jnp.zeros_like(dg_ref)

        xe = jnp.concatenate([xh_ref[...], xm_ref[...]], axis=0)
        _, vjp = jax.vjp(lambda a, gg, ww, bb, ss: _pool_tile(a, gg, ww, bb, ss, tile, tt),
                         xe, g_ref[...], _pool_weights(w_ref), b_ref[...], s_ref[...])
        dxe, dgv, dws, dbv, dsv = vjp(dh_ref[...])
        dx_ref[...] = dxe[POOL_HALO:, :]
        dx_ref[tt - POOL_HALO:tt, :] += carry[...]
        carry[...] = dxe[:POOL_HALO, :]
        for gi in range(4):
            dw_acc[gi] += dws[gi]
        db_ref[...] += dbv
        ds_ref[...] += dsv
        dg_ref[...] += dgv

        @pl.when(i == nt - 1)
        def _():
            for k in range(N_DEV):
                for gi in range(4):
                    dw_ref[k, gi * POOL_SHARD:(gi + 1) * POOL_SHARD, :] = dw_acc[gi, k * POOL_SHARD:(k + 1) * POOL_SHARD, :]

    return pl.pallas_call(
        body, name="pool_bwd", grid=(nt,),
        in_specs=[main, halo, main, vec, wsp, vec, vec],
        out_specs=[main, wsp, vec, vec, vec],
        out_shape=[jax.ShapeDtypeStruct((t, D_MODEL), F32),
                   jax.ShapeDtypeStruct((N_DEV, 4 * POOL_SHARD, POOL_GROUP), F32),
                   jax.ShapeDtypeStruct((1, D_MODEL), F32),
                   jax.ShapeDtypeStruct((1, D_MODEL), F32),
                   jax.ShapeDtypeStruct((1, D_MODEL), F32)],
        scratch_shapes=[pltpu.VMEM((POOL_HALO, D_MODEL), F32), pltpu.VMEM((4, POOL_GROUP, POOL_GROUP), F32)],
        compiler_params=_cp(("arbitrary",)),
    )(x, x, dh, g, w, b, scale)


def _mlp_weight_specs(layer):
    fb = D_FF // N_DEV
    return (pl.BlockSpec((None, None, D_MODEL, fb), lambda i, k: (k, layer, 0, 0)),
            pl.BlockSpec((None, None, fb, D_MODEL), lambda i, k: (k, layer, 0, 0)))


def _mlp_fwd(h, g, w1g, w2g, layer, name):
    t = h.shape[0]
    tt = min(t, 512)
    nk, fb = N_DEV, D_FF // N_DEV
    w1_spec, w2_spec = _mlp_weight_specs(layer)

    def body(h_ref, g_ref, w1_ref, w2_ref, o_ref, u_ref, hm_ref, hm_s, acc_s):
        k = pl.program_id(1)

        @pl.when(k == 0)
        def _():
            xv = h_ref[...]
            y, _, _ = _rms_fwd(xv, g_ref[...])
            hb = y.astype(BF16)
            hm_s[...] = hb
            hm_ref[...] = hb
            acc_s[...] = xv

        a = jnp.dot(hm_s[...], w1_ref[...], preferred_element_type=F32)
        u = jnp.maximum(a, 0.0)
        u_ref[...] = u.astype(BF16)
        acc_s[...] += jnp.dot((u * u).astype(BF16), w2_ref[...], preferred_element_type=F32)

        @pl.when(k == nk - 1)
        def _():
            o_ref[...] = acc_s[...]

    return pl.pallas_call(
        body, name=name, grid=(t // tt, nk),
        in_specs=[pl.BlockSpec((tt, D_MODEL), lambda i, k: (i, 0)),
                  pl.BlockSpec((1, D_MODEL), lambda i, k: (0, 0)),
                  w1_spec, w2_spec],
        out_specs=[pl.BlockSpec((tt, D_MODEL), lambda i, k: (i, 0)),
                   pl.BlockSpec((tt, fb), lambda i, k: (i, k)),
                   pl.BlockSpec((tt, D_MODEL), lambda i, k: (i, 0))],
        out_shape=[jax.ShapeDtypeStruct((t, D_MODEL), F32),
                   jax.ShapeDtypeStruct((t, nk * fb), BF16),
                   jax.ShapeDtypeStruct((t, D_MODEL), BF16)],
        scratch_shapes=[pltpu.VMEM((tt, D_MODEL), BF16), pltpu.VMEM((tt, D_MODEL), F32)],
        compiler_params=_cp(("arbitrary", "arbitrary")),
    )(h, g, w1g, w2g)


def _mlp_bwd(dh, h, g, u, w1g, w2g, layer, name):
    t = h.shape[0]
    tt = min(t, 512)
    nk, fb = N_DEV, D_FF // N_DEV
    w1_spec, w2_spec = _mlp_weight_specs(layer)

    def body(dh_ref, h_ref, g_ref, u_ref, w1_ref, w2_ref,
             dhin_ref, da_ref, dhb_ref, dg_ref, dhb_s, acc_s):
        i = pl.program_id(0)
        k = pl.program_id(1)

        @pl.when(jnp.logical_and(i == 0, k == 0))
        def _():
            dg_ref[...] = jnp.zeros_like(dg_ref)

        @pl.when(k == 0)
        def _():
            db = dh_ref[...].astype(BF16)
            dhb_s[...] = db
            dhb_ref[...] = db
            acc_s[...] = jnp.zeros_like(acc_s)

        dv = lax.dot_general(dhb_s[...], w2_ref[...], _NT, preferred_element_type=F32)
        dab = (dv * (2.0 * u_ref[...].astype(F32))).astype(BF16)
        da_ref[...] = dab
        acc_s[...] += lax.dot_general(dab, w1_ref[...], _NT, preferred_element_type=F32)

        @pl.when(k == nk - 1)
        def _():
            gv = g_ref[...]
            _, n, r = _rms_fwd(h_ref[...], gv)
            dx, dg = _rms_bwd(acc_s[...], n, r, gv)
            dhin_ref[...] = dh_ref[...] + dx
            dg_ref[...] += dg

    return pl.pallas_call(
        body, name=name, grid=(t // tt, nk),
        in_specs=[pl.BlockSpec((tt, D_MODEL), lambda i, k: (i, 0)),
                  pl.BlockSpec((tt, D_MODEL), lambda i, k: (i, 0)),
                  pl.BlockSpec((1, D_MODEL), lambda i, k: (0, 0)),
                  pl.BlockSpec((tt, fb), lambda i, k: (i, k)),
                  w1_spec, w2_spec],
        out_specs=[pl.BlockSpec((tt, D_MODEL), lambda i, k: (i, 0)),
                   pl.BlockSpec((tt, fb), lambda i, k: (i, k)),
                   pl.BlockSpec((tt, D_MODEL), lambda i, k: (i, 0)),
                   pl.BlockSpec((1, D_MODEL), lambda i, k: (0, 0))],
        out_shape=[jax.ShapeDtypeStruct((t, D_MODEL), F32),
                   jax.ShapeDtypeStruct((t, nk * fb), BF16),
                   jax.ShapeDtypeStruct((t, D_MODEL), BF16),
                   jax.ShapeDtypeStruct((1, D_MODEL), F32)],
        scratch_shapes=[pltpu.VMEM((tt, D_MODEL), BF16), pltpu.VMEM((tt, D_MODEL), F32)],
        compiler_params=_cp(("arbitrary", "arbitrary")),
    )(dh, h, g, u, w1g, w2g)


def _matmul_tn(a, b, name, square_a=False, col_blocked=False):
    t, k1 = a.shape
    k2 = b.shape[1]
    b1 = min(k1, 1024)
    tt = min(t, 512)
    nt = t // tt

    def body(a_ref, b_ref, o_ref, acc):
        s = pl.program_id(2)

        @pl.when(s == 0)
        def _():
            acc[...] = jnp.zeros_like(acc)

        av = a_ref[...]
        if square_a:
            af = av.astype(F32)
            av = (af * af).astype(BF16)
        acc[...] += lax.dot_general(av, b_ref[...], _TN, preferred_element_type=F32)

        @pl.when(s == nt - 1)
        def _():
            o_ref[...] = acc[...].astype(o_ref.dtype)

    if col_blocked:
        out_shape = jax.ShapeDtypeStruct((k2 // COL_BLK, k1, COL_BLK), BF16)
        out_spec = pl.BlockSpec((None, b1, COL_BLK), lambda i, j, s: (j, i, 0))
    else:
        out_shape = jax.ShapeDtypeStruct((k1, k2), BF16)
        out_spec = pl.BlockSpec((b1, COL_BLK), lambda i, j, s: (i, j))
    return pl.pallas_call(
        body, name=name, grid=(k1 // b1, k2 // COL_BLK, nt),
        in_specs=[pl.BlockSpec((tt, b1), lambda i, j, s: (s, i)),
                  pl.BlockSpec((tt, COL_BLK), lambda i, j, s: (s, j))],
        out_specs=out_spec, out_shape=out_shape,
        scratch_shapes=[pltpu.VMEM((b1, COL_BLK), F32)],
        compiler_params=_cp(("arbitrary", "arbitrary", "arbitrary")),
    )(a, b)


def _norm_matmul(h, g, w):
    t = h.shape[0]
    tt = min(t, 512)
    n = w.shape[1]

    def body(h_ref, g_ref, w_ref, o_ref, hn_ref, hn_s):
        @pl.when(pl.program_id(1) == 0)
        def _():
            y, _, _ = _rms_fwd(h_ref[...], g_ref[...])
            hb = y.astype(BF16)
            hn_s[...] = hb
            hn_ref[...] = hb

        o_ref[...] = jnp.dot(hn_s[...], w_ref[...], preferred_element_type=F32)

    return pl.pallas_call(
        body, name="ssm_in_proj", grid=(t // tt, n // COL_BLK),
        in_specs=[pl.BlockSpec((tt, D_MODEL), lambda i, j: (i, 0)),
                  pl.BlockSpec((1, D_MODEL), lambda i, j: (0, 0)),
                  pl.BlockSpec((D_MODEL, COL_BLK), lambda i, j: (0, j))],
        out_specs=[pl.BlockSpec((tt, COL_BLK), lambda i, j: (i, j)),
                   pl.BlockSpec((tt, D_MODEL), lambda i, j: (i, 0))],
        out_shape=[jax.ShapeDtypeStruct((t, n), F32), jax.ShapeDtypeStruct((t, D_MODEL), BF16)],
        scratch_shapes=[pltpu.VMEM((tt, D_MODEL), BF16)],
        compiler_params=_cp(("arbitrary", "arbitrary")),
    )(h, g, w)


def _in_proj_bwd(dzx, w, h, g, dh_next):
    t = h.shape[0]
    tt = min(t, 512)
    n = w.shape[1]
    nj = n // COL_BLK

    def body(dz_ref, w_ref, h_ref, g_ref, dn_ref, dh_ref, dg_ref, acc):
        i = pl.program_id(0)
        j = pl.program_id(1)

        @pl.when(jnp.logical_and(i == 0, j == 0))
        def _():
            dg_ref[...] = jnp.zeros_like(dg_ref)

        @pl.when(j == 0)
        def _():
            acc[...] = jnp.zeros_like(acc)

        acc[...] += lax.dot_general(dz_ref[...], w_ref[...], _NT, preferred_element_type=F32)

        @pl.when(j == nj - 1)
        def _():
            gv = g_ref[...]
            _, nn, r = _rms_fwd(h_ref[...], gv)
            dx, dg = _rms_bwd(acc[...], nn, r, gv)
            dh_ref[...] = dn_ref[...] + dx
            dg_ref[...] += dg

    return pl.pallas_call(
        body, name="ssm_in_proj_bwd", grid=(t // tt, nj),
        in_specs=[pl.BlockSpec((tt, COL_BLK), lambda i, j: (i, j)),
                  pl.BlockSpec((D_MODEL, COL_BLK), lambda i, j: (0, j)),
                  pl.BlockSpec((tt, D_MODEL), lambda i, j: (i, 0)),
                  pl.BlockSpec((1, D_MODEL), lambda i, j: (0, 0)),
                  pl.BlockSpec((tt, D_MODEL), lambda i, j: (i, 0))],
        out_specs=[pl.BlockSpec((tt, D_MODEL), lambda i, j: (i, 0)),
                   pl.BlockSpec((1, D_MODEL), lambda i, j: (0, 0))],
        out_shape=[jax.ShapeDtypeStruct((t, D_MODEL), F32), jax.ShapeDtypeStruct((1, D_MODEL), F32)],
        scratch_shapes=[pltpu.VMEM((tt, D_MODEL), F32)],
        compiler_params=_cp(("arbitrary", "arbitrary")),
    )(dzx, w, h, g, dh_next)


def _ssd_consts():
    lane = lax.broadcasted_iota(jnp.int32, (CHUNK, CHUNK), 1)
    row = lax.broadcasted_iota(jnp.int32, (CHUNK, CHUNK), 0)
    causal = lane <= row
    tri = _one(causal).astype(BF16)
    er = lax.broadcasted_iota(jnp.int32, (CHUNK, GROUP_X), 0)
    ec = lax.broadcasted_iota(jnp.int32, (CHUNK, GROUP_X), 1)
    expand = _one(jnp.right_shift(ec, 6) == er).astype(BF16)
    e2r = lax.broadcasted_iota(jnp.int32, (CHUNK, HEADS_PER_GROUP * CHUNK), 0)
    e2c = lax.broadcasted_iota(jnp.int32, (CHUNK, HEADS_PER_GROUP * CHUNK), 1)
    expand2 = _one(jnp.right_shift(e2c, 7) == e2r).astype(BF16)
    return dict(causal=causal, tri=tri, expand=expand, expand2=expand2, lo=lane < HEAD_DIM)


def _conv_silu(cur, prev, w, b):
    ext = jnp.concatenate([prev, cur], axis=0)
    acc = cur * w[3] + b
    for j in (1, 2, 3):
        acc = acc + _SHIFT[j](ext) * w[3 - j]
    return _silu(acc)


def _ssd_chunk(raw, rawp, ht, cw, cb_, dtb, alog, dsk, k):
    act = _conv_silu(raw[:, :GROUP_CONV], rawp[:, :GROUP_CONV], cw, cb_)
    xs = act[:, :GROUP_X]
    bm = act[:, GROUP_X:GROUP_X + D_STATE]
    cm = act[:, GROUP_X + D_STATE:]
    dt = _softplus(raw[:, GROUP_CONV:] + dtb)
    a = -jnp.exp(alog)
    adt = dt * a
    xc = _CMM["xc"]

    def lanes(rowv):
        return jnp.sum(xc(jnp.broadcast_to(rowv, (16, CHUNK)), k["expand"]), axis=0, keepdims=True) * (1.0 / 16.0)

    dt_e = xc(dt, k["expand"])
    adt_e = dt_e * lanes(a)
    acs_e = _CMM["cx"](adt_e, k["tri"])
    acs = _CMM["cx"](adt, k["tri"])
    tot_e = jnp.sum(adt_e, axis=0, keepdims=True)
    cb_all = xc(acs, k["expand2"])
    gmat = mm_nt(cm, bm)
    xdt = xs * dt_e
    ys = []
    for j in range(HEADS_PER_GROUP // 2):
        ms = []
        for hh in (2 * j, 2 * j + 1):
            cb = cb_all[:, hh * CHUNK:(hh + 1) * CHUNK]
            seg = cb - cb.T
            ms.append(gmat * jnp.exp(jnp.where(k["causal"], seg, -jnp.inf)))
        xp = xdt[:, j * CHUNK:(j + 1) * CHUNK]
        rhs = jnp.concatenate([jnp.where(k["lo"], xp, 0.0), jnp.where(k["lo"], 0.0, xp)], axis=0)
        ys.append(mm_nn(jnp.concatenate(ms, axis=1), rhs))
    y_diag = jnp.concatenate(ys, axis=1)
    y_off = jnp.exp(acs_e) * mm_nn(cm, ht)
    h_new = jnp.exp(tot_e) * ht + mm_tn(bm, xdt * jnp.exp(tot_e - acs_e))
    return y_diag + y_off + lanes(dsk) * xs, h_new


def _ssd_in_specs(nc, rev):
    c_of = (lambda c: nc - 1 - c) if rev else (lambda c: c)
    zx = [pl.BlockSpec((CHUNK, GROUP_COLS), lambda g, c: (c_of(c), g)),
          pl.BlockSpec((CHUNK, GROUP_COLS), lambda g, c: (jnp.maximum(c_of(c) - 1, 0), g))]
    conv = [pl.BlockSpec((4, GROUP_CONV), lambda g, c: (0, g)), pl.BlockSpec((1, GROUP_CONV), lambda g, c: (0, g))]
    head = [pl.BlockSpec((None, 1, 128), lambda g, c: (g, 0, 0))] * 3
    return zx + conv + head, c_of


def _load_chunk_args(refs, has_prev):
    raw, rawp, cw, cb_, dtb, alog, dsk = refs
    return (raw[...], rawp[...] * has_prev, tuple(cw[pl.ds(i, 1), :] for i in range(4)), cb_[...],
            dtb[...], alog[...], dsk[...])


def _ssd_fwd(zx, conv_w, conv_b, dtb, alog, dsk):
    t = zx.shape[0]
    nc = t // CHUNK
    in_specs, _ = _ssd_in_specs(nc, False)

    def body(*refs):
        ins, (y_ref, hs_ref, ht) = refs[:7], refs[7:]
        c = pl.program_id(1)

        @pl.when(c == 0)
        def _():
            ht[...] = jnp.zeros_like(ht)

        a = _load_chunk_args(ins, _one(c > 0))
        h_in = ht[...]
        y, h_new = _ssd_chunk(*a[:2], h_in, *a[2:], _ssd_consts())
        y_ref[...] = y
        hs_ref[...] = h_in
        ht[...] = h_new

    return pl.pallas_call(
        body, name="ssd_fwd", grid=(N_GROUPS, nc),
        in_specs=in_specs,
        out_specs=[pl.BlockSpec((CHUNK, GROUP_X), lambda g, c: (c, g)),
                   pl.BlockSpec((None, None, D_STATE, GROUP_X), lambda g, c: (g, c, 0, 0))],
        out_shape=[jax.ShapeDtypeStruct((t, D_INNER), F32),
                   jax.ShapeDtypeStruct((N_GROUPS, nc, D_STATE, GROUP_X), F32)],
        scratch_shapes=[pltpu.VMEM((D_STATE, GROUP_X), F32)],
        compiler_params=_cp(("arbitrary", "arbitrary")),
    )(zx, zx, conv_w, conv_b, dtb, alog, dsk)


def _ssd_bwd(zx, conv_w, conv_b, dtb, alog, dsk, hs, dy, dzx):
    t = zx.shape[0]
    nc = t // CHUNK
    in_specs, c_of = _ssd_in_specs(nc, True)
    n_in = 10

    def body(*refs):
        ins, hs_ref, dy_ref = refs[:7], refs[7], refs[8]
        (draw_ref, dcw, dcb, ddtb, dalog, ddsk, dht, carry) = refs[n_in:]
        cc = pl.program_id(1)
        accs = (dcw, dcb, ddtb, dalog, ddsk)

        @pl.when(cc == 0)
        def _():
            for r in (dht, carry) + accs:
                r[...] = jnp.zeros_like(r)

        has_prev = _one(c_of(cc) > 0)
        a = _load_chunk_args(ins, has_prev)
        k = _ssd_consts()
        fn = lambda *args: _ssd_chunk(*args, k)
        _, vjp = jax.vjp(fn, *a[:2], hs_ref[...], *a[2:])
        graw, grawp, ght, gcw, gcb, gdtb, galog, gdsk = vjp((dy_ref[...], dht[...]))
        draw_ref[...] = (graw + carry[...]).astype(BF16)
        carry[...] = grawp * has_prev
        dht[...] = ght
        for i in range(4):
            dcw[pl.ds(i, 1), :] += gcw[i]
        for ref, val in ((dcb, gcb), (ddtb, gdtb), (dalog, galog), (ddsk, gdsk)):
            ref[...] += val

    head_out = pl.BlockSpec((None, 1, 128), lambda g, c: (g, 0, 0))
    sds = jax.ShapeDtypeStruct
    return pl.pallas_call(
        body, name="ssd_bwd", grid=(N_GROUPS, nc),
        in_specs=in_specs + [
            pl.BlockSpec((None, None, D_STATE, GROUP_X), lambda g, c: (g, c_of(c), 0, 0)),
            pl.BlockSpec((CHUNK, GROUP_X), lambda g, c: (c_of(c), g)),
            pl.BlockSpec(memory_space=pl.ANY)],
        out_specs=[pl.BlockSpec((CHUNK, GROUP_COLS), lambda g, c: (c_of(c), g)),
                   pl.BlockSpec((4, GROUP_CONV), lambda g, c: (0, g)),
                   pl.BlockSpec((1, GROUP_CONV), lambda g, c: (0, g)),
                   head_out, head_out, head_out],
        out_shape=[sds((t, ZX_COLS), BF16), sds((4, N_GROUPS * GROUP_CONV), F32), sds((1, N_GROUPS * GROUP_CONV), F32),
                   sds((N_GROUPS, 1, 128), F32), sds((N_GROUPS, 1, 128), F32), sds((N_GROUPS, 1, 128), F32)],
        scratch_shapes=[pltpu.VMEM((D_STATE, GROUP_X), F32), pltpu.VMEM((CHUNK, GROUP_COLS), F32)],
        input_output_aliases={9: 0},
        compiler_params=_cp(("arbitrary", "arbitrary")),
    )(zx, zx, conv_w, conv_b, dtb, alog, dsk, hs, dy, dzx)


def _gate_norm(y, zs, ng):
    outs = []
    for k in range(N_GROUPS):
        s = y[:, k * GROUP_X:(k + 1) * GROUP_X] * _silu(zs[k])
        outs.append(s * lax.rsqrt(jnp.mean(s * s, axis=-1, keepdims=True) + RMS_EPS))
    return jnp.concatenate(outs, axis=1) * ng


def _z_specs(tt, grid_rank):
    first = Z_OFF // GROUP_X
    if grid_rank == 1:
        return [pl.BlockSpec((tt, GROUP_X), functools.partial(lambda k, i: (i, first + k), k)) for k in range(N_GROUPS)]
    return [pl.BlockSpec((tt, GROUP_X), functools.partial(lambda k, i, j: (i, first + k), k)) for k in range(N_GROUPS)]


def _ssm_out_fwd(y, zx, ng, w_out, h):
    t = h.shape[0]
    tt = min(t, 256)

    def body(y_ref, z0, z1, z2, z3, ng_ref, w_ref, h_ref, o_ref):
        yn = _gate_norm(y_ref[...], (z0[...], z1[...], z2[...], z3[...]), ng_ref[...])
        o_ref[...] = h_ref[...] + jnp.dot(yn.astype(BF16), w_ref[...], preferred_element_type=F32)

    return pl.pallas_call(
        body, name="ssm_out_fwd", grid=(t // tt,),
        in_specs=[pl.BlockSpec((tt, D_INNER), lambda i: (i, 0))] + _z_specs(tt, 1) + [
            pl.BlockSpec((1, D_INNER), lambda i: (0, 0)),
            pl.BlockSpec((D_INNER, D_MODEL), lambda i: (0, 0)),
            pl.BlockSpec((tt, D_MODEL), lambda i: (i, 0))],
        out_specs=pl.BlockSpec((tt, D_MODEL), lambda i: (i, 0)),
        out_shape=jax.ShapeDtypeStruct((t, D_MODEL), F32),
        compiler_params=_cp(("arbitrary",)),
    )(y, zx, zx, zx, zx, ng, w_out, h)


def _ssm_out_bwd(dh, y, zx, ng, w_out):
    t = dh.shape[0]
    tt = min(t, 256)

    def body(dh_ref, y_ref, z0, z1, z2, z3, ng_ref, w_ref, dy_ref, dzx_ref, yn_ref, dhb_ref, dng_ref, dz_s):
        k = pl.program_id(1)

        @pl.when(jnp.logical_and(pl.program_id(0) == 0, k == 0))
        def _():
            dng_ref[...] = jnp.zeros_like(dng_ref)

        @pl.when(k == 0)
        def _():
            dhb = dh_ref[...].astype(BF16)
            dhb_ref[...] = dhb
            dyn = lax.dot_general(dhb, w_ref[...], _NT, preferred_element_type=F32)
            yn, vjp = jax.vjp(_gate_norm, y_ref[...], (z0[...], z1[...], z2[...], z3[...]), ng_ref[...])
            dy, dzs, dng = vjp(dyn)
            dy_ref[...] = dy
            for q in range(N_GROUPS):
                dz_s[q] = dzs[q].astype(BF16)
            yn_ref[...] = yn.astype(BF16)
            dng_ref[...] += dng

        dzx_ref[...] = dz_s[k]

    wide = pl.BlockSpec((tt, D_INNER), lambda i, k: (i, 0))
    narrow = pl.BlockSpec((tt, D_MODEL), lambda i, k: (i, 0))
    gain = pl.BlockSpec((1, D_INNER), lambda i, k: (0, 0))
    first = Z_OFF // GROUP_X
    return pl.pallas_call(
        body, name="ssm_out_bwd", grid=(t // tt, N_GROUPS),
        in_specs=[narrow, wide] + _z_specs(tt, 2) + [gain, pl.BlockSpec((D_INNER, D_MODEL), lambda i, k: (0, 0))],
        out_specs=[wide, pl.BlockSpec((tt, GROUP_X), lambda i, k: (i, first + k)), wide, narrow, gain],
        out_shape=[jax.ShapeDtypeStruct((t, D_INNER), F32), jax.ShapeDtypeStruct((t, ZX_COLS), BF16),
                   jax.ShapeDtypeStruct((t, D_INNER), BF16), jax.ShapeDtypeStruct((t, D_MODEL), BF16),
                   jax.ShapeDtypeStruct((1, D_INNER), F32)],
        scratch_shapes=[pltpu.VMEM((N_GROUPS, tt, GROUP_X), BF16)],
        compiler_params=_cp(("arbitrary", "arbitrary")),
    )(dh, y, zx, zx, zx, zx, ng, w_out)


def _final(h, g, tgt):
    t = h.shape[0]
    tt = min(t, 512)
    nt = t // tt

    def body(h_ref, g_ref, t_ref, dh_ref, loss_ref, dg_ref, lacc):
        i = pl.program_id(0)

        @pl.when(i == 0)
        def _():
            dg_ref[...] = jnp.zeros_like(dg_ref)
            lacc[...] = jnp.zeros_like(lacc)

        gv = g_ref[...]
        y, n, r = _rms_fwd(h_ref[...], gv)
        err = y - t_ref[...]
        lacc[...] += jnp.sum(err * err, axis=0, keepdims=True)
        dx, dg = _rms_bwd(err * (1.0 / D_MODEL), n, r, gv)
        dh_ref[...] = dx
        dg_ref[...] += dg

        @pl.when(i == nt - 1)
        def _():
            loss_ref[...] = jnp.zeros_like(loss_ref) + (0.5 / D_MODEL) * jnp.sum(lacc[...])

    tile = pl.BlockSpec((tt, D_MODEL), lambda i: (i, 0))
    vec = pl.BlockSpec((1, D_MODEL), lambda i: (0, 0))
    return pl.pallas_call(
        body, name="final_loss", grid=(nt,),
        in_specs=[tile, vec, tile],
        out_specs=[tile, pl.BlockSpec((1, 128), lambda i: (0, 0)), vec],
        out_shape=[jax.ShapeDtypeStruct((t, D_MODEL), F32), jax.ShapeDtypeStruct((1, 128), F32),
                   jax.ShapeDtypeStruct((1, D_MODEL), F32)],
        scratch_shapes=[pltpu.VMEM((1, D_MODEL), F32)],
        compiler_params=_cp(("arbitrary",)),
    )(h, g, tgt)


_ANY = pl.BlockSpec(memory_space=pl.ANY)


def _place():
    return lax.axis_index("x"), lax.axis_index("y"), lax.axis_index("c")


def _chips(x, y):
    return [(1 - x, y), (x, 1 - y), (1 - x, 1 - y)]


def _gather(shards, layout, out_shapes, name):
    n, m = len(shards), len(out_shapes)

    def body(*refs):
        ins, outs = refs[:n], refs[n:n + m]
        send_sems, recv_sems, local_sems = refs[n + m:]
        x, y, c = _place()
        me, sibling = (x, y, c), (x, y, 1 - c)
        chips = _chips(x, y)

        def win(i, place):
            j, off = layout[i]
            return outs[j].at[4 * place[0] + 2 * place[1] + place[2], pl.ds(off, shards[i].shape[0])]

        def copy(i, k, block, to, src=None):
            return pltpu.make_async_remote_copy(
                src_ref=win(i, block) if src is None else src, dst_ref=win(i, block),
                send_sem=send_sems.at[7 * i + k], recv_sem=recv_sems.at[7 * i + k], device_id=to, device_id_type=MESH)

        mine = [pltpu.make_async_copy(ins[i], win(i, me), local_sems.at[i]) for i in range(n)]
        first = [[copy(i, 0, me, sibling, src=ins[i])] +
                 [copy(i, 1 + j, me, (*chip, c), src=ins[i]) for j, chip in enumerate(chips)] for i in range(n)]
        passed = [[copy(i, 4 + j, (*chip, c), sibling) for j, chip in enumerate(chips)] for i in range(n)]
        for i in range(n):
            mine[i].start()
            for cp in first[i]:
                cp.start()
        for i in range(n):
            for j, chip in enumerate(chips):
                copy(i, 1 + j, (*chip, c), me).wait_recv()
                passed[i][j].start()
        for i in range(n):
            copy(i, 0, sibling, me).wait_recv()
            for j, chip in enumerate(chips):
                copy(i, 4 + j, (*chip, 1 - c), me).wait_recv()
        for i in range(n):
            for cp in first[i] + passed[i]:
                cp.wait_send()
            mine[i].wait()

    return pl.pallas_call(
        body, name=name,
        out_shape=[jax.ShapeDtypeStruct((N_DEV,) + tuple(s), d) for s, d in out_shapes],
        in_specs=[_ANY] * n, out_specs=[_ANY] * m,
        scratch_shapes=[pltpu.SemaphoreType.DMA((7 * n,)), pltpu.SemaphoreType.DMA((7 * n,)),
                        pltpu.SemaphoreType.DMA((n,))],
    )(*shards)


def _exchange_cores(parts, layout, out_shapes, name):
    n, m = len(parts), len(out_shapes)

    def body(*refs):
        ins, outs = refs[:n], refs[n:n + m]
        send_sems, recv_sems = refs[n + m:]
        x, y, c = _place()
        sibling = (x, y, 1 - c)

        def copy(i, q):
            j, off = layout[i]
            return pltpu.make_async_remote_copy(
                src_ref=ins[i].at[2 * q + (1 - c)], dst_ref=outs[j].at[q, pl.ds(off, parts[i].shape[1])],
                send_sem=send_sems.at[4 * i + q], recv_sem=recv_sems.at[4 * i + q],
                device_id=sibling, device_id_type=MESH)

        cps = [copy(i, q) for i in range(n) for q in range(4)]
        for cp in cps:
            cp.start()
        for cp in cps:
            cp.wait_recv()
        for cp in cps:
            cp.wait_send()

    return pl.pallas_call(
        body, name=name,
        out_shape=[jax.ShapeDtypeStruct((4,) + tuple(s), d) for s, d in out_shapes],
        in_specs=[_ANY] * n, out_specs=[_ANY] * m,
        scratch_shapes=[pltpu.SemaphoreType.DMA((4 * n,)), pltpu.SemaphoreType.DMA((4 * n,))],
    )(*parts)


def _row_block(rows, cols, itemsize):
    rb = rows
    while rb * cols * itemsize > (1 << 20) and rb % 2 == 0 and (rb // 2) % 16 == 0:
        rb //= 2
    return rb


def _add_core_pair(cidx, part, land, off, name):
    _, rows, cols = part.shape
    rb = _row_block(rows, cols, part.dtype.itemsize)
    part4 = part.reshape(4, 2, rows, cols)

    def body(c_ref, a_ref, b_ref, o_ref):
        o_ref[...] = (a_ref[...].astype(F32) + b_ref[...].astype(F32)).astype(o_ref.dtype)

    return pl.pallas_call(
        body, name=name,
        grid_spec=pltpu.PrefetchScalarGridSpec(
            num_scalar_prefetch=1, grid=(4, rows // rb),
            in_specs=[pl.BlockSpec((None, None, rb, cols), lambda q, i, cref: (q, cref[0], i, 0)),
                      pl.BlockSpec((None, rb, cols), lambda q, i, cref: (q, off // rb + i, 0))],
            out_specs=pl.BlockSpec((None, rb, cols), lambda q, i, cref: (q, i, 0))),
        out_shape=jax.ShapeDtypeStruct((4, rows, cols), part.dtype),
        compiler_params=_cp(("arbitrary", "arbitrary")),
    )(cidx, part4, land)


def _exchange_chips(parts, layout, out_shapes, name):
    n, m = len(parts), len(out_shapes)

    def body(*refs):
        ins, outs = refs[:n], refs[n:n + m]
        send_sems, recv_sems, local_sems = refs[n + m:]
        x, y, c = _place()
        qme = 2 * x + y
        chips = _chips(x, y)

        def copy(i, k, chip, slot):
            j, off = layout[i]
            return pltpu.make_async_remote_copy(
                src_ref=ins[i].at[2 * chip[0] + chip[1]], dst_ref=outs[j].at[slot, pl.ds(off, parts[i].shape[1])],
                send_sem=send_sems.at[3 * i + k], recv_sem=recv_sems.at[3 * i + k],
                device_id=(*chip, c), device_id_type=MESH)

        def local(i):
            j, off = layout[i]
            return pltpu.make_async_copy(ins[i].at[qme], outs[j].at[qme, pl.ds(off, parts[i].shape[1])],
                                         local_sems.at[i])

        sends = [copy(i, k, chip, qme) for i in range(n) for k, chip in enumerate(chips)]
        for i in range(n):
            local(i).start()
        for cp in sends:
            cp.start()
        for i in range(n):
            for k, chip in enumerate(chips):
                copy(i, k, chip, 2 * chip[0] + chip[1]).wait_recv()
        for cp in sends:
            cp.wait_send()
        for i in range(n):
            local(i).wait()

    return pl.pallas_call(
        body, name=name,
        out_shape=[jax.ShapeDtypeStruct((4,) + tuple(s), d) for s, d in out_shapes],
        in_specs=[_ANY] * n, out_specs=[_ANY] * m,
        scratch_shapes=[pltpu.SemaphoreType.DMA((3 * n,)), pltpu.SemaphoreType.DMA((3 * n,)),
                        pltpu.SemaphoreType.DMA((n,))],
    )(*parts)


def _all_reduce_small(sp):
    rows, n = sp.shape

    def body(x_ref, o_ref, land, send_sems, recv_sems):
        x, y, c = _place()
        me = 4 * x + 2 * y + c
        land[me] = x_ref[...]
        cps = []
        for rel in range(1, N_DEV):
            dx, dy, dc = (rel >> 2) & 1, (rel >> 1) & 1, rel & 1
            px = x + dx - 2 * x * dx
            py = y + dy - 2 * y * dy
            pc = c + dc - 2 * c * dc
            peer = 4 * px + 2 * py + pc
            cps.append((pltpu.make_async_remote_copy(
                src_ref=x_ref, dst_ref=land.at[me], send_sem=send_sems.at[rel - 1], recv_sem=recv_sems.at[rel - 1],
                device_id=(px, py, pc), device_id_type=MESH),
                pltpu.make_async_remote_copy(
                src_ref=x_ref, dst_ref=land.at[peer], send_sem=send_sems.at[rel - 1], recv_sem=recv_sems.at[rel - 1],
                device_id=(px, py, pc), device_id_type=MESH)))
        for cp, _ in cps:
            cp.start()
        for _, arr in cps:
            arr.wait_recv()
        for cp, _ in cps:
            cp.wait_send()
        acc = land[0]
        for k in range(1, N_DEV):
            acc = acc + land[k]
        o_ref[...] = acc

    vm = pl.BlockSpec(memory_space=pltpu.VMEM)
    return pl.pallas_call(
        body, name="all_reduce_small",
        out_shape=jax.ShapeDtypeStruct((rows, n), F32),
        in_specs=[vm], out_specs=vm,
        scratch_shapes=[pltpu.VMEM((N_DEV, rows, n), F32),
                        pltpu.SemaphoreType.DMA((N_DEV - 1,)), pltpu.SemaphoreType.DMA((N_DEV - 1,))],
    )(sp)


def _adamw_math(wv, gv, mv, vv):
    m2 = ADAM_B1 * mv + (1.0 - ADAM_B1) * gv
    v2 = ADAM_B2 * vv + (1.0 - ADAM_B2) * (gv * gv)
    m_hat = m2 / (1.0 - ADAM_B1 ** ADAM_STEP)
    v_hat = v2 / (1.0 - ADAM_B2 ** ADAM_STEP)
    return -ADAM_LR * (m_hat / (jnp.sqrt(v_hat) + ADAM_EPS) + ADAM_WD * wv), m2, v2


def _adamw(w, g, m, v, name):
    rows, cols = w.shape
    br = rows if rows <= 256 else 256

    def body(w_ref, g_ref, m_ref, v_ref, d_ref, m2_ref, v2_ref):
        d_ref[...], m2_ref[...], v2_ref[...] = _adamw_math(w_ref[...], g_ref[...], m_ref[...], v_ref[...])

    spec = pl.BlockSpec((br, cols), lambda i: (i, 0))
    out = jax.ShapeDtypeStruct((rows, cols), F32)
    return pl.pallas_call(
        body, name=name, grid=(rows // br,),
        in_specs=[spec] * 4, out_specs=[spec] * 3, out_shape=[out] * 3,
        compiler_params=_cp(("arbitrary",)),
    )(w, g, m, v)


def _adamw_reduced(w, land, m, v, name):
    rows, cols = w.shape
    br = rows if rows <= 256 else 256

    def body(w_ref, l_ref, m_ref, v_ref, g_ref, d_ref, m2_ref, v2_ref):
        gv = l_ref[0].astype(F32)
        for q in range(1, 4):
            gv = gv + l_ref[q].astype(F32)
        g_ref[...] = gv
        d_ref[...], m2_ref[...], v2_ref[...] = _adamw_math(w_ref[...], gv, m_ref[...], v_ref[...])

    spec = pl.BlockSpec((br, cols), lambda i: (i, 0))
    out = jax.ShapeDtypeStruct((rows, cols), F32)
    return pl.pallas_call(
        body, name=name, grid=(rows // br,),
        in_specs=[spec, pl.BlockSpec((4, br, cols), lambda i: (0, i, 0)), spec, spec],
        out_specs=[spec] * 4, out_shape=[out] * 4,
        compiler_params=_cp(("arbitrary",)),
    )(w, land, m, v)


def _zx_source_col(col):
    blk = jnp.right_shift(col, 7)
    lane = jnp.bitwise_and(col, 127)
    per = GROUP_COLS // 128
    grp = jnp.where(blk >= per, 1, 0) + jnp.where(blk >= 2 * per, 1, 0) + jnp.where(blk >= 3 * per, 1, 0)
    o = blk - per * grp
    x_col = D_INNER + GROUP_X * grp + 128 * o + lane
    b_col = 2 * D_INNER + D_STATE * grp + lane
    c_col = 2 * D_INNER + N_GROUPS * D_STATE + D_STATE * grp + lane
    dt_col = jnp.where(lane < HEADS_PER_GROUP, D_INNER + CONV_DIM + HEADS_PER_GROUP * grp + lane, -1)
    src = jnp.where(o < 4, x_col, jnp.where(o == 4, b_col, jnp.where(o == 5, c_col, dt_col)))
    return jnp.where(col >= Z_OFF, col - Z_OFF, src)


def _zx_source_col_py(col):
    if col >= Z_OFF:
        return col - Z_OFF
    grp, o = divmod(col, GROUP_COLS)
    if o < GROUP_X:
        return D_INNER + GROUP_X * grp + o
    if o < GROUP_X + D_STATE:
        return 2 * D_INNER + D_STATE * grp + (o - GROUP_X)
    if o < GROUP_CONV:
        return 2 * D_INNER + N_GROUPS * D_STATE + D_STATE * grp + (o - GROUP_X - D_STATE)
    h = o - GROUP_CONV
    return D_INNER + CONV_DIM + HEADS_PER_GROUP * grp + h if h < HEADS_PER_GROUP else -1


def _overlap_tables():
    nblk = ZX_COLS // COL_BLK
    src = [_zx_source_col_py(c) for c in range(ZX_COLS)]
    fwd = [sorted({s // W_IN_SHARD for s in src[COL_BLK * j:COL_BLK * (j + 1)] if s >= 0}) for j in range(nblk)]
    dst = {s: c for c, s in enumerate(src) if s >= 0}
    bwd = [sorted({dst[s] // COL_BLK for s in range(W_IN_SHARD * k, W_IN_SHARD * (k + 1))}) for k in range(N_DEV)]

    def flat(rows):
        width = max(len(r) for r in rows)
        idx = [r + [r[-1]] * (width - len(r)) for r in rows]
        val = [[1] * len(r) + [0] * (width - len(r)) for r in rows]
        return (jnp.asarray(sum(idx, []), jnp.int32), jnp.asarray(sum(val, []), jnp.int32), width)

    return flat(fwd), flat(bwd)


def _w_in_to_zx(w_in_g):
    (tab, val, width), _ = _overlap_tables()
    nblk = ZX_COLS // COL_BLK

    def body(tab_ref, val_ref, w_ref, o_ref, acc):
        j = pl.program_id(0)
        s = pl.program_id(1)

        @pl.when(s == 0)
        def _():
            acc[...] = jnp.zeros_like(acc)

        @pl.when(val_ref[j * width + s] == 1)
        def _():
            k = tab_ref[j * width + s]
            col = COL_BLK * j + lax.broadcasted_iota(jnp.int32, (W_IN_SHARD, COL_BLK), 1)
            row = W_IN_SHARD * k + lax.broadcasted_iota(jnp.int32, (W_IN_SHARD, COL_BLK), 0)
            place = _one(_zx_source_col(col) == row).astype(BF16)
            acc[...] += jnp.dot(w_ref[...], place, preferred_element_type=F32)

        @pl.when(s == width - 1)
        def _():
            o_ref[...] = acc[...].astype(BF16)

    return pl.pallas_call(
        body, name="w_in_to_zx",
        grid_spec=pltpu.PrefetchScalarGridSpec(
            num_scalar_prefetch=2, grid=(nblk, width),
            in_specs=[pl.BlockSpec((None, D_MODEL, W_IN_SHARD), lambda j, s, tab, val: (tab[j * width + s], 0, 0))],
            out_specs=pl.BlockSpec((D_MODEL, COL_BLK), lambda j, s, tab, val: (0, j)),
            scratch_shapes=[pltpu.VMEM((D_MODEL, COL_BLK), F32)]),
        out_shape=jax.ShapeDtypeStruct((D_MODEL, ZX_COLS), BF16),
        compiler_params=_cp(("arbitrary", "arbitrary")),
    )(tab, val, w_in_g)


def _zx_to_w_in(d_wzx):
    _, (tab, val, width) = _overlap_tables()

    def body(tab_ref, val_ref, d_ref, o_ref, acc):
        k = pl.program_id(0)
        s = pl.program_id(1)

        @pl.when(s == 0)
        def _():
            acc[...] = jnp.zeros_like(acc)

        @pl.when(val_ref[k * width + s] == 1)
        def _():
            j = tab_ref[k * width + s]
            col = COL_BLK * j + lax.broadcasted_iota(jnp.int32, (COL_BLK, W_IN_SHARD), 0)
            row = W_IN_SHARD * k + lax.broadcasted_iota(jnp.int32, (COL_BLK, W_IN_SHARD), 1)
            place = _one(_zx_source_col(col) == row).astype(BF16)
            acc[...] += jnp.dot(d_ref[...], place, preferred_element_type=F32)

        @pl.when(s == width - 1)
        def _():
            o_ref[...] = acc[...].astype(BF16)

    return pl.pallas_call(
        body, name="zx_to_w_in",
        grid_spec=pltpu.PrefetchScalarGridSpec(
            num_scalar_prefetch=2, grid=(N_DEV, width),
            in_specs=[pl.BlockSpec((D_MODEL, COL_BLK), lambda k, s, tab, val: (0, tab[k * width + s]))],
            out_specs=pl.BlockSpec((None, D_MODEL, W_IN_SHARD), lambda k, s, tab, val: (k, 0, 0)),
            scratch_shapes=[pltpu.VMEM((D_MODEL, W_IN_SHARD), F32)]),
        out_shape=jax.ShapeDtypeStruct((N_DEV, D_MODEL, W_IN_SHARD), BF16),
        compiler_params=_cp(("arbitrary", "arbitrary")),
    )(tab, val, d_wzx)


def _group_conv_cols(a):
    rows = a.shape[0]
    x = a[:, :D_INNER].reshape(rows, N_GROUPS, GROUP_X)
    b = a[:, D_INNER:D_INNER + N_GROUPS * D_STATE].reshape(rows, N_GROUPS, D_STATE)
    c = a[:, D_INNER + N_GROUPS * D_STATE:].reshape(rows, N_GROUPS, D_STATE)
    return jnp.concatenate([x, b, c], axis=2).reshape(rows, N_GROUPS * GROUP_CONV)


def _ungroup_conv_cols(a):
    rows = a.shape[0]
    a3 = a.reshape(rows, N_GROUPS, GROUP_CONV)
    return jnp.concatenate([a3[:, :, :GROUP_X].reshape(rows, D_INNER),
                            a3[:, :, GROUP_X:GROUP_X + D_STATE].reshape(rows, N_GROUPS * D_STATE),
                            a3[:, :, GROUP_X + D_STATE:].reshape(rows, N_GROUPS * D_STATE)], axis=1)


def _small_shard(conv_w, conv_b, norm_g):
    ng = jnp.pad(norm_g.reshape(1, -1), ((0, 0), (0, CONV_SHARD - norm_g.shape[-1])))
    return jnp.concatenate([conv_w.reshape(4, CONV_SHARD), conv_b.reshape(1, CONV_SHARD), ng,
                            jnp.zeros((SMALL_ROWS - 6, CONV_SHARD), F32)], axis=0)


def _small_unshard(a):
    return a[0:4].reshape(1, 4, CONV_SHARD), a[4:5], a[5:6, :D_INNER // N_DEV]


def _heads_of(a):
    return a[:, :, :HEADS_PER_GROUP].reshape(1, N_HEADS)


def _head_params(p):
    return jnp.pad(p.reshape(N_GROUPS, 1, HEADS_PER_GROUP), ((0, 0), (0, 0), (0, 128 - HEADS_PER_GROUP)))


def _update(w, land, m, v, name):
    shp = w.shape
    to2 = lambda a: a.reshape(-1, shp[-1])
    return tuple(o.reshape(shp) for o in _adamw_reduced(to2(w), land, to2(m), to2(v), name))


def kernel(x, norm_mix_g, norm_mlp_g, pool_w, pool_b, pool_scale, ssm_w_in, ssm_conv_w, ssm_conv_b, ssm_dt_bias, ssm_a_log, ssm_d, ssm_norm_g, ssm_w_out, mlp_w1, mlp_w2, final_g, loss_target, m_norm_mix_g, m_norm_mlp_g, m_pool_w, m_pool_b, m_pool_scale, m_ssm_w_in, m_ssm_conv_w, m_ssm_conv_b, m_ssm_dt_bias, m_ssm_a_log, m_ssm_d, m_ssm_norm_g, m_ssm_w_out, m_mlp_w1, m_mlp_w2, m_final_g, v_norm_mix_g, v_norm_mlp_g, v_pool_w, v_pool_b, v_pool_scale, v_ssm_w_in, v_ssm_conv_w, v_ssm_conv_b, v_ssm_dt_bias, v_ssm_a_log, v_ssm_d, v_ssm_norm_g, v_ssm_w_out, v_mlp_w1, v_mlp_w2, v_final_g):
    x2 = x[0]
    tgt = loss_target[0]
    gm0, gm1 = norm_mix_g[0:1], norm_mix_g[1:2]
    gl0, gl1 = norm_mlp_g[0:1], norm_mlp_g[1:2]
    gfin = final_g.reshape(1, D_MODEL)

    fb = D_FF // N_DEV
    shards = [pool_w.reshape(4 * POOL_SHARD, POOL_GROUP).astype(BF16),
              mlp_w1[0].astype(BF16), mlp_w2[0].astype(BF16), mlp_w1[1].astype(BF16), mlp_w2[1].astype(BF16),
              ssm_w_in[0].astype(BF16), ssm_w_out[0].astype(BF16),
              _small_shard(ssm_conv_w, ssm_conv_b, ssm_norm_g)]
    layout = [(0, 0), (1, 0), (2, 0), (1, D_MODEL), (2, fb), (3, 0), (4, 0), (5, 0)]
    shapes = [((4 * POOL_SHARD, POOL_GROUP), BF16), ((2 * D_MODEL, fb), BF16), ((2 * fb, D_MODEL), BF16),
              ((D_MODEL, W_IN_SHARD), BF16), ((D_INNER // N_DEV, D_MODEL), BF16), ((SMALL_ROWS, CONV_SHARD), F32)]
    w_pool, w1g, w2g, w_in_g, w_out_g, small_g = _gather(shards, layout, shapes, "gather_weights")
    w1g = w1g.reshape(N_DEV, 2, D_MODEL, fb)
    w2g = w2g.reshape(N_DEV, 2, fb, D_MODEL)
    w_out = w_out_g.reshape(D_INNER, D_MODEL)
    w_zx = _w_in_to_zx(w_in_g)
    conv_w = _group_conv_cols(small_g[:, 0:4].transpose(1, 0, 2).reshape(4, CONV_DIM))
    conv_b = _group_conv_cols(small_g[:, 4].reshape(1, CONV_DIM))
    ssm_ng = small_g[:, 5, :D_INNER // N_DEV].reshape(1, D_INNER)
    dtb, alog, dsk = _head_params(ssm_dt_bias), _head_params(ssm_a_log), _head_params(ssm_d)

    h1 = _pool_fwd(x2, gm0, w_pool, pool_b, pool_scale)
    h2, u0, hm0 = _mlp_fwd(h1, gl0, w1g, w2g, 0, "mlp0_fwd")
    zx, hn1 = _norm_matmul(h2, gm1, w_zx)
    y_ssd, states = _ssd_fwd(zx, conv_w, conv_b, dtb, alog, dsk)
    h3 = _ssm_out_fwd(y_ssd, zx, ssm_ng, w_out, h2)
    h4, u1, hm1 = _mlp_fwd(h3, gl1, w1g, w2g, 1, "mlp1_fwd")
    dh4, loss_row, d_gfin = _final(h4, gfin, tgt)

    dh3, da1, dh4b, d_gl1 = _mlp_bwd(dh4, h3, gl1, u1, w1g, w2g, 1, "mlp1_bwd")
    d_w1_1 = _matmul_tn(hm1, da1, "mlp1_dw1", col_blocked=True)
    d_w2_1 = _matmul_tn(u1, dh4b, "mlp1_dw2", square_a=True).reshape(N_DEV, fb, D_MODEL)
    dy_ssd, dzx, yn, dh3b, d_ng = _ssm_out_bwd(dh3, y_ssd, zx, ssm_ng, w_out)
    d_wout = _matmul_tn(yn, dh3b, "ssm_dw_out").reshape(N_DEV, D_INNER // N_DEV, D_MODEL)
    dzx, d_cw, d_cb, d_dtb, d_alog, d_dsk = _ssd_bwd(zx, conv_w, conv_b, dtb, alog, dsk, states, dy_ssd, dzx)
    dh2, d_gm1 = _in_proj_bwd(dzx, w_zx, h2, gm1, dh3)
    d_w_in = _zx_to_w_in(_matmul_tn(hn1, dzx, "ssm_dw_in"))
    dh1, da0, dh2b, d_gl0 = _mlp_bwd(dh2, h1, gl0, u0, w1g, w2g, 0, "mlp0_bwd")
    d_w1_0 = _matmul_tn(hm0, da0, "mlp0_dw1", col_blocked=True)
    d_w2_0 = _matmul_tn(u0, dh2b, "mlp0_dw2", square_a=True).reshape(N_DEV, fb, D_MODEL)
    dx, d_pool, d_pb, d_ps, d_gm0 = _pool_bwd(x2, dh1, gm0, w_pool, pool_b, pool_scale)

    d_conv_w = _ungroup_conv_cols(d_cw).reshape(4, N_DEV, CONV_SHARD).transpose(1, 0, 2)
    d_conv_b = _ungroup_conv_cols(d_cb).reshape(N_DEV, 1, CONV_SHARD)
    d_gain = jnp.pad(d_ng.reshape(N_DEV, 1, D_INNER // N_DEV), ((0, 0), (0, 0), (0, CONV_SHARD - D_INNER // N_DEV)))
    d_small = jnp.concatenate([d_conv_w, d_conv_b, d_gain,
                               jnp.zeros((N_DEV, SMALL_ROWS - 6, CONV_SHARD), F32)], axis=1)
    parts = [d_pool.astype(BF16), d_w1_0, d_w2_0, d_w1_1, d_w2_1, d_w_in, d_wout, d_small]
    land_shapes = [s for s in shapes]
    cidx = lax.axis_index("c").astype(jnp.int32).reshape(1)
    land1 = _exchange_cores(parts, layout, land_shapes, "reduce_exchange_cores")
    names = ["pool", "w1_0", "w2_0", "w1_1", "w2_1", "w_in", "w_out", "small"]
    chip_parts = [_add_core_pair(cidx, p, land1[j], off, "reduce_add_pair_" + nm)
                  for p, (j, off), nm in zip(parts, layout, names)]
    land2 = _exchange_chips(chip_parts, layout, land_shapes, "reduce_exchange_chips")

    heads = jnp.concatenate([_heads_of(a) for a in (d_dtb, d_alog, d_dsk)], axis=1)
    sp = jnp.concatenate([d_gm0, d_gm1, d_gl0, d_gl1, d_pb, d_ps, d_gfin,
                          jnp.pad(heads, ((0, 0), (0, D_MODEL - 3 * N_HEADS)))], axis=0)
    sg = _all_reduce_small(sp)

    g_norm_mix = sg[0:2]
    g_norm_mlp = sg[2:4]
    g_pool_b, g_pool_scale = sg[4:5], sg[5:6]
    g_final = sg[6]
    g_dtb, g_alog, g_dsk = sg[7:8, 0:32], sg[7:8, 32:64], sg[7:8, 64:96]

    def rep_pack(nm, nl, pb, ps, fg, db, al, dk):
        hd = jnp.pad(jnp.concatenate([db, al, dk], axis=1), ((0, 0), (0, D_MODEL - 3 * N_HEADS)))
        return jnp.concatenate([nm, nl, pb, ps, fg.reshape(1, D_MODEL), hd], axis=0)

    rep = [rep_pack(*t) for t in (
        (norm_mix_g, norm_mlp_g, pool_b, pool_scale, final_g, ssm_dt_bias, ssm_a_log, ssm_d),
        (g_norm_mix, g_norm_mlp, g_pool_b, g_pool_scale, g_final, g_dtb, g_alog, g_dsk),
        (m_norm_mix_g, m_norm_mlp_g, m_pool_b, m_pool_scale, m_final_g, m_ssm_dt_bias, m_ssm_a_log, m_ssm_d),
        (v_norm_mix_g, v_norm_mlp_g, v_pool_b, v_pool_scale, v_final_g, v_ssm_dt_bias, v_ssm_a_log, v_ssm_d))]
    rep_out = _adamw(*rep, "adamw_replicated")

    def rep_unpack(a):
        return (a[0:2], a[2:4], a[4:5], a[5:6], a[6], a[7:8, 0:32], a[7:8, 32:64], a[7:8, 64:96])

    sm_out = _adamw_reduced(_small_shard(ssm_conv_w, ssm_conv_b, ssm_norm_g), land2[5],
                            _small_shard(m_ssm_conv_w, m_ssm_conv_b, m_ssm_norm_g),
                            _small_shard(v_ssm_conv_w, v_ssm_conv_b, v_ssm_norm_g), "adamw_small_shards")

    big = {
        "pool_w": _update(pool_w, land2[0], m_pool_w, v_pool_w, "adamw_pool_w"),
        "ssm_w_in": _update(ssm_w_in, land2[3], m_ssm_w_in, v_ssm_w_in, "adamw_w_in"),
        "ssm_w_out": _update(ssm_w_out, land2[4], m_ssm_w_out, v_ssm_w_out, "adamw_w_out"),
        "mlp_w1": _update(mlp_w1, land2[1], m_mlp_w1, v_mlp_w1, "adamw_w1"),
        "mlp_w2": _update(mlp_w2, land2[2], m_mlp_w2, v_mlp_w2, "adamw_w2"),
    }
    rep_all = (rep[1],) + tuple(rep_out)

    def ordered(kind):
        nm, nl, pb, ps, fg, db, al, dk = rep_unpack(rep_all[kind])
        cw, cb, ng = _small_unshard(sm_out[kind])
        return [nm, nl, big["pool_w"][kind], pb, ps, big["ssm_w_in"][kind], cw, cb, db, al, dk, ng,
                big["ssm_w_out"][kind], big["mlp_w1"][kind], big["mlp_w2"][kind], fg]

    loss = lax.psum(loss_row[0, 0], ("x", "y", "c"))
    return (loss, dx[None], *ordered(0), *ordered(1), *ordered(2), *ordered(3))
```

```python
import functools

import jax
import jax.numpy as jnp
from jax import lax
from jax.experimental import pallas as pl
from jax.experimental.pallas import tpu as pltpu

F32 = jnp.float32
BF16 = jnp.bfloat16
MESH = pl.DeviceIdType.MESH

D_MODEL = 1024
RMS_EPS = 1e-5
POOL_WINDOWS = (2, 4, 8, 16)
POOL_GROUP = 256
POOL_HALO = 16
POOL_SHARD = POOL_GROUP // 8
D_INNER = 2048
HEAD_DIM = 64
N_HEADS = 32
N_GROUPS = 4
HEADS_PER_GROUP = 8
D_STATE = 128
CHUNK = 128
CONV_DIM = 3072
IN_PROJ_DIM = 5152
D_FF = 4096
N_DEV = 8
GROUP_X = HEADS_PER_GROUP * HEAD_DIM
GROUP_CONV = GROUP_X + 2 * D_STATE
GROUP_COLS = GROUP_CONV + 128
Z_OFF = N_GROUPS * GROUP_COLS
ZX_COLS = Z_OFF + D_INNER
COL_BLK = 512
W_IN_SHARD = IN_PROJ_DIM // N_DEV

ADAM_LR = 0.001
ADAM_B1 = 0.9
ADAM_B2 = 0.999
ADAM_EPS = 1e-08
ADAM_WD = 0.01
ADAM_STEP = 10

VMEM_LIMIT_V7X = 48 * 1024 * 1024

CONV_SHARD = CONV_DIM // N_DEV
SMALL_ROWS = 8

_NN = (((1,), (0,)), ((), ()))
_NT = (((1,), (1,)), ((), ()))
_TN = (((0,), (0,)), ((), ()))


def _cp(sem):
    return pltpu.CompilerParams(dimension_semantics=sem, vmem_limit_bytes=VMEM_LIMIT_V7X)


_ANY = pl.BlockSpec(memory_space=pl.ANY)


def _place():
    return lax.axis_index("x"), lax.axis_index("y"), lax.axis_index("c")


class _Carried:
    def __init__(self, ins, outs, sems, start, finish):
        self.ins, self.outs, self.sems, self.start, self.finish = list(ins), list(outs), list(sems), start, finish


def _pcall(body, *, name, grid, in_specs, out_specs, out_shape, sem, args, scratch_shapes=(), carried=None,
           aliases=None):
    in_specs, out_specs, out_shape, scratch = list(in_specs), list(out_specs), list(out_shape), list(scratch_shapes)
    common = dict(name=name, grid=grid, input_output_aliases=aliases or {}, compiler_params=_cp(sem))
    if carried is None:
        res = pl.pallas_call(body, in_specs=in_specs, out_specs=out_specs, out_shape=out_shape,
                             scratch_shapes=scratch, **common)(*args)
        return list(res), []
    n_in, n_out, n_scr = len(in_specs), len(out_specs), len(scratch)
    ci, co = len(carried.ins), len(carried.outs)

    def wrapped(*refs):
        ins, cins = refs[:n_in], refs[n_in:n_in + ci]
        p = n_in + ci
        outs, couts = refs[p:p + n_out], refs[p + n_out:p + n_out + co]
        p += n_out + co
        scr, csems = refs[p:p + n_scr], refs[p + n_scr:]
        ids = [pl.program_id(a) for a in range(len(grid))]
        first = functools.reduce(jnp.logical_and, [i == 0 for i in ids])
        last = functools.reduce(jnp.logical_and, [i == g - 1 for i, g in zip(ids, grid)])

        @pl.when(first)
        def _():
            carried.start(cins, couts, csems)

        body(*ins, *outs, *scr)

        @pl.when(last)
        def _():
            carried.finish(cins, couts, csems)

    res = pl.pallas_call(wrapped, in_specs=in_specs + [_ANY] * ci, out_specs=out_specs + [_ANY] * co,
                         out_shape=out_shape + carried.outs, scratch_shapes=scratch + carried.sems,
                         **common)(*args, *carried.ins)
    return list(res[:n_out]), list(res[n_out:])


def _peers(x, y, c):
    out = []
    for rel in range(1, N_DEV):
        dx, dy, dc = (rel >> 2) & 1, (rel >> 1) & 1, rel & 1
        out.append((x + dx - 2 * x * dx, y + dy - 2 * y * dy, c + dc - 2 * c * dc))
    return out


def _direct_exchange(srcs, layout, out_shapes, scatter):
    n = len(srcs)

    def copies(ins, outs, sems):
        send, recv, loc = sems
        x, y, c = _place()
        me = 4 * x + 2 * y + c
        out, arrive, local = [], [], []
        for i in range(n):
            j, off = layout[i]
            rows = srcs[i].shape[-2]
            for r, peer in enumerate(_peers(x, y, c)):
                pidx = 4 * peer[0] + 2 * peer[1] + peer[2]
                src = ins[i].at[pidx] if scatter else ins[i]
                kw = dict(send_sem=send.at[7 * i + r], recv_sem=recv.at[7 * i + r], device_id=peer, device_id_type=MESH)
                out.append(pltpu.make_async_remote_copy(src_ref=src, dst_ref=outs[j].at[me, pl.ds(off, rows)], **kw))
                arrive.append(pltpu.make_async_remote_copy(src_ref=src, dst_ref=outs[j].at[pidx, pl.ds(off, rows)], **kw))
            own = ins[i].at[me] if scatter else ins[i]
            local.append(pltpu.make_async_copy(own, outs[j].at[me, pl.ds(off, rows)], loc.at[i]))
        return out, arrive, local

    def start(ins, outs, sems):
        out, _, local = copies(ins, outs, sems)
        for cp in local + out:
            cp.start()

    def finish(ins, outs, sems):
        out, arrive, local = copies(ins, outs, sems)
        for cp in arrive:
            cp.wait_recv()
        for cp in out:
            cp.wait_send()
        for cp in local:
            cp.wait()

    return _Carried(srcs, [jax.ShapeDtypeStruct((N_DEV,) + tuple(s), d) for s, d in out_shapes],
                    [pltpu.SemaphoreType.DMA((7 * n,)), pltpu.SemaphoreType.DMA((7 * n,)),
                     pltpu.SemaphoreType.DMA((n,))], start, finish)


def _run_exchange(carried, name):
    ci = len(carried.ins)

    def body(*refs):
        ins, outs, sems = refs[:ci], refs[ci:ci + len(carried.outs)], refs[ci + len(carried.outs):]
        carried.start(ins, outs, sems)
        carried.finish(ins, outs, sems)

    return list(pl.pallas_call(body, name=name, in_specs=[_ANY] * ci, out_specs=[_ANY] * len(carried.outs),
                               out_shape=carried.outs, scratch_shapes=carried.sems)(*carried.ins))


def _dg(a, b, dn):
    return lax.dot_general(a.astype(BF16), b.astype(BF16), dn, preferred_element_type=F32)


@jax.custom_vjp
def mm_nn(a, b):
    return _dg(a, b, _NN)


@jax.custom_vjp
def mm_nt(a, b):
    return _dg(a, b, _NT)


@jax.custom_vjp
def mm_tn(a, b):
    return _dg(a, b, _TN)


mm_nn.defvjp(lambda a, b: (_dg(a, b, _NN), (a, b)), lambda r, ct: (mm_nt(ct, r[1]), mm_tn(r[0], ct)))
mm_nt.defvjp(lambda a, b: (_dg(a, b, _NT), (a, b)), lambda r, ct: (mm_nn(ct, r[1]), mm_tn(ct, r[0])))
mm_tn.defvjp(lambda a, b: (_dg(a, b, _TN), (a, b)), lambda r, ct: (mm_nt(r[1], ct), mm_nn(r[0], ct)))


def _split3(x):
    p1 = x.astype(BF16)
    r1 = x - p1.astype(F32)
    p2 = r1.astype(BF16)
    r2 = r1 - p2.astype(F32)
    return p1, p2, r2.astype(BF16)


def _exact01(x, c, dn, const_left):
    acc = None
    for p in reversed(_split3(x)):
        t = (lax.dot_general(c, p, dn, preferred_element_type=F32) if const_left
             else lax.dot_general(p, c, dn, preferred_element_type=F32))
        acc = t if acc is None else acc + t
    return acc


def _make_cmm(dn, const_left, bwd_name):
    @jax.custom_vjp
    def f(x, c):
        return _exact01(x, c, dn, const_left)

    def fwd(x, c):
        return _exact01(x, c, dn, const_left), c

    def bwd(c, ct):
        return _CMM[bwd_name](ct, c), jnp.zeros_like(c)

    f.defvjp(fwd, bwd)
    return f


_CMM = {}
_CMM["xc"] = _make_cmm(_NN, False, "xct")
_CMM["xct"] = _make_cmm(_NT, False, "xc")
_CMM["cx"] = _make_cmm(_NN, True, "ctx")
_CMM["ctx"] = _make_cmm(_TN, True, "cx")


@jax.custom_vjp
def _silu(x):
    return x / (1.0 + jnp.exp(-x))


def _silu_fwd(x):
    return _silu(x), x


def _silu_bwd(x, ct):
    s = 1.0 / (1.0 + jnp.exp(-x))
    return (ct * (s * (1.0 + x * (1.0 - s))),)


_silu.defvjp(_silu_fwd, _silu_bwd)


def _log1p_pos(e):
    u = 1.0 + e
    d = u - 1.0
    return jnp.where(d == 0.0, e, jnp.log(u) * (e / jnp.where(d == 0.0, 1.0, d)))


@jax.custom_vjp
def _softplus(x):
    return jnp.maximum(x, 0.0) + _log1p_pos(jnp.exp(-jnp.abs(x)))


def _softplus_fwd(x):
    return _softplus(x), x


def _softplus_bwd(x, ct):
    return (ct / (1.0 + jnp.exp(-x)),)


_softplus.defvjp(_softplus_fwd, _softplus_bwd)


def _make_shift(j):
    @jax.custom_vjp
    def f(ext):
        return pltpu.roll(ext, j, 0)[CHUNK:, :]

    def fwd(ext):
        return f(ext), None

    def bwd(_, ct):
        pad = jnp.concatenate([jnp.zeros_like(ct), ct], axis=0)
        return (pltpu.roll(pad, 2 * CHUNK - j, 0),)

    f.defvjp(fwd, bwd)
    return f


_SHIFT = {j: _make_shift(j) for j in (1, 2, 3)}


def _rms_fwd(x, g):
    r = lax.rsqrt(jnp.mean(x * x, axis=-1, keepdims=True) + RMS_EPS)
    n = x * r
    return n * g, n, r


def _rms_bwd(dy, n, r, g):
    dn = dy * g
    dx = r * (dn - n * jnp.mean(dn * n, axis=-1, keepdims=True))
    dg = jnp.sum(dy * n, axis=0, keepdims=True)
    return dx, dg


def _one(cond):
    return jnp.where(cond, 1.0, 0.0)


def _pool_tile(xe, g, ws, b, scale, tile, tt):
    r = lax.rsqrt(jnp.mean(xe * xe, axis=-1, keepdims=True) + RMS_EPS)
    hn = xe * r * g
    row_e = lax.broadcasted_iota(jnp.int32, (tt + POOL_HALO, POOL_GROUP), 0)
    keep = _one(jnp.logical_or(row_e >= POOL_HALO, tile > 0))
    rr = lax.broadcasted_iota(jnp.int32, (tt, tt + POOL_HALO), 0)
    qq = lax.broadcasted_iota(jnp.int32, (tt, tt + POOL_HALO), 1)
    dd = qq - rr
    tpos = tile * tt + lax.broadcasted_iota(jnp.int32, (tt, POOL_GROUP), 0)
    outs = []
    for gi, w in enumerate(POOL_WINDOWS):
        hg = hn[:, gi * POOL_GROUP:(gi + 1) * POOL_GROUP] * keep
        band = _one(jnp.logical_and(dd >= POOL_HALO - w + 1, dd <= POOL_HALO)).astype(BF16)
        cnt = jnp.minimum(tpos + 1, w).astype(F32)
        pooled = _CMM["cx"](hg, band) / cnt
        mixed = pooled - hg[POOL_HALO:, :]
        outs.append(mm_nn(mixed, ws[gi]))
    out = (jnp.concatenate(outs, axis=1) + b) * scale
    return xe[POOL_HALO:, :] + out


def _pool_specs(tt, nt, rev):
    per = tt // POOL_HALO
    t_of = (lambda i: nt - 1 - i) if rev else (lambda i: i)
    main = pl.BlockSpec((tt, D_MODEL), lambda i: (t_of(i), 0))
    halo = pl.BlockSpec((POOL_HALO, D_MODEL), lambda i: (jnp.maximum(t_of(i) * per - 1, 0), 0))
    vec = pl.BlockSpec((1, D_MODEL), lambda i: (0, 0))
    wsp = pl.BlockSpec((N_DEV, 4 * POOL_SHARD, POOL_GROUP), lambda i: (0, 0, 0))
    return main, halo, vec, wsp


def _pool_weights(w_ref):
    return tuple(
        jnp.concatenate([w_ref[k, gi * POOL_SHARD:(gi + 1) * POOL_SHARD, :] for k in range(N_DEV)], axis=0).astype(F32)
        for gi in range(4))


def _pool_fwd(x, g, w, b, scale):
    t = x.shape[0]
    tt = min(t, 256)
    nt = t // tt
    main, halo, vec, wsp = _pool_specs(tt, nt, False)

    def body(xm_ref, xh_ref, g_ref, w_ref, b_ref, s_ref, o_ref):
        i = pl.program_id(0)
        xe = jnp.concatenate([xh_ref[...], xm_ref[...]], axis=0)
        o_ref[...] = _pool_tile(xe, g_ref[...], _pool_weights(w_ref), b_ref[...], s_ref[...], i, tt)

    return pl.pallas_call(
        body, name="pool_fwd", grid=(nt,),
        in_specs=[main, halo, vec, wsp, vec, vec], out_specs=main,
        out_shape=jax.ShapeDtypeStruct((t, D_MODEL), F32),
        compiler_params=_cp(("arbitrary",)),
    )(x, x, g, w, b, scale)


def _pool_bwd(x, dh, g, w, b, scale, carried=None):
    t = x.shape[0]
    tt = min(t, 256)
    nt = t // tt
    main, halo, vec, wsp = _pool_specs(tt, nt, True)

    def body(xm_ref, xh_ref, dh_ref, g_ref, w_ref, b_ref, s_ref,
             dx_ref, dw_ref, db_ref, ds_ref, dg_ref, carry, dw_acc):
        i = pl.program_id(0)
        tile = nt - 1 - i

        @pl.when(i == 0)
        def _():
            carry[...] = jnp.zeros_like(carry)
            dw_acc[...] = jnp.zeros_like(dw_acc)
            db_ref[...] = jnp.zeros_like(db_ref)
            ds_ref[...] = jnp.zeros_like(ds_ref)
            dg_ref[...] = jnp.zeros_like(dg_ref)

        xe = jnp.concatenate([xh_ref[...], xm_ref[...]], axis=0)
        _, vjp = jax.vjp(lambda a, gg, ww, bb, ss: _pool_tile(a, gg, ww, bb, ss, tile, tt),
                         xe, g_ref[...], _pool_weights(w_ref), b_ref[...], s_ref[...])
        dxe, dgv, dws, dbv, dsv = vjp(dh_ref[...])
        dx_ref[...] = dxe[POOL_HALO:, :]
        dx_ref[tt - POOL_HALO:tt, :] += carry[...]
        carry[...] = dxe[:POOL_HALO, :]
        for gi in range(4):
            dw_acc[gi] += dws[gi]
        db_ref[...] += dbv
        ds_ref[...] += dsv
        dg_ref[...] += dgv

        @pl.when(i == nt - 1)
        def _():
            for k in range(N_DEV):
                for gi in range(4):
                    dw_ref[k, gi * POOL_SHARD:(gi + 1) * POOL_SHARD, :] = dw_acc[gi, k * POOL_SHARD:(k + 1) * POOL_SHARD, :]

    return _pcall(
        body, name="pool_bwd", grid=(nt,),
        in_specs=[main, halo, main, vec, wsp, vec, vec],
        out_specs=[main, wsp, vec, vec, vec],
        out_shape=[jax.ShapeDtypeStruct((t, D_MODEL), F32),
                   jax.ShapeDtypeStruct((N_DEV, 4 * POOL_SHARD, POOL_GROUP), F32),
                   jax.ShapeDtypeStruct((1, D_MODEL), F32),
                   jax.ShapeDtypeStruct((1, D_MODEL), F32),
                   jax.ShapeDtypeStruct((1, D_MODEL), F32)],
        scratch_shapes=[pltpu.VMEM((POOL_HALO, D_MODEL), F32), pltpu.VMEM((4, POOL_GROUP, POOL_GROUP), F32)],
        sem=("arbitrary",), args=(x, x, dh, g, w, b, scale), carried=carried)


def _mlp_weight_specs():
    fb = D_FF // N_DEV
    return (pl.BlockSpec((None, D_MODEL, fb), lambda i, k: (k, 0, 0)),
            pl.BlockSpec((None, fb, D_MODEL), lambda i, k: (k, 0, 0)))


def _mlp_fwd(h, g, w1g, w2g, name, carried=None):
    t = h.shape[0]
    tt = min(t, 512)
    nk, fb = N_DEV, D_FF // N_DEV
    w1_spec, w2_spec = _mlp_weight_specs()

    def body(h_ref, g_ref, w1_ref, w2_ref, o_ref, u_ref, hm_ref, hm_s, acc_s):
        k = pl.program_id(1)

        @pl.when(k == 0)
        def _():
            xv = h_ref[...]
            y, _, _ = _rms_fwd(xv, g_ref[...])
            hb = y.astype(BF16)
            hm_s[...] = hb
            hm_ref[...] = hb
            acc_s[...] = xv

        a = jnp.dot(hm_s[...], w1_ref[...], preferred_element_type=F32)
        u = jnp.maximum(a, 0.0)
        u_ref[...] = u.astype(BF16)
        acc_s[...] += jnp.dot((u * u).astype(BF16), w2_ref[...], preferred_element_type=F32)

        @pl.when(k == nk - 1)
        def _():
            o_ref[...] = acc_s[...]

    return _pcall(
        body, name=name, grid=(t // tt, nk),
        in_specs=[pl.BlockSpec((tt, D_MODEL), lambda i, k: (i, 0)),
                  pl.BlockSpec((1, D_MODEL), lambda i, k: (0, 0)),
                  w1_spec, w2_spec],
        out_specs=[pl.BlockSpec((tt, D_MODEL), lambda i, k: (i, 0)),
                   pl.BlockSpec((tt, fb), lambda i, k: (i, k)),
                   pl.BlockSpec((tt, D_MODEL), lambda i, k: (i, 0))],
        out_shape=[jax.ShapeDtypeStruct((t, D_MODEL), F32),
                   jax.ShapeDtypeStruct((t, nk * fb), BF16),
                   jax.ShapeDtypeStruct((t, D_MODEL), BF16)],
        scratch_shapes=[pltpu.VMEM((tt, D_MODEL), BF16), pltpu.VMEM((tt, D_MODEL), F32)],
        sem=("arbitrary", "arbitrary"), args=(h, g, w1g, w2g), carried=carried)


def _mlp_bwd(dh, dhb, h, g, u, w1g, w2g, name, carried=None):
    t = h.shape[0]
    tt = min(t, 512)
    nk, fb = N_DEV, D_FF // N_DEV
    w1_spec, w2_spec = _mlp_weight_specs()

    def body(dh_ref, dhb_ref, h_ref, g_ref, u_ref, w1_ref, w2_ref,
             dhin_ref, dhinb_ref, da_ref, dg_ref, acc_s):
        i = pl.program_id(0)
        k = pl.program_id(1)

        @pl.when(jnp.logical_and(i == 0, k == 0))
        def _():
            dg_ref[...] = jnp.zeros_like(dg_ref)

        @pl.when(k == 0)
        def _():
            acc_s[...] = jnp.zeros_like(acc_s)

        dv = lax.dot_general(dhb_ref[...], w2_ref[...], _NT, preferred_element_type=F32)
        dab = (dv * (2.0 * u_ref[...].astype(F32))).astype(BF16)
        da_ref[...] = dab
        acc_s[...] += lax.dot_general(dab, w1_ref[...], _NT, preferred_element_type=F32)

        @pl.when(k == nk - 1)
        def _():
            gv = g_ref[...]
            _, n, r = _rms_fwd(h_ref[...], gv)
            dx, dg = _rms_bwd(acc_s[...], n, r, gv)
            dhin = dh_ref[...] + dx
            dhin_ref[...] = dhin
            dhinb_ref[...] = dhin.astype(BF16)
            dg_ref[...] += dg

    tile = pl.BlockSpec((tt, D_MODEL), lambda i, k: (i, 0))
    return _pcall(
        body, name=name, grid=(t // tt, nk),
        in_specs=[tile, tile, tile, pl.BlockSpec((1, D_MODEL), lambda i, k: (0, 0)),
                  pl.BlockSpec((tt, fb), lambda i, k: (i, k)), w1_spec, w2_spec],
        out_specs=[tile, tile, pl.BlockSpec((tt, fb), lambda i, k: (i, k)),
                   pl.BlockSpec((1, D_MODEL), lambda i, k: (0, 0))],
        out_shape=[jax.ShapeDtypeStruct((t, D_MODEL), F32),
                   jax.ShapeDtypeStruct((t, D_MODEL), BF16),
                   jax.ShapeDtypeStruct((t, nk * fb), BF16),
                   jax.ShapeDtypeStruct((1, D_MODEL), F32)],
        scratch_shapes=[pltpu.VMEM((tt, D_MODEL), F32)],
        sem=("arbitrary", "arbitrary"), args=(dh, dhb, h, g, u, w1g, w2g), carried=carried)


def _matmul_tn(a, b, name, square_a=False, col_blocked=False, carried=None):
    t, k1 = a.shape
    k2 = b.shape[1]
    b1 = min(k1, 1024)
    tt = min(t, 512)
    nt = t // tt

    def body(a_ref, b_ref, o_ref, acc):
        s = pl.program_id(2)

        @pl.when(s == 0)
        def _():
            acc[...] = jnp.zeros_like(acc)

        av = a_ref[...]
        if square_a:
            af = av.astype(F32)
            av = (af * af).astype(BF16)
        acc[...] += lax.dot_general(av, b_ref[...], _TN, preferred_element_type=F32)

        @pl.when(s == nt - 1)
        def _():
            o_ref[...] = acc[...].astype(o_ref.dtype)

    if col_blocked:
        out_shape = jax.ShapeDtypeStruct((k2 // COL_BLK, k1, COL_BLK), BF16)
        out_spec = pl.BlockSpec((None, b1, COL_BLK), lambda i, j, s: (j, i, 0))
    else:
        out_shape = jax.ShapeDtypeStruct((k1, k2), BF16)
        out_spec = pl.BlockSpec((b1, COL_BLK), lambda i, j, s: (i, j))
    outs, landed = _pcall(
        body, name=name, grid=(k1 // b1, k2 // COL_BLK, nt),
        in_specs=[pl.BlockSpec((tt, b1), lambda i, j, s: (s, i)),
                  pl.BlockSpec((tt, COL_BLK), lambda i, j, s: (s, j))],
        out_specs=[out_spec], out_shape=[out_shape],
        scratch_shapes=[pltpu.VMEM((b1, COL_BLK), F32)],
        sem=("arbitrary", "arbitrary", "arbitrary"), args=(a, b), carried=carried)
    return (outs[0], landed) if carried is not None else outs[0]


def _norm_matmul(h, g, w, carried=None):
    t = h.shape[0]
    tt = min(t, 512)
    n = w.shape[1]

    def body(h_ref, g_ref, w_ref, o_ref, hn_ref, hn_s):
        @pl.when(pl.program_id(1) == 0)
        def _():
            y, _, _ = _rms_fwd(h_ref[...], g_ref[...])
            hb = y.astype(BF16)
            hn_s[...] = hb
            hn_ref[...] = hb

        o_ref[...] = jnp.dot(hn_s[...], w_ref[...], preferred_element_type=F32)

    return _pcall(
        body, name="ssm_in_proj", grid=(t // tt, n // COL_BLK),
        in_specs=[pl.BlockSpec((tt, D_MODEL), lambda i, j: (i, 0)),
                  pl.BlockSpec((1, D_MODEL), lambda i, j: (0, 0)),
                  pl.BlockSpec((D_MODEL, COL_BLK), lambda i, j: (0, j))],
        out_specs=[pl.BlockSpec((tt, COL_BLK), lambda i, j: (i, j)),
                   pl.BlockSpec((tt, D_MODEL), lambda i, j: (i, 0))],
        out_shape=[jax.ShapeDtypeStruct((t, n), F32), jax.ShapeDtypeStruct((t, D_MODEL), BF16)],
        scratch_shapes=[pltpu.VMEM((tt, D_MODEL), BF16)],
        sem=("arbitrary", "arbitrary"), args=(h, g, w), carried=carried)


def _in_proj_bwd(dzx, w, h, g, dh_next):
    t = h.shape[0]
    tt = min(t, 512)
    n = w.shape[1]
    nj = n // COL_BLK

    def body(dz_ref, w_ref, h_ref, g_ref, dn_ref, dh_ref, dhb_ref, dg_ref, acc):
        i = pl.program_id(0)
        j = pl.program_id(1)

        @pl.when(jnp.logical_and(i == 0, j == 0))
        def _():
            dg_ref[...] = jnp.zeros_like(dg_ref)

        @pl.when(j == 0)
        def _():
            acc[...] = jnp.zeros_like(acc)

        acc[...] += lax.dot_general(dz_ref[...], w_ref[...], _NT, preferred_element_type=F32)

        @pl.when(j == nj - 1)
        def _():
            gv = g_ref[...]
            _, nn, r = _rms_fwd(h_ref[...], gv)
            dx, dg = _rms_bwd(acc[...], nn, r, gv)
            dh = dn_ref[...] + dx
            dh_ref[...] = dh
            dhb_ref[...] = dh.astype(BF16)
            dg_ref[...] += dg

    tile = pl.BlockSpec((tt, D_MODEL), lambda i, j: (i, 0))
    return pl.pallas_call(
        body, name="ssm_in_proj_bwd", grid=(t // tt, nj),
        in_specs=[pl.BlockSpec((tt, COL_BLK), lambda i, j: (i, j)),
                  pl.BlockSpec((D_MODEL, COL_BLK), lambda i, j: (0, j)),
                  tile, pl.BlockSpec((1, D_MODEL), lambda i, j: (0, 0)), tile],
        out_specs=[tile, tile, pl.BlockSpec((1, D_MODEL), lambda i, j: (0, 0))],
        out_shape=[jax.ShapeDtypeStruct((t, D_MODEL), F32), jax.ShapeDtypeStruct((t, D_MODEL), BF16),
                   jax.ShapeDtypeStruct((1, D_MODEL), F32)],
        scratch_shapes=[pltpu.VMEM((tt, D_MODEL), F32)],
        compiler_params=_cp(("arbitrary", "arbitrary")),
    )(dzx, w, h, g, dh_next)


def _ssd_consts():
    lane = lax.broadcasted_iota(jnp.int32, (CHUNK, CHUNK), 1)
    row = lax.broadcasted_iota(jnp.int32, (CHUNK, CHUNK), 0)
    causal = lane <= row
    tri = _one(causal).astype(BF16)
    er = lax.broadcasted_iota(jnp.int32, (CHUNK, GROUP_X), 0)
    ec = lax.broadcasted_iota(jnp.int32, (CHUNK, GROUP_X), 1)
    expand = _one(jnp.right_shift(ec, 6) == er).astype(BF16)
    e2r = lax.broadcasted_iota(jnp.int32, (CHUNK, HEADS_PER_GROUP * CHUNK), 0)
    e2c = lax.broadcasted_iota(jnp.int32, (CHUNK, HEADS_PER_GROUP * CHUNK), 1)
    expand2 = _one(jnp.right_shift(e2c, 7) == e2r).astype(BF16)
    return dict(causal=causal, tri=tri, expand=expand, expand2=expand2, lo=lane < HEAD_DIM)


def _conv_silu(cur, prev, w, b):
    ext = jnp.concatenate([prev, cur], axis=0)
    acc = cur * w[3] + b
    for j in (1, 2, 3):
        acc = acc + _SHIFT[j](ext) * w[3 - j]
    return _silu(acc)


def _ssd_chunk(raw, rawp, ht, cw, cb_, dtb, alog, dsk, k):
    act = _conv_silu(raw[:, :GROUP_CONV], rawp[:, :GROUP_CONV], cw, cb_)
    xs = act[:, :GROUP_X]
    bm = act[:, GROUP_X:GROUP_X + D_STATE]
    cm = act[:, GROUP_X + D_STATE:]
    dt = _softplus(raw[:, GROUP_CONV:] + dtb)
    a = -jnp.exp(alog)
    adt = dt * a
    xc = _CMM["xc"]

    def lanes(rowv):
        return jnp.sum(xc(jnp.broadcast_to(rowv, (16, CHUNK)), k["expand"]), axis=0, keepdims=True) * (1.0 / 16.0)

    dt_e = xc(dt, k["expand"])
    adt_e = dt_e * lanes(a)
    acs_e = _CMM["cx"](adt_e, k["tri"])
    acs = _CMM["cx"](adt, k["tri"])
    tot_e = jnp.sum(adt_e, axis=0, keepdims=True)
    cb_all = xc(acs, k["expand2"])
    gmat = mm_nt(cm, bm)
    xdt = xs * dt_e
    ys = []
    for j in range(HEADS_PER_GROUP // 2):
        ms = []
        for hh in (2 * j, 2 * j + 1):
            cb = cb_all[:, hh * CHUNK:(hh + 1) * CHUNK]
            seg = cb - cb.T
            ms.append(gmat * jnp.exp(jnp.where(k["causal"], seg, -jnp.inf)))
        xp = xdt[:, j * CHUNK:(j + 1) * CHUNK]
        rhs = jnp.concatenate([jnp.where(k["lo"], xp, 0.0), jnp.where(k["lo"], 0.0, xp)], axis=0)
        ys.append(mm_nn(jnp.concatenate(ms, axis=1), rhs))
    y_diag = jnp.concatenate(ys, axis=1)
    y_off = jnp.exp(acs_e) * mm_nn(cm, ht)
    h_new = jnp.exp(tot_e) * ht + mm_tn(bm, xdt * jnp.exp(tot_e - acs_e))
    return y_diag + y_off + lanes(dsk) * xs, h_new


def _ssd_in_specs(nc, rev):
    c_of = (lambda c: nc - 1 - c) if rev else (lambda c: c)
    zx = [pl.BlockSpec((CHUNK, GROUP_COLS), lambda g, c: (c_of(c), g)),
          pl.BlockSpec((CHUNK, GROUP_COLS), lambda g, c: (jnp.maximum(c_of(c) - 1, 0), g))]
    conv = [pl.BlockSpec((4, GROUP_CONV), lambda g, c: (0, g)), pl.BlockSpec((1, GROUP_CONV), lambda g, c: (0, g))]
    head = [pl.BlockSpec((None, 1, 128), lambda g, c: (g, 0, 0))] * 3
    return zx + conv + head, c_of


def _load_chunk_args(refs, has_prev):
    raw, rawp, cw, cb_, dtb, alog, dsk = refs
    return (raw[...], rawp[...] * has_prev, tuple(cw[pl.ds(i, 1), :] for i in range(4)), cb_[...],
            dtb[...], alog[...], dsk[...])


def _ssd_fwd(zx, conv_w, conv_b, dtb, alog, dsk, carried=None):
    t = zx.shape[0]
    nc = t // CHUNK
    in_specs, _ = _ssd_in_specs(nc, False)

    def body(*refs):
        ins, (y_ref, hs_ref, ht) = refs[:7], refs[7:]
        c = pl.program_id(1)

        @pl.when(c == 0)
        def _():
            ht[...] = jnp.zeros_like(ht)

        a = _load_chunk_args(ins, _one(c > 0))
        h_in = ht[...]
        y, h_new = _ssd_chunk(*a[:2], h_in, *a[2:], _ssd_consts())
        y_ref[...] = y
        hs_ref[...] = h_in
        ht[...] = h_new

    return _pcall(
        body, name="ssd_fwd", grid=(N_GROUPS, nc),
        in_specs=in_specs,
        out_specs=[pl.BlockSpec((CHUNK, GROUP_X), lambda g, c: (c, g)),
                   pl.BlockSpec((None, None, D_STATE, GROUP_X), lambda g, c: (g, c, 0, 0))],
        out_shape=[jax.ShapeDtypeStruct((t, D_INNER), F32),
                   jax.ShapeDtypeStruct((N_GROUPS, nc, D_STATE, GROUP_X), F32)],
        scratch_shapes=[pltpu.VMEM((D_STATE, GROUP_X), F32)],
        sem=("arbitrary", "arbitrary"), args=(zx, zx, conv_w, conv_b, dtb, alog, dsk), carried=carried)


def _ssd_bwd(zx, conv_w, conv_b, dtb, alog, dsk, hs, dy, dzx, carried=None):
    t = zx.shape[0]
    nc = t // CHUNK
    in_specs, c_of = _ssd_in_specs(nc, True)
    n_in = 10

    def body(*refs):
        ins, hs_ref, dy_ref = refs[:7], refs[7], refs[8]
        (draw_ref, dcw, dcb, ddtb, dalog, ddsk, dht, carry) = refs[n_in:]
        cc = pl.program_id(1)
        accs = (dcw, dcb, ddtb, dalog, ddsk)

        @pl.when(cc == 0)
        def _():
            for r in (dht, carry) + accs:
                r[...] = jnp.zeros_like(r)

        has_prev = _one(c_of(cc) > 0)
        a = _load_chunk_args(ins, has_prev)
        k = _ssd_consts()
        fn = lambda *args: _ssd_chunk(*args, k)
        _, vjp = jax.vjp(fn, *a[:2], hs_ref[...], *a[2:])
        graw, grawp, ght, gcw, gcb, gdtb, galog, gdsk = vjp((dy_ref[...], dht[...]))
        draw_ref[...] = (graw + carry[...]).astype(BF16)
        carry[...] = grawp * has_prev
        dht[...] = ght
        for i in range(4):
            dcw[pl.ds(i, 1), :] += gcw[i]
        for ref, val in ((dcb, gcb), (ddtb, gdtb), (dalog, galog), (ddsk, gdsk)):
            ref[...] += val

    head_out = pl.BlockSpec((None, 1, 128), lambda g, c: (g, 0, 0))
    sds = jax.ShapeDtypeStruct
    return _pcall(
        body, name="ssd_bwd", grid=(N_GROUPS, nc),
        in_specs=in_specs + [
            pl.BlockSpec((None, None, D_STATE, GROUP_X), lambda g, c: (g, c_of(c), 0, 0)),
            pl.BlockSpec((CHUNK, GROUP_X), lambda g, c: (c_of(c), g)),
            _ANY],
        out_specs=[pl.BlockSpec((CHUNK, GROUP_COLS), lambda g, c: (c_of(c), g)),
                   pl.BlockSpec((4, GROUP_CONV), lambda g, c: (0, g)),
                   pl.BlockSpec((1, GROUP_CONV), lambda g, c: (0, g)),
                   head_out, head_out, head_out],
        out_shape=[sds((t, ZX_COLS), BF16), sds((4, N_GROUPS * GROUP_CONV), F32), sds((1, N_GROUPS * GROUP_CONV), F32),
                   sds((N_GROUPS, 1, 128), F32), sds((N_GROUPS, 1, 128), F32), sds((N_GROUPS, 1, 128), F32)],
        scratch_shapes=[pltpu.VMEM((D_STATE, GROUP_X), F32), pltpu.VMEM((CHUNK, GROUP_COLS), F32)],
        sem=("arbitrary", "arbitrary"), args=(zx, zx, conv_w, conv_b, dtb, alog, dsk, hs, dy, dzx),
        aliases={9: 0}, carried=carried)


def _gate_norm(y, zs, ng):
    outs = []
    for k in range(N_GROUPS):
        s = y[:, k * GROUP_X:(k + 1) * GROUP_X] * _silu(zs[k])
        outs.append(s * lax.rsqrt(jnp.mean(s * s, axis=-1, keepdims=True) + RMS_EPS))
    return jnp.concatenate(outs, axis=1) * ng


def _z_specs(tt):
    first = Z_OFF // GROUP_X
    return [pl.BlockSpec((tt, GROUP_X), functools.partial(lambda k, i: (i, first + k), k)) for k in range(N_GROUPS)]


def _ssm_out_fwd(y, zx, ng, w_out, h):
    t = h.shape[0]
    tt = min(t, 256)

    def body(y_ref, z0, z1, z2, z3, ng_ref, w_ref, h_ref, o_ref):
        yn = _gate_norm(y_ref[...], (z0[...], z1[...], z2[...], z3[...]), ng_ref[...])
        o_ref[...] = h_ref[...] + jnp.dot(yn.astype(BF16), w_ref[...], preferred_element_type=F32)

    return pl.pallas_call(
        body, name="ssm_out_fwd", grid=(t // tt,),
        in_specs=[pl.BlockSpec((tt, D_INNER), lambda i: (i, 0))] + _z_specs(tt) + [
            pl.BlockSpec((1, D_INNER), lambda i: (0, 0)),
            pl.BlockSpec((D_INNER, D_MODEL), lambda i: (0, 0)),
            pl.BlockSpec((tt, D_MODEL), lambda i: (i, 0))],
        out_specs=pl.BlockSpec((tt, D_MODEL), lambda i: (i, 0)),
        out_shape=jax.ShapeDtypeStruct((t, D_MODEL), F32),
        compiler_params=_cp(("arbitrary",)),
    )(y, zx, zx, zx, zx, ng, w_out, h)


def _gate_norm_group(y, z, ng):
    s = y * _silu(z)
    return s * lax.rsqrt(jnp.mean(s * s, axis=-1, keepdims=True) + RMS_EPS) * ng


def _ssm_out_bwd(dhb, y, zx, ng, w_out):
    t = dhb.shape[0]
    tt = min(t, 512)
    first = Z_OFF // GROUP_X

    def body(dh_ref, y_ref, z_ref, ng_ref, w_ref, dy_ref, dzx_ref, yn_ref, dng_ref):
        @pl.when(pl.program_id(1) == 0)
        def _():
            dng_ref[...] = jnp.zeros_like(dng_ref)

        dyn = lax.dot_general(dh_ref[...], w_ref[...], _NT, preferred_element_type=F32)
        yn, vjp = jax.vjp(_gate_norm_group, y_ref[...], z_ref[...], ng_ref[...])
        dy, dz, dng = vjp(dyn)
        dy_ref[...] = dy
        dzx_ref[...] = dz.astype(BF16)
        yn_ref[...] = yn.astype(BF16)
        dng_ref[...] += dng

    grp = pl.BlockSpec((tt, GROUP_X), lambda k, i: (i, k))
    zgrp = pl.BlockSpec((tt, GROUP_X), lambda k, i: (i, first + k))
    gain = pl.BlockSpec((1, GROUP_X), lambda k, i: (0, k))
    return pl.pallas_call(
        body, name="ssm_out_bwd", grid=(N_GROUPS, t // tt),
        in_specs=[pl.BlockSpec((tt, D_MODEL), lambda k, i: (i, 0)), grp, zgrp, gain,
                  pl.BlockSpec((GROUP_X, D_MODEL), lambda k, i: (k, 0))],
        out_specs=[grp, zgrp, grp, gain],
        out_shape=[jax.ShapeDtypeStruct((t, D_INNER), F32), jax.ShapeDtypeStruct((t, ZX_COLS), BF16),
                   jax.ShapeDtypeStruct((t, D_INNER), BF16), jax.ShapeDtypeStruct((1, D_INNER), F32)],
        compiler_params=_cp(("arbitrary", "arbitrary")),
    )(dhb, y, zx, ng, w_out)


def _final(h, g, tgt):
    t = h.shape[0]
    tt = min(t, 512)
    nt = t // tt

    def body(h_ref, g_ref, t_ref, dh_ref, dhb_ref, loss_ref, dg_ref, lacc):
        i = pl.program_id(0)

        @pl.when(i == 0)
        def _():
            dg_ref[...] = jnp.zeros_like(dg_ref)
            lacc[...] = jnp.zeros_like(lacc)

        gv = g_ref[...]
        y, n, r = _rms_fwd(h_ref[...], gv)
        err = y - t_ref[...]
        lacc[...] += jnp.sum(err * err, axis=0, keepdims=True)
        dx, dg = _rms_bwd(err * (1.0 / D_MODEL), n, r, gv)
        dh_ref[...] = dx
        dhb_ref[...] = dx.astype(BF16)
        dg_ref[...] += dg

        @pl.when(i == nt - 1)
        def _():
            loss_ref[...] = jnp.zeros_like(loss_ref) + (0.5 / D_MODEL) * jnp.sum(lacc[...])

    tile = pl.BlockSpec((tt, D_MODEL), lambda i: (i, 0))
    vec = pl.BlockSpec((1, D_MODEL), lambda i: (0, 0))
    return pl.pallas_call(
        body, name="final_loss", grid=(nt,),
        in_specs=[tile, vec, tile],
        out_specs=[tile, tile, pl.BlockSpec((1, 128), lambda i: (0, 0)), vec],
        out_shape=[jax.ShapeDtypeStruct((t, D_MODEL), F32), jax.ShapeDtypeStruct((t, D_MODEL), BF16),
                   jax.ShapeDtypeStruct((1, 128), F32), jax.ShapeDtypeStruct((1, D_MODEL), F32)],
        scratch_shapes=[pltpu.VMEM((1, D_MODEL), F32)],
        compiler_params=_cp(("arbitrary",)),
    )(h, g, tgt)


def _chips(x, y):
    return [(1 - x, y), (x, 1 - y), (1 - x, 1 - y)]


def _gather(shards, layout, out_shapes, name):
    n, m = len(shards), len(out_shapes)

    def body(*refs):
        ins, outs = refs[:n], refs[n:n + m]
        send_sems, recv_sems, local_sems = refs[n + m:]
        x, y, c = _place()
        me, sibling = (x, y, c), (x, y, 1 - c)
        chips = _chips(x, y)

        def win(i, place):
            j, off = layout[i]
            return outs[j].at[4 * place[0] + 2 * place[1] + place[2], pl.ds(off, shards[i].shape[0])]

        def copy(i, k, block, to, src=None):
            return pltpu.make_async_remote_copy(
                src_ref=win(i, block) if src is None else src, dst_ref=win(i, block),
                send_sem=send_sems.at[7 * i + k], recv_sem=recv_sems.at[7 * i + k], device_id=to, device_id_type=MESH)

        mine = [pltpu.make_async_copy(ins[i], win(i, me), local_sems.at[i]) for i in range(n)]
        first = [[copy(i, 0, me, sibling, src=ins[i])] +
                 [copy(i, 1 + j, me, (*chip, c), src=ins[i]) for j, chip in enumerate(chips)] for i in range(n)]
        passed = [[copy(i, 4 + j, (*chip, c), sibling) for j, chip in enumerate(chips)] for i in range(n)]
        for i in range(n):
            mine[i].start()
            for cp in first[i]:
                cp.start()
        for i in range(n):
            for j, chip in enumerate(chips):
                copy(i, 1 + j, (*chip, c), me).wait_recv()
                passed[i][j].start()
        for i in range(n):
            copy(i, 0, sibling, me).wait_recv()
            for j, chip in enumerate(chips):
                copy(i, 4 + j, (*chip, 1 - c), me).wait_recv()
        for i in range(n):
            for cp in first[i] + passed[i]:
                cp.wait_send()
            mine[i].wait()

    return pl.pallas_call(
        body, name=name,
        out_shape=[jax.ShapeDtypeStruct((N_DEV,) + tuple(s), d) for s, d in out_shapes],
        in_specs=[_ANY] * n, out_specs=[_ANY] * m,
        scratch_shapes=[pltpu.SemaphoreType.DMA((7 * n,)), pltpu.SemaphoreType.DMA((7 * n,)),
                        pltpu.SemaphoreType.DMA((n,))],
    )(*shards)


def _all_reduce_small(sp):
    rows, n = sp.shape

    def body(x_ref, o_ref, land, send_sems, recv_sems):
        x, y, c = _place()
        me = 4 * x + 2 * y + c
        land[me] = x_ref[...]
        cps = []
        for rel in range(1, N_DEV):
            dx, dy, dc = (rel >> 2) & 1, (rel >> 1) & 1, rel & 1
            px = x + dx - 2 * x * dx
            py = y + dy - 2 * y * dy
            pc = c + dc - 2 * c * dc
            peer = 4 * px + 2 * py + pc
            cps.append((pltpu.make_async_remote_copy(
                src_ref=x_ref, dst_ref=land.at[me], send_sem=send_sems.at[rel - 1], recv_sem=recv_sems.at[rel - 1],
                device_id=(px, py, pc), device_id_type=MESH),
                pltpu.make_async_remote_copy(
                src_ref=x_ref, dst_ref=land.at[peer], send_sem=send_sems.at[rel - 1], recv_sem=recv_sems.at[rel - 1],
                device_id=(px, py, pc), device_id_type=MESH)))
        for cp, _ in cps:
            cp.start()
        for _, arr in cps:
            arr.wait_recv()
        for cp, _ in cps:
            cp.wait_send()
        acc = land[0]
        for k in range(1, N_DEV):
            acc = acc + land[k]
        o_ref[...] = acc

    vm = pl.BlockSpec(memory_space=pltpu.VMEM)
    return pl.pallas_call(
        body, name="all_reduce_small",
        out_shape=jax.ShapeDtypeStruct((rows, n), F32),
        in_specs=[vm], out_specs=vm,
        scratch_shapes=[pltpu.VMEM((N_DEV, rows, n), F32),
                        pltpu.SemaphoreType.DMA((N_DEV - 1,)), pltpu.SemaphoreType.DMA((N_DEV - 1,))],
    )(sp)


def _adamw_math(wv, gv, mv, vv):
    m2 = ADAM_B1 * mv + (1.0 - ADAM_B1) * gv
    v2 = ADAM_B2 * vv + (1.0 - ADAM_B2) * (gv * gv)
    m_hat = m2 / (1.0 - ADAM_B1 ** ADAM_STEP)
    v_hat = v2 / (1.0 - ADAM_B2 ** ADAM_STEP)
    return -ADAM_LR * (m_hat / (jnp.sqrt(v_hat) + ADAM_EPS) + ADAM_WD * wv), m2, v2


def _adamw(w, g, m, v, name):
    rows, cols = w.shape
    br = rows if rows <= 256 else 256

    def body(w_ref, g_ref, m_ref, v_ref, d_ref, m2_ref, v2_ref):
        d_ref[...], m2_ref[...], v2_ref[...] = _adamw_math(w_ref[...], g_ref[...], m_ref[...], v_ref[...])

    spec = pl.BlockSpec((br, cols), lambda i: (i, 0))
    out = jax.ShapeDtypeStruct((rows, cols), F32)
    return pl.pallas_call(
        body, name=name, grid=(rows // br,),
        in_specs=[spec] * 4, out_specs=[spec] * 3, out_shape=[out] * 3,
        compiler_params=_cp(("arbitrary",)),
    )(w, g, m, v)


def _adamw_reduced(w, land, m, v, name):
    rows, cols = w.shape
    br = rows if rows <= 256 else 256
    nl = land.shape[0]

    def body(w_ref, l_ref, m_ref, v_ref, g_ref, d_ref, m2_ref, v2_ref):
        gv = l_ref[0].astype(F32)
        for q in range(1, nl):
            gv = gv + l_ref[q].astype(F32)
        g_ref[...] = gv
        d_ref[...], m2_ref[...], v2_ref[...] = _adamw_math(w_ref[...], gv, m_ref[...], v_ref[...])

    spec = pl.BlockSpec((br, cols), lambda i: (i, 0))
    out = jax.ShapeDtypeStruct((rows, cols), F32)
    return pl.pallas_call(
        body, name=name, grid=(rows // br,),
        in_specs=[spec, pl.BlockSpec((nl, br, cols), lambda i: (0, i, 0)), spec, spec],
        out_specs=[spec] * 4, out_shape=[out] * 4,
        compiler_params=_cp(("arbitrary",)),
    )(w, land, m, v)


def _zx_source_col(col):
    blk = jnp.right_shift(col, 7)
    lane = jnp.bitwise_and(col, 127)
    per = GROUP_COLS // 128
    grp = jnp.where(blk >= per, 1, 0) + jnp.where(blk >= 2 * per, 1, 0) + jnp.where(blk >= 3 * per, 1, 0)
    o = blk - per * grp
    x_col = D_INNER + GROUP_X * grp + 128 * o + lane
    b_col = 2 * D_INNER + D_STATE * grp + lane
    c_col = 2 * D_INNER + N_GROUPS * D_STATE + D_STATE * grp + lane
    dt_col = jnp.where(lane < HEADS_PER_GROUP, D_INNER + CONV_DIM + HEADS_PER_GROUP * grp + lane, -1)
    src = jnp.where(o < 4, x_col, jnp.where(o == 4, b_col, jnp.where(o == 5, c_col, dt_col)))
    return jnp.where(col >= Z_OFF, col - Z_OFF, src)


def _zx_source_col_py(col):
    if col >= Z_OFF:
        return col - Z_OFF
    grp, o = divmod(col, GROUP_COLS)
    if o < GROUP_X:
        return D_INNER + GROUP_X * grp + o
    if o < GROUP_X + D_STATE:
        return 2 * D_INNER + D_STATE * grp + (o - GROUP_X)
    if o < GROUP_CONV:
        return 2 * D_INNER + N_GROUPS * D_STATE + D_STATE * grp + (o - GROUP_X - D_STATE)
    h = o - GROUP_CONV
    return D_INNER + CONV_DIM + HEADS_PER_GROUP * grp + h if h < HEADS_PER_GROUP else -1


def _overlap_tables():
    nblk = ZX_COLS // COL_BLK
    src = [_zx_source_col_py(c) for c in range(ZX_COLS)]
    fwd = [sorted({s // W_IN_SHARD for s in src[COL_BLK * j:COL_BLK * (j + 1)] if s >= 0}) for j in range(nblk)]
    dst = {s: c for c, s in enumerate(src) if s >= 0}
    bwd = [sorted({dst[s] // COL_BLK for s in range(W_IN_SHARD * k, W_IN_SHARD * (k + 1))}) for k in range(N_DEV)]

    def flat(rows):
        width = max(len(r) for r in rows)
        idx = [r + [r[-1]] * (width - len(r)) for r in rows]
        val = [[1] * len(r) + [0] * (width - len(r)) for r in rows]
        return (jnp.asarray(sum(idx, []), jnp.int32), jnp.asarray(sum(val, []), jnp.int32), width)

    return flat(fwd), flat(bwd)


def _w_in_to_zx(w_in_g):
    (tab, val, width), _ = _overlap_tables()
    nblk = ZX_COLS // COL_BLK

    def body(tab_ref, val_ref, w_ref, o_ref, acc):
        j = pl.program_id(0)
        s = pl.program_id(1)

        @pl.when(s == 0)
        def _():
            acc[...] = jnp.zeros_like(acc)

        @pl.when(val_ref[j * width + s] == 1)
        def _():
            k = tab_ref[j * width + s]
            col = COL_BLK * j + lax.broadcasted_iota(jnp.int32, (W_IN_SHARD, COL_BLK), 1)
            row = W_IN_SHARD * k + lax.broadcasted_iota(jnp.int32, (W_IN_SHARD, COL_BLK), 0)
            place = _one(_zx_source_col(col) == row).astype(BF16)
            acc[...] += jnp.dot(w_ref[...], place, preferred_element_type=F32)

        @pl.when(s == width - 1)
        def _():
            o_ref[...] = acc[...].astype(BF16)

    return pl.pallas_call(
        body, name="w_in_to_zx",
        grid_spec=pltpu.PrefetchScalarGridSpec(
            num_scalar_prefetch=2, grid=(nblk, width),
            in_specs=[pl.BlockSpec((None, D_MODEL, W_IN_SHARD), lambda j, s, tab, val: (tab[j * width + s], 0, 0))],
            out_specs=pl.BlockSpec((D_MODEL, COL_BLK), lambda j, s, tab, val: (0, j)),
            scratch_shapes=[pltpu.VMEM((D_MODEL, COL_BLK), F32)]),
        out_shape=jax.ShapeDtypeStruct((D_MODEL, ZX_COLS), BF16),
        compiler_params=_cp(("arbitrary", "arbitrary")),
    )(tab, val, w_in_g)


def _zx_to_w_in(d_wzx):
    _, (tab, val, width) = _overlap_tables()

    def body(tab_ref, val_ref, d_ref, o_ref, acc):
        k = pl.program_id(0)
        s = pl.program_id(1)

        @pl.when(s == 0)
        def _():
            acc[...] = jnp.zeros_like(acc)

        @pl.when(val_ref[k * width + s] == 1)
        def _():
            j = tab_ref[k * width + s]
            col = COL_BLK * j + lax.broadcasted_iota(jnp.int32, (COL_BLK, W_IN_SHARD), 0)
            row = W_IN_SHARD * k + lax.broadcasted_iota(jnp.int32, (COL_BLK, W_IN_SHARD), 1)
            place = _one(_zx_source_col(col) == row).astype(BF16)
            acc[...] += jnp.dot(d_ref[...], place, preferred_element_type=F32)

        @pl.when(s == width - 1)
        def _():
            o_ref[...] = acc[...].astype(BF16)

    return pl.pallas_call(
        body, name="zx_to_w_in",
        grid_spec=pltpu.PrefetchScalarGridSpec(
            num_scalar_prefetch=2, grid=(N_DEV, width),
            in_specs=[pl.BlockSpec((D_MODEL, COL_BLK), lambda k, s, tab, val: (0, tab[k * width + s]))],
            out_specs=pl.BlockSpec((None, D_MODEL, W_IN_SHARD), lambda k, s, tab, val: (k, 0, 0)),
            scratch_shapes=[pltpu.VMEM((D_MODEL, W_IN_SHARD), F32)]),
        out_shape=jax.ShapeDtypeStruct((N_DEV, D_MODEL, W_IN_SHARD), BF16),
        compiler_params=_cp(("arbitrary", "arbitrary")),
    )(tab, val, d_wzx)


def _group_conv_cols(a):
    rows = a.shape[0]
    x = a[:, :D_INNER].reshape(rows, N_GROUPS, GROUP_X)
    b = a[:, D_INNER:D_INNER + N_GROUPS * D_STATE].reshape(rows, N_GROUPS, D_STATE)
    c = a[:, D_INNER + N_GROUPS * D_STATE:].reshape(rows, N_GROUPS, D_STATE)
    return jnp.concatenate([x, b, c], axis=2).reshape(rows, N_GROUPS * GROUP_CONV)


def _ungroup_conv_cols(a):
    rows = a.shape[0]
    a3 = a.reshape(rows, N_GROUPS, GROUP_CONV)
    return jnp.concatenate([a3[:, :, :GROUP_X].reshape(rows, D_INNER),
                            a3[:, :, GROUP_X:GROUP_X + D_STATE].reshape(rows, N_GROUPS * D_STATE),
                            a3[:, :, GROUP_X + D_STATE:].reshape(rows, N_GROUPS * D_STATE)], axis=1)


def _small_shard(conv_w, conv_b, norm_g):
    ng = jnp.pad(norm_g.reshape(1, -1), ((0, 0), (0, CONV_SHARD - norm_g.shape[-1])))
    return jnp.concatenate([conv_w.reshape(4, CONV_SHARD), conv_b.reshape(1, CONV_SHARD), ng,
                            jnp.zeros((SMALL_ROWS - 6, CONV_SHARD), F32)], axis=0)


def _small_unshard(a):
    return a[0:4].reshape(1, 4, CONV_SHARD), a[4:5], a[5:6, :D_INNER // N_DEV]


def _heads_of(a):
    return a[:, :, :HEADS_PER_GROUP].reshape(1, N_HEADS)


def _head_params(p):
    return jnp.pad(p.reshape(N_GROUPS, 1, HEADS_PER_GROUP), ((0, 0), (0, 0), (0, 128 - HEADS_PER_GROUP)))


def _update(w, land, m, v, name):
    shp = w.shape
    to2 = lambda a: a.reshape(-1, shp[-1])
    return tuple(o.reshape(shp) for o in _adamw_reduced(to2(w), land, to2(m), to2(v), name))


def kernel(x, norm_mix_g, norm_mlp_g, pool_w, pool_b, pool_scale, ssm_w_in, ssm_conv_w, ssm_conv_b, ssm_dt_bias, ssm_a_log, ssm_d, ssm_norm_g, ssm_w_out, mlp_w1, mlp_w2, final_g, loss_target, m_norm_mix_g, m_norm_mlp_g, m_pool_w, m_pool_b, m_pool_scale, m_ssm_w_in, m_ssm_conv_w, m_ssm_conv_b, m_ssm_dt_bias, m_ssm_a_log, m_ssm_d, m_ssm_norm_g, m_ssm_w_out, m_mlp_w1, m_mlp_w2, m_final_g, v_norm_mix_g, v_norm_mlp_g, v_pool_w, v_pool_b, v_pool_scale, v_ssm_w_in, v_ssm_conv_w, v_ssm_conv_b, v_ssm_dt_bias, v_ssm_a_log, v_ssm_d, v_ssm_norm_g, v_ssm_w_out, v_mlp_w1, v_mlp_w2, v_final_g):
    x2 = x[0]
    tgt = loss_target[0]
    gm0, gm1 = norm_mix_g[0:1], norm_mix_g[1:2]
    gl0, gl1 = norm_mlp_g[0:1], norm_mlp_g[1:2]
    gfin = final_g.reshape(1, D_MODEL)

    fb = D_FF // N_DEV

    def bf(a):
        return a.astype(BF16)

    def gather_of(shards):
        return _direct_exchange(shards, [(i, 0) for i in range(len(shards))],
                                [(s.shape, s.dtype) for s in shards], scatter=False)

    def scatter_of(parts):
        return _direct_exchange(parts, [(i, 0) for i in range(len(parts))],
                                [(p.shape[1:], p.dtype) for p in parts], scatter=True)

    first = [bf(pool_w.reshape(4 * POOL_SHARD, POOL_GROUP)), bf(mlp_w1[0]), bf(mlp_w2[0]),
             _small_shard(ssm_conv_w, ssm_conv_b, ssm_norm_g)]
    w_pool, w1g0, w2g0, small_g = _gather(first, [(i, 0) for i in range(4)], [(s.shape, s.dtype) for s in first],
                                          "gather_layer0")
    conv_w = _group_conv_cols(small_g[:, 0:4].transpose(1, 0, 2).reshape(4, CONV_DIM))
    conv_b = _group_conv_cols(small_g[:, 4].reshape(1, CONV_DIM))
    ssm_ng = small_g[:, 5, :D_INNER // N_DEV].reshape(1, D_INNER)
    dtb, alog, dsk = _head_params(ssm_dt_bias), _head_params(ssm_a_log), _head_params(ssm_d)

    h1 = _pool_fwd(x2, gm0, w_pool, pool_b, pool_scale)
    (h2, u0, hm0), (w_in_g,) = _mlp_fwd(h1, gl0, w1g0, w2g0, "mlp0_fwd", carried=gather_of([bf(ssm_w_in[0])]))
    w_zx = _w_in_to_zx(w_in_g)
    (zx, hn1), (w1g1, w_out_g) = _norm_matmul(h2, gm1, w_zx, carried=gather_of([bf(mlp_w1[1]), bf(ssm_w_out[0])]))
    (y_ssd, states), (w2g1,) = _ssd_fwd(zx, conv_w, conv_b, dtb, alog, dsk, carried=gather_of([bf(mlp_w2[1])]))
    w_out = w_out_g.reshape(D_INNER, D_MODEL)
    h3 = _ssm_out_fwd(y_ssd, zx, ssm_ng, w_out, h2)
    (h4, u1, hm1), _ = _mlp_fwd(h3, gl1, w1g1, w2g1, "mlp1_fwd")
    dh4, dh4b, loss_row, d_gfin = _final(h4, gfin, tgt)

    (dh3, dh3b, da1, d_gl1), _ = _mlp_bwd(dh4, dh4b, h3, gl1, u1, w1g1, w2g1, "mlp1_bwd")
    d_w1_1 = _matmul_tn(hm1, da1, "mlp1_dw1", col_blocked=True)
    d_w2_1 = _matmul_tn(u1, dh4b, "mlp1_dw2", square_a=True).reshape(N_DEV, fb, D_MODEL)
    dy_ssd, dzx, yn, d_ng = _ssm_out_bwd(dh3b, y_ssd, zx, ssm_ng, w_out)
    d_wout = _matmul_tn(yn, dh3b, "ssm_dw_out").reshape(N_DEV, D_INNER // N_DEV, D_MODEL)
    (dzx, d_cw, d_cb, d_dtb, d_alog, d_dsk), (l_w1_1, l_w2_1, l_wout) = _ssd_bwd(
        zx, conv_w, conv_b, dtb, alog, dsk, states, dy_ssd, dzx, carried=scatter_of([d_w1_1, d_w2_1, d_wout]))
    dh2, dh2b, d_gm1 = _in_proj_bwd(dzx, w_zx, h2, gm1, dh3)
    d_w2_0 = _matmul_tn(u0, dh2b, "mlp0_dw2", square_a=True).reshape(N_DEV, fb, D_MODEL)
    d_wzx, (l_w2_0,) = _matmul_tn(hn1, dzx, "ssm_dw_in", carried=scatter_of([d_w2_0]))
    d_w_in = _zx_to_w_in(d_wzx)
    (dh1, _, da0, d_gl0), (l_w_in,) = _mlp_bwd(dh2, dh2b, h1, gl0, u0, w1g0, w2g0, "mlp0_bwd",
                                           carried=scatter_of([d_w_in]))
    d_w1_0 = _matmul_tn(hm0, da0, "mlp0_dw1", col_blocked=True)
    (dx, d_pool, d_pb, d_ps, d_gm0), (l_w1_0,) = _pool_bwd(x2, dh1, gm0, w_pool, pool_b, pool_scale,
                                                          carried=scatter_of([d_w1_0]))

    d_conv_w = _ungroup_conv_cols(d_cw).reshape(4, N_DEV, CONV_SHARD).transpose(1, 0, 2)
    d_conv_b = _ungroup_conv_cols(d_cb).reshape(N_DEV, 1, CONV_SHARD)
    d_gain = jnp.pad(d_ng.reshape(N_DEV, 1, D_INNER // N_DEV), ((0, 0), (0, 0), (0, CONV_SHARD - D_INNER // N_DEV)))
    d_small = jnp.concatenate([d_conv_w, d_conv_b, d_gain,
                               jnp.zeros((N_DEV, SMALL_ROWS - 6, CONV_SHARD), F32)], axis=1)
    l_pool, l_small = _run_exchange(scatter_of([bf(d_pool), d_small]), "reduce_scatter_tail")

    heads = jnp.concatenate([_heads_of(a) for a in (d_dtb, d_alog, d_dsk)], axis=1)
    sp = jnp.concatenate([d_gm0, d_gm1, d_gl0, d_gl1, d_pb, d_ps, d_gfin,
                          jnp.pad(heads, ((0, 0), (0, D_MODEL - 3 * N_HEADS)))], axis=0)
    sg = _all_reduce_small(sp)

    g_norm_mix = sg[0:2]
    g_norm_mlp = sg[2:4]
    g_pool_b, g_pool_scale = sg[4:5], sg[5:6]
    g_final = sg[6]
    g_dtb, g_alog, g_dsk = sg[7:8, 0:32], sg[7:8, 32:64], sg[7:8, 64:96]

    def rep_pack(nm, nl, pb, ps, fg, db, al, dk):
        hd = jnp.pad(jnp.concatenate([db, al, dk], axis=1), ((0, 0), (0, D_MODEL - 3 * N_HEADS)))
        return jnp.concatenate([nm, nl, pb, ps, fg.reshape(1, D_MODEL), hd], axis=0)

    rep = [rep_pack(*t) for t in (
        (norm_mix_g, norm_mlp_g, pool_b, pool_scale, final_g, ssm_dt_bias, ssm_a_log, ssm_d),
        (g_norm_mix, g_norm_mlp, g_pool_b, g_pool_scale, g_final, g_dtb, g_alog, g_dsk),
        (m_norm_mix_g, m_norm_mlp_g, m_pool_b, m_pool_scale, m_final_g, m_ssm_dt_bias, m_ssm_a_log, m_ssm_d),
        (v_norm_mix_g, v_norm_mlp_g, v_pool_b, v_pool_scale, v_final_g, v_ssm_dt_bias, v_ssm_a_log, v_ssm_d))]
    rep_out = _adamw(*rep, "adamw_replicated")

    def rep_unpack(a):
        return (a[0:2], a[2:4], a[4:5], a[5:6], a[6], a[7:8, 0:32], a[7:8, 32:64], a[7:8, 64:96])

    sm_out = _adamw_reduced(_small_shard(ssm_conv_w, ssm_conv_b, ssm_norm_g), l_small,
                            _small_shard(m_ssm_conv_w, m_ssm_conv_b, m_ssm_norm_g),
                            _small_shard(v_ssm_conv_w, v_ssm_conv_b, v_ssm_norm_g), "adamw_small_shards")

    def update_layers(w, lands, m, v, name):
        per = [_update(w[l], lands[l], m[l], v[l], name + str(l)) for l in range(2)]
        return tuple(jnp.stack([per[0][k], per[1][k]]) for k in range(4))

    big = {
        "pool_w": _update(pool_w, l_pool, m_pool_w, v_pool_w, "adamw_pool_w"),
        "ssm_w_in": _update(ssm_w_in, l_w_in, m_ssm_w_in, v_ssm_w_in, "adamw_w_in"),
        "ssm_w_out": _update(ssm_w_out, l_wout, m_ssm_w_out, v_ssm_w_out, "adamw_w_out"),
        "mlp_w1": update_layers(mlp_w1, (l_w1_0, l_w1_1), m_mlp_w1, v_mlp_w1, "adamw_w1_"),
        "mlp_w2": update_layers(mlp_w2, (l_w2_0, l_w2_1), m_mlp_w2, v_mlp_w2, "adamw_w2_"),
    }
    rep_all = (rep[1],) + tuple(rep_out)

    def ordered(kind):
        nm, nl, pb, ps, fg, db, al, dk = rep_unpack(rep_all[kind])
        cw, cb, ng = _small_unshard(sm_out[kind])
        return [nm, nl, big["pool_w"][kind], pb, ps, big["ssm_w_in"][kind], cw, cb, db, al, dk, ng,
                big["ssm_w_out"][kind], big["mlp_w1"][kind], big["mlp_w2"][kind], fg]

    loss = lax.psum(loss_row[0, 0], ("x", "y", "c"))
    return (loss, dx[None], *ordered(0), *ordered(1), *ordered(2), *ordered(3))
```

```python
import functools

import jax
import jax.numpy as jnp
from jax import lax
from jax.experimental import pallas as pl
from jax.experimental.pallas import tpu as pltpu

F32 = jnp.float32
BF16 = jnp.bfloat16
MESH = pl.DeviceIdType.MESH

D_MODEL = 1024
RMS_EPS = 1e-5
POOL_WINDOWS = (2, 4, 8, 16)
POOL_GROUP = 256
POOL_HALO = 16
POOL_SHARD = POOL_GROUP // 8
D_INNER = 2048
HEAD_DIM = 64
N_HEADS = 32
N_GROUPS = 4
HEADS_PER_GROUP = 8
D_STATE = 128
CHUNK = 128
CONV_DIM = 3072
IN_PROJ_DIM = 5152
D_FF = 4096
N_DEV = 8
GROUP_X = HEADS_PER_GROUP * HEAD_DIM
GROUP_CONV = GROUP_X + 2 * D_STATE
GROUP_COLS = GROUP_CONV + 128
Z_OFF = N_GROUPS * GROUP_COLS
ZX_COLS = Z_OFF + D_INNER
COL_BLK = 512
W_IN_SHARD = IN_PROJ_DIM // N_DEV

ADAM_LR = 0.001
ADAM_B1 = 0.9
ADAM_B2 = 0.999
ADAM_EPS = 1e-08
ADAM_WD = 0.01
ADAM_STEP = 10

VMEM_LIMIT_V7X = 56 * 1024 * 1024
MATMUL_TOKENS = 1024

CONV_SHARD = CONV_DIM // N_DEV
SMALL_ROWS = 8

_NN = (((1,), (0,)), ((), ()))
_NT = (((1,), (1,)), ((), ()))
_TN = (((0,), (0,)), ((), ()))


def _cp(sem):
    return pltpu.CompilerParams(dimension_semantics=sem, vmem_limit_bytes=VMEM_LIMIT_V7X)


_ANY = pl.BlockSpec(memory_space=pl.ANY)


def _place():
    return lax.axis_index("x"), lax.axis_index("y"), lax.axis_index("c")


class _Carried:
    def __init__(self, ins, outs, sems, start, finish):
        self.ins, self.outs, self.sems, self.start, self.finish = list(ins), list(outs), list(sems), start, finish


def _pcall(body, *, name, grid, in_specs, out_specs, out_shape, sem, args, scratch_shapes=(), carried=None,
           aliases=None):
    in_specs, out_specs, out_shape, scratch = list(in_specs), list(out_specs), list(out_shape), list(scratch_shapes)
    common = dict(name=name, grid=grid, input_output_aliases=aliases or {}, compiler_params=_cp(sem))
    if carried is None:
        res = pl.pallas_call(body, in_specs=in_specs, out_specs=out_specs, out_shape=out_shape,
                             scratch_shapes=scratch, **common)(*args)
        return list(res), []
    n_in, n_out, n_scr = len(in_specs), len(out_specs), len(scratch)
    ci, co = len(carried.ins), len(carried.outs)

    def wrapped(*refs):
        ins, cins = refs[:n_in], refs[n_in:n_in + ci]
        p = n_in + ci
        outs, couts = refs[p:p + n_out], refs[p + n_out:p + n_out + co]
        p += n_out + co
        scr, csems = refs[p:p + n_scr], refs[p + n_scr:]
        ids = [pl.program_id(a) for a in range(len(grid))]
        first = functools.reduce(jnp.logical_and, [i == 0 for i in ids])
        last = functools.reduce(jnp.logical_and, [i == g - 1 for i, g in zip(ids, grid)])

        @pl.when(first)
        def _():
            carried.start(cins, couts, csems)

        body(*ins, *outs, *scr)

        @pl.when(last)
        def _():
            carried.finish(cins, couts, csems)

    res = pl.pallas_call(wrapped, in_specs=in_specs + [_ANY] * ci, out_specs=out_specs + [_ANY] * co,
                         out_shape=out_shape + carried.outs, scratch_shapes=scratch + carried.sems,
                         **common)(*args, *carried.ins)
    return list(res[:n_out]), list(res[n_out:])


def _peers(x, y, c):
    out = []
    for rel in range(1, N_DEV):
        dx, dy, dc = (rel >> 2) & 1, (rel >> 1) & 1, rel & 1
        out.append((x + dx - 2 * x * dx, y + dy - 2 * y * dy, c + dc - 2 * c * dc))
    return out


def _direct_exchange(srcs, layout, out_shapes, scatter):
    n = len(srcs)

    def copies(ins, outs, sems):
        send, recv, loc = sems
        x, y, c = _place()
        me = 4 * x + 2 * y + c
        out, arrive, local = [], [], []
        for i in range(n):
            j, off = layout[i]
            rows = srcs[i].shape[-2]
            for r, peer in enumerate(_peers(x, y, c)):
                pidx = 4 * peer[0] + 2 * peer[1] + peer[2]
                src = ins[i].at[pidx] if scatter else ins[i]
                kw = dict(send_sem=send.at[7 * i + r], recv_sem=recv.at[7 * i + r], device_id=peer, device_id_type=MESH)
                out.append(pltpu.make_async_remote_copy(src_ref=src, dst_ref=outs[j].at[me, pl.ds(off, rows)], **kw))
                arrive.append(pltpu.make_async_remote_copy(src_ref=src, dst_ref=outs[j].at[pidx, pl.ds(off, rows)], **kw))
            own = ins[i].at[me] if scatter else ins[i]
            local.append(pltpu.make_async_copy(own, outs[j].at[me, pl.ds(off, rows)], loc.at[i]))
        return out, arrive, local

    def start(ins, outs, sems):
        out, _, local = copies(ins, outs, sems)
        for cp in local + out:
            cp.start()

    def finish(ins, outs, sems):
        out, arrive, local = copies(ins, outs, sems)
        for cp in arrive:
            cp.wait_recv()
        for cp in out:
            cp.wait_send()
        for cp in local:
            cp.wait()

    return _Carried(srcs, [jax.ShapeDtypeStruct((N_DEV,) + tuple(s), d) for s, d in out_shapes],
                    [pltpu.SemaphoreType.DMA((7 * n,)), pltpu.SemaphoreType.DMA((7 * n,)),
                     pltpu.SemaphoreType.DMA((n,))], start, finish)


def _run_exchange(carried, name):
    ci = len(carried.ins)

    def body(*refs):
        ins, outs, sems = refs[:ci], refs[ci:ci + len(carried.outs)], refs[ci + len(carried.outs):]
        carried.start(ins, outs, sems)
        carried.finish(ins, outs, sems)

    return list(pl.pallas_call(body, name=name, in_specs=[_ANY] * ci, out_specs=[_ANY] * len(carried.outs),
                               out_shape=carried.outs, scratch_shapes=carried.sems)(*carried.ins))


def _dg(a, b, dn):
    return lax.dot_general(a.astype(BF16), b.astype(BF16), dn, preferred_element_type=F32)


@jax.custom_vjp
def mm_nn(a, b):
    return _dg(a, b, _NN)


@jax.custom_vjp
def mm_nt(a, b):
    return _dg(a, b, _NT)


@jax.custom_vjp
def mm_tn(a, b):
    return _dg(a, b, _TN)


mm_nn.defvjp(lambda a, b: (_dg(a, b, _NN), (a, b)), lambda r, ct: (mm_nt(ct, r[1]), mm_tn(r[0], ct)))
mm_nt.defvjp(lambda a, b: (_dg(a, b, _NT), (a, b)), lambda r, ct: (mm_nn(ct, r[1]), mm_tn(ct, r[0])))
mm_tn.defvjp(lambda a, b: (_dg(a, b, _TN), (a, b)), lambda r, ct: (mm_nt(r[1], ct), mm_nn(r[0], ct)))


def _split3(x):
    p1 = x.astype(BF16)
    r1 = x - p1.astype(F32)
    p2 = r1.astype(BF16)
    r2 = r1 - p2.astype(F32)
    return p1, p2, r2.astype(BF16)


def _exact01(x, c, dn, const_left):
    acc = None
    for p in reversed(_split3(x)):
        t = (lax.dot_general(c, p, dn, preferred_element_type=F32) if const_left
             else lax.dot_general(p, c, dn, preferred_element_type=F32))
        acc = t if acc is None else acc + t
    return acc


def _make_cmm(dn, const_left, bwd_name):
    @jax.custom_vjp
    def f(x, c):
        return _exact01(x, c, dn, const_left)

    def fwd(x, c):
        return _exact01(x, c, dn, const_left), c

    def bwd(c, ct):
        return _CMM[bwd_name](ct, c), jnp.zeros_like(c)

    f.defvjp(fwd, bwd)
    return f


_CMM = {}
_CMM["xc"] = _make_cmm(_NN, False, "xct")
_CMM["xct"] = _make_cmm(_NT, False, "xc")
_CMM["cx"] = _make_cmm(_NN, True, "ctx")
_CMM["ctx"] = _make_cmm(_TN, True, "cx")


@jax.custom_vjp
def _silu(x):
    return x / (1.0 + jnp.exp(-x))


def _silu_fwd(x):
    return _silu(x), x


def _silu_bwd(x, ct):
    s = 1.0 / (1.0 + jnp.exp(-x))
    return (ct * (s * (1.0 + x * (1.0 - s))),)


_silu.defvjp(_silu_fwd, _silu_bwd)


def _log1p_pos(e):
    u = 1.0 + e
    d = u - 1.0
    return jnp.where(d == 0.0, e, jnp.log(u) * (e / jnp.where(d == 0.0, 1.0, d)))


@jax.custom_vjp
def _softplus(x):
    return jnp.maximum(x, 0.0) + _log1p_pos(jnp.exp(-jnp.abs(x)))


def _softplus_fwd(x):
    return _softplus(x), x


def _softplus_bwd(x, ct):
    return (ct / (1.0 + jnp.exp(-x)),)


_softplus.defvjp(_softplus_fwd, _softplus_bwd)


CONV_HALO = 8


def _make_shift(j):
    @jax.custom_vjp
    def f(ext):
        return pltpu.roll(ext, j, 0)[CONV_HALO:, :]

    def fwd(ext):
        return f(ext), None

    def bwd(_, ct):
        pad = jnp.concatenate([jnp.zeros((CONV_HALO, ct.shape[1]), ct.dtype), ct], axis=0)
        return (pltpu.roll(pad, CONV_HALO + CHUNK - j, 0),)

    f.defvjp(fwd, bwd)
    return f


_SHIFT = {j: _make_shift(j) for j in (1, 2, 3)}


@jax.custom_vjp
def _swap_halves(x):
    return pltpu.roll(x, HEAD_DIM, 1)


_swap_halves.defvjp(lambda x: (_swap_halves(x), None), lambda _, ct: (pltpu.roll(ct, HEAD_DIM, 1),))


def _rms_fwd(x, g):
    r = lax.rsqrt(jnp.mean(x * x, axis=-1, keepdims=True) + RMS_EPS)
    n = x * r
    return n * g, n, r


def _rms_bwd(dy, n, r, g):
    dn = dy * g
    dx = r * (dn - n * jnp.mean(dn * n, axis=-1, keepdims=True))
    dg = jnp.sum(dy * n, axis=0, keepdims=True)
    return dx, dg


def _one(cond):
    return jnp.where(cond, 1.0, 0.0)


def _pool_tile(xe, g, ws, b, scale, tile, tt):
    r = lax.rsqrt(jnp.mean(xe * xe, axis=-1, keepdims=True) + RMS_EPS)
    hn = xe * r * g
    row_e = lax.broadcasted_iota(jnp.int32, (tt + POOL_HALO, POOL_GROUP), 0)
    keep = _one(jnp.logical_or(row_e >= POOL_HALO, tile > 0))
    rr = lax.broadcasted_iota(jnp.int32, (tt, tt + POOL_HALO), 0)
    qq = lax.broadcasted_iota(jnp.int32, (tt, tt + POOL_HALO), 1)
    dd = qq - rr
    tpos = tile * tt + lax.broadcasted_iota(jnp.int32, (tt, POOL_GROUP), 0)
    outs = []
    for gi, w in enumerate(POOL_WINDOWS):
        hg = hn[:, gi * POOL_GROUP:(gi + 1) * POOL_GROUP] * keep
        band = _one(jnp.logical_and(dd >= POOL_HALO - w + 1, dd <= POOL_HALO)).astype(BF16)
        cnt = jnp.minimum(tpos + 1, w).astype(F32)
        pooled = _CMM["cx"](hg, band) / cnt
        mixed = pooled - hg[POOL_HALO:, :]
        outs.append(mm_nn(mixed, ws[gi]))
    out = (jnp.concatenate(outs, axis=1) + b) * scale
    return xe[POOL_HALO:, :] + out


def _pool_specs(tt, nt, rev):
    per = tt // POOL_HALO
    t_of = (lambda i: nt - 1 - i) if rev else (lambda i: i)
    main = pl.BlockSpec((tt, D_MODEL), lambda i: (t_of(i), 0))
    halo = pl.BlockSpec((POOL_HALO, D_MODEL), lambda i: (jnp.maximum(t_of(i) * per - 1, 0), 0))
    vec = pl.BlockSpec((1, D_MODEL), lambda i: (0, 0))
    wsp = pl.BlockSpec((N_DEV, 4 * POOL_SHARD, POOL_GROUP), lambda i: (0, 0, 0))
    return main, halo, vec, wsp


def _pool_weights(w_ref):
    return tuple(
        jnp.concatenate([w_ref[k, gi * POOL_SHARD:(gi + 1) * POOL_SHARD, :] for k in range(N_DEV)], axis=0).astype(F32)
        for gi in range(4))


def _pool_fwd(x, g, w, b, scale):
    t = x.shape[0]
    tt = min(t, 256)
    nt = t // tt
    main, halo, vec, wsp = _pool_specs(tt, nt, False)

    def body(xm_ref, xh_ref, g_ref, w_ref, b_ref, s_ref, o_ref):
        i = pl.program_id(0)
        xe = jnp.concatenate([xh_ref[...], xm_ref[...]], axis=0)
        o_ref[...] = _pool_tile(xe, g_ref[...], _pool_weights(w_ref), b_ref[...], s_ref[...], i, tt)

    return pl.pallas_call(
        body, name="pool_fwd", grid=(nt,),
        in_specs=[main, halo, vec, wsp, vec, vec], out_specs=main,
        out_shape=jax.ShapeDtypeStruct((t, D_MODEL), F32),
        compiler_params=_cp(("arbitrary",)),
    )(x, x, g, w, b, scale)


def _pool_bwd(x, dh, g, w, b, scale, carried=None):
    t = x.shape[0]
    tt = min(t, 256)
    nt = t // tt
    main, halo, vec, wsp = _pool_specs(tt, nt, True)

    def body(xm_ref, xh_ref, dh_ref, g_ref, w_ref, b_ref, s_ref,
             dx_ref, dw_ref, db_ref, ds_ref, dg_ref, carry, dw_acc):
        i = pl.program_id(0)
        tile = nt - 1 - i

        @pl.when(i == 0)
        def _():
            carry[...] = jnp.zeros_like(carry)
            dw_acc[...] = jnp.zeros_like(dw_acc)
            db_ref[...] = jnp.zeros_like(db_ref)
            ds_ref[...] = jnp.zeros_like(ds_ref)
            dg_ref[...] = jnp.zeros_like(dg_ref)

        xe = jnp.concatenate([xh_ref[...], xm_ref[...]], axis=0)
        _, vjp = jax.vjp(lambda a, gg, ww, bb, ss: _pool_tile(a, gg, ww, bb, ss, tile, tt),
                         xe, g_ref[...], _pool_weights(w_ref), b_ref[...], s_ref[...])
        dxe, dgv, dws, dbv, dsv = vjp(dh_ref[...])
        dx_ref[...] = dxe[POOL_HALO:, :]
        dx_ref[tt - POOL_HALO:tt, :] += carry[...]
        carry[...] = dxe[:POOL_HALO, :]
        for gi in range(4):
            dw_acc[gi] += dws[gi]
        db_ref[...] += dbv
        ds_ref[...] += dsv
        dg_ref[...] += dgv

        @pl.when(i == nt - 1)
        def _():
            for k in range(N_DEV):
                for gi in range(4):
                    dw_ref[k, gi * POOL_SHARD:(gi + 1) * POOL_SHARD, :] = dw_acc[gi, k * POOL_SHARD:(k + 1) * POOL_SHARD, :]

    return _pcall(
        body, name="pool_bwd", grid=(nt,),
        in_specs=[main, halo, main, vec, wsp, vec, vec],
        out_specs=[main, wsp, vec, vec, vec],
        out_shape=[jax.ShapeDtypeStruct((t, D_MODEL), F32),
                   jax.ShapeDtypeStruct((N_DEV, 4 * POOL_SHARD, POOL_GROUP), F32),
                   jax.ShapeDtypeStruct((1, D_MODEL), F32),
                   jax.ShapeDtypeStruct((1, D_MODEL), F32),
                   jax.ShapeDtypeStruct((1, D_MODEL), F32)],
        scratch_shapes=[pltpu.VMEM((POOL_HALO, D_MODEL), F32), pltpu.VMEM((4, POOL_GROUP, POOL_GROUP), F32)],
        sem=("arbitrary",), args=(x, x, dh, g, w, b, scale), carried=carried)


def _mlp_weight_specs():
    fb = D_FF // N_DEV
    return (pl.BlockSpec((None, D_MODEL, fb), lambda i, k: (k, 0, 0)),
            pl.BlockSpec((None, fb, D_MODEL), lambda i, k: (k, 0, 0)))


def _mlp_fwd(h, g, w1g, w2g, name, carried=None):
    t = h.shape[0]
    tt = min(t, MATMUL_TOKENS)
    nk, fb = N_DEV, D_FF // N_DEV
    w1_spec, w2_spec = _mlp_weight_specs()

    def body(h_ref, g_ref, w1_ref, w2_ref, o_ref, u_ref, hm_ref, hm_s, acc_s):
        k = pl.program_id(1)

        @pl.when(k == 0)
        def _():
            xv = h_ref[...]
            y, _, _ = _rms_fwd(xv, g_ref[...])
            hb = y.astype(BF16)
            hm_s[...] = hb
            hm_ref[...] = hb
            acc_s[...] = xv

        a = jnp.dot(hm_s[...], w1_ref[...], preferred_element_type=F32)
        u = jnp.maximum(a, 0.0)
        u_ref[...] = u.astype(BF16)
        acc_s[...] += jnp.dot((u * u).astype(BF16), w2_ref[...], preferred_element_type=F32)

        @pl.when(k == nk - 1)
        def _():
            o_ref[...] = acc_s[...]

    return _pcall(
        body, name=name, grid=(t // tt, nk),
        in_specs=[pl.BlockSpec((tt, D_MODEL), lambda i, k: (i, 0)),
                  pl.BlockSpec((1, D_MODEL), lambda i, k: (0, 0)),
                  w1_spec, w2_spec],
        out_specs=[pl.BlockSpec((tt, D_MODEL), lambda i, k: (i, 0)),
                   pl.BlockSpec((tt, fb), lambda i, k: (i, k)),
                   pl.BlockSpec((tt, D_MODEL), lambda i, k: (i, 0))],
        out_shape=[jax.ShapeDtypeStruct((t, D_MODEL), F32),
                   jax.ShapeDtypeStruct((t, nk * fb), BF16),
                   jax.ShapeDtypeStruct((t, D_MODEL), BF16)],
        scratch_shapes=[pltpu.VMEM((tt, D_MODEL), BF16), pltpu.VMEM((tt, D_MODEL), F32)],
        sem=("arbitrary", "arbitrary"), args=(h, g, w1g, w2g), carried=carried)


def _mlp_bwd(dh, dhb, h, g, u, w1g, w2g, name, carried=None):
    t = h.shape[0]
    tt = min(t, MATMUL_TOKENS)
    nk, fb = N_DEV, D_FF // N_DEV
    w1_spec, w2_spec = _mlp_weight_specs()

    def body(dh_ref, dhb_ref, h_ref, g_ref, u_ref, w1_ref, w2_ref,
             dhin_ref, dhinb_ref, da_ref, dg_ref, acc_s):
        i = pl.program_id(0)
        k = pl.program_id(1)

        @pl.when(jnp.logical_and(i == 0, k == 0))
        def _():
            dg_ref[...] = jnp.zeros_like(dg_ref)

        @pl.when(k == 0)
        def _():
            acc_s[...] = jnp.zeros_like(acc_s)

        dv = lax.dot_general(dhb_ref[...], w2_ref[...], _NT, preferred_element_type=F32)
        dab = (dv * (2.0 * u_ref[...].astype(F32))).astype(BF16)
        da_ref[...] = dab
        acc_s[...] += lax.dot_general(dab, w1_ref[...], _NT, preferred_element_type=F32)

        @pl.when(k == nk - 1)
        def _():
            gv = g_ref[...]
            _, n, r = _rms_fwd(h_ref[...], gv)
            dx, dg = _rms_bwd(acc_s[...], n, r, gv)
            dhin = dh_ref[...] + dx
            dhin_ref[...] = dhin
            dhinb_ref[...] = dhin.astype(BF16)
            dg_ref[...] += dg

    tile = pl.BlockSpec((tt, D_MODEL), lambda i, k: (i, 0))
    return _pcall(
        body, name=name, grid=(t // tt, nk),
        in_specs=[tile, tile, tile, pl.BlockSpec((1, D_MODEL), lambda i, k: (0, 0)),
                  pl.BlockSpec((tt, fb), lambda i, k: (i, k)), w1_spec, w2_spec],
        out_specs=[tile, tile, pl.BlockSpec((tt, fb), lambda i, k: (i, k)),
                   pl.BlockSpec((1, D_MODEL), lambda i, k: (0, 0))],
        out_shape=[jax.ShapeDtypeStruct((t, D_MODEL), F32),
                   jax.ShapeDtypeStruct((t, D_MODEL), BF16),
                   jax.ShapeDtypeStruct((t, nk * fb), BF16),
                   jax.ShapeDtypeStruct((1, D_MODEL), F32)],
        scratch_shapes=[pltpu.VMEM((tt, D_MODEL), F32)],
        sem=("arbitrary", "arbitrary"), args=(dh, dhb, h, g, u, w1g, w2g), carried=carried)


def _matmul_tn(a, b, name, square_a=False, col_blocked=False, carried=None):
    t, k1 = a.shape
    k2 = b.shape[1]
    b1 = min(k1, 1024)
    tt = min(t, MATMUL_TOKENS)
    nt = t // tt

    def body(a_ref, b_ref, o_ref, acc):
        s = pl.program_id(2)

        @pl.when(s == 0)
        def _():
            acc[...] = jnp.zeros_like(acc)

        av = a_ref[...]
        if square_a:
            af = av.astype(F32)
            av = (af * af).astype(BF16)
        acc[...] += lax.dot_general(av, b_ref[...], _TN, preferred_element_type=F32)

        @pl.when(s == nt - 1)
        def _():
            o_ref[...] = acc[...].astype(o_ref.dtype)

    if col_blocked:
        out_shape = jax.ShapeDtypeStruct((k2 // COL_BLK, k1, COL_BLK), BF16)
        out_spec = pl.BlockSpec((None, b1, COL_BLK), lambda i, j, s: (j, i, 0))
    else:
        out_shape = jax.ShapeDtypeStruct((k1, k2), BF16)
        out_spec = pl.BlockSpec((b1, COL_BLK), lambda i, j, s: (i, j))
    outs, landed = _pcall(
        body, name=name, grid=(k1 // b1, k2 // COL_BLK, nt),
        in_specs=[pl.BlockSpec((tt, b1), lambda i, j, s: (s, i)),
                  pl.BlockSpec((tt, COL_BLK), lambda i, j, s: (s, j))],
        out_specs=[out_spec], out_shape=[out_shape],
        scratch_shapes=[pltpu.VMEM((b1, COL_BLK), F32)],
        sem=("arbitrary", "arbitrary", "arbitrary"), args=(a, b), carried=carried)
    return (outs[0], landed) if carried is not None else outs[0]


def _norm_matmul(h, g, w, carried=None):
    t = h.shape[0]
    tt = min(t, MATMUL_TOKENS)
    n = w.shape[1]

    def body(h_ref, g_ref, w_ref, o_ref, hn_ref, hn_s):
        @pl.when(pl.program_id(1) == 0)
        def _():
            y, _, _ = _rms_fwd(h_ref[...], g_ref[...])
            hb = y.astype(BF16)
            hn_s[...] = hb
            hn_ref[...] = hb

        o_ref[...] = jnp.dot(hn_s[...], w_ref[...], preferred_element_type=F32)

    return _pcall(
        body, name="ssm_in_proj", grid=(t // tt, n // COL_BLK),
        in_specs=[pl.BlockSpec((tt, D_MODEL), lambda i, j: (i, 0)),
                  pl.BlockSpec((1, D_MODEL), lambda i, j: (0, 0)),
                  pl.BlockSpec((D_MODEL, COL_BLK), lambda i, j: (0, j))],
        out_specs=[pl.BlockSpec((tt, COL_BLK), lambda i, j: (i, j)),
                   pl.BlockSpec((tt, D_MODEL), lambda i, j: (i, 0))],
        out_shape=[jax.ShapeDtypeStruct((t, n), F32), jax.ShapeDtypeStruct((t, D_MODEL), BF16)],
        scratch_shapes=[pltpu.VMEM((tt, D_MODEL), BF16)],
        sem=("arbitrary", "arbitrary"), args=(h, g, w), carried=carried)


def _in_proj_bwd(dzx, w, h, g, dh_next):
    t = h.shape[0]
    tt = min(t, MATMUL_TOKENS)
    n = w.shape[1]
    nj = n // COL_BLK

    def body(dz_ref, w_ref, h_ref, g_ref, dn_ref, dh_ref, dhb_ref, dg_ref, acc):
        i = pl.program_id(0)
        j = pl.program_id(1)

        @pl.when(jnp.logical_and(i == 0, j == 0))
        def _():
            dg_ref[...] = jnp.zeros_like(dg_ref)

        @pl.when(j == 0)
        def _():
            acc[...] = jnp.zeros_like(acc)

        acc[...] += lax.dot_general(dz_ref[...], w_ref[...], _NT, preferred_element_type=F32)

        @pl.when(j == nj - 1)
        def _():
            gv = g_ref[...]
            _, nn, r = _rms_fwd(h_ref[...], gv)
            dx, dg = _rms_bwd(acc[...], nn, r, gv)
            dh = dn_ref[...] + dx
            dh_ref[...] = dh
            dhb_ref[...] = dh.astype(BF16)
            dg_ref[...] += dg

    tile = pl.BlockSpec((tt, D_MODEL), lambda i, j: (i, 0))
    return pl.pallas_call(
        body, name="ssm_in_proj_bwd", grid=(t // tt, nj),
        in_specs=[pl.BlockSpec((tt, COL_BLK), lambda i, j: (i, j)),
                  pl.BlockSpec((D_MODEL, COL_BLK), lambda i, j: (0, j)),
                  tile, pl.BlockSpec((1, D_MODEL), lambda i, j: (0, 0)), tile],
        out_specs=[tile, tile, pl.BlockSpec((1, D_MODEL), lambda i, j: (0, 0))],
        out_shape=[jax.ShapeDtypeStruct((t, D_MODEL), F32), jax.ShapeDtypeStruct((t, D_MODEL), BF16),
                   jax.ShapeDtypeStruct((1, D_MODEL), F32)],
        scratch_shapes=[pltpu.VMEM((tt, D_MODEL), F32)],
        compiler_params=_cp(("arbitrary", "arbitrary")),
    )(dzx, w, h, g, dh_next)


def _ssd_consts():
    lane = lax.broadcasted_iota(jnp.int32, (CHUNK, CHUNK), 1)
    row = lax.broadcasted_iota(jnp.int32, (CHUNK, CHUNK), 0)
    causal = lane <= row
    tri = _one(causal).astype(BF16)
    er = lax.broadcasted_iota(jnp.int32, (CHUNK, GROUP_X), 0)
    ec = lax.broadcasted_iota(jnp.int32, (CHUNK, GROUP_X), 1)
    expand = _one(jnp.right_shift(ec, 6) == er).astype(BF16)
    return dict(causal=causal, tri=tri, expand=expand, lo=lane < HEAD_DIM)


def _conv_silu(cur, prev, w, b):
    ext = jnp.concatenate([prev, cur], axis=0)
    acc = cur * w[3] + b
    for j in (1, 2, 3):
        acc = acc + _SHIFT[j](ext) * w[3 - j]
    return _silu(acc)


def _ssd_chunk(raw, rawp, ht, cw, cb_, dtb, alog, dsk, k):
    act = _conv_silu(raw[:, :GROUP_CONV], rawp[:, :GROUP_CONV], cw, cb_)
    xs = act[:, :GROUP_X]
    bm = act[:, GROUP_X:GROUP_X + D_STATE]
    cm = act[:, GROUP_X + D_STATE:]
    dt = _softplus(raw[:, GROUP_CONV:] + dtb)
    a = -jnp.exp(alog)
    xc = _CMM["xc"]

    def lanes(rowv):
        return jnp.sum(xc(jnp.broadcast_to(rowv, (16, CHUNK)), k["expand"]), axis=0, keepdims=True) * (1.0 / 16.0)

    dt_e = xc(dt, k["expand"])
    adt_e = dt_e * lanes(a)
    acs_e = _CMM["cx"](adt_e, k["tri"])
    tot_e = jnp.sum(adt_e, axis=0, keepdims=True)
    gmat = mm_nt(cm, bm)
    xdt = xs * dt_e
    ys = []
    for j in range(HEADS_PER_GROUP // 2):
        pair = acs_e[:, j * CHUNK:(j + 1) * CHUNK]
        swapped = _swap_halves(pair)
        ms = []
        for cb in (jnp.where(k["lo"], pair, swapped), jnp.where(k["lo"], swapped, pair)):
            seg = cb - cb.T
            ms.append(gmat * jnp.exp(jnp.where(k["causal"], seg, -jnp.inf)))
        xp = xdt[:, j * CHUNK:(j + 1) * CHUNK]
        rhs = jnp.concatenate([jnp.where(k["lo"], xp, 0.0), jnp.where(k["lo"], 0.0, xp)], axis=0)
        ys.append(mm_nn(jnp.concatenate(ms, axis=1), rhs))
    y_diag = jnp.concatenate(ys, axis=1)
    y_off = jnp.exp(acs_e) * mm_nn(cm, ht)
    h_new = jnp.exp(tot_e) * ht + mm_tn(bm, xdt * jnp.exp(tot_e - acs_e))
    return y_diag + y_off + lanes(dsk) * xs, h_new


def _ssd_in_specs(nc, rev):
    c_of = (lambda c: nc - 1 - c) if rev else (lambda c: c)
    per = CHUNK // CONV_HALO
    zx = [pl.BlockSpec((CHUNK, GROUP_COLS), lambda g, c: (c_of(c), g)),
          pl.BlockSpec((CONV_HALO, GROUP_COLS), lambda g, c: (jnp.maximum(c_of(c) * per - 1, 0), g))]
    conv = [pl.BlockSpec((4, GROUP_CONV), lambda g, c: (0, g)), pl.BlockSpec((1, GROUP_CONV), lambda g, c: (0, g))]
    head = [pl.BlockSpec((None, 1, 128), lambda g, c: (g, 0, 0))] * 3
    return zx + conv + head, c_of


def _load_chunk_args(refs, has_prev):
    raw, rawp, cw, cb_, dtb, alog, dsk = refs
    return (raw[...], rawp[...] * has_prev, tuple(cw[pl.ds(i, 1), :] for i in range(4)), cb_[...],
            dtb[...], alog[...], dsk[...])


def _ssd_fwd(zx, conv_w, conv_b, dtb, alog, dsk, carried=None):
    t = zx.shape[0]
    nc = t // CHUNK
    in_specs, _ = _ssd_in_specs(nc, False)

    def body(*refs):
        ins, (y_ref, hs_ref, ht) = refs[:7], refs[7:]
        c = pl.program_id(1)

        @pl.when(c == 0)
        def _():
            ht[...] = jnp.zeros_like(ht)

        a = _load_chunk_args(ins, _one(c > 0))
        h_in = ht[...]
        y, h_new = _ssd_chunk(*a[:2], h_in, *a[2:], _ssd_consts())
        y_ref[...] = y
        hs_ref[...] = h_in
        ht[...] = h_new

    return _pcall(
        body, name="ssd_fwd", grid=(N_GROUPS, nc),
        in_specs=in_specs,
        out_specs=[pl.BlockSpec((CHUNK, GROUP_X), lambda g, c: (c, g)),
                   pl.BlockSpec((None, None, D_STATE, GROUP_X), lambda g, c: (g, c, 0, 0))],
        out_shape=[jax.ShapeDtypeStruct((t, D_INNER), F32),
                   jax.ShapeDtypeStruct((N_GROUPS, nc, D_STATE, GROUP_X), F32)],
        scratch_shapes=[pltpu.VMEM((D_STATE, GROUP_X), F32)],
        sem=("arbitrary", "arbitrary"), args=(zx, zx, conv_w, conv_b, dtb, alog, dsk), carried=carried)


def _ssd_bwd(zx, conv_w, conv_b, dtb, alog, dsk, hs, dy, dzx, carried=None):
    t = zx.shape[0]
    nc = t // CHUNK
    in_specs, c_of = _ssd_in_specs(nc, True)
    n_in = 10

    def body(*refs):
        ins, hs_ref, dy_ref = refs[:7], refs[7], refs[8]
        (draw_ref, dcw, dcb, ddtb, dalog, ddsk, dht, carry) = refs[n_in:]
        cc = pl.program_id(1)
        accs = (dcw, dcb, ddtb, dalog, ddsk)

        @pl.when(cc == 0)
        def _():
            for r in (dht, carry) + accs:
                r[...] = jnp.zeros_like(r)

        has_prev = _one(c_of(cc) > 0)
        a = _load_chunk_args(ins, has_prev)
        k = _ssd_consts()
        fn = lambda *args: _ssd_chunk(*args, k)
        _, vjp = jax.vjp(fn, *a[:2], hs_ref[...], *a[2:])
        graw, grawp, ght, gcw, gcb, gdtb, galog, gdsk = vjp((dy_ref[...], dht[...]))
        tail = jnp.concatenate([jnp.zeros((CHUNK - CONV_HALO, GROUP_COLS), F32), carry[...]], axis=0)
        draw_ref[...] = (graw + tail).astype(BF16)
        carry[...] = grawp * has_prev
        dht[...] = ght
        for i in range(4):
            dcw[pl.ds(i, 1), :] += gcw[i]
        for ref, val in ((dcb, gcb), (ddtb, gdtb), (dalog, galog), (ddsk, gdsk)):
            ref[...] += val

    head_out = pl.BlockSpec((None, 1, 128), lambda g, c: (g, 0, 0))
    sds = jax.ShapeDtypeStruct
    return _pcall(
        body, name="ssd_bwd", grid=(N_GROUPS, nc),
        in_specs=in_specs + [
            pl.BlockSpec((None, None, D_STATE, GROUP_X), lambda g, c: (g, c_of(c), 0, 0)),
            pl.BlockSpec((CHUNK, GROUP_X), lambda g, c: (c_of(c), g)),
            _ANY],
        out_specs=[pl.BlockSpec((CHUNK, GROUP_COLS), lambda g, c: (c_of(c), g)),
                   pl.BlockSpec((4, GROUP_CONV), lambda g, c: (0, g)),
                   pl.BlockSpec((1, GROUP_CONV), lambda g, c: (0, g)),
                   head_out, head_out, head_out],
        out_shape=[sds((t, ZX_COLS), BF16), sds((4, N_GROUPS * GROUP_CONV), F32), sds((1, N_GROUPS * GROUP_CONV), F32),
                   sds((N_GROUPS, 1, 128), F32), sds((N_GROUPS, 1, 128), F32), sds((N_GROUPS, 1, 128), F32)],
        scratch_shapes=[pltpu.VMEM((D_STATE, GROUP_X), F32), pltpu.VMEM((CONV_HALO, GROUP_COLS), F32)],
        sem=("arbitrary", "arbitrary"), args=(zx, zx, conv_w, conv_b, dtb, alog, dsk, hs, dy, dzx),
        aliases={9: 0}, carried=carried)


def _gate_norm(y, zs, ng):
    outs = []
    for k in range(N_GROUPS):
        s = y[:, k * GROUP_X:(k + 1) * GROUP_X] * _silu(zs[k])
        outs.append(s * lax.rsqrt(jnp.mean(s * s, axis=-1, keepdims=True) + RMS_EPS))
    return jnp.concatenate(outs, axis=1) * ng


def _z_specs(tt):
    first = Z_OFF // GROUP_X
    return [pl.BlockSpec((tt, GROUP_X), functools.partial(lambda k, i: (i, first + k), k)) for k in range(N_GROUPS)]


def _ssm_out_fwd(y, zx, ng, w_out, h):
    t = h.shape[0]
    tt = min(t, 256)

    def body(y_ref, z0, z1, z2, z3, ng_ref, w_ref, h_ref, o_ref):
        yn = _gate_norm(y_ref[...], (z0[...], z1[...], z2[...], z3[...]), ng_ref[...])
        o_ref[...] = h_ref[...] + jnp.dot(yn.astype(BF16), w_ref[...], preferred_element_type=F32)

    return pl.pallas_call(
        body, name="ssm_out_fwd", grid=(t // tt,),
        in_specs=[pl.BlockSpec((tt, D_INNER), lambda i: (i, 0))] + _z_specs(tt) + [
            pl.BlockSpec((1, D_INNER), lambda i: (0, 0)),
            pl.BlockSpec((D_INNER, D_MODEL), lambda i: (0, 0)),
            pl.BlockSpec((tt, D_MODEL), lambda i: (i, 0))],
        out_specs=pl.BlockSpec((tt, D_MODEL), lambda i: (i, 0)),
        out_shape=jax.ShapeDtypeStruct((t, D_MODEL), F32),
        compiler_params=_cp(("arbitrary",)),
    )(y, zx, zx, zx, zx, ng, w_out, h)


def _gate_norm_group(y, z, ng):
    s = y * _silu(z)
    return s * lax.rsqrt(jnp.mean(s * s, axis=-1, keepdims=True) + RMS_EPS) * ng


def _ssm_out_bwd(dhb, y, zx, ng, w_out):
    t = dhb.shape[0]
    tt = min(t, 512)
    first = Z_OFF // GROUP_X

    def body(dh_ref, y_ref, z_ref, ng_ref, w_ref, dy_ref, dzx_ref, yn_ref, dng_ref):
        @pl.when(pl.program_id(1) == 0)
        def _():
            dng_ref[...] = jnp.zeros_like(dng_ref)

        dyn = lax.dot_general(dh_ref[...], w_ref[...], _NT, preferred_element_type=F32)
        yn, vjp = jax.vjp(_gate_norm_group, y_ref[...], z_ref[...], ng_ref[...])
        dy, dz, dng = vjp(dyn)
        dy_ref[...] = dy
        dzx_ref[...] = dz.astype(BF16)
        yn_ref[...] = yn.astype(BF16)
        dng_ref[...] += dng

    grp = pl.BlockSpec((tt, GROUP_X), lambda k, i: (i, k))
    zgrp = pl.BlockSpec((tt, GROUP_X), lambda k, i: (i, first + k))
    gain = pl.BlockSpec((1, GROUP_X), lambda k, i: (0, k))
    return pl.pallas_call(
        body, name="ssm_out_bwd", grid=(N_GROUPS, t // tt),
        in_specs=[pl.BlockSpec((tt, D_MODEL), lambda k, i: (i, 0)), grp, zgrp, gain,
                  pl.BlockSpec((GROUP_X, D_MODEL), lambda k, i: (k, 0))],
        out_specs=[grp, zgrp, grp, gain],
        out_shape=[jax.ShapeDtypeStruct((t, D_INNER), F32), jax.ShapeDtypeStruct((t, ZX_COLS), BF16),
                   jax.ShapeDtypeStruct((t, D_INNER), BF16), jax.ShapeDtypeStruct((1, D_INNER), F32)],
        compiler_params=_cp(("arbitrary", "arbitrary")),
    )(dhb, y, zx, ng, w_out)


def _final(h, g, tgt):
    t = h.shape[0]
    tt = min(t, 512)
    nt = t // tt

    def body(h_ref, g_ref, t_ref, dh_ref, dhb_ref, loss_ref, dg_ref, lacc):
        i = pl.program_id(0)

        @pl.when(i == 0)
        def _():
            dg_ref[...] = jnp.zeros_like(dg_ref)
            lacc[...] = jnp.zeros_like(lacc)

        gv = g_ref[...]
        y, n, r = _rms_fwd(h_ref[...], gv)
        err = y - t_ref[...]
        lacc[...] += jnp.sum(err * err, axis=0, keepdims=True)
        dx, dg = _rms_bwd(err * (1.0 / D_MODEL), n, r, gv)
        dh_ref[...] = dx
        dhb_ref[...] = dx.astype(BF16)
        dg_ref[...] += dg

        @pl.when(i == nt - 1)
        def _():
            loss_ref[...] = jnp.zeros_like(loss_ref) + (0.5 / D_MODEL) * jnp.sum(lacc[...])

    tile = pl.BlockSpec((tt, D_MODEL), lambda i: (i, 0))
    vec = pl.BlockSpec((1, D_MODEL), lambda i: (0, 0))
    return pl.pallas_call(
        body, name="final_loss", grid=(nt,),
        in_specs=[tile, vec, tile],
        out_specs=[tile, tile, pl.BlockSpec((1, 128), lambda i: (0, 0)), vec],
        out_shape=[jax.ShapeDtypeStruct((t, D_MODEL), F32), jax.ShapeDtypeStruct((t, D_MODEL), BF16),
                   jax.ShapeDtypeStruct((1, 128), F32), jax.ShapeDtypeStruct((1, D_MODEL), F32)],
        scratch_shapes=[pltpu.VMEM((1, D_MODEL), F32)],
        compiler_params=_cp(("arbitrary",)),
    )(h, g, tgt)


def _chips(x, y):
    return [(1 - x, y), (x, 1 - y), (1 - x, 1 - y)]


def _gather(shards, layout, out_shapes, name):
    n, m = len(shards), len(out_shapes)

    def body(*refs):
        ins, outs = refs[:n], refs[n:n + m]
        send_sems, recv_sems, local_sems = refs[n + m:]
        x, y, c = _place()
        me, sibling = (x, y, c), (x, y, 1 - c)
        chips = _chips(x, y)

        def win(i, place):
            j, off = layout[i]
            return outs[j].at[4 * place[0] + 2 * place[1] + place[2], pl.ds(off, shards[i].shape[0])]

        def copy(i, k, block, to, src=None):
            return pltpu.make_async_remote_copy(
                src_ref=win(i, block) if src is None else src, dst_ref=win(i, block),
                send_sem=send_sems.at[7 * i + k], recv_sem=recv_sems.at[7 * i + k], device_id=to, device_id_type=MESH)

        mine = [pltpu.make_async_copy(ins[i], win(i, me), local_sems.at[i]) for i in range(n)]
        first = [[copy(i, 0, me, sibling, src=ins[i])] +
                 [copy(i, 1 + j, me, (*chip, c), src=ins[i]) for j, chip in enumerate(chips)] for i in range(n)]
        passed = [[copy(i, 4 + j, (*chip, c), sibling) for j, chip in enumerate(chips)] for i in range(n)]
        for i in range(n):
            mine[i].start()
            for cp in first[i]:
                cp.start()
        for i in range(n):
            for j, chip in enumerate(chips):
                copy(i, 1 + j, (*chip, c), me).wait_recv()
                passed[i][j].start()
        for i in range(n):
            copy(i, 0, sibling, me).wait_recv()
            for j, chip in enumerate(chips):
                copy(i, 4 + j, (*chip, 1 - c), me).wait_recv()
        for i in range(n):
            for cp in first[i] + passed[i]:
                cp.wait_send()
            mine[i].wait()

    return pl.pallas_call(
        body, name=name,
        out_shape=[jax.ShapeDtypeStruct((N_DEV,) + tuple(s), d) for s, d in out_shapes],
        in_specs=[_ANY] * n, out_specs=[_ANY] * m,
        scratch_shapes=[pltpu.SemaphoreType.DMA((7 * n,)), pltpu.SemaphoreType.DMA((7 * n,)),
                        pltpu.SemaphoreType.DMA((n,))],
    )(*shards)


def _all_reduce_small(sp):
    rows, n = sp.shape

    def body(x_ref, o_ref, land, send_sems, recv_sems):
        x, y, c = _place()
        me = 4 * x + 2 * y + c
        land[me] = x_ref[...]
        cps = []
        for rel in range(1, N_DEV):
            dx, dy, dc = (rel >> 2) & 1, (rel >> 1) & 1, rel & 1
            px = x + dx - 2 * x * dx
            py = y + dy - 2 * y * dy
            pc = c + dc - 2 * c * dc
            peer = 4 * px + 2 * py + pc
            cps.append((pltpu.make_async_remote_copy(
                src_ref=x_ref, dst_ref=land.at[me], send_sem=send_sems.at[rel - 1], recv_sem=recv_sems.at[rel - 1],
                device_id=(px, py, pc), device_id_type=MESH),
                pltpu.make_async_remote_copy(
                src_ref=x_ref, dst_ref=land.at[peer], send_sem=send_sems.at[rel - 1], recv_sem=recv_sems.at[rel - 1],
                device_id=(px, py, pc), device_id_type=MESH)))
        for cp, _ in cps:
            cp.start()
        for _, arr in cps:
            arr.wait_recv()
        for cp, _ in cps:
            cp.wait_send()
        acc = land[0]
        for k in range(1, N_DEV):
            acc = acc + land[k]
        o_ref[...] = acc

    vm = pl.BlockSpec(memory_space=pltpu.VMEM)
    return pl.pallas_call(
        body, name="all_reduce_small",
        out_shape=jax.ShapeDtypeStruct((rows, n), F32),
        in_specs=[vm], out_specs=vm,
        scratch_shapes=[pltpu.VMEM((N_DEV, rows, n), F32),
                        pltpu.SemaphoreType.DMA((N_DEV - 1,)), pltpu.SemaphoreType.DMA((N_DEV - 1,))],
    )(sp)


def _adamw_math(wv, gv, mv, vv):
    m2 = ADAM_B1 * mv + (1.0 - ADAM_B1) * gv
    v2 = ADAM_B2 * vv + (1.0 - ADAM_B2) * (gv * gv)
    m_hat = m2 / (1.0 - ADAM_B1 ** ADAM_STEP)
    v_hat = v2 / (1.0 - ADAM_B2 ** ADAM_STEP)
    return -ADAM_LR * (m_hat / (jnp.sqrt(v_hat) + ADAM_EPS) + ADAM_WD * wv), m2, v2


def _adamw(w, g, m, v, name):
    rows, cols = w.shape
    br = rows if rows <= 256 else 256

    def body(w_ref, g_ref, m_ref, v_ref, d_ref, m2_ref, v2_ref):
        d_ref[...], m2_ref[...], v2_ref[...] = _adamw_math(w_ref[...], g_ref[...], m_ref[...], v_ref[...])

    spec = pl.BlockSpec((br, cols), lambda i: (i, 0))
    out = jax.ShapeDtypeStruct((rows, cols), F32)
    return pl.pallas_call(
        body, name=name, grid=(rows // br,),
        in_specs=[spec] * 4, out_specs=[spec] * 3, out_shape=[out] * 3,
        compiler_params=_cp(("arbitrary",)),
    )(w, g, m, v)


def _adamw_reduced(w, land, m, v, name):
    rows, cols = w.shape
    br = rows if rows <= 256 else 256
    nl = land.shape[0]

    def body(w_ref, l_ref, m_ref, v_ref, g_ref, d_ref, m2_ref, v2_ref):
        gv = l_ref[0].astype(F32)
        for q in range(1, nl):
            gv = gv + l_ref[q].astype(F32)
        g_ref[...] = gv
        d_ref[...], m2_ref[...], v2_ref[...] = _adamw_math(w_ref[...], gv, m_ref[...], v_ref[...])

    spec = pl.BlockSpec((br, cols), lambda i: (i, 0))
    out = jax.ShapeDtypeStruct((rows, cols), F32)
    return pl.pallas_call(
        body, name=name, grid=(rows // br,),
        in_specs=[spec, pl.BlockSpec((nl, br, cols), lambda i: (0, i, 0)), spec, spec],
        out_specs=[spec] * 4, out_shape=[out] * 4,
        compiler_params=_cp(("arbitrary",)),
    )(w, land, m, v)


def _zx_source_col(col):
    blk = jnp.right_shift(col, 7)
    lane = jnp.bitwise_and(col, 127)
    per = GROUP_COLS // 128
    grp = jnp.where(blk >= per, 1, 0) + jnp.where(blk >= 2 * per, 1, 0) + jnp.where(blk >= 3 * per, 1, 0)
    o = blk - per * grp
    x_col = D_INNER + GROUP_X * grp + 128 * o + lane
    b_col = 2 * D_INNER + D_STATE * grp + lane
    c_col = 2 * D_INNER + N_GROUPS * D_STATE + D_STATE * grp + lane
    dt_col = jnp.where(lane < HEADS_PER_GROUP, D_INNER + CONV_DIM + HEADS_PER_GROUP * grp + lane, -1)
    src = jnp.where(o < 4, x_col, jnp.where(o == 4, b_col, jnp.where(o == 5, c_col, dt_col)))
    return jnp.where(col >= Z_OFF, col - Z_OFF, src)


def _zx_source_col_py(col):
    if col >= Z_OFF:
        return col - Z_OFF
    grp, o = divmod(col, GROUP_COLS)
    if o < GROUP_X:
        return D_INNER + GROUP_X * grp + o
    if o < GROUP_X + D_STATE:
        return 2 * D_INNER + D_STATE * grp + (o - GROUP_X)
    if o < GROUP_CONV:
        return 2 * D_INNER + N_GROUPS * D_STATE + D_STATE * grp + (o - GROUP_X - D_STATE)
    h = o - GROUP_CONV
    return D_INNER + CONV_DIM + HEADS_PER_GROUP * grp + h if h < HEADS_PER_GROUP else -1


def _overlap_tables():
    nblk = ZX_COLS // COL_BLK
    src = [_zx_source_col_py(c) for c in range(ZX_COLS)]
    fwd = [sorted({s // W_IN_SHARD for s in src[COL_BLK * j:COL_BLK * (j + 1)] if s >= 0}) for j in range(nblk)]
    dst = {s: c for c, s in enumerate(src) if s >= 0}
    bwd = [sorted({dst[s] // COL_BLK for s in range(W_IN_SHARD * k, W_IN_SHARD * (k + 1))}) for k in range(N_DEV)]

    def flat(rows):
        width = max(len(r) for r in rows)
        idx = [r + [r[-1]] * (width - len(r)) for r in rows]
        val = [[1] * len(r) + [0] * (width - len(r)) for r in rows]
        return (jnp.asarray(sum(idx, []), jnp.int32), jnp.asarray(sum(val, []), jnp.int32), width)

    return flat(fwd), flat(bwd)


def _w_in_to_zx(w_in_g):
    (tab, val, width), _ = _overlap_tables()
    nblk = ZX_COLS // COL_BLK

    def body(tab_ref, val_ref, w_ref, o_ref, acc):
        j = pl.program_id(0)
        s = pl.program_id(1)

        @pl.when(s == 0)
        def _():
            acc[...] = jnp.zeros_like(acc)

        @pl.when(val_ref[j * width + s] == 1)
        def _():
            k = tab_ref[j * width + s]
            col = COL_BLK * j + lax.broadcasted_iota(jnp.int32, (W_IN_SHARD, COL_BLK), 1)
            row = W_IN_SHARD * k + lax.broadcasted_iota(jnp.int32, (W_IN_SHARD, COL_BLK), 0)
            place = _one(_zx_source_col(col) == row).astype(BF16)
            acc[...] += jnp.dot(w_ref[...], place, preferred_element_type=F32)

        @pl.when(s == width - 1)
        def _():
            o_ref[...] = acc[...].astype(BF16)

    return pl.pallas_call(
        body, name="w_in_to_zx",
        grid_spec=pltpu.PrefetchScalarGridSpec(
            num_scalar_prefetch=2, grid=(nblk, width),
            in_specs=[pl.BlockSpec((None, D_MODEL, W_IN_SHARD), lambda j, s, tab, val: (tab[j * width + s], 0, 0))],
            out_specs=pl.BlockSpec((D_MODEL, COL_BLK), lambda j, s, tab, val: (0, j)),
            scratch_shapes=[pltpu.VMEM((D_MODEL, COL_BLK), F32)]),
        out_shape=jax.ShapeDtypeStruct((D_MODEL, ZX_COLS), BF16),
        compiler_params=_cp(("arbitrary", "arbitrary")),
    )(tab, val, w_in_g)


def _zx_to_w_in(d_wzx):
    _, (tab, val, width) = _overlap_tables()

    def body(tab_ref, val_ref, d_ref, o_ref, acc):
        k = pl.program_id(0)
        s = pl.program_id(1)

        @pl.when(s == 0)
        def _():
            acc[...] = jnp.zeros_like(acc)

        @pl.when(val_ref[k * width + s] == 1)
        def _():
            j = tab_ref[k * width + s]
            col = COL_BLK * j + lax.broadcasted_iota(jnp.int32, (COL_BLK, W_IN_SHARD), 0)
            row = W_IN_SHARD * k + lax.broadcasted_iota(jnp.int32, (COL_BLK, W_IN_SHARD), 1)
            place = _one(_zx_source_col(col) == row).astype(BF16)
            acc[...] += jnp.dot(d_ref[...], place, preferred_element_type=F32)

        @pl.when(s == width - 1)
        def _():
            o_ref[...] = acc[...].astype(BF16)

    return pl.pallas_call(
        body, name="zx_to_w_in",
        grid_spec=pltpu.PrefetchScalarGridSpec(
            num_scalar_prefetch=2, grid=(N_DEV, width),
            in_specs=[pl.BlockSpec((D_MODEL, COL_BLK), lambda k, s, tab, val: (0, tab[k * width + s]))],
            out_specs=pl.BlockSpec((None, D_MODEL, W_IN_SHARD), lambda k, s, tab, val: (k, 0, 0)),
            scratch_shapes=[pltpu.VMEM((D_MODEL, W_IN_SHARD), F32)]),
        out_shape=jax.ShapeDtypeStruct((N_DEV, D_MODEL, W_IN_SHARD), BF16),
        compiler_params=_cp(("arbitrary", "arbitrary")),
    )(tab, val, d_wzx)


def _group_conv_cols(a):
    rows = a.shape[0]
    x = a[:, :D_INNER].reshape(rows, N_GROUPS, GROUP_X)
    b = a[:, D_INNER:D_INNER + N_GROUPS * D_STATE].reshape(rows, N_GROUPS, D_STATE)
    c = a[:, D_INNER + N_GROUPS * D_STATE:].reshape(rows, N_GROUPS, D_STATE)
    return jnp.concatenate([x, b, c], axis=2).reshape(rows, N_GROUPS * GROUP_CONV)


def _ungroup_conv_cols(a):
    rows = a.shape[0]
    a3 = a.reshape(rows, N_GROUPS, GROUP_CONV)
    return jnp.concatenate([a3[:, :, :GROUP_X].reshape(rows, D_INNER),
                            a3[:, :, GROUP_X:GROUP_X + D_STATE].reshape(rows, N_GROUPS * D_STATE),
                            a3[:, :, GROUP_X + D_STATE:].reshape(rows, N_GROUPS * D_STATE)], axis=1)


def _small_shard(conv_w, conv_b, norm_g):
    ng = jnp.pad(norm_g.reshape(1, -1), ((0, 0), (0, CONV_SHARD - norm_g.shape[-1])))
    return jnp.concatenate([conv_w.reshape(4, CONV_SHARD), conv_b.reshape(1, CONV_SHARD), ng,
                            jnp.zeros((SMALL_ROWS - 6, CONV_SHARD), F32)], axis=0)


def _small_unshard(a):
    return a[0:4].reshape(1, 4, CONV_SHARD), a[4:5], a[5:6, :D_INNER // N_DEV]


def _heads_of(a):
    return a[:, :, :HEADS_PER_GROUP].reshape(1, N_HEADS)


def _head_params(p):
    return jnp.pad(p.reshape(N_GROUPS, 1, HEADS_PER_GROUP), ((0, 0), (0, 0), (0, 128 - HEADS_PER_GROUP)))


def _update(w, land, m, v, name):
    shp = w.shape
    to2 = lambda a: a.reshape(-1, shp[-1])
    return tuple(o.reshape(shp) for o in _adamw_reduced(to2(w), land, to2(m), to2(v), name))


def kernel(x, norm_mix_g, norm_mlp_g, pool_w, pool_b, pool_scale, ssm_w_in, ssm_conv_w, ssm_conv_b, ssm_dt_bias, ssm_a_log, ssm_d, ssm_norm_g, ssm_w_out, mlp_w1, mlp_w2, final_g, loss_target, m_norm_mix_g, m_norm_mlp_g, m_pool_w, m_pool_b, m_pool_scale, m_ssm_w_in, m_ssm_conv_w, m_ssm_conv_b, m_ssm_dt_bias, m_ssm_a_log, m_ssm_d, m_ssm_norm_g, m_ssm_w_out, m_mlp_w1, m_mlp_w2, m_final_g, v_norm_mix_g, v_norm_mlp_g, v_pool_w, v_pool_b, v_pool_scale, v_ssm_w_in, v_ssm_conv_w, v_ssm_conv_b, v_ssm_dt_bias, v_ssm_a_log, v_ssm_d, v_ssm_norm_g, v_ssm_w_out, v_mlp_w1, v_mlp_w2, v_final_g):
    x2 = x[0]
    tgt = loss_target[0]
    gm0, gm1 = norm_mix_g[0:1], norm_mix_g[1:2]
    gl0, gl1 = norm_mlp_g[0:1], norm_mlp_g[1:2]
    gfin = final_g.reshape(1, D_MODEL)

    fb = D_FF // N_DEV

    def bf(a):
        return a.astype(BF16)

    def gather_of(shards):
        return _direct_exchange(shards, [(i, 0) for i in range(len(shards))],
                                [(s.shape, s.dtype) for s in shards], scatter=False)

    def scatter_of(parts):
        return _direct_exchange(parts, [(i, 0) for i in range(len(parts))],
                                [(p.shape[1:], p.dtype) for p in parts], scatter=True)

    first = [bf(pool_w.reshape(4 * POOL_SHARD, POOL_GROUP)), bf(mlp_w1[0]), bf(mlp_w2[0]),
             _small_shard(ssm_conv_w, ssm_conv_b, ssm_norm_g)]
    w_pool, w1g0, w2g0, small_g = _gather(first, [(i, 0) for i in range(4)], [(s.shape, s.dtype) for s in first],
                                          "gather_layer0")
    conv_w = _group_conv_cols(small_g[:, 0:4].transpose(1, 0, 2).reshape(4, CONV_DIM))
    conv_b = _group_conv_cols(small_g[:, 4].reshape(1, CONV_DIM))
    ssm_ng = small_g[:, 5, :D_INNER // N_DEV].reshape(1, D_INNER)
    dtb, alog, dsk = _head_params(ssm_dt_bias), _head_params(ssm_a_log), _head_params(ssm_d)

    h1 = _pool_fwd(x2, gm0, w_pool, pool_b, pool_scale)
    (h2, u0, hm0), (w_in_g,) = _mlp_fwd(h1, gl0, w1g0, w2g0, "mlp0_fwd", carried=gather_of([bf(ssm_w_in[0])]))
    w_zx = _w_in_to_zx(w_in_g)
    (zx, hn1), (w1g1, w_out_g) = _norm_matmul(h2, gm1, w_zx, carried=gather_of([bf(mlp_w1[1]), bf(ssm_w_out[0])]))
    (y_ssd, states), (w2g1,) = _ssd_fwd(zx, conv_w, conv_b, dtb, alog, dsk, carried=gather_of([bf(mlp_w2[1])]))
    w_out = w_out_g.reshape(D_INNER, D_MODEL)
    h3 = _ssm_out_fwd(y_ssd, zx, ssm_ng, w_out, h2)
    (h4, u1, hm1), _ = _mlp_fwd(h3, gl1, w1g1, w2g1, "mlp1_fwd")
    dh4, dh4b, loss_row, d_gfin = _final(h4, gfin, tgt)

    (dh3, dh3b, da1, d_gl1), _ = _mlp_bwd(dh4, dh4b, h3, gl1, u1, w1g1, w2g1, "mlp1_bwd")
    d_w1_1 = _matmul_tn(hm1, da1, "mlp1_dw1", col_blocked=True)
    d_w2_1 = _matmul_tn(u1, dh4b, "mlp1_dw2", square_a=True).reshape(N_DEV, fb, D_MODEL)
    dy_ssd, dzx, yn, d_ng = _ssm_out_bwd(dh3b, y_ssd, zx, ssm_ng, w_out)
    d_wout = _matmul_tn(yn, dh3b, "ssm_dw_out").reshape(N_DEV, D_INNER // N_DEV, D_MODEL)
    (dzx, d_cw, d_cb, d_dtb, d_alog, d_dsk), (l_w1_1, l_w2_1, l_wout) = _ssd_bwd(
        zx, conv_w, conv_b, dtb, alog, dsk, states, dy_ssd, dzx, carried=scatter_of([d_w1_1, d_w2_1, d_wout]))
    dh2, dh2b, d_gm1 = _in_proj_bwd(dzx, w_zx, h2, gm1, dh3)
    d_w2_0 = _matmul_tn(u0, dh2b, "mlp0_dw2", square_a=True).reshape(N_DEV, fb, D_MODEL)
    d_wzx, (l_w2_0,) = _matmul_tn(hn1, dzx, "ssm_dw_in", carried=scatter_of([d_w2_0]))
    d_w_in = _zx_to_w_in(d_wzx)
    (dh1, _, da0, d_gl0), (l_w_in,) = _mlp_bwd(dh2, dh2b, h1, gl0, u0, w1g0, w2g0, "mlp0_bwd",
                                           carried=scatter_of([d_w_in]))
    d_w1_0 = _matmul_tn(hm0, da0, "mlp0_dw1", col_blocked=True)
    (dx, d_pool, d_pb, d_ps, d_gm0), (l_w1_0,) = _pool_bwd(x2, dh1, gm0, w_pool, pool_b, pool_scale,
                                                          carried=scatter_of([d_w1_0]))

    d_conv_w = _ungroup_conv_cols(d_cw).reshape(4, N_DEV, CONV_SHARD).transpose(1, 0, 2)
    d_conv_b = _ungroup_conv_cols(d_cb).reshape(N_DEV, 1, CONV_SHARD)
    d_gain = jnp.pad(d_ng.reshape(N_DEV, 1, D_INNER // N_DEV), ((0, 0), (0, 0), (0, CONV_SHARD - D_INNER // N_DEV)))
    d_small = jnp.concatenate([d_conv_w, d_conv_b, d_gain,
                               jnp.zeros((N_DEV, SMALL_ROWS - 6, CONV_SHARD), F32)], axis=1)
    l_pool, l_small = _run_exchange(scatter_of([bf(d_pool), d_small]), "reduce_scatter_tail")

    heads = jnp.concatenate([_heads_of(a) for a in (d_dtb, d_alog, d_dsk)], axis=1)
    sp = jnp.concatenate([d_gm0, d_gm1, d_gl0, d_gl1, d_pb, d_ps, d_gfin,
                          jnp.pad(heads, ((0, 0), (0, D_MODEL - 3 * N_HEADS)))], axis=0)
    sg = _all_reduce_small(sp)

    g_norm_mix = sg[0:2]
    g_norm_mlp = sg[2:4]
    g_pool_b, g_pool_scale = sg[4:5], sg[5:6]
    g_final = sg[6]
    g_dtb, g_alog, g_dsk = sg[7:8, 0:32], sg[7:8, 32:64], sg[7:8, 64:96]

    def rep_pack(nm, nl, pb, ps, fg, db, al, dk):
        hd = jnp.pad(jnp.concatenate([db, al, dk], axis=1), ((0, 0), (0, D_MODEL - 3 * N_HEADS)))
        return jnp.concatenate([nm, nl, pb, ps, fg.reshape(1, D_MODEL), hd], axis=0)

    rep = [rep_pack(*t) for t in (
        (norm_mix_g, norm_mlp_g, pool_b, pool_scale, final_g, ssm_dt_bias, ssm_a_log, ssm_d),
        (g_norm_mix, g_norm_mlp, g_pool_b, g_pool_scale, g_final, g_dtb, g_alog, g_dsk),
        (m_norm_mix_g, m_norm_mlp_g, m_pool_b, m_pool_scale, m_final_g, m_ssm_dt_bias, m_ssm_a_log, m_ssm_d),
        (v_norm_mix_g, v_norm_mlp_g, v_pool_b, v_pool_scale, v_final_g, v_ssm_dt_bias, v_ssm_a_log, v_ssm_d))]
    rep_out = _adamw(*rep, "adamw_replicated")

    def rep_unpack(a):
        return (a[0:2], a[2:4], a[4:5], a[5:6], a[6], a[7:8, 0:32], a[7:8, 32:64], a[7:8, 64:96])

    sm_out = _adamw_reduced(_small_shard(ssm_conv_w, ssm_conv_b, ssm_norm_g), l_small,
                            _small_shard(m_ssm_conv_w, m_ssm_conv_b, m_ssm_norm_g),
                            _small_shard(v_ssm_conv_w, v_ssm_conv_b, v_ssm_norm_g), "adamw_small_shards")

    def update_layers(w, lands, m, v, name):
        per = [_update(w[l], lands[l], m[l], v[l], name + str(l)) for l in range(2)]
        return tuple(jnp.stack([per[0][k], per[1][k]]) for k in range(4))

    big = {
        "pool_w": _update(pool_w, l_pool, m_pool_w, v_pool_w, "adamw_pool_w"),
        "ssm_w_in": _update(ssm_w_in, l_w_in, m_ssm_w_in, v_ssm_w_in, "adamw_w_in"),
        "ssm_w_out": _update(ssm_w_out, l_wout, m_ssm_w_out, v_ssm_w_out, "adamw_w_out"),
        "mlp_w1": update_layers(mlp_w1, (l_w1_0, l_w1_1), m_mlp_w1, v_mlp_w1, "adamw_w1_"),
        "mlp_w2": update_layers(mlp_w2, (l_w2_0, l_w2_1), m_mlp_w2, v_mlp_w2, "adamw_w2_"),
    }
    rep_all = (rep[1],) + tuple(rep_out)

    def ordered(kind):
        nm, nl, pb, ps, fg, db, al, dk = rep_unpack(rep_all[kind])
        cw, cb, ng = _small_unshard(sm_out[kind])
        return [nm, nl, big["pool_w"][kind], pb, ps, big["ssm_w_in"][kind], cw, cb, db, al, dk, ng,
                big["ssm_w_out"][kind], big["mlp_w1"][kind], big["mlp_w2"][kind], fg]

    loss = lax.psum(loss_row[0, 0], ("x", "y", "c"))
    return (loss, dx[None], *ordered(0), *ordered(1), *ordered(2), *ordered(3))
```

```python
import functools

import jax
import jax.numpy as jnp
from jax import lax
from jax.experimental import pallas as pl
from jax.experimental.pallas import tpu as pltpu

F32 = jnp.float32
BF16 = jnp.bfloat16
MESH = pl.DeviceIdType.MESH

D_MODEL = 1024
RMS_EPS = 1e-5
POOL_WINDOWS = (2, 4, 8, 16)
POOL_GROUP = 256
POOL_HALO = 16
POOL_SHARD = POOL_GROUP // 8
D_INNER = 2048
HEAD_DIM = 64
N_HEADS = 32
N_GROUPS = 4
HEADS_PER_GROUP = 8
D_STATE = 128
CHUNK = 128
CONV_DIM = 3072
IN_PROJ_DIM = 5152
D_FF = 4096
N_DEV = 8
GROUP_X = HEADS_PER_GROUP * HEAD_DIM
GROUP_CONV = GROUP_X + 2 * D_STATE
GROUP_COLS = GROUP_CONV + 128
Z_OFF = N_GROUPS * GROUP_COLS
ZX_COLS = Z_OFF + D_INNER
COL_BLK = 512
W_IN_SHARD = IN_PROJ_DIM // N_DEV

ADAM_LR = 0.001
ADAM_B1 = 0.9
ADAM_B2 = 0.999
ADAM_EPS = 1e-08
ADAM_WD = 0.01
ADAM_STEP = 10

VMEM_LIMIT_V7X = 56 * 1024 * 1024
MID_STEP_PERCENT = 70
MATMUL_TOKENS = 1024

CONV_SHARD = CONV_DIM // N_DEV
SMALL_ROWS = 8

_NN = (((1,), (0,)), ((), ()))
_NT = (((1,), (1,)), ((), ()))
_TN = (((0,), (0,)), ((), ()))


def _cp(sem):
    return pltpu.CompilerParams(dimension_semantics=sem, vmem_limit_bytes=VMEM_LIMIT_V7X)


_ANY = pl.BlockSpec(memory_space=pl.ANY)


def _place():
    return lax.axis_index("x"), lax.axis_index("y"), lax.axis_index("c")


class _Carried:
    def __init__(self, ins, outs, sems, start, finish, mid=None):
        self.ins, self.outs, self.sems = list(ins), list(outs), list(sems)
        self.start, self.mid, self.finish = start, mid, finish


def _pcall(body, *, name, grid, in_specs, out_specs, out_shape, sem, args, scratch_shapes=(), carried=None,
           aliases=None):
    in_specs, out_specs, out_shape, scratch = list(in_specs), list(out_specs), list(out_shape), list(scratch_shapes)
    common = dict(name=name, grid=grid, input_output_aliases=aliases or {}, compiler_params=_cp(sem))
    if carried is None:
        res = pl.pallas_call(body, in_specs=in_specs, out_specs=out_specs, out_shape=out_shape,
                             scratch_shapes=scratch, **common)(*args)
        return list(res), []
    n_in, n_out, n_scr = len(in_specs), len(out_specs), len(scratch)
    ci, co = len(carried.ins), len(carried.outs)

    def wrapped(*refs):
        ins, cins = refs[:n_in], refs[n_in:n_in + ci]
        p = n_in + ci
        outs, couts = refs[p:p + n_out], refs[p + n_out:p + n_out + co]
        p += n_out + co
        scr, csems = refs[p:p + n_scr], refs[p + n_scr:]
        ids = [pl.program_id(a) for a in range(len(grid))]
        first = functools.reduce(jnp.logical_and, [i == 0 for i in ids])
        last = functools.reduce(jnp.logical_and, [i == g - 1 for i, g in zip(ids, grid)])

        @pl.when(first)
        def _():
            carried.start(cins, couts, csems)

        if carried.mid is not None:
            step, steps = 0, 1
            for i, g in zip(ids, grid):
                step, steps = step * g + i, steps * g

            @pl.when(step == (steps * MID_STEP_PERCENT) // 100)
            def _():
                carried.mid(cins, couts, csems)

        body(*ins, *outs, *scr)

        @pl.when(last)
        def _():
            carried.finish(cins, couts, csems)

    res = pl.pallas_call(wrapped, in_specs=in_specs + [_ANY] * ci, out_specs=out_specs + [_ANY] * co,
                         out_shape=out_shape + carried.outs, scratch_shapes=scratch + carried.sems,
                         **common)(*args, *carried.ins)
    return list(res[:n_out]), list(res[n_out:])


def _peers(x, y, c):
    out = []
    for rel in range(1, N_DEV):
        dx, dy, dc = (rel >> 2) & 1, (rel >> 1) & 1, rel & 1
        out.append((x + dx - 2 * x * dx, y + dy - 2 * y * dy, c + dc - 2 * c * dc))
    return out


def _direct_exchange(srcs, layout, out_shapes, scatter):
    n = len(srcs)

    def copies(ins, outs, sems):
        send, recv, loc = sems
        x, y, c = _place()
        me = 4 * x + 2 * y + c
        out, arrive, local = [], [], []
        for i in range(n):
            j, off = layout[i]
            rows = srcs[i].shape[-2]
            for r, peer in enumerate(_peers(x, y, c)):
                pidx = 4 * peer[0] + 2 * peer[1] + peer[2]
                src = ins[i].at[pidx] if scatter else ins[i]
                kw = dict(send_sem=send.at[7 * i + r], recv_sem=recv.at[7 * i + r], device_id=peer, device_id_type=MESH)
                out.append(pltpu.make_async_remote_copy(src_ref=src, dst_ref=outs[j].at[me, pl.ds(off, rows)], **kw))
                arrive.append(pltpu.make_async_remote_copy(src_ref=src, dst_ref=outs[j].at[pidx, pl.ds(off, rows)], **kw))
            own = ins[i].at[me] if scatter else ins[i]
            local.append(pltpu.make_async_copy(own, outs[j].at[me, pl.ds(off, rows)], loc.at[i]))
        return out, arrive, local

    def start(ins, outs, sems):
        out, _, local = copies(ins, outs, sems)
        for cp in local + out:
            cp.start()

    def finish(ins, outs, sems):
        out, arrive, local = copies(ins, outs, sems)
        for cp in arrive:
            cp.wait_recv()
        for cp in out:
            cp.wait_send()
        for cp in local:
            cp.wait()

    return _Carried(srcs, [jax.ShapeDtypeStruct((N_DEV,) + tuple(s), d) for s, d in out_shapes],
                    [pltpu.SemaphoreType.DMA((7 * n,)), pltpu.SemaphoreType.DMA((7 * n,)),
                     pltpu.SemaphoreType.DMA((n,))], start, finish)


def _two_level_gather(shards):
    n = len(shards)

    def copies(ins, outs, sems):
        send, recv, loc = sems
        x, y, c = _place()
        me, sibling = (x, y, c), (x, y, 1 - c)
        chips = [(1 - x, y), (x, 1 - y), (1 - x, 1 - y)]

        def win(i, place):
            return outs[i].at[4 * place[0] + 2 * place[1] + place[2]]

        def copy(i, k, block, to, src=None):
            return pltpu.make_async_remote_copy(
                src_ref=win(i, block) if src is None else src, dst_ref=win(i, block),
                send_sem=send.at[7 * i + k], recv_sem=recv.at[7 * i + k], device_id=to, device_id_type=MESH)

        own, passed, ici_in, d2d_in, local = [], [], [], [], []
        for i in range(n):
            own += [copy(i, 0, me, sibling, src=ins[i])]
            own += [copy(i, 1 + j, me, (*chip, c), src=ins[i]) for j, chip in enumerate(chips)]
            passed += [copy(i, 4 + j, (*chip, c), sibling) for j, chip in enumerate(chips)]
            ici_in += [copy(i, 1 + j, (*chip, c), me) for j, chip in enumerate(chips)]
            d2d_in += [copy(i, 0, sibling, me)] + [copy(i, 4 + j, (*chip, 1 - c), me) for j, chip in enumerate(chips)]
            local.append(pltpu.make_async_copy(ins[i], win(i, me), loc.at[i]))
        return own, passed, ici_in, d2d_in, local

    def start(ins, outs, sems):
        own, _, _, _, local = copies(ins, outs, sems)
        for cp in local + own:
            cp.start()

    def mid(ins, outs, sems):
        _, passed, ici_in, _, _ = copies(ins, outs, sems)
        for arrived, onward in zip(ici_in, passed):
            arrived.wait_recv()
            onward.start()

    def finish(ins, outs, sems):
        own, passed, _, d2d_in, local = copies(ins, outs, sems)
        for cp in d2d_in:
            cp.wait_recv()
        for cp in own + passed:
            cp.wait_send()
        for cp in local:
            cp.wait()

    return _Carried(shards, [jax.ShapeDtypeStruct((N_DEV,) + tuple(s.shape), s.dtype) for s in shards],
                    [pltpu.SemaphoreType.DMA((7 * n,)), pltpu.SemaphoreType.DMA((7 * n,)),
                     pltpu.SemaphoreType.DMA((n,))], start, finish, mid)


def _run_exchange(carried, name):
    ci = len(carried.ins)

    def body(*refs):
        ins, outs, sems = refs[:ci], refs[ci:ci + len(carried.outs)], refs[ci + len(carried.outs):]
        carried.start(ins, outs, sems)
        if carried.mid is not None:
            carried.mid(ins, outs, sems)
        carried.finish(ins, outs, sems)

    return list(pl.pallas_call(body, name=name, in_specs=[_ANY] * ci, out_specs=[_ANY] * len(carried.outs),
                               out_shape=carried.outs, scratch_shapes=carried.sems)(*carried.ins))


def _dg(a, b, dn):
    return lax.dot_general(a.astype(BF16), b.astype(BF16), dn, preferred_element_type=F32)


@jax.custom_vjp
def mm_nn(a, b):
    return _dg(a, b, _NN)


@jax.custom_vjp
def mm_nt(a, b):
    return _dg(a, b, _NT)


@jax.custom_vjp
def mm_tn(a, b):
    return _dg(a, b, _TN)


mm_nn.defvjp(lambda a, b: (_dg(a, b, _NN), (a, b)), lambda r, ct: (mm_nt(ct, r[1]), mm_tn(r[0], ct)))
mm_nt.defvjp(lambda a, b: (_dg(a, b, _NT), (a, b)), lambda r, ct: (mm_nn(ct, r[1]), mm_tn(ct, r[0])))
mm_tn.defvjp(lambda a, b: (_dg(a, b, _TN), (a, b)), lambda r, ct: (mm_nt(r[1], ct), mm_nn(r[0], ct)))


def _split3(x):
    p1 = x.astype(BF16)
    r1 = x - p1.astype(F32)
    p2 = r1.astype(BF16)
    r2 = r1 - p2.astype(F32)
    return p1, p2, r2.astype(BF16)


def _exact01(x, c, dn, const_left):
    acc = None
    for p in reversed(_split3(x)):
        t = (lax.dot_general(c, p, dn, preferred_element_type=F32) if const_left
             else lax.dot_general(p, c, dn, preferred_element_type=F32))
        acc = t if acc is None else acc + t
    return acc


def _make_cmm(dn, const_left, bwd_name):
    @jax.custom_vjp
    def f(x, c):
        return _exact01(x, c, dn, const_left)

    def fwd(x, c):
        return _exact01(x, c, dn, const_left), c

    def bwd(c, ct):
        return _CMM[bwd_name](ct, c), jnp.zeros_like(c)

    f.defvjp(fwd, bwd)
    return f


_CMM = {}
_CMM["xc"] = _make_cmm(_NN, False, "xct")
_CMM["xct"] = _make_cmm(_NT, False, "xc")
_CMM["cx"] = _make_cmm(_NN, True, "ctx")
_CMM["ctx"] = _make_cmm(_TN, True, "cx")


@jax.custom_vjp
def _silu(x):
    return x / (1.0 + jnp.exp(-x))


def _silu_fwd(x):
    return _silu(x), x


def _silu_bwd(x, ct):
    s = 1.0 / (1.0 + jnp.exp(-x))
    return (ct * (s * (1.0 + x * (1.0 - s))),)


_silu.defvjp(_silu_fwd, _silu_bwd)


def _log1p_pos(e):
    u = 1.0 + e
    d = u - 1.0
    return jnp.where(d == 0.0, e, jnp.log(u) * (e / jnp.where(d == 0.0, 1.0, d)))


@jax.custom_vjp
def _softplus(x):
    return jnp.maximum(x, 0.0) + _log1p_pos(jnp.exp(-jnp.abs(x)))


def _softplus_fwd(x):
    return _softplus(x), x


def _softplus_bwd(x, ct):
    return (ct / (1.0 + jnp.exp(-x)),)


_softplus.defvjp(_softplus_fwd, _softplus_bwd)


CONV_HALO = 8


def _make_shift(j):
    @jax.custom_vjp
    def f(ext):
        return pltpu.roll(ext, j, 0)[CONV_HALO:, :]

    def fwd(ext):
        return f(ext), None

    def bwd(_, ct):
        pad = jnp.concatenate([jnp.zeros((CONV_HALO, ct.shape[1]), ct.dtype), ct], axis=0)
        return (pltpu.roll(pad, CONV_HALO + CHUNK - j, 0),)

    f.defvjp(fwd, bwd)
    return f


_SHIFT = {j: _make_shift(j) for j in (1, 2, 3)}


@jax.custom_vjp
def _swap_halves(x):
    return pltpu.roll(x, HEAD_DIM, 1)


_swap_halves.defvjp(lambda x: (_swap_halves(x), None), lambda _, ct: (pltpu.roll(ct, HEAD_DIM, 1),))


def _rms_fwd(x, g):
    r = lax.rsqrt(jnp.mean(x * x, axis=-1, keepdims=True) + RMS_EPS)
    n = x * r
    return n * g, n, r


def _rms_bwd(dy, n, r, g):
    dn = dy * g
    dx = r * (dn - n * jnp.mean(dn * n, axis=-1, keepdims=True))
    dg = jnp.sum(dy * n, axis=0, keepdims=True)
    return dx, dg


def _one(cond):
    return jnp.where(cond, 1.0, 0.0)


def _pool_tile(xe, g, ws, b, scale, tile, tt):
    r = lax.rsqrt(jnp.mean(xe * xe, axis=-1, keepdims=True) + RMS_EPS)
    hn = xe * r * g
    row_e = lax.broadcasted_iota(jnp.int32, (tt + POOL_HALO, POOL_GROUP), 0)
    keep = _one(jnp.logical_or(row_e >= POOL_HALO, tile > 0))
    rr = lax.broadcasted_iota(jnp.int32, (tt, tt + POOL_HALO), 0)
    qq = lax.broadcasted_iota(jnp.int32, (tt, tt + POOL_HALO), 1)
    dd = qq - rr
    tpos = tile * tt + lax.broadcasted_iota(jnp.int32, (tt, POOL_GROUP), 0)
    outs = []
    for gi, w in enumerate(POOL_WINDOWS):
        hg = hn[:, gi * POOL_GROUP:(gi + 1) * POOL_GROUP] * keep
        band = _one(jnp.logical_and(dd >= POOL_HALO - w + 1, dd <= POOL_HALO)).astype(BF16)
        cnt = jnp.minimum(tpos + 1, w).astype(F32)
        pooled = _CMM["cx"](hg, band) / cnt
        mixed = pooled - hg[POOL_HALO:, :]
        outs.append(mm_nn(mixed, ws[gi]))
    out = (jnp.concatenate(outs, axis=1) + b) * scale
    return xe[POOL_HALO:, :] + out


def _pool_specs(tt, nt, rev):
    per = tt // POOL_HALO
    t_of = (lambda i: nt - 1 - i) if rev else (lambda i: i)
    main = pl.BlockSpec((tt, D_MODEL), lambda i: (t_of(i), 0))
    halo = pl.BlockSpec((POOL_HALO, D_MODEL), lambda i: (jnp.maximum(t_of(i) * per - 1, 0), 0))
    vec = pl.BlockSpec((1, D_MODEL), lambda i: (0, 0))
    wsp = pl.BlockSpec((N_DEV, 4 * POOL_SHARD, POOL_GROUP), lambda i: (0, 0, 0))
    return main, halo, vec, wsp


def _pool_weights(w_ref):
    return tuple(
        jnp.concatenate([w_ref[k, gi * POOL_SHARD:(gi + 1) * POOL_SHARD, :] for k in range(N_DEV)], axis=0).astype(F32)
        for gi in range(4))


def _pool_fwd(x, g, w, b, scale, carried=None):
    t = x.shape[0]
    tt = min(t, 256)
    nt = t // tt
    main, halo, vec, wsp = _pool_specs(tt, nt, False)

    def body(xm_ref, xh_ref, g_ref, w_ref, b_ref, s_ref, o_ref):
        i = pl.program_id(0)
        xe = jnp.concatenate([xh_ref[...], xm_ref[...]], axis=0)
        o_ref[...] = _pool_tile(xe, g_ref[...], _pool_weights(w_ref), b_ref[...], s_ref[...], i, tt)

    return _pcall(
        body, name="pool_fwd", grid=(nt,),
        in_specs=[main, halo, vec, wsp, vec, vec], out_specs=[main],
        out_shape=[jax.ShapeDtypeStruct((t, D_MODEL), F32)],
        sem=("arbitrary",), args=(x, x, g, w, b, scale), carried=carried)


def _pool_bwd(x, dh, g, w, b, scale, carried=None):
    t = x.shape[0]
    tt = min(t, 256)
    nt = t // tt
    main, halo, vec, wsp = _pool_specs(tt, nt, True)

    def body(xm_ref, xh_ref, dh_ref, g_ref, w_ref, b_ref, s_ref,
             dx_ref, dw_ref, db_ref, ds_ref, dg_ref, carry, dw_acc):
        i = pl.program_id(0)
        tile = nt - 1 - i

        @pl.when(i == 0)
        def _():
            carry[...] = jnp.zeros_like(carry)
            dw_acc[...] = jnp.zeros_like(dw_acc)
            db_ref[...] = jnp.zeros_like(db_ref)
            ds_ref[...] = jnp.zeros_like(ds_ref)
            dg_ref[...] = jnp.zeros_like(dg_ref)

        xe = jnp.concatenate([xh_ref[...], xm_ref[...]], axis=0)
        _, vjp = jax.vjp(lambda a, gg, ww, bb, ss: _pool_tile(a, gg, ww, bb, ss, tile, tt),
                         xe, g_ref[...], _pool_weights(w_ref), b_ref[...], s_ref[...])
        dxe, dgv, dws, dbv, dsv = vjp(dh_ref[...])
        dx_ref[...] = dxe[POOL_HALO:, :]
        dx_ref[tt - POOL_HALO:tt, :] += carry[...]
        carry[...] = dxe[:POOL_HALO, :]
        for gi in range(4):
            dw_acc[gi] += dws[gi]
        db_ref[...] += dbv
        ds_ref[...] += dsv
        dg_ref[...] += dgv

        @pl.when(i == nt - 1)
        def _():
            for k in range(N_DEV):
                for gi in range(4):
                    dw_ref[k, gi * POOL_SHARD:(gi + 1) * POOL_SHARD, :] = dw_acc[gi, k * POOL_SHARD:(k + 1) * POOL_SHARD, :]

    return _pcall(
        body, name="pool_bwd", grid=(nt,),
        in_specs=[main, halo, main, vec, wsp, vec, vec],
        out_specs=[main, wsp, vec, vec, vec],
        out_shape=[jax.ShapeDtypeStruct((t, D_MODEL), F32),
                   jax.ShapeDtypeStruct((N_DEV, 4 * POOL_SHARD, POOL_GROUP), F32),
                   jax.ShapeDtypeStruct((1, D_MODEL), F32),
                   jax.ShapeDtypeStruct((1, D_MODEL), F32),
                   jax.ShapeDtypeStruct((1, D_MODEL), F32)],
        scratch_shapes=[pltpu.VMEM((POOL_HALO, D_MODEL), F32), pltpu.VMEM((4, POOL_GROUP, POOL_GROUP), F32)],
        sem=("arbitrary",), args=(x, x, dh, g, w, b, scale), carried=carried)


def _mlp_weight_specs():
    fb = D_FF // N_DEV
    return (pl.BlockSpec((None, D_MODEL, fb), lambda i, k: (k, 0, 0)),
            pl.BlockSpec((None, fb, D_MODEL), lambda i, k: (k, 0, 0)))


def _mlp_fwd(h, g, w1g, w2g, name, carried=None):
    t = h.shape[0]
    tt = min(t, MATMUL_TOKENS)
    nk, fb = N_DEV, D_FF // N_DEV
    w1_spec, w2_spec = _mlp_weight_specs()

    def body(h_ref, g_ref, w1_ref, w2_ref, o_ref, u_ref, hm_ref, hm_s, acc_s):
        k = pl.program_id(1)

        @pl.when(k == 0)
        def _():
            xv = h_ref[...]
            y, _, _ = _rms_fwd(xv, g_ref[...])
            hb = y.astype(BF16)
            hm_s[...] = hb
            hm_ref[...] = hb
            acc_s[...] = xv

        a = jnp.dot(hm_s[...], w1_ref[...], preferred_element_type=F32)
        u = jnp.maximum(a, 0.0)
        u_ref[...] = u.astype(BF16)
        acc_s[...] += jnp.dot((u * u).astype(BF16), w2_ref[...], preferred_element_type=F32)

        @pl.when(k == nk - 1)
        def _():
            o_ref[...] = acc_s[...]

    return _pcall(
        body, name=name, grid=(t // tt, nk),
        in_specs=[pl.BlockSpec((tt, D_MODEL), lambda i, k: (i, 0)),
                  pl.BlockSpec((1, D_MODEL), lambda i, k: (0, 0)),
                  w1_spec, w2_spec],
        out_specs=[pl.BlockSpec((tt, D_MODEL), lambda i, k: (i, 0)),
                   pl.BlockSpec((tt, fb), lambda i, k: (i, k)),
                   pl.BlockSpec((tt, D_MODEL), lambda i, k: (i, 0))],
        out_shape=[jax.ShapeDtypeStruct((t, D_MODEL), F32),
                   jax.ShapeDtypeStruct((t, nk * fb), BF16),
                   jax.ShapeDtypeStruct((t, D_MODEL), BF16)],
        scratch_shapes=[pltpu.VMEM((tt, D_MODEL), BF16), pltpu.VMEM((tt, D_MODEL), F32)],
        sem=("arbitrary", "arbitrary"), args=(h, g, w1g, w2g), carried=carried)


def _mlp_bwd(dh, dhb, h, g, u, w1g, w2g, name, carried=None):
    t = h.shape[0]
    tt = min(t, MATMUL_TOKENS)
    nk, fb = N_DEV, D_FF // N_DEV
    w1_spec, w2_spec = _mlp_weight_specs()

    def body(dh_ref, dhb_ref, h_ref, g_ref, u_ref, w1_ref, w2_ref,
             dhin_ref, dhinb_ref, da_ref, dg_ref, acc_s):
        i = pl.program_id(0)
        k = pl.program_id(1)

        @pl.when(jnp.logical_and(i == 0, k == 0))
        def _():
            dg_ref[...] = jnp.zeros_like(dg_ref)

        @pl.when(k == 0)
        def _():
            acc_s[...] = jnp.zeros_like(acc_s)

        dv = lax.dot_general(dhb_ref[...], w2_ref[...], _NT, preferred_element_type=F32)
        dab = (dv * (2.0 * u_ref[...].astype(F32))).astype(BF16)
        da_ref[...] = dab
        acc_s[...] += lax.dot_general(dab, w1_ref[...], _NT, preferred_element_type=F32)

        @pl.when(k == nk - 1)
        def _():
            gv = g_ref[...]
            _, n, r = _rms_fwd(h_ref[...], gv)
            dx, dg = _rms_bwd(acc_s[...], n, r, gv)
            dhin = dh_ref[...] + dx
            dhin_ref[...] = dhin
            dhinb_ref[...] = dhin.astype(BF16)
            dg_ref[...] += dg

    tile = pl.BlockSpec((tt, D_MODEL), lambda i, k: (i, 0))
    return _pcall(
        body, name=name, grid=(t // tt, nk),
        in_specs=[tile, tile, tile, pl.BlockSpec((1, D_MODEL), lambda i, k: (0, 0)),
                  pl.BlockSpec((tt, fb), lambda i, k: (i, k)), w1_spec, w2_spec],
        out_specs=[tile, tile, pl.BlockSpec((tt, fb), lambda i, k: (i, k)),
                   pl.BlockSpec((1, D_MODEL), lambda i, k: (0, 0))],
        out_shape=[jax.ShapeDtypeStruct((t, D_MODEL), F32),
                   jax.ShapeDtypeStruct((t, D_MODEL), BF16),
                   jax.ShapeDtypeStruct((t, nk * fb), BF16),
                   jax.ShapeDtypeStruct((1, D_MODEL), F32)],
        scratch_shapes=[pltpu.VMEM((tt, D_MODEL), F32)],
        sem=("arbitrary", "arbitrary"), args=(dh, dhb, h, g, u, w1g, w2g), carried=carried)


def _matmul_tn(a, b, name, square_a=False, col_blocked=False, carried=None):
    t, k1 = a.shape
    k2 = b.shape[1]
    b1 = min(k1, 1024)
    tt = min(t, MATMUL_TOKENS)
    nt = t // tt

    def body(a_ref, b_ref, o_ref, acc):
        s = pl.program_id(2)

        @pl.when(s == 0)
        def _():
            acc[...] = jnp.zeros_like(acc)

        av = a_ref[...]
        if square_a:
            af = av.astype(F32)
            av = (af * af).astype(BF16)
        acc[...] += lax.dot_general(av, b_ref[...], _TN, preferred_element_type=F32)

        @pl.when(s == nt - 1)
        def _():
            o_ref[...] = acc[...].astype(o_ref.dtype)

    if col_blocked:
        out_shape = jax.ShapeDtypeStruct((k2 // COL_BLK, k1, COL_BLK), BF16)
        out_spec = pl.BlockSpec((None, b1, COL_BLK), lambda i, j, s: (j, i, 0))
    else:
        out_shape = jax.ShapeDtypeStruct((k1, k2), BF16)
        out_spec = pl.BlockSpec((b1, COL_BLK), lambda i, j, s: (i, j))
    outs, landed = _pcall(
        body, name=name, grid=(k1 // b1, k2 // COL_BLK, nt),
        in_specs=[pl.BlockSpec((tt, b1), lambda i, j, s: (s, i)),
                  pl.BlockSpec((tt, COL_BLK), lambda i, j, s: (s, j))],
        out_specs=[out_spec], out_shape=[out_shape],
        scratch_shapes=[pltpu.VMEM((b1, COL_BLK), F32)],
        sem=("arbitrary", "arbitrary", "arbitrary"), args=(a, b), carried=carried)
    return (outs[0], landed) if carried is not None else outs[0]


def _norm_matmul(h, g, w, carried=None):
    t = h.shape[0]
    tt = min(t, MATMUL_TOKENS)
    n = w.shape[1]

    def body(h_ref, g_ref, w_ref, o_ref, hn_ref, hn_s):
        @pl.when(pl.program_id(1) == 0)
        def _():
            y, _, _ = _rms_fwd(h_ref[...], g_ref[...])
            hb = y.astype(BF16)
            hn_s[...] = hb
            hn_ref[...] = hb

        o_ref[...] = jnp.dot(hn_s[...], w_ref[...], preferred_element_type=F32)

    return _pcall(
        body, name="ssm_in_proj", grid=(t // tt, n // COL_BLK),
        in_specs=[pl.BlockSpec((tt, D_MODEL), lambda i, j: (i, 0)),
                  pl.BlockSpec((1, D_MODEL), lambda i, j: (0, 0)),
                  pl.BlockSpec((D_MODEL, COL_BLK), lambda i, j: (0, j))],
        out_specs=[pl.BlockSpec((tt, COL_BLK), lambda i, j: (i, j)),
                   pl.BlockSpec((tt, D_MODEL), lambda i, j: (i, 0))],
        out_shape=[jax.ShapeDtypeStruct((t, n), F32), jax.ShapeDtypeStruct((t, D_MODEL), BF16)],
        scratch_shapes=[pltpu.VMEM((tt, D_MODEL), BF16)],
        sem=("arbitrary", "arbitrary"), args=(h, g, w), carried=carried)


def _in_proj_bwd(dzx, w, h, g, dh_next, carried=None):
    t = h.shape[0]
    tt = min(t, MATMUL_TOKENS)
    n = w.shape[1]
    nj = n // COL_BLK

    def body(dz_ref, w_ref, h_ref, g_ref, dn_ref, dh_ref, dhb_ref, dg_ref, acc):
        i = pl.program_id(0)
        j = pl.program_id(1)

        @pl.when(jnp.logical_and(i == 0, j == 0))
        def _():
            dg_ref[...] = jnp.zeros_like(dg_ref)

        @pl.when(j == 0)
        def _():
            acc[...] = jnp.zeros_like(acc)

        acc[...] += lax.dot_general(dz_ref[...], w_ref[...], _NT, preferred_element_type=F32)

        @pl.when(j == nj - 1)
        def _():
            gv = g_ref[...]
            _, nn, r = _rms_fwd(h_ref[...], gv)
            dx, dg = _rms_bwd(acc[...], nn, r, gv)
            dh = dn_ref[...] + dx
            dh_ref[...] = dh
            dhb_ref[...] = dh.astype(BF16)
            dg_ref[...] += dg

    tile = pl.BlockSpec((tt, D_MODEL), lambda i, j: (i, 0))
    return _pcall(
        body, name="ssm_in_proj_bwd", grid=(t // tt, nj),
        in_specs=[pl.BlockSpec((tt, COL_BLK), lambda i, j: (i, j)),
                  pl.BlockSpec((D_MODEL, COL_BLK), lambda i, j: (0, j)),
                  tile, pl.BlockSpec((1, D_MODEL), lambda i, j: (0, 0)), tile],
        out_specs=[tile, tile, pl.BlockSpec((1, D_MODEL), lambda i, j: (0, 0))],
        out_shape=[jax.ShapeDtypeStruct((t, D_MODEL), F32), jax.ShapeDtypeStruct((t, D_MODEL), BF16),
                   jax.ShapeDtypeStruct((1, D_MODEL), F32)],
        scratch_shapes=[pltpu.VMEM((tt, D_MODEL), F32)],
        sem=("arbitrary", "arbitrary"), args=(dzx, w, h, g, dh_next), carried=carried)


def _ssd_consts():
    lane = lax.broadcasted_iota(jnp.int32, (CHUNK, CHUNK), 1)
    row = lax.broadcasted_iota(jnp.int32, (CHUNK, CHUNK), 0)
    causal = lane <= row
    tri = _one(causal).astype(BF16)
    er = lax.broadcasted_iota(jnp.int32, (CHUNK, GROUP_X), 0)
    ec = lax.broadcasted_iota(jnp.int32, (CHUNK, GROUP_X), 1)
    expand = _one(jnp.right_shift(ec, 6) == er).astype(BF16)
    return dict(causal=causal, tri=tri, expand=expand, lo=lane < HEAD_DIM)


def _conv_silu(cur, prev, w, b):
    ext = jnp.concatenate([prev, cur], axis=0)
    acc = cur * w[3] + b
    for j in (1, 2, 3):
        acc = acc + _SHIFT[j](ext) * w[3 - j]
    return _silu(acc)


def _ssd_chunk(raw, rawp, ht, cw, cb_, dtb, alog, dsk, k):
    act = _conv_silu(raw[:, :GROUP_CONV], rawp[:, :GROUP_CONV], cw, cb_)
    xs = act[:, :GROUP_X]
    bm = act[:, GROUP_X:GROUP_X + D_STATE]
    cm = act[:, GROUP_X + D_STATE:]
    dt = _softplus(raw[:, GROUP_CONV:] + dtb)
    a = -jnp.exp(alog)
    xc = _CMM["xc"]

    def lanes(rowv):
        return jnp.sum(xc(jnp.broadcast_to(rowv, (16, CHUNK)), k["expand"]), axis=0, keepdims=True) * (1.0 / 16.0)

    dt_e = xc(dt, k["expand"])
    adt_e = dt_e * lanes(a)
    acs_e = _CMM["cx"](adt_e, k["tri"])
    tot_e = jnp.sum(adt_e, axis=0, keepdims=True)
    gmat = mm_nt(cm, bm)
    xdt = xs * dt_e
    ys = []
    for j in range(HEADS_PER_GROUP // 2):
        pair = acs_e[:, j * CHUNK:(j + 1) * CHUNK]
        swapped = _swap_halves(pair)
        ms = []
        for cb in (jnp.where(k["lo"], pair, swapped), jnp.where(k["lo"], swapped, pair)):
            seg = cb - cb.T
            ms.append(gmat * jnp.exp(jnp.where(k["causal"], seg, -jnp.inf)))
        xp = xdt[:, j * CHUNK:(j + 1) * CHUNK]
        rhs = jnp.concatenate([jnp.where(k["lo"], xp, 0.0), jnp.where(k["lo"], 0.0, xp)], axis=0)
        ys.append(mm_nn(jnp.concatenate(ms, axis=1), rhs))
    y_diag = jnp.concatenate(ys, axis=1)
    y_off = jnp.exp(acs_e) * mm_nn(cm, ht)
    h_new = jnp.exp(tot_e) * ht + mm_tn(bm, xdt * jnp.exp(tot_e - acs_e))
    return y_diag + y_off + lanes(dsk) * xs, h_new


def _ssd_in_specs(nc, rev):
    c_of = (lambda c: nc - 1 - c) if rev else (lambda c: c)
    per = CHUNK // CONV_HALO
    zx = [pl.BlockSpec((CHUNK, GROUP_COLS), lambda g, c: (c_of(c), g)),
          pl.BlockSpec((CONV_HALO, GROUP_COLS), lambda g, c: (jnp.maximum(c_of(c) * per - 1, 0), g))]
    conv = [pl.BlockSpec((4, GROUP_CONV), lambda g, c: (0, g)), pl.BlockSpec((1, GROUP_CONV), lambda g, c: (0, g))]
    head = [pl.BlockSpec((None, 1, 128), lambda g, c: (g, 0, 0))] * 3
    return zx + conv + head, c_of


def _load_chunk_args(refs, has_prev):
    raw, rawp, cw, cb_, dtb, alog, dsk = refs
    return (raw[...], rawp[...] * has_prev, tuple(cw[pl.ds(i, 1), :] for i in range(4)), cb_[...],
            dtb[...], alog[...], dsk[...])


def _ssd_fwd(zx, conv_w, conv_b, dtb, alog, dsk, carried=None):
    t = zx.shape[0]
    nc = t // CHUNK
    in_specs, _ = _ssd_in_specs(nc, False)

    def body(*refs):
        ins, (y_ref, hs_ref, ht) = refs[:7], refs[7:]
        c = pl.program_id(1)

        @pl.when(c == 0)
        def _():
            ht[...] = jnp.zeros_like(ht)

        a = _load_chunk_args(ins, _one(c > 0))
        h_in = ht[...]
        y, h_new = _ssd_chunk(*a[:2], h_in, *a[2:], _ssd_consts())
        y_ref[...] = y
        hs_ref[...] = h_in
        ht[...] = h_new

    return _pcall(
        body, name="ssd_fwd", grid=(N_GROUPS, nc),
        in_specs=in_specs,
        out_specs=[pl.BlockSpec((CHUNK, GROUP_X), lambda g, c: (c, g)),
                   pl.BlockSpec((None, None, D_STATE, GROUP_X), lambda g, c: (g, c, 0, 0))],
        out_shape=[jax.ShapeDtypeStruct((t, D_INNER), F32),
                   jax.ShapeDtypeStruct((N_GROUPS, nc, D_STATE, GROUP_X), F32)],
        scratch_shapes=[pltpu.VMEM((D_STATE, GROUP_X), F32)],
        sem=("arbitrary", "arbitrary"), args=(zx, zx, conv_w, conv_b, dtb, alog, dsk), carried=carried)


def _ssd_bwd(zx, conv_w, conv_b, dtb, alog, dsk, hs, dy, dzx, carried=None):
    t = zx.shape[0]
    nc = t // CHUNK
    in_specs, c_of = _ssd_in_specs(nc, True)
    n_in = 10

    def body(*refs):
        ins, hs_ref, dy_ref = refs[:7], refs[7], refs[8]
        (draw_ref, dcw, dcb, ddtb, dalog, ddsk, dht, carry) = refs[n_in:]
        cc = pl.program_id(1)
        accs = (dcw, dcb, ddtb, dalog, ddsk)

        @pl.when(cc == 0)
        def _():
            for r in (dht, carry) + accs:
                r[...] = jnp.zeros_like(r)

        has_prev = _one(c_of(cc) > 0)
        a = _load_chunk_args(ins, has_prev)
        k = _ssd_consts()
        fn = lambda *args: _ssd_chunk(*args, k)
        _, vjp = jax.vjp(fn, *a[:2], hs_ref[...], *a[2:])
        graw, grawp, ght, gcw, gcb, gdtb, galog, gdsk = vjp((dy_ref[...], dht[...]))
        tail = jnp.concatenate([jnp.zeros((CHUNK - CONV_HALO, GROUP_COLS), F32), carry[...]], axis=0)
        draw_ref[...] = (graw + tail).astype(BF16)
        carry[...] = grawp * has_prev
        dht[...] = ght
        for i in range(4):
            dcw[pl.ds(i, 1), :] += gcw[i]
        for ref, val in ((dcb, gcb), (ddtb, gdtb), (dalog, galog), (ddsk, gdsk)):
            ref[...] += val

    head_out = pl.BlockSpec((None, 1, 128), lambda g, c: (g, 0, 0))
    sds = jax.ShapeDtypeStruct
    return _pcall(
        body, name="ssd_bwd", grid=(N_GROUPS, nc),
        in_specs=in_specs + [
            pl.BlockSpec((None, None, D_STATE, GROUP_X), lambda g, c: (g, c_of(c), 0, 0)),
            pl.BlockSpec((CHUNK, GROUP_X), lambda g, c: (c_of(c), g)),
            _ANY],
        out_specs=[pl.BlockSpec((CHUNK, GROUP_COLS), lambda g, c: (c_of(c), g)),
                   pl.BlockSpec((4, GROUP_CONV), lambda g, c: (0, g)),
                   pl.BlockSpec((1, GROUP_CONV), lambda g, c: (0, g)),
                   head_out, head_out, head_out],
        out_shape=[sds((t, ZX_COLS), BF16), sds((4, N_GROUPS * GROUP_CONV), F32), sds((1, N_GROUPS * GROUP_CONV), F32),
                   sds((N_GROUPS, 1, 128), F32), sds((N_GROUPS, 1, 128), F32), sds((N_GROUPS, 1, 128), F32)],
        scratch_shapes=[pltpu.VMEM((D_STATE, GROUP_X), F32), pltpu.VMEM((CONV_HALO, GROUP_COLS), F32)],
        sem=("arbitrary", "arbitrary"), args=(zx, zx, conv_w, conv_b, dtb, alog, dsk, hs, dy, dzx),
        aliases={9: 0}, carried=carried)


def _gate_norm(y, zs, ng):
    outs = []
    for k in range(N_GROUPS):
        s = y[:, k * GROUP_X:(k + 1) * GROUP_X] * _silu(zs[k])
        outs.append(s * lax.rsqrt(jnp.mean(s * s, axis=-1, keepdims=True) + RMS_EPS))
    return jnp.concatenate(outs, axis=1) * ng


def _z_specs(tt):
    first = Z_OFF // GROUP_X
    return [pl.BlockSpec((tt, GROUP_X), functools.partial(lambda k, i: (i, first + k), k)) for k in range(N_GROUPS)]


def _ssm_out_fwd(y, zx, ng, w_out, h):
    t = h.shape[0]
    tt = min(t, 256)

    def body(y_ref, z0, z1, z2, z3, ng_ref, w_ref, h_ref, o_ref):
        yn = _gate_norm(y_ref[...], (z0[...], z1[...], z2[...], z3[...]), ng_ref[...])
        o_ref[...] = h_ref[...] + jnp.dot(yn.astype(BF16), w_ref[...], preferred_element_type=F32)

    return pl.pallas_call(
        body, name="ssm_out_fwd", grid=(t // tt,),
        in_specs=[pl.BlockSpec((tt, D_INNER), lambda i: (i, 0))] + _z_specs(tt) + [
            pl.BlockSpec((1, D_INNER), lambda i: (0, 0)),
            pl.BlockSpec((D_INNER, D_MODEL), lambda i: (0, 0)),
            pl.BlockSpec((tt, D_MODEL), lambda i: (i, 0))],
        out_specs=pl.BlockSpec((tt, D_MODEL), lambda i: (i, 0)),
        out_shape=jax.ShapeDtypeStruct((t, D_MODEL), F32),
        compiler_params=_cp(("arbitrary",)),
    )(y, zx, zx, zx, zx, ng, w_out, h)


def _gate_norm_group(y, z, ng):
    s = y * _silu(z)
    return s * lax.rsqrt(jnp.mean(s * s, axis=-1, keepdims=True) + RMS_EPS) * ng


def _ssm_out_bwd(dhb, y, zx, ng, w_out):
    t = dhb.shape[0]
    tt = min(t, 512)
    first = Z_OFF // GROUP_X

    def body(dh_ref, y_ref, z_ref, ng_ref, w_ref, dy_ref, dzx_ref, yn_ref, dng_ref):
        @pl.when(pl.program_id(1) == 0)
        def _():
            dng_ref[...] = jnp.zeros_like(dng_ref)

        dyn = lax.dot_general(dh_ref[...], w_ref[...], _NT, preferred_element_type=F32)
        yn, vjp = jax.vjp(_gate_norm_group, y_ref[...], z_ref[...], ng_ref[...])
        dy, dz, dng = vjp(dyn)
        dy_ref[...] = dy
        dzx_ref[...] = dz.astype(BF16)
        yn_ref[...] = yn.astype(BF16)
        dng_ref[...] += dng

    grp = pl.BlockSpec((tt, GROUP_X), lambda k, i: (i, k))
    zgrp = pl.BlockSpec((tt, GROUP_X), lambda k, i: (i, first + k))
    gain = pl.BlockSpec((1, GROUP_X), lambda k, i: (0, k))
    return pl.pallas_call(
        body, name="ssm_out_bwd", grid=(N_GROUPS, t // tt),
        in_specs=[pl.BlockSpec((tt, D_MODEL), lambda k, i: (i, 0)), grp, zgrp, gain,
                  pl.BlockSpec((GROUP_X, D_MODEL), lambda k, i: (k, 0))],
        out_specs=[grp, zgrp, grp, gain],
        out_shape=[jax.ShapeDtypeStruct((t, D_INNER), F32), jax.ShapeDtypeStruct((t, ZX_COLS), BF16),
                   jax.ShapeDtypeStruct((t, D_INNER), BF16), jax.ShapeDtypeStruct((1, D_INNER), F32)],
        compiler_params=_cp(("arbitrary", "arbitrary")),
    )(dhb, y, zx, ng, w_out)


def _final(h, g, tgt):
    t = h.shape[0]
    tt = min(t, 512)
    nt = t // tt

    def body(h_ref, g_ref, t_ref, dh_ref, dhb_ref, loss_ref, dg_ref, lacc):
        i = pl.program_id(0)

        @pl.when(i == 0)
        def _():
            dg_ref[...] = jnp.zeros_like(dg_ref)
            lacc[...] = jnp.zeros_like(lacc)

        gv = g_ref[...]
        y, n, r = _rms_fwd(h_ref[...], gv)
        err = y - t_ref[...]
        lacc[...] += jnp.sum(err * err, axis=0, keepdims=True)
        dx, dg = _rms_bwd(err * (1.0 / D_MODEL), n, r, gv)
        dh_ref[...] = dx
        dhb_ref[...] = dx.astype(BF16)
        dg_ref[...] += dg

        @pl.when(i == nt - 1)
        def _():
            loss_ref[...] = jnp.zeros_like(loss_ref) + (0.5 / D_MODEL) * jnp.sum(lacc[...])

    tile = pl.BlockSpec((tt, D_MODEL), lambda i: (i, 0))
    vec = pl.BlockSpec((1, D_MODEL), lambda i: (0, 0))
    return pl.pallas_call(
        body, name="final_loss", grid=(nt,),
        in_specs=[tile, vec, tile],
        out_specs=[tile, tile, pl.BlockSpec((1, 128), lambda i: (0, 0)), vec],
        out_shape=[jax.ShapeDtypeStruct((t, D_MODEL), F32), jax.ShapeDtypeStruct((t, D_MODEL), BF16),
                   jax.ShapeDtypeStruct((1, 128), F32), jax.ShapeDtypeStruct((1, D_MODEL), F32)],
        scratch_shapes=[pltpu.VMEM((1, D_MODEL), F32)],
        compiler_params=_cp(("arbitrary",)),
    )(h, g, tgt)


def _all_reduce_small(sp):
    rows, n = sp.shape

    def body(x_ref, o_ref, land, send_sems, recv_sems):
        x, y, c = _place()
        me = 4 * x + 2 * y + c
        land[me] = x_ref[...]
        cps = []
        for rel in range(1, N_DEV):
            dx, dy, dc = (rel >> 2) & 1, (rel >> 1) & 1, rel & 1
            px = x + dx - 2 * x * dx
            py = y + dy - 2 * y * dy
            pc = c + dc - 2 * c * dc
            peer = 4 * px + 2 * py + pc
            cps.append((pltpu.make_async_remote_copy(
                src_ref=x_ref, dst_ref=land.at[me], send_sem=send_sems.at[rel - 1], recv_sem=recv_sems.at[rel - 1],
                device_id=(px, py, pc), device_id_type=MESH),
                pltpu.make_async_remote_copy(
                src_ref=x_ref, dst_ref=land.at[peer], send_sem=send_sems.at[rel - 1], recv_sem=recv_sems.at[rel - 1],
                device_id=(px, py, pc), device_id_type=MESH)))
        for cp, _ in cps:
            cp.start()
        for _, arr in cps:
            arr.wait_recv()
        for cp, _ in cps:
            cp.wait_send()
        acc = land[0]
        for k in range(1, N_DEV):
            acc = acc + land[k]
        o_ref[...] = acc

    vm = pl.BlockSpec(memory_space=pltpu.VMEM)
    return pl.pallas_call(
        body, name="all_reduce_small",
        out_shape=jax.ShapeDtypeStruct((rows, n), F32),
        in_specs=[vm], out_specs=vm,
        scratch_shapes=[pltpu.VMEM((N_DEV, rows, n), F32),
                        pltpu.SemaphoreType.DMA((N_DEV - 1,)), pltpu.SemaphoreType.DMA((N_DEV - 1,))],
    )(sp)


def _adamw_math(wv, gv, mv, vv):
    m2 = ADAM_B1 * mv + (1.0 - ADAM_B1) * gv
    v2 = ADAM_B2 * vv + (1.0 - ADAM_B2) * (gv * gv)
    m_hat = m2 / (1.0 - ADAM_B1 ** ADAM_STEP)
    v_hat = v2 / (1.0 - ADAM_B2 ** ADAM_STEP)
    return -ADAM_LR * (m_hat / (jnp.sqrt(v_hat) + ADAM_EPS) + ADAM_WD * wv), m2, v2


def _adamw(w, g, m, v, name):
    rows, cols = w.shape
    br = rows if rows <= 256 else 256

    def body(w_ref, g_ref, m_ref, v_ref, d_ref, m2_ref, v2_ref):
        d_ref[...], m2_ref[...], v2_ref[...] = _adamw_math(w_ref[...], g_ref[...], m_ref[...], v_ref[...])

    spec = pl.BlockSpec((br, cols), lambda i: (i, 0))
    out = jax.ShapeDtypeStruct((rows, cols), F32)
    return pl.pallas_call(
        body, name=name, grid=(rows // br,),
        in_specs=[spec] * 4, out_specs=[spec] * 3, out_shape=[out] * 3,
        compiler_params=_cp(("arbitrary",)),
    )(w, g, m, v)


def _adamw_reduced(w, land, m, v, name):
    rows, cols = w.shape
    br = rows if rows <= 256 else 256
    nl = land.shape[0]

    def body(w_ref, l_ref, m_ref, v_ref, g_ref, d_ref, m2_ref, v2_ref):
        gv = l_ref[0].astype(F32)
        for q in range(1, nl):
            gv = gv + l_ref[q].astype(F32)
        g_ref[...] = gv
        d_ref[...], m2_ref[...], v2_ref[...] = _adamw_math(w_ref[...], gv, m_ref[...], v_ref[...])

    spec = pl.BlockSpec((br, cols), lambda i: (i, 0))
    out = jax.ShapeDtypeStruct((rows, cols), F32)
    return pl.pallas_call(
        body, name=name, grid=(rows // br,),
        in_specs=[spec, pl.BlockSpec((nl, br, cols), lambda i: (0, i, 0)), spec, spec],
        out_specs=[spec] * 4, out_shape=[out] * 4,
        compiler_params=_cp(("arbitrary",)),
    )(w, land, m, v)


def _zx_source_col(col):
    blk = jnp.right_shift(col, 7)
    lane = jnp.bitwise_and(col, 127)
    per = GROUP_COLS // 128
    grp = jnp.where(blk >= per, 1, 0) + jnp.where(blk >= 2 * per, 1, 0) + jnp.where(blk >= 3 * per, 1, 0)
    o = blk - per * grp
    x_col = D_INNER + GROUP_X * grp + 128 * o + lane
    b_col = 2 * D_INNER + D_STATE * grp + lane
    c_col = 2 * D_INNER + N_GROUPS * D_STATE + D_STATE * grp + lane
    dt_col = jnp.where(lane < HEADS_PER_GROUP, D_INNER + CONV_DIM + HEADS_PER_GROUP * grp + lane, -1)
    src = jnp.where(o < 4, x_col, jnp.where(o == 4, b_col, jnp.where(o == 5, c_col, dt_col)))
    return jnp.where(col >= Z_OFF, col - Z_OFF, src)


def _zx_source_col_py(col):
    if col >= Z_OFF:
        return col - Z_OFF
    grp, o = divmod(col, GROUP_COLS)
    if o < GROUP_X:
        return D_INNER + GROUP_X * grp + o
    if o < GROUP_X + D_STATE:
        return 2 * D_INNER + D_STATE * grp + (o - GROUP_X)
    if o < GROUP_CONV:
        return 2 * D_INNER + N_GROUPS * D_STATE + D_STATE * grp + (o - GROUP_X - D_STATE)
    h = o - GROUP_CONV
    return D_INNER + CONV_DIM + HEADS_PER_GROUP * grp + h if h < HEADS_PER_GROUP else -1


def _overlap_tables():
    nblk = ZX_COLS // COL_BLK
    src = [_zx_source_col_py(c) for c in range(ZX_COLS)]
    fwd = [sorted({s // W_IN_SHARD for s in src[COL_BLK * j:COL_BLK * (j + 1)] if s >= 0}) for j in range(nblk)]
    dst = {s: c for c, s in enumerate(src) if s >= 0}
    bwd = [sorted({dst[s] // COL_BLK for s in range(W_IN_SHARD * k, W_IN_SHARD * (k + 1))}) for k in range(N_DEV)]

    def flat(rows):
        width = max(len(r) for r in rows)
        idx = [r + [r[-1]] * (width - len(r)) for r in rows]
        val = [[1] * len(r) + [0] * (width - len(r)) for r in rows]
        return (jnp.asarray(sum(idx, []), jnp.int32), jnp.asarray(sum(val, []), jnp.int32), width)

    return flat(fwd), flat(bwd)


def _w_in_to_zx(w_in_g):
    (tab, val, width), _ = _overlap_tables()
    nblk = ZX_COLS // COL_BLK

    def body(tab_ref, val_ref, w_ref, o_ref, acc):
        j = pl.program_id(0)
        s = pl.program_id(1)

        @pl.when(s == 0)
        def _():
            acc[...] = jnp.zeros_like(acc)

        @pl.when(val_ref[j * width + s] == 1)
        def _():
            k = tab_ref[j * width + s]
            col = COL_BLK * j + lax.broadcasted_iota(jnp.int32, (W_IN_SHARD, COL_BLK), 1)
            row = W_IN_SHARD * k + lax.broadcasted_iota(jnp.int32, (W_IN_SHARD, COL_BLK), 0)
            place = _one(_zx_source_col(col) == row).astype(BF16)
            acc[...] += jnp.dot(w_ref[...], place, preferred_element_type=F32)

        @pl.when(s == width - 1)
        def _():
            o_ref[...] = acc[...].astype(BF16)

    return pl.pallas_call(
        body, name="w_in_to_zx",
        grid_spec=pltpu.PrefetchScalarGridSpec(
            num_scalar_prefetch=2, grid=(nblk, width),
            in_specs=[pl.BlockSpec((None, D_MODEL, W_IN_SHARD), lambda j, s, tab, val: (tab[j * width + s], 0, 0))],
            out_specs=pl.BlockSpec((D_MODEL, COL_BLK), lambda j, s, tab, val: (0, j)),
            scratch_shapes=[pltpu.VMEM((D_MODEL, COL_BLK), F32)]),
        out_shape=jax.ShapeDtypeStruct((D_MODEL, ZX_COLS), BF16),
        compiler_params=_cp(("arbitrary", "arbitrary")),
    )(tab, val, w_in_g)


def _zx_to_w_in(d_wzx):
    _, (tab, val, width) = _overlap_tables()

    def body(tab_ref, val_ref, d_ref, o_ref, acc):
        k = pl.program_id(0)
        s = pl.program_id(1)

        @pl.when(s == 0)
        def _():
            acc[...] = jnp.zeros_like(acc)

        @pl.when(val_ref[k * width + s] == 1)
        def _():
            j = tab_ref[k * width + s]
            col = COL_BLK * j + lax.broadcasted_iota(jnp.int32, (COL_BLK, W_IN_SHARD), 0)
            row = W_IN_SHARD * k + lax.broadcasted_iota(jnp.int32, (COL_BLK, W_IN_SHARD), 1)
            place = _one(_zx_source_col(col) == row).astype(BF16)
            acc[...] += jnp.dot(d_ref[...], place, preferred_element_type=F32)

        @pl.when(s == width - 1)
        def _():
            o_ref[...] = acc[...].astype(BF16)

    return pl.pallas_call(
        body, name="zx_to_w_in",
        grid_spec=pltpu.PrefetchScalarGridSpec(
            num_scalar_prefetch=2, grid=(N_DEV, width),
            in_specs=[pl.BlockSpec((D_MODEL, COL_BLK), lambda k, s, tab, val: (0, tab[k * width + s]))],
            out_specs=pl.BlockSpec((None, D_MODEL, W_IN_SHARD), lambda k, s, tab, val: (k, 0, 0)),
            scratch_shapes=[pltpu.VMEM((D_MODEL, W_IN_SHARD), F32)]),
        out_shape=jax.ShapeDtypeStruct((N_DEV, D_MODEL, W_IN_SHARD), BF16),
        compiler_params=_cp(("arbitrary", "arbitrary")),
    )(tab, val, d_wzx)


def _group_conv_cols(a):
    rows = a.shape[0]
    x = a[:, :D_INNER].reshape(rows, N_GROUPS, GROUP_X)
    b = a[:, D_INNER:D_INNER + N_GROUPS * D_STATE].reshape(rows, N_GROUPS, D_STATE)
    c = a[:, D_INNER + N_GROUPS * D_STATE:].reshape(rows, N_GROUPS, D_STATE)
    return jnp.concatenate([x, b, c], axis=2).reshape(rows, N_GROUPS * GROUP_CONV)


def _ungroup_conv_cols(a):
    rows = a.shape[0]
    a3 = a.reshape(rows, N_GROUPS, GROUP_CONV)
    return jnp.concatenate([a3[:, :, :GROUP_X].reshape(rows, D_INNER),
                            a3[:, :, GROUP_X:GROUP_X + D_STATE].reshape(rows, N_GROUPS * D_STATE),
                            a3[:, :, GROUP_X + D_STATE:].reshape(rows, N_GROUPS * D_STATE)], axis=1)


def _small_shard(conv_w, conv_b, norm_g):
    ng = jnp.pad(norm_g.reshape(1, -1), ((0, 0), (0, CONV_SHARD - norm_g.shape[-1])))
    return jnp.concatenate([conv_w.reshape(4, CONV_SHARD), conv_b.reshape(1, CONV_SHARD), ng,
                            jnp.zeros((SMALL_ROWS - 6, CONV_SHARD), F32)], axis=0)


def _small_unshard(a):
    return a[0:4].reshape(1, 4, CONV_SHARD), a[4:5], a[5:6, :D_INNER // N_DEV]


def _heads_of(a):
    return a[:, :, :HEADS_PER_GROUP].reshape(1, N_HEADS)


def _head_params(p):
    return jnp.pad(p.reshape(N_GROUPS, 1, HEADS_PER_GROUP), ((0, 0), (0, 0), (0, 128 - HEADS_PER_GROUP)))


def _update(w, land, m, v, name):
    shp = w.shape
    to2 = lambda a: a.reshape(-1, shp[-1])
    return tuple(o.reshape(shp) for o in _adamw_reduced(to2(w), land, to2(m), to2(v), name))


def kernel(x, norm_mix_g, norm_mlp_g, pool_w, pool_b, pool_scale, ssm_w_in, ssm_conv_w, ssm_conv_b, ssm_dt_bias, ssm_a_log, ssm_d, ssm_norm_g, ssm_w_out, mlp_w1, mlp_w2, final_g, loss_target, m_norm_mix_g, m_norm_mlp_g, m_pool_w, m_pool_b, m_pool_scale, m_ssm_w_in, m_ssm_conv_w, m_ssm_conv_b, m_ssm_dt_bias, m_ssm_a_log, m_ssm_d, m_ssm_norm_g, m_ssm_w_out, m_mlp_w1, m_mlp_w2, m_final_g, v_norm_mix_g, v_norm_mlp_g, v_pool_w, v_pool_b, v_pool_scale, v_ssm_w_in, v_ssm_conv_w, v_ssm_conv_b, v_ssm_dt_bias, v_ssm_a_log, v_ssm_d, v_ssm_norm_g, v_ssm_w_out, v_mlp_w1, v_mlp_w2, v_final_g):
    x2 = x[0]
    tgt = loss_target[0]
    gm0, gm1 = norm_mix_g[0:1], norm_mix_g[1:2]
    gl0, gl1 = norm_mlp_g[0:1], norm_mlp_g[1:2]
    gfin = final_g.reshape(1, D_MODEL)

    fb = D_FF // N_DEV

    def bf(a):
        return a.astype(BF16)

    def gather_of(shards):
        return _direct_exchange(shards, [(i, 0) for i in range(len(shards))],
                                [(s.shape, s.dtype) for s in shards], scatter=False)

    def scatter_of(parts):
        return _direct_exchange(parts, [(i, 0) for i in range(len(parts))],
                                [(p.shape[1:], p.dtype) for p in parts], scatter=True)

    w_pool, small_g = _run_exchange(_two_level_gather(
        [bf(pool_w.reshape(4 * POOL_SHARD, POOL_GROUP)), _small_shard(ssm_conv_w, ssm_conv_b, ssm_norm_g)]),
        "gather_first")
    conv_w = _group_conv_cols(small_g[:, 0:4].transpose(1, 0, 2).reshape(4, CONV_DIM))
    conv_b = _group_conv_cols(small_g[:, 4].reshape(1, CONV_DIM))
    ssm_ng = small_g[:, 5, :D_INNER // N_DEV].reshape(1, D_INNER)
    dtb, alog, dsk = _head_params(ssm_dt_bias), _head_params(ssm_a_log), _head_params(ssm_d)

    (h1,), (w1g0, w2g0) = _pool_fwd(x2, gm0, w_pool, pool_b, pool_scale,
                                    carried=_two_level_gather([bf(mlp_w1[0]), bf(mlp_w2[0])]))
    (h2, u0, hm0), (w_in_g,) = _mlp_fwd(h1, gl0, w1g0, w2g0, "mlp0_fwd",
                                        carried=_two_level_gather([bf(ssm_w_in[0])]))
    w_zx = _w_in_to_zx(w_in_g)
    (zx, hn1), (w_out_g,) = _norm_matmul(h2, gm1, w_zx, carried=gather_of([bf(ssm_w_out[0])]))
    (y_ssd, states), (w1g1, w2g1) = _ssd_fwd(zx, conv_w, conv_b, dtb, alog, dsk,
                                             carried=_two_level_gather([bf(mlp_w1[1]), bf(mlp_w2[1])]))
    w_out = w_out_g.reshape(D_INNER, D_MODEL)
    h3 = _ssm_out_fwd(y_ssd, zx, ssm_ng, w_out, h2)
    (h4, u1, hm1), _ = _mlp_fwd(h3, gl1, w1g1, w2g1, "mlp1_fwd")
    dh4, dh4b, loss_row, d_gfin = _final(h4, gfin, tgt)

    (dh3, dh3b, da1, d_gl1), _ = _mlp_bwd(dh4, dh4b, h3, gl1, u1, w1g1, w2g1, "mlp1_bwd")
    d_w1_1 = _matmul_tn(hm1, da1, "mlp1_dw1", col_blocked=True)
    d_w2_1 = _matmul_tn(u1, dh4b, "mlp1_dw2", square_a=True).reshape(N_DEV, fb, D_MODEL)
    dy_ssd, dzx, yn, d_ng = _ssm_out_bwd(dh3b, y_ssd, zx, ssm_ng, w_out)
    d_wout = _matmul_tn(yn, dh3b, "ssm_dw_out").reshape(N_DEV, D_INNER // N_DEV, D_MODEL)
    (dzx, d_cw, d_cb, d_dtb, d_alog, d_dsk), (l_w1_1, l_w2_1, l_wout) = _ssd_bwd(
        zx, conv_w, conv_b, dtb, alog, dsk, states, dy_ssd, dzx, carried=scatter_of([d_w1_1, d_w2_1, d_wout]))
    d_w_in = _zx_to_w_in(_matmul_tn(hn1, dzx, "ssm_dw_in"))
    (dh2, dh2b, d_gm1), (l_w_in,) = _in_proj_bwd(dzx, w_zx, h2, gm1, dh3, carried=scatter_of([d_w_in]))
    d_w2_0 = _matmul_tn(u0, dh2b, "mlp0_dw2", square_a=True).reshape(N_DEV, fb, D_MODEL)
    (dh1, _, da0, d_gl0), (l_w2_0,) = _mlp_bwd(dh2, dh2b, h1, gl0, u0, w1g0, w2g0, "mlp0_bwd",
                                           carried=scatter_of([d_w2_0]))
    d_w1_0 = _matmul_tn(hm0, da0, "mlp0_dw1", col_blocked=True)
    (dx, d_pool, d_pb, d_ps, d_gm0), (l_w1_0,) = _pool_bwd(x2, dh1, gm0, w_pool, pool_b, pool_scale,
                                                          carried=scatter_of([d_w1_0]))

    d_conv_w = _ungroup_conv_cols(d_cw).reshape(4, N_DEV, CONV_SHARD).transpose(1, 0, 2)
    d_conv_b = _ungroup_conv_cols(d_cb).reshape(N_DEV, 1, CONV_SHARD)
    d_gain = jnp.pad(d_ng.reshape(N_DEV, 1, D_INNER // N_DEV), ((0, 0), (0, 0), (0, CONV_SHARD - D_INNER // N_DEV)))
    d_small = jnp.concatenate([d_conv_w, d_conv_b, d_gain,
                               jnp.zeros((N_DEV, SMALL_ROWS - 6, CONV_SHARD), F32)], axis=1)
    l_pool, l_small = _run_exchange(scatter_of([bf(d_pool), d_small]), "reduce_scatter_tail")

    heads = jnp.concatenate([_heads_of(a) for a in (d_dtb, d_alog, d_dsk)], axis=1)
    sp = jnp.concatenate([d_gm0, d_gm1, d_gl0, d_gl1, d_pb, d_ps, d_gfin,
                          jnp.pad(heads, ((0, 0), (0, D_MODEL - 3 * N_HEADS)))], axis=0)
    sg = _all_reduce_small(sp)

    g_norm_mix = sg[0:2]
    g_norm_mlp = sg[2:4]
    g_pool_b, g_pool_scale = sg[4:5], sg[5:6]
    g_final = sg[6]
    g_dtb, g_alog, g_dsk = sg[7:8, 0:32], sg[7:8, 32:64], sg[7:8, 64:96]

    def rep_pack(nm, nl, pb, ps, fg, db, al, dk):
        hd = jnp.pad(jnp.concatenate([db, al, dk], axis=1), ((0, 0), (0, D_MODEL - 3 * N_HEADS)))
        return jnp.concatenate([nm, nl, pb, ps, fg.reshape(1, D_MODEL), hd], axis=0)

    rep = [rep_pack(*t) for t in (
        (norm_mix_g, norm_mlp_g, pool_b, pool_scale, final_g, ssm_dt_bias, ssm_a_log, ssm_d),
        (g_norm_mix, g_norm_mlp, g_pool_b, g_pool_scale, g_final, g_dtb, g_alog, g_dsk),
        (m_norm_mix_g, m_norm_mlp_g, m_pool_b, m_pool_scale, m_final_g, m_ssm_dt_bias, m_ssm_a_log, m_ssm_d),
        (v_norm_mix_g, v_norm_mlp_g, v_pool_b, v_pool_scale, v_final_g, v_ssm_dt_bias, v_ssm_a_log, v_ssm_d))]
    rep_out = _adamw(*rep, "adamw_replicated")

    def rep_unpack(a):
        return (a[0:2], a[2:4], a[4:5], a[5:6], a[6], a[7:8, 0:32], a[7:8, 32:64], a[7:8, 64:96])

    sm_out = _adamw_reduced(_small_shard(ssm_conv_w, ssm_conv_b, ssm_norm_g), l_small,
                            _small_shard(m_ssm_conv_w, m_ssm_conv_b, m_ssm_norm_g),
                            _small_shard(v_ssm_conv_w, v_ssm_conv_b, v_ssm_norm_g), "adamw_small_shards")

    def update_layers(w, lands, m, v, name):
        per = [_update(w[l], lands[l], m[l], v[l], name + str(l)) for l in range(2)]
        return tuple(jnp.stack([per[0][k], per[1][k]]) for k in range(4))

    big = {
        "pool_w": _update(pool_w, l_pool, m_pool_w, v_pool_w, "adamw_pool_w"),
        "ssm_w_in": _update(ssm_w_in, l_w_in, m_ssm_w_in, v_ssm_w_in, "adamw_w_in"),
        "ssm_w_out": _update(ssm_w_out, l_wout, m_ssm_w_out, v_ssm_w_out, "adamw_w_out"),
        "mlp_w1": update_layers(mlp_w1, (l_w1_0, l_w1_1), m_mlp_w1, v_mlp_w1, "adamw_w1_"),
        "mlp_w2": update_layers(mlp_w2, (l_w2_0, l_w2_1), m_mlp_w2, v_mlp_w2, "adamw_w2_"),
    }
    rep_all = (rep[1],) + tuple(rep_out)

    def ordered(kind):
        nm, nl, pb, ps, fg, db, al, dk = rep_unpack(rep_all[kind])
        cw, cb, ng = _small_unshard(sm_out[kind])
        return [nm, nl, big["pool_w"][kind], pb, ps, big["ssm_w_in"][kind], cw, cb, db, al, dk, ng,
                big["ssm_w_out"][kind], big["mlp_w1"][kind], big["mlp_w2"][kind], fg]

    loss = lax.psum(loss_row[0, 0], ("x", "y", "c"))
    return (loss, dx[None], *ordered(0), *ordered(1), *ordered(2), *ordered(3))
```

```python
import functools

import jax
import jax.numpy as jnp
from jax import lax
from jax.experimental import pallas as pl
from jax.experimental.pallas import tpu as pltpu

F32 = jnp.float32
BF16 = jnp.bfloat16
MESH = pl.DeviceIdType.MESH

D_MODEL = 1024
RMS_EPS = 1e-5
POOL_WINDOWS = (2, 4, 8, 16)
POOL_GROUP = 256
POOL_HALO = 16
POOL_SHARD = POOL_GROUP // 8
D_INNER = 2048
HEAD_DIM = 64
N_HEADS = 32
N_GROUPS = 4
HEADS_PER_GROUP = 8
D_STATE = 128
CHUNK = 128
CONV_DIM = 3072
IN_PROJ_DIM = 5152
D_FF = 4096
N_DEV = 8
GROUP_X = HEADS_PER_GROUP * HEAD_DIM
GROUP_CONV = GROUP_X + 2 * D_STATE
GROUP_COLS = GROUP_CONV + 128
Z_OFF = N_GROUPS * GROUP_COLS
ZX_COLS = Z_OFF + D_INNER
COL_BLK = 512
W_IN_SHARD = IN_PROJ_DIM // N_DEV

ADAM_LR = 0.001
ADAM_B1 = 0.9
ADAM_B2 = 0.999
ADAM_EPS = 1e-08
ADAM_WD = 0.01
ADAM_STEP = 10

VMEM_LIMIT_V7X = 56 * 1024 * 1024
MID_STEP_PERCENT = 70
TN_TOKENS = 512
TN_ACC_BYTES = 16 * 1024 * 1024
MATMUL_TOKENS = 1024

CONV_SHARD = CONV_DIM // N_DEV
SMALL_ROWS = 8

_NN = (((1,), (0,)), ((), ()))
_NT = (((1,), (1,)), ((), ()))
_TN = (((0,), (0,)), ((), ()))


def _cp(sem):
    return pltpu.CompilerParams(dimension_semantics=sem, vmem_limit_bytes=VMEM_LIMIT_V7X)


_ANY = pl.BlockSpec(memory_space=pl.ANY)


def _place():
    return lax.axis_index("x"), lax.axis_index("y"), lax.axis_index("c")


class _Carried:
    def __init__(self, ins, outs, sems, start, finish, mid=None):
        self.ins, self.outs, self.sems = list(ins), list(outs), list(sems)
        self.start, self.mid, self.finish = start, mid, finish


def _pcall(body, *, name, grid, in_specs, out_specs, out_shape, sem, args, scratch_shapes=(), carried=None,
           aliases=None):
    in_specs, out_specs, out_shape, scratch = list(in_specs), list(out_specs), list(out_shape), list(scratch_shapes)
    common = dict(name=name, grid=grid, input_output_aliases=aliases or {}, compiler_params=_cp(sem))
    if carried is None:
        res = pl.pallas_call(body, in_specs=in_specs, out_specs=out_specs, out_shape=out_shape,
                             scratch_shapes=scratch, **common)(*args)
        return list(res), []
    n_in, n_out, n_scr = len(in_specs), len(out_specs), len(scratch)
    ci, co = len(carried.ins), len(carried.outs)

    def wrapped(*refs):
        ins, cins = refs[:n_in], refs[n_in:n_in + ci]
        p = n_in + ci
        outs, couts = refs[p:p + n_out], refs[p + n_out:p + n_out + co]
        p += n_out + co
        scr, csems = refs[p:p + n_scr], refs[p + n_scr:]
        ids = [pl.program_id(a) for a in range(len(grid))]
        first = functools.reduce(jnp.logical_and, [i == 0 for i in ids])
        last = functools.reduce(jnp.logical_and, [i == g - 1 for i, g in zip(ids, grid)])

        @pl.when(first)
        def _():
            carried.start(cins, couts, csems)

        if carried.mid is not None:
            step, steps = 0, 1
            for i, g in zip(ids, grid):
                step, steps = step * g + i, steps * g

            @pl.when(step == (steps * MID_STEP_PERCENT) // 100)
            def _():
                carried.mid(cins, couts, csems)

        body(*ins, *outs, *scr)

        @pl.when(last)
        def _():
            carried.finish(cins, couts, csems)

    res = pl.pallas_call(wrapped, in_specs=in_specs + [_ANY] * ci, out_specs=out_specs + [_ANY] * co,
                         out_shape=out_shape + carried.outs, scratch_shapes=scratch + carried.sems,
                         **common)(*args, *carried.ins)
    return list(res[:n_out]), list(res[n_out:])


def _peers(x, y, c):
    out = []
    for rel in range(1, N_DEV):
        dx, dy, dc = (rel >> 2) & 1, (rel >> 1) & 1, rel & 1
        out.append((x + dx - 2 * x * dx, y + dy - 2 * y * dy, c + dc - 2 * c * dc))
    return out


def _direct_exchange(srcs, layout, out_shapes, scatter):
    n = len(srcs)

    def copies(ins, outs, sems):
        send, recv, loc = sems
        x, y, c = _place()
        me = 4 * x + 2 * y + c
        out, arrive, local = [], [], []
        for i in range(n):
            j, off = layout[i]
            rows = srcs[i].shape[-2]
            for r, peer in enumerate(_peers(x, y, c)):
                pidx = 4 * peer[0] + 2 * peer[1] + peer[2]
                src = ins[i].at[pidx] if scatter else ins[i]
                kw = dict(send_sem=send.at[7 * i + r], recv_sem=recv.at[7 * i + r], device_id=peer, device_id_type=MESH)
                out.append(pltpu.make_async_remote_copy(src_ref=src, dst_ref=outs[j].at[me, pl.ds(off, rows)], **kw))
                arrive.append(pltpu.make_async_remote_copy(src_ref=src, dst_ref=outs[j].at[pidx, pl.ds(off, rows)], **kw))
            own = ins[i].at[me] if scatter else ins[i]
            local.append(pltpu.make_async_copy(own, outs[j].at[me, pl.ds(off, rows)], loc.at[i]))
        return out, arrive, local

    def start(ins, outs, sems):
        out, _, local = copies(ins, outs, sems)
        for cp in local + out:
            cp.start()

    def finish(ins, outs, sems):
        out, arrive, local = copies(ins, outs, sems)
        for cp in arrive:
            cp.wait_recv()
        for cp in out:
            cp.wait_send()
        for cp in local:
            cp.wait()

    return _Carried(srcs, [jax.ShapeDtypeStruct((N_DEV,) + tuple(s), d) for s, d in out_shapes],
                    [pltpu.SemaphoreType.DMA((7 * n,)), pltpu.SemaphoreType.DMA((7 * n,)),
                     pltpu.SemaphoreType.DMA((n,))], start, finish)


def _two_level_gather(shards):
    n = len(shards)

    def copies(ins, outs, sems):
        send, recv, loc = sems
        x, y, c = _place()
        me, sibling = (x, y, c), (x, y, 1 - c)
        chips = [(1 - x, y), (x, 1 - y), (1 - x, 1 - y)]

        def win(i, place):
            return outs[i].at[4 * place[0] + 2 * place[1] + place[2]]

        def copy(i, k, block, to, src=None):
            return pltpu.make_async_remote_copy(
                src_ref=win(i, block) if src is None else src, dst_ref=win(i, block),
                send_sem=send.at[7 * i + k], recv_sem=recv.at[7 * i + k], device_id=to, device_id_type=MESH)

        own, passed, ici_in, d2d_in, local = [], [], [], [], []
        for i in range(n):
            own += [copy(i, 0, me, sibling, src=ins[i])]
            own += [copy(i, 1 + j, me, (*chip, c), src=ins[i]) for j, chip in enumerate(chips)]
            passed += [copy(i, 4 + j, (*chip, c), sibling) for j, chip in enumerate(chips)]
            ici_in += [copy(i, 1 + j, (*chip, c), me) for j, chip in enumerate(chips)]
            d2d_in += [copy(i, 0, sibling, me)] + [copy(i, 4 + j, (*chip, 1 - c), me) for j, chip in enumerate(chips)]
            local.append(pltpu.make_async_copy(ins[i], win(i, me), loc.at[i]))
        return own, passed, ici_in, d2d_in, local

    def start(ins, outs, sems):
        own, _, _, _, local = copies(ins, outs, sems)
        for cp in local + own:
            cp.start()

    def mid(ins, outs, sems):
        _, passed, ici_in, _, _ = copies(ins, outs, sems)
        for arrived, onward in zip(ici_in, passed):
            arrived.wait_recv()
            onward.start()

    def finish(ins, outs, sems):
        own, passed, _, d2d_in, local = copies(ins, outs, sems)
        for cp in d2d_in:
            cp.wait_recv()
        for cp in own + passed:
            cp.wait_send()
        for cp in local:
            cp.wait()

    return _Carried(shards, [jax.ShapeDtypeStruct((N_DEV,) + tuple(s.shape), s.dtype) for s in shards],
                    [pltpu.SemaphoreType.DMA((7 * n,)), pltpu.SemaphoreType.DMA((7 * n,)),
                     pltpu.SemaphoreType.DMA((n,))], start, finish, mid)


def _run_exchange(carried, name):
    ci = len(carried.ins)

    def body(*refs):
        ins, outs, sems = refs[:ci], refs[ci:ci + len(carried.outs)], refs[ci + len(carried.outs):]
        carried.start(ins, outs, sems)
        if carried.mid is not None:
            carried.mid(ins, outs, sems)
        carried.finish(ins, outs, sems)

    return list(pl.pallas_call(body, name=name, in_specs=[_ANY] * ci, out_specs=[_ANY] * len(carried.outs),
                               out_shape=carried.outs, scratch_shapes=carried.sems)(*carried.ins))


def _dg(a, b, dn):
    return lax.dot_general(a.astype(BF16), b.astype(BF16), dn, preferred_element_type=F32)


@jax.custom_vjp
def mm_nn(a, b):
    return _dg(a, b, _NN)


@jax.custom_vjp
def mm_nt(a, b):
    return _dg(a, b, _NT)


@jax.custom_vjp
def mm_tn(a, b):
    return _dg(a, b, _TN)


mm_nn.defvjp(lambda a, b: (_dg(a, b, _NN), (a, b)), lambda r, ct: (mm_nt(ct, r[1]), mm_tn(r[0], ct)))
mm_nt.defvjp(lambda a, b: (_dg(a, b, _NT), (a, b)), lambda r, ct: (mm_nn(ct, r[1]), mm_tn(ct, r[0])))
mm_tn.defvjp(lambda a, b: (_dg(a, b, _TN), (a, b)), lambda r, ct: (mm_nt(r[1], ct), mm_nn(r[0], ct)))


def _split3(x):
    p1 = x.astype(BF16)
    r1 = x - p1.astype(F32)
    p2 = r1.astype(BF16)
    r2 = r1 - p2.astype(F32)
    return p1, p2, r2.astype(BF16)


def _exact01(x, c, dn, const_left):
    acc = None
    for p in reversed(_split3(x)):
        t = (lax.dot_general(c, p, dn, preferred_element_type=F32) if const_left
             else lax.dot_general(p, c, dn, preferred_element_type=F32))
        acc = t if acc is None else acc + t
    return acc


def _make_cmm(dn, const_left, bwd_name):
    @jax.custom_vjp
    def f(x, c):
        return _exact01(x, c, dn, const_left)

    def fwd(x, c):
        return _exact01(x, c, dn, const_left), c

    def bwd(c, ct):
        return _CMM[bwd_name](ct, c), jnp.zeros_like(c)

    f.defvjp(fwd, bwd)
    return f


_CMM = {}
_CMM["xc"] = _make_cmm(_NN, False, "xct")
_CMM["xct"] = _make_cmm(_NT, False, "xc")
_CMM["cx"] = _make_cmm(_NN, True, "ctx")
_CMM["ctx"] = _make_cmm(_TN, True, "cx")


@jax.custom_vjp
def _silu(x):
    return x / (1.0 + jnp.exp(-x))


def _silu_fwd(x):
    return _silu(x), x


def _silu_bwd(x, ct):
    s = 1.0 / (1.0 + jnp.exp(-x))
    return (ct * (s * (1.0 + x * (1.0 - s))),)


_silu.defvjp(_silu_fwd, _silu_bwd)


def _log1p_pos(e):
    u = 1.0 + e
    d = u - 1.0
    return jnp.where(d == 0.0, e, jnp.log(u) * (e / jnp.where(d == 0.0, 1.0, d)))


@jax.custom_vjp
def _softplus(x):
    return jnp.maximum(x, 0.0) + _log1p_pos(jnp.exp(-jnp.abs(x)))


def _softplus_fwd(x):
    return _softplus(x), x


def _softplus_bwd(x, ct):
    return (ct / (1.0 + jnp.exp(-x)),)


_softplus.defvjp(_softplus_fwd, _softplus_bwd)


CONV_HALO = 8


def _make_shift(j):
    @jax.custom_vjp
    def f(ext):
        return pltpu.roll(ext, j, 0)[CONV_HALO:, :]

    def fwd(ext):
        return f(ext), None

    def bwd(_, ct):
        pad = jnp.concatenate([jnp.zeros((CONV_HALO, ct.shape[1]), ct.dtype), ct], axis=0)
        return (pltpu.roll(pad, CONV_HALO + CHUNK - j, 0),)

    f.defvjp(fwd, bwd)
    return f


_SHIFT = {j: _make_shift(j) for j in (1, 2, 3)}


@jax.custom_vjp
def _swap_halves(x):
    return pltpu.roll(x, HEAD_DIM, 1)


_swap_halves.defvjp(lambda x: (_swap_halves(x), None), lambda _, ct: (pltpu.roll(ct, HEAD_DIM, 1),))


def _rms_fwd(x, g):
    r = lax.rsqrt(jnp.mean(x * x, axis=-1, keepdims=True) + RMS_EPS)
    n = x * r
    return n * g, n, r


def _rms_bwd(dy, n, r, g):
    dn = dy * g
    dx = r * (dn - n * jnp.mean(dn * n, axis=-1, keepdims=True))
    dg = jnp.sum(dy * n, axis=0, keepdims=True)
    return dx, dg


def _one(cond):
    return jnp.where(cond, 1.0, 0.0)


def _pool_tile(xe, g, ws, b, scale, tile, tt):
    r = lax.rsqrt(jnp.mean(xe * xe, axis=-1, keepdims=True) + RMS_EPS)
    hn = xe * r * g
    row_e = lax.broadcasted_iota(jnp.int32, (tt + POOL_HALO, POOL_GROUP), 0)
    keep = _one(jnp.logical_or(row_e >= POOL_HALO, tile > 0))
    rr = lax.broadcasted_iota(jnp.int32, (tt, tt + POOL_HALO), 0)
    qq = lax.broadcasted_iota(jnp.int32, (tt, tt + POOL_HALO), 1)
    dd = qq - rr
    tpos = tile * tt + lax.broadcasted_iota(jnp.int32, (tt, POOL_GROUP), 0)
    outs = []
    for gi, w in enumerate(POOL_WINDOWS):
        hg = hn[:, gi * POOL_GROUP:(gi + 1) * POOL_GROUP] * keep
        band = _one(jnp.logical_and(dd >= POOL_HALO - w + 1, dd <= POOL_HALO)).astype(BF16)
        cnt = jnp.minimum(tpos + 1, w).astype(F32)
        pooled = _CMM["cx"](hg, band) / cnt
        mixed = pooled - hg[POOL_HALO:, :]
        outs.append(mm_nn(mixed, ws[gi]))
    out = (jnp.concatenate(outs, axis=1) + b) * scale
    return xe[POOL_HALO:, :] + out


def _pool_specs(tt, nt, rev):
    per = tt // POOL_HALO
    t_of = (lambda i: nt - 1 - i) if rev else (lambda i: i)
    main = pl.BlockSpec((tt, D_MODEL), lambda i: (t_of(i), 0))
    halo = pl.BlockSpec((POOL_HALO, D_MODEL), lambda i: (jnp.maximum(t_of(i) * per - 1, 0), 0))
    vec = pl.BlockSpec((1, D_MODEL), lambda i: (0, 0))
    wsp = pl.BlockSpec((N_DEV, 4 * POOL_SHARD, POOL_GROUP), lambda i: (0, 0, 0))
    return main, halo, vec, wsp


def _pool_weights(w_ref):
    return tuple(
        jnp.concatenate([w_ref[k, gi * POOL_SHARD:(gi + 1) * POOL_SHARD, :] for k in range(N_DEV)], axis=0).astype(F32)
        for gi in range(4))


def _pool_fwd(x, g, w, b, scale, carried=None):
    t = x.shape[0]
    tt = min(t, 256)
    nt = t // tt
    main, halo, vec, wsp = _pool_specs(tt, nt, False)

    def body(xm_ref, xh_ref, g_ref, w_ref, b_ref, s_ref, o_ref):
        i = pl.program_id(0)
        xe = jnp.concatenate([xh_ref[...], xm_ref[...]], axis=0)
        o_ref[...] = _pool_tile(xe, g_ref[...], _pool_weights(w_ref), b_ref[...], s_ref[...], i, tt)

    return _pcall(
        body, name="pool_fwd", grid=(nt,),
        in_specs=[main, halo, vec, wsp, vec, vec], out_specs=[main],
        out_shape=[jax.ShapeDtypeStruct((t, D_MODEL), F32)],
        sem=("arbitrary",), args=(x, x, g, w, b, scale), carried=carried)


def _pool_bwd(x, dh, g, w, b, scale, carried=None):
    t = x.shape[0]
    tt = min(t, 256)
    nt = t // tt
    main, halo, vec, wsp = _pool_specs(tt, nt, True)

    def body(xm_ref, xh_ref, dh_ref, g_ref, w_ref, b_ref, s_ref,
             dx_ref, dw_ref, db_ref, ds_ref, dg_ref, carry, dw_acc):
        i = pl.program_id(0)
        tile = nt - 1 - i

        @pl.when(i == 0)
        def _():
            carry[...] = jnp.zeros_like(carry)
            dw_acc[...] = jnp.zeros_like(dw_acc)
            db_ref[...] = jnp.zeros_like(db_ref)
            ds_ref[...] = jnp.zeros_like(ds_ref)
            dg_ref[...] = jnp.zeros_like(dg_ref)

        xe = jnp.concatenate([xh_ref[...], xm_ref[...]], axis=0)
        _, vjp = jax.vjp(lambda a, gg, ww, bb, ss: _pool_tile(a, gg, ww, bb, ss, tile, tt),
                         xe, g_ref[...], _pool_weights(w_ref), b_ref[...], s_ref[...])
        dxe, dgv, dws, dbv, dsv = vjp(dh_ref[...])
        dx_ref[...] = dxe[POOL_HALO:, :]
        dx_ref[tt - POOL_HALO:tt, :] += carry[...]
        carry[...] = dxe[:POOL_HALO, :]
        for gi in range(4):
            dw_acc[gi] += dws[gi]
        db_ref[...] += dbv
        ds_ref[...] += dsv
        dg_ref[...] += dgv

        @pl.when(i == nt - 1)
        def _():
            for k in range(N_DEV):
                for gi in range(4):
                    dw_ref[k, gi * POOL_SHARD:(gi + 1) * POOL_SHARD, :] = dw_acc[gi, k * POOL_SHARD:(k + 1) * POOL_SHARD, :]

    return _pcall(
        body, name="pool_bwd", grid=(nt,),
        in_specs=[main, halo, main, vec, wsp, vec, vec],
        out_specs=[main, wsp, vec, vec, vec],
        out_shape=[jax.ShapeDtypeStruct((t, D_MODEL), F32),
                   jax.ShapeDtypeStruct((N_DEV, 4 * POOL_SHARD, POOL_GROUP), F32),
                   jax.ShapeDtypeStruct((1, D_MODEL), F32),
                   jax.ShapeDtypeStruct((1, D_MODEL), F32),
                   jax.ShapeDtypeStruct((1, D_MODEL), F32)],
        scratch_shapes=[pltpu.VMEM((POOL_HALO, D_MODEL), F32), pltpu.VMEM((4, POOL_GROUP, POOL_GROUP), F32)],
        sem=("arbitrary",), args=(x, x, dh, g, w, b, scale), carried=carried)


def _mlp_weight_specs():
    fb = D_FF // N_DEV
    return (pl.BlockSpec((None, D_MODEL, fb), lambda i, k: (k, 0, 0)),
            pl.BlockSpec((None, fb, D_MODEL), lambda i, k: (k, 0, 0)))


def _mlp_fwd(h, g, w1g, w2g, name, carried=None):
    t = h.shape[0]
    tt = min(t, MATMUL_TOKENS)
    nk, fb = N_DEV, D_FF // N_DEV
    w1_spec, w2_spec = _mlp_weight_specs()

    def body(h_ref, g_ref, w1_ref, w2_ref, o_ref, u_ref, hm_ref, hm_s, acc_s):
        k = pl.program_id(1)

        @pl.when(k == 0)
        def _():
            xv = h_ref[...]
            y, _, _ = _rms_fwd(xv, g_ref[...])
            hb = y.astype(BF16)
            hm_s[...] = hb
            hm_ref[...] = hb
            acc_s[...] = xv

        a = jnp.dot(hm_s[...], w1_ref[...], preferred_element_type=F32)
        u = jnp.maximum(a, 0.0)
        u_ref[...] = u.astype(BF16)
        acc_s[...] += jnp.dot((u * u).astype(BF16), w2_ref[...], preferred_element_type=F32)

        @pl.when(k == nk - 1)
        def _():
            o_ref[...] = acc_s[...]

    return _pcall(
        body, name=name, grid=(t // tt, nk),
        in_specs=[pl.BlockSpec((tt, D_MODEL), lambda i, k: (i, 0)),
                  pl.BlockSpec((1, D_MODEL), lambda i, k: (0, 0)),
                  w1_spec, w2_spec],
        out_specs=[pl.BlockSpec((tt, D_MODEL), lambda i, k: (i, 0)),
                   pl.BlockSpec((tt, fb), lambda i, k: (i, k)),
                   pl.BlockSpec((tt, D_MODEL), lambda i, k: (i, 0))],
        out_shape=[jax.ShapeDtypeStruct((t, D_MODEL), F32),
                   jax.ShapeDtypeStruct((t, nk * fb), BF16),
                   jax.ShapeDtypeStruct((t, D_MODEL), BF16)],
        scratch_shapes=[pltpu.VMEM((tt, D_MODEL), BF16), pltpu.VMEM((tt, D_MODEL), F32)],
        sem=("arbitrary", "arbitrary"), args=(h, g, w1g, w2g), carried=carried)


def _mlp_bwd(dh, dhb, h, g, u, w1g, w2g, name, carried=None):
    t = h.shape[0]
    tt = min(t, MATMUL_TOKENS)
    nk, fb = N_DEV, D_FF // N_DEV
    w1_spec, w2_spec = _mlp_weight_specs()

    def body(dh_ref, dhb_ref, h_ref, g_ref, u_ref, w1_ref, w2_ref,
             dhin_ref, dhinb_ref, da_ref, dg_ref, acc_s):
        i = pl.program_id(0)
        k = pl.program_id(1)

        @pl.when(jnp.logical_and(i == 0, k == 0))
        def _():
            dg_ref[...] = jnp.zeros_like(dg_ref)

        @pl.when(k == 0)
        def _():
            acc_s[...] = jnp.zeros_like(acc_s)

        dv = lax.dot_general(dhb_ref[...], w2_ref[...], _NT, preferred_element_type=F32)
        dab = (dv * (2.0 * u_ref[...].astype(F32))).astype(BF16)
        da_ref[...] = dab
        acc_s[...] += lax.dot_general(dab, w1_ref[...], _NT, preferred_element_type=F32)

        @pl.when(k == nk - 1)
        def _():
            gv = g_ref[...]
            _, n, r = _rms_fwd(h_ref[...], gv)
            dx, dg = _rms_bwd(acc_s[...], n, r, gv)
            dhin = dh_ref[...] + dx
            dhin_ref[...] = dhin
            dhinb_ref[...] = dhin.astype(BF16)
            dg_ref[...] += dg

    tile = pl.BlockSpec((tt, D_MODEL), lambda i, k: (i, 0))
    return _pcall(
        body, name=name, grid=(t // tt, nk),
        in_specs=[tile, tile, tile, pl.BlockSpec((1, D_MODEL), lambda i, k: (0, 0)),
                  pl.BlockSpec((tt, fb), lambda i, k: (i, k)), w1_spec, w2_spec],
        out_specs=[tile, tile, pl.BlockSpec((tt, fb), lambda i, k: (i, k)),
                   pl.BlockSpec((1, D_MODEL), lambda i, k: (0, 0))],
        out_shape=[jax.ShapeDtypeStruct((t, D_MODEL), F32),
                   jax.ShapeDtypeStruct((t, D_MODEL), BF16),
                   jax.ShapeDtypeStruct((t, nk * fb), BF16),
                   jax.ShapeDtypeStruct((1, D_MODEL), F32)],
        scratch_shapes=[pltpu.VMEM((tt, D_MODEL), F32)],
        sem=("arbitrary", "arbitrary"), args=(dh, dhb, h, g, u, w1g, w2g), carried=carried)


def _matmul_tn(a, b, name, square_a=False, col_blocked=False, carried=None):
    t, k1 = a.shape
    k2 = b.shape[1]
    tt = min(t, TN_TOKENS)
    nt = t // tt
    wc = k2 if k1 * k2 * 4 <= TN_ACC_BYTES else k2 // 2
    nb = wc // COL_BLK

    def body(a_ref, b_ref, o_ref, acc):
        s = pl.program_id(1)

        @pl.when(s == 0)
        def _():
            acc[...] = jnp.zeros_like(acc)

        av = a_ref[...]
        if square_a:
            af = av.astype(F32)
            av = (af * af).astype(BF16)
        acc[...] += lax.dot_general(av, b_ref[...], _TN, preferred_element_type=F32)

        @pl.when(s == nt - 1)
        def _():
            if col_blocked:
                for k in range(nb):
                    o_ref[k] = acc[:, k * COL_BLK:(k + 1) * COL_BLK].astype(o_ref.dtype)
            else:
                o_ref[...] = acc[...].astype(o_ref.dtype)

    if col_blocked:
        out_shape = jax.ShapeDtypeStruct((k2 // COL_BLK, k1, COL_BLK), BF16)
        out_spec = pl.BlockSpec((nb, k1, COL_BLK), lambda j, s: (j, 0, 0))
    else:
        out_shape = jax.ShapeDtypeStruct((k1, k2), BF16)
        out_spec = pl.BlockSpec((k1, wc), lambda j, s: (0, j))
    outs, landed = _pcall(
        body, name=name, grid=(k2 // wc, nt),
        in_specs=[pl.BlockSpec((tt, k1), lambda j, s: (s, 0)),
                  pl.BlockSpec((tt, wc), lambda j, s: (s, j))],
        out_specs=[out_spec], out_shape=[out_shape],
        scratch_shapes=[pltpu.VMEM((k1, wc), F32)],
        sem=("arbitrary", "arbitrary"), args=(a, b), carried=carried)
    return (outs[0], landed) if carried is not None else outs[0]


def _norm_matmul(h, g, w, carried=None):
    t = h.shape[0]
    tt = min(t, MATMUL_TOKENS)
    n = w.shape[1]

    def body(h_ref, g_ref, w_ref, o_ref, hn_ref, hn_s):
        @pl.when(pl.program_id(1) == 0)
        def _():
            y, _, _ = _rms_fwd(h_ref[...], g_ref[...])
            hb = y.astype(BF16)
            hn_s[...] = hb
            hn_ref[...] = hb

        o_ref[...] = jnp.dot(hn_s[...], w_ref[...], preferred_element_type=F32)

    return _pcall(
        body, name="ssm_in_proj", grid=(t // tt, n // COL_BLK),
        in_specs=[pl.BlockSpec((tt, D_MODEL), lambda i, j: (i, 0)),
                  pl.BlockSpec((1, D_MODEL), lambda i, j: (0, 0)),
                  pl.BlockSpec((D_MODEL, COL_BLK), lambda i, j: (0, j))],
        out_specs=[pl.BlockSpec((tt, COL_BLK), lambda i, j: (i, j)),
                   pl.BlockSpec((tt, D_MODEL), lambda i, j: (i, 0))],
        out_shape=[jax.ShapeDtypeStruct((t, n), F32), jax.ShapeDtypeStruct((t, D_MODEL), BF16)],
        scratch_shapes=[pltpu.VMEM((tt, D_MODEL), BF16)],
        sem=("arbitrary", "arbitrary"), args=(h, g, w), carried=carried)


def _in_proj_bwd(dzx, w, h, g, dh_next, carried=None):
    t = h.shape[0]
    tt = min(t, MATMUL_TOKENS)
    n = w.shape[1]
    nj = n // COL_BLK

    def body(dz_ref, w_ref, h_ref, g_ref, dn_ref, dh_ref, dhb_ref, dg_ref, acc):
        i = pl.program_id(0)
        j = pl.program_id(1)

        @pl.when(jnp.logical_and(i == 0, j == 0))
        def _():
            dg_ref[...] = jnp.zeros_like(dg_ref)

        @pl.when(j == 0)
        def _():
            acc[...] = jnp.zeros_like(acc)

        acc[...] += lax.dot_general(dz_ref[...], w_ref[...], _NT, preferred_element_type=F32)

        @pl.when(j == nj - 1)
        def _():
            gv = g_ref[...]
            _, nn, r = _rms_fwd(h_ref[...], gv)
            dx, dg = _rms_bwd(acc[...], nn, r, gv)
            dh = dn_ref[...] + dx
            dh_ref[...] = dh
            dhb_ref[...] = dh.astype(BF16)
            dg_ref[...] += dg

    tile = pl.BlockSpec((tt, D_MODEL), lambda i, j: (i, 0))
    return _pcall(
        body, name="ssm_in_proj_bwd", grid=(t // tt, nj),
        in_specs=[pl.BlockSpec((tt, COL_BLK), lambda i, j: (i, j)),
                  pl.BlockSpec((D_MODEL, COL_BLK), lambda i, j: (0, j)),
                  tile, pl.BlockSpec((1, D_MODEL), lambda i, j: (0, 0)), tile],
        out_specs=[tile, tile, pl.BlockSpec((1, D_MODEL), lambda i, j: (0, 0))],
        out_shape=[jax.ShapeDtypeStruct((t, D_MODEL), F32), jax.ShapeDtypeStruct((t, D_MODEL), BF16),
                   jax.ShapeDtypeStruct((1, D_MODEL), F32)],
        scratch_shapes=[pltpu.VMEM((tt, D_MODEL), F32)],
        sem=("arbitrary", "arbitrary"), args=(dzx, w, h, g, dh_next), carried=carried)


def _ssd_consts():
    lane = lax.broadcasted_iota(jnp.int32, (CHUNK, CHUNK), 1)
    row = lax.broadcasted_iota(jnp.int32, (CHUNK, CHUNK), 0)
    causal = lane <= row
    tri = _one(causal).astype(BF16)
    er = lax.broadcasted_iota(jnp.int32, (CHUNK, GROUP_X), 0)
    ec = lax.broadcasted_iota(jnp.int32, (CHUNK, GROUP_X), 1)
    expand = _one(jnp.right_shift(ec, 6) == er).astype(BF16)
    return dict(causal=causal, tri=tri, expand=expand, lo=lane < HEAD_DIM)


def _conv_silu(cur, prev, w, b):
    ext = jnp.concatenate([prev, cur], axis=0)
    acc = cur * w[3] + b
    for j in (1, 2, 3):
        acc = acc + _SHIFT[j](ext) * w[3 - j]
    return _silu(acc)


def _ssd_chunk(raw, rawp, ht, cw, cb_, dtb, alog, dsk, k):
    act = _conv_silu(raw[:, :GROUP_CONV], rawp[:, :GROUP_CONV], cw, cb_)
    xs = act[:, :GROUP_X]
    bm = act[:, GROUP_X:GROUP_X + D_STATE]
    cm = act[:, GROUP_X + D_STATE:]
    dt = _softplus(raw[:, GROUP_CONV:] + dtb)
    a = -jnp.exp(alog)
    xc = _CMM["xc"]

    def lanes(rowv):
        return jnp.sum(xc(jnp.broadcast_to(rowv, (16, CHUNK)), k["expand"]), axis=0, keepdims=True) * (1.0 / 16.0)

    dt_e = xc(dt, k["expand"])
    adt_e = dt_e * lanes(a)
    acs_e = _CMM["cx"](adt_e, k["tri"])
    tot_e = jnp.sum(adt_e, axis=0, keepdims=True)
    gmat = mm_nt(cm, bm)
    xdt = xs * dt_e
    ys = []
    for j in range(HEADS_PER_GROUP // 2):
        pair = acs_e[:, j * CHUNK:(j + 1) * CHUNK]
        swapped = _swap_halves(pair)
        ms = []
        for cb in (jnp.where(k["lo"], pair, swapped), jnp.where(k["lo"], swapped, pair)):
            seg = cb - cb.T
            ms.append(gmat * jnp.exp(jnp.where(k["causal"], seg, -jnp.inf)))
        xp = xdt[:, j * CHUNK:(j + 1) * CHUNK]
        rhs = jnp.concatenate([jnp.where(k["lo"], xp, 0.0), jnp.where(k["lo"], 0.0, xp)], axis=0)
        ys.append(mm_nn(jnp.concatenate(ms, axis=1), rhs))
    y_diag = jnp.concatenate(ys, axis=1)
    y_off = jnp.exp(acs_e) * mm_nn(cm, ht)
    h_new = jnp.exp(tot_e) * ht + mm_tn(bm, xdt * jnp.exp(tot_e - acs_e))
    return y_diag + y_off + lanes(dsk) * xs, h_new


def _ssd_in_specs(nc, rev):
    c_of = (lambda c: nc - 1 - c) if rev else (lambda c: c)
    per = CHUNK // CONV_HALO
    zx = [pl.BlockSpec((CHUNK, GROUP_COLS), lambda g, c: (c_of(c), g)),
          pl.BlockSpec((CONV_HALO, GROUP_COLS), lambda g, c: (jnp.maximum(c_of(c) * per - 1, 0), g))]
    conv = [pl.BlockSpec((4, GROUP_CONV), lambda g, c: (0, g)), pl.BlockSpec((1, GROUP_CONV), lambda g, c: (0, g))]
    head = [pl.BlockSpec((None, 1, 128), lambda g, c: (g, 0, 0))] * 3
    return zx + conv + head, c_of


def _load_chunk_args(refs, has_prev):
    raw, rawp, cw, cb_, dtb, alog, dsk = refs
    return (raw[...], rawp[...] * has_prev, tuple(cw[pl.ds(i, 1), :] for i in range(4)), cb_[...],
            dtb[...], alog[...], dsk[...])


def _ssd_fwd(zx, conv_w, conv_b, dtb, alog, dsk, carried=None):
    t = zx.shape[0]
    nc = t // CHUNK
    in_specs, _ = _ssd_in_specs(nc, False)

    def body(*refs):
        ins, (y_ref, hs_ref, ht) = refs[:7], refs[7:]
        c = pl.program_id(1)

        @pl.when(c == 0)
        def _():
            ht[...] = jnp.zeros_like(ht)

        a = _load_chunk_args(ins, _one(c > 0))
        h_in = ht[...]
        y, h_new = _ssd_chunk(*a[:2], h_in, *a[2:], _ssd_consts())
        y_ref[...] = y
        hs_ref[...] = h_in
        ht[...] = h_new

    return _pcall(
        body, name="ssd_fwd", grid=(N_GROUPS, nc),
        in_specs=in_specs,
        out_specs=[pl.BlockSpec((CHUNK, GROUP_X), lambda g, c: (c, g)),
                   pl.BlockSpec((None, None, D_STATE, GROUP_X), lambda g, c: (g, c, 0, 0))],
        out_shape=[jax.ShapeDtypeStruct((t, D_INNER), F32),
                   jax.ShapeDtypeStruct((N_GROUPS, nc, D_STATE, GROUP_X), F32)],
        scratch_shapes=[pltpu.VMEM((D_STATE, GROUP_X), F32)],
        sem=("arbitrary", "arbitrary"), args=(zx, zx, conv_w, conv_b, dtb, alog, dsk), carried=carried)


def _ssd_bwd(zx, conv_w, conv_b, dtb, alog, dsk, hs, dy, dzx, carried=None):
    t = zx.shape[0]
    nc = t // CHUNK
    in_specs, c_of = _ssd_in_specs(nc, True)
    n_in = 10

    def body(*refs):
        ins, hs_ref, dy_ref = refs[:7], refs[7], refs[8]
        (draw_ref, dcw, dcb, ddtb, dalog, ddsk, dht, carry) = refs[n_in:]
        cc = pl.program_id(1)
        accs = (dcw, dcb, ddtb, dalog, ddsk)

        @pl.when(cc == 0)
        def _():
            for r in (dht, carry) + accs:
                r[...] = jnp.zeros_like(r)

        has_prev = _one(c_of(cc) > 0)
        a = _load_chunk_args(ins, has_prev)
        k = _ssd_consts()
        fn = lambda *args: _ssd_chunk(*args, k)
        _, vjp = jax.vjp(fn, *a[:2], hs_ref[...], *a[2:])
        graw, grawp, ght, gcw, gcb, gdtb, galog, gdsk = vjp((dy_ref[...], dht[...]))
        tail = jnp.concatenate([jnp.zeros((CHUNK - CONV_HALO, GROUP_COLS), F32), carry[...]], axis=0)
        draw_ref[...] = (graw + tail).astype(BF16)
        carry[...] = grawp * has_prev
        dht[...] = ght
        for i in range(4):
            dcw[pl.ds(i, 1), :] += gcw[i]
        for ref, val in ((dcb, gcb), (ddtb, gdtb), (dalog, galog), (ddsk, gdsk)):
            ref[...] += val

    head_out = pl.BlockSpec((None, 1, 128), lambda g, c: (g, 0, 0))
    sds = jax.ShapeDtypeStruct
    return _pcall(
        body, name="ssd_bwd", grid=(N_GROUPS, nc),
        in_specs=in_specs + [
            pl.BlockSpec((None, None, D_STATE, GROUP_X), lambda g, c: (g, c_of(c), 0, 0)),
            pl.BlockSpec((CHUNK, GROUP_X), lambda g, c: (c_of(c), g)),
            _ANY],
        out_specs=[pl.BlockSpec((CHUNK, GROUP_COLS), lambda g, c: (c_of(c), g)),
                   pl.BlockSpec((4, GROUP_CONV), lambda g, c: (0, g)),
                   pl.BlockSpec((1, GROUP_CONV), lambda g, c: (0, g)),
                   head_out, head_out, head_out],
        out_shape=[sds((t, ZX_COLS), BF16), sds((4, N_GROUPS * GROUP_CONV), F32), sds((1, N_GROUPS * GROUP_CONV), F32),
                   sds((N_GROUPS, 1, 128), F32), sds((N_GROUPS, 1, 128), F32), sds((N_GROUPS, 1, 128), F32)],
        scratch_shapes=[pltpu.VMEM((D_STATE, GROUP_X), F32), pltpu.VMEM((CONV_HALO, GROUP_COLS), F32)],
        sem=("arbitrary", "arbitrary"), args=(zx, zx, conv_w, conv_b, dtb, alog, dsk, hs, dy, dzx),
        aliases={9: 0}, carried=carried)


def _gate_norm(y, zs, ng):
    outs = []
    for k in range(N_GROUPS):
        s = y[:, k * GROUP_X:(k + 1) * GROUP_X] * _silu(zs[k])
        outs.append(s * lax.rsqrt(jnp.mean(s * s, axis=-1, keepdims=True) + RMS_EPS))
    return jnp.concatenate(outs, axis=1) * ng


def _z_specs(tt):
    first = Z_OFF // GROUP_X
    return [pl.BlockSpec((tt, GROUP_X), functools.partial(lambda k, i: (i, first + k), k)) for k in range(N_GROUPS)]


def _ssm_out_fwd(y, zx, ng, w_out, h):
    t = h.shape[0]
    tt = min(t, 256)

    def body(y_ref, z0, z1, z2, z3, ng_ref, w_ref, h_ref, o_ref):
        yn = _gate_norm(y_ref[...], (z0[...], z1[...], z2[...], z3[...]), ng_ref[...])
        o_ref[...] = h_ref[...] + jnp.dot(yn.astype(BF16), w_ref[...], preferred_element_type=F32)

    return pl.pallas_call(
        body, name="ssm_out_fwd", grid=(t // tt,),
        in_specs=[pl.BlockSpec((tt, D_INNER), lambda i: (i, 0))] + _z_specs(tt) + [
            pl.BlockSpec((1, D_INNER), lambda i: (0, 0)),
            pl.BlockSpec((D_INNER, D_MODEL), lambda i: (0, 0)),
            pl.BlockSpec((tt, D_MODEL), lambda i: (i, 0))],
        out_specs=pl.BlockSpec((tt, D_MODEL), lambda i: (i, 0)),
        out_shape=jax.ShapeDtypeStruct((t, D_MODEL), F32),
        compiler_params=_cp(("arbitrary",)),
    )(y, zx, zx, zx, zx, ng, w_out, h)


def _gate_norm_group(y, z, ng):
    s = y * _silu(z)
    return s * lax.rsqrt(jnp.mean(s * s, axis=-1, keepdims=True) + RMS_EPS) * ng


def _ssm_out_bwd(dhb, y, zx, ng, w_out):
    t = dhb.shape[0]
    tt = min(t, 512)
    first = Z_OFF // GROUP_X

    def body(dh_ref, y_ref, z_ref, ng_ref, w_ref, dy_ref, dzx_ref, yn_ref, dng_ref):
        @pl.when(pl.program_id(1) == 0)
        def _():
            dng_ref[...] = jnp.zeros_like(dng_ref)

        dyn = lax.dot_general(dh_ref[...], w_ref[...], _NT, preferred_element_type=F32)
        yn, vjp = jax.vjp(_gate_norm_group, y_ref[...], z_ref[...], ng_ref[...])
        dy, dz, dng = vjp(dyn)
        dy_ref[...] = dy
        dzx_ref[...] = dz.astype(BF16)
        yn_ref[...] = yn.astype(BF16)
        dng_ref[...] += dng

    grp = pl.BlockSpec((tt, GROUP_X), lambda k, i: (i, k))
    zgrp = pl.BlockSpec((tt, GROUP_X), lambda k, i: (i, first + k))
    gain = pl.BlockSpec((1, GROUP_X), lambda k, i: (0, k))
    return pl.pallas_call(
        body, name="ssm_out_bwd", grid=(N_GROUPS, t // tt),
        in_specs=[pl.BlockSpec((tt, D_MODEL), lambda k, i: (i, 0)), grp, zgrp, gain,
                  pl.BlockSpec((GROUP_X, D_MODEL), lambda k, i: (k, 0))],
        out_specs=[grp, zgrp, grp, gain],
        out_shape=[jax.ShapeDtypeStruct((t, D_INNER), F32), jax.ShapeDtypeStruct((t, ZX_COLS), BF16),
                   jax.ShapeDtypeStruct((t, D_INNER), BF16), jax.ShapeDtypeStruct((1, D_INNER), F32)],
        compiler_params=_cp(("arbitrary", "arbitrary")),
    )(dhb, y, zx, ng, w_out)


def _final(h, g, tgt):
    t = h.shape[0]
    tt = min(t, 512)
    nt = t // tt

    def body(h_ref, g_ref, t_ref, dh_ref, dhb_ref, loss_ref, dg_ref, lacc):
        i = pl.program_id(0)

        @pl.when(i == 0)
        def _():
            dg_ref[...] = jnp.zeros_like(dg_ref)
            lacc[...] = jnp.zeros_like(lacc)

        gv = g_ref[...]
        y, n, r = _rms_fwd(h_ref[...], gv)
        err = y - t_ref[...]
        lacc[...] += jnp.sum(err * err, axis=0, keepdims=True)
        dx, dg = _rms_bwd(err * (1.0 / D_MODEL), n, r, gv)
        dh_ref[...] = dx
        dhb_ref[...] = dx.astype(BF16)
        dg_ref[...] += dg

        @pl.when(i == nt - 1)
        def _():
            loss_ref[...] = jnp.zeros_like(loss_ref) + (0.5 / D_MODEL) * jnp.sum(lacc[...])

    tile = pl.BlockSpec((tt, D_MODEL), lambda i: (i, 0))
    vec = pl.BlockSpec((1, D_MODEL), lambda i: (0, 0))
    return pl.pallas_call(
        body, name="final_loss", grid=(nt,),
        in_specs=[tile, vec, tile],
        out_specs=[tile, tile, pl.BlockSpec((1, 128), lambda i: (0, 0)), vec],
        out_shape=[jax.ShapeDtypeStruct((t, D_MODEL), F32), jax.ShapeDtypeStruct((t, D_MODEL), BF16),
                   jax.ShapeDtypeStruct((1, 128), F32), jax.ShapeDtypeStruct((1, D_MODEL), F32)],
        scratch_shapes=[pltpu.VMEM((1, D_MODEL), F32)],
        compiler_params=_cp(("arbitrary",)),
    )(h, g, tgt)


def _all_reduce_small(sp):
    rows, n = sp.shape

    def body(x_ref, o_ref, land, send_sems, recv_sems):
        x, y, c = _place()
        me = 4 * x + 2 * y + c
        land[me] = x_ref[...]
        cps = []
        for rel in range(1, N_DEV):
            dx, dy, dc = (rel >> 2) & 1, (rel >> 1) & 1, rel & 1
            px = x + dx - 2 * x * dx
            py = y + dy - 2 * y * dy
            pc = c + dc - 2 * c * dc
            peer = 4 * px + 2 * py + pc
            cps.append((pltpu.make_async_remote_copy(
                src_ref=x_ref, dst_ref=land.at[me], send_sem=send_sems.at[rel - 1], recv_sem=recv_sems.at[rel - 1],
                device_id=(px, py, pc), device_id_type=MESH),
                pltpu.make_async_remote_copy(
                src_ref=x_ref, dst_ref=land.at[peer], send_sem=send_sems.at[rel - 1], recv_sem=recv_sems.at[rel - 1],
                device_id=(px, py, pc), device_id_type=MESH)))
        for cp, _ in cps:
            cp.start()
        for _, arr in cps:
            arr.wait_recv()
        for cp, _ in cps:
            cp.wait_send()
        acc = land[0]
        for k in range(1, N_DEV):
            acc = acc + land[k]
        o_ref[...] = acc

    vm = pl.BlockSpec(memory_space=pltpu.VMEM)
    return pl.pallas_call(
        body, name="all_reduce_small",
        out_shape=jax.ShapeDtypeStruct((rows, n), F32),
        in_specs=[vm], out_specs=vm,
        scratch_shapes=[pltpu.VMEM((N_DEV, rows, n), F32),
                        pltpu.SemaphoreType.DMA((N_DEV - 1,)), pltpu.SemaphoreType.DMA((N_DEV - 1,))],
    )(sp)


def _adamw_math(wv, gv, mv, vv):
    m2 = ADAM_B1 * mv + (1.0 - ADAM_B1) * gv
    v2 = ADAM_B2 * vv + (1.0 - ADAM_B2) * (gv * gv)
    m_hat = m2 / (1.0 - ADAM_B1 ** ADAM_STEP)
    v_hat = v2 / (1.0 - ADAM_B2 ** ADAM_STEP)
    return -ADAM_LR * (m_hat / (jnp.sqrt(v_hat) + ADAM_EPS) + ADAM_WD * wv), m2, v2


def _adamw(w, g, m, v, name):
    rows, cols = w.shape
    br = rows if rows <= 256 else 256

    def body(w_ref, g_ref, m_ref, v_ref, d_ref, m2_ref, v2_ref):
        d_ref[...], m2_ref[...], v2_ref[...] = _adamw_math(w_ref[...], g_ref[...], m_ref[...], v_ref[...])

    spec = pl.BlockSpec((br, cols), lambda i: (i, 0))
    out = jax.ShapeDtypeStruct((rows, cols), F32)
    return pl.pallas_call(
        body, name=name, grid=(rows // br,),
        in_specs=[spec] * 4, out_specs=[spec] * 3, out_shape=[out] * 3,
        compiler_params=_cp(("arbitrary",)),
    )(w, g, m, v)


def _adamw_reduced(w, land, m, v, name):
    rows, cols = w.shape
    br = rows if rows <= 256 else 256
    nl = land.shape[0]

    def body(w_ref, l_ref, m_ref, v_ref, g_ref, d_ref, m2_ref, v2_ref):
        gv = l_ref[0].astype(F32)
        for q in range(1, nl):
            gv = gv + l_ref[q].astype(F32)
        g_ref[...] = gv
        d_ref[...], m2_ref[...], v2_ref[...] = _adamw_math(w_ref[...], gv, m_ref[...], v_ref[...])

    spec = pl.BlockSpec((br, cols), lambda i: (i, 0))
    out = jax.ShapeDtypeStruct((rows, cols), F32)
    return pl.pallas_call(
        body, name=name, grid=(rows // br,),
        in_specs=[spec, pl.BlockSpec((nl, br, cols), lambda i: (0, i, 0)), spec, spec],
        out_specs=[spec] * 4, out_shape=[out] * 4,
        compiler_params=_cp(("arbitrary",)),
    )(w, land, m, v)


def _zx_source_col(col):
    blk = jnp.right_shift(col, 7)
    lane = jnp.bitwise_and(col, 127)
    per = GROUP_COLS // 128
    grp = jnp.where(blk >= per, 1, 0) + jnp.where(blk >= 2 * per, 1, 0) + jnp.where(blk >= 3 * per, 1, 0)
    o = blk - per * grp
    x_col = D_INNER + GROUP_X * grp + 128 * o + lane
    b_col = 2 * D_INNER + D_STATE * grp + lane
    c_col = 2 * D_INNER + N_GROUPS * D_STATE + D_STATE * grp + lane
    dt_col = jnp.where(lane < HEADS_PER_GROUP, D_INNER + CONV_DIM + HEADS_PER_GROUP * grp + lane, -1)
    src = jnp.where(o < 4, x_col, jnp.where(o == 4, b_col, jnp.where(o == 5, c_col, dt_col)))
    return jnp.where(col >= Z_OFF, col - Z_OFF, src)


def _zx_source_col_py(col):
    if col >= Z_OFF:
        return col - Z_OFF
    grp, o = divmod(col, GROUP_COLS)
    if o < GROUP_X:
        return D_INNER + GROUP_X * grp + o
    if o < GROUP_X + D_STATE:
        return 2 * D_INNER + D_STATE * grp + (o - GROUP_X)
    if o < GROUP_CONV:
        return 2 * D_INNER + N_GROUPS * D_STATE + D_STATE * grp + (o - GROUP_X - D_STATE)
    h = o - GROUP_CONV
    return D_INNER + CONV_DIM + HEADS_PER_GROUP * grp + h if h < HEADS_PER_GROUP else -1


def _overlap_tables():
    nblk = ZX_COLS // COL_BLK
    src = [_zx_source_col_py(c) for c in range(ZX_COLS)]
    fwd = [sorted({s // W_IN_SHARD for s in src[COL_BLK * j:COL_BLK * (j + 1)] if s >= 0}) for j in range(nblk)]
    dst = {s: c for c, s in enumerate(src) if s >= 0}
    bwd = [sorted({dst[s] // COL_BLK for s in range(W_IN_SHARD * k, W_IN_SHARD * (k + 1))}) for k in range(N_DEV)]

    def flat(rows):
        width = max(len(r) for r in rows)
        idx = [r + [r[-1]] * (width - len(r)) for r in rows]
        val = [[1] * len(r) + [0] * (width - len(r)) for r in rows]
        return (jnp.asarray(sum(idx, []), jnp.int32), jnp.asarray(sum(val, []), jnp.int32), width)

    return flat(fwd), flat(bwd)


def _w_in_to_zx(w_in_g):
    (tab, val, width), _ = _overlap_tables()
    nblk = ZX_COLS // COL_BLK

    def body(tab_ref, val_ref, w_ref, o_ref, acc):
        j = pl.program_id(0)
        s = pl.program_id(1)

        @pl.when(s == 0)
        def _():
            acc[...] = jnp.zeros_like(acc)

        @pl.when(val_ref[j * width + s] == 1)
        def _():
            k = tab_ref[j * width + s]
            col = COL_BLK * j + lax.broadcasted_iota(jnp.int32, (W_IN_SHARD, COL_BLK), 1)
            row = W_IN_SHARD * k + lax.broadcasted_iota(jnp.int32, (W_IN_SHARD, COL_BLK), 0)
            place = _one(_zx_source_col(col) == row).astype(BF16)
            acc[...] += jnp.dot(w_ref[...], place, preferred_element_type=F32)

        @pl.when(s == width - 1)
        def _():
            o_ref[...] = acc[...].astype(BF16)

    return pl.pallas_call(
        body, name="w_in_to_zx",
        grid_spec=pltpu.PrefetchScalarGridSpec(
            num_scalar_prefetch=2, grid=(nblk, width),
            in_specs=[pl.BlockSpec((None, D_MODEL, W_IN_SHARD), lambda j, s, tab, val: (tab[j * width + s], 0, 0))],
            out_specs=pl.BlockSpec((D_MODEL, COL_BLK), lambda j, s, tab, val: (0, j)),
            scratch_shapes=[pltpu.VMEM((D_MODEL, COL_BLK), F32)]),
        out_shape=jax.ShapeDtypeStruct((D_MODEL, ZX_COLS), BF16),
        compiler_params=_cp(("arbitrary", "arbitrary")),
    )(tab, val, w_in_g)


def _zx_to_w_in(d_wzx):
    _, (tab, val, width) = _overlap_tables()

    def body(tab_ref, val_ref, d_ref, o_ref, acc):
        k = pl.program_id(0)
        s = pl.program_id(1)

        @pl.when(s == 0)
        def _():
            acc[...] = jnp.zeros_like(acc)

        @pl.when(val_ref[k * width + s] == 1)
        def _():
            j = tab_ref[k * width + s]
            col = COL_BLK * j + lax.broadcasted_iota(jnp.int32, (COL_BLK, W_IN_SHARD), 0)
            row = W_IN_SHARD * k + lax.broadcasted_iota(jnp.int32, (COL_BLK, W_IN_SHARD), 1)
            place = _one(_zx_source_col(col) == row).astype(BF16)
            acc[...] += jnp.dot(d_ref[...], place, preferred_element_type=F32)

        @pl.when(s == width - 1)
        def _():
            o_ref[...] = acc[...].astype(BF16)

    return pl.pallas_call(
        body, name="zx_to_w_in",
        grid_spec=pltpu.PrefetchScalarGridSpec(
            num_scalar_prefetch=2, grid=(N_DEV, width),
            in_specs=[pl.BlockSpec((D_MODEL, COL_BLK), lambda k, s, tab, val: (0, tab[k * width + s]))],
            out_specs=pl.BlockSpec((None, D_MODEL, W_IN_SHARD), lambda k, s, tab, val: (k, 0, 0)),
            scratch_shapes=[pltpu.VMEM((D_MODEL, W_IN_SHARD), F32)]),
        out_shape=jax.ShapeDtypeStruct((N_DEV, D_MODEL, W_IN_SHARD), BF16),
        compiler_params=_cp(("arbitrary", "arbitrary")),
    )(tab, val, d_wzx)


def _group_conv_cols(a):
    rows = a.shape[0]
    x = a[:, :D_INNER].reshape(rows, N_GROUPS, GROUP_X)
    b = a[:, D_INNER:D_INNER + N_GROUPS * D_STATE].reshape(rows, N_GROUPS, D_STATE)
    c = a[:, D_INNER + N_GROUPS * D_STATE:].reshape(rows, N_GROUPS, D_STATE)
    return jnp.concatenate([x, b, c], axis=2).reshape(rows, N_GROUPS * GROUP_CONV)


def _ungroup_conv_cols(a):
    rows = a.shape[0]
    a3 = a.reshape(rows, N_GROUPS, GROUP_CONV)
    return jnp.concatenate([a3[:, :, :GROUP_X].reshape(rows, D_INNER),
                            a3[:, :, GROUP_X:GROUP_X + D_STATE].reshape(rows, N_GROUPS * D_STATE),
                            a3[:, :, GROUP_X + D_STATE:].reshape(rows, N_GROUPS * D_STATE)], axis=1)


def _small_shard(conv_w, conv_b, norm_g):
    ng = jnp.pad(norm_g.reshape(1, -1), ((0, 0), (0, CONV_SHARD - norm_g.shape[-1])))
    return jnp.concatenate([conv_w.reshape(4, CONV_SHARD), conv_b.reshape(1, CONV_SHARD), ng,
                            jnp.zeros((SMALL_ROWS - 6, CONV_SHARD), F32)], axis=0)


def _small_unshard(a):
    return a[0:4].reshape(1, 4, CONV_SHARD), a[4:5], a[5:6, :D_INNER // N_DEV]


def _heads_of(a):
    return a[:, :, :HEADS_PER_GROUP].reshape(1, N_HEADS)


def _head_params(p):
    return jnp.pad(p.reshape(N_GROUPS, 1, HEADS_PER_GROUP), ((0, 0), (0, 0), (0, 128 - HEADS_PER_GROUP)))


def _update(w, land, m, v, name):
    shp = w.shape
    to2 = lambda a: a.reshape(-1, shp[-1])
    return tuple(o.reshape(shp) for o in _adamw_reduced(to2(w), land, to2(m), to2(v), name))


def kernel(x, norm_mix_g, norm_mlp_g, pool_w, pool_b, pool_scale, ssm_w_in, ssm_conv_w, ssm_conv_b, ssm_dt_bias, ssm_a_log, ssm_d, ssm_norm_g, ssm_w_out, mlp_w1, mlp_w2, final_g, loss_target, m_norm_mix_g, m_norm_mlp_g, m_pool_w, m_pool_b, m_pool_scale, m_ssm_w_in, m_ssm_conv_w, m_ssm_conv_b, m_ssm_dt_bias, m_ssm_a_log, m_ssm_d, m_ssm_norm_g, m_ssm_w_out, m_mlp_w1, m_mlp_w2, m_final_g, v_norm_mix_g, v_norm_mlp_g, v_pool_w, v_pool_b, v_pool_scale, v_ssm_w_in, v_ssm_conv_w, v_ssm_conv_b, v_ssm_dt_bias, v_ssm_a_log, v_ssm_d, v_ssm_norm_g, v_ssm_w_out, v_mlp_w1, v_mlp_w2, v_final_g):
    x2 = x[0]
    tgt = loss_target[0]
    gm0, gm1 = norm_mix_g[0:1], norm_mix_g[1:2]
    gl0, gl1 = norm_mlp_g[0:1], norm_mlp_g[1:2]
    gfin = final_g.reshape(1, D_MODEL)

    fb = D_FF // N_DEV

    def bf(a):
        return a.astype(BF16)

    def gather_of(shards):
        return _direct_exchange(shards, [(i, 0) for i in range(len(shards))],
                                [(s.shape, s.dtype) for s in shards], scatter=False)

    def scatter_of(parts):
        return _direct_exchange(parts, [(i, 0) for i in range(len(parts))],
                                [(p.shape[1:], p.dtype) for p in parts], scatter=True)

    w_pool, small_g = _run_exchange(_two_level_gather(
        [bf(pool_w.reshape(4 * POOL_SHARD, POOL_GROUP)), _small_shard(ssm_conv_w, ssm_conv_b, ssm_norm_g)]),
        "gather_first")
    conv_w = _group_conv_cols(small_g[:, 0:4].transpose(1, 0, 2).reshape(4, CONV_DIM))
    conv_b = _group_conv_cols(small_g[:, 4].reshape(1, CONV_DIM))
    ssm_ng = small_g[:, 5, :D_INNER // N_DEV].reshape(1, D_INNER)
    dtb, alog, dsk = _head_params(ssm_dt_bias), _head_params(ssm_a_log), _head_params(ssm_d)

    (h1,), (w1g0, w2g0) = _pool_fwd(x2, gm0, w_pool, pool_b, pool_scale,
                                    carried=_two_level_gather([bf(mlp_w1[0]), bf(mlp_w2[0])]))
    (h2, u0, hm0), (w_in_g,) = _mlp_fwd(h1, gl0, w1g0, w2g0, "mlp0_fwd",
                                        carried=_two_level_gather([bf(ssm_w_in[0])]))
    w_zx = _w_in_to_zx(w_in_g)
    (zx, hn1), (w_out_g,) = _norm_matmul(h2, gm1, w_zx, carried=gather_of([bf(ssm_w_out[0])]))
    (y_ssd, states), (w1g1, w2g1) = _ssd_fwd(zx, conv_w, conv_b, dtb, alog, dsk,
                                             carried=_two_level_gather([bf(mlp_w1[1]), bf(mlp_w2[1])]))
    w_out = w_out_g.reshape(D_INNER, D_MODEL)
    h3 = _ssm_out_fwd(y_ssd, zx, ssm_ng, w_out, h2)
    (h4, u1, hm1), _ = _mlp_fwd(h3, gl1, w1g1, w2g1, "mlp1_fwd")
    dh4, dh4b, loss_row, d_gfin = _final(h4, gfin, tgt)

    (dh3, dh3b, da1, d_gl1), _ = _mlp_bwd(dh4, dh4b, h3, gl1, u1, w1g1, w2g1, "mlp1_bwd")
    d_w1_1 = _matmul_tn(hm1, da1, "mlp1_dw1", col_blocked=True)
    d_w2_1 = _matmul_tn(u1, dh4b, "mlp1_dw2", square_a=True).reshape(N_DEV, fb, D_MODEL)
    dy_ssd, dzx, yn, d_ng = _ssm_out_bwd(dh3b, y_ssd, zx, ssm_ng, w_out)
    d_wout = _matmul_tn(yn, dh3b, "ssm_dw_out").reshape(N_DEV, D_INNER // N_DEV, D_MODEL)
    (dzx, d_cw, d_cb, d_dtb, d_alog, d_dsk), (l_w1_1, l_w2_1, l_wout) = _ssd_bwd(
        zx, conv_w, conv_b, dtb, alog, dsk, states, dy_ssd, dzx, carried=scatter_of([d_w1_1, d_w2_1, d_wout]))
    d_w_in = _zx_to_w_in(_matmul_tn(hn1, dzx, "ssm_dw_in"))
    (dh2, dh2b, d_gm1), (l_w_in,) = _in_proj_bwd(dzx, w_zx, h2, gm1, dh3, carried=scatter_of([d_w_in]))
    d_w2_0 = _matmul_tn(u0, dh2b, "mlp0_dw2", square_a=True).reshape(N_DEV, fb, D_MODEL)
    (dh1, _, da0, d_gl0), (l_w2_0,) = _mlp_bwd(dh2, dh2b, h1, gl0, u0, w1g0, w2g0, "mlp0_bwd",
                                           carried=scatter_of([d_w2_0]))
    d_w1_0 = _matmul_tn(hm0, da0, "mlp0_dw1", col_blocked=True)
    (dx, d_pool, d_pb, d_ps, d_gm0), (l_w1_0,) = _pool_bwd(x2, dh1, gm0, w_pool, pool_b, pool_scale,
                                                          carried=scatter_of([d_w1_0]))

    d_conv_w = _ungroup_conv_cols(d_cw).reshape(4, N_DEV, CONV_SHARD).transpose(1, 0, 2)
    d_conv_b = _ungroup_conv_cols(d_cb).reshape(N_DEV, 1, CONV_SHARD)
    d_gain = jnp.pad(d_ng.reshape(N_DEV, 1, D_INNER // N_DEV), ((0, 0), (0, 0), (0, CONV_SHARD - D_INNER // N_DEV)))
    d_small = jnp.concatenate([d_conv_w, d_conv_b, d_gain,
                               jnp.zeros((N_DEV, SMALL_ROWS - 6, CONV_SHARD), F32)], axis=1)
    l_pool, l_small = _run_exchange(scatter_of([bf(d_pool), d_small]), "reduce_scatter_tail")

    heads = jnp.concatenate([_heads_of(a) for a in (d_dtb, d_alog, d_dsk)], axis=1)
    sp = jnp.concatenate([d_gm0, d_gm1, d_gl0, d_gl1, d_pb, d_ps, d_gfin,
                          jnp.pad(heads, ((0, 0), (0, D_MODEL - 3 * N_HEADS)))], axis=0)
    sg = _all_reduce_small(sp)

    g_norm_mix = sg[0:2]
    g_norm_mlp = sg[2:4]
    g_pool_b, g_pool_scale = sg[4:5], sg[5:6]
    g_final = sg[6]
    g_dtb, g_alog, g_dsk = sg[7:8, 0:32], sg[7:8, 32:64], sg[7:8, 64:96]

    def rep_pack(nm, nl, pb, ps, fg, db, al, dk):
        hd = jnp.pad(jnp.concatenate([db, al, dk], axis=1), ((0, 0), (0, D_MODEL - 3 * N_HEADS)))
        return jnp.concatenate([nm, nl, pb, ps, fg.reshape(1, D_MODEL), hd], axis=0)

    rep = [rep_pack(*t) for t in (
        (norm_mix_g, norm_mlp_g, pool_b, pool_scale, final_g, ssm_dt_bias, ssm_a_log, ssm_d),
        (g_norm_mix, g_norm_mlp, g_pool_b, g_pool_scale, g_final, g_dtb, g_alog, g_dsk),
        (m_norm_mix_g, m_norm_mlp_g, m_pool_b, m_pool_scale, m_final_g, m_ssm_dt_bias, m_ssm_a_log, m_ssm_d),
        (v_norm_mix_g, v_norm_mlp_g, v_pool_b, v_pool_scale, v_final_g, v_ssm_dt_bias, v_ssm_a_log, v_ssm_d))]
    rep_out = _adamw(*rep, "adamw_replicated")

    def rep_unpack(a):
        return (a[0:2], a[2:4], a[4:5], a[5:6], a[6], a[7:8, 0:32], a[7:8, 32:64], a[7:8, 64:96])

    sm_out = _adamw_reduced(_small_shard(ssm_conv_w, ssm_conv_b, ssm_norm_g), l_small,
                            _small_shard(m_ssm_conv_w, m_ssm_conv_b, m_ssm_norm_g),
                            _small_shard(v_ssm_conv_w, v_ssm_conv_b, v_ssm_norm_g), "adamw_small_shards")

    def update_layers(w, lands, m, v, name):
        per = [_update(w[l], lands[l], m[l], v[l], name + str(l)) for l in range(2)]
        return tuple(jnp.stack([per[0][k], per[1][k]]) for k in range(4))

    big = {
        "pool_w": _update(pool_w, l_pool, m_pool_w, v_pool_w, "adamw_pool_w"),
        "ssm_w_in": _update(ssm_w_in, l_w_in, m_ssm_w_in, v_ssm_w_in, "adamw_w_in"),
        "ssm_w_out": _update(ssm_w_out, l_wout, m_ssm_w_out, v_ssm_w_out, "adamw_w_out"),
        "mlp_w1": update_layers(mlp_w1, (l_w1_0, l_w1_1), m_mlp_w1, v_mlp_w1, "adamw_w1_"),
        "mlp_w2": update_layers(mlp_w2, (l_w2_0, l_w2_1), m_mlp_w2, v_mlp_w2, "adamw_w2_"),
    }
    rep_all = (rep[1],) + tuple(rep_out)

    def ordered(kind):
        nm, nl, pb, ps, fg, db, al, dk = rep_unpack(rep_all[kind])
        cw, cb, ng = _small_unshard(sm_out[kind])
        return [nm, nl, big["pool_w"][kind], pb, ps, big["ssm_w_in"][kind], cw, cb, db, al, dk, ng,
                big["ssm_w_out"][kind], big["mlp_w1"][kind], big["mlp_w2"][kind], fg]

    loss = lax.psum(loss_row[0, 0], ("x", "y", "c"))
    return (loss, dx[None], *ordered(0), *ordered(1), *ordered(2), *ordered(3))
```

```python
import functools

import jax
import jax.numpy as jnp
from jax import lax
from jax.experimental import pallas as pl
from jax.experimental.pallas import tpu as pltpu

F32 = jnp.float32
BF16 = jnp.bfloat16
MESH = pl.DeviceIdType.MESH

D_MODEL = 1024
RMS_EPS = 1e-5
POOL_WINDOWS = (2, 4, 8, 16)
POOL_GROUP = 256
POOL_HALO = 16
POOL_SHARD = POOL_GROUP // 8
D_INNER = 2048
HEAD_DIM = 64
N_HEADS = 32
N_GROUPS = 4
HEADS_PER_GROUP = 8
D_STATE = 128
CHUNK = 128
CONV_DIM = 3072
IN_PROJ_DIM = 5152
D_FF = 4096
N_DEV = 8
GROUP_X = HEADS_PER_GROUP * HEAD_DIM
GROUP_CONV = GROUP_X + 2 * D_STATE
GROUP_COLS = GROUP_CONV + 128
Z_OFF = N_GROUPS * GROUP_COLS
ZX_COLS = Z_OFF + D_INNER
COL_BLK = 512
W_IN_SHARD = IN_PROJ_DIM // N_DEV

ADAM_LR = 0.001
ADAM_B1 = 0.9
ADAM_B2 = 0.999
ADAM_EPS = 1e-08
ADAM_WD = 0.01
ADAM_STEP = 10

VMEM_LIMIT_V7X = 56 * 1024 * 1024
MID_STEP_PERCENT = 70
TN_TOKENS = 512
TN_ACC_BYTES = 16 * 1024 * 1024
MATMUL_TOKENS = 1024

CONV_SHARD = CONV_DIM // N_DEV
SMALL_ROWS = 8

_NN = (((1,), (0,)), ((), ()))
_NT = (((1,), (1,)), ((), ()))
_TN = (((0,), (0,)), ((), ()))


def _cp(sem):
    return pltpu.CompilerParams(dimension_semantics=sem, vmem_limit_bytes=VMEM_LIMIT_V7X)


_ANY = pl.BlockSpec(memory_space=pl.ANY)


def _place():
    return lax.axis_index("x"), lax.axis_index("y"), lax.axis_index("c")


class _Carried:
    def __init__(self, ins, outs, sems, start, finish, mid=None, mid_percent=None):
        self.ins, self.outs, self.sems = list(ins), list(outs), list(sems)
        self.start, self.mid, self.finish, self.mid_percent = start, mid, finish, mid_percent


def _pcall(body, *, name, grid, in_specs, out_specs, out_shape, sem, args, scratch_shapes=(), carried=None,
           aliases=None):
    in_specs, out_specs, out_shape, scratch = list(in_specs), list(out_specs), list(out_shape), list(scratch_shapes)
    common = dict(name=name, grid=grid, input_output_aliases=aliases or {}, compiler_params=_cp(sem))
    if carried is None:
        res = pl.pallas_call(body, in_specs=in_specs, out_specs=out_specs, out_shape=out_shape,
                             scratch_shapes=scratch, **common)(*args)
        return list(res), []
    n_in, n_out, n_scr = len(in_specs), len(out_specs), len(scratch)
    ci, co = len(carried.ins), len(carried.outs)

    def wrapped(*refs):
        ins, cins = refs[:n_in], refs[n_in:n_in + ci]
        p = n_in + ci
        outs, couts = refs[p:p + n_out], refs[p + n_out:p + n_out + co]
        p += n_out + co
        scr, csems = refs[p:p + n_scr], refs[p + n_scr:]
        ids = [pl.program_id(a) for a in range(len(grid))]
        first = functools.reduce(jnp.logical_and, [i == 0 for i in ids])
        last = functools.reduce(jnp.logical_and, [i == g - 1 for i, g in zip(ids, grid)])

        @pl.when(first)
        def _():
            carried.start(cins, couts, csems)

        if carried.mid is not None:
            step, steps = 0, 1
            for i, g in zip(ids, grid):
                step, steps = step * g + i, steps * g

            @pl.when(step == min(steps - 1, (steps * carried.mid_percent) // 100))
            def _():
                carried.mid(cins, couts, csems)

        body(*ins, *outs, *scr)

        @pl.when(last)
        def _():
            carried.finish(cins, couts, csems)

    res = pl.pallas_call(wrapped, in_specs=in_specs + [_ANY] * ci, out_specs=out_specs + [_ANY] * co,
                         out_shape=out_shape + carried.outs, scratch_shapes=scratch + carried.sems,
                         **common)(*args, *carried.ins)
    return list(res[:n_out]), list(res[n_out:])


def _peers(x, y, c):
    out = []
    for rel in range(1, N_DEV):
        dx, dy, dc = (rel >> 2) & 1, (rel >> 1) & 1, rel & 1
        out.append((x + dx - 2 * x * dx, y + dy - 2 * y * dy, c + dc - 2 * c * dc))
    return out


def _direct_exchange(srcs, layout, out_shapes, scatter, src_rows=None):
    n = len(srcs)

    def copies(ins, outs, sems):
        send, recv, loc = sems
        x, y, c = _place()
        me = 4 * x + 2 * y + c
        out, arrive, local = [], [], []
        for i in range(n):
            j, off = layout[i]
            first, rows = (0, srcs[i].shape[-2]) if src_rows is None else src_rows[i]

            def piece(k):
                return ins[i].at[k, pl.ds(first, rows)] if scatter else ins[i]

            for r, peer in enumerate(_peers(x, y, c)):
                pidx = 4 * peer[0] + 2 * peer[1] + peer[2]
                kw = dict(send_sem=send.at[7 * i + r], recv_sem=recv.at[7 * i + r], device_id=peer, device_id_type=MESH)
                out.append(pltpu.make_async_remote_copy(
                    src_ref=piece(pidx), dst_ref=outs[j].at[me, pl.ds(off, rows)], **kw))
                arrive.append(pltpu.make_async_remote_copy(
                    src_ref=piece(pidx), dst_ref=outs[j].at[pidx, pl.ds(off, rows)], **kw))
            local.append(pltpu.make_async_copy(piece(me), outs[j].at[me, pl.ds(off, rows)], loc.at[i]))
        return out, arrive, local

    def start(ins, outs, sems):
        out, _, local = copies(ins, outs, sems)
        for cp in local + out:
            cp.start()

    def finish(ins, outs, sems):
        out, arrive, local = copies(ins, outs, sems)
        for cp in arrive:
            cp.wait_recv()
        for cp in out:
            cp.wait_send()
        for cp in local:
            cp.wait()

    return _Carried(srcs, [jax.ShapeDtypeStruct((N_DEV,) + tuple(s), d) for s, d in out_shapes],
                    [pltpu.SemaphoreType.DMA((7 * n,)), pltpu.SemaphoreType.DMA((7 * n,)),
                     pltpu.SemaphoreType.DMA((n,))], start, finish)


def _two_level_gather(shards, mid_percent=MID_STEP_PERCENT):
    n = len(shards)

    def copies(ins, outs, sems):
        send, recv, loc = sems
        x, y, c = _place()
        me, sibling = (x, y, c), (x, y, 1 - c)
        chips = [(1 - x, y), (x, 1 - y), (1 - x, 1 - y)]

        def win(i, place):
            return outs[i].at[4 * place[0] + 2 * place[1] + place[2]]

        def copy(i, k, block, to, src=None):
            return pltpu.make_async_remote_copy(
                src_ref=win(i, block) if src is None else src, dst_ref=win(i, block),
                send_sem=send.at[7 * i + k], recv_sem=recv.at[7 * i + k], device_id=to, device_id_type=MESH)

        own, passed, ici_in, d2d_in, local = [], [], [], [], []
        for i in range(n):
            own += [copy(i, 0, me, sibling, src=ins[i])]
            own += [copy(i, 1 + j, me, (*chip, c), src=ins[i]) for j, chip in enumerate(chips)]
            passed += [copy(i, 4 + j, (*chip, c), sibling) for j, chip in enumerate(chips)]
            ici_in += [copy(i, 1 + j, (*chip, c), me) for j, chip in enumerate(chips)]
            d2d_in += [copy(i, 0, sibling, me)] + [copy(i, 4 + j, (*chip, 1 - c), me) for j, chip in enumerate(chips)]
            local.append(pltpu.make_async_copy(ins[i], win(i, me), loc.at[i]))
        return own, passed, ici_in, d2d_in, local

    def start(ins, outs, sems):
        own, _, _, _, local = copies(ins, outs, sems)
        for cp in local + own:
            cp.start()

    def mid(ins, outs, sems):
        _, passed, ici_in, _, _ = copies(ins, outs, sems)
        for arrived, onward in zip(ici_in, passed):
            arrived.wait_recv()
            onward.start()

    def finish(ins, outs, sems):
        own, passed, _, d2d_in, local = copies(ins, outs, sems)
        for cp in d2d_in:
            cp.wait_recv()
        for cp in own + passed:
            cp.wait_send()
        for cp in local:
            cp.wait()

    return _Carried(shards, [jax.ShapeDtypeStruct((N_DEV,) + tuple(s.shape), s.dtype) for s in shards],
                    [pltpu.SemaphoreType.DMA((7 * n,)), pltpu.SemaphoreType.DMA((7 * n,)),
                     pltpu.SemaphoreType.DMA((n,))], start, finish, mid, mid_percent)


def _run_exchange(carried, name):
    ci = len(carried.ins)

    def body(*refs):
        ins, outs, sems = refs[:ci], refs[ci:ci + len(carried.outs)], refs[ci + len(carried.outs):]
        carried.start(ins, outs, sems)
        if carried.mid is not None:
            carried.mid(ins, outs, sems)
        carried.finish(ins, outs, sems)

    return list(pl.pallas_call(body, name=name, in_specs=[_ANY] * ci, out_specs=[_ANY] * len(carried.outs),
                               out_shape=carried.outs, scratch_shapes=carried.sems)(*carried.ins))


def _dg(a, b, dn):
    return lax.dot_general(a.astype(BF16), b.astype(BF16), dn, preferred_element_type=F32)


@jax.custom_vjp
def mm_nn(a, b):
    return _dg(a, b, _NN)


@jax.custom_vjp
def mm_nt(a, b):
    return _dg(a, b, _NT)


@jax.custom_vjp
def mm_tn(a, b):
    return _dg(a, b, _TN)


mm_nn.defvjp(lambda a, b: (_dg(a, b, _NN), (a, b)), lambda r, ct: (mm_nt(ct, r[1]), mm_tn(r[0], ct)))
mm_nt.defvjp(lambda a, b: (_dg(a, b, _NT), (a, b)), lambda r, ct: (mm_nn(ct, r[1]), mm_tn(ct, r[0])))
mm_tn.defvjp(lambda a, b: (_dg(a, b, _TN), (a, b)), lambda r, ct: (mm_nt(r[1], ct), mm_nn(r[0], ct)))


def _split3(x):
    p1 = x.astype(BF16)
    r1 = x - p1.astype(F32)
    p2 = r1.astype(BF16)
    r2 = r1 - p2.astype(F32)
    return p1, p2, r2.astype(BF16)


def _exact01(x, c, dn, const_left):
    acc = None
    for p in reversed(_split3(x)):
        t = (lax.dot_general(c, p, dn, preferred_element_type=F32) if const_left
             else lax.dot_general(p, c, dn, preferred_element_type=F32))
        acc = t if acc is None else acc + t
    return acc


def _make_cmm(dn, const_left, bwd_name):
    @jax.custom_vjp
    def f(x, c):
        return _exact01(x, c, dn, const_left)

    def fwd(x, c):
        return _exact01(x, c, dn, const_left), c

    def bwd(c, ct):
        return _CMM[bwd_name](ct, c), jnp.zeros_like(c)

    f.defvjp(fwd, bwd)
    return f


_CMM = {}
_CMM["xc"] = _make_cmm(_NN, False, "xct")
_CMM["xct"] = _make_cmm(_NT, False, "xc")
_CMM["cx"] = _make_cmm(_NN, True, "ctx")
_CMM["ctx"] = _make_cmm(_TN, True, "cx")


@jax.custom_vjp
def _silu(x):
    return x / (1.0 + jnp.exp(-x))


def _silu_fwd(x):
    return _silu(x), x


def _silu_bwd(x, ct):
    s = 1.0 / (1.0 + jnp.exp(-x))
    return (ct * (s * (1.0 + x * (1.0 - s))),)


_silu.defvjp(_silu_fwd, _silu_bwd)


def _log1p_pos(e):
    u = 1.0 + e
    d = u - 1.0
    return jnp.where(d == 0.0, e, jnp.log(u) * (e / jnp.where(d == 0.0, 1.0, d)))


@jax.custom_vjp
def _softplus(x):
    return jnp.maximum(x, 0.0) + _log1p_pos(jnp.exp(-jnp.abs(x)))


def _softplus_fwd(x):
    return _softplus(x), x


def _softplus_bwd(x, ct):
    return (ct / (1.0 + jnp.exp(-x)),)


_softplus.defvjp(_softplus_fwd, _softplus_bwd)


CONV_HALO = 8


def _make_shift(j):
    @jax.custom_vjp
    def f(ext):
        return pltpu.roll(ext, j, 0)[CONV_HALO:, :]

    def fwd(ext):
        return f(ext), None

    def bwd(_, ct):
        pad = jnp.concatenate([jnp.zeros((CONV_HALO, ct.shape[1]), ct.dtype), ct], axis=0)
        return (pltpu.roll(pad, CONV_HALO + CHUNK - j, 0),)

    f.defvjp(fwd, bwd)
    return f


_SHIFT = {j: _make_shift(j) for j in (1, 2, 3)}


@jax.custom_vjp
def _swap_halves(x):
    return pltpu.roll(x, HEAD_DIM, 1)


_swap_halves.defvjp(lambda x: (_swap_halves(x), None), lambda _, ct: (pltpu.roll(ct, HEAD_DIM, 1),))


def _rms_fwd(x, g):
    r = lax.rsqrt(jnp.mean(x * x, axis=-1, keepdims=True) + RMS_EPS)
    n = x * r
    return n * g, n, r


def _rms_bwd(dy, n, r, g):
    dn = dy * g
    dx = r * (dn - n * jnp.mean(dn * n, axis=-1, keepdims=True))
    dg = jnp.sum(dy * n, axis=0, keepdims=True)
    return dx, dg


def _one(cond):
    return jnp.where(cond, 1.0, 0.0)


def _pool_tile(xe, g, ws, b, scale, tile, tt):
    r = lax.rsqrt(jnp.mean(xe * xe, axis=-1, keepdims=True) + RMS_EPS)
    hn = xe * r * g
    row_e = lax.broadcasted_iota(jnp.int32, (tt + POOL_HALO, POOL_GROUP), 0)
    keep = _one(jnp.logical_or(row_e >= POOL_HALO, tile > 0))
    rr = lax.broadcasted_iota(jnp.int32, (tt, tt + POOL_HALO), 0)
    qq = lax.broadcasted_iota(jnp.int32, (tt, tt + POOL_HALO), 1)
    dd = qq - rr
    tpos = tile * tt + lax.broadcasted_iota(jnp.int32, (tt, POOL_GROUP), 0)
    outs = []
    for gi, w in enumerate(POOL_WINDOWS):
        hg = hn[:, gi * POOL_GROUP:(gi + 1) * POOL_GROUP] * keep
        band = _one(jnp.logical_and(dd >= POOL_HALO - w + 1, dd <= POOL_HALO)).astype(BF16)
        cnt = jnp.minimum(tpos + 1, w).astype(F32)
        pooled = _CMM["cx"](hg, band) / cnt
        mixed = pooled - hg[POOL_HALO:, :]
        outs.append(mm_nn(mixed, ws[gi]))
    out = (jnp.concatenate(outs, axis=1) + b) * scale
    return xe[POOL_HALO:, :] + out


def _pool_specs(tt, nt, rev):
    per = tt // POOL_HALO
    t_of = (lambda i: nt - 1 - i) if rev else (lambda i: i)
    main = pl.BlockSpec((tt, D_MODEL), lambda i: (t_of(i), 0))
    halo = pl.BlockSpec((POOL_HALO, D_MODEL), lambda i: (jnp.maximum(t_of(i) * per - 1, 0), 0))
    vec = pl.BlockSpec((1, D_MODEL), lambda i: (0, 0))
    wsp = pl.BlockSpec((N_DEV, 4 * POOL_SHARD, POOL_GROUP), lambda i: (0, 0, 0))
    return main, halo, vec, wsp


def _pool_weights(w_ref):
    return tuple(
        jnp.concatenate([w_ref[k, gi * POOL_SHARD:(gi + 1) * POOL_SHARD, :] for k in range(N_DEV)], axis=0).astype(F32)
        for gi in range(4))


def _pool_fwd(x, g, w, b, scale, carried=None):
    t = x.shape[0]
    tt = min(t, 256)
    nt = t // tt
    main, halo, vec, wsp = _pool_specs(tt, nt, False)

    def body(xm_ref, xh_ref, g_ref, w_ref, b_ref, s_ref, o_ref):
        i = pl.program_id(0)
        xe = jnp.concatenate([xh_ref[...], xm_ref[...]], axis=0)
        o_ref[...] = _pool_tile(xe, g_ref[...], _pool_weights(w_ref), b_ref[...], s_ref[...], i, tt)

    return _pcall(
        body, name="pool_fwd", grid=(nt,),
        in_specs=[main, halo, vec, wsp, vec, vec], out_specs=[main],
        out_shape=[jax.ShapeDtypeStruct((t, D_MODEL), F32)],
        sem=("arbitrary",), args=(x, x, g, w, b, scale), carried=carried)


def _pool_bwd(x, dh, g, w, b, scale, carried=None):
    t = x.shape[0]
    tt = min(t, 256)
    nt = t // tt
    main, halo, vec, wsp = _pool_specs(tt, nt, True)

    def body(xm_ref, xh_ref, dh_ref, g_ref, w_ref, b_ref, s_ref,
             dx_ref, dw_ref, db_ref, ds_ref, dg_ref, carry, dw_acc):
        i = pl.program_id(0)
        tile = nt - 1 - i

        @pl.when(i == 0)
        def _():
            carry[...] = jnp.zeros_like(carry)
            dw_acc[...] = jnp.zeros_like(dw_acc)
            db_ref[...] = jnp.zeros_like(db_ref)
            ds_ref[...] = jnp.zeros_like(ds_ref)
            dg_ref[...] = jnp.zeros_like(dg_ref)

        xe = jnp.concatenate([xh_ref[...], xm_ref[...]], axis=0)
        _, vjp = jax.vjp(lambda a, gg, ww, bb, ss: _pool_tile(a, gg, ww, bb, ss, tile, tt),
                         xe, g_ref[...], _pool_weights(w_ref), b_ref[...], s_ref[...])
        dxe, dgv, dws, dbv, dsv = vjp(dh_ref[...])
        dx_ref[...] = dxe[POOL_HALO:, :]
        dx_ref[tt - POOL_HALO:tt, :] += carry[...]
        carry[...] = dxe[:POOL_HALO, :]
        for gi in range(4):
            dw_acc[gi] += dws[gi]
        db_ref[...] += dbv
        ds_ref[...] += dsv
        dg_ref[...] += dgv

        @pl.when(i == nt - 1)
        def _():
            for k in range(N_DEV):
                for gi in range(4):
                    dw_ref[k, gi * POOL_SHARD:(gi + 1) * POOL_SHARD, :] = dw_acc[gi, k * POOL_SHARD:(k + 1) * POOL_SHARD, :]

    return _pcall(
        body, name="pool_bwd", grid=(nt,),
        in_specs=[main, halo, main, vec, wsp, vec, vec],
        out_specs=[main, wsp, vec, vec, vec],
        out_shape=[jax.ShapeDtypeStruct((t, D_MODEL), F32),
                   jax.ShapeDtypeStruct((N_DEV, 4 * POOL_SHARD, POOL_GROUP), F32),
                   jax.ShapeDtypeStruct((1, D_MODEL), F32),
                   jax.ShapeDtypeStruct((1, D_MODEL), F32),
                   jax.ShapeDtypeStruct((1, D_MODEL), F32)],
        scratch_shapes=[pltpu.VMEM((POOL_HALO, D_MODEL), F32), pltpu.VMEM((4, POOL_GROUP, POOL_GROUP), F32)],
        sem=("arbitrary",), args=(x, x, dh, g, w, b, scale), carried=carried)


def _mlp_weight_specs():
    fb = D_FF // N_DEV
    return (pl.BlockSpec((None, D_MODEL, fb), lambda i, k: (k, 0, 0)),
            pl.BlockSpec((None, fb, D_MODEL), lambda i, k: (k, 0, 0)))


def _mlp_fwd(h, g, w1g, w2g, name, carried=None):
    t = h.shape[0]
    tt = min(t, MATMUL_TOKENS)
    nk, fb = N_DEV, D_FF // N_DEV
    w1_spec, w2_spec = _mlp_weight_specs()

    def body(h_ref, g_ref, w1_ref, w2_ref, o_ref, u_ref, hm_ref, hm_s, acc_s):
        k = pl.program_id(1)

        @pl.when(k == 0)
        def _():
            xv = h_ref[...]
            y, _, _ = _rms_fwd(xv, g_ref[...])
            hb = y.astype(BF16)
            hm_s[...] = hb
            hm_ref[...] = hb
            acc_s[...] = xv

        a = jnp.dot(hm_s[...], w1_ref[...], preferred_element_type=F32)
        u = jnp.maximum(a, 0.0)
        u_ref[...] = u.astype(BF16)
        acc_s[...] += jnp.dot((u * u).astype(BF16), w2_ref[...], preferred_element_type=F32)

        @pl.when(k == nk - 1)
        def _():
            o_ref[...] = acc_s[...]

    return _pcall(
        body, name=name, grid=(t // tt, nk),
        in_specs=[pl.BlockSpec((tt, D_MODEL), lambda i, k: (i, 0)),
                  pl.BlockSpec((1, D_MODEL), lambda i, k: (0, 0)),
                  w1_spec, w2_spec],
        out_specs=[pl.BlockSpec((tt, D_MODEL), lambda i, k: (i, 0)),
                   pl.BlockSpec((tt, fb), lambda i, k: (i, k)),
                   pl.BlockSpec((tt, D_MODEL), lambda i, k: (i, 0))],
        out_shape=[jax.ShapeDtypeStruct((t, D_MODEL), F32),
                   jax.ShapeDtypeStruct((t, nk * fb), BF16),
                   jax.ShapeDtypeStruct((t, D_MODEL), BF16)],
        scratch_shapes=[pltpu.VMEM((tt, D_MODEL), BF16), pltpu.VMEM((tt, D_MODEL), F32)],
        sem=("arbitrary", "arbitrary"), args=(h, g, w1g, w2g), carried=carried)


def _mlp_bwd(dh, dhb, h, g, u, w1g, w2g, name, carried=None):
    t = h.shape[0]
    tt = min(t, MATMUL_TOKENS)
    nk, fb = N_DEV, D_FF // N_DEV
    w1_spec, w2_spec = _mlp_weight_specs()

    def body(dh_ref, dhb_ref, h_ref, g_ref, u_ref, w1_ref, w2_ref,
             dhin_ref, dhinb_ref, da_ref, dg_ref, acc_s):
        i = pl.program_id(0)
        k = pl.program_id(1)

        @pl.when(jnp.logical_and(i == 0, k == 0))
        def _():
            dg_ref[...] = jnp.zeros_like(dg_ref)

        @pl.when(k == 0)
        def _():
            acc_s[...] = jnp.zeros_like(acc_s)

        dv = lax.dot_general(dhb_ref[...], w2_ref[...], _NT, preferred_element_type=F32)
        dab = (dv * (2.0 * u_ref[...].astype(F32))).astype(BF16)
        da_ref[...] = dab
        acc_s[...] += lax.dot_general(dab, w1_ref[...], _NT, preferred_element_type=F32)

        @pl.when(k == nk - 1)
        def _():
            gv = g_ref[...]
            _, n, r = _rms_fwd(h_ref[...], gv)
            dx, dg = _rms_bwd(acc_s[...], n, r, gv)
            dhin = dh_ref[...] + dx
            dhin_ref[...] = dhin
            dhinb_ref[...] = dhin.astype(BF16)
            dg_ref[...] += dg

    tile = pl.BlockSpec((tt, D_MODEL), lambda i, k: (i, 0))
    return _pcall(
        body, name=name, grid=(t // tt, nk),
        in_specs=[tile, tile, tile, pl.BlockSpec((1, D_MODEL), lambda i, k: (0, 0)),
                  pl.BlockSpec((tt, fb), lambda i, k: (i, k)), w1_spec, w2_spec],
        out_specs=[tile, tile, pl.BlockSpec((tt, fb), lambda i, k: (i, k)),
                   pl.BlockSpec((1, D_MODEL), lambda i, k: (0, 0))],
        out_shape=[jax.ShapeDtypeStruct((t, D_MODEL), F32),
                   jax.ShapeDtypeStruct((t, D_MODEL), BF16),
                   jax.ShapeDtypeStruct((t, nk * fb), BF16),
                   jax.ShapeDtypeStruct((1, D_MODEL), F32)],
        scratch_shapes=[pltpu.VMEM((tt, D_MODEL), F32)],
        sem=("arbitrary", "arbitrary"), args=(dh, dhb, h, g, u, w1g, w2g), carried=carried)


def _matmul_tn(a, b, name, square_a=False, col_blocked=False, carried=None):
    t, k1 = a.shape
    k2 = b.shape[1]
    tt = min(t, TN_TOKENS)
    nt = t // tt
    wc = k2 if k1 * k2 * 4 <= TN_ACC_BYTES else k2 // 2
    nb = wc // COL_BLK

    def body(a_ref, b_ref, o_ref, acc):
        s = pl.program_id(1)

        @pl.when(s == 0)
        def _():
            acc[...] = jnp.zeros_like(acc)

        av = a_ref[...]
        if square_a:
            af = av.astype(F32)
            av = (af * af).astype(BF16)
        acc[...] += lax.dot_general(av, b_ref[...], _TN, preferred_element_type=F32)

        @pl.when(s == nt - 1)
        def _():
            if col_blocked:
                for k in range(nb):
                    o_ref[k] = acc[:, k * COL_BLK:(k + 1) * COL_BLK].astype(o_ref.dtype)
            else:
                o_ref[...] = acc[...].astype(o_ref.dtype)

    if col_blocked:
        out_shape = jax.ShapeDtypeStruct((k2 // COL_BLK, k1, COL_BLK), BF16)
        out_spec = pl.BlockSpec((nb, k1, COL_BLK), lambda j, s: (j, 0, 0))
    else:
        out_shape = jax.ShapeDtypeStruct((k1, k2), BF16)
        out_spec = pl.BlockSpec((k1, wc), lambda j, s: (0, j))
    outs, landed = _pcall(
        body, name=name, grid=(k2 // wc, nt),
        in_specs=[pl.BlockSpec((tt, k1), lambda j, s: (s, 0)),
                  pl.BlockSpec((tt, wc), lambda j, s: (s, j))],
        out_specs=[out_spec], out_shape=[out_shape],
        scratch_shapes=[pltpu.VMEM((k1, wc), F32)],
        sem=("arbitrary", "arbitrary"), args=(a, b), carried=carried)
    return (outs[0], landed) if carried is not None else outs[0]


def _norm_matmul(h, g, w, carried=None):
    t = h.shape[0]
    tt = min(t, MATMUL_TOKENS)
    n = w.shape[1]

    def body(h_ref, g_ref, w_ref, o_ref, hn_ref, hn_s):
        @pl.when(pl.program_id(1) == 0)
        def _():
            y, _, _ = _rms_fwd(h_ref[...], g_ref[...])
            hb = y.astype(BF16)
            hn_s[...] = hb
            hn_ref[...] = hb

        o_ref[...] = jnp.dot(hn_s[...], w_ref[...], preferred_element_type=F32)

    return _pcall(
        body, name="ssm_in_proj", grid=(t // tt, n // COL_BLK),
        in_specs=[pl.BlockSpec((tt, D_MODEL), lambda i, j: (i, 0)),
                  pl.BlockSpec((1, D_MODEL), lambda i, j: (0, 0)),
                  pl.BlockSpec((D_MODEL, COL_BLK), lambda i, j: (0, j))],
        out_specs=[pl.BlockSpec((tt, COL_BLK), lambda i, j: (i, j)),
                   pl.BlockSpec((tt, D_MODEL), lambda i, j: (i, 0))],
        out_shape=[jax.ShapeDtypeStruct((t, n), F32), jax.ShapeDtypeStruct((t, D_MODEL), BF16)],
        scratch_shapes=[pltpu.VMEM((tt, D_MODEL), BF16)],
        sem=("arbitrary", "arbitrary"), args=(h, g, w), carried=carried)


def _in_proj_bwd(dzx, w, h, g, dh_next, carried=None):
    t = h.shape[0]
    tt = min(t, MATMUL_TOKENS)
    n = w.shape[1]
    nj = n // COL_BLK

    def body(dz_ref, w_ref, h_ref, g_ref, dn_ref, dh_ref, dhb_ref, dg_ref, acc):
        i = pl.program_id(0)
        j = pl.program_id(1)

        @pl.when(jnp.logical_and(i == 0, j == 0))
        def _():
            dg_ref[...] = jnp.zeros_like(dg_ref)

        @pl.when(j == 0)
        def _():
            acc[...] = jnp.zeros_like(acc)

        acc[...] += lax.dot_general(dz_ref[...], w_ref[...], _NT, preferred_element_type=F32)

        @pl.when(j == nj - 1)
        def _():
            gv = g_ref[...]
            _, nn, r = _rms_fwd(h_ref[...], gv)
            dx, dg = _rms_bwd(acc[...], nn, r, gv)
            dh = dn_ref[...] + dx
            dh_ref[...] = dh
            dhb_ref[...] = dh.astype(BF16)
            dg_ref[...] += dg

    tile = pl.BlockSpec((tt, D_MODEL), lambda i, j: (i, 0))
    return _pcall(
        body, name="ssm_in_proj_bwd", grid=(t // tt, nj),
        in_specs=[pl.BlockSpec((tt, COL_BLK), lambda i, j: (i, j)),
                  pl.BlockSpec((D_MODEL, COL_BLK), lambda i, j: (0, j)),
                  tile, pl.BlockSpec((1, D_MODEL), lambda i, j: (0, 0)), tile],
        out_specs=[tile, tile, pl.BlockSpec((1, D_MODEL), lambda i, j: (0, 0))],
        out_shape=[jax.ShapeDtypeStruct((t, D_MODEL), F32), jax.ShapeDtypeStruct((t, D_MODEL), BF16),
                   jax.ShapeDtypeStruct((1, D_MODEL), F32)],
        scratch_shapes=[pltpu.VMEM((tt, D_MODEL), F32)],
        sem=("arbitrary", "arbitrary"), args=(dzx, w, h, g, dh_next), carried=carried)


def _ssd_consts():
    lane = lax.broadcasted_iota(jnp.int32, (CHUNK, CHUNK), 1)
    row = lax.broadcasted_iota(jnp.int32, (CHUNK, CHUNK), 0)
    causal = lane <= row
    tri = _one(causal).astype(BF16)
    er = lax.broadcasted_iota(jnp.int32, (CHUNK, GROUP_X), 0)
    ec = lax.broadcasted_iota(jnp.int32, (CHUNK, GROUP_X), 1)
    expand = _one(jnp.right_shift(ec, 6) == er).astype(BF16)
    return dict(causal=causal, tri=tri, expand=expand, lo=lane < HEAD_DIM)


def _conv_silu(cur, prev, w, b):
    ext = jnp.concatenate([prev, cur], axis=0)
    acc = cur * w[3] + b
    for j in (1, 2, 3):
        acc = acc + _SHIFT[j](ext) * w[3 - j]
    return _silu(acc)


def _ssd_chunk(raw, rawp, ht, cw, cb_, dtb, alog, dsk, k):
    act = _conv_silu(raw[:, :GROUP_CONV], rawp[:, :GROUP_CONV], cw, cb_)
    xs = act[:, :GROUP_X]
    bm = act[:, GROUP_X:GROUP_X + D_STATE]
    cm = act[:, GROUP_X + D_STATE:]
    dt = _softplus(raw[:, GROUP_CONV:] + dtb)
    a = -jnp.exp(alog)
    xc = _CMM["xc"]

    def lanes(rowv):
        return jnp.sum(xc(jnp.broadcast_to(rowv, (16, CHUNK)), k["expand"]), axis=0, keepdims=True) * (1.0 / 16.0)

    dt_e = xc(dt, k["expand"])
    adt_e = dt_e * lanes(a)
    acs_e = _CMM["cx"](adt_e, k["tri"])
    tot_e = jnp.sum(adt_e, axis=0, keepdims=True)
    gmat = mm_nt(cm, bm)
    xdt = xs * dt_e
    ys = []
    for j in range(HEADS_PER_GROUP // 2):
        pair = acs_e[:, j * CHUNK:(j + 1) * CHUNK]
        swapped = _swap_halves(pair)
        ms = []
        for cb in (jnp.where(k["lo"], pair, swapped), jnp.where(k["lo"], swapped, pair)):
            seg = cb - cb.T
            ms.append(gmat * jnp.exp(jnp.where(k["causal"], seg, -jnp.inf)))
        xp = xdt[:, j * CHUNK:(j + 1) * CHUNK]
        rhs = jnp.concatenate([jnp.where(k["lo"], xp, 0.0), jnp.where(k["lo"], 0.0, xp)], axis=0)
        ys.append(mm_nn(jnp.concatenate(ms, axis=1), rhs))
    y_diag = jnp.concatenate(ys, axis=1)
    y_off = jnp.exp(acs_e) * mm_nn(cm, ht)
    h_new = jnp.exp(tot_e) * ht + mm_tn(bm, xdt * jnp.exp(tot_e - acs_e))
    return y_diag + y_off + lanes(dsk) * xs, h_new


def _ssd_in_specs(nc, rev):
    c_of = (lambda c: nc - 1 - c) if rev else (lambda c: c)
    per = CHUNK // CONV_HALO
    zx = [pl.BlockSpec((CHUNK, GROUP_COLS), lambda g, c: (c_of(c), g)),
          pl.BlockSpec((CONV_HALO, GROUP_COLS), lambda g, c: (jnp.maximum(c_of(c) * per - 1, 0), g))]
    conv = [pl.BlockSpec((4, GROUP_CONV), lambda g, c: (0, g)), pl.BlockSpec((1, GROUP_CONV), lambda g, c: (0, g))]
    head = [pl.BlockSpec((None, 1, 128), lambda g, c: (g, 0, 0))] * 3
    return zx + conv + head, c_of


def _load_chunk_args(refs, has_prev):
    raw, rawp, cw, cb_, dtb, alog, dsk = refs
    return (raw[...], rawp[...] * has_prev, tuple(cw[pl.ds(i, 1), :] for i in range(4)), cb_[...],
            dtb[...], alog[...], dsk[...])


def _ssd_fwd(zx, conv_w, conv_b, dtb, alog, dsk, carried=None):
    t = zx.shape[0]
    nc = t // CHUNK
    in_specs, _ = _ssd_in_specs(nc, False)

    def body(*refs):
        ins, (y_ref, hs_ref, ht) = refs[:7], refs[7:]
        c = pl.program_id(1)

        @pl.when(c == 0)
        def _():
            ht[...] = jnp.zeros_like(ht)

        a = _load_chunk_args(ins, _one(c > 0))
        h_in = ht[...]
        y, h_new = _ssd_chunk(*a[:2], h_in, *a[2:], _ssd_consts())
        y_ref[...] = y
        hs_ref[...] = h_in
        ht[...] = h_new

    return _pcall(
        body, name="ssd_fwd", grid=(N_GROUPS, nc),
        in_specs=in_specs,
        out_specs=[pl.BlockSpec((CHUNK, GROUP_X), lambda g, c: (c, g)),
                   pl.BlockSpec((None, None, D_STATE, GROUP_X), lambda g, c: (g, c, 0, 0))],
        out_shape=[jax.ShapeDtypeStruct((t, D_INNER), F32),
                   jax.ShapeDtypeStruct((N_GROUPS, nc, D_STATE, GROUP_X), F32)],
        scratch_shapes=[pltpu.VMEM((D_STATE, GROUP_X), F32)],
        sem=("arbitrary", "arbitrary"), args=(zx, zx, conv_w, conv_b, dtb, alog, dsk), carried=carried)


def _ssd_bwd(zx, conv_w, conv_b, dtb, alog, dsk, hs, dy, dzx, carried=None):
    t = zx.shape[0]
    nc = t // CHUNK
    in_specs, c_of = _ssd_in_specs(nc, True)
    n_in = 10

    def body(*refs):
        ins, hs_ref, dy_ref = refs[:7], refs[7], refs[8]
        (draw_ref, dcw, dcb, ddtb, dalog, ddsk, dht, carry) = refs[n_in:]
        cc = pl.program_id(1)
        accs = (dcw, dcb, ddtb, dalog, ddsk)

        @pl.when(cc == 0)
        def _():
            for r in (dht, carry) + accs:
                r[...] = jnp.zeros_like(r)

        has_prev = _one(c_of(cc) > 0)
        a = _load_chunk_args(ins, has_prev)
        k = _ssd_consts()
        fn = lambda *args: _ssd_chunk(*args, k)
        _, vjp = jax.vjp(fn, *a[:2], hs_ref[...], *a[2:])
        graw, grawp, ght, gcw, gcb, gdtb, galog, gdsk = vjp((dy_ref[...], dht[...]))
        tail = jnp.concatenate([jnp.zeros((CHUNK - CONV_HALO, GROUP_COLS), F32), carry[...]], axis=0)
        draw_ref[...] = (graw + tail).astype(BF16)
        carry[...] = grawp * has_prev
        dht[...] = ght
        for i in range(4):
            dcw[pl.ds(i, 1), :] += gcw[i]
        for ref, val in ((dcb, gcb), (ddtb, gdtb), (dalog, galog), (ddsk, gdsk)):
            ref[...] += val

    head_out = pl.BlockSpec((None, 1, 128), lambda g, c: (g, 0, 0))
    sds = jax.ShapeDtypeStruct
    return _pcall(
        body, name="ssd_bwd", grid=(N_GROUPS, nc),
        in_specs=in_specs + [
            pl.BlockSpec((None, None, D_STATE, GROUP_X), lambda g, c: (g, c_of(c), 0, 0)),
            pl.BlockSpec((CHUNK, GROUP_X), lambda g, c: (c_of(c), g)),
            _ANY],
        out_specs=[pl.BlockSpec((CHUNK, GROUP_COLS), lambda g, c: (c_of(c), g)),
                   pl.BlockSpec((4, GROUP_CONV), lambda g, c: (0, g)),
                   pl.BlockSpec((1, GROUP_CONV), lambda g, c: (0, g)),
                   head_out, head_out, head_out],
        out_shape=[sds((t, ZX_COLS), BF16), sds((4, N_GROUPS * GROUP_CONV), F32), sds((1, N_GROUPS * GROUP_CONV), F32),
                   sds((N_GROUPS, 1, 128), F32), sds((N_GROUPS, 1, 128), F32), sds((N_GROUPS, 1, 128), F32)],
        scratch_shapes=[pltpu.VMEM((D_STATE, GROUP_X), F32), pltpu.VMEM((CONV_HALO, GROUP_COLS), F32)],
        sem=("arbitrary", "arbitrary"), args=(zx, zx, conv_w, conv_b, dtb, alog, dsk, hs, dy, dzx),
        aliases={9: 0}, carried=carried)


def _gate_norm(y, zs, ng):
    outs = []
    for k in range(N_GROUPS):
        s = y[:, k * GROUP_X:(k + 1) * GROUP_X] * _silu(zs[k])
        outs.append(s * lax.rsqrt(jnp.mean(s * s, axis=-1, keepdims=True) + RMS_EPS))
    return jnp.concatenate(outs, axis=1) * ng


def _z_specs(tt):
    first = Z_OFF // GROUP_X
    return [pl.BlockSpec((tt, GROUP_X), functools.partial(lambda k, i: (i, first + k), k)) for k in range(N_GROUPS)]


def _ssm_out_fwd(y, zx, ng, w_out, h):
    t = h.shape[0]
    tt = min(t, 256)

    def body(y_ref, z0, z1, z2, z3, ng_ref, w_ref, h_ref, o_ref):
        yn = _gate_norm(y_ref[...], (z0[...], z1[...], z2[...], z3[...]), ng_ref[...])
        o_ref[...] = h_ref[...] + jnp.dot(yn.astype(BF16), w_ref[...], preferred_element_type=F32)

    return pl.pallas_call(
        body, name="ssm_out_fwd", grid=(t // tt,),
        in_specs=[pl.BlockSpec((tt, D_INNER), lambda i: (i, 0))] + _z_specs(tt) + [
            pl.BlockSpec((1, D_INNER), lambda i: (0, 0)),
            pl.BlockSpec((D_INNER, D_MODEL), lambda i: (0, 0)),
            pl.BlockSpec((tt, D_MODEL), lambda i: (i, 0))],
        out_specs=pl.BlockSpec((tt, D_MODEL), lambda i: (i, 0)),
        out_shape=jax.ShapeDtypeStruct((t, D_MODEL), F32),
        compiler_params=_cp(("arbitrary",)),
    )(y, zx, zx, zx, zx, ng, w_out, h)


def _gate_norm_group(y, z, ng):
    s = y * _silu(z)
    return s * lax.rsqrt(jnp.mean(s * s, axis=-1, keepdims=True) + RMS_EPS) * ng


def _ssm_out_bwd(dhb, y, zx, ng, w_out):
    t = dhb.shape[0]
    tt = min(t, 512)
    first = Z_OFF // GROUP_X

    def body(dh_ref, y_ref, z_ref, ng_ref, w_ref, dy_ref, dzx_ref, yn_ref, dng_ref):
        @pl.when(pl.program_id(1) == 0)
        def _():
            dng_ref[...] = jnp.zeros_like(dng_ref)

        dyn = lax.dot_general(dh_ref[...], w_ref[...], _NT, preferred_element_type=F32)
        yn, vjp = jax.vjp(_gate_norm_group, y_ref[...], z_ref[...], ng_ref[...])
        dy, dz, dng = vjp(dyn)
        dy_ref[...] = dy
        dzx_ref[...] = dz.astype(BF16)
        yn_ref[...] = yn.astype(BF16)
        dng_ref[...] += dng

    grp = pl.BlockSpec((tt, GROUP_X), lambda k, i: (i, k))
    zgrp = pl.BlockSpec((tt, GROUP_X), lambda k, i: (i, first + k))
    gain = pl.BlockSpec((1, GROUP_X), lambda k, i: (0, k))
    return pl.pallas_call(
        body, name="ssm_out_bwd", grid=(N_GROUPS, t // tt),
        in_specs=[pl.BlockSpec((tt, D_MODEL), lambda k, i: (i, 0)), grp, zgrp, gain,
                  pl.BlockSpec((GROUP_X, D_MODEL), lambda k, i: (k, 0))],
        out_specs=[grp, zgrp, grp, gain],
        out_shape=[jax.ShapeDtypeStruct((t, D_INNER), F32), jax.ShapeDtypeStruct((t, ZX_COLS), BF16),
                   jax.ShapeDtypeStruct((t, D_INNER), BF16), jax.ShapeDtypeStruct((1, D_INNER), F32)],
        compiler_params=_cp(("arbitrary", "arbitrary")),
    )(dhb, y, zx, ng, w_out)


def _final(h, g, tgt):
    t = h.shape[0]
    tt = min(t, 512)
    nt = t // tt

    def body(h_ref, g_ref, t_ref, dh_ref, dhb_ref, loss_ref, dg_ref, lacc):
        i = pl.program_id(0)

        @pl.when(i == 0)
        def _():
            dg_ref[...] = jnp.zeros_like(dg_ref)
            lacc[...] = jnp.zeros_like(lacc)

        gv = g_ref[...]
        y, n, r = _rms_fwd(h_ref[...], gv)
        err = y - t_ref[...]
        lacc[...] += jnp.sum(err * err, axis=0, keepdims=True)
        dx, dg = _rms_bwd(err * (1.0 / D_MODEL), n, r, gv)
        dh_ref[...] = dx
        dhb_ref[...] = dx.astype(BF16)
        dg_ref[...] += dg

        @pl.when(i == nt - 1)
        def _():
            loss_ref[...] = jnp.zeros_like(loss_ref) + (0.5 / D_MODEL) * jnp.sum(lacc[...])

    tile = pl.BlockSpec((tt, D_MODEL), lambda i: (i, 0))
    vec = pl.BlockSpec((1, D_MODEL), lambda i: (0, 0))
    return pl.pallas_call(
        body, name="final_loss", grid=(nt,),
        in_specs=[tile, vec, tile],
        out_specs=[tile, tile, pl.BlockSpec((1, 128), lambda i: (0, 0)), vec],
        out_shape=[jax.ShapeDtypeStruct((t, D_MODEL), F32), jax.ShapeDtypeStruct((t, D_MODEL), BF16),
                   jax.ShapeDtypeStruct((1, 128), F32), jax.ShapeDtypeStruct((1, D_MODEL), F32)],
        scratch_shapes=[pltpu.VMEM((1, D_MODEL), F32)],
        compiler_params=_cp(("arbitrary",)),
    )(h, g, tgt)


def _adamw_reduced_layers(w, lands, m, v, name):
    _, rows, cols = w.shape
    br = rows if rows <= 256 else 256
    nb = rows // br
    nl = lands[0].shape[0]

    def body(w_ref, l0_ref, l1_ref, m_ref, v_ref, g_ref, d_ref, m2_ref, v2_ref):
        def total(ref):
            acc = ref[0].astype(F32)
            for q in range(1, nl):
                acc = acc + ref[q].astype(F32)
            return acc

        gv = jnp.where(pl.program_id(0) == 0, total(l0_ref), total(l1_ref))
        g_ref[...] = gv
        d_ref[...], m2_ref[...], v2_ref[...] = _adamw_math(w_ref[...], gv, m_ref[...], v_ref[...])

    spec = pl.BlockSpec((None, br, cols), lambda l, i: (l, i, 0))
    land0 = pl.BlockSpec((nl, br, cols), lambda l, i: (0, jnp.where(l == 0, i, nb - 1), 0))
    land1 = pl.BlockSpec((nl, br, cols), lambda l, i: (0, jnp.where(l == 1, i, 0), 0))
    out = jax.ShapeDtypeStruct(w.shape, F32)
    return pl.pallas_call(
        body, name=name, grid=(2, nb),
        in_specs=[spec, land0, land1, spec, spec], out_specs=[spec] * 4, out_shape=[out] * 4,
        compiler_params=_cp(("arbitrary", "arbitrary")),
    )(w, lands[0], lands[1], m, v)


def _all_reduce_small(sp):
    rows, n = sp.shape

    def body(x_ref, o_ref, land, send_sems, recv_sems):
        x, y, c = _place()
        me = 4 * x + 2 * y + c
        land[me] = x_ref[...]
        cps = []
        for rel in range(1, N_DEV):
            dx, dy, dc = (rel >> 2) & 1, (rel >> 1) & 1, rel & 1
            px = x + dx - 2 * x * dx
            py = y + dy - 2 * y * dy
            pc = c + dc - 2 * c * dc
            peer = 4 * px + 2 * py + pc
            cps.append((pltpu.make_async_remote_copy(
                src_ref=x_ref, dst_ref=land.at[me], send_sem=send_sems.at[rel - 1], recv_sem=recv_sems.at[rel - 1],
                device_id=(px, py, pc), device_id_type=MESH),
                pltpu.make_async_remote_copy(
                src_ref=x_ref, dst_ref=land.at[peer], send_sem=send_sems.at[rel - 1], recv_sem=recv_sems.at[rel - 1],
                device_id=(px, py, pc), device_id_type=MESH)))
        for cp, _ in cps:
            cp.start()
        for _, arr in cps:
            arr.wait_recv()
        for cp, _ in cps:
            cp.wait_send()
        acc = land[0]
        for k in range(1, N_DEV):
            acc = acc + land[k]
        o_ref[...] = acc

    vm = pl.BlockSpec(memory_space=pltpu.VMEM)
    return pl.pallas_call(
        body, name="all_reduce_small",
        out_shape=jax.ShapeDtypeStruct((rows, n), F32),
        in_specs=[vm], out_specs=vm,
        scratch_shapes=[pltpu.VMEM((N_DEV, rows, n), F32),
                        pltpu.SemaphoreType.DMA((N_DEV - 1,)), pltpu.SemaphoreType.DMA((N_DEV - 1,))],
    )(sp)


def _adamw_math(wv, gv, mv, vv):
    m2 = ADAM_B1 * mv + (1.0 - ADAM_B1) * gv
    v2 = ADAM_B2 * vv + (1.0 - ADAM_B2) * (gv * gv)
    m_hat = m2 / (1.0 - ADAM_B1 ** ADAM_STEP)
    v_hat = v2 / (1.0 - ADAM_B2 ** ADAM_STEP)
    return -ADAM_LR * (m_hat / (jnp.sqrt(v_hat) + ADAM_EPS) + ADAM_WD * wv), m2, v2


def _adamw(w, g, m, v, name):
    rows, cols = w.shape
    br = rows if rows <= 256 else 256

    def body(w_ref, g_ref, m_ref, v_ref, d_ref, m2_ref, v2_ref):
        d_ref[...], m2_ref[...], v2_ref[...] = _adamw_math(w_ref[...], g_ref[...], m_ref[...], v_ref[...])

    spec = pl.BlockSpec((br, cols), lambda i: (i, 0))
    out = jax.ShapeDtypeStruct((rows, cols), F32)
    return pl.pallas_call(
        body, name=name, grid=(rows // br,),
        in_specs=[spec] * 4, out_specs=[spec] * 3, out_shape=[out] * 3,
        compiler_params=_cp(("arbitrary",)),
    )(w, g, m, v)


def _adamw_reduced(w, land, m, v, name):
    rows, cols = w.shape
    br = rows if rows <= 256 else 256
    nl = land.shape[0]

    def body(w_ref, l_ref, m_ref, v_ref, g_ref, d_ref, m2_ref, v2_ref):
        gv = l_ref[0].astype(F32)
        for q in range(1, nl):
            gv = gv + l_ref[q].astype(F32)
        g_ref[...] = gv
        d_ref[...], m2_ref[...], v2_ref[...] = _adamw_math(w_ref[...], gv, m_ref[...], v_ref[...])

    spec = pl.BlockSpec((br, cols), lambda i: (i, 0))
    out = jax.ShapeDtypeStruct((rows, cols), F32)
    return pl.pallas_call(
        body, name=name, grid=(rows // br,),
        in_specs=[spec, pl.BlockSpec((nl, br, cols), lambda i: (0, i, 0)), spec, spec],
        out_specs=[spec] * 4, out_shape=[out] * 4,
        compiler_params=_cp(("arbitrary",)),
    )(w, land, m, v)


def _zx_source_col(col):
    blk = jnp.right_shift(col, 7)
    lane = jnp.bitwise_and(col, 127)
    per = GROUP_COLS // 128
    grp = jnp.where(blk >= per, 1, 0) + jnp.where(blk >= 2 * per, 1, 0) + jnp.where(blk >= 3 * per, 1, 0)
    o = blk - per * grp
    x_col = D_INNER + GROUP_X * grp + 128 * o + lane
    b_col = 2 * D_INNER + D_STATE * grp + lane
    c_col = 2 * D_INNER + N_GROUPS * D_STATE + D_STATE * grp + lane
    dt_col = jnp.where(lane < HEADS_PER_GROUP, D_INNER + CONV_DIM + HEADS_PER_GROUP * grp + lane, -1)
    src = jnp.where(o < 4, x_col, jnp.where(o == 4, b_col, jnp.where(o == 5, c_col, dt_col)))
    return jnp.where(col >= Z_OFF, col - Z_OFF, src)


def _zx_source_col_py(col):
    if col >= Z_OFF:
        return col - Z_OFF
    grp, o = divmod(col, GROUP_COLS)
    if o < GROUP_X:
        return D_INNER + GROUP_X * grp + o
    if o < GROUP_X + D_STATE:
        return 2 * D_INNER + D_STATE * grp + (o - GROUP_X)
    if o < GROUP_CONV:
        return 2 * D_INNER + N_GROUPS * D_STATE + D_STATE * grp + (o - GROUP_X - D_STATE)
    h = o - GROUP_CONV
    return D_INNER + CONV_DIM + HEADS_PER_GROUP * grp + h if h < HEADS_PER_GROUP else -1


def _overlap_tables():
    nblk = ZX_COLS // COL_BLK
    src = [_zx_source_col_py(c) for c in range(ZX_COLS)]
    fwd = [sorted({s // W_IN_SHARD for s in src[COL_BLK * j:COL_BLK * (j + 1)] if s >= 0}) for j in range(nblk)]
    dst = {s: c for c, s in enumerate(src) if s >= 0}
    bwd = [sorted({dst[s] // COL_BLK for s in range(W_IN_SHARD * k, W_IN_SHARD * (k + 1))}) for k in range(N_DEV)]

    def flat(rows):
        width = max(len(r) for r in rows)
        idx = [r + [r[-1]] * (width - len(r)) for r in rows]
        val = [[1] * len(r) + [0] * (width - len(r)) for r in rows]
        return (jnp.asarray(sum(idx, []), jnp.int32), jnp.asarray(sum(val, []), jnp.int32), width)

    return flat(fwd), flat(bwd)


def _w_in_to_zx(w_in_g):
    (tab, val, width), _ = _overlap_tables()
    nblk = ZX_COLS // COL_BLK

    def body(tab_ref, val_ref, w_ref, o_ref, acc):
        j = pl.program_id(0)
        s = pl.program_id(1)

        @pl.when(s == 0)
        def _():
            acc[...] = jnp.zeros_like(acc)

        @pl.when(val_ref[j * width + s] == 1)
        def _():
            k = tab_ref[j * width + s]
            col = COL_BLK * j + lax.broadcasted_iota(jnp.int32, (W_IN_SHARD, COL_BLK), 1)
            row = W_IN_SHARD * k + lax.broadcasted_iota(jnp.int32, (W_IN_SHARD, COL_BLK), 0)
            place = _one(_zx_source_col(col) == row).astype(BF16)
            acc[...] += jnp.dot(w_ref[...], place, preferred_element_type=F32)

        @pl.when(s == width - 1)
        def _():
            o_ref[...] = acc[...].astype(BF16)

    return pl.pallas_call(
        body, name="w_in_to_zx",
        grid_spec=pltpu.PrefetchScalarGridSpec(
            num_scalar_prefetch=2, grid=(nblk, width),
            in_specs=[pl.BlockSpec((None, D_MODEL, W_IN_SHARD), lambda j, s, tab, val: (tab[j * width + s], 0, 0))],
            out_specs=pl.BlockSpec((D_MODEL, COL_BLK), lambda j, s, tab, val: (0, j)),
            scratch_shapes=[pltpu.VMEM((D_MODEL, COL_BLK), F32)]),
        out_shape=jax.ShapeDtypeStruct((D_MODEL, ZX_COLS), BF16),
        compiler_params=_cp(("arbitrary", "arbitrary")),
    )(tab, val, w_in_g)


def _zx_to_w_in(d_wzx):
    _, (tab, val, width) = _overlap_tables()

    def body(tab_ref, val_ref, d_ref, o_ref, acc):
        k = pl.program_id(0)
        s = pl.program_id(1)

        @pl.when(s == 0)
        def _():
            acc[...] = jnp.zeros_like(acc)

        @pl.when(val_ref[k * width + s] == 1)
        def _():
            j = tab_ref[k * width + s]
            col = COL_BLK * j + lax.broadcasted_iota(jnp.int32, (COL_BLK, W_IN_SHARD), 0)
            row = W_IN_SHARD * k + lax.broadcasted_iota(jnp.int32, (COL_BLK, W_IN_SHARD), 1)
            place = _one(_zx_source_col(col) == row).astype(BF16)
            acc[...] += jnp.dot(d_ref[...], place, preferred_element_type=F32)

        @pl.when(s == width - 1)
        def _():
            o_ref[...] = acc[...].astype(BF16)

    return pl.pallas_call(
        body, name="zx_to_w_in",
        grid_spec=pltpu.PrefetchScalarGridSpec(
            num_scalar_prefetch=2, grid=(N_DEV, width),
            in_specs=[pl.BlockSpec((D_MODEL, COL_BLK), lambda k, s, tab, val: (0, tab[k * width + s]))],
            out_specs=pl.BlockSpec((None, D_MODEL, W_IN_SHARD), lambda k, s, tab, val: (k, 0, 0)),
            scratch_shapes=[pltpu.VMEM((D_MODEL, W_IN_SHARD), F32)]),
        out_shape=jax.ShapeDtypeStruct((N_DEV, D_MODEL, W_IN_SHARD), BF16),
        compiler_params=_cp(("arbitrary", "arbitrary")),
    )(tab, val, d_wzx)


def _group_conv_cols(a):
    rows = a.shape[0]
    x = a[:, :D_INNER].reshape(rows, N_GROUPS, GROUP_X)
    b = a[:, D_INNER:D_INNER + N_GROUPS * D_STATE].reshape(rows, N_GROUPS, D_STATE)
    c = a[:, D_INNER + N_GROUPS * D_STATE:].reshape(rows, N_GROUPS, D_STATE)
    return jnp.concatenate([x, b, c], axis=2).reshape(rows, N_GROUPS * GROUP_CONV)


def _ungroup_conv_cols(a):
    rows = a.shape[0]
    a3 = a.reshape(rows, N_GROUPS, GROUP_CONV)
    return jnp.concatenate([a3[:, :, :GROUP_X].reshape(rows, D_INNER),
                            a3[:, :, GROUP_X:GROUP_X + D_STATE].reshape(rows, N_GROUPS * D_STATE),
                            a3[:, :, GROUP_X + D_STATE:].reshape(rows, N_GROUPS * D_STATE)], axis=1)


def _small_shard(conv_w, conv_b, norm_g):
    ng = jnp.pad(norm_g.reshape(1, -1), ((0, 0), (0, CONV_SHARD - norm_g.shape[-1])))
    return jnp.concatenate([conv_w.reshape(4, CONV_SHARD), conv_b.reshape(1, CONV_SHARD), ng,
                            jnp.zeros((SMALL_ROWS - 6, CONV_SHARD), F32)], axis=0)


def _small_unshard(a):
    return a[0:4].reshape(1, 4, CONV_SHARD), a[4:5], a[5:6, :D_INNER // N_DEV]


def _heads_of(a):
    return a[:, :, :HEADS_PER_GROUP].reshape(1, N_HEADS)


def _head_params(p):
    return jnp.pad(p.reshape(N_GROUPS, 1, HEADS_PER_GROUP), ((0, 0), (0, 0), (0, 128 - HEADS_PER_GROUP)))


def _update(w, land, m, v, name):
    shp = w.shape
    to2 = lambda a: a.reshape(-1, shp[-1])
    return tuple(o.reshape(shp) for o in _adamw_reduced(to2(w), land, to2(m), to2(v), name))


def kernel(x, norm_mix_g, norm_mlp_g, pool_w, pool_b, pool_scale, ssm_w_in, ssm_conv_w, ssm_conv_b, ssm_dt_bias, ssm_a_log, ssm_d, ssm_norm_g, ssm_w_out, mlp_w1, mlp_w2, final_g, loss_target, m_norm_mix_g, m_norm_mlp_g, m_pool_w, m_pool_b, m_pool_scale, m_ssm_w_in, m_ssm_conv_w, m_ssm_conv_b, m_ssm_dt_bias, m_ssm_a_log, m_ssm_d, m_ssm_norm_g, m_ssm_w_out, m_mlp_w1, m_mlp_w2, m_final_g, v_norm_mix_g, v_norm_mlp_g, v_pool_w, v_pool_b, v_pool_scale, v_ssm_w_in, v_ssm_conv_w, v_ssm_conv_b, v_ssm_dt_bias, v_ssm_a_log, v_ssm_d, v_ssm_norm_g, v_ssm_w_out, v_mlp_w1, v_mlp_w2, v_final_g):
    x2 = x[0]
    tgt = loss_target[0]
    gm0, gm1 = norm_mix_g[0:1], norm_mix_g[1:2]
    gl0, gl1 = norm_mlp_g[0:1], norm_mlp_g[1:2]
    gfin = final_g.reshape(1, D_MODEL)

    fb = D_FF // N_DEV

    def bf(a):
        return a.astype(BF16)

    def gather_of(shards):
        return _direct_exchange(shards, [(i, 0) for i in range(len(shards))],
                                [(s.shape, s.dtype) for s in shards], scatter=False)

    def scatter_of(parts, rows=None):
        shapes = [((p.shape[1] if rows is None else rows[1], p.shape[2]), p.dtype) for p in parts]
        return _direct_exchange(parts, [(i, 0) for i in range(len(parts))], shapes, scatter=True,
                                src_rows=None if rows is None else [rows] * len(parts))

    w_pool, small_g = _run_exchange(_two_level_gather(
        [bf(pool_w.reshape(4 * POOL_SHARD, POOL_GROUP)), _small_shard(ssm_conv_w, ssm_conv_b, ssm_norm_g)]),
        "gather_first")
    conv_w = _group_conv_cols(small_g[:, 0:4].transpose(1, 0, 2).reshape(4, CONV_DIM))
    conv_b = _group_conv_cols(small_g[:, 4].reshape(1, CONV_DIM))
    ssm_ng = small_g[:, 5, :D_INNER // N_DEV].reshape(1, D_INNER)
    dtb, alog, dsk = _head_params(ssm_dt_bias), _head_params(ssm_a_log), _head_params(ssm_d)

    (h1,), (w1g0, w2g0) = _pool_fwd(x2, gm0, w_pool, pool_b, pool_scale,
                                    carried=_two_level_gather([bf(mlp_w1[0]), bf(mlp_w2[0])], mid_percent=100))
    (h2, u0, hm0), (w_in_g,) = _mlp_fwd(h1, gl0, w1g0, w2g0, "mlp0_fwd",
                                        carried=_two_level_gather([bf(ssm_w_in[0])]))
    w_zx = _w_in_to_zx(w_in_g)
    (zx, hn1), (w_out_g,) = _norm_matmul(h2, gm1, w_zx, carried=gather_of([bf(ssm_w_out[0])]))
    (y_ssd, states), (w1g1, w2g1) = _ssd_fwd(zx, conv_w, conv_b, dtb, alog, dsk,
                                             carried=_two_level_gather([bf(mlp_w1[1]), bf(mlp_w2[1])]))
    w_out = w_out_g.reshape(D_INNER, D_MODEL)
    h3 = _ssm_out_fwd(y_ssd, zx, ssm_ng, w_out, h2)
    (h4, u1, hm1), _ = _mlp_fwd(h3, gl1, w1g1, w2g1, "mlp1_fwd")
    dh4, dh4b, loss_row, d_gfin = _final(h4, gfin, tgt)

    (dh3, dh3b, da1, d_gl1), _ = _mlp_bwd(dh4, dh4b, h3, gl1, u1, w1g1, w2g1, "mlp1_bwd")
    d_w1_1 = _matmul_tn(hm1, da1, "mlp1_dw1", col_blocked=True)
    d_w2_1 = _matmul_tn(u1, dh4b, "mlp1_dw2", square_a=True).reshape(N_DEV, fb, D_MODEL)
    dy_ssd, dzx, yn, d_ng = _ssm_out_bwd(dh3b, y_ssd, zx, ssm_ng, w_out)
    d_wout = _matmul_tn(yn, dh3b, "ssm_dw_out").reshape(N_DEV, D_INNER // N_DEV, D_MODEL)
    (dzx, d_cw, d_cb, d_dtb, d_alog, d_dsk), (l_w1_1, l_w2_1, l_wout) = _ssd_bwd(
        zx, conv_w, conv_b, dtb, alog, dsk, states, dy_ssd, dzx, carried=scatter_of([d_w1_1, d_w2_1, d_wout]))
    d_w_in = _zx_to_w_in(_matmul_tn(hn1, dzx, "ssm_dw_in"))
    half = D_MODEL // 2
    (dh2, dh2b, d_gm1), (l_w_in_a,) = _in_proj_bwd(dzx, w_zx, h2, gm1, dh3, carried=scatter_of([d_w_in], (0, half)))
    d_w2_0, (l_w_in_b,) = _matmul_tn(u0, dh2b, "mlp0_dw2", square_a=True, carried=scatter_of([d_w_in], (half, half)))
    d_w2_0 = d_w2_0.reshape(N_DEV, fb, D_MODEL)
    (dh1, _, da0, d_gl0), (l_w2_0,) = _mlp_bwd(dh2, dh2b, h1, gl0, u0, w1g0, w2g0, "mlp0_bwd",
                                           carried=scatter_of([d_w2_0]))
    d_w1_0 = _matmul_tn(hm0, da0, "mlp0_dw1", col_blocked=True)
    (dx, d_pool, d_pb, d_ps, d_gm0), (l_w1_0,) = _pool_bwd(x2, dh1, gm0, w_pool, pool_b, pool_scale,
                                                          carried=scatter_of([d_w1_0]))

    d_conv_w = _ungroup_conv_cols(d_cw).reshape(4, N_DEV, CONV_SHARD).transpose(1, 0, 2)
    d_conv_b = _ungroup_conv_cols(d_cb).reshape(N_DEV, 1, CONV_SHARD)
    d_gain = jnp.pad(d_ng.reshape(N_DEV, 1, D_INNER // N_DEV), ((0, 0), (0, 0), (0, CONV_SHARD - D_INNER // N_DEV)))
    d_small = jnp.concatenate([d_conv_w, d_conv_b, d_gain,
                               jnp.zeros((N_DEV, SMALL_ROWS - 6, CONV_SHARD), F32)], axis=1)
    l_pool, l_small = _run_exchange(scatter_of([bf(d_pool), d_small]), "reduce_scatter_tail")

    heads = jnp.concatenate([_heads_of(a) for a in (d_dtb, d_alog, d_dsk)], axis=1)
    sp = jnp.concatenate([d_gm0, d_gm1, d_gl0, d_gl1, d_pb, d_ps, d_gfin,
                          jnp.pad(heads, ((0, 0), (0, D_MODEL - 3 * N_HEADS)))], axis=0)
    sg = _all_reduce_small(sp)

    g_norm_mix = sg[0:2]
    g_norm_mlp = sg[2:4]
    g_pool_b, g_pool_scale = sg[4:5], sg[5:6]
    g_final = sg[6]
    g_dtb, g_alog, g_dsk = sg[7:8, 0:32], sg[7:8, 32:64], sg[7:8, 64:96]

    def rep_pack(nm, nl, pb, ps, fg, db, al, dk):
        hd = jnp.pad(jnp.concatenate([db, al, dk], axis=1), ((0, 0), (0, D_MODEL - 3 * N_HEADS)))
        return jnp.concatenate([nm, nl, pb, ps, fg.reshape(1, D_MODEL), hd], axis=0)

    rep = [rep_pack(*t) for t in (
        (norm_mix_g, norm_mlp_g, pool_b, pool_scale, final_g, ssm_dt_bias, ssm_a_log, ssm_d),
        (g_norm_mix, g_norm_mlp, g_pool_b, g_pool_scale, g_final, g_dtb, g_alog, g_dsk),
        (m_norm_mix_g, m_norm_mlp_g, m_pool_b, m_pool_scale, m_final_g, m_ssm_dt_bias, m_ssm_a_log, m_ssm_d),
        (v_norm_mix_g, v_norm_mlp_g, v_pool_b, v_pool_scale, v_final_g, v_ssm_dt_bias, v_ssm_a_log, v_ssm_d))]
    rep_out = _adamw(*rep, "adamw_replicated")

    def rep_unpack(a):
        return (a[0:2], a[2:4], a[4:5], a[5:6], a[6], a[7:8, 0:32], a[7:8, 32:64], a[7:8, 64:96])

    sm_out = _adamw_reduced(_small_shard(ssm_conv_w, ssm_conv_b, ssm_norm_g), l_small,
                            _small_shard(m_ssm_conv_w, m_ssm_conv_b, m_ssm_norm_g),
                            _small_shard(v_ssm_conv_w, v_ssm_conv_b, v_ssm_norm_g), "adamw_small_shards")

    big = {
        "pool_w": _update(pool_w, l_pool, m_pool_w, v_pool_w, "adamw_pool_w"),
        "ssm_w_in": tuple(o.reshape(ssm_w_in.shape) for o in _adamw_reduced_layers(
            ssm_w_in.reshape(2, half, W_IN_SHARD), (l_w_in_a, l_w_in_b), m_ssm_w_in.reshape(2, half, W_IN_SHARD),
            v_ssm_w_in.reshape(2, half, W_IN_SHARD), "adamw_w_in")),
        "ssm_w_out": _update(ssm_w_out, l_wout, m_ssm_w_out, v_ssm_w_out, "adamw_w_out"),
        "mlp_w1": _adamw_reduced_layers(mlp_w1, (l_w1_0, l_w1_1), m_mlp_w1, v_mlp_w1, "adamw_w1"),
        "mlp_w2": _adamw_reduced_layers(mlp_w2, (l_w2_0, l_w2_1), m_mlp_w2, v_mlp_w2, "adamw_w2"),
    }
    rep_all = (rep[1],) + tuple(rep_out)

    def ordered(kind):
        nm, nl, pb, ps, fg, db, al, dk = rep_unpack(rep_all[kind])
        cw, cb, ng = _small_unshard(sm_out[kind])
        return [nm, nl, big["pool_w"][kind], pb, ps, big["ssm_w_in"][kind], cw, cb, db, al, dk, ng,
                big["ssm_w_out"][kind], big["mlp_w1"][kind], big["mlp_w2"][kind], fg]

    loss = lax.psum(loss_row[0, 0], ("x", "y", "c"))
    return (loss, dx[None], *ordered(0), *ordered(1), *ordered(2), *ordered(3))
```

```python
import functools

import jax
import jax.numpy as jnp
from jax import lax
from jax.experimental import pallas as pl
from jax.experimental.pallas import tpu as pltpu

F32 = jnp.float32
BF16 = jnp.bfloat16
MESH = pl.DeviceIdType.MESH

D_MODEL = 1024
RMS_EPS = 1e-5
POOL_WINDOWS = (2, 4, 8, 16)
POOL_GROUP = 256
POOL_HALO = 16
POOL_SHARD = POOL_GROUP // 8
D_INNER = 2048
HEAD_DIM = 64
N_HEADS = 32
N_GROUPS = 4
HEADS_PER_GROUP = 8
D_STATE = 128
CHUNK = 128
CONV_DIM = 3072
IN_PROJ_DIM = 5152
D_FF = 4096
N_DEV = 8
GROUP_X = HEADS_PER_GROUP * HEAD_DIM
GROUP_CONV = GROUP_X + 2 * D_STATE
GROUP_COLS = GROUP_CONV + 128
Z_OFF = N_GROUPS * GROUP_COLS
ZX_COLS = Z_OFF + D_INNER
COL_BLK = 512
W_IN_SHARD = IN_PROJ_DIM // N_DEV

ADAM_LR = 0.001
ADAM_B1 = 0.9
ADAM_B2 = 0.999
ADAM_EPS = 1e-08
ADAM_WD = 0.01
ADAM_STEP = 10

VMEM_LIMIT_V7X = 56 * 1024 * 1024
MID_STEP_PERCENT = 70
TN_TOKENS = 512
TN_ACC_BYTES = 16 * 1024 * 1024
MATMUL_TOKENS = 1024

CONV_SHARD = CONV_DIM // N_DEV
SMALL_ROWS = 8

_NN = (((1,), (0,)), ((), ()))
_NT = (((1,), (1,)), ((), ()))
_TN = (((0,), (0,)), ((), ()))


def _cp(sem):
    return pltpu.CompilerParams(dimension_semantics=sem, vmem_limit_bytes=VMEM_LIMIT_V7X)


_ANY = pl.BlockSpec(memory_space=pl.ANY)


def _place():
    return lax.axis_index("x"), lax.axis_index("y"), lax.axis_index("c")


class _Carried:
    def __init__(self, ins, outs, sems, start, finish, mid=None, mid_percent=None):
        self.ins, self.outs, self.sems = list(ins), list(outs), list(sems)
        self.start, self.mid, self.finish, self.mid_percent = start, mid, finish, mid_percent


def _pcall(body, *, name, grid, in_specs, out_specs, out_shape, sem, args, scratch_shapes=(), carried=None,
           aliases=None):
    in_specs, out_specs, out_shape, scratch = list(in_specs), list(out_specs), list(out_shape), list(scratch_shapes)
    common = dict(name=name, grid=grid, input_output_aliases=aliases or {}, compiler_params=_cp(sem))
    if carried is None:
        res = pl.pallas_call(body, in_specs=in_specs, out_specs=out_specs, out_shape=out_shape,
                             scratch_shapes=scratch, **common)(*args)
        return list(res), []
    n_in, n_out, n_scr = len(in_specs), len(out_specs), len(scratch)
    ci, co = len(carried.ins), len(carried.outs)

    def wrapped(*refs):
        ins, cins = refs[:n_in], refs[n_in:n_in + ci]
        p = n_in + ci
        outs, couts = refs[p:p + n_out], refs[p + n_out:p + n_out + co]
        p += n_out + co
        scr, csems = refs[p:p + n_scr], refs[p + n_scr:]
        ids = [pl.program_id(a) for a in range(len(grid))]
        first = functools.reduce(jnp.logical_and, [i == 0 for i in ids])
        last = functools.reduce(jnp.logical_and, [i == g - 1 for i, g in zip(ids, grid)])

        @pl.when(first)
        def _():
            carried.start(cins, couts, csems)

        if carried.mid is not None:
            step, steps = 0, 1
            for i, g in zip(ids, grid):
                step, steps = step * g + i, steps * g

            @pl.when(step == min(steps - 1, (steps * carried.mid_percent) // 100))
            def _():
                carried.mid(cins, couts, csems)

        body(*ins, *outs, *scr)

        @pl.when(last)
        def _():
            carried.finish(cins, couts, csems)

    res = pl.pallas_call(wrapped, in_specs=in_specs + [_ANY] * ci, out_specs=out_specs + [_ANY] * co,
                         out_shape=out_shape + carried.outs, scratch_shapes=scratch + carried.sems,
                         **common)(*args, *carried.ins)
    return list(res[:n_out]), list(res[n_out:])


def _peers(x, y, c):
    out = []
    for rel in range(1, N_DEV):
        dx, dy, dc = (rel >> 2) & 1, (rel >> 1) & 1, rel & 1
        out.append((x + dx - 2 * x * dx, y + dy - 2 * y * dy, c + dc - 2 * c * dc))
    return out


def _direct_exchange(srcs, layout, out_shapes, scatter, src_rows=None):
    n = len(srcs)

    def copies(ins, outs, sems):
        send, recv, loc = sems
        x, y, c = _place()
        me = 4 * x + 2 * y + c
        out, arrive, local = [], [], []
        for i in range(n):
            j, off = layout[i]
            first, rows = (0, srcs[i].shape[-2]) if src_rows is None else src_rows[i]

            def piece(k):
                return ins[i].at[k, pl.ds(first, rows)] if scatter else ins[i]

            for r, peer in enumerate(_peers(x, y, c)):
                pidx = 4 * peer[0] + 2 * peer[1] + peer[2]
                kw = dict(send_sem=send.at[7 * i + r], recv_sem=recv.at[7 * i + r], device_id=peer, device_id_type=MESH)
                out.append(pltpu.make_async_remote_copy(
                    src_ref=piece(pidx), dst_ref=outs[j].at[me, pl.ds(off, rows)], **kw))
                arrive.append(pltpu.make_async_remote_copy(
                    src_ref=piece(pidx), dst_ref=outs[j].at[pidx, pl.ds(off, rows)], **kw))
            local.append(pltpu.make_async_copy(piece(me), outs[j].at[me, pl.ds(off, rows)], loc.at[i]))
        return out, arrive, local

    def start(ins, outs, sems):
        out, _, local = copies(ins, outs, sems)
        for cp in local + out:
            cp.start()

    def finish(ins, outs, sems):
        out, arrive, local = copies(ins, outs, sems)
        for cp in arrive:
            cp.wait_recv()
        for cp in out:
            cp.wait_send()
        for cp in local:
            cp.wait()

    return _Carried(srcs, [jax.ShapeDtypeStruct((N_DEV,) + tuple(s), d) for s, d in out_shapes],
                    [pltpu.SemaphoreType.DMA((7 * n,)), pltpu.SemaphoreType.DMA((7 * n,)),
                     pltpu.SemaphoreType.DMA((n,))], start, finish)


def _two_level_gather(shards, mid_percent=MID_STEP_PERCENT):
    n = len(shards)

    def copies(ins, outs, sems):
        send, recv, loc = sems
        x, y, c = _place()
        me, sibling = (x, y, c), (x, y, 1 - c)
        chips = [(1 - x, y), (x, 1 - y), (1 - x, 1 - y)]

        def win(i, place):
            return outs[i].at[4 * place[0] + 2 * place[1] + place[2]]

        def copy(i, k, block, to, src=None):
            return pltpu.make_async_remote_copy(
                src_ref=win(i, block) if src is None else src, dst_ref=win(i, block),
                send_sem=send.at[7 * i + k], recv_sem=recv.at[7 * i + k], device_id=to, device_id_type=MESH)

        own, passed, ici_in, d2d_in, local = [], [], [], [], []
        for i in range(n):
            own += [copy(i, 0, me, sibling, src=ins[i])]
            own += [copy(i, 1 + j, me, (*chip, c), src=ins[i]) for j, chip in enumerate(chips)]
            passed += [copy(i, 4 + j, (*chip, c), sibling) for j, chip in enumerate(chips)]
            ici_in += [copy(i, 1 + j, (*chip, c), me) for j, chip in enumerate(chips)]
            d2d_in += [copy(i, 0, sibling, me)] + [copy(i, 4 + j, (*chip, 1 - c), me) for j, chip in enumerate(chips)]
            local.append(pltpu.make_async_copy(ins[i], win(i, me), loc.at[i]))
        return own, passed, ici_in, d2d_in, local

    def start(ins, outs, sems):
        own, _, _, _, local = copies(ins, outs, sems)
        for cp in local + own:
            cp.start()

    def mid(ins, outs, sems):
        _, passed, ici_in, _, _ = copies(ins, outs, sems)
        for arrived, onward in zip(ici_in, passed):
            arrived.wait_recv()
            onward.start()

    def finish(ins, outs, sems):
        own, passed, _, d2d_in, local = copies(ins, outs, sems)
        for cp in d2d_in:
            cp.wait_recv()
        for cp in own + passed:
            cp.wait_send()
        for cp in local:
            cp.wait()

    return _Carried(shards, [jax.ShapeDtypeStruct((N_DEV,) + tuple(s.shape), s.dtype) for s in shards],
                    [pltpu.SemaphoreType.DMA((7 * n,)), pltpu.SemaphoreType.DMA((7 * n,)),
                     pltpu.SemaphoreType.DMA((n,))], start, finish, mid, mid_percent)


def _run_exchange(carried, name):
    ci = len(carried.ins)

    def body(*refs):
        ins, outs, sems = refs[:ci], refs[ci:ci + len(carried.outs)], refs[ci + len(carried.outs):]
        carried.start(ins, outs, sems)
        if carried.mid is not None:
            carried.mid(ins, outs, sems)
        carried.finish(ins, outs, sems)

    return list(pl.pallas_call(body, name=name, in_specs=[_ANY] * ci, out_specs=[_ANY] * len(carried.outs),
                               out_shape=carried.outs, scratch_shapes=carried.sems)(*carried.ins))


def _dg(a, b, dn):
    return lax.dot_general(a.astype(BF16), b.astype(BF16), dn, preferred_element_type=F32)


@jax.custom_vjp
def mm_nn(a, b):
    return _dg(a, b, _NN)


@jax.custom_vjp
def mm_nt(a, b):
    return _dg(a, b, _NT)


@jax.custom_vjp
def mm_tn(a, b):
    return _dg(a, b, _TN)


mm_nn.defvjp(lambda a, b: (_dg(a, b, _NN), (a, b)), lambda r, ct: (mm_nt(ct, r[1]), mm_tn(r[0], ct)))
mm_nt.defvjp(lambda a, b: (_dg(a, b, _NT), (a, b)), lambda r, ct: (mm_nn(ct, r[1]), mm_tn(ct, r[0])))
mm_tn.defvjp(lambda a, b: (_dg(a, b, _TN), (a, b)), lambda r, ct: (mm_nt(r[1], ct), mm_nn(r[0], ct)))


def _split3(x):
    p1 = x.astype(BF16)
    r1 = x - p1.astype(F32)
    p2 = r1.astype(BF16)
    r2 = r1 - p2.astype(F32)
    return p1, p2, r2.astype(BF16)


def _exact01(x, c, dn, const_left):
    acc = None
    for p in reversed(_split3(x)):
        t = (lax.dot_general(c, p, dn, preferred_element_type=F32) if const_left
             else lax.dot_general(p, c, dn, preferred_element_type=F32))
        acc = t if acc is None else acc + t
    return acc


def _make_cmm(dn, const_left, bwd_name):
    @jax.custom_vjp
    def f(x, c):
        return _exact01(x, c, dn, const_left)

    def fwd(x, c):
        return _exact01(x, c, dn, const_left), c

    def bwd(c, ct):
        return _CMM[bwd_name](ct, c), jnp.zeros_like(c)

    f.defvjp(fwd, bwd)
    return f


_CMM = {}
_CMM["xc"] = _make_cmm(_NN, False, "xct")
_CMM["xct"] = _make_cmm(_NT, False, "xc")
_CMM["cx"] = _make_cmm(_NN, True, "ctx")
_CMM["ctx"] = _make_cmm(_TN, True, "cx")


def _sigmoid(x):
    return 0.5 * jnp.tanh(0.5 * x) + 0.5


@jax.custom_vjp
def _silu(x):
    return x * _sigmoid(x)


def _silu_fwd(x):
    return _silu(x), x


def _silu_bwd(x, ct):
    s = _sigmoid(x)
    return (ct * (s * (1.0 + x * (1.0 - s))),)


_silu.defvjp(_silu_fwd, _silu_bwd)


def _log1p_pos(e):
    u = 1.0 + e
    d = u - 1.0
    return jnp.where(d == 0.0, e, jnp.log(u) * (e / jnp.where(d == 0.0, 1.0, d)))


@jax.custom_vjp
def _softplus(x):
    return jnp.maximum(x, 0.0) + _log1p_pos(jnp.exp(-jnp.abs(x)))


def _softplus_fwd(x):
    return _softplus(x), x


def _softplus_bwd(x, ct):
    return (ct * _sigmoid(x),)


_softplus.defvjp(_softplus_fwd, _softplus_bwd)


CONV_HALO = 8


def _make_shift(j):
    @jax.custom_vjp
    def f(ext):
        return pltpu.roll(ext, j, 0)[CONV_HALO:, :]

    def fwd(ext):
        return f(ext), None

    def bwd(_, ct):
        pad = jnp.concatenate([jnp.zeros((CONV_HALO, ct.shape[1]), ct.dtype), ct], axis=0)
        return (pltpu.roll(pad, CONV_HALO + CHUNK - j, 0),)

    f.defvjp(fwd, bwd)
    return f


_SHIFT = {j: _make_shift(j) for j in (1, 2, 3)}


@jax.custom_vjp
def _swap_halves(x):
    return pltpu.roll(x, HEAD_DIM, 1)


_swap_halves.defvjp(lambda x: (_swap_halves(x), None), lambda _, ct: (pltpu.roll(ct, HEAD_DIM, 1),))


def _rms_fwd(x, g):
    r = lax.rsqrt(jnp.mean(x * x, axis=-1, keepdims=True) + RMS_EPS)
    n = x * r
    return n * g, n, r


def _rms_bwd(dy, n, r, g):
    dn = dy * g
    dx = r * (dn - n * jnp.mean(dn * n, axis=-1, keepdims=True))
    dg = jnp.sum(dy * n, axis=0, keepdims=True)
    return dx, dg


def _one(cond):
    return jnp.where(cond, 1.0, 0.0)


def _pool_tile(xe, g, ws, b, scale, tile, tt):
    r = lax.rsqrt(jnp.mean(xe * xe, axis=-1, keepdims=True) + RMS_EPS)
    hn = xe * r * g
    row_e = lax.broadcasted_iota(jnp.int32, (tt + POOL_HALO, POOL_GROUP), 0)
    keep = _one(jnp.logical_or(row_e >= POOL_HALO, tile > 0))
    rr = lax.broadcasted_iota(jnp.int32, (tt, tt + POOL_HALO), 0)
    qq = lax.broadcasted_iota(jnp.int32, (tt, tt + POOL_HALO), 1)
    dd = qq - rr
    tpos = tile * tt + lax.broadcasted_iota(jnp.int32, (tt, POOL_GROUP), 0)
    outs = []
    for gi, w in enumerate(POOL_WINDOWS):
        hg = hn[:, gi * POOL_GROUP:(gi + 1) * POOL_GROUP] * keep
        band = _one(jnp.logical_and(dd >= POOL_HALO - w + 1, dd <= POOL_HALO)).astype(BF16)
        cnt = jnp.minimum(tpos + 1, w).astype(F32)
        pooled = _CMM["cx"](hg, band) / cnt
        mixed = pooled - hg[POOL_HALO:, :]
        outs.append(mm_nn(mixed, ws[gi]))
    out = (jnp.concatenate(outs, axis=1) + b) * scale
    return xe[POOL_HALO:, :] + out


def _pool_specs(tt, nt, rev):
    per = tt // POOL_HALO
    t_of = (lambda i: nt - 1 - i) if rev else (lambda i: i)
    main = pl.BlockSpec((tt, D_MODEL), lambda i: (t_of(i), 0))
    halo = pl.BlockSpec((POOL_HALO, D_MODEL), lambda i: (jnp.maximum(t_of(i) * per - 1, 0), 0))
    vec = pl.BlockSpec((1, D_MODEL), lambda i: (0, 0))
    wsp = pl.BlockSpec((N_DEV, 4 * POOL_SHARD, POOL_GROUP), lambda i: (0, 0, 0))
    return main, halo, vec, wsp


def _pool_weights(w_ref):
    return tuple(
        jnp.concatenate([w_ref[k, gi * POOL_SHARD:(gi + 1) * POOL_SHARD, :] for k in range(N_DEV)], axis=0).astype(F32)
        for gi in range(4))


def _pool_fwd(x, g, w, b, scale, carried=None):
    t = x.shape[0]
    tt = min(t, 256)
    nt = t // tt
    main, halo, vec, wsp = _pool_specs(tt, nt, False)

    def body(xm_ref, xh_ref, g_ref, w_ref, b_ref, s_ref, o_ref):
        i = pl.program_id(0)
        xe = jnp.concatenate([xh_ref[...], xm_ref[...]], axis=0)
        o_ref[...] = _pool_tile(xe, g_ref[...], _pool_weights(w_ref), b_ref[...], s_ref[...], i, tt)

    return _pcall(
        body, name="pool_fwd", grid=(nt,),
        in_specs=[main, halo, vec, wsp, vec, vec], out_specs=[main],
        out_shape=[jax.ShapeDtypeStruct((t, D_MODEL), F32)],
        sem=("arbitrary",), args=(x, x, g, w, b, scale), carried=carried)


def _pool_bwd(x, dh, g, w, b, scale, carried=None):
    t = x.shape[0]
    tt = min(t, 256)
    nt = t // tt
    main, halo, vec, wsp = _pool_specs(tt, nt, True)

    def body(xm_ref, xh_ref, dh_ref, g_ref, w_ref, b_ref, s_ref,
             dx_ref, dw_ref, db_ref, ds_ref, dg_ref, carry, dw_acc):
        i = pl.program_id(0)
        tile = nt - 1 - i

        @pl.when(i == 0)
        def _():
            carry[...] = jnp.zeros_like(carry)
            dw_acc[...] = jnp.zeros_like(dw_acc)
            db_ref[...] = jnp.zeros_like(db_ref)
            ds_ref[...] = jnp.zeros_like(ds_ref)
            dg_ref[...] = jnp.zeros_like(dg_ref)

        xe = jnp.concatenate([xh_ref[...], xm_ref[...]], axis=0)
        _, vjp = jax.vjp(lambda a, gg, ww, bb, ss: _pool_tile(a, gg, ww, bb, ss, tile, tt),
                         xe, g_ref[...], _pool_weights(w_ref), b_ref[...], s_ref[...])
        dxe, dgv, dws, dbv, dsv = vjp(dh_ref[...])
        dx_ref[...] = dxe[POOL_HALO:, :]
        dx_ref[tt - POOL_HALO:tt, :] += carry[...]
        carry[...] = dxe[:POOL_HALO, :]
        for gi in range(4):
            dw_acc[gi] += dws[gi]
        db_ref[...] += dbv
        ds_ref[...] += dsv
        dg_ref[...] += dgv

        @pl.when(i == nt - 1)
        def _():
            for k in range(N_DEV):
                for gi in range(4):
                    dw_ref[k, gi * POOL_SHARD:(gi + 1) * POOL_SHARD, :] = dw_acc[gi, k * POOL_SHARD:(k + 1) * POOL_SHARD, :]

    return _pcall(
        body, name="pool_bwd", grid=(nt,),
        in_specs=[main, halo, main, vec, wsp, vec, vec],
        out_specs=[main, wsp, vec, vec, vec],
        out_shape=[jax.ShapeDtypeStruct((t, D_MODEL), F32),
                   jax.ShapeDtypeStruct((N_DEV, 4 * POOL_SHARD, POOL_GROUP), F32),
                   jax.ShapeDtypeStruct((1, D_MODEL), F32),
                   jax.ShapeDtypeStruct((1, D_MODEL), F32),
                   jax.ShapeDtypeStruct((1, D_MODEL), F32)],
        scratch_shapes=[pltpu.VMEM((POOL_HALO, D_MODEL), F32), pltpu.VMEM((4, POOL_GROUP, POOL_GROUP), F32)],
        sem=("arbitrary",), args=(x, x, dh, g, w, b, scale), carried=carried)


def _mlp_weight_specs():
    fb = D_FF // N_DEV
    return (pl.BlockSpec((None, D_MODEL, fb), lambda i, k: (k, 0, 0)),
            pl.BlockSpec((None, fb, D_MODEL), lambda i, k: (k, 0, 0)))


def _mlp_fwd(h, g, w1g, w2g, name, carried=None):
    t = h.shape[0]
    tt = min(t, MATMUL_TOKENS)
    nk, fb = N_DEV, D_FF // N_DEV
    w1_spec, w2_spec = _mlp_weight_specs()

    def body(h_ref, g_ref, w1_ref, w2_ref, o_ref, u_ref, hm_ref, hm_s, acc_s):
        k = pl.program_id(1)

        @pl.when(k == 0)
        def _():
            xv = h_ref[...]
            y, _, _ = _rms_fwd(xv, g_ref[...])
            hb = y.astype(BF16)
            hm_s[...] = hb
            hm_ref[...] = hb
            acc_s[...] = xv

        a = jnp.dot(hm_s[...], w1_ref[...], preferred_element_type=F32)
        u = jnp.maximum(a, 0.0)
        u_ref[...] = u.astype(BF16)
        acc_s[...] += jnp.dot((u * u).astype(BF16), w2_ref[...], preferred_element_type=F32)

        @pl.when(k == nk - 1)
        def _():
            o_ref[...] = acc_s[...]

    return _pcall(
        body, name=name, grid=(t // tt, nk),
        in_specs=[pl.BlockSpec((tt, D_MODEL), lambda i, k: (i, 0)),
                  pl.BlockSpec((1, D_MODEL), lambda i, k: (0, 0)),
                  w1_spec, w2_spec],
        out_specs=[pl.BlockSpec((tt, D_MODEL), lambda i, k: (i, 0)),
                   pl.BlockSpec((tt, fb), lambda i, k: (i, k)),
                   pl.BlockSpec((tt, D_MODEL), lambda i, k: (i, 0))],
        out_shape=[jax.ShapeDtypeStruct((t, D_MODEL), F32),
                   jax.ShapeDtypeStruct((t, nk * fb), BF16),
                   jax.ShapeDtypeStruct((t, D_MODEL), BF16)],
        scratch_shapes=[pltpu.VMEM((tt, D_MODEL), BF16), pltpu.VMEM((tt, D_MODEL), F32)],
        sem=("arbitrary", "arbitrary"), args=(h, g, w1g, w2g), carried=carried)


def _mlp_bwd(dh, dhb, h, g, u, w1g, w2g, name, carried=None):
    t = h.shape[0]
    tt = min(t, MATMUL_TOKENS)
    nk, fb = N_DEV, D_FF // N_DEV
    w1_spec, w2_spec = _mlp_weight_specs()

    def body(dh_ref, dhb_ref, h_ref, g_ref, u_ref, w1_ref, w2_ref,
             dhin_ref, dhinb_ref, da_ref, dg_ref, acc_s):
        i = pl.program_id(0)
        k = pl.program_id(1)

        @pl.when(jnp.logical_and(i == 0, k == 0))
        def _():
            dg_ref[...] = jnp.zeros_like(dg_ref)

        @pl.when(k == 0)
        def _():
            acc_s[...] = jnp.zeros_like(acc_s)

        dv = lax.dot_general(dhb_ref[...], w2_ref[...], _NT, preferred_element_type=F32)
        dab = (dv * (2.0 * u_ref[...].astype(F32))).astype(BF16)
        da_ref[...] = dab
        acc_s[...] += lax.dot_general(dab, w1_ref[...], _NT, preferred_element_type=F32)

        @pl.when(k == nk - 1)
        def _():
            gv = g_ref[...]
            _, n, r = _rms_fwd(h_ref[...], gv)
            dx, dg = _rms_bwd(acc_s[...], n, r, gv)
            dhin = dh_ref[...] + dx
            dhin_ref[...] = dhin
            dhinb_ref[...] = dhin.astype(BF16)
            dg_ref[...] += dg

    tile = pl.BlockSpec((tt, D_MODEL), lambda i, k: (i, 0))
    return _pcall(
        body, name=name, grid=(t // tt, nk),
        in_specs=[tile, tile, tile, pl.BlockSpec((1, D_MODEL), lambda i, k: (0, 0)),
                  pl.BlockSpec((tt, fb), lambda i, k: (i, k)), w1_spec, w2_spec],
        out_specs=[tile, tile, pl.BlockSpec((tt, fb), lambda i, k: (i, k)),
                   pl.BlockSpec((1, D_MODEL), lambda i, k: (0, 0))],
        out_shape=[jax.ShapeDtypeStruct((t, D_MODEL), F32),
                   jax.ShapeDtypeStruct((t, D_MODEL), BF16),
                   jax.ShapeDtypeStruct((t, nk * fb), BF16),
                   jax.ShapeDtypeStruct((1, D_MODEL), F32)],
        scratch_shapes=[pltpu.VMEM((tt, D_MODEL), F32)],
        sem=("arbitrary", "arbitrary"), args=(dh, dhb, h, g, u, w1g, w2g), carried=carried)


def _matmul_tn(a, b, name, square_a=False, col_blocked=False, carried=None):
    t, k1 = a.shape
    k2 = b.shape[1]
    tt = min(t, TN_TOKENS)
    nt = t // tt
    wc = k2 if k1 * k2 * 4 <= TN_ACC_BYTES else k2 // 2
    nb = wc // COL_BLK

    def body(a_ref, b_ref, o_ref, acc):
        s = pl.program_id(1)

        @pl.when(s == 0)
        def _():
            acc[...] = jnp.zeros_like(acc)

        av = a_ref[...]
        if square_a:
            af = av.astype(F32)
            av = (af * af).astype(BF16)
        acc[...] += lax.dot_general(av, b_ref[...], _TN, preferred_element_type=F32)

        @pl.when(s == nt - 1)
        def _():
            if col_blocked:
                for k in range(nb):
                    o_ref[k] = acc[:, k * COL_BLK:(k + 1) * COL_BLK].astype(o_ref.dtype)
            else:
                o_ref[...] = acc[...].astype(o_ref.dtype)

    if col_blocked:
        out_shape = jax.ShapeDtypeStruct((k2 // COL_BLK, k1, COL_BLK), BF16)
        out_spec = pl.BlockSpec((nb, k1, COL_BLK), lambda j, s: (j, 0, 0))
    else:
        out_shape = jax.ShapeDtypeStruct((k1, k2), BF16)
        out_spec = pl.BlockSpec((k1, wc), lambda j, s: (0, j))
    outs, landed = _pcall(
        body, name=name, grid=(k2 // wc, nt),
        in_specs=[pl.BlockSpec((tt, k1), lambda j, s: (s, 0)),
                  pl.BlockSpec((tt, wc), lambda j, s: (s, j))],
        out_specs=[out_spec], out_shape=[out_shape],
        scratch_shapes=[pltpu.VMEM((k1, wc), F32)],
        sem=("arbitrary", "arbitrary"), args=(a, b), carried=carried)
    return (outs[0], landed) if carried is not None else outs[0]


def _norm_matmul(h, g, w, carried=None):
    t = h.shape[0]
    tt = min(t, MATMUL_TOKENS)
    n = w.shape[1]

    def body(h_ref, g_ref, w_ref, o_ref, hn_ref, hn_s):
        @pl.when(pl.program_id(1) == 0)
        def _():
            y, _, _ = _rms_fwd(h_ref[...], g_ref[...])
            hb = y.astype(BF16)
            hn_s[...] = hb
            hn_ref[...] = hb

        o_ref[...] = jnp.dot(hn_s[...], w_ref[...], preferred_element_type=F32)

    return _pcall(
        body, name="ssm_in_proj", grid=(t // tt, n // COL_BLK),
        in_specs=[pl.BlockSpec((tt, D_MODEL), lambda i, j: (i, 0)),
                  pl.BlockSpec((1, D_MODEL), lambda i, j: (0, 0)),
                  pl.BlockSpec((D_MODEL, COL_BLK), lambda i, j: (0, j))],
        out_specs=[pl.BlockSpec((tt, COL_BLK), lambda i, j: (i, j)),
                   pl.BlockSpec((tt, D_MODEL), lambda i, j: (i, 0))],
        out_shape=[jax.ShapeDtypeStruct((t, n), F32), jax.ShapeDtypeStruct((t, D_MODEL), BF16)],
        scratch_shapes=[pltpu.VMEM((tt, D_MODEL), BF16)],
        sem=("arbitrary", "arbitrary"), args=(h, g, w), carried=carried)


def _in_proj_bwd(dzx, w, h, g, dh_next, carried=None):
    t = h.shape[0]
    tt = min(t, MATMUL_TOKENS)
    n = w.shape[1]
    nj = n // COL_BLK

    def body(dz_ref, w_ref, h_ref, g_ref, dn_ref, dh_ref, dhb_ref, dg_ref, acc):
        i = pl.program_id(0)
        j = pl.program_id(1)

        @pl.when(jnp.logical_and(i == 0, j == 0))
        def _():
            dg_ref[...] = jnp.zeros_like(dg_ref)

        @pl.when(j == 0)
        def _():
            acc[...] = jnp.zeros_like(acc)

        acc[...] += lax.dot_general(dz_ref[...], w_ref[...], _NT, preferred_element_type=F32)

        @pl.when(j == nj - 1)
        def _():
            gv = g_ref[...]
            _, nn, r = _rms_fwd(h_ref[...], gv)
            dx, dg = _rms_bwd(acc[...], nn, r, gv)
            dh = dn_ref[...] + dx
            dh_ref[...] = dh
            dhb_ref[...] = dh.astype(BF16)
            dg_ref[...] += dg

    tile = pl.BlockSpec((tt, D_MODEL), lambda i, j: (i, 0))
    return _pcall(
        body, name="ssm_in_proj_bwd", grid=(t // tt, nj),
        in_specs=[pl.BlockSpec((tt, COL_BLK), lambda i, j: (i, j)),
                  pl.BlockSpec((D_MODEL, COL_BLK), lambda i, j: (0, j)),
                  tile, pl.BlockSpec((1, D_MODEL), lambda i, j: (0, 0)), tile],
        out_specs=[tile, tile, pl.BlockSpec((1, D_MODEL), lambda i, j: (0, 0))],
        out_shape=[jax.ShapeDtypeStruct((t, D_MODEL), F32), jax.ShapeDtypeStruct((t, D_MODEL), BF16),
                   jax.ShapeDtypeStruct((1, D_MODEL), F32)],
        scratch_shapes=[pltpu.VMEM((tt, D_MODEL), F32)],
        sem=("arbitrary", "arbitrary"), args=(dzx, w, h, g, dh_next), carried=carried)


def _ssd_consts():
    lane = lax.broadcasted_iota(jnp.int32, (CHUNK, CHUNK), 1)
    row = lax.broadcasted_iota(jnp.int32, (CHUNK, CHUNK), 0)
    causal = lane <= row
    tri = _one(causal).astype(BF16)
    er = lax.broadcasted_iota(jnp.int32, (CHUNK, GROUP_X), 0)
    ec = lax.broadcasted_iota(jnp.int32, (CHUNK, GROUP_X), 1)
    expand = _one(jnp.right_shift(ec, 6) == er).astype(BF16)
    return dict(causal=causal, tri=tri, expand=expand, lo=lane < HEAD_DIM)


def _conv_silu(cur, prev, w, b):
    ext = jnp.concatenate([prev, cur], axis=0)
    acc = cur * w[3] + b
    for j in (1, 2, 3):
        acc = acc + _SHIFT[j](ext) * w[3 - j]
    return _silu(acc)


def _ssd_chunk(raw, rawp, ht, cw, cb_, dtb, alog, dsk, k):
    act = _conv_silu(raw[:, :GROUP_CONV], rawp[:, :GROUP_CONV], cw, cb_)
    xs = act[:, :GROUP_X]
    bm = act[:, GROUP_X:GROUP_X + D_STATE]
    cm = act[:, GROUP_X + D_STATE:]
    dt = _softplus(raw[:, GROUP_CONV:] + dtb)
    a = -jnp.exp(alog)
    xc = _CMM["xc"]

    def lanes(rowv):
        return jnp.sum(xc(jnp.broadcast_to(rowv, (16, CHUNK)), k["expand"]), axis=0, keepdims=True) * (1.0 / 16.0)

    dt_e = xc(dt, k["expand"])
    adt_e = dt_e * lanes(a)
    acs_e = _CMM["cx"](adt_e, k["tri"])
    tot_e = jnp.sum(adt_e, axis=0, keepdims=True)
    gmat = mm_nt(cm, bm)
    xdt = xs * dt_e
    ys = []
    for j in range(HEADS_PER_GROUP // 2):
        pair = acs_e[:, j * CHUNK:(j + 1) * CHUNK]
        swapped = _swap_halves(pair)
        ms = []
        for cb in (jnp.where(k["lo"], pair, swapped), jnp.where(k["lo"], swapped, pair)):
            seg = cb - cb.T
            ms.append(gmat * jnp.exp(jnp.where(k["causal"], seg, -jnp.inf)))
        xp = xdt[:, j * CHUNK:(j + 1) * CHUNK]
        rhs = jnp.concatenate([jnp.where(k["lo"], xp, 0.0), jnp.where(k["lo"], 0.0, xp)], axis=0)
        ys.append(mm_nn(jnp.concatenate(ms, axis=1), rhs))
    y_diag = jnp.concatenate(ys, axis=1)
    y_off = jnp.exp(acs_e) * mm_nn(cm, ht)
    h_new = jnp.exp(tot_e) * ht + mm_tn(bm, xdt * jnp.exp(tot_e - acs_e))
    return y_diag + y_off + lanes(dsk) * xs, h_new


def _ssd_in_specs(nc, rev):
    c_of = (lambda c: nc - 1 - c) if rev else (lambda c: c)
    per = CHUNK // CONV_HALO
    zx = [pl.BlockSpec((CHUNK, GROUP_COLS), lambda g, c: (c_of(c), g)),
          pl.BlockSpec((CONV_HALO, GROUP_COLS), lambda g, c: (jnp.maximum(c_of(c) * per - 1, 0), g))]
    conv = [pl.BlockSpec((4, GROUP_CONV), lambda g, c: (0, g)), pl.BlockSpec((1, GROUP_CONV), lambda g, c: (0, g))]
    head = [pl.BlockSpec((None, 1, 128), lambda g, c: (g, 0, 0))] * 3
    return zx + conv + head, c_of


def _load_chunk_args(refs, has_prev):
    raw, rawp, cw, cb_, dtb, alog, dsk = refs
    return (raw[...], rawp[...] * has_prev, tuple(cw[pl.ds(i, 1), :] for i in range(4)), cb_[...],
            dtb[...], alog[...], dsk[...])


def _ssd_fwd(zx, conv_w, conv_b, dtb, alog, dsk, carried=None):
    t = zx.shape[0]
    nc = t // CHUNK
    in_specs, _ = _ssd_in_specs(nc, False)

    def body(*refs):
        ins, (y_ref, hs_ref, ht) = refs[:7], refs[7:]
        c = pl.program_id(1)

        @pl.when(c == 0)
        def _():
            ht[...] = jnp.zeros_like(ht)

        a = _load_chunk_args(ins, _one(c > 0))
        h_in = ht[...]
        y, h_new = _ssd_chunk(*a[:2], h_in, *a[2:], _ssd_consts())
        y_ref[...] = y
        hs_ref[...] = h_in
        ht[...] = h_new

    return _pcall(
        body, name="ssd_fwd", grid=(N_GROUPS, nc),
        in_specs=in_specs,
        out_specs=[pl.BlockSpec((CHUNK, GROUP_X), lambda g, c: (c, g)),
                   pl.BlockSpec((None, None, D_STATE, GROUP_X), lambda g, c: (g, c, 0, 0))],
        out_shape=[jax.ShapeDtypeStruct((t, D_INNER), F32),
                   jax.ShapeDtypeStruct((N_GROUPS, nc, D_STATE, GROUP_X), F32)],
        scratch_shapes=[pltpu.VMEM((D_STATE, GROUP_X), F32)],
        sem=("arbitrary", "arbitrary"), args=(zx, zx, conv_w, conv_b, dtb, alog, dsk), carried=carried)


def _ssd_bwd(zx, conv_w, conv_b, dtb, alog, dsk, hs, dy, dzx, carried=None):
    t = zx.shape[0]
    nc = t // CHUNK
    in_specs, c_of = _ssd_in_specs(nc, True)
    n_in = 10

    def body(*refs):
        ins, hs_ref, dy_ref = refs[:7], refs[7], refs[8]
        (draw_ref, dcw, dcb, ddtb, dalog, ddsk, dht, carry) = refs[n_in:]
        cc = pl.program_id(1)
        accs = (dcw, dcb, ddtb, dalog, ddsk)

        @pl.when(cc == 0)
        def _():
            for r in (dht, carry) + accs:
                r[...] = jnp.zeros_like(r)

        has_prev = _one(c_of(cc) > 0)
        a = _load_chunk_args(ins, has_prev)
        k = _ssd_consts()
        fn = lambda *args: _ssd_chunk(*args, k)
        _, vjp = jax.vjp(fn, *a[:2], hs_ref[...], *a[2:])
        graw, grawp, ght, gcw, gcb, gdtb, galog, gdsk = vjp((dy_ref[...], dht[...]))
        tail = jnp.concatenate([jnp.zeros((CHUNK - CONV_HALO, GROUP_COLS), F32), carry[...]], axis=0)
        draw_ref[...] = (graw + tail).astype(BF16)
        carry[...] = grawp * has_prev
        dht[...] = ght
        for i in range(4):
            dcw[pl.ds(i, 1), :] += gcw[i]
        for ref, val in ((dcb, gcb), (ddtb, gdtb), (dalog, galog), (ddsk, gdsk)):
            ref[...] += val

    head_out = pl.BlockSpec((None, 1, 128), lambda g, c: (g, 0, 0))
    sds = jax.ShapeDtypeStruct
    return _pcall(
        body, name="ssd_bwd", grid=(N_GROUPS, nc),
        in_specs=in_specs + [
            pl.BlockSpec((None, None, D_STATE, GROUP_X), lambda g, c: (g, c_of(c), 0, 0)),
            pl.BlockSpec((CHUNK, GROUP_X), lambda g, c: (c_of(c), g)),
            _ANY],
        out_specs=[pl.BlockSpec((CHUNK, GROUP_COLS), lambda g, c: (c_of(c), g)),
                   pl.BlockSpec((4, GROUP_CONV), lambda g, c: (0, g)),
                   pl.BlockSpec((1, GROUP_CONV), lambda g, c: (0, g)),
                   head_out, head_out, head_out],
        out_shape=[sds((t, ZX_COLS), BF16), sds((4, N_GROUPS * GROUP_CONV), F32), sds((1, N_GROUPS * GROUP_CONV), F32),
                   sds((N_GROUPS, 1, 128), F32), sds((N_GROUPS, 1, 128), F32), sds((N_GROUPS, 1, 128), F32)],
        scratch_shapes=[pltpu.VMEM((D_STATE, GROUP_X), F32), pltpu.VMEM((CONV_HALO, GROUP_COLS), F32)],
        sem=("arbitrary", "arbitrary"), args=(zx, zx, conv_w, conv_b, dtb, alog, dsk, hs, dy, dzx),
        aliases={9: 0}, carried=carried)


def _gate_norm(y, zs, ng):
    outs = []
    for k in range(N_GROUPS):
        s = y[:, k * GROUP_X:(k + 1) * GROUP_X] * _silu(zs[k])
        outs.append(s * lax.rsqrt(jnp.mean(s * s, axis=-1, keepdims=True) + RMS_EPS))
    return jnp.concatenate(outs, axis=1) * ng


def _z_specs(tt):
    first = Z_OFF // GROUP_X
    return [pl.BlockSpec((tt, GROUP_X), functools.partial(lambda k, i: (i, first + k), k)) for k in range(N_GROUPS)]


def _ssm_out_fwd(y, zx, ng, w_out, h):
    t = h.shape[0]
    tt = min(t, 256)

    def body(y_ref, z0, z1, z2, z3, ng_ref, w_ref, h_ref, o_ref):
        yn = _gate_norm(y_ref[...], (z0[...], z1[...], z2[...], z3[...]), ng_ref[...])
        o_ref[...] = h_ref[...] + jnp.dot(yn.astype(BF16), w_ref[...], preferred_element_type=F32)

    return pl.pallas_call(
        body, name="ssm_out_fwd", grid=(t // tt,),
        in_specs=[pl.BlockSpec((tt, D_INNER), lambda i: (i, 0))] + _z_specs(tt) + [
            pl.BlockSpec((1, D_INNER), lambda i: (0, 0)),
            pl.BlockSpec((D_INNER, D_MODEL), lambda i: (0, 0)),
            pl.BlockSpec((tt, D_MODEL), lambda i: (i, 0))],
        out_specs=pl.BlockSpec((tt, D_MODEL), lambda i: (i, 0)),
        out_shape=jax.ShapeDtypeStruct((t, D_MODEL), F32),
        compiler_params=_cp(("arbitrary",)),
    )(y, zx, zx, zx, zx, ng, w_out, h)


def _gate_norm_group(y, z, ng):
    s = y * _silu(z)
    return s * lax.rsqrt(jnp.mean(s * s, axis=-1, keepdims=True) + RMS_EPS) * ng


def _ssm_out_bwd(dhb, y, zx, ng, w_out):
    t = dhb.shape[0]
    tt = min(t, 512)
    first = Z_OFF // GROUP_X

    def body(dh_ref, y_ref, z_ref, ng_ref, w_ref, dy_ref, dzx_ref, yn_ref, dng_ref):
        @pl.when(pl.program_id(1) == 0)
        def _():
            dng_ref[...] = jnp.zeros_like(dng_ref)

        dyn = lax.dot_general(dh_ref[...], w_ref[...], _NT, preferred_element_type=F32)
        yn, vjp = jax.vjp(_gate_norm_group, y_ref[...], z_ref[...], ng_ref[...])
        dy, dz, dng = vjp(dyn)
        dy_ref[...] = dy
        dzx_ref[...] = dz.astype(BF16)
        yn_ref[...] = yn.astype(BF16)
        dng_ref[...] += dng

    grp = pl.BlockSpec((tt, GROUP_X), lambda k, i: (i, k))
    zgrp = pl.BlockSpec((tt, GROUP_X), lambda k, i: (i, first + k))
    gain = pl.BlockSpec((1, GROUP_X), lambda k, i: (0, k))
    return pl.pallas_call(
        body, name="ssm_out_bwd", grid=(N_GROUPS, t // tt),
        in_specs=[pl.BlockSpec((tt, D_MODEL), lambda k, i: (i, 0)), grp, zgrp, gain,
                  pl.BlockSpec((GROUP_X, D_MODEL), lambda k, i: (k, 0))],
        out_specs=[grp, zgrp, grp, gain],
        out_shape=[jax.ShapeDtypeStruct((t, D_INNER), F32), jax.ShapeDtypeStruct((t, ZX_COLS), BF16),
                   jax.ShapeDtypeStruct((t, D_INNER), BF16), jax.ShapeDtypeStruct((1, D_INNER), F32)],
        compiler_params=_cp(("arbitrary", "arbitrary")),
    )(dhb, y, zx, ng, w_out)


def _final(h, g, tgt):
    t = h.shape[0]
    tt = min(t, 512)
    nt = t // tt

    def body(h_ref, g_ref, t_ref, dh_ref, dhb_ref, loss_ref, dg_ref, lacc):
        i = pl.program_id(0)

        @pl.when(i == 0)
        def _():
            dg_ref[...] = jnp.zeros_like(dg_ref)
            lacc[...] = jnp.zeros_like(lacc)

        gv = g_ref[...]
        y, n, r = _rms_fwd(h_ref[...], gv)
        err = y - t_ref[...]
        lacc[...] += jnp.sum(err * err, axis=0, keepdims=True)
        dx, dg = _rms_bwd(err * (1.0 / D_MODEL), n, r, gv)
        dh_ref[...] = dx
        dhb_ref[...] = dx.astype(BF16)
        dg_ref[...] += dg

        @pl.when(i == nt - 1)
        def _():
            loss_ref[...] = jnp.zeros_like(loss_ref) + (0.5 / D_MODEL) * jnp.sum(lacc[...])

    tile = pl.BlockSpec((tt, D_MODEL), lambda i: (i, 0))
    vec = pl.BlockSpec((1, D_MODEL), lambda i: (0, 0))
    return pl.pallas_call(
        body, name="final_loss", grid=(nt,),
        in_specs=[tile, vec, tile],
        out_specs=[tile, tile, pl.BlockSpec((1, 128), lambda i: (0, 0)), vec],
        out_shape=[jax.ShapeDtypeStruct((t, D_MODEL), F32), jax.ShapeDtypeStruct((t, D_MODEL), BF16),
                   jax.ShapeDtypeStruct((1, 128), F32), jax.ShapeDtypeStruct((1, D_MODEL), F32)],
        scratch_shapes=[pltpu.VMEM((1, D_MODEL), F32)],
        compiler_params=_cp(("arbitrary",)),
    )(h, g, tgt)


def _adamw_reduced_parts(w, lands, m, v, name):
    rows, cols = w.shape
    br = 256
    nl = lands[0].shape[0]
    starts, blocks = [], []
    for land in lands:
        starts.append(sum(blocks))
        blocks.append(land.shape[1] // br)

    def body(w_ref, *refs):
        l_refs, (m_ref, v_ref, g_ref, d_ref, m2_ref, v2_ref) = refs[:len(lands)], refs[len(lands):]
        i = pl.program_id(0)
        gv = None
        for ref, first in zip(l_refs, starts):
            acc = ref[0].astype(F32)
            for q in range(1, nl):
                acc = acc + ref[q].astype(F32)
            gv = acc if gv is None else jnp.where(i >= first, acc, gv)
        g_ref[...] = gv
        d_ref[...], m2_ref[...], v2_ref[...] = _adamw_math(w_ref[...], gv, m_ref[...], v_ref[...])

    spec = pl.BlockSpec((br, cols), lambda i: (i, 0))
    land_specs = [pl.BlockSpec((nl, br, cols), functools.partial(
        lambda first, nb, i: (0, jnp.clip(i - first, 0, nb - 1), 0), first, nb)) for first, nb in zip(starts, blocks)]
    out = jax.ShapeDtypeStruct((rows, cols), F32)
    return pl.pallas_call(
        body, name=name, grid=(rows // br,),
        in_specs=[spec] + land_specs + [spec, spec], out_specs=[spec] * 4, out_shape=[out] * 4,
        compiler_params=_cp(("arbitrary",)),
    )(w, *lands, m, v)


def _adamw_reduced_layers(w, lands, m, v, name):
    _, rows, cols = w.shape
    br = rows if rows <= 256 else 256
    nb = rows // br
    nl = lands[0].shape[0]

    def body(w_ref, l0_ref, l1_ref, m_ref, v_ref, g_ref, d_ref, m2_ref, v2_ref):
        def total(ref):
            acc = ref[0].astype(F32)
            for q in range(1, nl):
                acc = acc + ref[q].astype(F32)
            return acc

        gv = jnp.where(pl.program_id(0) == 0, total(l0_ref), total(l1_ref))
        g_ref[...] = gv
        d_ref[...], m2_ref[...], v2_ref[...] = _adamw_math(w_ref[...], gv, m_ref[...], v_ref[...])

    spec = pl.BlockSpec((None, br, cols), lambda l, i: (l, i, 0))
    land0 = pl.BlockSpec((nl, br, cols), lambda l, i: (0, jnp.where(l == 0, i, nb - 1), 0))
    land1 = pl.BlockSpec((nl, br, cols), lambda l, i: (0, jnp.where(l == 1, i, 0), 0))
    out = jax.ShapeDtypeStruct(w.shape, F32)
    return pl.pallas_call(
        body, name=name, grid=(2, nb),
        in_specs=[spec, land0, land1, spec, spec], out_specs=[spec] * 4, out_shape=[out] * 4,
        compiler_params=_cp(("arbitrary", "arbitrary")),
    )(w, lands[0], lands[1], m, v)


def _all_reduce_small(sp):
    rows, n = sp.shape

    def body(x_ref, o_ref, land, send_sems, recv_sems):
        x, y, c = _place()
        me = 4 * x + 2 * y + c
        land[me] = x_ref[...]
        cps = []
        for rel in range(1, N_DEV):
            dx, dy, dc = (rel >> 2) & 1, (rel >> 1) & 1, rel & 1
            px = x + dx - 2 * x * dx
            py = y + dy - 2 * y * dy
            pc = c + dc - 2 * c * dc
            peer = 4 * px + 2 * py + pc
            cps.append((pltpu.make_async_remote_copy(
                src_ref=x_ref, dst_ref=land.at[me], send_sem=send_sems.at[rel - 1], recv_sem=recv_sems.at[rel - 1],
                device_id=(px, py, pc), device_id_type=MESH),
                pltpu.make_async_remote_copy(
                src_ref=x_ref, dst_ref=land.at[peer], send_sem=send_sems.at[rel - 1], recv_sem=recv_sems.at[rel - 1],
                device_id=(px, py, pc), device_id_type=MESH)))
        for cp, _ in cps:
            cp.start()
        for _, arr in cps:
            arr.wait_recv()
        for cp, _ in cps:
            cp.wait_send()
        acc = land[0]
        for k in range(1, N_DEV):
            acc = acc + land[k]
        o_ref[...] = acc

    vm = pl.BlockSpec(memory_space=pltpu.VMEM)
    return pl.pallas_call(
        body, name="all_reduce_small",
        out_shape=jax.ShapeDtypeStruct((rows, n), F32),
        in_specs=[vm], out_specs=vm,
        scratch_shapes=[pltpu.VMEM((N_DEV, rows, n), F32),
                        pltpu.SemaphoreType.DMA((N_DEV - 1,)), pltpu.SemaphoreType.DMA((N_DEV - 1,))],
    )(sp)


def _adamw_math(wv, gv, mv, vv):
    m2 = ADAM_B1 * mv + (1.0 - ADAM_B1) * gv
    v2 = ADAM_B2 * vv + (1.0 - ADAM_B2) * (gv * gv)
    m_hat = m2 / (1.0 - ADAM_B1 ** ADAM_STEP)
    v_hat = v2 / (1.0 - ADAM_B2 ** ADAM_STEP)
    return -ADAM_LR * (m_hat / (jnp.sqrt(v_hat) + ADAM_EPS) + ADAM_WD * wv), m2, v2


def _adamw(w, g, m, v, name):
    rows, cols = w.shape
    br = rows if rows <= 256 else 256

    def body(w_ref, g_ref, m_ref, v_ref, d_ref, m2_ref, v2_ref):
        d_ref[...], m2_ref[...], v2_ref[...] = _adamw_math(w_ref[...], g_ref[...], m_ref[...], v_ref[...])

    spec = pl.BlockSpec((br, cols), lambda i: (i, 0))
    out = jax.ShapeDtypeStruct((rows, cols), F32)
    return pl.pallas_call(
        body, name=name, grid=(rows // br,),
        in_specs=[spec] * 4, out_specs=[spec] * 3, out_shape=[out] * 3,
        compiler_params=_cp(("arbitrary",)),
    )(w, g, m, v)


def _adamw_reduced(w, land, m, v, name):
    rows, cols = w.shape
    br = rows if rows <= 256 else 256
    nl = land.shape[0]

    def body(w_ref, l_ref, m_ref, v_ref, g_ref, d_ref, m2_ref, v2_ref):
        gv = l_ref[0].astype(F32)
        for q in range(1, nl):
            gv = gv + l_ref[q].astype(F32)
        g_ref[...] = gv
        d_ref[...], m2_ref[...], v2_ref[...] = _adamw_math(w_ref[...], gv, m_ref[...], v_ref[...])

    spec = pl.BlockSpec((br, cols), lambda i: (i, 0))
    out = jax.ShapeDtypeStruct((rows, cols), F32)
    return pl.pallas_call(
        body, name=name, grid=(rows // br,),
        in_specs=[spec, pl.BlockSpec((nl, br, cols), lambda i: (0, i, 0)), spec, spec],
        out_specs=[spec] * 4, out_shape=[out] * 4,
        compiler_params=_cp(("arbitrary",)),
    )(w, land, m, v)


def _zx_source_col(col):
    blk = jnp.right_shift(col, 7)
    lane = jnp.bitwise_and(col, 127)
    per = GROUP_COLS // 128
    grp = jnp.where(blk >= per, 1, 0) + jnp.where(blk >= 2 * per, 1, 0) + jnp.where(blk >= 3 * per, 1, 0)
    o = blk - per * grp
    x_col = D_INNER + GROUP_X * grp + 128 * o + lane
    b_col = 2 * D_INNER + D_STATE * grp + lane
    c_col = 2 * D_INNER + N_GROUPS * D_STATE + D_STATE * grp + lane
    dt_col = jnp.where(lane < HEADS_PER_GROUP, D_INNER + CONV_DIM + HEADS_PER_GROUP * grp + lane, -1)
    src = jnp.where(o < 4, x_col, jnp.where(o == 4, b_col, jnp.where(o == 5, c_col, dt_col)))
    return jnp.where(col >= Z_OFF, col - Z_OFF, src)


def _zx_source_col_py(col):
    if col >= Z_OFF:
        return col - Z_OFF
    grp, o = divmod(col, GROUP_COLS)
    if o < GROUP_X:
        return D_INNER + GROUP_X * grp + o
    if o < GROUP_X + D_STATE:
        return 2 * D_INNER + D_STATE * grp + (o - GROUP_X)
    if o < GROUP_CONV:
        return 2 * D_INNER + N_GROUPS * D_STATE + D_STATE * grp + (o - GROUP_X - D_STATE)
    h = o - GROUP_CONV
    return D_INNER + CONV_DIM + HEADS_PER_GROUP * grp + h if h < HEADS_PER_GROUP else -1


def _overlap_tables():
    nblk = ZX_COLS // COL_BLK
    src = [_zx_source_col_py(c) for c in range(ZX_COLS)]
    fwd = [sorted({s // W_IN_SHARD for s in src[COL_BLK * j:COL_BLK * (j + 1)] if s >= 0}) for j in range(nblk)]
    dst = {s: c for c, s in enumerate(src) if s >= 0}
    bwd = [sorted({dst[s] // COL_BLK for s in range(W_IN_SHARD * k, W_IN_SHARD * (k + 1))}) for k in range(N_DEV)]

    def flat(rows):
        width = max(len(r) for r in rows)
        idx = [r + [r[-1]] * (width - len(r)) for r in rows]
        val = [[1] * len(r) + [0] * (width - len(r)) for r in rows]
        return (jnp.asarray(sum(idx, []), jnp.int32), jnp.asarray(sum(val, []), jnp.int32), width)

    return flat(fwd), flat(bwd)


def _w_in_to_zx(w_in_g):
    (tab, val, width), _ = _overlap_tables()
    nblk = ZX_COLS // COL_BLK

    def body(tab_ref, val_ref, w_ref, o_ref, acc):
        j = pl.program_id(0)
        s = pl.program_id(1)

        @pl.when(s == 0)
        def _():
            acc[...] = jnp.zeros_like(acc)

        @pl.when(val_ref[j * width + s] == 1)
        def _():
            k = tab_ref[j * width + s]
            col = COL_BLK * j + lax.broadcasted_iota(jnp.int32, (W_IN_SHARD, COL_BLK), 1)
            row = W_IN_SHARD * k + lax.broadcasted_iota(jnp.int32, (W_IN_SHARD, COL_BLK), 0)
            place = _one(_zx_source_col(col) == row).astype(BF16)
            acc[...] += jnp.dot(w_ref[...], place, preferred_element_type=F32)

        @pl.when(s == width - 1)
        def _():
            o_ref[...] = acc[...].astype(BF16)

    return pl.pallas_call(
        body, name="w_in_to_zx",
        grid_spec=pltpu.PrefetchScalarGridSpec(
            num_scalar_prefetch=2, grid=(nblk, width),
            in_specs=[pl.BlockSpec((None, D_MODEL, W_IN_SHARD), lambda j, s, tab, val: (tab[j * width + s], 0, 0))],
            out_specs=pl.BlockSpec((D_MODEL, COL_BLK), lambda j, s, tab, val: (0, j)),
            scratch_shapes=[pltpu.VMEM((D_MODEL, COL_BLK), F32)]),
        out_shape=jax.ShapeDtypeStruct((D_MODEL, ZX_COLS), BF16),
        compiler_params=_cp(("arbitrary", "arbitrary")),
    )(tab, val, w_in_g)


def _zx_to_w_in(d_wzx):
    _, (tab, val, width) = _overlap_tables()

    def body(tab_ref, val_ref, d_ref, o_ref, acc):
        k = pl.program_id(0)
        s = pl.program_id(1)

        @pl.when(s == 0)
        def _():
            acc[...] = jnp.zeros_like(acc)

        @pl.when(val_ref[k * width + s] == 1)
        def _():
            j = tab_ref[k * width + s]
            col = COL_BLK * j + lax.broadcasted_iota(jnp.int32, (COL_BLK, W_IN_SHARD), 0)
            row = W_IN_SHARD * k + lax.broadcasted_iota(jnp.int32, (COL_BLK, W_IN_SHARD), 1)
            place = _one(_zx_source_col(col) == row).astype(BF16)
            acc[...] += jnp.dot(d_ref[...], place, preferred_element_type=F32)

        @pl.when(s == width - 1)
        def _():
            o_ref[...] = acc[...].astype(BF16)

    return pl.pallas_call(
        body, name="zx_to_w_in",
        grid_spec=pltpu.PrefetchScalarGridSpec(
            num_scalar_prefetch=2, grid=(N_DEV, width),
            in_specs=[pl.BlockSpec((D_MODEL, COL_BLK), lambda k, s, tab, val: (0, tab[k * width + s]))],
            out_specs=pl.BlockSpec((None, D_MODEL, W_IN_SHARD), lambda k, s, tab, val: (k, 0, 0)),
            scratch_shapes=[pltpu.VMEM((D_MODEL, W_IN_SHARD), F32)]),
        out_shape=jax.ShapeDtypeStruct((N_DEV, D_MODEL, W_IN_SHARD), BF16),
        compiler_params=_cp(("arbitrary", "arbitrary")),
    )(tab, val, d_wzx)


def _group_conv_cols(a):
    rows = a.shape[0]
    x = a[:, :D_INNER].reshape(rows, N_GROUPS, GROUP_X)
    b = a[:, D_INNER:D_INNER + N_GROUPS * D_STATE].reshape(rows, N_GROUPS, D_STATE)
    c = a[:, D_INNER + N_GROUPS * D_STATE:].reshape(rows, N_GROUPS, D_STATE)
    return jnp.concatenate([x, b, c], axis=2).reshape(rows, N_GROUPS * GROUP_CONV)


def _ungroup_conv_cols(a):
    rows = a.shape[0]
    a3 = a.reshape(rows, N_GROUPS, GROUP_CONV)
    return jnp.concatenate([a3[:, :, :GROUP_X].reshape(rows, D_INNER),
                            a3[:, :, GROUP_X:GROUP_X + D_STATE].reshape(rows, N_GROUPS * D_STATE),
                            a3[:, :, GROUP_X + D_STATE:].reshape(rows, N_GROUPS * D_STATE)], axis=1)


def _small_shard(conv_w, conv_b, norm_g):
    ng = jnp.pad(norm_g.reshape(1, -1), ((0, 0), (0, CONV_SHARD - norm_g.shape[-1])))
    return jnp.concatenate([conv_w.reshape(4, CONV_SHARD), conv_b.reshape(1, CONV_SHARD), ng,
                            jnp.zeros((SMALL_ROWS - 6, CONV_SHARD), F32)], axis=0)


def _small_unshard(a):
    return a[0:4].reshape(1, 4, CONV_SHARD), a[4:5], a[5:6, :D_INNER // N_DEV]


def _heads_of(a):
    return a[:, :, :HEADS_PER_GROUP].reshape(1, N_HEADS)


def _head_params(p):
    return jnp.pad(p.reshape(N_GROUPS, 1, HEADS_PER_GROUP), ((0, 0), (0, 0), (0, 128 - HEADS_PER_GROUP)))


def _update(w, land, m, v, name):
    shp = w.shape
    to2 = lambda a: a.reshape(-1, shp[-1])
    return tuple(o.reshape(shp) for o in _adamw_reduced(to2(w), land, to2(m), to2(v), name))


def kernel(x, norm_mix_g, norm_mlp_g, pool_w, pool_b, pool_scale, ssm_w_in, ssm_conv_w, ssm_conv_b, ssm_dt_bias, ssm_a_log, ssm_d, ssm_norm_g, ssm_w_out, mlp_w1, mlp_w2, final_g, loss_target, m_norm_mix_g, m_norm_mlp_g, m_pool_w, m_pool_b, m_pool_scale, m_ssm_w_in, m_ssm_conv_w, m_ssm_conv_b, m_ssm_dt_bias, m_ssm_a_log, m_ssm_d, m_ssm_norm_g, m_ssm_w_out, m_mlp_w1, m_mlp_w2, m_final_g, v_norm_mix_g, v_norm_mlp_g, v_pool_w, v_pool_b, v_pool_scale, v_ssm_w_in, v_ssm_conv_w, v_ssm_conv_b, v_ssm_dt_bias, v_ssm_a_log, v_ssm_d, v_ssm_norm_g, v_ssm_w_out, v_mlp_w1, v_mlp_w2, v_final_g):
    x2 = x[0]
    tgt = loss_target[0]
    gm0, gm1 = norm_mix_g[0:1], norm_mix_g[1:2]
    gl0, gl1 = norm_mlp_g[0:1], norm_mlp_g[1:2]
    gfin = final_g.reshape(1, D_MODEL)

    fb = D_FF // N_DEV

    def bf(a):
        return a.astype(BF16)

    def gather_of(shards):
        return _direct_exchange(shards, [(i, 0) for i in range(len(shards))],
                                [(s.shape, s.dtype) for s in shards], scatter=False)

    def scatter_of(parts, rows=None):
        shapes = [((p.shape[1] if rows is None else rows[1], p.shape[2]), p.dtype) for p in parts]
        return _direct_exchange(parts, [(i, 0) for i in range(len(parts))], shapes, scatter=True,
                                src_rows=None if rows is None else [rows] * len(parts))

    w_pool, small_g = _run_exchange(_two_level_gather(
        [bf(pool_w.reshape(4 * POOL_SHARD, POOL_GROUP)), _small_shard(ssm_conv_w, ssm_conv_b, ssm_norm_g)]),
        "gather_first")
    conv_w = _group_conv_cols(small_g[:, 0:4].transpose(1, 0, 2).reshape(4, CONV_DIM))
    conv_b = _group_conv_cols(small_g[:, 4].reshape(1, CONV_DIM))
    ssm_ng = small_g[:, 5, :D_INNER // N_DEV].reshape(1, D_INNER)
    dtb, alog, dsk = _head_params(ssm_dt_bias), _head_params(ssm_a_log), _head_params(ssm_d)

    (h1,), (w1g0, w2g0) = _pool_fwd(x2, gm0, w_pool, pool_b, pool_scale,
                                    carried=_two_level_gather([bf(mlp_w1[0]), bf(mlp_w2[0])], mid_percent=100))
    (h2, u0, hm0), (w_in_g,) = _mlp_fwd(h1, gl0, w1g0, w2g0, "mlp0_fwd",
                                        carried=_two_level_gather([bf(ssm_w_in[0])]))
    w_zx = _w_in_to_zx(w_in_g)
    (zx, hn1), (w_out_g,) = _norm_matmul(h2, gm1, w_zx, carried=gather_of([bf(ssm_w_out[0])]))
    (y_ssd, states), (w1g1, w2g1) = _ssd_fwd(zx, conv_w, conv_b, dtb, alog, dsk,
                                             carried=_two_level_gather([bf(mlp_w1[1]), bf(mlp_w2[1])]))
    w_out = w_out_g.reshape(D_INNER, D_MODEL)
    h3 = _ssm_out_fwd(y_ssd, zx, ssm_ng, w_out, h2)
    (h4, u1, hm1), _ = _mlp_fwd(h3, gl1, w1g1, w2g1, "mlp1_fwd")
    dh4, dh4b, loss_row, d_gfin = _final(h4, gfin, tgt)

    (dh3, dh3b, da1, d_gl1), _ = _mlp_bwd(dh4, dh4b, h3, gl1, u1, w1g1, w2g1, "mlp1_bwd")
    d_w1_1 = _matmul_tn(hm1, da1, "mlp1_dw1", col_blocked=True)
    d_w2_1 = _matmul_tn(u1, dh4b, "mlp1_dw2", square_a=True).reshape(N_DEV, fb, D_MODEL)
    dy_ssd, dzx, yn, d_ng = _ssm_out_bwd(dh3b, y_ssd, zx, ssm_ng, w_out)
    d_wout = _matmul_tn(yn, dh3b, "ssm_dw_out").reshape(N_DEV, D_INNER // N_DEV, D_MODEL)
    (dzx, d_cw, d_cb, d_dtb, d_alog, d_dsk), (l_w1_1, l_w2_1, l_wout) = _ssd_bwd(
        zx, conv_w, conv_b, dtb, alog, dsk, states, dy_ssd, dzx, carried=scatter_of([d_w1_1, d_w2_1, d_wout]))
    d_w_in = _zx_to_w_in(_matmul_tn(hn1, dzx, "ssm_dw_in"))
    most = 3 * D_MODEL // 4
    (dh2, dh2b, d_gm1), (l_w_in_a,) = _in_proj_bwd(dzx, w_zx, h2, gm1, dh3, carried=scatter_of([d_w_in], (0, most)))
    d_w2_0, (l_w_in_b,) = _matmul_tn(u0, dh2b, "mlp0_dw2", square_a=True,
                                     carried=scatter_of([d_w_in], (most, D_MODEL - most)))
    d_w2_0 = d_w2_0.reshape(N_DEV, fb, D_MODEL)
    (dh1, _, da0, d_gl0), (l_w2_0,) = _mlp_bwd(dh2, dh2b, h1, gl0, u0, w1g0, w2g0, "mlp0_bwd",
                                           carried=scatter_of([d_w2_0]))
    d_w1_0 = _matmul_tn(hm0, da0, "mlp0_dw1", col_blocked=True)
    (dx, d_pool, d_pb, d_ps, d_gm0), (l_w1_0,) = _pool_bwd(x2, dh1, gm0, w_pool, pool_b, pool_scale,
                                                          carried=scatter_of([d_w1_0]))

    d_conv_w = _ungroup_conv_cols(d_cw).reshape(4, N_DEV, CONV_SHARD).transpose(1, 0, 2)
    d_conv_b = _ungroup_conv_cols(d_cb).reshape(N_DEV, 1, CONV_SHARD)
    d_gain = jnp.pad(d_ng.reshape(N_DEV, 1, D_INNER // N_DEV), ((0, 0), (0, 0), (0, CONV_SHARD - D_INNER // N_DEV)))
    d_small = jnp.concatenate([d_conv_w, d_conv_b, d_gain,
                               jnp.zeros((N_DEV, SMALL_ROWS - 6, CONV_SHARD), F32)], axis=1)
    l_pool, l_small = _run_exchange(scatter_of([bf(d_pool), d_small]), "reduce_scatter_tail")

    heads = jnp.concatenate([_heads_of(a) for a in (d_dtb, d_alog, d_dsk)], axis=1)
    sp = jnp.concatenate([d_gm0, d_gm1, d_gl0, d_gl1, d_pb, d_ps, d_gfin,
                          jnp.pad(heads, ((0, 0), (0, D_MODEL - 3 * N_HEADS)))], axis=0)
    sg = _all_reduce_small(sp)

    g_norm_mix = sg[0:2]
    g_norm_mlp = sg[2:4]
    g_pool_b, g_pool_scale = sg[4:5], sg[5:6]
    g_final = sg[6]
    g_dtb, g_alog, g_dsk = sg[7:8, 0:32], sg[7:8, 32:64], sg[7:8, 64:96]

    def rep_pack(nm, nl, pb, ps, fg, db, al, dk):
        hd = jnp.pad(jnp.concatenate([db, al, dk], axis=1), ((0, 0), (0, D_MODEL - 3 * N_HEADS)))
        return jnp.concatenate([nm, nl, pb, ps, fg.reshape(1, D_MODEL), hd], axis=0)

    rep = [rep_pack(*t) for t in (
        (norm_mix_g, norm_mlp_g, pool_b, pool_scale, final_g, ssm_dt_bias, ssm_a_log, ssm_d),
        (g_norm_mix, g_norm_mlp, g_pool_b, g_pool_scale, g_final, g_dtb, g_alog, g_dsk),
        (m_norm_mix_g, m_norm_mlp_g, m_pool_b, m_pool_scale, m_final_g, m_ssm_dt_bias, m_ssm_a_log, m_ssm_d),
        (v_norm_mix_g, v_norm_mlp_g, v_pool_b, v_pool_scale, v_final_g, v_ssm_dt_bias, v_ssm_a_log, v_ssm_d))]
    rep_out = _adamw(*rep, "adamw_replicated")

    def rep_unpack(a):
        return (a[0:2], a[2:4], a[4:5], a[5:6], a[6], a[7:8, 0:32], a[7:8, 32:64], a[7:8, 64:96])

    sm_out = _adamw_reduced(_small_shard(ssm_conv_w, ssm_conv_b, ssm_norm_g), l_small,
                            _small_shard(m_ssm_conv_w, m_ssm_conv_b, m_ssm_norm_g),
                            _small_shard(v_ssm_conv_w, v_ssm_conv_b, v_ssm_norm_g), "adamw_small_shards")

    big = {
        "pool_w": _update(pool_w, l_pool, m_pool_w, v_pool_w, "adamw_pool_w"),
        "ssm_w_in": tuple(o.reshape(ssm_w_in.shape) for o in _adamw_reduced_parts(
            ssm_w_in[0], (l_w_in_a, l_w_in_b), m_ssm_w_in[0], v_ssm_w_in[0], "adamw_w_in")),
        "ssm_w_out": _update(ssm_w_out, l_wout, m_ssm_w_out, v_ssm_w_out, "adamw_w_out"),
        "mlp_w1": _adamw_reduced_layers(mlp_w1, (l_w1_0, l_w1_1), m_mlp_w1, v_mlp_w1, "adamw_w1"),
        "mlp_w2": _adamw_reduced_layers(mlp_w2, (l_w2_0, l_w2_1), m_mlp_w2, v_mlp_w2, "adamw_w2"),
    }
    rep_all = (rep[1],) + tuple(rep_out)

    def ordered(kind):
        nm, nl, pb, ps, fg, db, al, dk = rep_unpack(rep_all[kind])
        cw, cb, ng = _small_unshard(sm_out[kind])
        return [nm, nl, big["pool_w"][kind], pb, ps, big["ssm_w_in"][kind], cw, cb, db, al, dk, ng,
                big["ssm_w_out"][kind], big["mlp_w1"][kind], big["mlp_w2"][kind], fg]

    loss = lax.psum(loss_row[0, 0], ("x", "y", "c"))
    return (loss, dx[None], *ordered(0), *ordered(1), *ordered(2), *ordered(3))
```

```python
import functools

import jax
import jax.numpy as jnp
from jax import lax
from jax.experimental import pallas as pl
from jax.experimental.pallas import tpu as pltpu

F32 = jnp.float32
BF16 = jnp.bfloat16
MESH = pl.DeviceIdType.MESH

D_MODEL = 1024
RMS_EPS = 1e-5
POOL_WINDOWS = (2, 4, 8, 16)
POOL_GROUP = 256
POOL_HALO = 16
POOL_SHARD = POOL_GROUP // 8
D_INNER = 2048
HEAD_DIM = 64
N_HEADS = 32
N_GROUPS = 4
HEADS_PER_GROUP = 8
D_STATE = 128
CHUNK = 128
CONV_DIM = 3072
IN_PROJ_DIM = 5152
D_FF = 4096
N_DEV = 8
GROUP_X = HEADS_PER_GROUP * HEAD_DIM
GROUP_CONV = GROUP_X + 2 * D_STATE
GROUP_COLS = GROUP_CONV + 128
Z_OFF = N_GROUPS * GROUP_COLS
ZX_COLS = Z_OFF + D_INNER
COL_BLK = 512
PROJ_BLK = ZX_COLS // 4
W_IN_SHARD = IN_PROJ_DIM // N_DEV

ADAM_LR = 0.001
ADAM_B1 = 0.9
ADAM_B2 = 0.999
ADAM_EPS = 1e-08
ADAM_WD = 0.01
ADAM_STEP = 10

VMEM_LIMIT_V7X = 56 * 1024 * 1024
MID_STEP_PERCENT = 70
TN_TOKENS = 512
TN_ACC_BYTES = 16 * 1024 * 1024
MATMUL_TOKENS = 1024

CONV_SHARD = CONV_DIM // N_DEV
SMALL_ROWS = 8

_NN = (((1,), (0,)), ((), ()))
_NT = (((1,), (1,)), ((), ()))
_TN = (((0,), (0,)), ((), ()))


def _cp(sem):
    return pltpu.CompilerParams(dimension_semantics=sem, vmem_limit_bytes=VMEM_LIMIT_V7X)


_ANY = pl.BlockSpec(memory_space=pl.ANY)


def _place():
    return lax.axis_index("x"), lax.axis_index("y"), lax.axis_index("c")


class _Carried:
    def __init__(self, ins, outs, sems, start, finish, mid=None, mid_percent=None):
        self.ins, self.outs, self.sems = list(ins), list(outs), list(sems)
        self.start, self.mid, self.finish, self.mid_percent = start, mid, finish, mid_percent


def _pcall(body, *, name, grid, in_specs, out_specs, out_shape, sem, args, scratch_shapes=(), carried=None,
           aliases=None):
    in_specs, out_specs, out_shape, scratch = list(in_specs), list(out_specs), list(out_shape), list(scratch_shapes)
    common = dict(name=name, grid=grid, input_output_aliases=aliases or {}, compiler_params=_cp(sem))
    if carried is None:
        res = pl.pallas_call(body, in_specs=in_specs, out_specs=out_specs, out_shape=out_shape,
                             scratch_shapes=scratch, **common)(*args)
        return list(res), []
    n_in, n_out, n_scr = len(in_specs), len(out_specs), len(scratch)
    ci, co = len(carried.ins), len(carried.outs)

    def wrapped(*refs):
        ins, cins = refs[:n_in], refs[n_in:n_in + ci]
        p = n_in + ci
        outs, couts = refs[p:p + n_out], refs[p + n_out:p + n_out + co]
        p += n_out + co
        scr, csems = refs[p:p + n_scr], refs[p + n_scr:]
        ids = [pl.program_id(a) for a in range(len(grid))]
        first = functools.reduce(jnp.logical_and, [i == 0 for i in ids])
        last = functools.reduce(jnp.logical_and, [i == g - 1 for i, g in zip(ids, grid)])

        @pl.when(first)
        def _():
            carried.start(cins, couts, csems)

        if carried.mid is not None:
            step, steps = 0, 1
            for i, g in zip(ids, grid):
                step, steps = step * g + i, steps * g

            @pl.when(step == min(steps - 1, (steps * carried.mid_percent) // 100))
            def _():
                carried.mid(cins, couts, csems)

        body(*ins, *outs, *scr)

        @pl.when(last)
        def _():
            carried.finish(cins, couts, csems)

    res = pl.pallas_call(wrapped, in_specs=in_specs + [_ANY] * ci, out_specs=out_specs + [_ANY] * co,
                         out_shape=out_shape + carried.outs, scratch_shapes=scratch + carried.sems,
                         **common)(*args, *carried.ins)
    return list(res[:n_out]), list(res[n_out:])


def _peers(x, y, c):
    out = []
    for rel in range(1, N_DEV):
        dx, dy, dc = (rel >> 2) & 1, (rel >> 1) & 1, rel & 1
        out.append((x + dx - 2 * x * dx, y + dy - 2 * y * dy, c + dc - 2 * c * dc))
    return out


def _direct_exchange(srcs, layout, out_shapes, scatter, src_rows=None):
    n = len(srcs)

    def copies(ins, outs, sems):
        send, recv, loc = sems
        x, y, c = _place()
        me = 4 * x + 2 * y + c
        out, arrive, local = [], [], []
        for i in range(n):
            j, off = layout[i]
            first, rows = (0, srcs[i].shape[-2]) if src_rows is None else src_rows[i]

            def piece(k):
                return ins[i].at[k, pl.ds(first, rows)] if scatter else ins[i]

            for r, peer in enumerate(_peers(x, y, c)):
                pidx = 4 * peer[0] + 2 * peer[1] + peer[2]
                kw = dict(send_sem=send.at[7 * i + r], recv_sem=recv.at[7 * i + r], device_id=peer, device_id_type=MESH)
                out.append(pltpu.make_async_remote_copy(
                    src_ref=piece(pidx), dst_ref=outs[j].at[me, pl.ds(off, rows)], **kw))
                arrive.append(pltpu.make_async_remote_copy(
                    src_ref=piece(pidx), dst_ref=outs[j].at[pidx, pl.ds(off, rows)], **kw))
            local.append(pltpu.make_async_copy(piece(me), outs[j].at[me, pl.ds(off, rows)], loc.at[i]))
        return out, arrive, local

    def start(ins, outs, sems):
        out, _, local = copies(ins, outs, sems)
        for cp in local + out:
            cp.start()

    def finish(ins, outs, sems):
        out, arrive, local = copies(ins, outs, sems)
        for cp in arrive:
            cp.wait_recv()
        for cp in out:
            cp.wait_send()
        for cp in local:
            cp.wait()

    return _Carried(srcs, [jax.ShapeDtypeStruct((N_DEV,) + tuple(s), d) for s, d in out_shapes],
                    [pltpu.SemaphoreType.DMA((7 * n,)), pltpu.SemaphoreType.DMA((7 * n,)),
                     pltpu.SemaphoreType.DMA((n,))], start, finish)


def _two_level_gather(shards, mid_percent=MID_STEP_PERCENT):
    n = len(shards)

    def copies(ins, outs, sems):
        send, recv, loc = sems
        x, y, c = _place()
        me, sibling = (x, y, c), (x, y, 1 - c)
        chips = [(1 - x, y), (x, 1 - y), (1 - x, 1 - y)]

        def win(i, place):
            return outs[i].at[4 * place[0] + 2 * place[1] + place[2]]

        def copy(i, k, block, to, src=None):
            return pltpu.make_async_remote_copy(
                src_ref=win(i, block) if src is None else src, dst_ref=win(i, block),
                send_sem=send.at[7 * i + k], recv_sem=recv.at[7 * i + k], device_id=to, device_id_type=MESH)

        own, passed, ici_in, d2d_in, local = [], [], [], [], []
        for i in range(n):
            own += [copy(i, 0, me, sibling, src=ins[i])]
            own += [copy(i, 1 + j, me, (*chip, c), src=ins[i]) for j, chip in enumerate(chips)]
            passed += [copy(i, 4 + j, (*chip, c), sibling) for j, chip in enumerate(chips)]
            ici_in += [copy(i, 1 + j, (*chip, c), me) for j, chip in enumerate(chips)]
            d2d_in += [copy(i, 0, sibling, me)] + [copy(i, 4 + j, (*chip, 1 - c), me) for j, chip in enumerate(chips)]
            local.append(pltpu.make_async_copy(ins[i], win(i, me), loc.at[i]))
        return own, passed, ici_in, d2d_in, local

    def start(ins, outs, sems):
        own, _, _, _, local = copies(ins, outs, sems)
        for cp in local + own:
            cp.start()

    def mid(ins, outs, sems):
        _, passed, ici_in, _, _ = copies(ins, outs, sems)
        for arrived, onward in zip(ici_in, passed):
            arrived.wait_recv()
            onward.start()

    def finish(ins, outs, sems):
        own, passed, _, d2d_in, local = copies(ins, outs, sems)
        for cp in d2d_in:
            cp.wait_recv()
        for cp in own + passed:
            cp.wait_send()
        for cp in local:
            cp.wait()

    return _Carried(shards, [jax.ShapeDtypeStruct((N_DEV,) + tuple(s.shape), s.dtype) for s in shards],
                    [pltpu.SemaphoreType.DMA((7 * n,)), pltpu.SemaphoreType.DMA((7 * n,)),
                     pltpu.SemaphoreType.DMA((n,))], start, finish, mid, mid_percent)


def _run_exchange(carried, name):
    ci = len(carried.ins)

    def body(*refs):
        ins, outs, sems = refs[:ci], refs[ci:ci + len(carried.outs)], refs[ci + len(carried.outs):]
        carried.start(ins, outs, sems)
        if carried.mid is not None:
            carried.mid(ins, outs, sems)
        carried.finish(ins, outs, sems)

    return list(pl.pallas_call(body, name=name, in_specs=[_ANY] * ci, out_specs=[_ANY] * len(carried.outs),
                               out_shape=carried.outs, scratch_shapes=carried.sems)(*carried.ins))


def _dg(a, b, dn):
    return lax.dot_general(a.astype(BF16), b.astype(BF16), dn, preferred_element_type=F32)


@jax.custom_vjp
def mm_nn(a, b):
    return _dg(a, b, _NN)


@jax.custom_vjp
def mm_nt(a, b):
    return _dg(a, b, _NT)


@jax.custom_vjp
def mm_tn(a, b):
    return _dg(a, b, _TN)


mm_nn.defvjp(lambda a, b: (_dg(a, b, _NN), (a, b)), lambda r, ct: (mm_nt(ct, r[1]), mm_tn(r[0], ct)))
mm_nt.defvjp(lambda a, b: (_dg(a, b, _NT), (a, b)), lambda r, ct: (mm_nn(ct, r[1]), mm_tn(ct, r[0])))
mm_tn.defvjp(lambda a, b: (_dg(a, b, _TN), (a, b)), lambda r, ct: (mm_nt(r[1], ct), mm_nn(r[0], ct)))


def _split3(x):
    p1 = x.astype(BF16)
    r1 = x - p1.astype(F32)
    p2 = r1.astype(BF16)
    r2 = r1 - p2.astype(F32)
    return p1, p2, r2.astype(BF16)


def _exact01(x, c, dn, const_left):
    acc = None
    for p in reversed(_split3(x)):
        t = (lax.dot_general(c, p, dn, preferred_element_type=F32) if const_left
             else lax.dot_general(p, c, dn, preferred_element_type=F32))
        acc = t if acc is None else acc + t
    return acc


def _make_cmm(dn, const_left, bwd_name):
    @jax.custom_vjp
    def f(x, c):
        return _exact01(x, c, dn, const_left)

    def fwd(x, c):
        return _exact01(x, c, dn, const_left), c

    def bwd(c, ct):
        return _CMM[bwd_name](ct, c), jnp.zeros_like(c)

    f.defvjp(fwd, bwd)
    return f


_CMM = {}
_CMM["xc"] = _make_cmm(_NN, False, "xct")
_CMM["xct"] = _make_cmm(_NT, False, "xc")
_CMM["cx"] = _make_cmm(_NN, True, "ctx")
_CMM["ctx"] = _make_cmm(_TN, True, "cx")


def _sigmoid(x):
    return 0.5 * jnp.tanh(0.5 * x) + 0.5


@jax.custom_vjp
def _silu(x):
    return x * _sigmoid(x)


def _silu_fwd(x):
    return _silu(x), x


def _silu_bwd(x, ct):
    s = _sigmoid(x)
    return (ct * (s * (1.0 + x * (1.0 - s))),)


_silu.defvjp(_silu_fwd, _silu_bwd)


def _log1p_pos(e):
    u = 1.0 + e
    d = u - 1.0
    return jnp.where(d == 0.0, e, jnp.log(u) * (e / jnp.where(d == 0.0, 1.0, d)))


@jax.custom_vjp
def _softplus(x):
    return jnp.maximum(x, 0.0) + _log1p_pos(jnp.exp(-jnp.abs(x)))


def _softplus_fwd(x):
    return _softplus(x), x


def _softplus_bwd(x, ct):
    return (ct * _sigmoid(x),)


_softplus.defvjp(_softplus_fwd, _softplus_bwd)


CONV_HALO = 8


def _make_shift(j):
    @jax.custom_vjp
    def f(ext):
        return pltpu.roll(ext, j, 0)[CONV_HALO:, :]

    def fwd(ext):
        return f(ext), None

    def bwd(_, ct):
        pad = jnp.concatenate([jnp.zeros((CONV_HALO, ct.shape[1]), ct.dtype), ct], axis=0)
        return (pltpu.roll(pad, CONV_HALO + CHUNK - j, 0),)

    f.defvjp(fwd, bwd)
    return f


_SHIFT = {j: _make_shift(j) for j in (1, 2, 3)}


@jax.custom_vjp
def _swap_halves(x):
    return pltpu.roll(x, HEAD_DIM, 1)


_swap_halves.defvjp(lambda x: (_swap_halves(x), None), lambda _, ct: (pltpu.roll(ct, HEAD_DIM, 1),))


def _rms_fwd(x, g):
    r = lax.rsqrt(jnp.mean(x * x, axis=-1, keepdims=True) + RMS_EPS)
    n = x * r
    return n * g, n, r


def _rms_bwd(dy, n, r, g):
    dn = dy * g
    dx = r * (dn - n * jnp.mean(dn * n, axis=-1, keepdims=True))
    dg = jnp.sum(dy * n, axis=0, keepdims=True)
    return dx, dg


def _one(cond):
    return jnp.where(cond, 1.0, 0.0)


def _pool_tile(xe, g, ws, b, scale, tile, tt):
    r = lax.rsqrt(jnp.mean(xe * xe, axis=-1, keepdims=True) + RMS_EPS)
    hn = xe * r * g
    row_e = lax.broadcasted_iota(jnp.int32, (tt + POOL_HALO, POOL_GROUP), 0)
    keep = _one(jnp.logical_or(row_e >= POOL_HALO, tile > 0))
    rr = lax.broadcasted_iota(jnp.int32, (tt, tt + POOL_HALO), 0)
    qq = lax.broadcasted_iota(jnp.int32, (tt, tt + POOL_HALO), 1)
    dd = qq - rr
    tpos = tile * tt + lax.broadcasted_iota(jnp.int32, (tt, POOL_GROUP), 0)
    outs = []
    for gi, w in enumerate(POOL_WINDOWS):
        hg = hn[:, gi * POOL_GROUP:(gi + 1) * POOL_GROUP] * keep
        band = _one(jnp.logical_and(dd >= POOL_HALO - w + 1, dd <= POOL_HALO)).astype(BF16)
        cnt = jnp.minimum(tpos + 1, w).astype(F32)
        pooled = _CMM["cx"](hg, band) / cnt
        mixed = pooled - hg[POOL_HALO:, :]
        outs.append(mm_nn(mixed, ws[gi]))
    out = (jnp.concatenate(outs, axis=1) + b) * scale
    return xe[POOL_HALO:, :] + out


def _pool_specs(tt, nt, rev):
    per = tt // POOL_HALO
    t_of = (lambda i: nt - 1 - i) if rev else (lambda i: i)
    main = pl.BlockSpec((tt, D_MODEL), lambda i: (t_of(i), 0))
    halo = pl.BlockSpec((POOL_HALO, D_MODEL), lambda i: (jnp.maximum(t_of(i) * per - 1, 0), 0))
    vec = pl.BlockSpec((1, D_MODEL), lambda i: (0, 0))
    wsp = pl.BlockSpec((N_DEV, 4 * POOL_SHARD, POOL_GROUP), lambda i: (0, 0, 0))
    return main, halo, vec, wsp


def _pool_weights(w_ref):
    return tuple(
        jnp.concatenate([w_ref[k, gi * POOL_SHARD:(gi + 1) * POOL_SHARD, :] for k in range(N_DEV)], axis=0).astype(F32)
        for gi in range(4))


def _pool_fwd(x, g, w, b, scale, carried=None):
    t = x.shape[0]
    tt = min(t, 256)
    nt = t // tt
    main, halo, vec, wsp = _pool_specs(tt, nt, False)

    def body(xm_ref, xh_ref, g_ref, w_ref, b_ref, s_ref, o_ref):
        i = pl.program_id(0)
        xe = jnp.concatenate([xh_ref[...], xm_ref[...]], axis=0)
        o_ref[...] = _pool_tile(xe, g_ref[...], _pool_weights(w_ref), b_ref[...], s_ref[...], i, tt)

    return _pcall(
        body, name="pool_fwd", grid=(nt,),
        in_specs=[main, halo, vec, wsp, vec, vec], out_specs=[main],
        out_shape=[jax.ShapeDtypeStruct((t, D_MODEL), F32)],
        sem=("arbitrary",), args=(x, x, g, w, b, scale), carried=carried)


def _pool_bwd(x, dh, g, w, b, scale, carried=None):
    t = x.shape[0]
    tt = min(t, 256)
    nt = t // tt
    main, halo, vec, wsp = _pool_specs(tt, nt, True)

    def body(xm_ref, xh_ref, dh_ref, g_ref, w_ref, b_ref, s_ref,
             dx_ref, dw_ref, db_ref, ds_ref, dg_ref, carry, dw_acc):
        i = pl.program_id(0)
        tile = nt - 1 - i

        @pl.when(i == 0)
        def _():
            carry[...] = jnp.zeros_like(carry)
            dw_acc[...] = jnp.zeros_like(dw_acc)
            db_ref[...] = jnp.zeros_like(db_ref)
            ds_ref[...] = jnp.zeros_like(ds_ref)
            dg_ref[...] = jnp.zeros_like(dg_ref)

        xe = jnp.concatenate([xh_ref[...], xm_ref[...]], axis=0)
        _, vjp = jax.vjp(lambda a, gg, ww, bb, ss: _pool_tile(a, gg, ww, bb, ss, tile, tt),
                         xe, g_ref[...], _pool_weights(w_ref), b_ref[...], s_ref[...])
        dxe, dgv, dws, dbv, dsv = vjp(dh_ref[...])
        dx_ref[...] = dxe[POOL_HALO:, :]
        dx_ref[tt - POOL_HALO:tt, :] += carry[...]
        carry[...] = dxe[:POOL_HALO, :]
        for gi in range(4):
            dw_acc[gi] += dws[gi]
        db_ref[...] += dbv
        ds_ref[...] += dsv
        dg_ref[...] += dgv

        @pl.when(i == nt - 1)
        def _():
            for k in range(N_DEV):
                for gi in range(4):
                    dw_ref[k, gi * POOL_SHARD:(gi + 1) * POOL_SHARD, :] = dw_acc[gi, k * POOL_SHARD:(k + 1) * POOL_SHARD, :]

    return _pcall(
        body, name="pool_bwd", grid=(nt,),
        in_specs=[main, halo, main, vec, wsp, vec, vec],
        out_specs=[main, wsp, vec, vec, vec],
        out_shape=[jax.ShapeDtypeStruct((t, D_MODEL), F32),
                   jax.ShapeDtypeStruct((N_DEV, 4 * POOL_SHARD, POOL_GROUP), F32),
                   jax.ShapeDtypeStruct((1, D_MODEL), F32),
                   jax.ShapeDtypeStruct((1, D_MODEL), F32),
                   jax.ShapeDtypeStruct((1, D_MODEL), F32)],
        scratch_shapes=[pltpu.VMEM((POOL_HALO, D_MODEL), F32), pltpu.VMEM((4, POOL_GROUP, POOL_GROUP), F32)],
        sem=("arbitrary",), args=(x, x, dh, g, w, b, scale), carried=carried)


def _mlp_weight_specs():
    fb = D_FF // N_DEV
    return (pl.BlockSpec((None, D_MODEL, fb), lambda i, k: (k, 0, 0)),
            pl.BlockSpec((None, fb, D_MODEL), lambda i, k: (k, 0, 0)))


def _mlp_fwd(h, g, w1g, w2g, name, carried=None):
    t = h.shape[0]
    tt = min(t, MATMUL_TOKENS)
    nk, fb = N_DEV, D_FF // N_DEV
    w1_spec, w2_spec = _mlp_weight_specs()

    def body(h_ref, g_ref, w1_ref, w2_ref, o_ref, u_ref, hm_ref, hm_s, acc_s):
        k = pl.program_id(1)

        @pl.when(k == 0)
        def _():
            xv = h_ref[...]
            y, _, _ = _rms_fwd(xv, g_ref[...])
            hb = y.astype(BF16)
            hm_s[...] = hb
            hm_ref[...] = hb
            acc_s[...] = xv

        a = jnp.dot(hm_s[...], w1_ref[...], preferred_element_type=F32)
        u = jnp.maximum(a, 0.0)
        u_ref[...] = u.astype(BF16)
        acc_s[...] += jnp.dot((u * u).astype(BF16), w2_ref[...], preferred_element_type=F32)

        @pl.when(k == nk - 1)
        def _():
            o_ref[...] = acc_s[...]

    return _pcall(
        body, name=name, grid=(t // tt, nk),
        in_specs=[pl.BlockSpec((tt, D_MODEL), lambda i, k: (i, 0)),
                  pl.BlockSpec((1, D_MODEL), lambda i, k: (0, 0)),
                  w1_spec, w2_spec],
        out_specs=[pl.BlockSpec((tt, D_MODEL), lambda i, k: (i, 0)),
                   pl.BlockSpec((tt, fb), lambda i, k: (i, k)),
                   pl.BlockSpec((tt, D_MODEL), lambda i, k: (i, 0))],
        out_shape=[jax.ShapeDtypeStruct((t, D_MODEL), F32),
                   jax.ShapeDtypeStruct((t, nk * fb), BF16),
                   jax.ShapeDtypeStruct((t, D_MODEL), BF16)],
        scratch_shapes=[pltpu.VMEM((tt, D_MODEL), BF16), pltpu.VMEM((tt, D_MODEL), F32)],
        sem=("arbitrary", "arbitrary"), args=(h, g, w1g, w2g), carried=carried)


def _mlp_bwd(dh, dhb, h, g, u, w1g, w2g, name, carried=None):
    t = h.shape[0]
    tt = min(t, MATMUL_TOKENS)
    nk, fb = N_DEV, D_FF // N_DEV
    w1_spec, w2_spec = _mlp_weight_specs()

    def body(dh_ref, dhb_ref, h_ref, g_ref, u_ref, w1_ref, w2_ref,
             dhin_ref, dhinb_ref, da_ref, dg_ref, acc_s):
        i = pl.program_id(0)
        k = pl.program_id(1)

        @pl.when(jnp.logical_and(i == 0, k == 0))
        def _():
            dg_ref[...] = jnp.zeros_like(dg_ref)

        @pl.when(k == 0)
        def _():
            acc_s[...] = jnp.zeros_like(acc_s)

        dv = lax.dot_general(dhb_ref[...], w2_ref[...], _NT, preferred_element_type=F32)
        dab = (dv * (2.0 * u_ref[...].astype(F32))).astype(BF16)
        da_ref[...] = dab
        acc_s[...] += lax.dot_general(dab, w1_ref[...], _NT, preferred_element_type=F32)

        @pl.when(k == nk - 1)
        def _():
            gv = g_ref[...]
            _, n, r = _rms_fwd(h_ref[...], gv)
            dx, dg = _rms_bwd(acc_s[...], n, r, gv)
            dhin = dh_ref[...] + dx
            dhin_ref[...] = dhin
            dhinb_ref[...] = dhin.astype(BF16)
            dg_ref[...] += dg

    tile = pl.BlockSpec((tt, D_MODEL), lambda i, k: (i, 0))
    return _pcall(
        body, name=name, grid=(t // tt, nk),
        in_specs=[tile, tile, tile, pl.BlockSpec((1, D_MODEL), lambda i, k: (0, 0)),
                  pl.BlockSpec((tt, fb), lambda i, k: (i, k)), w1_spec, w2_spec],
        out_specs=[tile, tile, pl.BlockSpec((tt, fb), lambda i, k: (i, k)),
                   pl.BlockSpec((1, D_MODEL), lambda i, k: (0, 0))],
        out_shape=[jax.ShapeDtypeStruct((t, D_MODEL), F32),
                   jax.ShapeDtypeStruct((t, D_MODEL), BF16),
                   jax.ShapeDtypeStruct((t, nk * fb), BF16),
                   jax.ShapeDtypeStruct((1, D_MODEL), F32)],
        scratch_shapes=[pltpu.VMEM((tt, D_MODEL), F32)],
        sem=("arbitrary", "arbitrary"), args=(dh, dhb, h, g, u, w1g, w2g), carried=carried)


def _matmul_tn(a, b, name, square_a=False, col_blocked=False, carried=None):
    t, k1 = a.shape
    k2 = b.shape[1]
    tt = min(t, TN_TOKENS)
    nt = t // tt
    wc = k2 if k1 * k2 * 4 <= TN_ACC_BYTES else k2 // 2
    nb = wc // COL_BLK

    def body(a_ref, b_ref, o_ref, acc):
        s = pl.program_id(1)

        @pl.when(s == 0)
        def _():
            acc[...] = jnp.zeros_like(acc)

        av = a_ref[...]
        if square_a:
            af = av.astype(F32)
            av = (af * af).astype(BF16)
        acc[...] += lax.dot_general(av, b_ref[...], _TN, preferred_element_type=F32)

        @pl.when(s == nt - 1)
        def _():
            if col_blocked:
                for k in range(nb):
                    o_ref[k] = acc[:, k * COL_BLK:(k + 1) * COL_BLK].astype(o_ref.dtype)
            else:
                o_ref[...] = acc[...].astype(o_ref.dtype)

    if col_blocked:
        out_shape = jax.ShapeDtypeStruct((k2 // COL_BLK, k1, COL_BLK), BF16)
        out_spec = pl.BlockSpec((nb, k1, COL_BLK), lambda j, s: (j, 0, 0))
    else:
        out_shape = jax.ShapeDtypeStruct((k1, k2), BF16)
        out_spec = pl.BlockSpec((k1, wc), lambda j, s: (0, j))
    outs, landed = _pcall(
        body, name=name, grid=(k2 // wc, nt),
        in_specs=[pl.BlockSpec((tt, k1), lambda j, s: (s, 0)),
                  pl.BlockSpec((tt, wc), lambda j, s: (s, j))],
        out_specs=[out_spec], out_shape=[out_shape],
        scratch_shapes=[pltpu.VMEM((k1, wc), F32)],
        sem=("arbitrary", "arbitrary"), args=(a, b), carried=carried)
    return (outs[0], landed) if carried is not None else outs[0]


def _norm_matmul(h, g, w, carried=None):
    t = h.shape[0]
    tt = min(t, MATMUL_TOKENS)
    n = w.shape[1]

    def body(h_ref, g_ref, w_ref, o_ref, hn_ref, hn_s):
        @pl.when(pl.program_id(1) == 0)
        def _():
            y, _, _ = _rms_fwd(h_ref[...], g_ref[...])
            hb = y.astype(BF16)
            hn_s[...] = hb
            hn_ref[...] = hb

        o_ref[...] = jnp.dot(hn_s[...], w_ref[...], preferred_element_type=F32)

    return _pcall(
        body, name="ssm_in_proj", grid=(t // tt, n // PROJ_BLK),
        in_specs=[pl.BlockSpec((tt, D_MODEL), lambda i, j: (i, 0)),
                  pl.BlockSpec((1, D_MODEL), lambda i, j: (0, 0)),
                  pl.BlockSpec((D_MODEL, PROJ_BLK), lambda i, j: (0, j))],
        out_specs=[pl.BlockSpec((tt, PROJ_BLK), lambda i, j: (i, j)),
                   pl.BlockSpec((tt, D_MODEL), lambda i, j: (i, 0))],
        out_shape=[jax.ShapeDtypeStruct((t, n), F32), jax.ShapeDtypeStruct((t, D_MODEL), BF16)],
        scratch_shapes=[pltpu.VMEM((tt, D_MODEL), BF16)],
        sem=("arbitrary", "arbitrary"), args=(h, g, w), carried=carried)


def _in_proj_bwd(dzx, w, h, g, dh_next, carried=None):
    t = h.shape[0]
    tt = min(t, MATMUL_TOKENS)
    n = w.shape[1]
    nj = n // PROJ_BLK

    def body(dz_ref, w_ref, h_ref, g_ref, dn_ref, dh_ref, dhb_ref, dg_ref, acc):
        i = pl.program_id(0)
        j = pl.program_id(1)

        @pl.when(jnp.logical_and(i == 0, j == 0))
        def _():
            dg_ref[...] = jnp.zeros_like(dg_ref)

        @pl.when(j == 0)
        def _():
            acc[...] = jnp.zeros_like(acc)

        acc[...] += lax.dot_general(dz_ref[...], w_ref[...], _NT, preferred_element_type=F32)

        @pl.when(j == nj - 1)
        def _():
            gv = g_ref[...]
            _, nn, r = _rms_fwd(h_ref[...], gv)
            dx, dg = _rms_bwd(acc[...], nn, r, gv)
            dh = dn_ref[...] + dx
            dh_ref[...] = dh
            dhb_ref[...] = dh.astype(BF16)
            dg_ref[...] += dg

    tile = pl.BlockSpec((tt, D_MODEL), lambda i, j: (i, 0))
    return _pcall(
        body, name="ssm_in_proj_bwd", grid=(t // tt, nj),
        in_specs=[pl.BlockSpec((tt, PROJ_BLK), lambda i, j: (i, j)),
                  pl.BlockSpec((D_MODEL, PROJ_BLK), lambda i, j: (0, j)),
                  tile, pl.BlockSpec((1, D_MODEL), lambda i, j: (0, 0)), tile],
        out_specs=[tile, tile, pl.BlockSpec((1, D_MODEL), lambda i, j: (0, 0))],
        out_shape=[jax.ShapeDtypeStruct((t, D_MODEL), F32), jax.ShapeDtypeStruct((t, D_MODEL), BF16),
                   jax.ShapeDtypeStruct((1, D_MODEL), F32)],
        scratch_shapes=[pltpu.VMEM((tt, D_MODEL), F32)],
        sem=("arbitrary", "arbitrary"), args=(dzx, w, h, g, dh_next), carried=carried)


def _ssd_consts():
    lane = lax.broadcasted_iota(jnp.int32, (CHUNK, CHUNK), 1)
    row = lax.broadcasted_iota(jnp.int32, (CHUNK, CHUNK), 0)
    causal = lane <= row
    tri = _one(causal).astype(BF16)
    er = lax.broadcasted_iota(jnp.int32, (CHUNK, GROUP_X), 0)
    ec = lax.broadcasted_iota(jnp.int32, (CHUNK, GROUP_X), 1)
    expand = _one(jnp.right_shift(ec, 6) == er).astype(BF16)
    return dict(causal=causal, tri=tri, expand=expand, lo=lane < HEAD_DIM)


def _conv_silu(cur, prev, w, b):
    ext = jnp.concatenate([prev, cur], axis=0)
    acc = cur * w[3] + b
    for j in (1, 2, 3):
        acc = acc + _SHIFT[j](ext) * w[3 - j]
    return _silu(acc)


def _ssd_chunk(raw, rawp, ht, cw, cb_, dtb, alog, dsk, k):
    act = _conv_silu(raw[:, :GROUP_CONV], rawp[:, :GROUP_CONV], cw, cb_)
    xs = act[:, :GROUP_X]
    bm = act[:, GROUP_X:GROUP_X + D_STATE]
    cm = act[:, GROUP_X + D_STATE:]
    dt = _softplus(raw[:, GROUP_CONV:] + dtb)
    a = -jnp.exp(alog)
    xc = _CMM["xc"]

    def lanes(rowv):
        return jnp.sum(xc(jnp.broadcast_to(rowv, (16, CHUNK)), k["expand"]), axis=0, keepdims=True) * (1.0 / 16.0)

    dt_e = xc(dt, k["expand"])
    adt_e = dt_e * lanes(a)
    acs_e = _CMM["cx"](adt_e, k["tri"])
    tot_e = jnp.sum(adt_e, axis=0, keepdims=True)
    gmat = mm_nt(cm, bm)
    xdt = xs * dt_e
    ys = []
    for j in range(HEADS_PER_GROUP // 2):
        pair = acs_e[:, j * CHUNK:(j + 1) * CHUNK]
        swapped = _swap_halves(pair)
        ms = []
        for cb in (jnp.where(k["lo"], pair, swapped), jnp.where(k["lo"], swapped, pair)):
            seg = cb - cb.T
            ms.append(gmat * jnp.exp(jnp.where(k["causal"], seg, -jnp.inf)))
        xp = xdt[:, j * CHUNK:(j + 1) * CHUNK]
        rhs = jnp.concatenate([jnp.where(k["lo"], xp, 0.0), jnp.where(k["lo"], 0.0, xp)], axis=0)
        ys.append(mm_nn(jnp.concatenate(ms, axis=1), rhs))
    y_diag = jnp.concatenate(ys, axis=1)
    y_off = jnp.exp(acs_e) * mm_nn(cm, ht)
    h_new = jnp.exp(tot_e) * ht + mm_tn(bm, xdt * jnp.exp(tot_e - acs_e))
    return y_diag + y_off + lanes(dsk) * xs, h_new


def _ssd_in_specs(nc, rev):
    c_of = (lambda c: nc - 1 - c) if rev else (lambda c: c)
    per = CHUNK // CONV_HALO
    zx = [pl.BlockSpec((CHUNK, GROUP_COLS), lambda g, c: (c_of(c), g)),
          pl.BlockSpec((CONV_HALO, GROUP_COLS), lambda g, c: (jnp.maximum(c_of(c) * per - 1, 0), g))]
    conv = [pl.BlockSpec((4, GROUP_CONV), lambda g, c: (0, g)), pl.BlockSpec((1, GROUP_CONV), lambda g, c: (0, g))]
    head = [pl.BlockSpec((None, 1, 128), lambda g, c: (g, 0, 0))] * 3
    return zx + conv + head, c_of


def _load_chunk_args(refs, has_prev):
    raw, rawp, cw, cb_, dtb, alog, dsk = refs
    return (raw[...], rawp[...] * has_prev, tuple(cw[pl.ds(i, 1), :] for i in range(4)), cb_[...],
            dtb[...], alog[...], dsk[...])


def _ssd_fwd(zx, conv_w, conv_b, dtb, alog, dsk, carried=None):
    t = zx.shape[0]
    nc = t // CHUNK
    in_specs, _ = _ssd_in_specs(nc, False)

    def body(*refs):
        ins, (y_ref, hs_ref, ht) = refs[:7], refs[7:]
        c = pl.program_id(1)

        @pl.when(c == 0)
        def _():
            ht[...] = jnp.zeros_like(ht)

        a = _load_chunk_args(ins, _one(c > 0))
        h_in = ht[...]
        y, h_new = _ssd_chunk(*a[:2], h_in, *a[2:], _ssd_consts())
        y_ref[...] = y
        hs_ref[...] = h_in
        ht[...] = h_new

    return _pcall(
        body, name="ssd_fwd", grid=(N_GROUPS, nc),
        in_specs=in_specs,
        out_specs=[pl.BlockSpec((CHUNK, GROUP_X), lambda g, c: (c, g)),
                   pl.BlockSpec((None, None, D_STATE, GROUP_X), lambda g, c: (g, c, 0, 0))],
        out_shape=[jax.ShapeDtypeStruct((t, D_INNER), F32),
                   jax.ShapeDtypeStruct((N_GROUPS, nc, D_STATE, GROUP_X), F32)],
        scratch_shapes=[pltpu.VMEM((D_STATE, GROUP_X), F32)],
        sem=("arbitrary", "arbitrary"), args=(zx, zx, conv_w, conv_b, dtb, alog, dsk), carried=carried)


def _ssd_bwd(zx, conv_w, conv_b, dtb, alog, dsk, hs, dy, dzx, carried=None):
    t = zx.shape[0]
    nc = t // CHUNK
    in_specs, c_of = _ssd_in_specs(nc, True)
    n_in = 10

    def body(*refs):
        ins, hs_ref, dy_ref = refs[:7], refs[7], refs[8]
        (draw_ref, dcw, dcb, ddtb, dalog, ddsk, dht, carry) = refs[n_in:]
        cc = pl.program_id(1)
        accs = (dcw, dcb, ddtb, dalog, ddsk)

        @pl.when(cc == 0)
        def _():
            for r in (dht, carry) + accs:
                r[...] = jnp.zeros_like(r)

        has_prev = _one(c_of(cc) > 0)
        a = _load_chunk_args(ins, has_prev)
        k = _ssd_consts()
        fn = lambda *args: _ssd_chunk(*args, k)
        _, vjp = jax.vjp(fn, *a[:2], hs_ref[...], *a[2:])
        graw, grawp, ght, gcw, gcb, gdtb, galog, gdsk = vjp((dy_ref[...], dht[...]))
        tail = jnp.concatenate([jnp.zeros((CHUNK - CONV_HALO, GROUP_COLS), F32), carry[...]], axis=0)
        draw_ref[...] = (graw + tail).astype(BF16)
        carry[...] = grawp * has_prev
        dht[...] = ght
        for i in range(4):
            dcw[pl.ds(i, 1), :] += gcw[i]
        for ref, val in ((dcb, gcb), (ddtb, gdtb), (dalog, galog), (ddsk, gdsk)):
            ref[...] += val

    head_out = pl.BlockSpec((None, 1, 128), lambda g, c: (g, 0, 0))
    sds = jax.ShapeDtypeStruct
    return _pcall(
        body, name="ssd_bwd", grid=(N_GROUPS, nc),
        in_specs=in_specs + [
            pl.BlockSpec((None, None, D_STATE, GROUP_X), lambda g, c: (g, c_of(c), 0, 0)),
            pl.BlockSpec((CHUNK, GROUP_X), lambda g, c: (c_of(c), g)),
            _ANY],
        out_specs=[pl.BlockSpec((CHUNK, GROUP_COLS), lambda g, c: (c_of(c), g)),
                   pl.BlockSpec((4, GROUP_CONV), lambda g, c: (0, g)),
                   pl.BlockSpec((1, GROUP_CONV), lambda g, c: (0, g)),
                   head_out, head_out, head_out],
        out_shape=[sds((t, ZX_COLS), BF16), sds((4, N_GROUPS * GROUP_CONV), F32), sds((1, N_GROUPS * GROUP_CONV), F32),
                   sds((N_GROUPS, 1, 128), F32), sds((N_GROUPS, 1, 128), F32), sds((N_GROUPS, 1, 128), F32)],
        scratch_shapes=[pltpu.VMEM((D_STATE, GROUP_X), F32), pltpu.VMEM((CONV_HALO, GROUP_COLS), F32)],
        sem=("arbitrary", "arbitrary"), args=(zx, zx, conv_w, conv_b, dtb, alog, dsk, hs, dy, dzx),
        aliases={9: 0}, carried=carried)


def _gate_norm(y, zs, ng):
    outs = []
    for k in range(N_GROUPS):
        s = y[:, k * GROUP_X:(k + 1) * GROUP_X] * _silu(zs[k])
        outs.append(s * lax.rsqrt(jnp.mean(s * s, axis=-1, keepdims=True) + RMS_EPS))
    return jnp.concatenate(outs, axis=1) * ng


def _z_specs(tt):
    first = Z_OFF // GROUP_X
    return [pl.BlockSpec((tt, GROUP_X), functools.partial(lambda k, i: (i, first + k), k)) for k in range(N_GROUPS)]


def _ssm_out_fwd(y, zx, ng, w_out, h):
    t = h.shape[0]
    tt = min(t, 256)

    def body(y_ref, z0, z1, z2, z3, ng_ref, w_ref, h_ref, o_ref):
        yn = _gate_norm(y_ref[...], (z0[...], z1[...], z2[...], z3[...]), ng_ref[...])
        o_ref[...] = h_ref[...] + jnp.dot(yn.astype(BF16), w_ref[...], preferred_element_type=F32)

    return pl.pallas_call(
        body, name="ssm_out_fwd", grid=(t // tt,),
        in_specs=[pl.BlockSpec((tt, D_INNER), lambda i: (i, 0))] + _z_specs(tt) + [
            pl.BlockSpec((1, D_INNER), lambda i: (0, 0)),
            pl.BlockSpec((D_INNER, D_MODEL), lambda i: (0, 0)),
            pl.BlockSpec((tt, D_MODEL), lambda i: (i, 0))],
        out_specs=pl.BlockSpec((tt, D_MODEL), lambda i: (i, 0)),
        out_shape=jax.ShapeDtypeStruct((t, D_MODEL), F32),
        compiler_params=_cp(("arbitrary",)),
    )(y, zx, zx, zx, zx, ng, w_out, h)


def _gate_norm_group(y, z, ng):
    s = y * _silu(z)
    return s * lax.rsqrt(jnp.mean(s * s, axis=-1, keepdims=True) + RMS_EPS) * ng


def _ssm_out_bwd(dhb, y, zx, ng, w_out):
    t = dhb.shape[0]
    tt = min(t, 512)
    first = Z_OFF // GROUP_X

    def body(dh_ref, y_ref, z_ref, ng_ref, w_ref, dy_ref, dzx_ref, yn_ref, dng_ref):
        @pl.when(pl.program_id(1) == 0)
        def _():
            dng_ref[...] = jnp.zeros_like(dng_ref)

        dyn = lax.dot_general(dh_ref[...], w_ref[...], _NT, preferred_element_type=F32)
        yn, vjp = jax.vjp(_gate_norm_group, y_ref[...], z_ref[...], ng_ref[...])
        dy, dz, dng = vjp(dyn)
        dy_ref[...] = dy
        dzx_ref[...] = dz.astype(BF16)
        yn_ref[...] = yn.astype(BF16)
        dng_ref[...] += dng

    grp = pl.BlockSpec((tt, GROUP_X), lambda k, i: (i, k))
    zgrp = pl.BlockSpec((tt, GROUP_X), lambda k, i: (i, first + k))
    gain = pl.BlockSpec((1, GROUP_X), lambda k, i: (0, k))
    return pl.pallas_call(
        body, name="ssm_out_bwd", grid=(N_GROUPS, t // tt),
        in_specs=[pl.BlockSpec((tt, D_MODEL), lambda k, i: (i, 0)), grp, zgrp, gain,
                  pl.BlockSpec((GROUP_X, D_MODEL), lambda k, i: (k, 0))],
        out_specs=[grp, zgrp, grp, gain],
        out_shape=[jax.ShapeDtypeStruct((t, D_INNER), F32), jax.ShapeDtypeStruct((t, ZX_COLS), BF16),
                   jax.ShapeDtypeStruct((t, D_INNER), BF16), jax.ShapeDtypeStruct((1, D_INNER), F32)],
        compiler_params=_cp(("arbitrary", "arbitrary")),
    )(dhb, y, zx, ng, w_out)


def _final(h, g, tgt):
    t = h.shape[0]
    tt = min(t, 512)
    nt = t // tt

    def body(h_ref, g_ref, t_ref, dh_ref, dhb_ref, loss_ref, dg_ref, lacc):
        i = pl.program_id(0)

        @pl.when(i == 0)
        def _():
            dg_ref[...] = jnp.zeros_like(dg_ref)
            lacc[...] = jnp.zeros_like(lacc)

        gv = g_ref[...]
        y, n, r = _rms_fwd(h_ref[...], gv)
        err = y - t_ref[...]
        lacc[...] += jnp.sum(err * err, axis=0, keepdims=True)
        dx, dg = _rms_bwd(err * (1.0 / D_MODEL), n, r, gv)
        dh_ref[...] = dx
        dhb_ref[...] = dx.astype(BF16)
        dg_ref[...] += dg

        @pl.when(i == nt - 1)
        def _():
            loss_ref[...] = jnp.zeros_like(loss_ref) + (0.5 / D_MODEL) * jnp.sum(lacc[...])

    tile = pl.BlockSpec((tt, D_MODEL), lambda i: (i, 0))
    vec = pl.BlockSpec((1, D_MODEL), lambda i: (0, 0))
    return pl.pallas_call(
        body, name="final_loss", grid=(nt,),
        in_specs=[tile, vec, tile],
        out_specs=[tile, tile, pl.BlockSpec((1, 128), lambda i: (0, 0)), vec],
        out_shape=[jax.ShapeDtypeStruct((t, D_MODEL), F32), jax.ShapeDtypeStruct((t, D_MODEL), BF16),
                   jax.ShapeDtypeStruct((1, 128), F32), jax.ShapeDtypeStruct((1, D_MODEL), F32)],
        scratch_shapes=[pltpu.VMEM((1, D_MODEL), F32)],
        compiler_params=_cp(("arbitrary",)),
    )(h, g, tgt)


def _adamw_reduced_parts(w, lands, m, v, name):
    rows, cols = w.shape
    br = 256
    nl = lands[0].shape[0]
    starts, blocks = [], []
    for land in lands:
        starts.append(sum(blocks))
        blocks.append(land.shape[1] // br)

    def body(w_ref, *refs):
        l_refs, (m_ref, v_ref, g_ref, d_ref, m2_ref, v2_ref) = refs[:len(lands)], refs[len(lands):]
        i = pl.program_id(0)
        gv = None
        for ref, first in zip(l_refs, starts):
            acc = ref[0].astype(F32)
            for q in range(1, nl):
                acc = acc + ref[q].astype(F32)
            gv = acc if gv is None else jnp.where(i >= first, acc, gv)
        g_ref[...] = gv
        d_ref[...], m2_ref[...], v2_ref[...] = _adamw_math(w_ref[...], gv, m_ref[...], v_ref[...])

    spec = pl.BlockSpec((br, cols), lambda i: (i, 0))
    land_specs = [pl.BlockSpec((nl, br, cols), functools.partial(
        lambda first, nb, i: (0, jnp.clip(i - first, 0, nb - 1), 0), first, nb)) for first, nb in zip(starts, blocks)]
    out = jax.ShapeDtypeStruct((rows, cols), F32)
    return pl.pallas_call(
        body, name=name, grid=(rows // br,),
        in_specs=[spec] + land_specs + [spec, spec], out_specs=[spec] * 4, out_shape=[out] * 4,
        compiler_params=_cp(("arbitrary",)),
    )(w, *lands, m, v)


def _adamw_reduced_layers(w, lands, m, v, name):
    _, rows, cols = w.shape
    br = rows if rows <= 256 else 256
    nb = rows // br
    nl = lands[0].shape[0]

    def body(w_ref, l0_ref, l1_ref, m_ref, v_ref, g_ref, d_ref, m2_ref, v2_ref):
        def total(ref):
            acc = ref[0].astype(F32)
            for q in range(1, nl):
                acc = acc + ref[q].astype(F32)
            return acc

        gv = jnp.where(pl.program_id(0) == 0, total(l0_ref), total(l1_ref))
        g_ref[...] = gv
        d_ref[...], m2_ref[...], v2_ref[...] = _adamw_math(w_ref[...], gv, m_ref[...], v_ref[...])

    spec = pl.BlockSpec((None, br, cols), lambda l, i: (l, i, 0))
    land0 = pl.BlockSpec((nl, br, cols), lambda l, i: (0, jnp.where(l == 0, i, nb - 1), 0))
    land1 = pl.BlockSpec((nl, br, cols), lambda l, i: (0, jnp.where(l == 1, i, 0), 0))
    out = jax.ShapeDtypeStruct(w.shape, F32)
    return pl.pallas_call(
        body, name=name, grid=(2, nb),
        in_specs=[spec, land0, land1, spec, spec], out_specs=[spec] * 4, out_shape=[out] * 4,
        compiler_params=_cp(("arbitrary", "arbitrary")),
    )(w, lands[0], lands[1], m, v)


def _all_reduce_small(sp):
    rows, n = sp.shape

    def body(x_ref, o_ref, land, send_sems, recv_sems):
        x, y, c = _place()
        me = 4 * x + 2 * y + c
        land[me] = x_ref[...]
        cps = []
        for rel in range(1, N_DEV):
            dx, dy, dc = (rel >> 2) & 1, (rel >> 1) & 1, rel & 1
            px = x + dx - 2 * x * dx
            py = y + dy - 2 * y * dy
            pc = c + dc - 2 * c * dc
            peer = 4 * px + 2 * py + pc
            cps.append((pltpu.make_async_remote_copy(
                src_ref=x_ref, dst_ref=land.at[me], send_sem=send_sems.at[rel - 1], recv_sem=recv_sems.at[rel - 1],
                device_id=(px, py, pc), device_id_type=MESH),
                pltpu.make_async_remote_copy(
                src_ref=x_ref, dst_ref=land.at[peer], send_sem=send_sems.at[rel - 1], recv_sem=recv_sems.at[rel - 1],
                device_id=(px, py, pc), device_id_type=MESH)))
        for cp, _ in cps:
            cp.start()
        for _, arr in cps:
            arr.wait_recv()
        for cp, _ in cps:
            cp.wait_send()
        acc = land[0]
        for k in range(1, N_DEV):
            acc = acc + land[k]
        o_ref[...] = acc

    vm = pl.BlockSpec(memory_space=pltpu.VMEM)
    return pl.pallas_call(
        body, name="all_reduce_small",
        out_shape=jax.ShapeDtypeStruct((rows, n), F32),
        in_specs=[vm], out_specs=vm,
        scratch_shapes=[pltpu.VMEM((N_DEV, rows, n), F32),
                        pltpu.SemaphoreType.DMA((N_DEV - 1,)), pltpu.SemaphoreType.DMA((N_DEV - 1,))],
    )(sp)


def _adamw_math(wv, gv, mv, vv):
    m2 = ADAM_B1 * mv + (1.0 - ADAM_B1) * gv
    v2 = ADAM_B2 * vv + (1.0 - ADAM_B2) * (gv * gv)
    m_hat = m2 / (1.0 - ADAM_B1 ** ADAM_STEP)
    v_hat = v2 / (1.0 - ADAM_B2 ** ADAM_STEP)
    return -ADAM_LR * (m_hat / (jnp.sqrt(v_hat) + ADAM_EPS) + ADAM_WD * wv), m2, v2


def _adamw(w, g, m, v, name):
    rows, cols = w.shape
    br = rows if rows <= 256 else 256

    def body(w_ref, g_ref, m_ref, v_ref, d_ref, m2_ref, v2_ref):
        d_ref[...], m2_ref[...], v2_ref[...] = _adamw_math(w_ref[...], g_ref[...], m_ref[...], v_ref[...])

    spec = pl.BlockSpec((br, cols), lambda i: (i, 0))
    out = jax.ShapeDtypeStruct((rows, cols), F32)
    return pl.pallas_call(
        body, name=name, grid=(rows // br,),
        in_specs=[spec] * 4, out_specs=[spec] * 3, out_shape=[out] * 3,
        compiler_params=_cp(("arbitrary",)),
    )(w, g, m, v)


def _adamw_reduced(w, land, m, v, name):
    rows, cols = w.shape
    br = rows if rows <= 256 else 256
    nl = land.shape[0]

    def body(w_ref, l_ref, m_ref, v_ref, g_ref, d_ref, m2_ref, v2_ref):
        gv = l_ref[0].astype(F32)
        for q in range(1, nl):
            gv = gv + l_ref[q].astype(F32)
        g_ref[...] = gv
        d_ref[...], m2_ref[...], v2_ref[...] = _adamw_math(w_ref[...], gv, m_ref[...], v_ref[...])

    spec = pl.BlockSpec((br, cols), lambda i: (i, 0))
    out = jax.ShapeDtypeStruct((rows, cols), F32)
    return pl.pallas_call(
        body, name=name, grid=(rows // br,),
        in_specs=[spec, pl.BlockSpec((nl, br, cols), lambda i: (0, i, 0)), spec, spec],
        out_specs=[spec] * 4, out_shape=[out] * 4,
        compiler_params=_cp(("arbitrary",)),
    )(w, land, m, v)


def _zx_source_col(col):
    blk = jnp.right_shift(col, 7)
    lane = jnp.bitwise_and(col, 127)
    per = GROUP_COLS // 128
    grp = jnp.where(blk >= per, 1, 0) + jnp.where(blk >= 2 * per, 1, 0) + jnp.where(blk >= 3 * per, 1, 0)
    o = blk - per * grp
    x_col = D_INNER + GROUP_X * grp + 128 * o + lane
    b_col = 2 * D_INNER + D_STATE * grp + lane
    c_col = 2 * D_INNER + N_GROUPS * D_STATE + D_STATE * grp + lane
    dt_col = jnp.where(lane < HEADS_PER_GROUP, D_INNER + CONV_DIM + HEADS_PER_GROUP * grp + lane, -1)
    src = jnp.where(o < 4, x_col, jnp.where(o == 4, b_col, jnp.where(o == 5, c_col, dt_col)))
    return jnp.where(col >= Z_OFF, col - Z_OFF, src)


def _zx_source_col_py(col):
    if col >= Z_OFF:
        return col - Z_OFF
    grp, o = divmod(col, GROUP_COLS)
    if o < GROUP_X:
        return D_INNER + GROUP_X * grp + o
    if o < GROUP_X + D_STATE:
        return 2 * D_INNER + D_STATE * grp + (o - GROUP_X)
    if o < GROUP_CONV:
        return 2 * D_INNER + N_GROUPS * D_STATE + D_STATE * grp + (o - GROUP_X - D_STATE)
    h = o - GROUP_CONV
    return D_INNER + CONV_DIM + HEADS_PER_GROUP * grp + h if h < HEADS_PER_GROUP else -1


def _overlap_tables():
    nblk = ZX_COLS // COL_BLK
    src = [_zx_source_col_py(c) for c in range(ZX_COLS)]
    fwd = [sorted({s // W_IN_SHARD for s in src[COL_BLK * j:COL_BLK * (j + 1)] if s >= 0}) for j in range(nblk)]
    dst = {s: c for c, s in enumerate(src) if s >= 0}
    bwd = [sorted({dst[s] // COL_BLK for s in range(W_IN_SHARD * k, W_IN_SHARD * (k + 1))}) for k in range(N_DEV)]

    def flat(rows):
        width = max(len(r) for r in rows)
        idx = [r + [r[-1]] * (width - len(r)) for r in rows]
        val = [[1] * len(r) + [0] * (width - len(r)) for r in rows]
        return (jnp.asarray(sum(idx, []), jnp.int32), jnp.asarray(sum(val, []), jnp.int32), width)

    return flat(fwd), flat(bwd)


def _w_in_to_zx(w_in_g):
    (tab, val, width), _ = _overlap_tables()
    nblk = ZX_COLS // COL_BLK

    def body(tab_ref, val_ref, w_ref, o_ref, acc):
        j = pl.program_id(0)
        s = pl.program_id(1)

        @pl.when(s == 0)
        def _():
            acc[...] = jnp.zeros_like(acc)

        @pl.when(val_ref[j * width + s] == 1)
        def _():
            k = tab_ref[j * width + s]
            col = COL_BLK * j + lax.broadcasted_iota(jnp.int32, (8, COL_BLK), 1)
            src = jnp.broadcast_to(_zx_source_col(col)[0:1, :], (W_IN_SHARD, COL_BLK))
            row = W_IN_SHARD * k + lax.broadcasted_iota(jnp.int32, (W_IN_SHARD, COL_BLK), 0)
            place = _one(src == row).astype(BF16)
            acc[...] += jnp.dot(w_ref[...], place, preferred_element_type=F32)

        @pl.when(s == width - 1)
        def _():
            o_ref[...] = acc[...].astype(BF16)

    return pl.pallas_call(
        body, name="w_in_to_zx",
        grid_spec=pltpu.PrefetchScalarGridSpec(
            num_scalar_prefetch=2, grid=(nblk, width),
            in_specs=[pl.BlockSpec((None, D_MODEL, W_IN_SHARD), lambda j, s, tab, val: (tab[j * width + s], 0, 0))],
            out_specs=pl.BlockSpec((D_MODEL, COL_BLK), lambda j, s, tab, val: (0, j)),
            scratch_shapes=[pltpu.VMEM((D_MODEL, COL_BLK), F32)]),
        out_shape=jax.ShapeDtypeStruct((D_MODEL, ZX_COLS), BF16),
        compiler_params=_cp(("arbitrary", "arbitrary")),
    )(tab, val, w_in_g)


def _zx_to_w_in(d_wzx):
    _, (tab, val, width) = _overlap_tables()

    def body(tab_ref, val_ref, d_ref, o_ref, acc):
        k = pl.program_id(0)
        s = pl.program_id(1)

        @pl.when(s == 0)
        def _():
            acc[...] = jnp.zeros_like(acc)

        @pl.when(val_ref[k * width + s] == 1)
        def _():
            j = tab_ref[k * width + s]
            col = COL_BLK * j + lax.broadcasted_iota(jnp.int32, (COL_BLK, 128), 0)
            src = jnp.broadcast_to(_zx_source_col(col)[:, 0:1], (COL_BLK, W_IN_SHARD))
            row = W_IN_SHARD * k + lax.broadcasted_iota(jnp.int32, (COL_BLK, W_IN_SHARD), 1)
            place = _one(src == row).astype(BF16)
            acc[...] += jnp.dot(d_ref[...], place, preferred_element_type=F32)

        @pl.when(s == width - 1)
        def _():
            o_ref[...] = acc[...].astype(BF16)

    return pl.pallas_call(
        body, name="zx_to_w_in",
        grid_spec=pltpu.PrefetchScalarGridSpec(
            num_scalar_prefetch=2, grid=(N_DEV, width),
            in_specs=[pl.BlockSpec((D_MODEL, COL_BLK), lambda k, s, tab, val: (0, tab[k * width + s]))],
            out_specs=pl.BlockSpec((None, D_MODEL, W_IN_SHARD), lambda k, s, tab, val: (k, 0, 0)),
            scratch_shapes=[pltpu.VMEM((D_MODEL, W_IN_SHARD), F32)]),
        out_shape=jax.ShapeDtypeStruct((N_DEV, D_MODEL, W_IN_SHARD), BF16),
        compiler_params=_cp(("arbitrary", "arbitrary")),
    )(tab, val, d_wzx)


def _group_conv_cols(a):
    rows = a.shape[0]
    x = a[:, :D_INNER].reshape(rows, N_GROUPS, GROUP_X)
    b = a[:, D_INNER:D_INNER + N_GROUPS * D_STATE].reshape(rows, N_GROUPS, D_STATE)
    c = a[:, D_INNER + N_GROUPS * D_STATE:].reshape(rows, N_GROUPS, D_STATE)
    return jnp.concatenate([x, b, c], axis=2).reshape(rows, N_GROUPS * GROUP_CONV)


def _ungroup_conv_cols(a):
    rows = a.shape[0]
    a3 = a.reshape(rows, N_GROUPS, GROUP_CONV)
    return jnp.concatenate([a3[:, :, :GROUP_X].reshape(rows, D_INNER),
                            a3[:, :, GROUP_X:GROUP_X + D_STATE].reshape(rows, N_GROUPS * D_STATE),
                            a3[:, :, GROUP_X + D_STATE:].reshape(rows, N_GROUPS * D_STATE)], axis=1)


def _small_shard(conv_w, conv_b, norm_g):
    ng = jnp.pad(norm_g.reshape(1, -1), ((0, 0), (0, CONV_SHARD - norm_g.shape[-1])))
    return jnp.concatenate([conv_w.reshape(4, CONV_SHARD), conv_b.reshape(1, CONV_SHARD), ng,
                            jnp.zeros((SMALL_ROWS - 6, CONV_SHARD), F32)], axis=0)


def _small_unshard(a):
    return a[0:4].reshape(1, 4, CONV_SHARD), a[4:5], a[5:6, :D_INNER // N_DEV]


def _heads_of(a):
    return a[:, :, :HEADS_PER_GROUP].reshape(1, N_HEADS)


def _head_params(p):
    return jnp.pad(p.reshape(N_GROUPS, 1, HEADS_PER_GROUP), ((0, 0), (0, 0), (0, 128 - HEADS_PER_GROUP)))


def _update(w, land, m, v, name):
    shp = w.shape
    to2 = lambda a: a.reshape(-1, shp[-1])
    return tuple(o.reshape(shp) for o in _adamw_reduced(to2(w), land, to2(m), to2(v), name))


def kernel(x, norm_mix_g, norm_mlp_g, pool_w, pool_b, pool_scale, ssm_w_in, ssm_conv_w, ssm_conv_b, ssm_dt_bias, ssm_a_log, ssm_d, ssm_norm_g, ssm_w_out, mlp_w1, mlp_w2, final_g, loss_target, m_norm_mix_g, m_norm_mlp_g, m_pool_w, m_pool_b, m_pool_scale, m_ssm_w_in, m_ssm_conv_w, m_ssm_conv_b, m_ssm_dt_bias, m_ssm_a_log, m_ssm_d, m_ssm_norm_g, m_ssm_w_out, m_mlp_w1, m_mlp_w2, m_final_g, v_norm_mix_g, v_norm_mlp_g, v_pool_w, v_pool_b, v_pool_scale, v_ssm_w_in, v_ssm_conv_w, v_ssm_conv_b, v_ssm_dt_bias, v_ssm_a_log, v_ssm_d, v_ssm_norm_g, v_ssm_w_out, v_mlp_w1, v_mlp_w2, v_final_g):
    x2 = x[0]
    tgt = loss_target[0]
    gm0, gm1 = norm_mix_g[0:1], norm_mix_g[1:2]
    gl0, gl1 = norm_mlp_g[0:1], norm_mlp_g[1:2]
    gfin = final_g.reshape(1, D_MODEL)

    fb = D_FF // N_DEV

    def bf(a):
        return a.astype(BF16)

    def gather_of(shards):
        return _direct_exchange(shards, [(i, 0) for i in range(len(shards))],
                                [(s.shape, s.dtype) for s in shards], scatter=False)

    def scatter_of(parts, rows=None):
        shapes = [((p.shape[1] if rows is None else rows[1], p.shape[2]), p.dtype) for p in parts]
        return _direct_exchange(parts, [(i, 0) for i in range(len(parts))], shapes, scatter=True,
                                src_rows=None if rows is None else [rows] * len(parts))

    w_pool, small_g = _run_exchange(_two_level_gather(
        [bf(pool_w.reshape(4 * POOL_SHARD, POOL_GROUP)), _small_shard(ssm_conv_w, ssm_conv_b, ssm_norm_g)]),
        "gather_first")
    conv_w = _group_conv_cols(small_g[:, 0:4].transpose(1, 0, 2).reshape(4, CONV_DIM))
    conv_b = _group_conv_cols(small_g[:, 4].reshape(1, CONV_DIM))
    ssm_ng = small_g[:, 5, :D_INNER // N_DEV].reshape(1, D_INNER)
    dtb, alog, dsk = _head_params(ssm_dt_bias), _head_params(ssm_a_log), _head_params(ssm_d)

    (h1,), (w1g0, w2g0) = _pool_fwd(x2, gm0, w_pool, pool_b, pool_scale,
                                    carried=_two_level_gather([bf(mlp_w1[0]), bf(mlp_w2[0])], mid_percent=100))
    (h2, u0, hm0), (w_in_g,) = _mlp_fwd(h1, gl0, w1g0, w2g0, "mlp0_fwd",
                                        carried=_two_level_gather([bf(ssm_w_in[0])]))
    w_zx = _w_in_to_zx(w_in_g)
    (zx, hn1), (w_out_g,) = _norm_matmul(h2, gm1, w_zx, carried=gather_of([bf(ssm_w_out[0])]))
    (y_ssd, states), (w1g1, w2g1) = _ssd_fwd(zx, conv_w, conv_b, dtb, alog, dsk,
                                             carried=_two_level_gather([bf(mlp_w1[1]), bf(mlp_w2[1])]))
    w_out = w_out_g.reshape(D_INNER, D_MODEL)
    h3 = _ssm_out_fwd(y_ssd, zx, ssm_ng, w_out, h2)
    (h4, u1, hm1), _ = _mlp_fwd(h3, gl1, w1g1, w2g1, "mlp1_fwd")
    dh4, dh4b, loss_row, d_gfin = _final(h4, gfin, tgt)

    (dh3, dh3b, da1, d_gl1), _ = _mlp_bwd(dh4, dh4b, h3, gl1, u1, w1g1, w2g1, "mlp1_bwd")
    d_w1_1 = _matmul_tn(hm1, da1, "mlp1_dw1", col_blocked=True)
    d_w2_1 = _matmul_tn(u1, dh4b, "mlp1_dw2", square_a=True).reshape(N_DEV, fb, D_MODEL)
    dy_ssd, dzx, yn, d_ng = _ssm_out_bwd(dh3b, y_ssd, zx, ssm_ng, w_out)
    d_wout = _matmul_tn(yn, dh3b, "ssm_dw_out").reshape(N_DEV, D_INNER // N_DEV, D_MODEL)
    (dzx, d_cw, d_cb, d_dtb, d_alog, d_dsk), (l_w1_1, l_w2_1, l_wout) = _ssd_bwd(
        zx, conv_w, conv_b, dtb, alog, dsk, states, dy_ssd, dzx, carried=scatter_of([d_w1_1, d_w2_1, d_wout]))
    d_w_in = _zx_to_w_in(_matmul_tn(hn1, dzx, "ssm_dw_in"))
    most = 3 * D_MODEL // 4
    (dh2, dh2b, d_gm1), (l_w_in_a,) = _in_proj_bwd(dzx, w_zx, h2, gm1, dh3, carried=scatter_of([d_w_in], (0, most)))
    d_w2_0, (l_w_in_b,) = _matmul_tn(u0, dh2b, "mlp0_dw2", square_a=True,
                                     carried=scatter_of([d_w_in], (most, D_MODEL - most)))
    d_w2_0 = d_w2_0.reshape(N_DEV, fb, D_MODEL)
    (dh1, _, da0, d_gl0), (l_w2_0,) = _mlp_bwd(dh2, dh2b, h1, gl0, u0, w1g0, w2g0, "mlp0_bwd",
                                           carried=scatter_of([d_w2_0]))
    d_w1_0 = _matmul_tn(hm0, da0, "mlp0_dw1", col_blocked=True)
    (dx, d_pool, d_pb, d_ps, d_gm0), (l_w1_0,) = _pool_bwd(x2, dh1, gm0, w_pool, pool_b, pool_scale,
                                                          carried=scatter_of([d_w1_0]))

    d_conv_w = _ungroup_conv_cols(d_cw).reshape(4, N_DEV, CONV_SHARD).transpose(1, 0, 2)
    d_conv_b = _ungroup_conv_cols(d_cb).reshape(N_DEV, 1, CONV_SHARD)
    d_gain = jnp.pad(d_ng.reshape(N_DEV, 1, D_INNER // N_DEV), ((0, 0), (0, 0), (0, CONV_SHARD - D_INNER // N_DEV)))
    d_small = jnp.concatenate([d_conv_w, d_conv_b, d_gain,
                               jnp.zeros((N_DEV, SMALL_ROWS - 6, CONV_SHARD), F32)], axis=1)
    l_pool, l_small = _run_exchange(scatter_of([bf(d_pool), d_small]), "reduce_scatter_tail")

    heads = jnp.concatenate([_heads_of(a) for a in (d_dtb, d_alog, d_dsk)], axis=1)
    sp = jnp.concatenate([d_gm0, d_gm1, d_gl0, d_gl1, d_pb, d_ps, d_gfin,
                          jnp.pad(heads, ((0, 0), (0, D_MODEL - 3 * N_HEADS)))], axis=0)
    sg = _all_reduce_small(sp)

    g_norm_mix = sg[0:2]
    g_norm_mlp = sg[2:4]
    g_pool_b, g_pool_scale = sg[4:5], sg[5:6]
    g_final = sg[6]
    g_dtb, g_alog, g_dsk = sg[7:8, 0:32], sg[7:8, 32:64], sg[7:8, 64:96]

    def rep_pack(nm, nl, pb, ps, fg, db, al, dk):
        hd = jnp.pad(jnp.concatenate([db, al, dk], axis=1), ((0, 0), (0, D_MODEL - 3 * N_HEADS)))
        return jnp.concatenate([nm, nl, pb, ps, fg.reshape(1, D_MODEL), hd], axis=0)

    rep = [rep_pack(*t) for t in (
        (norm_mix_g, norm_mlp_g, pool_b, pool_scale, final_g, ssm_dt_bias, ssm_a_log, ssm_d),
        (g_norm_mix, g_norm_mlp, g_pool_b, g_pool_scale, g_final, g_dtb, g_alog, g_dsk),
        (m_norm_mix_g, m_norm_mlp_g, m_pool_b, m_pool_scale, m_final_g, m_ssm_dt_bias, m_ssm_a_log, m_ssm_d),
        (v_norm_mix_g, v_norm_mlp_g, v_pool_b, v_pool_scale, v_final_g, v_ssm_dt_bias, v_ssm_a_log, v_ssm_d))]
    rep_out = _adamw(*rep, "adamw_replicated")

    def rep_unpack(a):
        return (a[0:2], a[2:4], a[4:5], a[5:6], a[6], a[7:8, 0:32], a[7:8, 32:64], a[7:8, 64:96])

    sm_out = _adamw_reduced(_small_shard(ssm_conv_w, ssm_conv_b, ssm_norm_g), l_small,
                            _small_shard(m_ssm_conv_w, m_ssm_conv_b, m_ssm_norm_g),
                            _small_shard(v_ssm_conv_w, v_ssm_conv_b, v_ssm_norm_g), "adamw_small_shards")

    big = {
        "pool_w": _update(pool_w, l_pool, m_pool_w, v_pool_w, "adamw_pool_w"),
        "ssm_w_in": tuple(o.reshape(ssm_w_in.shape) for o in _adamw_reduced_parts(
            ssm_w_in[0], (l_w_in_a, l_w_in_b), m_ssm_w_in[0], v_ssm_w_in[0], "adamw_w_in")),
        "ssm_w_out": _update(ssm_w_out, l_wout, m_ssm_w_out, v_ssm_w_out, "adamw_w_out"),
        "mlp_w1": _adamw_reduced_layers(mlp_w1, (l_w1_0, l_w1_1), m_mlp_w1, v_mlp_w1, "adamw_w1"),
        "mlp_w2": _adamw_reduced_layers(mlp_w2, (l_w2_0, l_w2_1), m_mlp_w2, v_mlp_w2, "adamw_w2"),
    }
    rep_all = (rep[1],) + tuple(rep_out)

    def ordered(kind):
        nm, nl, pb, ps, fg, db, al, dk = rep_unpack(rep_all[kind])
        cw, cb, ng = _small_unshard(sm_out[kind])
        return [nm, nl, big["pool_w"][kind], pb, ps, big["ssm_w_in"][kind], cw, cb, db, al, dk, ng,
                big["ssm_w_out"][kind], big["mlp_w1"][kind], big["mlp_w2"][kind], fg]

    loss = lax.psum(loss_row[0, 0], ("x", "y", "c"))
    return (loss, dx[None], *ordered(0), *ordered(1), *ordered(2), *ordered(3))
```

```python
import functools

import jax
import jax.numpy as jnp
from jax import lax
from jax.experimental import pallas as pl
from jax.experimental.pallas import tpu as pltpu

F32 = jnp.float32
BF16 = jnp.bfloat16
MESH = pl.DeviceIdType.MESH

D_MODEL = 1024
RMS_EPS = 1e-5
POOL_WINDOWS = (2, 4, 8, 16)
POOL_GROUP = 256
POOL_HALO = 16
POOL_SHARD = POOL_GROUP // 8
D_INNER = 2048
HEAD_DIM = 64
N_HEADS = 32
N_GROUPS = 4
HEADS_PER_GROUP = 8
D_STATE = 128
CHUNK = 128
CONV_DIM = 3072
IN_PROJ_DIM = 5152
D_FF = 4096
N_DEV = 8
GROUP_X = HEADS_PER_GROUP * HEAD_DIM
GROUP_CONV = GROUP_X + 2 * D_STATE
GROUP_COLS = GROUP_CONV + 128
Z_OFF = N_GROUPS * GROUP_COLS
ZX_COLS = Z_OFF + D_INNER
COL_BLK = 512
PROJ_BLK = ZX_COLS // 4
W_IN_SHARD = IN_PROJ_DIM // N_DEV

ADAM_LR = 0.001
ADAM_B1 = 0.9
ADAM_B2 = 0.999
ADAM_EPS = 1e-08
ADAM_WD = 0.01
ADAM_STEP = 10

VMEM_LIMIT_V7X = 56 * 1024 * 1024
MID_STEP_PERCENT = 85
TN_TOKENS = 512
TN_ACC_BYTES = 16 * 1024 * 1024
MATMUL_TOKENS = 1024

CONV_SHARD = CONV_DIM // N_DEV
SMALL_ROWS = 8

_NN = (((1,), (0,)), ((), ()))
_NT = (((1,), (1,)), ((), ()))
_TN = (((0,), (0,)), ((), ()))


def _cp(sem):
    return pltpu.CompilerParams(dimension_semantics=sem, vmem_limit_bytes=VMEM_LIMIT_V7X)


_ANY = pl.BlockSpec(memory_space=pl.ANY)


def _place():
    return lax.axis_index("x"), lax.axis_index("y"), lax.axis_index("c")


class _Carried:
    def __init__(self, ins, outs, sems, start, finish, mid=None, mid_percent=None):
        self.ins, self.outs, self.sems = list(ins), list(outs), list(sems)
        self.start, self.mid, self.finish, self.mid_percent = start, mid, finish, mid_percent


def _pcall(body, *, name, grid, in_specs, out_specs, out_shape, sem, args, scratch_shapes=(), carried=None,
           aliases=None):
    in_specs, out_specs, out_shape, scratch = list(in_specs), list(out_specs), list(out_shape), list(scratch_shapes)
    common = dict(name=name, grid=grid, input_output_aliases=aliases or {}, compiler_params=_cp(sem))
    if carried is None:
        res = pl.pallas_call(body, in_specs=in_specs, out_specs=out_specs, out_shape=out_shape,
                             scratch_shapes=scratch, **common)(*args)
        return list(res), []
    n_in, n_out, n_scr = len(in_specs), len(out_specs), len(scratch)
    ci, co = len(carried.ins), len(carried.outs)

    def wrapped(*refs):
        ins, cins = refs[:n_in], refs[n_in:n_in + ci]
        p = n_in + ci
        outs, couts = refs[p:p + n_out], refs[p + n_out:p + n_out + co]
        p += n_out + co
        scr, csems = refs[p:p + n_scr], refs[p + n_scr:]
        ids = [pl.program_id(a) for a in range(len(grid))]
        first = functools.reduce(jnp.logical_and, [i == 0 for i in ids])
        last = functools.reduce(jnp.logical_and, [i == g - 1 for i, g in zip(ids, grid)])

        @pl.when(first)
        def _():
            carried.start(cins, couts, csems)

        if carried.mid is not None:
            step, steps = 0, 1
            for i, g in zip(ids, grid):
                step, steps = step * g + i, steps * g

            @pl.when(step == min(steps - 1, (steps * carried.mid_percent) // 100))
            def _():
                carried.mid(cins, couts, csems)

        body(*ins, *outs, *scr)

        @pl.when(last)
        def _():
            carried.finish(cins, couts, csems)

    res = pl.pallas_call(wrapped, in_specs=in_specs + [_ANY] * ci, out_specs=out_specs + [_ANY] * co,
                         out_shape=out_shape + carried.outs, scratch_shapes=scratch + carried.sems,
                         **common)(*args, *carried.ins)
    return list(res[:n_out]), list(res[n_out:])


def _peers(x, y, c):
    out = []
    for rel in range(1, N_DEV):
        dx, dy, dc = (rel >> 2) & 1, (rel >> 1) & 1, rel & 1
        out.append((x + dx - 2 * x * dx, y + dy - 2 * y * dy, c + dc - 2 * c * dc))
    return out


def _direct_exchange(srcs, layout, out_shapes, scatter, src_rows=None):
    n = len(srcs)

    def copies(ins, outs, sems):
        send, recv, loc = sems
        x, y, c = _place()
        me = 4 * x + 2 * y + c
        out, arrive, local = [], [], []
        for i in range(n):
            j, off = layout[i]
            first, rows = (0, srcs[i].shape[-2]) if src_rows is None else src_rows[i]

            def piece(k):
                return ins[i].at[k, pl.ds(first, rows)] if scatter else ins[i]

            for r, peer in enumerate(_peers(x, y, c)):
                pidx = 4 * peer[0] + 2 * peer[1] + peer[2]
                kw = dict(send_sem=send.at[7 * i + r], recv_sem=recv.at[7 * i + r], device_id=peer, device_id_type=MESH)
                out.append(pltpu.make_async_remote_copy(
                    src_ref=piece(pidx), dst_ref=outs[j].at[me, pl.ds(off, rows)], **kw))
                arrive.append(pltpu.make_async_remote_copy(
                    src_ref=piece(pidx), dst_ref=outs[j].at[pidx, pl.ds(off, rows)], **kw))
            local.append(pltpu.make_async_copy(piece(me), outs[j].at[me, pl.ds(off, rows)], loc.at[i]))
        return out, arrive, local

    def start(ins, outs, sems):
        out, _, local = copies(ins, outs, sems)
        for cp in local + out:
            cp.start()

    def finish(ins, outs, sems):
        out, arrive, local = copies(ins, outs, sems)
        for cp in arrive:
            cp.wait_recv()
        for cp in out:
            cp.wait_send()
        for cp in local:
            cp.wait()

    return _Carried(srcs, [jax.ShapeDtypeStruct((N_DEV,) + tuple(s), d) for s, d in out_shapes],
                    [pltpu.SemaphoreType.DMA((7 * n,)), pltpu.SemaphoreType.DMA((7 * n,)),
                     pltpu.SemaphoreType.DMA((n,))], start, finish)


def _two_level_gather(shards, mid_percent=MID_STEP_PERCENT):
    n = len(shards)

    def copies(ins, outs, sems):
        send, recv, loc = sems
        x, y, c = _place()
        me, sibling = (x, y, c), (x, y, 1 - c)
        chips = [(1 - x, y), (x, 1 - y), (1 - x, 1 - y)]

        def win(i, place):
            return outs[i].at[4 * place[0] + 2 * place[1] + place[2]]

        def copy(i, k, block, to, src=None):
            return pltpu.make_async_remote_copy(
                src_ref=win(i, block) if src is None else src, dst_ref=win(i, block),
                send_sem=send.at[7 * i + k], recv_sem=recv.at[7 * i + k], device_id=to, device_id_type=MESH)

        own, passed, ici_in, d2d_in, local = [], [], [], [], []
        for i in range(n):
            own += [copy(i, 0, me, sibling, src=ins[i])]
            own += [copy(i, 1 + j, me, (*chip, c), src=ins[i]) for j, chip in enumerate(chips)]
            passed += [copy(i, 4 + j, (*chip, c), sibling) for j, chip in enumerate(chips)]
            ici_in += [copy(i, 1 + j, (*chip, c), me) for j, chip in enumerate(chips)]
            d2d_in += [copy(i, 0, sibling, me)] + [copy(i, 4 + j, (*chip, 1 - c), me) for j, chip in enumerate(chips)]
            local.append(pltpu.make_async_copy(ins[i], win(i, me), loc.at[i]))
        return own, passed, ici_in, d2d_in, local

    def start(ins, outs, sems):
        own, _, _, _, local = copies(ins, outs, sems)
        for cp in local + own:
            cp.start()

    def mid(ins, outs, sems):
        _, passed, ici_in, _, _ = copies(ins, outs, sems)
        for arrived, onward in zip(ici_in, passed):
            arrived.wait_recv()
            onward.start()

    def finish(ins, outs, sems):
        own, passed, _, d2d_in, local = copies(ins, outs, sems)
        for cp in d2d_in:
            cp.wait_recv()
        for cp in own + passed:
            cp.wait_send()
        for cp in local:
            cp.wait()

    return _Carried(shards, [jax.ShapeDtypeStruct((N_DEV,) + tuple(s.shape), s.dtype) for s in shards],
                    [pltpu.SemaphoreType.DMA((7 * n,)), pltpu.SemaphoreType.DMA((7 * n,)),
                     pltpu.SemaphoreType.DMA((n,))], start, finish, mid, mid_percent)


def _run_exchange(carried, name):
    ci = len(carried.ins)

    def body(*refs):
        ins, outs, sems = refs[:ci], refs[ci:ci + len(carried.outs)], refs[ci + len(carried.outs):]
        carried.start(ins, outs, sems)
        if carried.mid is not None:
            carried.mid(ins, outs, sems)
        carried.finish(ins, outs, sems)

    return list(pl.pallas_call(body, name=name, in_specs=[_ANY] * ci, out_specs=[_ANY] * len(carried.outs),
                               out_shape=carried.outs, scratch_shapes=carried.sems)(*carried.ins))


def _dg(a, b, dn):
    return lax.dot_general(a.astype(BF16), b.astype(BF16), dn, preferred_element_type=F32)


@jax.custom_vjp
def mm_nn(a, b):
    return _dg(a, b, _NN)


@jax.custom_vjp
def mm_nt(a, b):
    return _dg(a, b, _NT)


@jax.custom_vjp
def mm_tn(a, b):
    return _dg(a, b, _TN)


mm_nn.defvjp(lambda a, b: (_dg(a, b, _NN), (a, b)), lambda r, ct: (mm_nt(ct, r[1]), mm_tn(r[0], ct)))
mm_nt.defvjp(lambda a, b: (_dg(a, b, _NT), (a, b)), lambda r, ct: (mm_nn(ct, r[1]), mm_tn(ct, r[0])))
mm_tn.defvjp(lambda a, b: (_dg(a, b, _TN), (a, b)), lambda r, ct: (mm_nt(r[1], ct), mm_nn(r[0], ct)))


def _split3(x):
    p1 = x.astype(BF16)
    r1 = x - p1.astype(F32)
    p2 = r1.astype(BF16)
    r2 = r1 - p2.astype(F32)
    return p1, p2, r2.astype(BF16)


def _exact01(x, c, dn, const_left):
    acc = None
    for p in reversed(_split3(x)):
        t = (lax.dot_general(c, p, dn, preferred_element_type=F32) if const_left
             else lax.dot_general(p, c, dn, preferred_element_type=F32))
        acc = t if acc is None else acc + t
    return acc


def _make_cmm(dn, const_left, bwd_name):
    @jax.custom_vjp
    def f(x, c):
        return _exact01(x, c, dn, const_left)

    def fwd(x, c):
        return _exact01(x, c, dn, const_left), c

    def bwd(c, ct):
        return _CMM[bwd_name](ct, c), jnp.zeros_like(c)

    f.defvjp(fwd, bwd)
    return f


_CMM = {}
_CMM["xc"] = _make_cmm(_NN, False, "xct")
_CMM["xct"] = _make_cmm(_NT, False, "xc")
_CMM["cx"] = _make_cmm(_NN, True, "ctx")
_CMM["ctx"] = _make_cmm(_TN, True, "cx")


def _sigmoid(x):
    return 0.5 * jnp.tanh(0.5 * x) + 0.5


@jax.custom_vjp
def _silu(x):
    return x * _sigmoid(x)


def _silu_fwd(x):
    return _silu(x), x


def _silu_bwd(x, ct):
    s = _sigmoid(x)
    return (ct * (s * (1.0 + x * (1.0 - s))),)


_silu.defvjp(_silu_fwd, _silu_bwd)


def _log1p_pos(e):
    u = 1.0 + e
    d = u - 1.0
    return jnp.where(d == 0.0, e, jnp.log(u) * (e / jnp.where(d == 0.0, 1.0, d)))


@jax.custom_vjp
def _softplus(x):
    return jnp.maximum(x, 0.0) + _log1p_pos(jnp.exp(-jnp.abs(x)))


def _softplus_fwd(x):
    return _softplus(x), x


def _softplus_bwd(x, ct):
    return (ct * _sigmoid(x),)


_softplus.defvjp(_softplus_fwd, _softplus_bwd)


CONV_HALO = 8


def _make_shift(j):
    @jax.custom_vjp
    def f(ext):
        return pltpu.roll(ext, j, 0)[CONV_HALO:, :]

    def fwd(ext):
        return f(ext), None

    def bwd(_, ct):
        pad = jnp.concatenate([jnp.zeros((CONV_HALO, ct.shape[1]), ct.dtype), ct], axis=0)
        return (pltpu.roll(pad, CONV_HALO + CHUNK - j, 0),)

    f.defvjp(fwd, bwd)
    return f


_SHIFT = {j: _make_shift(j) for j in (1, 2, 3)}


@jax.custom_vjp
def _swap_halves(x):
    return pltpu.roll(x, HEAD_DIM, 1)


_swap_halves.defvjp(lambda x: (_swap_halves(x), None), lambda _, ct: (pltpu.roll(ct, HEAD_DIM, 1),))


def _rms_fwd(x, g):
    r = lax.rsqrt(jnp.mean(x * x, axis=-1, keepdims=True) + RMS_EPS)
    n = x * r
    return n * g, n, r


def _rms_bwd(dy, n, r, g):
    dn = dy * g
    dx = r * (dn - n * jnp.mean(dn * n, axis=-1, keepdims=True))
    dg = jnp.sum(dy * n, axis=0, keepdims=True)
    return dx, dg


def _one(cond):
    return jnp.where(cond, 1.0, 0.0)


def _pool_tile(xe, g, ws, b, scale, tile, tt):
    r = lax.rsqrt(jnp.mean(xe * xe, axis=-1, keepdims=True) + RMS_EPS)
    hn = xe * r * g
    row_e = lax.broadcasted_iota(jnp.int32, (tt + POOL_HALO, POOL_GROUP), 0)
    keep = _one(jnp.logical_or(row_e >= POOL_HALO, tile > 0))
    rr = lax.broadcasted_iota(jnp.int32, (tt, tt + POOL_HALO), 0)
    qq = lax.broadcasted_iota(jnp.int32, (tt, tt + POOL_HALO), 1)
    dd = qq - rr
    tpos = tile * tt + lax.broadcasted_iota(jnp.int32, (tt, POOL_GROUP), 0)
    outs = []
    for gi, w in enumerate(POOL_WINDOWS):
        hg = hn[:, gi * POOL_GROUP:(gi + 1) * POOL_GROUP] * keep
        band = _one(jnp.logical_and(dd >= POOL_HALO - w + 1, dd <= POOL_HALO)).astype(BF16)
        cnt = jnp.minimum(tpos + 1, w).astype(F32)
        pooled = _CMM["cx"](hg, band) / cnt
        mixed = pooled - hg[POOL_HALO:, :]
        outs.append(mm_nn(mixed, ws[gi]))
    out = (jnp.concatenate(outs, axis=1) + b) * scale
    return xe[POOL_HALO:, :] + out


def _pool_specs(tt, nt, rev):
    per = tt // POOL_HALO
    t_of = (lambda i: nt - 1 - i) if rev else (lambda i: i)
    main = pl.BlockSpec((tt, D_MODEL), lambda i: (t_of(i), 0))
    halo = pl.BlockSpec((POOL_HALO, D_MODEL), lambda i: (jnp.maximum(t_of(i) * per - 1, 0), 0))
    vec = pl.BlockSpec((1, D_MODEL), lambda i: (0, 0))
    wsp = pl.BlockSpec((N_DEV, 4 * POOL_SHARD, POOL_GROUP), lambda i: (0, 0, 0))
    return main, halo, vec, wsp


def _pool_weights(w_ref):
    return tuple(
        jnp.concatenate([w_ref[k, gi * POOL_SHARD:(gi + 1) * POOL_SHARD, :] for k in range(N_DEV)], axis=0).astype(F32)
        for gi in range(4))


def _pool_fwd(x, g, w, b, scale, carried=None):
    t = x.shape[0]
    tt = min(t, 256)
    nt = t // tt
    main, halo, vec, wsp = _pool_specs(tt, nt, False)

    def body(xm_ref, xh_ref, g_ref, w_ref, b_ref, s_ref, o_ref):
        i = pl.program_id(0)
        xe = jnp.concatenate([xh_ref[...], xm_ref[...]], axis=0)
        o_ref[...] = _pool_tile(xe, g_ref[...], _pool_weights(w_ref), b_ref[...], s_ref[...], i, tt)

    return _pcall(
        body, name="pool_fwd", grid=(nt,),
        in_specs=[main, halo, vec, wsp, vec, vec], out_specs=[main],
        out_shape=[jax.ShapeDtypeStruct((t, D_MODEL), F32)],
        sem=("arbitrary",), args=(x, x, g, w, b, scale), carried=carried)


def _pool_bwd(x, dh, g, w, b, scale, carried=None):
    t = x.shape[0]
    tt = min(t, 256)
    nt = t // tt
    main, halo, vec, wsp = _pool_specs(tt, nt, True)

    def body(xm_ref, xh_ref, dh_ref, g_ref, w_ref, b_ref, s_ref,
             dx_ref, dw_ref, db_ref, ds_ref, dg_ref, carry, dw_acc):
        i = pl.program_id(0)
        tile = nt - 1 - i

        @pl.when(i == 0)
        def _():
            carry[...] = jnp.zeros_like(carry)
            dw_acc[...] = jnp.zeros_like(dw_acc)
            db_ref[...] = jnp.zeros_like(db_ref)
            ds_ref[...] = jnp.zeros_like(ds_ref)
            dg_ref[...] = jnp.zeros_like(dg_ref)

        xe = jnp.concatenate([xh_ref[...], xm_ref[...]], axis=0)
        _, vjp = jax.vjp(lambda a, gg, ww, bb, ss: _pool_tile(a, gg, ww, bb, ss, tile, tt),
                         xe, g_ref[...], _pool_weights(w_ref), b_ref[...], s_ref[...])
        dxe, dgv, dws, dbv, dsv = vjp(dh_ref[...])
        dx_ref[...] = dxe[POOL_HALO:, :]
        dx_ref[tt - POOL_HALO:tt, :] += carry[...]
        carry[...] = dxe[:POOL_HALO, :]
        for gi in range(4):
            dw_acc[gi] += dws[gi]
        db_ref[...] += dbv
        ds_ref[...] += dsv
        dg_ref[...] += dgv

        @pl.when(i == nt - 1)
        def _():
            for k in range(N_DEV):
                for gi in range(4):
                    dw_ref[k, gi * POOL_SHARD:(gi + 1) * POOL_SHARD, :] = dw_acc[gi, k * POOL_SHARD:(k + 1) * POOL_SHARD, :]

    return _pcall(
        body, name="pool_bwd", grid=(nt,),
        in_specs=[main, halo, main, vec, wsp, vec, vec],
        out_specs=[main, wsp, vec, vec, vec],
        out_shape=[jax.ShapeDtypeStruct((t, D_MODEL), F32),
                   jax.ShapeDtypeStruct((N_DEV, 4 * POOL_SHARD, POOL_GROUP), F32),
                   jax.ShapeDtypeStruct((1, D_MODEL), F32),
                   jax.ShapeDtypeStruct((1, D_MODEL), F32),
                   jax.ShapeDtypeStruct((1, D_MODEL), F32)],
        scratch_shapes=[pltpu.VMEM((POOL_HALO, D_MODEL), F32), pltpu.VMEM((4, POOL_GROUP, POOL_GROUP), F32)],
        sem=("arbitrary",), args=(x, x, dh, g, w, b, scale), carried=carried)


def _mlp_weight_specs():
    fb = D_FF // N_DEV
    return (pl.BlockSpec((None, D_MODEL, fb), lambda i, k: (k, 0, 0)),
            pl.BlockSpec((None, fb, D_MODEL), lambda i, k: (k, 0, 0)))


def _mlp_fwd(h, g, w1g, w2g, name, carried=None):
    t = h.shape[0]
    tt = min(t, MATMUL_TOKENS)
    nk, fb = N_DEV, D_FF // N_DEV
    w1_spec, w2_spec = _mlp_weight_specs()

    def body(h_ref, g_ref, w1_ref, w2_ref, o_ref, u_ref, hm_ref, hm_s, acc_s):
        k = pl.program_id(1)

        @pl.when(k == 0)
        def _():
            xv = h_ref[...]
            y, _, _ = _rms_fwd(xv, g_ref[...])
            hb = y.astype(BF16)
            hm_s[...] = hb
            hm_ref[...] = hb
            acc_s[...] = xv

        a = jnp.dot(hm_s[...], w1_ref[...], preferred_element_type=F32)
        u = jnp.maximum(a, 0.0)
        u_ref[...] = u.astype(BF16)
        acc_s[...] += jnp.dot((u * u).astype(BF16), w2_ref[...], preferred_element_type=F32)

        @pl.when(k == nk - 1)
        def _():
            o_ref[...] = acc_s[...]

    return _pcall(
        body, name=name, grid=(t // tt, nk),
        in_specs=[pl.BlockSpec((tt, D_MODEL), lambda i, k: (i, 0)),
                  pl.BlockSpec((1, D_MODEL), lambda i, k: (0, 0)),
                  w1_spec, w2_spec],
        out_specs=[pl.BlockSpec((tt, D_MODEL), lambda i, k: (i, 0)),
                   pl.BlockSpec((tt, fb), lambda i, k: (i, k)),
                   pl.BlockSpec((tt, D_MODEL), lambda i, k: (i, 0))],
        out_shape=[jax.ShapeDtypeStruct((t, D_MODEL), F32),
                   jax.ShapeDtypeStruct((t, nk * fb), BF16),
                   jax.ShapeDtypeStruct((t, D_MODEL), BF16)],
        scratch_shapes=[pltpu.VMEM((tt, D_MODEL), BF16), pltpu.VMEM((tt, D_MODEL), F32)],
        sem=("arbitrary", "arbitrary"), args=(h, g, w1g, w2g), carried=carried)


def _mlp_bwd(dh, dhb, h, g, u, w1g, w2g, name, carried=None):
    t = h.shape[0]
    tt = min(t, MATMUL_TOKENS)
    nk, fb = N_DEV, D_FF // N_DEV
    w1_spec, w2_spec = _mlp_weight_specs()

    def body(dh_ref, dhb_ref, h_ref, g_ref, u_ref, w1_ref, w2_ref,
             dhin_ref, dhinb_ref, da_ref, dg_ref, acc_s):
        i = pl.program_id(0)
        k = pl.program_id(1)

        @pl.when(jnp.logical_and(i == 0, k == 0))
        def _():
            dg_ref[...] = jnp.zeros_like(dg_ref)

        @pl.when(k == 0)
        def _():
            acc_s[...] = jnp.zeros_like(acc_s)

        dv = lax.dot_general(dhb_ref[...], w2_ref[...], _NT, preferred_element_type=F32)
        dab = (dv * (2.0 * u_ref[...].astype(F32))).astype(BF16)
        da_ref[...] = dab
        acc_s[...] += lax.dot_general(dab, w1_ref[...], _NT, preferred_element_type=F32)

        @pl.when(k == nk - 1)
        def _():
            gv = g_ref[...]
            _, n, r = _rms_fwd(h_ref[...], gv)
            dx, dg = _rms_bwd(acc_s[...], n, r, gv)
            dhin = dh_ref[...] + dx
            dhin_ref[...] = dhin
            dhinb_ref[...] = dhin.astype(BF16)
            dg_ref[...] += dg

    tile = pl.BlockSpec((tt, D_MODEL), lambda i, k: (i, 0))
    return _pcall(
        body, name=name, grid=(t // tt, nk),
        in_specs=[tile, tile, tile, pl.BlockSpec((1, D_MODEL), lambda i, k: (0, 0)),
                  pl.BlockSpec((tt, fb), lambda i, k: (i, k)), w1_spec, w2_spec],
        out_specs=[tile, tile, pl.BlockSpec((tt, fb), lambda i, k: (i, k)),
                   pl.BlockSpec((1, D_MODEL), lambda i, k: (0, 0))],
        out_shape=[jax.ShapeDtypeStruct((t, D_MODEL), F32),
                   jax.ShapeDtypeStruct((t, D_MODEL), BF16),
                   jax.ShapeDtypeStruct((t, nk * fb), BF16),
                   jax.ShapeDtypeStruct((1, D_MODEL), F32)],
        scratch_shapes=[pltpu.VMEM((tt, D_MODEL), F32)],
        sem=("arbitrary", "arbitrary"), args=(dh, dhb, h, g, u, w1g, w2g), carried=carried)


def _matmul_tn(a, b, name, square_a=False, col_blocked=False, carried=None):
    t, k1 = a.shape
    k2 = b.shape[1]
    tt = min(t, TN_TOKENS)
    nt = t // tt
    wc = k2 if k1 * k2 * 4 <= TN_ACC_BYTES else k2 // 2
    nb = wc // COL_BLK

    def body(a_ref, b_ref, o_ref, acc):
        s = pl.program_id(1)

        @pl.when(s == 0)
        def _():
            acc[...] = jnp.zeros_like(acc)

        av = a_ref[...]
        if square_a:
            af = av.astype(F32)
            av = (af * af).astype(BF16)
        acc[...] += lax.dot_general(av, b_ref[...], _TN, preferred_element_type=F32)

        @pl.when(s == nt - 1)
        def _():
            if col_blocked:
                for k in range(nb):
                    o_ref[k] = acc[:, k * COL_BLK:(k + 1) * COL_BLK].astype(o_ref.dtype)
            else:
                o_ref[...] = acc[...].astype(o_ref.dtype)

    if col_blocked:
        out_shape = jax.ShapeDtypeStruct((k2 // COL_BLK, k1, COL_BLK), BF16)
        out_spec = pl.BlockSpec((nb, k1, COL_BLK), lambda j, s: (j, 0, 0))
    else:
        out_shape = jax.ShapeDtypeStruct((k1, k2), BF16)
        out_spec = pl.BlockSpec((k1, wc), lambda j, s: (0, j))
    outs, landed = _pcall(
        body, name=name, grid=(k2 // wc, nt),
        in_specs=[pl.BlockSpec((tt, k1), lambda j, s: (s, 0)),
                  pl.BlockSpec((tt, wc), lambda j, s: (s, j))],
        out_specs=[out_spec], out_shape=[out_shape],
        scratch_shapes=[pltpu.VMEM((k1, wc), F32)],
        sem=("arbitrary", "arbitrary"), args=(a, b), carried=carried)
    return (outs[0], landed) if carried is not None else outs[0]


def _norm_matmul(h, g, w, carried=None):
    t = h.shape[0]
    tt = min(t, MATMUL_TOKENS)
    n = w.shape[1]

    def body(h_ref, g_ref, w_ref, o_ref, hn_ref, hn_s):
        @pl.when(pl.program_id(1) == 0)
        def _():
            y, _, _ = _rms_fwd(h_ref[...], g_ref[...])
            hb = y.astype(BF16)
            hn_s[...] = hb
            hn_ref[...] = hb

        o_ref[...] = jnp.dot(hn_s[...], w_ref[...], preferred_element_type=F32)

    return _pcall(
        body, name="ssm_in_proj", grid=(t // tt, n // PROJ_BLK),
        in_specs=[pl.BlockSpec((tt, D_MODEL), lambda i, j: (i, 0)),
                  pl.BlockSpec((1, D_MODEL), lambda i, j: (0, 0)),
                  pl.BlockSpec((D_MODEL, PROJ_BLK), lambda i, j: (0, j))],
        out_specs=[pl.BlockSpec((tt, PROJ_BLK), lambda i, j: (i, j)),
                   pl.BlockSpec((tt, D_MODEL), lambda i, j: (i, 0))],
        out_shape=[jax.ShapeDtypeStruct((t, n), F32), jax.ShapeDtypeStruct((t, D_MODEL), BF16)],
        scratch_shapes=[pltpu.VMEM((tt, D_MODEL), BF16)],
        sem=("arbitrary", "arbitrary"), args=(h, g, w), carried=carried)


def _in_proj_bwd(dzx, w, h, g, dh_next, carried=None):
    t = h.shape[0]
    tt = min(t, MATMUL_TOKENS)
    n = w.shape[1]
    nj = n // PROJ_BLK

    def body(dz_ref, w_ref, h_ref, g_ref, dn_ref, dh_ref, dhb_ref, dg_ref, acc):
        i = pl.program_id(0)
        j = pl.program_id(1)

        @pl.when(jnp.logical_and(i == 0, j == 0))
        def _():
            dg_ref[...] = jnp.zeros_like(dg_ref)

        @pl.when(j == 0)
        def _():
            acc[...] = jnp.zeros_like(acc)

        acc[...] += lax.dot_general(dz_ref[...], w_ref[...], _NT, preferred_element_type=F32)

        @pl.when(j == nj - 1)
        def _():
            gv = g_ref[...]
            _, nn, r = _rms_fwd(h_ref[...], gv)
            dx, dg = _rms_bwd(acc[...], nn, r, gv)
            dh = dn_ref[...] + dx
            dh_ref[...] = dh
            dhb_ref[...] = dh.astype(BF16)
            dg_ref[...] += dg

    tile = pl.BlockSpec((tt, D_MODEL), lambda i, j: (i, 0))
    return _pcall(
        body, name="ssm_in_proj_bwd", grid=(t // tt, nj),
        in_specs=[pl.BlockSpec((tt, PROJ_BLK), lambda i, j: (i, j)),
                  pl.BlockSpec((D_MODEL, PROJ_BLK), lambda i, j: (0, j)),
                  tile, pl.BlockSpec((1, D_MODEL), lambda i, j: (0, 0)), tile],
        out_specs=[tile, tile, pl.BlockSpec((1, D_MODEL), lambda i, j: (0, 0))],
        out_shape=[jax.ShapeDtypeStruct((t, D_MODEL), F32), jax.ShapeDtypeStruct((t, D_MODEL), BF16),
                   jax.ShapeDtypeStruct((1, D_MODEL), F32)],
        scratch_shapes=[pltpu.VMEM((tt, D_MODEL), F32)],
        sem=("arbitrary", "arbitrary"), args=(dzx, w, h, g, dh_next), carried=carried)


def _ssd_consts():
    lane = lax.broadcasted_iota(jnp.int32, (CHUNK, CHUNK), 1)
    row = lax.broadcasted_iota(jnp.int32, (CHUNK, CHUNK), 0)
    causal = lane <= row
    tri = _one(causal).astype(BF16)
    er = lax.broadcasted_iota(jnp.int32, (CHUNK, GROUP_X), 0)
    ec = lax.broadcasted_iota(jnp.int32, (CHUNK, GROUP_X), 1)
    expand = _one(jnp.right_shift(ec, 6) == er).astype(BF16)
    return dict(causal=causal, tri=tri, expand=expand, lo=lane < HEAD_DIM)


def _conv_silu(cur, prev, w, b):
    ext = jnp.concatenate([prev, cur], axis=0)
    acc = cur * w[3] + b
    for j in (1, 2, 3):
        acc = acc + _SHIFT[j](ext) * w[3 - j]
    return _silu(acc)


def _ssd_chunk(raw, rawp, ht, cw, cb_, dtb, alog, dsk, k):
    act = _conv_silu(raw[:, :GROUP_CONV], rawp[:, :GROUP_CONV], cw, cb_)
    xs = act[:, :GROUP_X]
    bm = act[:, GROUP_X:GROUP_X + D_STATE]
    cm = act[:, GROUP_X + D_STATE:]
    dt = _softplus(raw[:, GROUP_CONV:] + dtb)
    a = -jnp.exp(alog)
    xc = _CMM["xc"]

    def lanes(rowv):
        return jnp.sum(xc(jnp.broadcast_to(rowv, (16, CHUNK)), k["expand"]), axis=0, keepdims=True) * (1.0 / 16.0)

    dt_e = xc(dt, k["expand"])
    adt_e = dt_e * lanes(a)
    acs_e = _CMM["cx"](adt_e, k["tri"])
    tot_e = jnp.sum(adt_e, axis=0, keepdims=True)
    gmat = mm_nt(cm, bm)
    xdt = xs * dt_e
    ys = []
    for j in range(HEADS_PER_GROUP // 2):
        pair = acs_e[:, j * CHUNK:(j + 1) * CHUNK]
        swapped = _swap_halves(pair)
        ms = []
        for cb in (jnp.where(k["lo"], pair, swapped), jnp.where(k["lo"], swapped, pair)):
            seg = cb - cb.T
            ms.append(gmat * jnp.exp(jnp.where(k["causal"], seg, -jnp.inf)))
        xp = xdt[:, j * CHUNK:(j + 1) * CHUNK]
        rhs = jnp.concatenate([jnp.where(k["lo"], xp, 0.0), jnp.where(k["lo"], 0.0, xp)], axis=0)
        ys.append(mm_nn(jnp.concatenate(ms, axis=1), rhs))
    y_diag = jnp.concatenate(ys, axis=1)
    y_off = jnp.exp(acs_e) * mm_nn(cm, ht)
    h_new = jnp.exp(tot_e) * ht + mm_tn(bm, xdt * jnp.exp(tot_e - acs_e))
    return y_diag + y_off + lanes(dsk) * xs, h_new


def _ssd_in_specs(nc, rev):
    c_of = (lambda c: nc - 1 - c) if rev else (lambda c: c)
    per = CHUNK // CONV_HALO
    zx = [pl.BlockSpec((CHUNK, GROUP_COLS), lambda g, c: (c_of(c), g)),
          pl.BlockSpec((CONV_HALO, GROUP_COLS), lambda g, c: (jnp.maximum(c_of(c) * per - 1, 0), g))]
    conv = [pl.BlockSpec((4, GROUP_CONV), lambda g, c: (0, g)), pl.BlockSpec((1, GROUP_CONV), lambda g, c: (0, g))]
    head = [pl.BlockSpec((None, 1, 128), lambda g, c: (g, 0, 0))] * 3
    return zx + conv + head, c_of


def _load_chunk_args(refs, has_prev):
    raw, rawp, cw, cb_, dtb, alog, dsk = refs
    return (raw[...], rawp[...] * has_prev, tuple(cw[pl.ds(i, 1), :] for i in range(4)), cb_[...],
            dtb[...], alog[...], dsk[...])


def _ssd_fwd(zx, conv_w, conv_b, dtb, alog, dsk, carried=None):
    t = zx.shape[0]
    nc = t // CHUNK
    in_specs, _ = _ssd_in_specs(nc, False)

    def body(*refs):
        ins, (y_ref, hs_ref, ht) = refs[:7], refs[7:]
        c = pl.program_id(1)

        @pl.when(c == 0)
        def _():
            ht[...] = jnp.zeros_like(ht)

        a = _load_chunk_args(ins, _one(c > 0))
        h_in = ht[...]
        y, h_new = _ssd_chunk(*a[:2], h_in, *a[2:], _ssd_consts())
        y_ref[...] = y
        hs_ref[...] = h_in
        ht[...] = h_new

    return _pcall(
        body, name="ssd_fwd", grid=(N_GROUPS, nc),
        in_specs=in_specs,
        out_specs=[pl.BlockSpec((CHUNK, GROUP_X), lambda g, c: (c, g)),
                   pl.BlockSpec((None, None, D_STATE, GROUP_X), lambda g, c: (g, c, 0, 0))],
        out_shape=[jax.ShapeDtypeStruct((t, D_INNER), F32),
                   jax.ShapeDtypeStruct((N_GROUPS, nc, D_STATE, GROUP_X), F32)],
        scratch_shapes=[pltpu.VMEM((D_STATE, GROUP_X), F32)],
        sem=("arbitrary", "arbitrary"), args=(zx, zx, conv_w, conv_b, dtb, alog, dsk), carried=carried)


def _ssd_bwd(zx, conv_w, conv_b, dtb, alog, dsk, hs, dy, dzx, carried=None):
    t = zx.shape[0]
    nc = t // CHUNK
    in_specs, c_of = _ssd_in_specs(nc, True)
    n_in = 10

    def body(*refs):
        ins, hs_ref, dy_ref = refs[:7], refs[7], refs[8]
        (draw_ref, dcw, dcb, ddtb, dalog, ddsk, dht, carry) = refs[n_in:]
        cc = pl.program_id(1)
        accs = (dcw, dcb, ddtb, dalog, ddsk)

        @pl.when(cc == 0)
        def _():
            for r in (dht, carry) + accs:
                r[...] = jnp.zeros_like(r)

        has_prev = _one(c_of(cc) > 0)
        a = _load_chunk_args(ins, has_prev)
        k = _ssd_consts()
        fn = lambda *args: _ssd_chunk(*args, k)
        _, vjp = jax.vjp(fn, *a[:2], hs_ref[...], *a[2:])
        graw, grawp, ght, gcw, gcb, gdtb, galog, gdsk = vjp((dy_ref[...], dht[...]))
        tail = jnp.concatenate([jnp.zeros((CHUNK - CONV_HALO, GROUP_COLS), F32), carry[...]], axis=0)
        draw_ref[...] = (graw + tail).astype(BF16)
        carry[...] = grawp * has_prev
        dht[...] = ght
        for i in range(4):
            dcw[pl.ds(i, 1), :] += gcw[i]
        for ref, val in ((dcb, gcb), (ddtb, gdtb), (dalog, galog), (ddsk, gdsk)):
            ref[...] += val

    head_out = pl.BlockSpec((None, 1, 128), lambda g, c: (g, 0, 0))
    sds = jax.ShapeDtypeStruct
    return _pcall(
        body, name="ssd_bwd", grid=(N_GROUPS, nc),
        in_specs=in_specs + [
            pl.BlockSpec((None, None, D_STATE, GROUP_X), lambda g, c: (g, c_of(c), 0, 0)),
            pl.BlockSpec((CHUNK, GROUP_X), lambda g, c: (c_of(c), g)),
            _ANY],
        out_specs=[pl.BlockSpec((CHUNK, GROUP_COLS), lambda g, c: (c_of(c), g)),
                   pl.BlockSpec((4, GROUP_CONV), lambda g, c: (0, g)),
                   pl.BlockSpec((1, GROUP_CONV), lambda g, c: (0, g)),
                   head_out, head_out, head_out],
        out_shape=[sds((t, ZX_COLS), BF16), sds((4, N_GROUPS * GROUP_CONV), F32), sds((1, N_GROUPS * GROUP_CONV), F32),
                   sds((N_GROUPS, 1, 128), F32), sds((N_GROUPS, 1, 128), F32), sds((N_GROUPS, 1, 128), F32)],
        scratch_shapes=[pltpu.VMEM((D_STATE, GROUP_X), F32), pltpu.VMEM((CONV_HALO, GROUP_COLS), F32)],
        sem=("arbitrary", "arbitrary"), args=(zx, zx, conv_w, conv_b, dtb, alog, dsk, hs, dy, dzx),
        aliases={9: 0}, carried=carried)


def _gate_norm(y, zs, ng):
    outs = []
    for k in range(N_GROUPS):
        s = y[:, k * GROUP_X:(k + 1) * GROUP_X] * _silu(zs[k])
        outs.append(s * lax.rsqrt(jnp.mean(s * s, axis=-1, keepdims=True) + RMS_EPS))
    return jnp.concatenate(outs, axis=1) * ng


def _z_specs(tt):
    first = Z_OFF // GROUP_X
    return [pl.BlockSpec((tt, GROUP_X), functools.partial(lambda k, i: (i, first + k), k)) for k in range(N_GROUPS)]


def _ssm_out_fwd(y, zx, ng, w_out, h):
    t = h.shape[0]
    tt = min(t, 256)

    def body(y_ref, z0, z1, z2, z3, ng_ref, w_ref, h_ref, o_ref):
        yn = _gate_norm(y_ref[...], (z0[...], z1[...], z2[...], z3[...]), ng_ref[...])
        o_ref[...] = h_ref[...] + jnp.dot(yn.astype(BF16), w_ref[...], preferred_element_type=F32)

    return pl.pallas_call(
        body, name="ssm_out_fwd", grid=(t // tt,),
        in_specs=[pl.BlockSpec((tt, D_INNER), lambda i: (i, 0))] + _z_specs(tt) + [
            pl.BlockSpec((1, D_INNER), lambda i: (0, 0)),
            pl.BlockSpec((D_INNER, D_MODEL), lambda i: (0, 0)),
            pl.BlockSpec((tt, D_MODEL), lambda i: (i, 0))],
        out_specs=pl.BlockSpec((tt, D_MODEL), lambda i: (i, 0)),
        out_shape=jax.ShapeDtypeStruct((t, D_MODEL), F32),
        compiler_params=_cp(("arbitrary",)),
    )(y, zx, zx, zx, zx, ng, w_out, h)


def _gate_norm_group(y, z, ng):
    s = y * _silu(z)
    return s * lax.rsqrt(jnp.mean(s * s, axis=-1, keepdims=True) + RMS_EPS) * ng


def _ssm_out_bwd(dhb, y, zx, ng, w_out):
    t = dhb.shape[0]
    tt = min(t, 512)
    first = Z_OFF // GROUP_X

    def body(dh_ref, y_ref, z_ref, ng_ref, w_ref, dy_ref, dzx_ref, yn_ref, dng_ref):
        @pl.when(pl.program_id(1) == 0)
        def _():
            dng_ref[...] = jnp.zeros_like(dng_ref)

        dyn = lax.dot_general(dh_ref[...], w_ref[...], _NT, preferred_element_type=F32)
        yn, vjp = jax.vjp(_gate_norm_group, y_ref[...], z_ref[...], ng_ref[...])
        dy, dz, dng = vjp(dyn)
        dy_ref[...] = dy
        dzx_ref[...] = dz.astype(BF16)
        yn_ref[...] = yn.astype(BF16)
        dng_ref[...] += dng

    grp = pl.BlockSpec((tt, GROUP_X), lambda k, i: (i, k))
    zgrp = pl.BlockSpec((tt, GROUP_X), lambda k, i: (i, first + k))
    gain = pl.BlockSpec((1, GROUP_X), lambda k, i: (0, k))
    return pl.pallas_call(
        body, name="ssm_out_bwd", grid=(N_GROUPS, t // tt),
        in_specs=[pl.BlockSpec((tt, D_MODEL), lambda k, i: (i, 0)), grp, zgrp, gain,
                  pl.BlockSpec((GROUP_X, D_MODEL), lambda k, i: (k, 0))],
        out_specs=[grp, zgrp, grp, gain],
        out_shape=[jax.ShapeDtypeStruct((t, D_INNER), F32), jax.ShapeDtypeStruct((t, ZX_COLS), BF16),
                   jax.ShapeDtypeStruct((t, D_INNER), BF16), jax.ShapeDtypeStruct((1, D_INNER), F32)],
        compiler_params=_cp(("arbitrary", "arbitrary")),
    )(dhb, y, zx, ng, w_out)


def _final(h, g, tgt):
    t = h.shape[0]
    tt = min(t, 512)
    nt = t // tt

    def body(h_ref, g_ref, t_ref, dh_ref, dhb_ref, loss_ref, dg_ref, lacc):
        i = pl.program_id(0)

        @pl.when(i == 0)
        def _():
            dg_ref[...] = jnp.zeros_like(dg_ref)
            lacc[...] = jnp.zeros_like(lacc)

        gv = g_ref[...]
        y, n, r = _rms_fwd(h_ref[...], gv)
        err = y - t_ref[...]
        lacc[...] += jnp.sum(err * err, axis=0, keepdims=True)
        dx, dg = _rms_bwd(err * (1.0 / D_MODEL), n, r, gv)
        dh_ref[...] = dx
        dhb_ref[...] = dx.astype(BF16)
        dg_ref[...] += dg

        @pl.when(i == nt - 1)
        def _():
            loss_ref[...] = jnp.zeros_like(loss_ref) + (0.5 / D_MODEL) * jnp.sum(lacc[...])

    tile = pl.BlockSpec((tt, D_MODEL), lambda i: (i, 0))
    vec = pl.BlockSpec((1, D_MODEL), lambda i: (0, 0))
    return pl.pallas_call(
        body, name="final_loss", grid=(nt,),
        in_specs=[tile, vec, tile],
        out_specs=[tile, tile, pl.BlockSpec((1, 128), lambda i: (0, 0)), vec],
        out_shape=[jax.ShapeDtypeStruct((t, D_MODEL), F32), jax.ShapeDtypeStruct((t, D_MODEL), BF16),
                   jax.ShapeDtypeStruct((1, 128), F32), jax.ShapeDtypeStruct((1, D_MODEL), F32)],
        scratch_shapes=[pltpu.VMEM((1, D_MODEL), F32)],
        compiler_params=_cp(("arbitrary",)),
    )(h, g, tgt)


def _adamw_reduced_parts(w, lands, m, v, name):
    rows, cols = w.shape
    br = 128
    nl = lands[0].shape[0]
    starts, blocks = [], []
    for land in lands:
        starts.append(sum(blocks))
        blocks.append(land.shape[1] // br)

    def body(w_ref, *refs):
        l_refs, (m_ref, v_ref, g_ref, d_ref, m2_ref, v2_ref) = refs[:len(lands)], refs[len(lands):]
        i = pl.program_id(0)
        gv = None
        for ref, first in zip(l_refs, starts):
            acc = ref[0].astype(F32)
            for q in range(1, nl):
                acc = acc + ref[q].astype(F32)
            gv = acc if gv is None else jnp.where(i >= first, acc, gv)
        g_ref[...] = gv
        d_ref[...], m2_ref[...], v2_ref[...] = _adamw_math(w_ref[...], gv, m_ref[...], v_ref[...])

    spec = pl.BlockSpec((br, cols), lambda i: (i, 0))
    land_specs = [pl.BlockSpec((nl, br, cols), functools.partial(
        lambda first, nb, i: (0, jnp.clip(i - first, 0, nb - 1), 0), first, nb)) for first, nb in zip(starts, blocks)]
    out = jax.ShapeDtypeStruct((rows, cols), F32)
    return pl.pallas_call(
        body, name=name, grid=(rows // br,),
        in_specs=[spec] + land_specs + [spec, spec], out_specs=[spec] * 4, out_shape=[out] * 4,
        compiler_params=_cp(("arbitrary",)),
    )(w, *lands, m, v)


def _adamw_reduced_layers(w, lands, m, v, name):
    _, rows, cols = w.shape
    br = rows if rows <= 256 else 256
    nb = rows // br
    nl = lands[0].shape[0]

    def body(w_ref, l0_ref, l1_ref, m_ref, v_ref, g_ref, d_ref, m2_ref, v2_ref):
        def total(ref):
            acc = ref[0].astype(F32)
            for q in range(1, nl):
                acc = acc + ref[q].astype(F32)
            return acc

        gv = jnp.where(pl.program_id(0) == 0, total(l0_ref), total(l1_ref))
        g_ref[...] = gv
        d_ref[...], m2_ref[...], v2_ref[...] = _adamw_math(w_ref[...], gv, m_ref[...], v_ref[...])

    spec = pl.BlockSpec((None, br, cols), lambda l, i: (l, i, 0))
    land0 = pl.BlockSpec((nl, br, cols), lambda l, i: (0, jnp.where(l == 0, i, nb - 1), 0))
    land1 = pl.BlockSpec((nl, br, cols), lambda l, i: (0, jnp.where(l == 1, i, 0), 0))
    out = jax.ShapeDtypeStruct(w.shape, F32)
    return pl.pallas_call(
        body, name=name, grid=(2, nb),
        in_specs=[spec, land0, land1, spec, spec], out_specs=[spec] * 4, out_shape=[out] * 4,
        compiler_params=_cp(("arbitrary", "arbitrary")),
    )(w, lands[0], lands[1], m, v)


def _all_reduce_small(sp):
    rows, n = sp.shape

    def body(x_ref, o_ref, land, send_sems, recv_sems):
        x, y, c = _place()
        me = 4 * x + 2 * y + c
        land[me] = x_ref[...]
        cps = []
        for rel in range(1, N_DEV):
            dx, dy, dc = (rel >> 2) & 1, (rel >> 1) & 1, rel & 1
            px = x + dx - 2 * x * dx
            py = y + dy - 2 * y * dy
            pc = c + dc - 2 * c * dc
            peer = 4 * px + 2 * py + pc
            cps.append((pltpu.make_async_remote_copy(
                src_ref=x_ref, dst_ref=land.at[me], send_sem=send_sems.at[rel - 1], recv_sem=recv_sems.at[rel - 1],
                device_id=(px, py, pc), device_id_type=MESH),
                pltpu.make_async_remote_copy(
                src_ref=x_ref, dst_ref=land.at[peer], send_sem=send_sems.at[rel - 1], recv_sem=recv_sems.at[rel - 1],
                device_id=(px, py, pc), device_id_type=MESH)))
        for cp, _ in cps:
            cp.start()
        for _, arr in cps:
            arr.wait_recv()
        for cp, _ in cps:
            cp.wait_send()
        acc = land[0]
        for k in range(1, N_DEV):
            acc = acc + land[k]
        o_ref[...] = acc

    vm = pl.BlockSpec(memory_space=pltpu.VMEM)
    return pl.pallas_call(
        body, name="all_reduce_small",
        out_shape=jax.ShapeDtypeStruct((rows, n), F32),
        in_specs=[vm], out_specs=vm,
        scratch_shapes=[pltpu.VMEM((N_DEV, rows, n), F32),
                        pltpu.SemaphoreType.DMA((N_DEV - 1,)), pltpu.SemaphoreType.DMA((N_DEV - 1,))],
    )(sp)


def _adamw_math(wv, gv, mv, vv):
    m2 = ADAM_B1 * mv + (1.0 - ADAM_B1) * gv
    v2 = ADAM_B2 * vv + (1.0 - ADAM_B2) * (gv * gv)
    m_hat = m2 / (1.0 - ADAM_B1 ** ADAM_STEP)
    v_hat = v2 / (1.0 - ADAM_B2 ** ADAM_STEP)
    return -ADAM_LR * (m_hat / (jnp.sqrt(v_hat) + ADAM_EPS) + ADAM_WD * wv), m2, v2


def _adamw(w, g, m, v, name):
    rows, cols = w.shape
    br = rows if rows <= 256 else 256

    def body(w_ref, g_ref, m_ref, v_ref, d_ref, m2_ref, v2_ref):
        d_ref[...], m2_ref[...], v2_ref[...] = _adamw_math(w_ref[...], g_ref[...], m_ref[...], v_ref[...])

    spec = pl.BlockSpec((br, cols), lambda i: (i, 0))
    out = jax.ShapeDtypeStruct((rows, cols), F32)
    return pl.pallas_call(
        body, name=name, grid=(rows // br,),
        in_specs=[spec] * 4, out_specs=[spec] * 3, out_shape=[out] * 3,
        compiler_params=_cp(("arbitrary",)),
    )(w, g, m, v)


def _adamw_reduced(w, land, m, v, name):
    rows, cols = w.shape
    br = rows if rows <= 256 else 256
    nl = land.shape[0]

    def body(w_ref, l_ref, m_ref, v_ref, g_ref, d_ref, m2_ref, v2_ref):
        gv = l_ref[0].astype(F32)
        for q in range(1, nl):
            gv = gv + l_ref[q].astype(F32)
        g_ref[...] = gv
        d_ref[...], m2_ref[...], v2_ref[...] = _adamw_math(w_ref[...], gv, m_ref[...], v_ref[...])

    spec = pl.BlockSpec((br, cols), lambda i: (i, 0))
    out = jax.ShapeDtypeStruct((rows, cols), F32)
    return pl.pallas_call(
        body, name=name, grid=(rows // br,),
        in_specs=[spec, pl.BlockSpec((nl, br, cols), lambda i: (0, i, 0)), spec, spec],
        out_specs=[spec] * 4, out_shape=[out] * 4,
        compiler_params=_cp(("arbitrary",)),
    )(w, land, m, v)


def _zx_source_col(col):
    blk = jnp.right_shift(col, 7)
    lane = jnp.bitwise_and(col, 127)
    per = GROUP_COLS // 128
    grp = jnp.where(blk >= per, 1, 0) + jnp.where(blk >= 2 * per, 1, 0) + jnp.where(blk >= 3 * per, 1, 0)
    o = blk - per * grp
    x_col = D_INNER + GROUP_X * grp + 128 * o + lane
    b_col = 2 * D_INNER + D_STATE * grp + lane
    c_col = 2 * D_INNER + N_GROUPS * D_STATE + D_STATE * grp + lane
    dt_col = jnp.where(lane < HEADS_PER_GROUP, D_INNER + CONV_DIM + HEADS_PER_GROUP * grp + lane, -1)
    src = jnp.where(o < 4, x_col, jnp.where(o == 4, b_col, jnp.where(o == 5, c_col, dt_col)))
    return jnp.where(col >= Z_OFF, col - Z_OFF, src)


def _zx_source_col_py(col):
    if col >= Z_OFF:
        return col - Z_OFF
    grp, o = divmod(col, GROUP_COLS)
    if o < GROUP_X:
        return D_INNER + GROUP_X * grp + o
    if o < GROUP_X + D_STATE:
        return 2 * D_INNER + D_STATE * grp + (o - GROUP_X)
    if o < GROUP_CONV:
        return 2 * D_INNER + N_GROUPS * D_STATE + D_STATE * grp + (o - GROUP_X - D_STATE)
    h = o - GROUP_CONV
    return D_INNER + CONV_DIM + HEADS_PER_GROUP * grp + h if h < HEADS_PER_GROUP else -1


def _overlap_tables():
    nblk = ZX_COLS // COL_BLK
    src = [_zx_source_col_py(c) for c in range(ZX_COLS)]
    fwd = [sorted({s // W_IN_SHARD for s in src[COL_BLK * j:COL_BLK * (j + 1)] if s >= 0}) for j in range(nblk)]
    dst = {s: c for c, s in enumerate(src) if s >= 0}
    bwd = [sorted({dst[s] // COL_BLK for s in range(W_IN_SHARD * k, W_IN_SHARD * (k + 1))}) for k in range(N_DEV)]

    def flat(rows):
        width = max(len(r) for r in rows)
        idx = [r + [r[-1]] * (width - len(r)) for r in rows]
        val = [[1] * len(r) + [0] * (width - len(r)) for r in rows]
        return (jnp.asarray(sum(idx, []), jnp.int32), jnp.asarray(sum(val, []), jnp.int32), width)

    return flat(fwd), flat(bwd)


def _w_in_to_zx(w_in_g):
    (tab, val, width), _ = _overlap_tables()
    nblk = ZX_COLS // COL_BLK

    def body(tab_ref, val_ref, w_ref, o_ref, acc):
        j = pl.program_id(0)
        s = pl.program_id(1)

        @pl.when(s == 0)
        def _():
            acc[...] = jnp.zeros_like(acc)

        @pl.when(val_ref[j * width + s] == 1)
        def _():
            k = tab_ref[j * width + s]
            col = COL_BLK * j + lax.broadcasted_iota(jnp.int32, (8, COL_BLK), 1)
            src = jnp.broadcast_to(_zx_source_col(col)[0:1, :], (W_IN_SHARD, COL_BLK))
            row = W_IN_SHARD * k + lax.broadcasted_iota(jnp.int32, (W_IN_SHARD, COL_BLK), 0)
            place = _one(src == row).astype(BF16)
            acc[...] += jnp.dot(w_ref[...], place, preferred_element_type=F32)

        @pl.when(s == width - 1)
        def _():
            o_ref[...] = acc[...].astype(BF16)

    return pl.pallas_call(
        body, name="w_in_to_zx",
        grid_spec=pltpu.PrefetchScalarGridSpec(
            num_scalar_prefetch=2, grid=(nblk, width),
            in_specs=[pl.BlockSpec((None, D_MODEL, W_IN_SHARD), lambda j, s, tab, val: (tab[j * width + s], 0, 0))],
            out_specs=pl.BlockSpec((D_MODEL, COL_BLK), lambda j, s, tab, val: (0, j)),
            scratch_shapes=[pltpu.VMEM((D_MODEL, COL_BLK), F32)]),
        out_shape=jax.ShapeDtypeStruct((D_MODEL, ZX_COLS), BF16),
        compiler_params=_cp(("arbitrary", "arbitrary")),
    )(tab, val, w_in_g)


def _zx_to_w_in(d_wzx):
    _, (tab, val, width) = _overlap_tables()

    def body(tab_ref, val_ref, d_ref, o_ref, acc):
        k = pl.program_id(0)
        s = pl.program_id(1)

        @pl.when(s == 0)
        def _():
            acc[...] = jnp.zeros_like(acc)

        @pl.when(val_ref[k * width + s] == 1)
        def _():
            j = tab_ref[k * width + s]
            col = COL_BLK * j + lax.broadcasted_iota(jnp.int32, (COL_BLK, 128), 0)
            src = jnp.broadcast_to(_zx_source_col(col)[:, 0:1], (COL_BLK, W_IN_SHARD))
            row = W_IN_SHARD * k + lax.broadcasted_iota(jnp.int32, (COL_BLK, W_IN_SHARD), 1)
            place = _one(src == row).astype(BF16)
            acc[...] += jnp.dot(d_ref[...], place, preferred_element_type=F32)

        @pl.when(s == width - 1)
        def _():
            o_ref[...] = acc[...].astype(BF16)

    return pl.pallas_call(
        body, name="zx_to_w_in",
        grid_spec=pltpu.PrefetchScalarGridSpec(
            num_scalar_prefetch=2, grid=(N_DEV, width),
            in_specs=[pl.BlockSpec((D_MODEL, COL_BLK), lambda k, s, tab, val: (0, tab[k * width + s]))],
            out_specs=pl.BlockSpec((None, D_MODEL, W_IN_SHARD), lambda k, s, tab, val: (k, 0, 0)),
            scratch_shapes=[pltpu.VMEM((D_MODEL, W_IN_SHARD), F32)]),
        out_shape=jax.ShapeDtypeStruct((N_DEV, D_MODEL, W_IN_SHARD), BF16),
        compiler_params=_cp(("arbitrary", "arbitrary")),
    )(tab, val, d_wzx)


def _group_conv_cols(a):
    rows = a.shape[0]
    x = a[:, :D_INNER].reshape(rows, N_GROUPS, GROUP_X)
    b = a[:, D_INNER:D_INNER + N_GROUPS * D_STATE].reshape(rows, N_GROUPS, D_STATE)
    c = a[:, D_INNER + N_GROUPS * D_STATE:].reshape(rows, N_GROUPS, D_STATE)
    return jnp.concatenate([x, b, c], axis=2).reshape(rows, N_GROUPS * GROUP_CONV)


def _ungroup_conv_cols(a):
    rows = a.shape[0]
    a3 = a.reshape(rows, N_GROUPS, GROUP_CONV)
    return jnp.concatenate([a3[:, :, :GROUP_X].reshape(rows, D_INNER),
                            a3[:, :, GROUP_X:GROUP_X + D_STATE].reshape(rows, N_GROUPS * D_STATE),
                            a3[:, :, GROUP_X + D_STATE:].reshape(rows, N_GROUPS * D_STATE)], axis=1)


def _small_shard(conv_w, conv_b, norm_g):
    ng = jnp.pad(norm_g.reshape(1, -1), ((0, 0), (0, CONV_SHARD - norm_g.shape[-1])))
    return jnp.concatenate([conv_w.reshape(4, CONV_SHARD), conv_b.reshape(1, CONV_SHARD), ng,
                            jnp.zeros((SMALL_ROWS - 6, CONV_SHARD), F32)], axis=0)


def _small_unshard(a):
    return a[0:4].reshape(1, 4, CONV_SHARD), a[4:5], a[5:6, :D_INNER // N_DEV]


def _heads_of(a):
    return a[:, :, :HEADS_PER_GROUP].reshape(1, N_HEADS)


def _head_params(p):
    return jnp.pad(p.reshape(N_GROUPS, 1, HEADS_PER_GROUP), ((0, 0), (0, 0), (0, 128 - HEADS_PER_GROUP)))


def _update(w, land, m, v, name):
    shp = w.shape
    to2 = lambda a: a.reshape(-1, shp[-1])
    return tuple(o.reshape(shp) for o in _adamw_reduced(to2(w), land, to2(m), to2(v), name))


def kernel(x, norm_mix_g, norm_mlp_g, pool_w, pool_b, pool_scale, ssm_w_in, ssm_conv_w, ssm_conv_b, ssm_dt_bias, ssm_a_log, ssm_d, ssm_norm_g, ssm_w_out, mlp_w1, mlp_w2, final_g, loss_target, m_norm_mix_g, m_norm_mlp_g, m_pool_w, m_pool_b, m_pool_scale, m_ssm_w_in, m_ssm_conv_w, m_ssm_conv_b, m_ssm_dt_bias, m_ssm_a_log, m_ssm_d, m_ssm_norm_g, m_ssm_w_out, m_mlp_w1, m_mlp_w2, m_final_g, v_norm_mix_g, v_norm_mlp_g, v_pool_w, v_pool_b, v_pool_scale, v_ssm_w_in, v_ssm_conv_w, v_ssm_conv_b, v_ssm_dt_bias, v_ssm_a_log, v_ssm_d, v_ssm_norm_g, v_ssm_w_out, v_mlp_w1, v_mlp_w2, v_final_g):
    x2 = x[0]
    tgt = loss_target[0]
    gm0, gm1 = norm_mix_g[0:1], norm_mix_g[1:2]
    gl0, gl1 = norm_mlp_g[0:1], norm_mlp_g[1:2]
    gfin = final_g.reshape(1, D_MODEL)

    fb = D_FF // N_DEV

    def bf(a):
        return a.astype(BF16)

    def gather_of(shards):
        return _direct_exchange(shards, [(i, 0) for i in range(len(shards))],
                                [(s.shape, s.dtype) for s in shards], scatter=False)

    def scatter_of(parts, rows=None):
        shapes = [((p.shape[1] if rows is None else rows[1], p.shape[2]), p.dtype) for p in parts]
        return _direct_exchange(parts, [(i, 0) for i in range(len(parts))], shapes, scatter=True,
                                src_rows=None if rows is None else [rows] * len(parts))

    w_pool, small_g = _run_exchange(_two_level_gather(
        [bf(pool_w.reshape(4 * POOL_SHARD, POOL_GROUP)), _small_shard(ssm_conv_w, ssm_conv_b, ssm_norm_g)]),
        "gather_first")
    conv_w = _group_conv_cols(small_g[:, 0:4].transpose(1, 0, 2).reshape(4, CONV_DIM))
    conv_b = _group_conv_cols(small_g[:, 4].reshape(1, CONV_DIM))
    ssm_ng = small_g[:, 5, :D_INNER // N_DEV].reshape(1, D_INNER)
    dtb, alog, dsk = _head_params(ssm_dt_bias), _head_params(ssm_a_log), _head_params(ssm_d)

    (h1,), (w1g0, w2g0) = _pool_fwd(x2, gm0, w_pool, pool_b, pool_scale,
                                    carried=_two_level_gather([bf(mlp_w1[0]), bf(mlp_w2[0])], mid_percent=100))
    (h2, u0, hm0), (w_in_g,) = _mlp_fwd(h1, gl0, w1g0, w2g0, "mlp0_fwd",
                                        carried=_two_level_gather([bf(ssm_w_in[0])]))
    w_zx = _w_in_to_zx(w_in_g)
    (zx, hn1), (w_out_g,) = _norm_matmul(h2, gm1, w_zx, carried=gather_of([bf(ssm_w_out[0])]))
    (y_ssd, states), (w1g1, w2g1) = _ssd_fwd(zx, conv_w, conv_b, dtb, alog, dsk,
                                             carried=_two_level_gather([bf(mlp_w1[1]), bf(mlp_w2[1])]))
    w_out = w_out_g.reshape(D_INNER, D_MODEL)
    h3 = _ssm_out_fwd(y_ssd, zx, ssm_ng, w_out, h2)
    (h4, u1, hm1), _ = _mlp_fwd(h3, gl1, w1g1, w2g1, "mlp1_fwd")
    dh4, dh4b, loss_row, d_gfin = _final(h4, gfin, tgt)

    (dh3, dh3b, da1, d_gl1), _ = _mlp_bwd(dh4, dh4b, h3, gl1, u1, w1g1, w2g1, "mlp1_bwd")
    d_w1_1 = _matmul_tn(hm1, da1, "mlp1_dw1", col_blocked=True)
    d_w2_1 = _matmul_tn(u1, dh4b, "mlp1_dw2", square_a=True).reshape(N_DEV, fb, D_MODEL)
    dy_ssd, dzx, yn, d_ng = _ssm_out_bwd(dh3b, y_ssd, zx, ssm_ng, w_out)
    d_wout = _matmul_tn(yn, dh3b, "ssm_dw_out").reshape(N_DEV, D_INNER // N_DEV, D_MODEL)
    (dzx, d_cw, d_cb, d_dtb, d_alog, d_dsk), (l_w1_1, l_w2_1, l_wout) = _ssd_bwd(
        zx, conv_w, conv_b, dtb, alog, dsk, states, dy_ssd, dzx, carried=scatter_of([d_w1_1, d_w2_1, d_wout]))
    d_w_in = _zx_to_w_in(_matmul_tn(hn1, dzx, "ssm_dw_in"))
    most = 5 * D_MODEL // 8
    (dh2, dh2b, d_gm1), (l_w_in_a,) = _in_proj_bwd(dzx, w_zx, h2, gm1, dh3, carried=scatter_of([d_w_in], (0, most)))
    d_w2_0, (l_w_in_b,) = _matmul_tn(u0, dh2b, "mlp0_dw2", square_a=True,
                                     carried=scatter_of([d_w_in], (most, D_MODEL - most)))
    d_w2_0 = d_w2_0.reshape(N_DEV, fb, D_MODEL)
    (dh1, _, da0, d_gl0), (l_w2_0,) = _mlp_bwd(dh2, dh2b, h1, gl0, u0, w1g0, w2g0, "mlp0_bwd",
                                           carried=scatter_of([d_w2_0]))
    d_w1_0 = _matmul_tn(hm0, da0, "mlp0_dw1", col_blocked=True)
    (dx, d_pool, d_pb, d_ps, d_gm0), (l_w1_0,) = _pool_bwd(x2, dh1, gm0, w_pool, pool_b, pool_scale,
                                                          carried=scatter_of([d_w1_0]))

    d_conv_w = _ungroup_conv_cols(d_cw).reshape(4, N_DEV, CONV_SHARD).transpose(1, 0, 2)
    d_conv_b = _ungroup_conv_cols(d_cb).reshape(N_DEV, 1, CONV_SHARD)
    d_gain = jnp.pad(d_ng.reshape(N_DEV, 1, D_INNER // N_DEV), ((0, 0), (0, 0), (0, CONV_SHARD - D_INNER // N_DEV)))
    d_small = jnp.concatenate([d_conv_w, d_conv_b, d_gain,
                               jnp.zeros((N_DEV, SMALL_ROWS - 6, CONV_SHARD), F32)], axis=1)
    l_pool, l_small = _run_exchange(scatter_of([bf(d_pool), d_small]), "reduce_scatter_tail")

    heads = jnp.concatenate([_heads_of(a) for a in (d_dtb, d_alog, d_dsk)], axis=1)
    sp = jnp.concatenate([d_gm0, d_gm1, d_gl0, d_gl1, d_pb, d_ps, d_gfin,
                          jnp.pad(heads, ((0, 0), (0, D_MODEL - 3 * N_HEADS)))], axis=0)
    sg = _all_reduce_small(sp)

    g_norm_mix = sg[0:2]
    g_norm_mlp = sg[2:4]
    g_pool_b, g_pool_scale = sg[4:5], sg[5:6]
    g_final = sg[6]
    g_dtb, g_alog, g_dsk = sg[7:8, 0:32], sg[7:8, 32:64], sg[7:8, 64:96]

    def rep_pack(nm, nl, pb, ps, fg, db, al, dk):
        hd = jnp.pad(jnp.concatenate([db, al, dk], axis=1), ((0, 0), (0, D_MODEL - 3 * N_HEADS)))
        return jnp.concatenate([nm, nl, pb, ps, fg.reshape(1, D_MODEL), hd], axis=0)

    rep = [rep_pack(*t) for t in (
        (norm_mix_g, norm_mlp_g, pool_b, pool_scale, final_g, ssm_dt_bias, ssm_a_log, ssm_d),
        (g_norm_mix, g_norm_mlp, g_pool_b, g_pool_scale, g_final, g_dtb, g_alog, g_dsk),
        (m_norm_mix_g, m_norm_mlp_g, m_pool_b, m_pool_scale, m_final_g, m_ssm_dt_bias, m_ssm_a_log, m_ssm_d),
        (v_norm_mix_g, v_norm_mlp_g, v_pool_b, v_pool_scale, v_final_g, v_ssm_dt_bias, v_ssm_a_log, v_ssm_d))]
    rep_out = _adamw(*rep, "adamw_replicated")

    def rep_unpack(a):
        return (a[0:2], a[2:4], a[4:5], a[5:6], a[6], a[7:8, 0:32], a[7:8, 32:64], a[7:8, 64:96])

    sm_out = _adamw_reduced(_small_shard(ssm_conv_w, ssm_conv_b, ssm_norm_g), l_small,
                            _small_shard(m_ssm_conv_w, m_ssm_conv_b, m_ssm_norm_g),
                            _small_shard(v_ssm_conv_w, v_ssm_conv_b, v_ssm_norm_g), "adamw_small_shards")

    big = {
        "pool_w": _update(pool_w, l_pool, m_pool_w, v_pool_w, "adamw_pool_w"),
        "ssm_w_in": tuple(o.reshape(ssm_w_in.shape) for o in _adamw_reduced_parts(
            ssm_w_in[0], (l_w_in_a, l_w_in_b), m_ssm_w_in[0], v_ssm_w_in[0], "adamw_w_in")),
        "ssm_w_out": _update(ssm_w_out, l_wout, m_ssm_w_out, v_ssm_w_out, "adamw_w_out"),
        "mlp_w1": _adamw_reduced_layers(mlp_w1, (l_w1_0, l_w1_1), m_mlp_w1, v_mlp_w1, "adamw_w1"),
        "mlp_w2": _adamw_reduced_layers(mlp_w2, (l_w2_0, l_w2_1), m_mlp_w2, v_mlp_w2, "adamw_w2"),
    }
    rep_all = (rep[1],) + tuple(rep_out)

    def ordered(kind):
        nm, nl, pb, ps, fg, db, al, dk = rep_unpack(rep_all[kind])
        cw, cb, ng = _small_unshard(sm_out[kind])
        return [nm, nl, big["pool_w"][kind], pb, ps, big["ssm_w_in"][kind], cw, cb, db, al, dk, ng,
                big["ssm_w_out"][kind], big["mlp_w1"][kind], big["mlp_w2"][kind], fg]

    loss = lax.psum(loss_row[0, 0], ("x", "y", "c"))
    return (loss, dx[None], *ordered(0), *ordered(1), *ordered(2), *ordered(3))
```

```python
import functools

import jax
import jax.numpy as jnp
from jax import lax
from jax.experimental import pallas as pl
from jax.experimental.pallas import tpu as pltpu

F32 = jnp.float32
BF16 = jnp.bfloat16
MESH = pl.DeviceIdType.MESH

D_MODEL = 1024
RMS_EPS = 1e-5
POOL_WINDOWS = (2, 4, 8, 16)
POOL_GROUP = 256
POOL_HALO = 16
POOL_SHARD = POOL_GROUP // 8
D_INNER = 2048
HEAD_DIM = 64
N_HEADS = 32
N_GROUPS = 4
HEADS_PER_GROUP = 8
D_STATE = 128
CHUNK = 128
CONV_DIM = 3072
IN_PROJ_DIM = 5152
D_FF = 4096
N_DEV = 8
GROUP_X = HEADS_PER_GROUP * HEAD_DIM
GROUP_CONV = GROUP_X + 2 * D_STATE
GROUP_COLS = GROUP_CONV + 128
Z_OFF = N_GROUPS * GROUP_COLS
ZX_COLS = Z_OFF + D_INNER
COL_BLK = 512
PROJ_BLK = ZX_COLS // 4
W_IN_SHARD = IN_PROJ_DIM // N_DEV

ADAM_LR = 0.001
ADAM_B1 = 0.9
ADAM_B2 = 0.999
ADAM_EPS = 1e-08
ADAM_WD = 0.01
ADAM_STEP = 10

VMEM_LIMIT_V7X = 56 * 1024 * 1024
MID_STEP_PERCENT = 85
TN_TOKENS = 512
TN_ACC_BYTES = 16 * 1024 * 1024
MATMUL_TOKENS = 1024

CONV_SHARD = CONV_DIM // N_DEV
SMALL_ROWS = 8

_NN = (((1,), (0,)), ((), ()))
_NT = (((1,), (1,)), ((), ()))
_TN = (((0,), (0,)), ((), ()))


def _cp(sem):
    return pltpu.CompilerParams(dimension_semantics=sem, vmem_limit_bytes=VMEM_LIMIT_V7X)


_ANY = pl.BlockSpec(memory_space=pl.ANY)


def _place():
    return lax.axis_index("x"), lax.axis_index("y"), lax.axis_index("c")


class _Carried:
    def __init__(self, ins, outs, sems, start, finish, mid=None, mid_percent=None):
        self.ins, self.outs, self.sems = list(ins), list(outs), list(sems)
        self.start, self.mid, self.finish, self.mid_percent = start, mid, finish, mid_percent


def _pcall(body, *, name, grid, in_specs, out_specs, out_shape, sem, args, scratch_shapes=(), carried=None,
           aliases=None):
    in_specs, out_specs, out_shape, scratch = list(in_specs), list(out_specs), list(out_shape), list(scratch_shapes)
    common = dict(name=name, grid=grid, input_output_aliases=aliases or {}, compiler_params=_cp(sem))
    if carried is None:
        res = pl.pallas_call(body, in_specs=in_specs, out_specs=out_specs, out_shape=out_shape,
                             scratch_shapes=scratch, **common)(*args)
        return list(res), []
    n_in, n_out, n_scr = len(in_specs), len(out_specs), len(scratch)
    ci, co = len(carried.ins), len(carried.outs)

    def wrapped(*refs):
        ins, cins = refs[:n_in], refs[n_in:n_in + ci]
        p = n_in + ci
        outs, couts = refs[p:p + n_out], refs[p + n_out:p + n_out + co]
        p += n_out + co
        scr, csems = refs[p:p + n_scr], refs[p + n_scr:]
        ids = [pl.program_id(a) for a in range(len(grid))]
        first = functools.reduce(jnp.logical_and, [i == 0 for i in ids])
        last = functools.reduce(jnp.logical_and, [i == g - 1 for i, g in zip(ids, grid)])

        @pl.when(first)
        def _():
            carried.start(cins, couts, csems)

        if carried.mid is not None:
            step, steps = 0, 1
            for i, g in zip(ids, grid):
                step, steps = step * g + i, steps * g

            @pl.when(step == min(steps - 1, (steps * carried.mid_percent) // 100))
            def _():
                carried.mid(cins, couts, csems)

        body(*ins, *outs, *scr)

        @pl.when(last)
        def _():
            carried.finish(cins, couts, csems)

    res = pl.pallas_call(wrapped, in_specs=in_specs + [_ANY] * ci, out_specs=out_specs + [_ANY] * co,
                         out_shape=out_shape + carried.outs, scratch_shapes=scratch + carried.sems,
                         **common)(*args, *carried.ins)
    return list(res[:n_out]), list(res[n_out:])


def _peers(x, y, c):
    out = []
    for rel in range(1, N_DEV):
        dx, dy, dc = (rel >> 2) & 1, (rel >> 1) & 1, rel & 1
        out.append((x + dx - 2 * x * dx, y + dy - 2 * y * dy, c + dc - 2 * c * dc))
    return out


def _direct_exchange(srcs, layout, out_shapes, scatter, src_rows=None):
    n = len(srcs)

    def copies(ins, outs, sems):
        send, recv, loc = sems
        x, y, c = _place()
        me = 4 * x + 2 * y + c
        out, arrive, local = [], [], []
        for i in range(n):
            j, off = layout[i]
            first, rows = (0, srcs[i].shape[-2]) if src_rows is None else src_rows[i]

            def piece(k):
                return ins[i].at[k, pl.ds(first, rows)] if scatter else ins[i]

            for r, peer in enumerate(_peers(x, y, c)):
                pidx = 4 * peer[0] + 2 * peer[1] + peer[2]
                kw = dict(send_sem=send.at[7 * i + r], recv_sem=recv.at[7 * i + r], device_id=peer, device_id_type=MESH)
                out.append(pltpu.make_async_remote_copy(
                    src_ref=piece(pidx), dst_ref=outs[j].at[me, pl.ds(off, rows)], **kw))
                arrive.append(pltpu.make_async_remote_copy(
                    src_ref=piece(pidx), dst_ref=outs[j].at[pidx, pl.ds(off, rows)], **kw))
            local.append(pltpu.make_async_copy(piece(me), outs[j].at[me, pl.ds(off, rows)], loc.at[i]))
        return out, arrive, local

    def start(ins, outs, sems):
        out, _, local = copies(ins, outs, sems)
        for cp in local + out:
            cp.start()

    def finish(ins, outs, sems):
        out, arrive, local = copies(ins, outs, sems)
        for cp in arrive:
            cp.wait_recv()
        for cp in out:
            cp.wait_send()
        for cp in local:
            cp.wait()

    return _Carried(srcs, [jax.ShapeDtypeStruct((N_DEV,) + tuple(s), d) for s, d in out_shapes],
                    [pltpu.SemaphoreType.DMA((7 * n,)), pltpu.SemaphoreType.DMA((7 * n,)),
                     pltpu.SemaphoreType.DMA((n,))], start, finish)


def _two_level_gather(shards, mid_percent=MID_STEP_PERCENT):
    n = len(shards)

    def copies(ins, outs, sems):
        send, recv, loc = sems
        x, y, c = _place()
        me, sibling = (x, y, c), (x, y, 1 - c)
        chips = [(1 - x, y), (x, 1 - y), (1 - x, 1 - y)]

        def win(i, place):
            return outs[i].at[4 * place[0] + 2 * place[1] + place[2]]

        def copy(i, k, block, to, src=None):
            return pltpu.make_async_remote_copy(
                src_ref=win(i, block) if src is None else src, dst_ref=win(i, block),
                send_sem=send.at[7 * i + k], recv_sem=recv.at[7 * i + k], device_id=to, device_id_type=MESH)

        own, passed, ici_in, d2d_in, local = [], [], [], [], []
        for i in range(n):
            own += [copy(i, 0, me, sibling, src=ins[i])]
            own += [copy(i, 1 + j, me, (*chip, c), src=ins[i]) for j, chip in enumerate(chips)]
            passed += [copy(i, 4 + j, (*chip, c), sibling) for j, chip in enumerate(chips)]
            ici_in += [copy(i, 1 + j, (*chip, c), me) for j, chip in enumerate(chips)]
            d2d_in += [copy(i, 0, sibling, me)] + [copy(i, 4 + j, (*chip, 1 - c), me) for j, chip in enumerate(chips)]
            local.append(pltpu.make_async_copy(ins[i], win(i, me), loc.at[i]))
        return own, passed, ici_in, d2d_in, local

    def start(ins, outs, sems):
        own, _, _, _, local = copies(ins, outs, sems)
        for cp in local + own:
            cp.start()

    def mid(ins, outs, sems):
        _, passed, ici_in, _, _ = copies(ins, outs, sems)
        for arrived, onward in zip(ici_in, passed):
            arrived.wait_recv()
            onward.start()

    def finish(ins, outs, sems):
        own, passed, _, d2d_in, local = copies(ins, outs, sems)
        for cp in d2d_in:
            cp.wait_recv()
        for cp in own + passed:
            cp.wait_send()
        for cp in local:
            cp.wait()

    return _Carried(shards, [jax.ShapeDtypeStruct((N_DEV,) + tuple(s.shape), s.dtype) for s in shards],
                    [pltpu.SemaphoreType.DMA((7 * n,)), pltpu.SemaphoreType.DMA((7 * n,)),
                     pltpu.SemaphoreType.DMA((n,))], start, finish, mid, mid_percent)


def _run_exchange(carried, name):
    ci = len(carried.ins)

    def body(*refs):
        ins, outs, sems = refs[:ci], refs[ci:ci + len(carried.outs)], refs[ci + len(carried.outs):]
        carried.start(ins, outs, sems)
        if carried.mid is not None:
            carried.mid(ins, outs, sems)
        carried.finish(ins, outs, sems)

    return list(pl.pallas_call(body, name=name, in_specs=[_ANY] * ci, out_specs=[_ANY] * len(carried.outs),
                               out_shape=carried.outs, scratch_shapes=carried.sems)(*carried.ins))


def _dg(a, b, dn):
    return lax.dot_general(a.astype(BF16), b.astype(BF16), dn, preferred_element_type=F32)


@jax.custom_vjp
def mm_nn(a, b):
    return _dg(a, b, _NN)


@jax.custom_vjp
def mm_nt(a, b):
    return _dg(a, b, _NT)


@jax.custom_vjp
def mm_tn(a, b):
    return _dg(a, b, _TN)


mm_nn.defvjp(lambda a, b: (_dg(a, b, _NN), (a, b)), lambda r, ct: (mm_nt(ct, r[1]), mm_tn(r[0], ct)))
mm_nt.defvjp(lambda a, b: (_dg(a, b, _NT), (a, b)), lambda r, ct: (mm_nn(ct, r[1]), mm_tn(ct, r[0])))
mm_tn.defvjp(lambda a, b: (_dg(a, b, _TN), (a, b)), lambda r, ct: (mm_nt(r[1], ct), mm_nn(r[0], ct)))


def _split3(x):
    p1 = x.astype(BF16)
    r1 = x - p1.astype(F32)
    p2 = r1.astype(BF16)
    r2 = r1 - p2.astype(F32)
    return p1, p2, r2.astype(BF16)


def _exact01(x, c, dn, const_left):
    acc = None
    for p in reversed(_split3(x)):
        t = (lax.dot_general(c, p, dn, preferred_element_type=F32) if const_left
             else lax.dot_general(p, c, dn, preferred_element_type=F32))
        acc = t if acc is None else acc + t
    return acc


def _make_cmm(dn, const_left, bwd_name):
    @jax.custom_vjp
    def f(x, c):
        return _exact01(x, c, dn, const_left)

    def fwd(x, c):
        return _exact01(x, c, dn, const_left), c

    def bwd(c, ct):
        return _CMM[bwd_name](ct, c), jnp.zeros_like(c)

    f.defvjp(fwd, bwd)
    return f


_CMM = {}
_CMM["xc"] = _make_cmm(_NN, False, "xct")
_CMM["xct"] = _make_cmm(_NT, False, "xc")
_CMM["cx"] = _make_cmm(_NN, True, "ctx")
_CMM["ctx"] = _make_cmm(_TN, True, "cx")


def _sigmoid(x):
    return 0.5 * jnp.tanh(0.5 * x) + 0.5


@jax.custom_vjp
def _silu(x):
    return x * _sigmoid(x)


def _silu_fwd(x):
    return _silu(x), x


def _silu_bwd(x, ct):
    s = _sigmoid(x)
    return (ct * (s * (1.0 + x * (1.0 - s))),)


_silu.defvjp(_silu_fwd, _silu_bwd)


def _log1p_pos(e):
    u = 1.0 + e
    d = u - 1.0
    return jnp.where(d == 0.0, e, jnp.log(u) * (e / jnp.where(d == 0.0, 1.0, d)))


@jax.custom_vjp
def _softplus(x):
    return jnp.maximum(x, 0.0) + _log1p_pos(jnp.exp(-jnp.abs(x)))


def _softplus_fwd(x):
    return _softplus(x), x


def _softplus_bwd(x, ct):
    return (ct * _sigmoid(x),)


_softplus.defvjp(_softplus_fwd, _softplus_bwd)


CONV_HALO = 8


def _make_shift(j):
    @jax.custom_vjp
    def f(ext):
        return pltpu.roll(ext, j, 0)[CONV_HALO:, :]

    def fwd(ext):
        return f(ext), None

    def bwd(_, ct):
        pad = jnp.concatenate([jnp.zeros((CONV_HALO, ct.shape[1]), ct.dtype), ct], axis=0)
        return (pltpu.roll(pad, CONV_HALO + CHUNK - j, 0),)

    f.defvjp(fwd, bwd)
    return f


_SHIFT = {j: _make_shift(j) for j in (1, 2, 3)}


@jax.custom_vjp
def _swap_halves(x):
    return pltpu.roll(x, HEAD_DIM, 1)


_swap_halves.defvjp(lambda x: (_swap_halves(x), None), lambda _, ct: (pltpu.roll(ct, HEAD_DIM, 1),))


def _rms_fwd(x, g):
    r = lax.rsqrt(jnp.mean(x * x, axis=-1, keepdims=True) + RMS_EPS)
    n = x * r
    return n * g, n, r


def _rms_bwd(dy, n, r, g):
    dn = dy * g
    dx = r * (dn - n * jnp.mean(dn * n, axis=-1, keepdims=True))
    dg = jnp.sum(dy * n, axis=0, keepdims=True)
    return dx, dg


def _one(cond):
    return jnp.where(cond, 1.0, 0.0)


def _pool_tile(xe, g, ws, b, scale, tile, tt):
    r = lax.rsqrt(jnp.mean(xe * xe, axis=-1, keepdims=True) + RMS_EPS)
    hn = xe * r * g
    row_e = lax.broadcasted_iota(jnp.int32, (tt + POOL_HALO, POOL_GROUP), 0)
    keep = _one(jnp.logical_or(row_e >= POOL_HALO, tile > 0))
    rr = lax.broadcasted_iota(jnp.int32, (tt, tt + POOL_HALO), 0)
    qq = lax.broadcasted_iota(jnp.int32, (tt, tt + POOL_HALO), 1)
    dd = qq - rr
    tpos = tile * tt + lax.broadcasted_iota(jnp.int32, (tt, POOL_GROUP), 0)
    outs = []
    for gi, w in enumerate(POOL_WINDOWS):
        hg = hn[:, gi * POOL_GROUP:(gi + 1) * POOL_GROUP] * keep
        band = _one(jnp.logical_and(dd >= POOL_HALO - w + 1, dd <= POOL_HALO)).astype(BF16)
        cnt = jnp.minimum(tpos + 1, w).astype(F32)
        pooled = _CMM["cx"](hg, band) / cnt
        mixed = pooled - hg[POOL_HALO:, :]
        outs.append(mm_nn(mixed, ws[gi]))
    out = (jnp.concatenate(outs, axis=1) + b) * scale
    return xe[POOL_HALO:, :] + out


def _pool_specs(tt, nt, rev):
    per = tt // POOL_HALO
    t_of = (lambda i: nt - 1 - i) if rev else (lambda i: i)
    main = pl.BlockSpec((tt, D_MODEL), lambda i: (t_of(i), 0))
    halo = pl.BlockSpec((POOL_HALO, D_MODEL), lambda i: (jnp.maximum(t_of(i) * per - 1, 0), 0))
    vec = pl.BlockSpec((1, D_MODEL), lambda i: (0, 0))
    wsp = pl.BlockSpec((N_DEV, 4 * POOL_SHARD, POOL_GROUP), lambda i: (0, 0, 0))
    return main, halo, vec, wsp


def _pool_weights(w_ref):
    return tuple(
        jnp.concatenate([w_ref[k, gi * POOL_SHARD:(gi + 1) * POOL_SHARD, :] for k in range(N_DEV)], axis=0).astype(F32)
        for gi in range(4))


def _pool_fwd(x, g, w, b, scale, carried=None):
    t = x.shape[0]
    tt = min(t, 256)
    nt = t // tt
    main, halo, vec, wsp = _pool_specs(tt, nt, False)

    def body(xm_ref, xh_ref, g_ref, w_ref, b_ref, s_ref, o_ref):
        i = pl.program_id(0)
        xe = jnp.concatenate([xh_ref[...], xm_ref[...]], axis=0)
        o_ref[...] = _pool_tile(xe, g_ref[...], _pool_weights(w_ref), b_ref[...], s_ref[...], i, tt)

    return _pcall(
        body, name="pool_fwd", grid=(nt,),
        in_specs=[main, halo, vec, wsp, vec, vec], out_specs=[main],
        out_shape=[jax.ShapeDtypeStruct((t, D_MODEL), F32)],
        sem=("arbitrary",), args=(x, x, g, w, b, scale), carried=carried)


def _pool_bwd(x, dh, g, w, b, scale, carried=None):
    t = x.shape[0]
    tt = min(t, 256)
    nt = t // tt
    main, halo, vec, wsp = _pool_specs(tt, nt, True)

    def body(xm_ref, xh_ref, dh_ref, g_ref, w_ref, b_ref, s_ref,
             dx_ref, dw_ref, db_ref, ds_ref, dg_ref, carry, dw_acc):
        i = pl.program_id(0)
        tile = nt - 1 - i

        @pl.when(i == 0)
        def _():
            carry[...] = jnp.zeros_like(carry)
            dw_acc[...] = jnp.zeros_like(dw_acc)
            db_ref[...] = jnp.zeros_like(db_ref)
            ds_ref[...] = jnp.zeros_like(ds_ref)
            dg_ref[...] = jnp.zeros_like(dg_ref)

        xe = jnp.concatenate([xh_ref[...], xm_ref[...]], axis=0)
        _, vjp = jax.vjp(lambda a, gg, ww, bb, ss: _pool_tile(a, gg, ww, bb, ss, tile, tt),
                         xe, g_ref[...], _pool_weights(w_ref), b_ref[...], s_ref[...])
        dxe, dgv, dws, dbv, dsv = vjp(dh_ref[...])
        dx_ref[...] = dxe[POOL_HALO:, :]
        dx_ref[tt - POOL_HALO:tt, :] += carry[...]
        carry[...] = dxe[:POOL_HALO, :]
        for gi in range(4):
            dw_acc[gi] += dws[gi]
        db_ref[...] += dbv
        ds_ref[...] += dsv
        dg_ref[...] += dgv

        @pl.when(i == nt - 1)
        def _():
            for k in range(N_DEV):
                for gi in range(4):
                    dw_ref[k, gi * POOL_SHARD:(gi + 1) * POOL_SHARD, :] = dw_acc[gi, k * POOL_SHARD:(k + 1) * POOL_SHARD, :]

    return _pcall(
        body, name="pool_bwd", grid=(nt,),
        in_specs=[main, halo, main, vec, wsp, vec, vec],
        out_specs=[main, wsp, vec, vec, vec],
        out_shape=[jax.ShapeDtypeStruct((t, D_MODEL), F32),
                   jax.ShapeDtypeStruct((N_DEV, 4 * POOL_SHARD, POOL_GROUP), F32),
                   jax.ShapeDtypeStruct((1, D_MODEL), F32),
                   jax.ShapeDtypeStruct((1, D_MODEL), F32),
                   jax.ShapeDtypeStruct((1, D_MODEL), F32)],
        scratch_shapes=[pltpu.VMEM((POOL_HALO, D_MODEL), F32), pltpu.VMEM((4, POOL_GROUP, POOL_GROUP), F32)],
        sem=("arbitrary",), args=(x, x, dh, g, w, b, scale), carried=carried)


def _mlp_weight_specs():
    fb = D_FF // N_DEV
    return (pl.BlockSpec((None, D_MODEL, fb), lambda i, k: (k, 0, 0)),
            pl.BlockSpec((None, fb, D_MODEL), lambda i, k: (k, 0, 0)))


def _mlp_fwd(h, g, w1g, w2g, name, carried=None):
    t = h.shape[0]
    tt = min(t, MATMUL_TOKENS)
    nk, fb = N_DEV, D_FF // N_DEV
    w1_spec, w2_spec = _mlp_weight_specs()

    def body(h_ref, g_ref, w1_ref, w2_ref, o_ref, u_ref, hm_ref, hm_s, acc_s):
        k = pl.program_id(1)

        @pl.when(k == 0)
        def _():
            xv = h_ref[...]
            y, _, _ = _rms_fwd(xv, g_ref[...])
            hb = y.astype(BF16)
            hm_s[...] = hb
            hm_ref[...] = hb
            acc_s[...] = xv

        a = jnp.dot(hm_s[...], w1_ref[...], preferred_element_type=F32)
        u = jnp.maximum(a, 0.0)
        u_ref[...] = u.astype(BF16)
        acc_s[...] += jnp.dot((u * u).astype(BF16), w2_ref[...], preferred_element_type=F32)

        @pl.when(k == nk - 1)
        def _():
            o_ref[...] = acc_s[...]

    return _pcall(
        body, name=name, grid=(t // tt, nk),
        in_specs=[pl.BlockSpec((tt, D_MODEL), lambda i, k: (i, 0)),
                  pl.BlockSpec((1, D_MODEL), lambda i, k: (0, 0)),
                  w1_spec, w2_spec],
        out_specs=[pl.BlockSpec((tt, D_MODEL), lambda i, k: (i, 0)),
                   pl.BlockSpec((tt, fb), lambda i, k: (i, k)),
                   pl.BlockSpec((tt, D_MODEL), lambda i, k: (i, 0))],
        out_shape=[jax.ShapeDtypeStruct((t, D_MODEL), F32),
                   jax.ShapeDtypeStruct((t, nk * fb), BF16),
                   jax.ShapeDtypeStruct((t, D_MODEL), BF16)],
        scratch_shapes=[pltpu.VMEM((tt, D_MODEL), BF16), pltpu.VMEM((tt, D_MODEL), F32)],
        sem=("arbitrary", "arbitrary"), args=(h, g, w1g, w2g), carried=carried)


def _mlp_bwd(dh, dhb, h, g, u, w1g, w2g, name, carried=None):
    t = h.shape[0]
    tt = min(t, MATMUL_TOKENS)
    nk, fb = N_DEV, D_FF // N_DEV
    w1_spec, w2_spec = _mlp_weight_specs()

    def body(dh_ref, dhb_ref, h_ref, g_ref, u_ref, w1_ref, w2_ref,
             dhin_ref, dhinb_ref, da_ref, dg_ref, acc_s):
        i = pl.program_id(0)
        k = pl.program_id(1)

        @pl.when(jnp.logical_and(i == 0, k == 0))
        def _():
            dg_ref[...] = jnp.zeros_like(dg_ref)

        @pl.when(k == 0)
        def _():
            acc_s[...] = jnp.zeros_like(acc_s)

        dv = lax.dot_general(dhb_ref[...], w2_ref[...], _NT, preferred_element_type=F32)
        dab = (dv * u_ref[...].astype(F32)).astype(BF16)
        da_ref[...] = dab
        acc_s[...] += lax.dot_general(dab, w1_ref[...], _NT, preferred_element_type=F32)

        @pl.when(k == nk - 1)
        def _():
            gv = g_ref[...]
            _, n, r = _rms_fwd(h_ref[...], gv)
            dx, dg = _rms_bwd(2.0 * acc_s[...], n, r, gv)
            dhin = dh_ref[...] + dx
            dhin_ref[...] = dhin
            dhinb_ref[...] = dhin.astype(BF16)
            dg_ref[...] += dg

    tile = pl.BlockSpec((tt, D_MODEL), lambda i, k: (i, 0))
    return _pcall(
        body, name=name, grid=(t // tt, nk),
        in_specs=[tile, tile, tile, pl.BlockSpec((1, D_MODEL), lambda i, k: (0, 0)),
                  pl.BlockSpec((tt, fb), lambda i, k: (i, k)), w1_spec, w2_spec],
        out_specs=[tile, tile, pl.BlockSpec((tt, fb), lambda i, k: (i, k)),
                   pl.BlockSpec((1, D_MODEL), lambda i, k: (0, 0))],
        out_shape=[jax.ShapeDtypeStruct((t, D_MODEL), F32),
                   jax.ShapeDtypeStruct((t, D_MODEL), BF16),
                   jax.ShapeDtypeStruct((t, nk * fb), BF16),
                   jax.ShapeDtypeStruct((1, D_MODEL), F32)],
        scratch_shapes=[pltpu.VMEM((tt, D_MODEL), F32)],
        sem=("arbitrary", "arbitrary"), args=(dh, dhb, h, g, u, w1g, w2g), carried=carried)


def _matmul_tn(a, b, name, square_a=False, col_blocked=False, carried=None, scale=None):
    t, k1 = a.shape
    k2 = b.shape[1]
    tt = min(t, TN_TOKENS)
    nt = t // tt
    wc = k2 if k1 * k2 * 4 <= TN_ACC_BYTES else k2 // 2
    nb = wc // COL_BLK

    def body(a_ref, b_ref, o_ref, acc):
        s = pl.program_id(1)

        @pl.when(s == 0)
        def _():
            acc[...] = jnp.zeros_like(acc)

        av = a_ref[...]
        if square_a:
            af = av.astype(F32)
            av = (af * af).astype(BF16)
        acc[...] += lax.dot_general(av, b_ref[...], _TN, preferred_element_type=F32)

        @pl.when(s == nt - 1)
        def _():
            def done(v):
                return (v if scale is None else scale * v).astype(o_ref.dtype)

            if col_blocked:
                for k in range(nb):
                    o_ref[k] = done(acc[:, k * COL_BLK:(k + 1) * COL_BLK])
            else:
                o_ref[...] = done(acc[...])

    if col_blocked:
        out_shape = jax.ShapeDtypeStruct((k2 // COL_BLK, k1, COL_BLK), BF16)
        out_spec = pl.BlockSpec((nb, k1, COL_BLK), lambda j, s: (j, 0, 0))
    else:
        out_shape = jax.ShapeDtypeStruct((k1, k2), BF16)
        out_spec = pl.BlockSpec((k1, wc), lambda j, s: (0, j))
    outs, landed = _pcall(
        body, name=name, grid=(k2 // wc, nt),
        in_specs=[pl.BlockSpec((tt, k1), lambda j, s: (s, 0)),
                  pl.BlockSpec((tt, wc), lambda j, s: (s, j))],
        out_specs=[out_spec], out_shape=[out_shape],
        scratch_shapes=[pltpu.VMEM((k1, wc), F32)],
        sem=("arbitrary", "arbitrary"), args=(a, b), carried=carried)
    return (outs[0], landed) if carried is not None else outs[0]


def _norm_matmul(h, g, w, carried=None):
    t = h.shape[0]
    tt = min(t, MATMUL_TOKENS)
    n = w.shape[1]

    def body(h_ref, g_ref, w_ref, o_ref, hn_ref, hn_s):
        @pl.when(pl.program_id(1) == 0)
        def _():
            y, _, _ = _rms_fwd(h_ref[...], g_ref[...])
            hb = y.astype(BF16)
            hn_s[...] = hb
            hn_ref[...] = hb

        o_ref[...] = jnp.dot(hn_s[...], w_ref[...], preferred_element_type=F32)

    return _pcall(
        body, name="ssm_in_proj", grid=(t // tt, n // PROJ_BLK),
        in_specs=[pl.BlockSpec((tt, D_MODEL), lambda i, j: (i, 0)),
                  pl.BlockSpec((1, D_MODEL), lambda i, j: (0, 0)),
                  pl.BlockSpec((D_MODEL, PROJ_BLK), lambda i, j: (0, j))],
        out_specs=[pl.BlockSpec((tt, PROJ_BLK), lambda i, j: (i, j)),
                   pl.BlockSpec((tt, D_MODEL), lambda i, j: (i, 0))],
        out_shape=[jax.ShapeDtypeStruct((t, n), F32), jax.ShapeDtypeStruct((t, D_MODEL), BF16)],
        scratch_shapes=[pltpu.VMEM((tt, D_MODEL), BF16)],
        sem=("arbitrary", "arbitrary"), args=(h, g, w), carried=carried)


def _in_proj_bwd(dzx, w, h, g, dh_next, carried=None):
    t = h.shape[0]
    tt = min(t, MATMUL_TOKENS)
    n = w.shape[1]
    nj = n // PROJ_BLK

    def body(dz_ref, w_ref, h_ref, g_ref, dn_ref, dh_ref, dhb_ref, dg_ref, acc):
        i = pl.program_id(0)
        j = pl.program_id(1)

        @pl.when(jnp.logical_and(i == 0, j == 0))
        def _():
            dg_ref[...] = jnp.zeros_like(dg_ref)

        @pl.when(j == 0)
        def _():
            acc[...] = jnp.zeros_like(acc)

        acc[...] += lax.dot_general(dz_ref[...], w_ref[...], _NT, preferred_element_type=F32)

        @pl.when(j == nj - 1)
        def _():
            gv = g_ref[...]
            _, nn, r = _rms_fwd(h_ref[...], gv)
            dx, dg = _rms_bwd(acc[...], nn, r, gv)
            dh = dn_ref[...] + dx
            dh_ref[...] = dh
            dhb_ref[...] = dh.astype(BF16)
            dg_ref[...] += dg

    tile = pl.BlockSpec((tt, D_MODEL), lambda i, j: (i, 0))
    return _pcall(
        body, name="ssm_in_proj_bwd", grid=(t // tt, nj),
        in_specs=[pl.BlockSpec((tt, PROJ_BLK), lambda i, j: (i, j)),
                  pl.BlockSpec((D_MODEL, PROJ_BLK), lambda i, j: (0, j)),
                  tile, pl.BlockSpec((1, D_MODEL), lambda i, j: (0, 0)), tile],
        out_specs=[tile, tile, pl.BlockSpec((1, D_MODEL), lambda i, j: (0, 0))],
        out_shape=[jax.ShapeDtypeStruct((t, D_MODEL), F32), jax.ShapeDtypeStruct((t, D_MODEL), BF16),
                   jax.ShapeDtypeStruct((1, D_MODEL), F32)],
        scratch_shapes=[pltpu.VMEM((tt, D_MODEL), F32)],
        sem=("arbitrary", "arbitrary"), args=(dzx, w, h, g, dh_next), carried=carried)


def _ssd_consts():
    lane = lax.broadcasted_iota(jnp.int32, (CHUNK, CHUNK), 1)
    row = lax.broadcasted_iota(jnp.int32, (CHUNK, CHUNK), 0)
    causal = lane <= row
    tri = _one(causal).astype(BF16)
    er = lax.broadcasted_iota(jnp.int32, (CHUNK, GROUP_X), 0)
    ec = lax.broadcasted_iota(jnp.int32, (CHUNK, GROUP_X), 1)
    expand = _one(jnp.right_shift(ec, 6) == er).astype(BF16)
    return dict(causal=causal, tri=tri, expand=expand, lo=lane < HEAD_DIM)


def _conv_silu(cur, prev, w, b):
    ext = jnp.concatenate([prev, cur], axis=0)
    acc = cur * w[3] + b
    for j in (1, 2, 3):
        acc = acc + _SHIFT[j](ext) * w[3 - j]
    return _silu(acc)


def _ssd_chunk(raw, rawp, ht, cw, cb_, dtb, alog, dsk, k):
    act = _conv_silu(raw[:, :GROUP_CONV], rawp[:, :GROUP_CONV], cw, cb_)
    xs = act[:, :GROUP_X]
    bm = act[:, GROUP_X:GROUP_X + D_STATE]
    cm = act[:, GROUP_X + D_STATE:]
    dt = _softplus(raw[:, GROUP_CONV:] + dtb)
    a = -jnp.exp(alog)
    xc = _CMM["xc"]

    def lanes(rowv):
        return jnp.sum(xc(jnp.broadcast_to(rowv, (16, CHUNK)), k["expand"]), axis=0, keepdims=True) * (1.0 / 16.0)

    dt_e = xc(dt, k["expand"])
    adt_e = dt_e * lanes(a)
    acs_e = _CMM["cx"](adt_e, k["tri"])
    tot_e = jnp.sum(adt_e, axis=0, keepdims=True)
    gmat = mm_nt(cm, bm)
    xdt = xs * dt_e
    ys = []
    for j in range(HEADS_PER_GROUP // 2):
        pair = acs_e[:, j * CHUNK:(j + 1) * CHUNK]
        swapped = _swap_halves(pair)
        ms = []
        for cb in (jnp.where(k["lo"], pair, swapped), jnp.where(k["lo"], swapped, pair)):
            seg = cb - cb.T
            ms.append(gmat * jnp.exp(jnp.where(k["causal"], seg, -jnp.inf)))
        xp = xdt[:, j * CHUNK:(j + 1) * CHUNK]
        rhs = jnp.concatenate([jnp.where(k["lo"], xp, 0.0), jnp.where(k["lo"], 0.0, xp)], axis=0)
        ys.append(mm_nn(jnp.concatenate(ms, axis=1), rhs))
    y_diag = jnp.concatenate(ys, axis=1)
    y_off = jnp.exp(acs_e) * mm_nn(cm, ht)
    h_new = jnp.exp(tot_e) * ht + mm_tn(bm, xdt * jnp.exp(tot_e - acs_e))
    return y_diag + y_off + lanes(dsk) * xs, h_new


def _ssd_in_specs(nc, rev):
    c_of = (lambda c: nc - 1 - c) if rev else (lambda c: c)
    per = CHUNK // CONV_HALO
    zx = [pl.BlockSpec((CHUNK, GROUP_COLS), lambda g, c: (c_of(c), g)),
          pl.BlockSpec((CONV_HALO, GROUP_COLS), lambda g, c: (jnp.maximum(c_of(c) * per - 1, 0), g))]
    conv = [pl.BlockSpec((4, GROUP_CONV), lambda g, c: (0, g)), pl.BlockSpec((1, GROUP_CONV), lambda g, c: (0, g))]
    head = [pl.BlockSpec((None, 1, 128), lambda g, c: (g, 0, 0))] * 3
    return zx + conv + head, c_of


def _load_chunk_args(refs, has_prev):
    raw, rawp, cw, cb_, dtb, alog, dsk = refs
    return (raw[...], rawp[...] * has_prev, tuple(cw[pl.ds(i, 1), :] for i in range(4)), cb_[...],
            dtb[...], alog[...], dsk[...])


def _ssd_fwd(zx, conv_w, conv_b, dtb, alog, dsk, carried=None):
    t = zx.shape[0]
    nc = t // CHUNK
    in_specs, _ = _ssd_in_specs(nc, False)

    def body(*refs):
        ins, (y_ref, hs_ref, ht) = refs[:7], refs[7:]
        c = pl.program_id(1)

        @pl.when(c == 0)
        def _():
            ht[...] = jnp.zeros_like(ht)

        a = _load_chunk_args(ins, _one(c > 0))
        h_in = ht[...]
        y, h_new = _ssd_chunk(*a[:2], h_in, *a[2:], _ssd_consts())
        y_ref[...] = y
        hs_ref[...] = h_in
        ht[...] = h_new

    return _pcall(
        body, name="ssd_fwd", grid=(N_GROUPS, nc),
        in_specs=in_specs,
        out_specs=[pl.BlockSpec((CHUNK, GROUP_X), lambda g, c: (c, g)),
                   pl.BlockSpec((None, None, D_STATE, GROUP_X), lambda g, c: (g, c, 0, 0))],
        out_shape=[jax.ShapeDtypeStruct((t, D_INNER), F32),
                   jax.ShapeDtypeStruct((N_GROUPS, nc, D_STATE, GROUP_X), F32)],
        scratch_shapes=[pltpu.VMEM((D_STATE, GROUP_X), F32)],
        sem=("arbitrary", "arbitrary"), args=(zx, zx, conv_w, conv_b, dtb, alog, dsk), carried=carried)


def _ssd_bwd(zx, conv_w, conv_b, dtb, alog, dsk, hs, dy, dzx, carried=None):
    t = zx.shape[0]
    nc = t // CHUNK
    in_specs, c_of = _ssd_in_specs(nc, True)
    n_in = 10

    def body(*refs):
        ins, hs_ref, dy_ref = refs[:7], refs[7], refs[8]
        (draw_ref, dcw, dcb, ddtb, dalog, ddsk, dht, carry) = refs[n_in:]
        cc = pl.program_id(1)
        accs = (dcw, dcb, ddtb, dalog, ddsk)

        @pl.when(cc == 0)
        def _():
            for r in (dht, carry) + accs:
                r[...] = jnp.zeros_like(r)

        has_prev = _one(c_of(cc) > 0)
        a = _load_chunk_args(ins, has_prev)
        k = _ssd_consts()
        fn = lambda *args: _ssd_chunk(*args, k)
        _, vjp = jax.vjp(fn, *a[:2], hs_ref[...], *a[2:])
        graw, grawp, ght, gcw, gcb, gdtb, galog, gdsk = vjp((dy_ref[...], dht[...]))
        tail = jnp.concatenate([jnp.zeros((CHUNK - CONV_HALO, GROUP_COLS), F32), carry[...]], axis=0)
        draw_ref[...] = (graw + tail).astype(BF16)
        carry[...] = grawp * has_prev
        dht[...] = ght
        for i in range(4):
            dcw[pl.ds(i, 1), :] += gcw[i]
        for ref, val in ((dcb, gcb), (ddtb, gdtb), (dalog, galog), (ddsk, gdsk)):
            ref[...] += val

    head_out = pl.BlockSpec((None, 1, 128), lambda g, c: (g, 0, 0))
    sds = jax.ShapeDtypeStruct
    return _pcall(
        body, name="ssd_bwd", grid=(N_GROUPS, nc),
        in_specs=in_specs + [
            pl.BlockSpec((None, None, D_STATE, GROUP_X), lambda g, c: (g, c_of(c), 0, 0)),
            pl.BlockSpec((CHUNK, GROUP_X), lambda g, c: (c_of(c), g)),
            _ANY],
        out_specs=[pl.BlockSpec((CHUNK, GROUP_COLS), lambda g, c: (c_of(c), g)),
                   pl.BlockSpec((4, GROUP_CONV), lambda g, c: (0, g)),
                   pl.BlockSpec((1, GROUP_CONV), lambda g, c: (0, g)),
                   head_out, head_out, head_out],
        out_shape=[sds((t, ZX_COLS), BF16), sds((4, N_GROUPS * GROUP_CONV), F32), sds((1, N_GROUPS * GROUP_CONV), F32),
                   sds((N_GROUPS, 1, 128), F32), sds((N_GROUPS, 1, 128), F32), sds((N_GROUPS, 1, 128), F32)],
        scratch_shapes=[pltpu.VMEM((D_STATE, GROUP_X), F32), pltpu.VMEM((CONV_HALO, GROUP_COLS), F32)],
        sem=("arbitrary", "arbitrary"), args=(zx, zx, conv_w, conv_b, dtb, alog, dsk, hs, dy, dzx),
        aliases={9: 0}, carried=carried)


def _gate_norm(y, zs, ng):
    outs = []
    for k in range(N_GROUPS):
        s = y[:, k * GROUP_X:(k + 1) * GROUP_X] * _silu(zs[k])
        outs.append(s * lax.rsqrt(jnp.mean(s * s, axis=-1, keepdims=True) + RMS_EPS))
    return jnp.concatenate(outs, axis=1) * ng


def _z_specs(tt):
    first = Z_OFF // GROUP_X
    return [pl.BlockSpec((tt, GROUP_X), functools.partial(lambda k, i: (i, first + k), k)) for k in range(N_GROUPS)]


def _ssm_out_fwd(y, zx, ng, w_out, h):
    t = h.shape[0]
    tt = min(t, 512)

    def body(y_ref, z0, z1, z2, z3, ng_ref, w_ref, h_ref, o_ref):
        yn = _gate_norm(y_ref[...], (z0[...], z1[...], z2[...], z3[...]), ng_ref[...])
        o_ref[...] = h_ref[...] + jnp.dot(yn.astype(BF16), w_ref[...], preferred_element_type=F32)

    return pl.pallas_call(
        body, name="ssm_out_fwd", grid=(t // tt,),
        in_specs=[pl.BlockSpec((tt, D_INNER), lambda i: (i, 0))] + _z_specs(tt) + [
            pl.BlockSpec((1, D_INNER), lambda i: (0, 0)),
            pl.BlockSpec((D_INNER, D_MODEL), lambda i: (0, 0)),
            pl.BlockSpec((tt, D_MODEL), lambda i: (i, 0))],
        out_specs=pl.BlockSpec((tt, D_MODEL), lambda i: (i, 0)),
        out_shape=jax.ShapeDtypeStruct((t, D_MODEL), F32),
        compiler_params=_cp(("arbitrary",)),
    )(y, zx, zx, zx, zx, ng, w_out, h)


def _gate_norm_group(y, z, ng):
    s = y * _silu(z)
    return s * lax.rsqrt(jnp.mean(s * s, axis=-1, keepdims=True) + RMS_EPS) * ng


def _ssm_out_bwd(dhb, y, zx, ng, w_out):
    t = dhb.shape[0]
    tt = min(t, MATMUL_TOKENS)
    first = Z_OFF // GROUP_X

    def body(dh_ref, y_ref, z_ref, ng_ref, w_ref, dy_ref, dzx_ref, yn_ref, dng_ref):
        @pl.when(pl.program_id(1) == 0)
        def _():
            dng_ref[...] = jnp.zeros_like(dng_ref)

        dyn = lax.dot_general(dh_ref[...], w_ref[...], _NT, preferred_element_type=F32)
        yn, vjp = jax.vjp(_gate_norm_group, y_ref[...], z_ref[...], ng_ref[...])
        dy, dz, dng = vjp(dyn)
        dy_ref[...] = dy
        dzx_ref[...] = dz.astype(BF16)
        yn_ref[...] = yn.astype(BF16)
        dng_ref[...] += dng

    grp = pl.BlockSpec((tt, GROUP_X), lambda k, i: (i, k))
    zgrp = pl.BlockSpec((tt, GROUP_X), lambda k, i: (i, first + k))
    gain = pl.BlockSpec((1, GROUP_X), lambda k, i: (0, k))
    return pl.pallas_call(
        body, name="ssm_out_bwd", grid=(N_GROUPS, t // tt),
        in_specs=[pl.BlockSpec((tt, D_MODEL), lambda k, i: (i, 0)), grp, zgrp, gain,
                  pl.BlockSpec((GROUP_X, D_MODEL), lambda k, i: (k, 0))],
        out_specs=[grp, zgrp, grp, gain],
        out_shape=[jax.ShapeDtypeStruct((t, D_INNER), F32), jax.ShapeDtypeStruct((t, ZX_COLS), BF16),
                   jax.ShapeDtypeStruct((t, D_INNER), BF16), jax.ShapeDtypeStruct((1, D_INNER), F32)],
        compiler_params=_cp(("arbitrary", "arbitrary")),
    )(dhb, y, zx, ng, w_out)


def _final(h, g, tgt):
    t = h.shape[0]
    tt = min(t, 512)
    nt = t // tt

    def body(h_ref, g_ref, t_ref, dh_ref, dhb_ref, loss_ref, dg_ref, lacc):
        i = pl.program_id(0)

        @pl.when(i == 0)
        def _():
            dg_ref[...] = jnp.zeros_like(dg_ref)
            lacc[...] = jnp.zeros_like(lacc)

        gv = g_ref[...]
        y, n, r = _rms_fwd(h_ref[...], gv)
        err = y - t_ref[...]
        lacc[...] += jnp.sum(err * err, axis=0, keepdims=True)
        dx, dg = _rms_bwd(err * (1.0 / D_MODEL), n, r, gv)
        dh_ref[...] = dx
        dhb_ref[...] = dx.astype(BF16)
        dg_ref[...] += dg

        @pl.when(i == nt - 1)
        def _():
            loss_ref[...] = jnp.zeros_like(loss_ref) + (0.5 / D_MODEL) * jnp.sum(lacc[...])

    tile = pl.BlockSpec((tt, D_MODEL), lambda i: (i, 0))
    vec = pl.BlockSpec((1, D_MODEL), lambda i: (0, 0))
    return pl.pallas_call(
        body, name="final_loss", grid=(nt,),
        in_specs=[tile, vec, tile],
        out_specs=[tile, tile, pl.BlockSpec((1, 128), lambda i: (0, 0)), vec],
        out_shape=[jax.ShapeDtypeStruct((t, D_MODEL), F32), jax.ShapeDtypeStruct((t, D_MODEL), BF16),
                   jax.ShapeDtypeStruct((1, 128), F32), jax.ShapeDtypeStruct((1, D_MODEL), F32)],
        scratch_shapes=[pltpu.VMEM((1, D_MODEL), F32)],
        compiler_params=_cp(("arbitrary",)),
    )(h, g, tgt)


def _adamw_reduced_parts(w, lands, m, v, name):
    rows, cols = w.shape
    br = 128
    nl = lands[0].shape[0]
    starts, blocks = [], []
    for land in lands:
        starts.append(sum(blocks))
        blocks.append(land.shape[1] // br)

    def body(w_ref, *refs):
        l_refs, (m_ref, v_ref, g_ref, d_ref, m2_ref, v2_ref) = refs[:len(lands)], refs[len(lands):]
        i = pl.program_id(0)
        gv = None
        for ref, first in zip(l_refs, starts):
            acc = ref[0].astype(F32)
            for q in range(1, nl):
                acc = acc + ref[q].astype(F32)
            gv = acc if gv is None else jnp.where(i >= first, acc, gv)
        g_ref[...] = gv
        d_ref[...], m2_ref[...], v2_ref[...] = _adamw_math(w_ref[...], gv, m_ref[...], v_ref[...])

    spec = pl.BlockSpec((br, cols), lambda i: (i, 0))
    land_specs = [pl.BlockSpec((nl, br, cols), functools.partial(
        lambda first, nb, i: (0, jnp.clip(i - first, 0, nb - 1), 0), first, nb)) for first, nb in zip(starts, blocks)]
    out = jax.ShapeDtypeStruct((rows, cols), F32)
    return pl.pallas_call(
        body, name=name, grid=(rows // br,),
        in_specs=[spec] + land_specs + [spec, spec], out_specs=[spec] * 4, out_shape=[out] * 4,
        compiler_params=_cp(("arbitrary",)),
    )(w, *lands, m, v)


def _adamw_reduced_layers(w, lands, m, v, name):
    _, rows, cols = w.shape
    br = rows if rows <= 256 else 256
    nb = rows // br
    nl = lands[0].shape[0]

    def body(w_ref, l0_ref, l1_ref, m_ref, v_ref, g_ref, d_ref, m2_ref, v2_ref):
        def total(ref):
            acc = ref[0].astype(F32)
            for q in range(1, nl):
                acc = acc + ref[q].astype(F32)
            return acc

        gv = jnp.where(pl.program_id(0) == 0, total(l0_ref), total(l1_ref))
        g_ref[...] = gv
        d_ref[...], m2_ref[...], v2_ref[...] = _adamw_math(w_ref[...], gv, m_ref[...], v_ref[...])

    spec = pl.BlockSpec((None, br, cols), lambda l, i: (l, i, 0))
    land0 = pl.BlockSpec((nl, br, cols), lambda l, i: (0, jnp.where(l == 0, i, nb - 1), 0))
    land1 = pl.BlockSpec((nl, br, cols), lambda l, i: (0, jnp.where(l == 1, i, 0), 0))
    out = jax.ShapeDtypeStruct(w.shape, F32)
    return pl.pallas_call(
        body, name=name, grid=(2, nb),
        in_specs=[spec, land0, land1, spec, spec], out_specs=[spec] * 4, out_shape=[out] * 4,
        compiler_params=_cp(("arbitrary", "arbitrary")),
    )(w, lands[0], lands[1], m, v)


def _all_reduce_small(sp):
    rows, n = sp.shape

    def body(x_ref, o_ref, land, send_sems, recv_sems):
        x, y, c = _place()
        me = 4 * x + 2 * y + c
        land[me] = x_ref[...]
        cps = []
        for rel in range(1, N_DEV):
            dx, dy, dc = (rel >> 2) & 1, (rel >> 1) & 1, rel & 1
            px = x + dx - 2 * x * dx
            py = y + dy - 2 * y * dy
            pc = c + dc - 2 * c * dc
            peer = 4 * px + 2 * py + pc
            cps.append((pltpu.make_async_remote_copy(
                src_ref=x_ref, dst_ref=land.at[me], send_sem=send_sems.at[rel - 1], recv_sem=recv_sems.at[rel - 1],
                device_id=(px, py, pc), device_id_type=MESH),
                pltpu.make_async_remote_copy(
                src_ref=x_ref, dst_ref=land.at[peer], send_sem=send_sems.at[rel - 1], recv_sem=recv_sems.at[rel - 1],
                device_id=(px, py, pc), device_id_type=MESH)))
        for cp, _ in cps:
            cp.start()
        for _, arr in cps:
            arr.wait_recv()
        for cp, _ in cps:
            cp.wait_send()
        acc = land[0]
        for k in range(1, N_DEV):
            acc = acc + land[k]
        o_ref[...] = acc

    vm = pl.BlockSpec(memory_space=pltpu.VMEM)
    return pl.pallas_call(
        body, name="all_reduce_small",
        out_shape=jax.ShapeDtypeStruct((rows, n), F32),
        in_specs=[vm], out_specs=vm,
        scratch_shapes=[pltpu.VMEM((N_DEV, rows, n), F32),
                        pltpu.SemaphoreType.DMA((N_DEV - 1,)), pltpu.SemaphoreType.DMA((N_DEV - 1,))],
    )(sp)


def _adamw_math(wv, gv, mv, vv):
    m2 = ADAM_B1 * mv + (1.0 - ADAM_B1) * gv
    v2 = ADAM_B2 * vv + (1.0 - ADAM_B2) * (gv * gv)
    m_hat = m2 / (1.0 - ADAM_B1 ** ADAM_STEP)
    v_hat = v2 / (1.0 - ADAM_B2 ** ADAM_STEP)
    return -ADAM_LR * (m_hat / (jnp.sqrt(v_hat) + ADAM_EPS) + ADAM_WD * wv), m2, v2


def _adamw(w, g, m, v, name):
    rows, cols = w.shape
    br = rows if rows <= 256 else 256

    def body(w_ref, g_ref, m_ref, v_ref, d_ref, m2_ref, v2_ref):
        d_ref[...], m2_ref[...], v2_ref[...] = _adamw_math(w_ref[...], g_ref[...], m_ref[...], v_ref[...])

    spec = pl.BlockSpec((br, cols), lambda i: (i, 0))
    out = jax.ShapeDtypeStruct((rows, cols), F32)
    return pl.pallas_call(
        body, name=name, grid=(rows // br,),
        in_specs=[spec] * 4, out_specs=[spec] * 3, out_shape=[out] * 3,
        compiler_params=_cp(("arbitrary",)),
    )(w, g, m, v)


def _adamw_reduced(w, land, m, v, name):
    rows, cols = w.shape
    br = rows if rows <= 256 else 256
    nl = land.shape[0]

    def body(w_ref, l_ref, m_ref, v_ref, g_ref, d_ref, m2_ref, v2_ref):
        gv = l_ref[0].astype(F32)
        for q in range(1, nl):
            gv = gv + l_ref[q].astype(F32)
        g_ref[...] = gv
        d_ref[...], m2_ref[...], v2_ref[...] = _adamw_math(w_ref[...], gv, m_ref[...], v_ref[...])

    spec = pl.BlockSpec((br, cols), lambda i: (i, 0))
    out = jax.ShapeDtypeStruct((rows, cols), F32)
    return pl.pallas_call(
        body, name=name, grid=(rows // br,),
        in_specs=[spec, pl.BlockSpec((nl, br, cols), lambda i: (0, i, 0)), spec, spec],
        out_specs=[spec] * 4, out_shape=[out] * 4,
        compiler_params=_cp(("arbitrary",)),
    )(w, land, m, v)


def _zx_source_col(col):
    blk = jnp.right_shift(col, 7)
    lane = jnp.bitwise_and(col, 127)
    per = GROUP_COLS // 128
    grp = jnp.where(blk >= per, 1, 0) + jnp.where(blk >= 2 * per, 1, 0) + jnp.where(blk >= 3 * per, 1, 0)
    o = blk - per * grp
    x_col = D_INNER + GROUP_X * grp + 128 * o + lane
    b_col = 2 * D_INNER + D_STATE * grp + lane
    c_col = 2 * D_INNER + N_GROUPS * D_STATE + D_STATE * grp + lane
    dt_col = jnp.where(lane < HEADS_PER_GROUP, D_INNER + CONV_DIM + HEADS_PER_GROUP * grp + lane, -1)
    src = jnp.where(o < 4, x_col, jnp.where(o == 4, b_col, jnp.where(o == 5, c_col, dt_col)))
    return jnp.where(col >= Z_OFF, col - Z_OFF, src)


def _zx_source_col_py(col):
    if col >= Z_OFF:
        return col - Z_OFF
    grp, o = divmod(col, GROUP_COLS)
    if o < GROUP_X:
        return D_INNER + GROUP_X * grp + o
    if o < GROUP_X + D_STATE:
        return 2 * D_INNER + D_STATE * grp + (o - GROUP_X)
    if o < GROUP_CONV:
        return 2 * D_INNER + N_GROUPS * D_STATE + D_STATE * grp + (o - GROUP_X - D_STATE)
    h = o - GROUP_CONV
    return D_INNER + CONV_DIM + HEADS_PER_GROUP * grp + h if h < HEADS_PER_GROUP else -1


def _overlap_tables():
    nblk = ZX_COLS // COL_BLK
    src = [_zx_source_col_py(c) for c in range(ZX_COLS)]
    fwd = [sorted({s // W_IN_SHARD for s in src[COL_BLK * j:COL_BLK * (j + 1)] if s >= 0}) for j in range(nblk)]
    dst = {s: c for c, s in enumerate(src) if s >= 0}
    bwd = [sorted({dst[s] // COL_BLK for s in range(W_IN_SHARD * k, W_IN_SHARD * (k + 1))}) for k in range(N_DEV)]

    def flat(rows):
        width = max(len(r) for r in rows)
        idx = [r + [r[-1]] * (width - len(r)) for r in rows]
        val = [[1] * len(r) + [0] * (width - len(r)) for r in rows]
        return (jnp.asarray(sum(idx, []), jnp.int32), jnp.asarray(sum(val, []), jnp.int32), width)

    return flat(fwd), flat(bwd)


def _w_in_to_zx(w_in_g):
    (tab, val, width), _ = _overlap_tables()
    nblk = ZX_COLS // COL_BLK

    def body(tab_ref, val_ref, w_ref, o_ref, acc):
        j = pl.program_id(0)
        s = pl.program_id(1)

        @pl.when(s == 0)
        def _():
            acc[...] = jnp.zeros_like(acc)

        @pl.when(val_ref[j * width + s] == 1)
        def _():
            k = tab_ref[j * width + s]
            col = COL_BLK * j + lax.broadcasted_iota(jnp.int32, (8, COL_BLK), 1)
            src = jnp.broadcast_to(_zx_source_col(col)[0:1, :], (W_IN_SHARD, COL_BLK))
            row = W_IN_SHARD * k + lax.broadcasted_iota(jnp.int32, (W_IN_SHARD, COL_BLK), 0)
            place = _one(src == row).astype(BF16)
            acc[...] += jnp.dot(w_ref[...], place, preferred_element_type=F32)

        @pl.when(s == width - 1)
        def _():
            o_ref[...] = acc[...].astype(BF16)

    return pl.pallas_call(
        body, name="w_in_to_zx",
        grid_spec=pltpu.PrefetchScalarGridSpec(
            num_scalar_prefetch=2, grid=(nblk, width),
            in_specs=[pl.BlockSpec((None, D_MODEL, W_IN_SHARD), lambda j, s, tab, val: (tab[j * width + s], 0, 0))],
            out_specs=pl.BlockSpec((D_MODEL, COL_BLK), lambda j, s, tab, val: (0, j)),
            scratch_shapes=[pltpu.VMEM((D_MODEL, COL_BLK), F32)]),
        out_shape=jax.ShapeDtypeStruct((D_MODEL, ZX_COLS), BF16),
        compiler_params=_cp(("arbitrary", "arbitrary")),
    )(tab, val, w_in_g)


def _zx_to_w_in(d_wzx):
    _, (tab, val, width) = _overlap_tables()

    def body(tab_ref, val_ref, d_ref, o_ref, acc):
        k = pl.program_id(0)
        s = pl.program_id(1)

        @pl.when(s == 0)
        def _():
            acc[...] = jnp.zeros_like(acc)

        @pl.when(val_ref[k * width + s] == 1)
        def _():
            j = tab_ref[k * width + s]
            col = COL_BLK * j + lax.broadcasted_iota(jnp.int32, (COL_BLK, 128), 0)
            src = jnp.broadcast_to(_zx_source_col(col)[:, 0:1], (COL_BLK, W_IN_SHARD))
            row = W_IN_SHARD * k + lax.broadcasted_iota(jnp.int32, (COL_BLK, W_IN_SHARD), 1)
            place = _one(src == row).astype(BF16)
            acc[...] += jnp.dot(d_ref[...], place, preferred_element_type=F32)

        @pl.when(s == width - 1)
        def _():
            o_ref[...] = acc[...].astype(BF16)

    return pl.pallas_call(
        body, name="zx_to_w_in",
        grid_spec=pltpu.PrefetchScalarGridSpec(
            num_scalar_prefetch=2, grid=(N_DEV, width),
            in_specs=[pl.BlockSpec((D_MODEL, COL_BLK), lambda k, s, tab, val: (0, tab[k * width + s]))],
            out_specs=pl.BlockSpec((None, D_MODEL, W_IN_SHARD), lambda k, s, tab, val: (k, 0, 0)),
            scratch_shapes=[pltpu.VMEM((D_MODEL, W_IN_SHARD), F32)]),
        out_shape=jax.ShapeDtypeStruct((N_DEV, D_MODEL, W_IN_SHARD), BF16),
        compiler_params=_cp(("arbitrary", "arbitrary")),
    )(tab, val, d_wzx)


def _group_conv_cols(a):
    rows = a.shape[0]
    x = a[:, :D_INNER].reshape(rows, N_GROUPS, GROUP_X)
    b = a[:, D_INNER:D_INNER + N_GROUPS * D_STATE].reshape(rows, N_GROUPS, D_STATE)
    c = a[:, D_INNER + N_GROUPS * D_STATE:].reshape(rows, N_GROUPS, D_STATE)
    return jnp.concatenate([x, b, c], axis=2).reshape(rows, N_GROUPS * GROUP_CONV)


def _ungroup_conv_cols(a):
    rows = a.shape[0]
    a3 = a.reshape(rows, N_GROUPS, GROUP_CONV)
    return jnp.concatenate([a3[:, :, :GROUP_X].reshape(rows, D_INNER),
                            a3[:, :, GROUP_X:GROUP_X + D_STATE].reshape(rows, N_GROUPS * D_STATE),
                            a3[:, :, GROUP_X + D_STATE:].reshape(rows, N_GROUPS * D_STATE)], axis=1)


def _small_shard(conv_w, conv_b, norm_g):
    ng = jnp.pad(norm_g.reshape(1, -1), ((0, 0), (0, CONV_SHARD - norm_g.shape[-1])))
    return jnp.concatenate([conv_w.reshape(4, CONV_SHARD), conv_b.reshape(1, CONV_SHARD), ng,
                            jnp.zeros((SMALL_ROWS - 6, CONV_SHARD), F32)], axis=0)


def _small_unshard(a):
    return a[0:4].reshape(1, 4, CONV_SHARD), a[4:5], a[5:6, :D_INNER // N_DEV]


def _heads_of(a):
    return a[:, :, :HEADS_PER_GROUP].reshape(1, N_HEADS)


def _head_params(p):
    return jnp.pad(p.reshape(N_GROUPS, 1, HEADS_PER_GROUP), ((0, 0), (0, 0), (0, 128 - HEADS_PER_GROUP)))


def _update(w, land, m, v, name):
    shp = w.shape
    to2 = lambda a: a.reshape(-1, shp[-1])
    return tuple(o.reshape(shp) for o in _adamw_reduced(to2(w), land, to2(m), to2(v), name))


def kernel(x, norm_mix_g, norm_mlp_g, pool_w, pool_b, pool_scale, ssm_w_in, ssm_conv_w, ssm_conv_b, ssm_dt_bias, ssm_a_log, ssm_d, ssm_norm_g, ssm_w_out, mlp_w1, mlp_w2, final_g, loss_target, m_norm_mix_g, m_norm_mlp_g, m_pool_w, m_pool_b, m_pool_scale, m_ssm_w_in, m_ssm_conv_w, m_ssm_conv_b, m_ssm_dt_bias, m_ssm_a_log, m_ssm_d, m_ssm_norm_g, m_ssm_w_out, m_mlp_w1, m_mlp_w2, m_final_g, v_norm_mix_g, v_norm_mlp_g, v_pool_w, v_pool_b, v_pool_scale, v_ssm_w_in, v_ssm_conv_w, v_ssm_conv_b, v_ssm_dt_bias, v_ssm_a_log, v_ssm_d, v_ssm_norm_g, v_ssm_w_out, v_mlp_w1, v_mlp_w2, v_final_g):
    x2 = x[0]
    tgt = loss_target[0]
    gm0, gm1 = norm_mix_g[0:1], norm_mix_g[1:2]
    gl0, gl1 = norm_mlp_g[0:1], norm_mlp_g[1:2]
    gfin = final_g.reshape(1, D_MODEL)

    fb = D_FF // N_DEV

    def bf(a):
        return a.astype(BF16)

    def gather_of(shards):
        return _direct_exchange(shards, [(i, 0) for i in range(len(shards))],
                                [(s.shape, s.dtype) for s in shards], scatter=False)

    def scatter_of(parts, rows=None):
        shapes = [((p.shape[1] if rows is None else rows[1], p.shape[2]), p.dtype) for p in parts]
        return _direct_exchange(parts, [(i, 0) for i in range(len(parts))], shapes, scatter=True,
                                src_rows=None if rows is None else [rows] * len(parts))

    w_pool, small_g = _run_exchange(_two_level_gather(
        [bf(pool_w.reshape(4 * POOL_SHARD, POOL_GROUP)), _small_shard(ssm_conv_w, ssm_conv_b, ssm_norm_g)]),
        "gather_first")
    conv_w = _group_conv_cols(small_g[:, 0:4].transpose(1, 0, 2).reshape(4, CONV_DIM))
    conv_b = _group_conv_cols(small_g[:, 4].reshape(1, CONV_DIM))
    ssm_ng = small_g[:, 5, :D_INNER // N_DEV].reshape(1, D_INNER)
    dtb, alog, dsk = _head_params(ssm_dt_bias), _head_params(ssm_a_log), _head_params(ssm_d)

    (h1,), (w1g0, w2g0) = _pool_fwd(x2, gm0, w_pool, pool_b, pool_scale,
                                    carried=_two_level_gather([bf(mlp_w1[0]), bf(mlp_w2[0])], mid_percent=100))
    (h2, u0, hm0), (w_in_g,) = _mlp_fwd(h1, gl0, w1g0, w2g0, "mlp0_fwd",
                                        carried=_two_level_gather([bf(ssm_w_in[0])]))
    w_zx = _w_in_to_zx(w_in_g)
    (zx, hn1), (w_out_g,) = _norm_matmul(h2, gm1, w_zx, carried=gather_of([bf(ssm_w_out[0])]))
    (y_ssd, states), (w1g1, w2g1) = _ssd_fwd(zx, conv_w, conv_b, dtb, alog, dsk,
                                             carried=_two_level_gather([bf(mlp_w1[1]), bf(mlp_w2[1])]))
    w_out = w_out_g.reshape(D_INNER, D_MODEL)
    h3 = _ssm_out_fwd(y_ssd, zx, ssm_ng, w_out, h2)
    (h4, u1, hm1), _ = _mlp_fwd(h3, gl1, w1g1, w2g1, "mlp1_fwd")
    dh4, dh4b, loss_row, d_gfin = _final(h4, gfin, tgt)

    (dh3, dh3b, da1, d_gl1), _ = _mlp_bwd(dh4, dh4b, h3, gl1, u1, w1g1, w2g1, "mlp1_bwd")
    d_w1_1 = _matmul_tn(hm1, da1, "mlp1_dw1", col_blocked=True, scale=2.0)
    d_w2_1 = _matmul_tn(u1, dh4b, "mlp1_dw2", square_a=True).reshape(N_DEV, fb, D_MODEL)
    dy_ssd, dzx, yn, d_ng = _ssm_out_bwd(dh3b, y_ssd, zx, ssm_ng, w_out)
    d_wout = _matmul_tn(yn, dh3b, "ssm_dw_out").reshape(N_DEV, D_INNER // N_DEV, D_MODEL)
    (dzx, d_cw, d_cb, d_dtb, d_alog, d_dsk), (l_w1_1, l_w2_1, l_wout) = _ssd_bwd(
        zx, conv_w, conv_b, dtb, alog, dsk, states, dy_ssd, dzx, carried=scatter_of([d_w1_1, d_w2_1, d_wout]))
    d_w_in = _zx_to_w_in(_matmul_tn(hn1, dzx, "ssm_dw_in"))
    most = 5 * D_MODEL // 8
    (dh2, dh2b, d_gm1), (l_w_in_a,) = _in_proj_bwd(dzx, w_zx, h2, gm1, dh3, carried=scatter_of([d_w_in], (0, most)))
    d_w2_0, (l_w_in_b,) = _matmul_tn(u0, dh2b, "mlp0_dw2", square_a=True,
                                     carried=scatter_of([d_w_in], (most, D_MODEL - most)))
    d_w2_0 = d_w2_0.reshape(N_DEV, fb, D_MODEL)
    (dh1, _, da0, d_gl0), (l_w2_0,) = _mlp_bwd(dh2, dh2b, h1, gl0, u0, w1g0, w2g0, "mlp0_bwd",
                                           carried=scatter_of([d_w2_0]))
    d_w1_0 = _matmul_tn(hm0, da0, "mlp0_dw1", col_blocked=True, scale=2.0)
    (dx, d_pool, d_pb, d_ps, d_gm0), (l_w1_0,) = _pool_bwd(x2, dh1, gm0, w_pool, pool_b, pool_scale,
                                                          carried=scatter_of([d_w1_0]))

    d_conv_w = _ungroup_conv_cols(d_cw).reshape(4, N_DEV, CONV_SHARD).transpose(1, 0, 2)
    d_conv_b = _ungroup_conv_cols(d_cb).reshape(N_DEV, 1, CONV_SHARD)
    d_gain = jnp.pad(d_ng.reshape(N_DEV, 1, D_INNER // N_DEV), ((0, 0), (0, 0), (0, CONV_SHARD - D_INNER // N_DEV)))
    d_small = jnp.concatenate([d_conv_w, d_conv_b, d_gain,
                               jnp.zeros((N_DEV, SMALL_ROWS - 6, CONV_SHARD), F32)], axis=1)
    l_pool, l_small = _run_exchange(scatter_of([bf(d_pool), d_small]), "reduce_scatter_tail")

    heads = jnp.concatenate([_heads_of(a) for a in (d_dtb, d_alog, d_dsk)], axis=1)
    sp = jnp.concatenate([d_gm0, d_gm1, d_gl0, d_gl1, d_pb, d_ps, d_gfin,
                          jnp.pad(heads, ((0, 0), (0, D_MODEL - 3 * N_HEADS)))], axis=0)
    sg = _all_reduce_small(sp)

    g_norm_mix = sg[0:2]
    g_norm_mlp = sg[2:4]
    g_pool_b, g_pool_scale = sg[4:5], sg[5:6]
    g_final = sg[6]
    g_dtb, g_alog, g_dsk = sg[7:8, 0:32], sg[7:8, 32:64], sg[7:8, 64:96]

    def rep_pack(nm, nl, pb, ps, fg, db, al, dk):
        hd = jnp.pad(jnp.concatenate([db, al, dk], axis=1), ((0, 0), (0, D_MODEL - 3 * N_HEADS)))
        return jnp.concatenate([nm, nl, pb, ps, fg.reshape(1, D_MODEL), hd], axis=0)

    rep = [rep_pack(*t) for t in (
        (norm_mix_g, norm_mlp_g, pool_b, pool_scale, final_g, ssm_dt_bias, ssm_a_log, ssm_d),
        (g_norm_mix, g_norm_mlp, g_pool_b, g_pool_scale, g_final, g_dtb, g_alog, g_dsk),
        (m_norm_mix_g, m_norm_mlp_g, m_pool_b, m_pool_scale, m_final_g, m_ssm_dt_bias, m_ssm_a_log, m_ssm_d),
        (v_norm_mix_g, v_norm_mlp_g, v_pool_b, v_pool_scale, v_final_g, v_ssm_dt_bias, v_ssm_a_log, v_ssm_d))]
    rep_out = _adamw(*rep, "adamw_replicated")

    def rep_unpack(a):
        return (a[0:2], a[2:4], a[4:5], a[5:6], a[6], a[7:8, 0:32], a[7:8, 32:64], a[7:8, 64:96])

    sm_out = _adamw_reduced(_small_shard(ssm_conv_w, ssm_conv_b, ssm_norm_g), l_small,
                            _small_shard(m_ssm_conv_w, m_ssm_conv_b, m_ssm_norm_g),
                            _small_shard(v_ssm_conv_w, v_ssm_conv_b, v_ssm_norm_g), "adamw_small_shards")

    big = {
        "pool_w": _update(pool_w, l_pool, m_pool_w, v_pool_w, "adamw_pool_w"),
        "ssm_w_in": tuple(o.reshape(ssm_w_in.shape) for o in _adamw_reduced_parts(
            ssm_w_in[0], (l_w_in_a, l_w_in_b), m_ssm_w_in[0], v_ssm_w_in[0], "adamw_w_in")),
        "ssm_w_out": _update(ssm_w_out, l_wout, m_ssm_w_out, v_ssm_w_out, "adamw_w_out"),
        "mlp_w1": _adamw_reduced_layers(mlp_w1, (l_w1_0, l_w1_1), m_mlp_w1, v_mlp_w1, "adamw_w1"),
        "mlp_w2": _adamw_reduced_layers(mlp_w2, (l_w2_0, l_w2_1), m_mlp_w2, v_mlp_w2, "adamw_w2"),
    }
    rep_all = (rep[1],) + tuple(rep_out)

    def ordered(kind):
        nm, nl, pb, ps, fg, db, al, dk = rep_unpack(rep_all[kind])
        cw, cb, ng = _small_unshard(sm_out[kind])
        return [nm, nl, big["pool_w"][kind], pb, ps, big["ssm_w_in"][kind], cw, cb, db, al, dk, ng,
                big["ssm_w_out"][kind], big["mlp_w1"][kind], big["mlp_w2"][kind], fg]

    loss = lax.psum(loss_row[0, 0], ("x", "y", "c"))
    return (loss, dx[None], *ordered(0), *ordered(1), *ordered(2), *ordered(3))
```

```python
import functools

import jax
import jax.numpy as jnp
from jax import lax
from jax.experimental import pallas as pl
from jax.experimental.pallas import tpu as pltpu

F32 = jnp.float32
BF16 = jnp.bfloat16
MESH = pl.DeviceIdType.MESH

D_MODEL = 1024
RMS_EPS = 1e-5
POOL_WINDOWS = (2, 4, 8, 16)
POOL_GROUP = 256
POOL_HALO = 16
POOL_SHARD = POOL_GROUP // 8
D_INNER = 2048
HEAD_DIM = 64
N_HEADS = 32
N_GROUPS = 4
HEADS_PER_GROUP = 8
D_STATE = 128
CHUNK = 128
CONV_DIM = 3072
IN_PROJ_DIM = 5152
D_FF = 4096
N_DEV = 8
GROUP_X = HEADS_PER_GROUP * HEAD_DIM
GROUP_CONV = GROUP_X + 2 * D_STATE
GROUP_COLS = GROUP_CONV + 128
Z_OFF = N_GROUPS * GROUP_COLS
ZX_COLS = Z_OFF + D_INNER
COL_BLK = 512
PROJ_BLK = ZX_COLS // 4
W_IN_SHARD = IN_PROJ_DIM // N_DEV

ADAM_LR = 0.001
ADAM_B1 = 0.9
ADAM_B2 = 0.999
ADAM_EPS = 1e-08
ADAM_WD = 0.01
ADAM_STEP = 10

VMEM_LIMIT_V7X = 56 * 1024 * 1024
MID_STEP_PERCENT = 85
TN_TOKENS = 512
TN_ACC_BYTES = 16 * 1024 * 1024
MATMUL_TOKENS = 1024

CONV_SHARD = CONV_DIM // N_DEV
SMALL_ROWS = 8

_NN = (((1,), (0,)), ((), ()))
_NT = (((1,), (1,)), ((), ()))
_TN = (((0,), (0,)), ((), ()))


def _cp(sem):
    return pltpu.CompilerParams(dimension_semantics=sem, vmem_limit_bytes=VMEM_LIMIT_V7X)


_ANY = pl.BlockSpec(memory_space=pl.ANY)


def _place():
    return lax.axis_index("x"), lax.axis_index("y"), lax.axis_index("c")


class _Carried:
    def __init__(self, ins, outs, sems, start, finish, mid=None, mid_percent=None):
        self.ins, self.outs, self.sems = list(ins), list(outs), list(sems)
        self.start, self.mid, self.finish, self.mid_percent = start, mid, finish, mid_percent


def _pcall(body, *, name, grid, in_specs, out_specs, out_shape, sem, args, scratch_shapes=(), carried=None,
           aliases=None):
    in_specs, out_specs, out_shape, scratch = list(in_specs), list(out_specs), list(out_shape), list(scratch_shapes)
    common = dict(name=name, grid=grid, input_output_aliases=aliases or {}, compiler_params=_cp(sem))
    if carried is None:
        res = pl.pallas_call(body, in_specs=in_specs, out_specs=out_specs, out_shape=out_shape,
                             scratch_shapes=scratch, **common)(*args)
        return list(res), []
    n_in, n_out, n_scr = len(in_specs), len(out_specs), len(scratch)
    ci, co = len(carried.ins), len(carried.outs)

    def wrapped(*refs):
        ins, cins = refs[:n_in], refs[n_in:n_in + ci]
        p = n_in + ci
        outs, couts = refs[p:p + n_out], refs[p + n_out:p + n_out + co]
        p += n_out + co
        scr, csems = refs[p:p + n_scr], refs[p + n_scr:]
        ids = [pl.program_id(a) for a in range(len(grid))]
        first = functools.reduce(jnp.logical_and, [i == 0 for i in ids])
        last = functools.reduce(jnp.logical_and, [i == g - 1 for i, g in zip(ids, grid)])

        @pl.when(first)
        def _():
            carried.start(cins, couts, csems)

        if carried.mid is not None:
            step, steps = 0, 1
            for i, g in zip(ids, grid):
                step, steps = step * g + i, steps * g

            @pl.when(step == min(steps - 1, (steps * carried.mid_percent) // 100))
            def _():
                carried.mid(cins, couts, csems)

        body(*ins, *outs, *scr)

        @pl.when(last)
        def _():
            carried.finish(cins, couts, csems)

    res = pl.pallas_call(wrapped, in_specs=in_specs + [_ANY] * ci, out_specs=out_specs + [_ANY] * co,
                         out_shape=out_shape + carried.outs, scratch_shapes=scratch + carried.sems,
                         **common)(*args, *carried.ins)
    return list(res[:n_out]), list(res[n_out:])


def _peers(x, y, c):
    out = []
    for rel in range(1, N_DEV):
        dx, dy, dc = (rel >> 2) & 1, (rel >> 1) & 1, rel & 1
        out.append((x + dx - 2 * x * dx, y + dy - 2 * y * dy, c + dc - 2 * c * dc))
    return out


def _direct_exchange(srcs, layout, out_shapes, scatter, src_rows=None):
    n = len(srcs)

    def copies(ins, outs, sems):
        send, recv, loc = sems
        x, y, c = _place()
        me = 4 * x + 2 * y + c
        out, arrive, local = [], [], []
        for i in range(n):
            j, off = layout[i]
            first, rows = (0, srcs[i].shape[-2]) if src_rows is None else src_rows[i]

            def piece(k):
                return ins[i].at[k, pl.ds(first, rows)] if scatter else ins[i]

            for r, peer in enumerate(_peers(x, y, c)):
                pidx = 4 * peer[0] + 2 * peer[1] + peer[2]
                kw = dict(send_sem=send.at[7 * i + r], recv_sem=recv.at[7 * i + r], device_id=peer, device_id_type=MESH)
                out.append(pltpu.make_async_remote_copy(
                    src_ref=piece(pidx), dst_ref=outs[j].at[me, pl.ds(off, rows)], **kw))
                arrive.append(pltpu.make_async_remote_copy(
                    src_ref=piece(pidx), dst_ref=outs[j].at[pidx, pl.ds(off, rows)], **kw))
            local.append(pltpu.make_async_copy(piece(me), outs[j].at[me, pl.ds(off, rows)], loc.at[i]))
        return out, arrive, local

    def start(ins, outs, sems):
        out, _, local = copies(ins, outs, sems)
        for cp in local + out:
            cp.start()

    def finish(ins, outs, sems):
        out, arrive, local = copies(ins, outs, sems)
        for cp in arrive:
            cp.wait_recv()
        for cp in out:
            cp.wait_send()
        for cp in local:
            cp.wait()

    return _Carried(srcs, [jax.ShapeDtypeStruct((N_DEV,) + tuple(s), d) for s, d in out_shapes],
                    [pltpu.SemaphoreType.DMA((7 * n,)), pltpu.SemaphoreType.DMA((7 * n,)),
                     pltpu.SemaphoreType.DMA((n,))], start, finish)


def _two_level_gather(shards, mid_percent=MID_STEP_PERCENT):
    n = len(shards)

    def copies(ins, outs, sems):
        send, recv, loc = sems
        x, y, c = _place()
        me, sibling = (x, y, c), (x, y, 1 - c)
        chips = [(1 - x, y), (x, 1 - y), (1 - x, 1 - y)]

        def win(i, place):
            return outs[i].at[4 * place[0] + 2 * place[1] + place[2]]

        def copy(i, k, block, to, src=None):
            return pltpu.make_async_remote_copy(
                src_ref=win(i, block) if src is None else src, dst_ref=win(i, block),
                send_sem=send.at[7 * i + k], recv_sem=recv.at[7 * i + k], device_id=to, device_id_type=MESH)

        own, passed, ici_in, d2d_in, local = [], [], [], [], []
        for i in range(n):
            own += [copy(i, 0, me, sibling, src=ins[i])]
            own += [copy(i, 1 + j, me, (*chip, c), src=ins[i]) for j, chip in enumerate(chips)]
            passed += [copy(i, 4 + j, (*chip, c), sibling) for j, chip in enumerate(chips)]
            ici_in += [copy(i, 1 + j, (*chip, c), me) for j, chip in enumerate(chips)]
            d2d_in += [copy(i, 0, sibling, me)] + [copy(i, 4 + j, (*chip, 1 - c), me) for j, chip in enumerate(chips)]
            local.append(pltpu.make_async_copy(ins[i], win(i, me), loc.at[i]))
        return own, passed, ici_in, d2d_in, local

    def start(ins, outs, sems):
        own, _, _, _, local = copies(ins, outs, sems)
        for cp in local + own:
            cp.start()

    def mid(ins, outs, sems):
        _, passed, ici_in, _, _ = copies(ins, outs, sems)
        for arrived, onward in zip(ici_in, passed):
            arrived.wait_recv()
            onward.start()

    def finish(ins, outs, sems):
        own, passed, _, d2d_in, local = copies(ins, outs, sems)
        for cp in d2d_in:
            cp.wait_recv()
        for cp in own + passed:
            cp.wait_send()
        for cp in local:
            cp.wait()

    return _Carried(shards, [jax.ShapeDtypeStruct((N_DEV,) + tuple(s.shape), s.dtype) for s in shards],
                    [pltpu.SemaphoreType.DMA((7 * n,)), pltpu.SemaphoreType.DMA((7 * n,)),
                     pltpu.SemaphoreType.DMA((n,))], start, finish, mid, mid_percent)


def _run_exchange(carried, name):
    ci = len(carried.ins)

    def body(*refs):
        ins, outs, sems = refs[:ci], refs[ci:ci + len(carried.outs)], refs[ci + len(carried.outs):]
        carried.start(ins, outs, sems)
        if carried.mid is not None:
            carried.mid(ins, outs, sems)
        carried.finish(ins, outs, sems)

    return list(pl.pallas_call(body, name=name, in_specs=[_ANY] * ci, out_specs=[_ANY] * len(carried.outs),
                               out_shape=carried.outs, scratch_shapes=carried.sems)(*carried.ins))


def _dg(a, b, dn):
    return lax.dot_general(a.astype(BF16), b.astype(BF16), dn, preferred_element_type=F32)


@jax.custom_vjp
def mm_nn(a, b):
    return _dg(a, b, _NN)


@jax.custom_vjp
def mm_nt(a, b):
    return _dg(a, b, _NT)


@jax.custom_vjp
def mm_tn(a, b):
    return _dg(a, b, _TN)


mm_nn.defvjp(lambda a, b: (_dg(a, b, _NN), (a, b)), lambda r, ct: (mm_nt(ct, r[1]), mm_tn(r[0], ct)))
mm_nt.defvjp(lambda a, b: (_dg(a, b, _NT), (a, b)), lambda r, ct: (mm_nn(ct, r[1]), mm_tn(ct, r[0])))
mm_tn.defvjp(lambda a, b: (_dg(a, b, _TN), (a, b)), lambda r, ct: (mm_nt(r[1], ct), mm_nn(r[0], ct)))


def _split3(x):
    p1 = x.astype(BF16)
    r1 = x - p1.astype(F32)
    p2 = r1.astype(BF16)
    r2 = r1 - p2.astype(F32)
    return p1, p2, r2.astype(BF16)


def _exact01(x, c, dn, const_left):
    acc = None
    for p in reversed(_split3(x)):
        t = (lax.dot_general(c, p, dn, preferred_element_type=F32) if const_left
             else lax.dot_general(p, c, dn, preferred_element_type=F32))
        acc = t if acc is None else acc + t
    return acc


def _make_cmm(dn, const_left, bwd_name):
    @jax.custom_vjp
    def f(x, c):
        return _exact01(x, c, dn, const_left)

    def fwd(x, c):
        return _exact01(x, c, dn, const_left), c

    def bwd(c, ct):
        return _CMM[bwd_name](ct, c), jnp.zeros_like(c)

    f.defvjp(fwd, bwd)
    return f


_CMM = {}
_CMM["xc"] = _make_cmm(_NN, False, "xct")
_CMM["xct"] = _make_cmm(_NT, False, "xc")
_CMM["cx"] = _make_cmm(_NN, True, "ctx")
_CMM["ctx"] = _make_cmm(_TN, True, "cx")


def _sigmoid(x):
    return 0.5 * jnp.tanh(0.5 * x) + 0.5


@jax.custom_vjp
def _silu(x):
    return x * _sigmoid(x)


def _silu_fwd(x):
    return _silu(x), x


def _silu_bwd(x, ct):
    s = _sigmoid(x)
    return (ct * (s * (1.0 + x * (1.0 - s))),)


_silu.defvjp(_silu_fwd, _silu_bwd)


def _log1p_pos(e):
    u = 1.0 + e
    d = u - 1.0
    return jnp.where(d == 0.0, e, jnp.log(u) * (e / jnp.where(d == 0.0, 1.0, d)))


@jax.custom_vjp
def _softplus(x):
    return jnp.maximum(x, 0.0) + _log1p_pos(jnp.exp(-jnp.abs(x)))


def _softplus_fwd(x):
    return _softplus(x), x


def _softplus_bwd(x, ct):
    return (ct * _sigmoid(x),)


_softplus.defvjp(_softplus_fwd, _softplus_bwd)


CONV_HALO = 8


def _make_shift(j):
    @jax.custom_vjp
    def f(ext):
        return pltpu.roll(ext, j, 0)[CONV_HALO:, :]

    def fwd(ext):
        return f(ext), None

    def bwd(_, ct):
        pad = jnp.concatenate([jnp.zeros((CONV_HALO, ct.shape[1]), ct.dtype), ct], axis=0)
        return (pltpu.roll(pad, CONV_HALO + CHUNK - j, 0),)

    f.defvjp(fwd, bwd)
    return f


_SHIFT = {j: _make_shift(j) for j in (1, 2, 3)}


@jax.custom_vjp
def _swap_halves(x):
    return pltpu.roll(x, HEAD_DIM, 1)


_swap_halves.defvjp(lambda x: (_swap_halves(x), None), lambda _, ct: (pltpu.roll(ct, HEAD_DIM, 1),))


def _rms_fwd(x, g):
    r = lax.rsqrt(jnp.mean(x * x, axis=-1, keepdims=True) + RMS_EPS)
    n = x * r
    return n * g, n, r


def _rms_bwd(dy, n, r, g):
    dn = dy * g
    dx = r * (dn - n * jnp.mean(dn * n, axis=-1, keepdims=True))
    dg = jnp.sum(dy * n, axis=0, keepdims=True)
    return dx, dg


def _one(cond):
    return jnp.where(cond, 1.0, 0.0)


def _pool_tile(xe, g, ws, b, scale, tile, tt):
    r = lax.rsqrt(jnp.mean(xe * xe, axis=-1, keepdims=True) + RMS_EPS)
    hn = xe * r * g
    row_e = lax.broadcasted_iota(jnp.int32, (tt + POOL_HALO, POOL_GROUP), 0)
    keep = _one(jnp.logical_or(row_e >= POOL_HALO, tile > 0))
    rr = lax.broadcasted_iota(jnp.int32, (tt, tt + POOL_HALO), 0)
    qq = lax.broadcasted_iota(jnp.int32, (tt, tt + POOL_HALO), 1)
    dd = qq - rr
    tpos = tile * tt + lax.broadcasted_iota(jnp.int32, (tt, POOL_GROUP), 0)
    outs = []
    for gi, w in enumerate(POOL_WINDOWS):
        hg = hn[:, gi * POOL_GROUP:(gi + 1) * POOL_GROUP] * keep
        band = _one(jnp.logical_and(dd >= POOL_HALO - w + 1, dd <= POOL_HALO)).astype(BF16)
        cnt = jnp.minimum(tpos + 1, w).astype(F32)
        pooled = _CMM["cx"](hg, band) / cnt
        mixed = pooled - hg[POOL_HALO:, :]
        outs.append(mm_nn(mixed, ws[gi]))
    out = (jnp.concatenate(outs, axis=1) + b) * scale
    return xe[POOL_HALO:, :] + out


def _pool_specs(tt, nt, rev):
    per = tt // POOL_HALO
    t_of = (lambda i: nt - 1 - i) if rev else (lambda i: i)
    main = pl.BlockSpec((tt, D_MODEL), lambda i: (t_of(i), 0))
    halo = pl.BlockSpec((POOL_HALO, D_MODEL), lambda i: (jnp.maximum(t_of(i) * per - 1, 0), 0))
    vec = pl.BlockSpec((1, D_MODEL), lambda i: (0, 0))
    wsp = pl.BlockSpec((N_DEV, 4 * POOL_SHARD, POOL_GROUP), lambda i: (0, 0, 0))
    return main, halo, vec, wsp


def _pool_weights(w_ref):
    return tuple(
        jnp.concatenate([w_ref[k, gi * POOL_SHARD:(gi + 1) * POOL_SHARD, :] for k in range(N_DEV)], axis=0).astype(F32)
        for gi in range(4))


def _pool_fwd(x, g, w, b, scale, carried=None):
    t = x.shape[0]
    tt = min(t, 256)
    nt = t // tt
    main, halo, vec, wsp = _pool_specs(tt, nt, False)

    def body(xm_ref, xh_ref, g_ref, w_ref, b_ref, s_ref, o_ref):
        i = pl.program_id(0)
        xe = jnp.concatenate([xh_ref[...], xm_ref[...]], axis=0)
        o_ref[...] = _pool_tile(xe, g_ref[...], _pool_weights(w_ref), b_ref[...], s_ref[...], i, tt)

    return _pcall(
        body, name="pool_fwd", grid=(nt,),
        in_specs=[main, halo, vec, wsp, vec, vec], out_specs=[main],
        out_shape=[jax.ShapeDtypeStruct((t, D_MODEL), F32)],
        sem=("arbitrary",), args=(x, x, g, w, b, scale), carried=carried)


def _pool_bwd(x, dh, g, w, b, scale, carried=None):
    t = x.shape[0]
    tt = min(t, 256)
    nt = t // tt
    main, halo, vec, wsp = _pool_specs(tt, nt, True)

    def body(xm_ref, xh_ref, dh_ref, g_ref, w_ref, b_ref, s_ref,
             dx_ref, dw_ref, db_ref, ds_ref, dg_ref, carry, dw_acc):
        i = pl.program_id(0)
        tile = nt - 1 - i

        @pl.when(i == 0)
        def _():
            carry[...] = jnp.zeros_like(carry)
            dw_acc[...] = jnp.zeros_like(dw_acc)
            db_ref[...] = jnp.zeros_like(db_ref)
            ds_ref[...] = jnp.zeros_like(ds_ref)
            dg_ref[...] = jnp.zeros_like(dg_ref)

        xe = jnp.concatenate([xh_ref[...], xm_ref[...]], axis=0)
        _, vjp = jax.vjp(lambda a, gg, ww, bb, ss: _pool_tile(a, gg, ww, bb, ss, tile, tt),
                         xe, g_ref[...], _pool_weights(w_ref), b_ref[...], s_ref[...])
        dxe, dgv, dws, dbv, dsv = vjp(dh_ref[...])
        dx_ref[...] = dxe[POOL_HALO:, :]
        dx_ref[tt - POOL_HALO:tt, :] += carry[...]
        carry[...] = dxe[:POOL_HALO, :]
        for gi in range(4):
            dw_acc[gi] += dws[gi]
        db_ref[...] += dbv
        ds_ref[...] += dsv
        dg_ref[...] += dgv

        @pl.when(i == nt - 1)
        def _():
            for k in range(N_DEV):
                for gi in range(4):
                    dw_ref[k, gi * POOL_SHARD:(gi + 1) * POOL_SHARD, :] = dw_acc[gi, k * POOL_SHARD:(k + 1) * POOL_SHARD, :]

    return _pcall(
        body, name="pool_bwd", grid=(nt,),
        in_specs=[main, halo, main, vec, wsp, vec, vec],
        out_specs=[main, wsp, vec, vec, vec],
        out_shape=[jax.ShapeDtypeStruct((t, D_MODEL), F32),
                   jax.ShapeDtypeStruct((N_DEV, 4 * POOL_SHARD, POOL_GROUP), F32),
                   jax.ShapeDtypeStruct((1, D_MODEL), F32),
                   jax.ShapeDtypeStruct((1, D_MODEL), F32),
                   jax.ShapeDtypeStruct((1, D_MODEL), F32)],
        scratch_shapes=[pltpu.VMEM((POOL_HALO, D_MODEL), F32), pltpu.VMEM((4, POOL_GROUP, POOL_GROUP), F32)],
        sem=("arbitrary",), args=(x, x, dh, g, w, b, scale), carried=carried)


def _mlp_weight_specs():
    fb = D_FF // N_DEV
    return (pl.BlockSpec((None, D_MODEL, fb), lambda i, k: (k, 0, 0)),
            pl.BlockSpec((None, fb, D_MODEL), lambda i, k: (k, 0, 0)))


def _mlp_fwd(h, g, w1g, w2g, name, carried=None):
    t = h.shape[0]
    tt = min(t, MATMUL_TOKENS)
    nk, fb = N_DEV, D_FF // N_DEV
    w1_spec, w2_spec = _mlp_weight_specs()

    def body(h_ref, g_ref, w1_ref, w2_ref, o_ref, u_ref, hm_ref, hm_s, acc_s):
        k = pl.program_id(1)

        @pl.when(k == 0)
        def _():
            xv = h_ref[...]
            y, _, _ = _rms_fwd(xv, g_ref[...])
            hb = y.astype(BF16)
            hm_s[...] = hb
            hm_ref[...] = hb
            acc_s[...] = xv

        a = jnp.dot(hm_s[...], w1_ref[...], preferred_element_type=F32)
        u = jnp.maximum(a, 0.0)
        u_ref[...] = u.astype(BF16)
        acc_s[...] += jnp.dot((u * u).astype(BF16), w2_ref[...], preferred_element_type=F32)

        @pl.when(k == nk - 1)
        def _():
            o_ref[...] = acc_s[...]

    return _pcall(
        body, name=name, grid=(t // tt, nk),
        in_specs=[pl.BlockSpec((tt, D_MODEL), lambda i, k: (i, 0)),
                  pl.BlockSpec((1, D_MODEL), lambda i, k: (0, 0)),
                  w1_spec, w2_spec],
        out_specs=[pl.BlockSpec((tt, D_MODEL), lambda i, k: (i, 0)),
                   pl.BlockSpec((tt, fb), lambda i, k: (i, k)),
                   pl.BlockSpec((tt, D_MODEL), lambda i, k: (i, 0))],
        out_shape=[jax.ShapeDtypeStruct((t, D_MODEL), F32),
                   jax.ShapeDtypeStruct((t, nk * fb), BF16),
                   jax.ShapeDtypeStruct((t, D_MODEL), BF16)],
        scratch_shapes=[pltpu.VMEM((tt, D_MODEL), BF16), pltpu.VMEM((tt, D_MODEL), F32)],
        sem=("arbitrary", "arbitrary"), args=(h, g, w1g, w2g), carried=carried)


def _mlp_fwd_up(h, g, w1g, name, carried=None):
    t = h.shape[0]
    tt = min(t, MATMUL_TOKENS)
    nk, fb = N_DEV, D_FF // N_DEV
    w1_spec, _ = _mlp_weight_specs()

    def body(h_ref, g_ref, w1_ref, u_ref, v_ref, hm_ref, hm_s):
        @pl.when(pl.program_id(1) == 0)
        def _():
            y, _, _ = _rms_fwd(h_ref[...], g_ref[...])
            hb = y.astype(BF16)
            hm_s[...] = hb
            hm_ref[...] = hb

        u = jnp.maximum(jnp.dot(hm_s[...], w1_ref[...], preferred_element_type=F32), 0.0)
        u_ref[...] = u.astype(BF16)
        v_ref[...] = (u * u).astype(BF16)

    tile = pl.BlockSpec((tt, D_MODEL), lambda i, k: (i, 0))
    blk = pl.BlockSpec((tt, fb), lambda i, k: (i, k))
    return _pcall(
        body, name=name, grid=(t // tt, nk),
        in_specs=[tile, pl.BlockSpec((1, D_MODEL), lambda i, k: (0, 0)), w1_spec],
        out_specs=[blk, blk, tile],
        out_shape=[jax.ShapeDtypeStruct((t, nk * fb), BF16), jax.ShapeDtypeStruct((t, nk * fb), BF16),
                   jax.ShapeDtypeStruct((t, D_MODEL), BF16)],
        scratch_shapes=[pltpu.VMEM((tt, D_MODEL), BF16)],
        sem=("arbitrary", "arbitrary"), args=(h, g, w1g), carried=carried)


def _mlp_fwd_down(h, v, w2g, name, carried=None):
    t = h.shape[0]
    tt = min(t, MATMUL_TOKENS)
    nk, fb = N_DEV, D_FF // N_DEV
    _, w2_spec = _mlp_weight_specs()

    def body(h_ref, v_ref, w2_ref, o_ref, acc_s):
        k = pl.program_id(1)

        @pl.when(k == 0)
        def _():
            acc_s[...] = h_ref[...]

        acc_s[...] += jnp.dot(v_ref[...], w2_ref[...], preferred_element_type=F32)

        @pl.when(k == nk - 1)
        def _():
            o_ref[...] = acc_s[...]

    tile = pl.BlockSpec((tt, D_MODEL), lambda i, k: (i, 0))
    return _pcall(
        body, name=name, grid=(t // tt, nk),
        in_specs=[tile, pl.BlockSpec((tt, fb), lambda i, k: (i, k)), w2_spec],
        out_specs=[tile], out_shape=[jax.ShapeDtypeStruct((t, D_MODEL), F32)],
        scratch_shapes=[pltpu.VMEM((tt, D_MODEL), F32)],
        sem=("arbitrary", "arbitrary"), args=(h, v, w2g), carried=carried)


def _mlp_bwd(dh, dhb, h, g, u, w1g, w2g, name, carried=None):
    t = h.shape[0]
    tt = min(t, MATMUL_TOKENS)
    nk, fb = N_DEV, D_FF // N_DEV
    w1_spec, w2_spec = _mlp_weight_specs()

    def body(dh_ref, dhb_ref, h_ref, g_ref, u_ref, w1_ref, w2_ref,
             dhin_ref, dhinb_ref, da_ref, dg_ref, acc_s):
        i = pl.program_id(0)
        k = pl.program_id(1)

        @pl.when(jnp.logical_and(i == 0, k == 0))
        def _():
            dg_ref[...] = jnp.zeros_like(dg_ref)

        @pl.when(k == 0)
        def _():
            acc_s[...] = jnp.zeros_like(acc_s)

        dv = lax.dot_general(dhb_ref[...], w2_ref[...], _NT, preferred_element_type=F32)
        dab = (dv * u_ref[...].astype(F32)).astype(BF16)
        da_ref[...] = dab
        acc_s[...] += lax.dot_general(dab, w1_ref[...], _NT, preferred_element_type=F32)

        @pl.when(k == nk - 1)
        def _():
            gv = g_ref[...]
            _, n, r = _rms_fwd(h_ref[...], gv)
            dx, dg = _rms_bwd(2.0 * acc_s[...], n, r, gv)
            dhin = dh_ref[...] + dx
            dhin_ref[...] = dhin
            dhinb_ref[...] = dhin.astype(BF16)
            dg_ref[...] += dg

    tile = pl.BlockSpec((tt, D_MODEL), lambda i, k: (i, 0))
    return _pcall(
        body, name=name, grid=(t // tt, nk),
        in_specs=[tile, tile, tile, pl.BlockSpec((1, D_MODEL), lambda i, k: (0, 0)),
                  pl.BlockSpec((tt, fb), lambda i, k: (i, k)), w1_spec, w2_spec],
        out_specs=[tile, tile, pl.BlockSpec((tt, fb), lambda i, k: (i, k)),
                   pl.BlockSpec((1, D_MODEL), lambda i, k: (0, 0))],
        out_shape=[jax.ShapeDtypeStruct((t, D_MODEL), F32),
                   jax.ShapeDtypeStruct((t, D_MODEL), BF16),
                   jax.ShapeDtypeStruct((t, nk * fb), BF16),
                   jax.ShapeDtypeStruct((1, D_MODEL), F32)],
        scratch_shapes=[pltpu.VMEM((tt, D_MODEL), F32)],
        sem=("arbitrary", "arbitrary"), args=(dh, dhb, h, g, u, w1g, w2g), carried=carried)


def _matmul_tn(a, b, name, square_a=False, col_blocked=False, carried=None, scale=None):
    t, k1 = a.shape
    k2 = b.shape[1]
    tt = min(t, TN_TOKENS)
    nt = t // tt
    wc = k2 if k1 * k2 * 4 <= TN_ACC_BYTES else k2 // 2
    nb = wc // COL_BLK

    def body(a_ref, b_ref, o_ref, acc):
        s = pl.program_id(1)

        @pl.when(s == 0)
        def _():
            acc[...] = jnp.zeros_like(acc)

        av = a_ref[...]
        if square_a:
            af = av.astype(F32)
            av = (af * af).astype(BF16)
        acc[...] += lax.dot_general(av, b_ref[...], _TN, preferred_element_type=F32)

        @pl.when(s == nt - 1)
        def _():
            def done(v):
                return (v if scale is None else scale * v).astype(o_ref.dtype)

            if col_blocked:
                for k in range(nb):
                    o_ref[k] = done(acc[:, k * COL_BLK:(k + 1) * COL_BLK])
            else:
                o_ref[...] = done(acc[...])

    if col_blocked:
        out_shape = jax.ShapeDtypeStruct((k2 // COL_BLK, k1, COL_BLK), BF16)
        out_spec = pl.BlockSpec((nb, k1, COL_BLK), lambda j, s: (j, 0, 0))
    else:
        out_shape = jax.ShapeDtypeStruct((k1, k2), BF16)
        out_spec = pl.BlockSpec((k1, wc), lambda j, s: (0, j))
    outs, landed = _pcall(
        body, name=name, grid=(k2 // wc, nt),
        in_specs=[pl.BlockSpec((tt, k1), lambda j, s: (s, 0)),
                  pl.BlockSpec((tt, wc), lambda j, s: (s, j))],
        out_specs=[out_spec], out_shape=[out_shape],
        scratch_shapes=[pltpu.VMEM((k1, wc), F32)],
        sem=("arbitrary", "arbitrary"), args=(a, b), carried=carried)
    return (outs[0], landed) if carried is not None else outs[0]


def _norm_matmul(h, g, w, carried=None):
    t = h.shape[0]
    tt = min(t, MATMUL_TOKENS)
    n = w.shape[1]

    def body(h_ref, g_ref, w_ref, o_ref, hn_ref, hn_s):
        @pl.when(pl.program_id(1) == 0)
        def _():
            y, _, _ = _rms_fwd(h_ref[...], g_ref[...])
            hb = y.astype(BF16)
            hn_s[...] = hb
            hn_ref[...] = hb

        o_ref[...] = jnp.dot(hn_s[...], w_ref[...], preferred_element_type=F32)

    return _pcall(
        body, name="ssm_in_proj", grid=(t // tt, n // PROJ_BLK),
        in_specs=[pl.BlockSpec((tt, D_MODEL), lambda i, j: (i, 0)),
                  pl.BlockSpec((1, D_MODEL), lambda i, j: (0, 0)),
                  pl.BlockSpec((D_MODEL, PROJ_BLK), lambda i, j: (0, j))],
        out_specs=[pl.BlockSpec((tt, PROJ_BLK), lambda i, j: (i, j)),
                   pl.BlockSpec((tt, D_MODEL), lambda i, j: (i, 0))],
        out_shape=[jax.ShapeDtypeStruct((t, n), F32), jax.ShapeDtypeStruct((t, D_MODEL), BF16)],
        scratch_shapes=[pltpu.VMEM((tt, D_MODEL), BF16)],
        sem=("arbitrary", "arbitrary"), args=(h, g, w), carried=carried)


def _in_proj_bwd(dzx, w, h, g, dh_next, carried=None):
    t = h.shape[0]
    tt = min(t, MATMUL_TOKENS)
    n = w.shape[1]
    nj = n // PROJ_BLK

    def body(dz_ref, w_ref, h_ref, g_ref, dn_ref, dh_ref, dhb_ref, dg_ref, acc):
        i = pl.program_id(0)
        j = pl.program_id(1)

        @pl.when(jnp.logical_and(i == 0, j == 0))
        def _():
            dg_ref[...] = jnp.zeros_like(dg_ref)

        @pl.when(j == 0)
        def _():
            acc[...] = jnp.zeros_like(acc)

        acc[...] += lax.dot_general(dz_ref[...], w_ref[...], _NT, preferred_element_type=F32)

        @pl.when(j == nj - 1)
        def _():
            gv = g_ref[...]
            _, nn, r = _rms_fwd(h_ref[...], gv)
            dx, dg = _rms_bwd(acc[...], nn, r, gv)
            dh = dn_ref[...] + dx
            dh_ref[...] = dh
            dhb_ref[...] = dh.astype(BF16)
            dg_ref[...] += dg

    tile = pl.BlockSpec((tt, D_MODEL), lambda i, j: (i, 0))
    return _pcall(
        body, name="ssm_in_proj_bwd", grid=(t // tt, nj),
        in_specs=[pl.BlockSpec((tt, PROJ_BLK), lambda i, j: (i, j)),
                  pl.BlockSpec((D_MODEL, PROJ_BLK), lambda i, j: (0, j)),
                  tile, pl.BlockSpec((1, D_MODEL), lambda i, j: (0, 0)), tile],
        out_specs=[tile, tile, pl.BlockSpec((1, D_MODEL), lambda i, j: (0, 0))],
        out_shape=[jax.ShapeDtypeStruct((t, D_MODEL), F32), jax.ShapeDtypeStruct((t, D_MODEL), BF16),
                   jax.ShapeDtypeStruct((1, D_MODEL), F32)],
        scratch_shapes=[pltpu.VMEM((tt, D_MODEL), F32)],
        sem=("arbitrary", "arbitrary"), args=(dzx, w, h, g, dh_next), carried=carried)


def _ssd_consts():
    lane = lax.broadcasted_iota(jnp.int32, (CHUNK, CHUNK), 1)
    row = lax.broadcasted_iota(jnp.int32, (CHUNK, CHUNK), 0)
    causal = lane <= row
    tri = _one(causal).astype(BF16)
    er = lax.broadcasted_iota(jnp.int32, (CHUNK, GROUP_X), 0)
    ec = lax.broadcasted_iota(jnp.int32, (CHUNK, GROUP_X), 1)
    expand = _one(jnp.right_shift(ec, 6) == er).astype(BF16)
    return dict(causal=causal, tri=tri, expand=expand, lo=lane < HEAD_DIM)


def _conv_silu(cur, prev, w, b):
    ext = jnp.concatenate([prev, cur], axis=0)
    acc = cur * w[3] + b
    for j in (1, 2, 3):
        acc = acc + _SHIFT[j](ext) * w[3 - j]
    return _silu(acc)


def _ssd_chunk(raw, rawp, ht, cw, cb_, dtb, alog, dsk, k):
    act = _conv_silu(raw[:, :GROUP_CONV], rawp[:, :GROUP_CONV], cw, cb_)
    xs = act[:, :GROUP_X]
    bm = act[:, GROUP_X:GROUP_X + D_STATE]
    cm = act[:, GROUP_X + D_STATE:]
    dt = _softplus(raw[:, GROUP_CONV:] + dtb)
    a = -jnp.exp(alog)
    xc = _CMM["xc"]

    def lanes(rowv):
        return jnp.sum(xc(jnp.broadcast_to(rowv, (16, CHUNK)), k["expand"]), axis=0, keepdims=True) * (1.0 / 16.0)

    dt_e = xc(dt, k["expand"])
    adt_e = dt_e * lanes(a)
    acs_e = _CMM["cx"](adt_e, k["tri"])
    tot_e = jnp.sum(adt_e, axis=0, keepdims=True)
    gmat = mm_nt(cm, bm)
    xdt = xs * dt_e
    ys = []
    for j in range(HEADS_PER_GROUP // 2):
        pair = acs_e[:, j * CHUNK:(j + 1) * CHUNK]
        swapped = _swap_halves(pair)
        ms = []
        for cb in (jnp.where(k["lo"], pair, swapped), jnp.where(k["lo"], swapped, pair)):
            seg = cb - cb.T
            ms.append(gmat * jnp.exp(jnp.where(k["causal"], seg, -jnp.inf)))
        xp = xdt[:, j * CHUNK:(j + 1) * CHUNK]
        rhs = jnp.concatenate([jnp.where(k["lo"], xp, 0.0), jnp.where(k["lo"], 0.0, xp)], axis=0)
        ys.append(mm_nn(jnp.concatenate(ms, axis=1), rhs))
    y_diag = jnp.concatenate(ys, axis=1)
    y_off = jnp.exp(acs_e) * mm_nn(cm, ht)
    h_new = jnp.exp(tot_e) * ht + mm_tn(bm, xdt * jnp.exp(tot_e - acs_e))
    return y_diag + y_off + lanes(dsk) * xs, h_new


def _ssd_in_specs(nc, rev):
    c_of = (lambda c: nc - 1 - c) if rev else (lambda c: c)
    per = CHUNK // CONV_HALO
    zx = [pl.BlockSpec((CHUNK, GROUP_COLS), lambda g, c: (c_of(c), g)),
          pl.BlockSpec((CONV_HALO, GROUP_COLS), lambda g, c: (jnp.maximum(c_of(c) * per - 1, 0), g))]
    conv = [pl.BlockSpec((4, GROUP_CONV), lambda g, c: (0, g)), pl.BlockSpec((1, GROUP_CONV), lambda g, c: (0, g))]
    head = [pl.BlockSpec((None, 1, 128), lambda g, c: (g, 0, 0))] * 3
    return zx + conv + head, c_of


def _load_chunk_args(refs, has_prev):
    raw, rawp, cw, cb_, dtb, alog, dsk = refs
    return (raw[...], rawp[...] * has_prev, tuple(cw[pl.ds(i, 1), :] for i in range(4)), cb_[...],
            dtb[...], alog[...], dsk[...])


def _ssd_fwd(zx, conv_w, conv_b, dtb, alog, dsk, carried=None):
    t = zx.shape[0]
    nc = t // CHUNK
    in_specs, _ = _ssd_in_specs(nc, False)

    def body(*refs):
        ins, (y_ref, hs_ref, ht) = refs[:7], refs[7:]
        c = pl.program_id(1)

        @pl.when(c == 0)
        def _():
            ht[...] = jnp.zeros_like(ht)

        a = _load_chunk_args(ins, _one(c > 0))
        h_in = ht[...]
        y, h_new = _ssd_chunk(*a[:2], h_in, *a[2:], _ssd_consts())
        y_ref[...] = y
        hs_ref[...] = h_in
        ht[...] = h_new

    return _pcall(
        body, name="ssd_fwd", grid=(N_GROUPS, nc),
        in_specs=in_specs,
        out_specs=[pl.BlockSpec((CHUNK, GROUP_X), lambda g, c: (c, g)),
                   pl.BlockSpec((None, None, D_STATE, GROUP_X), lambda g, c: (g, c, 0, 0))],
        out_shape=[jax.ShapeDtypeStruct((t, D_INNER), F32),
                   jax.ShapeDtypeStruct((N_GROUPS, nc, D_STATE, GROUP_X), F32)],
        scratch_shapes=[pltpu.VMEM((D_STATE, GROUP_X), F32)],
        sem=("arbitrary", "arbitrary"), args=(zx, zx, conv_w, conv_b, dtb, alog, dsk), carried=carried)


def _ssd_bwd(zx, conv_w, conv_b, dtb, alog, dsk, hs, dy, dzx, carried=None):
    t = zx.shape[0]
    nc = t // CHUNK
    in_specs, c_of = _ssd_in_specs(nc, True)
    n_in = 10

    def body(*refs):
        ins, hs_ref, dy_ref = refs[:7], refs[7], refs[8]
        (draw_ref, dcw, dcb, ddtb, dalog, ddsk, dht, carry) = refs[n_in:]
        cc = pl.program_id(1)
        accs = (dcw, dcb, ddtb, dalog, ddsk)

        @pl.when(cc == 0)
        def _():
            for r in (dht, carry) + accs:
                r[...] = jnp.zeros_like(r)

        has_prev = _one(c_of(cc) > 0)
        a = _load_chunk_args(ins, has_prev)
        k = _ssd_consts()
        fn = lambda *args: _ssd_chunk(*args, k)
        _, vjp = jax.vjp(fn, *a[:2], hs_ref[...], *a[2:])
        graw, grawp, ght, gcw, gcb, gdtb, galog, gdsk = vjp((dy_ref[...], dht[...]))
        tail = jnp.concatenate([jnp.zeros((CHUNK - CONV_HALO, GROUP_COLS), F32), carry[...]], axis=0)
        draw_ref[...] = (graw + tail).astype(BF16)
        carry[...] = grawp * has_prev
        dht[...] = ght
        for i in range(4):
            dcw[pl.ds(i, 1), :] += gcw[i]
        for ref, val in ((dcb, gcb), (ddtb, gdtb), (dalog, galog), (ddsk, gdsk)):
            ref[...] += val

    head_out = pl.BlockSpec((None, 1, 128), lambda g, c: (g, 0, 0))
    sds = jax.ShapeDtypeStruct
    return _pcall(
        body, name="ssd_bwd", grid=(N_GROUPS, nc),
        in_specs=in_specs + [
            pl.BlockSpec((None, None, D_STATE, GROUP_X), lambda g, c: (g, c_of(c), 0, 0)),
            pl.BlockSpec((CHUNK, GROUP_X), lambda g, c: (c_of(c), g)),
            _ANY],
        out_specs=[pl.BlockSpec((CHUNK, GROUP_COLS), lambda g, c: (c_of(c), g)),
                   pl.BlockSpec((4, GROUP_CONV), lambda g, c: (0, g)),
                   pl.BlockSpec((1, GROUP_CONV), lambda g, c: (0, g)),
                   head_out, head_out, head_out],
        out_shape=[sds((t, ZX_COLS), BF16), sds((4, N_GROUPS * GROUP_CONV), F32), sds((1, N_GROUPS * GROUP_CONV), F32),
                   sds((N_GROUPS, 1, 128), F32), sds((N_GROUPS, 1, 128), F32), sds((N_GROUPS, 1, 128), F32)],
        scratch_shapes=[pltpu.VMEM((D_STATE, GROUP_X), F32), pltpu.VMEM((CONV_HALO, GROUP_COLS), F32)],
        sem=("arbitrary", "arbitrary"), args=(zx, zx, conv_w, conv_b, dtb, alog, dsk, hs, dy, dzx),
        aliases={9: 0}, carried=carried)


def _gate_norm(y, zs, ng):
    outs = []
    for k in range(N_GROUPS):
        s = y[:, k * GROUP_X:(k + 1) * GROUP_X] * _silu(zs[k])
        outs.append(s * lax.rsqrt(jnp.mean(s * s, axis=-1, keepdims=True) + RMS_EPS))
    return jnp.concatenate(outs, axis=1) * ng


def _z_specs(tt):
    first = Z_OFF // GROUP_X
    return [pl.BlockSpec((tt, GROUP_X), functools.partial(lambda k, i: (i, first + k), k)) for k in range(N_GROUPS)]


def _ssm_out_fwd(y, zx, ng, w_out, h):
    t = h.shape[0]
    tt = min(t, 512)

    def body(y_ref, z0, z1, z2, z3, ng_ref, w_ref, h_ref, o_ref):
        yn = _gate_norm(y_ref[...], (z0[...], z1[...], z2[...], z3[...]), ng_ref[...])
        o_ref[...] = h_ref[...] + jnp.dot(yn.astype(BF16), w_ref[...], preferred_element_type=F32)

    return pl.pallas_call(
        body, name="ssm_out_fwd", grid=(t // tt,),
        in_specs=[pl.BlockSpec((tt, D_INNER), lambda i: (i, 0))] + _z_specs(tt) + [
            pl.BlockSpec((1, D_INNER), lambda i: (0, 0)),
            pl.BlockSpec((D_INNER, D_MODEL), lambda i: (0, 0)),
            pl.BlockSpec((tt, D_MODEL), lambda i: (i, 0))],
        out_specs=pl.BlockSpec((tt, D_MODEL), lambda i: (i, 0)),
        out_shape=jax.ShapeDtypeStruct((t, D_MODEL), F32),
        compiler_params=_cp(("arbitrary",)),
    )(y, zx, zx, zx, zx, ng, w_out, h)


def _gate_norm_group(y, z, ng):
    s = y * _silu(z)
    return s * lax.rsqrt(jnp.mean(s * s, axis=-1, keepdims=True) + RMS_EPS) * ng


def _ssm_out_bwd(dhb, y, zx, ng, w_out):
    t = dhb.shape[0]
    tt = min(t, MATMUL_TOKENS)
    first = Z_OFF // GROUP_X

    def body(dh_ref, y_ref, z_ref, ng_ref, w_ref, dy_ref, dzx_ref, yn_ref, dng_ref):
        @pl.when(pl.program_id(1) == 0)
        def _():
            dng_ref[...] = jnp.zeros_like(dng_ref)

        dyn = lax.dot_general(dh_ref[...], w_ref[...], _NT, preferred_element_type=F32)
        yn, vjp = jax.vjp(_gate_norm_group, y_ref[...], z_ref[...], ng_ref[...])
        dy, dz, dng = vjp(dyn)
        dy_ref[...] = dy
        dzx_ref[...] = dz.astype(BF16)
        yn_ref[...] = yn.astype(BF16)
        dng_ref[...] += dng

    grp = pl.BlockSpec((tt, GROUP_X), lambda k, i: (i, k))
    zgrp = pl.BlockSpec((tt, GROUP_X), lambda k, i: (i, first + k))
    gain = pl.BlockSpec((1, GROUP_X), lambda k, i: (0, k))
    return pl.pallas_call(
        body, name="ssm_out_bwd", grid=(N_GROUPS, t // tt),
        in_specs=[pl.BlockSpec((tt, D_MODEL), lambda k, i: (i, 0)), grp, zgrp, gain,
                  pl.BlockSpec((GROUP_X, D_MODEL), lambda k, i: (k, 0))],
        out_specs=[grp, zgrp, grp, gain],
        out_shape=[jax.ShapeDtypeStruct((t, D_INNER), F32), jax.ShapeDtypeStruct((t, ZX_COLS), BF16),
                   jax.ShapeDtypeStruct((t, D_INNER), BF16), jax.ShapeDtypeStruct((1, D_INNER), F32)],
        compiler_params=_cp(("arbitrary", "arbitrary")),
    )(dhb, y, zx, ng, w_out)


def _final(h, g, tgt):
    t = h.shape[0]
    tt = min(t, 512)
    nt = t // tt

    def body(h_ref, g_ref, t_ref, dh_ref, dhb_ref, loss_ref, dg_ref, lacc):
        i = pl.program_id(0)

        @pl.when(i == 0)
        def _():
            dg_ref[...] = jnp.zeros_like(dg_ref)
            lacc[...] = jnp.zeros_like(lacc)

        gv = g_ref[...]
        y, n, r = _rms_fwd(h_ref[...], gv)
        err = y - t_ref[...]
        lacc[...] += jnp.sum(err * err, axis=0, keepdims=True)
        dx, dg = _rms_bwd(err * (1.0 / D_MODEL), n, r, gv)
        dh_ref[...] = dx
        dhb_ref[...] = dx.astype(BF16)
        dg_ref[...] += dg

        @pl.when(i == nt - 1)
        def _():
            loss_ref[...] = jnp.zeros_like(loss_ref) + (0.5 / D_MODEL) * jnp.sum(lacc[...])

    tile = pl.BlockSpec((tt, D_MODEL), lambda i: (i, 0))
    vec = pl.BlockSpec((1, D_MODEL), lambda i: (0, 0))
    return pl.pallas_call(
        body, name="final_loss", grid=(nt,),
        in_specs=[tile, vec, tile],
        out_specs=[tile, tile, pl.BlockSpec((1, 128), lambda i: (0, 0)), vec],
        out_shape=[jax.ShapeDtypeStruct((t, D_MODEL), F32), jax.ShapeDtypeStruct((t, D_MODEL), BF16),
                   jax.ShapeDtypeStruct((1, 128), F32), jax.ShapeDtypeStruct((1, D_MODEL), F32)],
        scratch_shapes=[pltpu.VMEM((1, D_MODEL), F32)],
        compiler_params=_cp(("arbitrary",)),
    )(h, g, tgt)


def _adamw_reduced_parts(w, lands, m, v, name):
    rows, cols = w.shape
    br = 128
    nl = lands[0].shape[0]
    starts, blocks = [], []
    for land in lands:
        starts.append(sum(blocks))
        blocks.append(land.shape[1] // br)

    def body(w_ref, *refs):
        l_refs, (m_ref, v_ref, g_ref, d_ref, m2_ref, v2_ref) = refs[:len(lands)], refs[len(lands):]
        i = pl.program_id(0)
        gv = None
        for ref, first in zip(l_refs, starts):
            acc = ref[0].astype(F32)
            for q in range(1, nl):
                acc = acc + ref[q].astype(F32)
            gv = acc if gv is None else jnp.where(i >= first, acc, gv)
        g_ref[...] = gv
        d_ref[...], m2_ref[...], v2_ref[...] = _adamw_math(w_ref[...], gv, m_ref[...], v_ref[...])

    spec = pl.BlockSpec((br, cols), lambda i: (i, 0))
    land_specs = [pl.BlockSpec((nl, br, cols), functools.partial(
        lambda first, nb, i: (0, jnp.clip(i - first, 0, nb - 1), 0), first, nb)) for first, nb in zip(starts, blocks)]
    out = jax.ShapeDtypeStruct((rows, cols), F32)
    return pl.pallas_call(
        body, name=name, grid=(rows // br,),
        in_specs=[spec] + land_specs + [spec, spec], out_specs=[spec] * 4, out_shape=[out] * 4,
        compiler_params=_cp(("arbitrary",)),
    )(w, *lands, m, v)


def _adamw_reduced_layers(w, lands, m, v, name):
    _, rows, cols = w.shape
    br = rows if rows <= 256 else 256
    nb = rows // br
    nl = lands[0].shape[0]

    def body(w_ref, l0_ref, l1_ref, m_ref, v_ref, g_ref, d_ref, m2_ref, v2_ref):
        def total(ref):
            acc = ref[0].astype(F32)
            for q in range(1, nl):
                acc = acc + ref[q].astype(F32)
            return acc

        gv = jnp.where(pl.program_id(0) == 0, total(l0_ref), total(l1_ref))
        g_ref[...] = gv
        d_ref[...], m2_ref[...], v2_ref[...] = _adamw_math(w_ref[...], gv, m_ref[...], v_ref[...])

    spec = pl.BlockSpec((None, br, cols), lambda l, i: (l, i, 0))
    land0 = pl.BlockSpec((nl, br, cols), lambda l, i: (0, jnp.where(l == 0, i, nb - 1), 0))
    land1 = pl.BlockSpec((nl, br, cols), lambda l, i: (0, jnp.where(l == 1, i, 0), 0))
    out = jax.ShapeDtypeStruct(w.shape, F32)
    return pl.pallas_call(
        body, name=name, grid=(2, nb),
        in_specs=[spec, land0, land1, spec, spec], out_specs=[spec] * 4, out_shape=[out] * 4,
        compiler_params=_cp(("arbitrary", "arbitrary")),
    )(w, lands[0], lands[1], m, v)


def _all_reduce_small(sp):
    rows, n = sp.shape

    def body(x_ref, o_ref, land, send_sems, recv_sems):
        x, y, c = _place()
        me = 4 * x + 2 * y + c
        land[me] = x_ref[...]
        cps = []
        for rel in range(1, N_DEV):
            dx, dy, dc = (rel >> 2) & 1, (rel >> 1) & 1, rel & 1
            px = x + dx - 2 * x * dx
            py = y + dy - 2 * y * dy
            pc = c + dc - 2 * c * dc
            peer = 4 * px + 2 * py + pc
            cps.append((pltpu.make_async_remote_copy(
                src_ref=x_ref, dst_ref=land.at[me], send_sem=send_sems.at[rel - 1], recv_sem=recv_sems.at[rel - 1],
                device_id=(px, py, pc), device_id_type=MESH),
                pltpu.make_async_remote_copy(
                src_ref=x_ref, dst_ref=land.at[peer], send_sem=send_sems.at[rel - 1], recv_sem=recv_sems.at[rel - 1],
                device_id=(px, py, pc), device_id_type=MESH)))
        for cp, _ in cps:
            cp.start()
        for _, arr in cps:
            arr.wait_recv()
        for cp, _ in cps:
            cp.wait_send()
        acc = land[0]
        for k in range(1, N_DEV):
            acc = acc + land[k]
        o_ref[...] = acc

    vm = pl.BlockSpec(memory_space=pltpu.VMEM)
    return pl.pallas_call(
        body, name="all_reduce_small",
        out_shape=jax.ShapeDtypeStruct((rows, n), F32),
        in_specs=[vm], out_specs=vm,
        scratch_shapes=[pltpu.VMEM((N_DEV, rows, n), F32),
                        pltpu.SemaphoreType.DMA((N_DEV - 1,)), pltpu.SemaphoreType.DMA((N_DEV - 1,))],
    )(sp)


def _adamw_math(wv, gv, mv, vv):
    m2 = ADAM_B1 * mv + (1.0 - ADAM_B1) * gv
    v2 = ADAM_B2 * vv + (1.0 - ADAM_B2) * (gv * gv)
    m_hat = m2 / (1.0 - ADAM_B1 ** ADAM_STEP)
    v_hat = v2 / (1.0 - ADAM_B2 ** ADAM_STEP)
    return -ADAM_LR * (m_hat / (jnp.sqrt(v_hat) + ADAM_EPS) + ADAM_WD * wv), m2, v2


def _adamw(w, g, m, v, name):
    rows, cols = w.shape
    br = rows if rows <= 256 else 256

    def body(w_ref, g_ref, m_ref, v_ref, d_ref, m2_ref, v2_ref):
        d_ref[...], m2_ref[...], v2_ref[...] = _adamw_math(w_ref[...], g_ref[...], m_ref[...], v_ref[...])

    spec = pl.BlockSpec((br, cols), lambda i: (i, 0))
    out = jax.ShapeDtypeStruct((rows, cols), F32)
    return pl.pallas_call(
        body, name=name, grid=(rows // br,),
        in_specs=[spec] * 4, out_specs=[spec] * 3, out_shape=[out] * 3,
        compiler_params=_cp(("arbitrary",)),
    )(w, g, m, v)


def _adamw_reduced(w, land, m, v, name):
    rows, cols = w.shape
    br = rows if rows <= 256 else 256
    nl = land.shape[0]

    def body(w_ref, l_ref, m_ref, v_ref, g_ref, d_ref, m2_ref, v2_ref):
        gv = l_ref[0].astype(F32)
        for q in range(1, nl):
            gv = gv + l_ref[q].astype(F32)
        g_ref[...] = gv
        d_ref[...], m2_ref[...], v2_ref[...] = _adamw_math(w_ref[...], gv, m_ref[...], v_ref[...])

    spec = pl.BlockSpec((br, cols), lambda i: (i, 0))
    out = jax.ShapeDtypeStruct((rows, cols), F32)
    return pl.pallas_call(
        body, name=name, grid=(rows // br,),
        in_specs=[spec, pl.BlockSpec((nl, br, cols), lambda i: (0, i, 0)), spec, spec],
        out_specs=[spec] * 4, out_shape=[out] * 4,
        compiler_params=_cp(("arbitrary",)),
    )(w, land, m, v)


def _zx_source_col(col):
    blk = jnp.right_shift(col, 7)
    lane = jnp.bitwise_and(col, 127)
    per = GROUP_COLS // 128
    grp = jnp.where(blk >= per, 1, 0) + jnp.where(blk >= 2 * per, 1, 0) + jnp.where(blk >= 3 * per, 1, 0)
    o = blk - per * grp
    x_col = D_INNER + GROUP_X * grp + 128 * o + lane
    b_col = 2 * D_INNER + D_STATE * grp + lane
    c_col = 2 * D_INNER + N_GROUPS * D_STATE + D_STATE * grp + lane
    dt_col = jnp.where(lane < HEADS_PER_GROUP, D_INNER + CONV_DIM + HEADS_PER_GROUP * grp + lane, -1)
    src = jnp.where(o < 4, x_col, jnp.where(o == 4, b_col, jnp.where(o == 5, c_col, dt_col)))
    return jnp.where(col >= Z_OFF, col - Z_OFF, src)


def _zx_source_col_py(col):
    if col >= Z_OFF:
        return col - Z_OFF
    grp, o = divmod(col, GROUP_COLS)
    if o < GROUP_X:
        return D_INNER + GROUP_X * grp + o
    if o < GROUP_X + D_STATE:
        return 2 * D_INNER + D_STATE * grp + (o - GROUP_X)
    if o < GROUP_CONV:
        return 2 * D_INNER + N_GROUPS * D_STATE + D_STATE * grp + (o - GROUP_X - D_STATE)
    h = o - GROUP_CONV
    return D_INNER + CONV_DIM + HEADS_PER_GROUP * grp + h if h < HEADS_PER_GROUP else -1


def _overlap_tables():
    nblk = ZX_COLS // COL_BLK
    src = [_zx_source_col_py(c) for c in range(ZX_COLS)]
    fwd = [sorted({s // W_IN_SHARD for s in src[COL_BLK * j:COL_BLK * (j + 1)] if s >= 0}) for j in range(nblk)]
    dst = {s: c for c, s in enumerate(src) if s >= 0}
    bwd = [sorted({dst[s] // COL_BLK for s in range(W_IN_SHARD * k, W_IN_SHARD * (k + 1))}) for k in range(N_DEV)]

    def flat(rows):
        width = max(len(r) for r in rows)
        idx = [r + [r[-1]] * (width - len(r)) for r in rows]
        val = [[1] * len(r) + [0] * (width - len(r)) for r in rows]
        return (jnp.asarray(sum(idx, []), jnp.int32), jnp.asarray(sum(val, []), jnp.int32), width)

    return flat(fwd), flat(bwd)


def _w_in_to_zx(w_in_g):
    (tab, val, width), _ = _overlap_tables()
    nblk = ZX_COLS // COL_BLK

    def body(tab_ref, val_ref, w_ref, o_ref, acc):
        j = pl.program_id(0)
        s = pl.program_id(1)

        @pl.when(s == 0)
        def _():
            acc[...] = jnp.zeros_like(acc)

        @pl.when(val_ref[j * width + s] == 1)
        def _():
            k = tab_ref[j * width + s]
            col = COL_BLK * j + lax.broadcasted_iota(jnp.int32, (8, COL_BLK), 1)
            src = jnp.broadcast_to(_zx_source_col(col)[0:1, :], (W_IN_SHARD, COL_BLK))
            row = W_IN_SHARD * k + lax.broadcasted_iota(jnp.int32, (W_IN_SHARD, COL_BLK), 0)
            place = _one(src == row).astype(BF16)
            acc[...] += jnp.dot(w_ref[...], place, preferred_element_type=F32)

        @pl.when(s == width - 1)
        def _():
            o_ref[...] = acc[...].astype(BF16)

    return pl.pallas_call(
        body, name="w_in_to_zx",
        grid_spec=pltpu.PrefetchScalarGridSpec(
            num_scalar_prefetch=2, grid=(nblk, width),
            in_specs=[pl.BlockSpec((None, D_MODEL, W_IN_SHARD), lambda j, s, tab, val: (tab[j * width + s], 0, 0))],
            out_specs=pl.BlockSpec((D_MODEL, COL_BLK), lambda j, s, tab, val: (0, j)),
            scratch_shapes=[pltpu.VMEM((D_MODEL, COL_BLK), F32)]),
        out_shape=jax.ShapeDtypeStruct((D_MODEL, ZX_COLS), BF16),
        compiler_params=_cp(("arbitrary", "arbitrary")),
    )(tab, val, w_in_g)


def _zx_to_w_in(d_wzx):
    _, (tab, val, width) = _overlap_tables()

    def body(tab_ref, val_ref, d_ref, o_ref, acc):
        k = pl.program_id(0)
        s = pl.program_id(1)

        @pl.when(s == 0)
        def _():
            acc[...] = jnp.zeros_like(acc)

        @pl.when(val_ref[k * width + s] == 1)
        def _():
            j = tab_ref[k * width + s]
            col = COL_BLK * j + lax.broadcasted_iota(jnp.int32, (COL_BLK, 128), 0)
            src = jnp.broadcast_to(_zx_source_col(col)[:, 0:1], (COL_BLK, W_IN_SHARD))
            row = W_IN_SHARD * k + lax.broadcasted_iota(jnp.int32, (COL_BLK, W_IN_SHARD), 1)
            place = _one(src == row).astype(BF16)
            acc[...] += jnp.dot(d_ref[...], place, preferred_element_type=F32)

        @pl.when(s == width - 1)
        def _():
            o_ref[...] = acc[...].astype(BF16)

    return pl.pallas_call(
        body, name="zx_to_w_in",
        grid_spec=pltpu.PrefetchScalarGridSpec(
            num_scalar_prefetch=2, grid=(N_DEV, width),
            in_specs=[pl.BlockSpec((D_MODEL, COL_BLK), lambda k, s, tab, val: (0, tab[k * width + s]))],
            out_specs=pl.BlockSpec((None, D_MODEL, W_IN_SHARD), lambda k, s, tab, val: (k, 0, 0)),
            scratch_shapes=[pltpu.VMEM((D_MODEL, W_IN_SHARD), F32)]),
        out_shape=jax.ShapeDtypeStruct((N_DEV, D_MODEL, W_IN_SHARD), BF16),
        compiler_params=_cp(("arbitrary", "arbitrary")),
    )(tab, val, d_wzx)


def _group_conv_cols(a):
    rows = a.shape[0]
    x = a[:, :D_INNER].reshape(rows, N_GROUPS, GROUP_X)
    b = a[:, D_INNER:D_INNER + N_GROUPS * D_STATE].reshape(rows, N_GROUPS, D_STATE)
    c = a[:, D_INNER + N_GROUPS * D_STATE:].reshape(rows, N_GROUPS, D_STATE)
    return jnp.concatenate([x, b, c], axis=2).reshape(rows, N_GROUPS * GROUP_CONV)


def _ungroup_conv_cols(a):
    rows = a.shape[0]
    a3 = a.reshape(rows, N_GROUPS, GROUP_CONV)
    return jnp.concatenate([a3[:, :, :GROUP_X].reshape(rows, D_INNER),
                            a3[:, :, GROUP_X:GROUP_X + D_STATE].reshape(rows, N_GROUPS * D_STATE),
                            a3[:, :, GROUP_X + D_STATE:].reshape(rows, N_GROUPS * D_STATE)], axis=1)


def _small_shard(conv_w, conv_b, norm_g):
    ng = jnp.pad(norm_g.reshape(1, -1), ((0, 0), (0, CONV_SHARD - norm_g.shape[-1])))
    return jnp.concatenate([conv_w.reshape(4, CONV_SHARD), conv_b.reshape(1, CONV_SHARD), ng,
                            jnp.zeros((SMALL_ROWS - 6, CONV_SHARD), F32)], axis=0)


def _small_unshard(a):
    return a[0:4].reshape(1, 4, CONV_SHARD), a[4:5], a[5:6, :D_INNER // N_DEV]


def _heads_of(a):
    return a[:, :, :HEADS_PER_GROUP].reshape(1, N_HEADS)


def _head_params(p):
    return jnp.pad(p.reshape(N_GROUPS, 1, HEADS_PER_GROUP), ((0, 0), (0, 0), (0, 128 - HEADS_PER_GROUP)))


def _update(w, land, m, v, name):
    shp = w.shape
    to2 = lambda a: a.reshape(-1, shp[-1])
    return tuple(o.reshape(shp) for o in _adamw_reduced(to2(w), land, to2(m), to2(v), name))


def kernel(x, norm_mix_g, norm_mlp_g, pool_w, pool_b, pool_scale, ssm_w_in, ssm_conv_w, ssm_conv_b, ssm_dt_bias, ssm_a_log, ssm_d, ssm_norm_g, ssm_w_out, mlp_w1, mlp_w2, final_g, loss_target, m_norm_mix_g, m_norm_mlp_g, m_pool_w, m_pool_b, m_pool_scale, m_ssm_w_in, m_ssm_conv_w, m_ssm_conv_b, m_ssm_dt_bias, m_ssm_a_log, m_ssm_d, m_ssm_norm_g, m_ssm_w_out, m_mlp_w1, m_mlp_w2, m_final_g, v_norm_mix_g, v_norm_mlp_g, v_pool_w, v_pool_b, v_pool_scale, v_ssm_w_in, v_ssm_conv_w, v_ssm_conv_b, v_ssm_dt_bias, v_ssm_a_log, v_ssm_d, v_ssm_norm_g, v_ssm_w_out, v_mlp_w1, v_mlp_w2, v_final_g):
    x2 = x[0]
    tgt = loss_target[0]
    gm0, gm1 = norm_mix_g[0:1], norm_mix_g[1:2]
    gl0, gl1 = norm_mlp_g[0:1], norm_mlp_g[1:2]
    gfin = final_g.reshape(1, D_MODEL)

    fb = D_FF // N_DEV

    def bf(a):
        return a.astype(BF16)

    def gather_of(shards):
        return _direct_exchange(shards, [(i, 0) for i in range(len(shards))],
                                [(s.shape, s.dtype) for s in shards], scatter=False)

    def scatter_of(parts, rows=None):
        shapes = [((p.shape[1] if rows is None else rows[1], p.shape[2]), p.dtype) for p in parts]
        return _direct_exchange(parts, [(i, 0) for i in range(len(parts))], shapes, scatter=True,
                                src_rows=None if rows is None else [rows] * len(parts))

    w_pool, small_g = _run_exchange(_two_level_gather(
        [bf(pool_w.reshape(4 * POOL_SHARD, POOL_GROUP)), _small_shard(ssm_conv_w, ssm_conv_b, ssm_norm_g)]),
        "gather_first")
    conv_w = _group_conv_cols(small_g[:, 0:4].transpose(1, 0, 2).reshape(4, CONV_DIM))
    conv_b = _group_conv_cols(small_g[:, 4].reshape(1, CONV_DIM))
    ssm_ng = small_g[:, 5, :D_INNER // N_DEV].reshape(1, D_INNER)
    dtb, alog, dsk = _head_params(ssm_dt_bias), _head_params(ssm_a_log), _head_params(ssm_d)

    (h1,), (w1g0,) = _pool_fwd(x2, gm0, w_pool, pool_b, pool_scale,
                               carried=_two_level_gather([bf(mlp_w1[0])], mid_percent=100))
    (u0, v0, hm0), (w2g0,) = _mlp_fwd_up(h1, gl0, w1g0, "mlp0_fwd_up", carried=_two_level_gather([bf(mlp_w2[0])]))
    (h2,), (w_in_g,) = _mlp_fwd_down(h1, v0, w2g0, "mlp0_fwd_down", carried=_two_level_gather([bf(ssm_w_in[0])]))
    w_zx = _w_in_to_zx(w_in_g)
    (zx, hn1), (w_out_g,) = _norm_matmul(h2, gm1, w_zx, carried=gather_of([bf(ssm_w_out[0])]))
    (y_ssd, states), (w1g1, w2g1) = _ssd_fwd(zx, conv_w, conv_b, dtb, alog, dsk,
                                             carried=_two_level_gather([bf(mlp_w1[1]), bf(mlp_w2[1])]))
    w_out = w_out_g.reshape(D_INNER, D_MODEL)
    h3 = _ssm_out_fwd(y_ssd, zx, ssm_ng, w_out, h2)
    (h4, u1, hm1), _ = _mlp_fwd(h3, gl1, w1g1, w2g1, "mlp1_fwd")
    dh4, dh4b, loss_row, d_gfin = _final(h4, gfin, tgt)

    (dh3, dh3b, da1, d_gl1), _ = _mlp_bwd(dh4, dh4b, h3, gl1, u1, w1g1, w2g1, "mlp1_bwd")
    d_w1_1 = _matmul_tn(hm1, da1, "mlp1_dw1", col_blocked=True, scale=2.0)
    d_w2_1 = _matmul_tn(u1, dh4b, "mlp1_dw2", square_a=True).reshape(N_DEV, fb, D_MODEL)
    dy_ssd, dzx, yn, d_ng = _ssm_out_bwd(dh3b, y_ssd, zx, ssm_ng, w_out)
    d_wout = _matmul_tn(yn, dh3b, "ssm_dw_out").reshape(N_DEV, D_INNER // N_DEV, D_MODEL)
    (dzx, d_cw, d_cb, d_dtb, d_alog, d_dsk), (l_w1_1, l_w2_1, l_wout) = _ssd_bwd(
        zx, conv_w, conv_b, dtb, alog, dsk, states, dy_ssd, dzx, carried=scatter_of([d_w1_1, d_w2_1, d_wout]))
    d_w_in = _zx_to_w_in(_matmul_tn(hn1, dzx, "ssm_dw_in"))
    most = 5 * D_MODEL // 8
    (dh2, dh2b, d_gm1), (l_w_in_a,) = _in_proj_bwd(dzx, w_zx, h2, gm1, dh3, carried=scatter_of([d_w_in], (0, most)))
    d_w2_0, (l_w_in_b,) = _matmul_tn(u0, dh2b, "mlp0_dw2", square_a=True,
                                     carried=scatter_of([d_w_in], (most, D_MODEL - most)))
    d_w2_0 = d_w2_0.reshape(N_DEV, fb, D_MODEL)
    (dh1, _, da0, d_gl0), (l_w2_0,) = _mlp_bwd(dh2, dh2b, h1, gl0, u0, w1g0, w2g0, "mlp0_bwd",
                                           carried=scatter_of([d_w2_0]))
    d_w1_0 = _matmul_tn(hm0, da0, "mlp0_dw1", col_blocked=True, scale=2.0)
    (dx, d_pool, d_pb, d_ps, d_gm0), (l_w1_0,) = _pool_bwd(x2, dh1, gm0, w_pool, pool_b, pool_scale,
                                                          carried=scatter_of([d_w1_0]))

    d_conv_w = _ungroup_conv_cols(d_cw).reshape(4, N_DEV, CONV_SHARD).transpose(1, 0, 2)
    d_conv_b = _ungroup_conv_cols(d_cb).reshape(N_DEV, 1, CONV_SHARD)
    d_gain = jnp.pad(d_ng.reshape(N_DEV, 1, D_INNER // N_DEV), ((0, 0), (0, 0), (0, CONV_SHARD - D_INNER // N_DEV)))
    d_small = jnp.concatenate([d_conv_w, d_conv_b, d_gain,
                               jnp.zeros((N_DEV, SMALL_ROWS - 6, CONV_SHARD), F32)], axis=1)
    l_pool, l_small = _run_exchange(scatter_of([bf(d_pool), d_small]), "reduce_scatter_tail")

    heads = jnp.concatenate([_heads_of(a) for a in (d_dtb, d_alog, d_dsk)], axis=1)
    sp = jnp.concatenate([d_gm0, d_gm1, d_gl0, d_gl1, d_pb, d_ps, d_gfin,
                          jnp.pad(heads, ((0, 0), (0, D_MODEL - 3 * N_HEADS)))], axis=0)
    sg = _all_reduce_small(sp)

    g_norm_mix = sg[0:2]
    g_norm_mlp = sg[2:4]
    g_pool_b, g_pool_scale = sg[4:5], sg[5:6]
    g_final = sg[6]
    g_dtb, g_alog, g_dsk = sg[7:8, 0:32], sg[7:8, 32:64], sg[7:8, 64:96]

    def rep_pack(nm, nl, pb, ps, fg, db, al, dk):
        hd = jnp.pad(jnp.concatenate([db, al, dk], axis=1), ((0, 0), (0, D_MODEL - 3 * N_HEADS)))
        return jnp.concatenate([nm, nl, pb, ps, fg.reshape(1, D_MODEL), hd], axis=0)

    rep = [rep_pack(*t) for t in (
        (norm_mix_g, norm_mlp_g, pool_b, pool_scale, final_g, ssm_dt_bias, ssm_a_log, ssm_d),
        (g_norm_mix, g_norm_mlp, g_pool_b, g_pool_scale, g_final, g_dtb, g_alog, g_dsk),
        (m_norm_mix_g, m_norm_mlp_g, m_pool_b, m_pool_scale, m_final_g, m_ssm_dt_bias, m_ssm_a_log, m_ssm_d),
        (v_norm_mix_g, v_norm_mlp_g, v_pool_b, v_pool_scale, v_final_g, v_ssm_dt_bias, v_ssm_a_log, v_ssm_d))]
    rep_out = _adamw(*rep, "adamw_replicated")

    def rep_unpack(a):
        return (a[0:2], a[2:4], a[4:5], a[5:6], a[6], a[7:8, 0:32], a[7:8, 32:64], a[7:8, 64:96])

    sm_out = _adamw_reduced(_small_shard(ssm_conv_w, ssm_conv_b, ssm_norm_g), l_small,
                            _small_shard(m_ssm_conv_w, m_ssm_conv_b, m_ssm_norm_g),
                            _small_shard(v_ssm_conv_w, v_ssm_conv_b, v_ssm_norm_g), "adamw_small_shards")

    big = {
        "pool_w": _update(pool_w, l_pool, m_pool_w, v_pool_w, "adamw_pool_w"),
        "ssm_w_in": tuple(o.reshape(ssm_w_in.shape) for o in _adamw_reduced_parts(
            ssm_w_in[0], (l_w_in_a, l_w_in_b), m_ssm_w_in[0], v_ssm_w_in[0], "adamw_w_in")),
        "ssm_w_out": _update(ssm_w_out, l_wout, m_ssm_w_out, v_ssm_w_out, "adamw_w_out"),
        "mlp_w1": _adamw_reduced_layers(mlp_w1, (l_w1_0, l_w1_1), m_mlp_w1, v_mlp_w1, "adamw_w1"),
        "mlp_w2": _adamw_reduced_layers(mlp_w2, (l_w2_0, l_w2_1), m_mlp_w2, v_mlp_w2, "adamw_w2"),
    }
    rep_all = (rep[1],) + tuple(rep_out)

    def ordered(kind):
        nm, nl, pb, ps, fg, db, al, dk = rep_unpack(rep_all[kind])
        cw, cb, ng = _small_unshard(sm_out[kind])
        return [nm, nl, big["pool_w"][kind], pb, ps, big["ssm_w_in"][kind], cw, cb, db, al, dk, ng,
                big["ssm_w_out"][kind], big["mlp_w1"][kind], big["mlp_w2"][kind], fg]

    loss = lax.psum(loss_row[0, 0], ("x", "y", "c"))
    return (loss, dx[None], *ordered(0), *ordered(1), *ordered(2), *ordered(3))
```

```python
import functools

import jax
import jax.numpy as jnp
from jax import lax
from jax.experimental import pallas as pl
from jax.experimental.pallas import tpu as pltpu

F32 = jnp.float32
BF16 = jnp.bfloat16
MESH = pl.DeviceIdType.MESH

D_MODEL = 1024
RMS_EPS = 1e-5
POOL_WINDOWS = (2, 4, 8, 16)
POOL_GROUP = 256
POOL_HALO = 16
POOL_SHARD = POOL_GROUP // 8
D_INNER = 2048
HEAD_DIM = 64
N_HEADS = 32
N_GROUPS = 4
HEADS_PER_GROUP = 8
D_STATE = 128
CHUNK = 128
CONV_DIM = 3072
IN_PROJ_DIM = 5152
D_FF = 4096
N_DEV = 8
GROUP_X = HEADS_PER_GROUP * HEAD_DIM
GROUP_CONV = GROUP_X + 2 * D_STATE
GROUP_COLS = GROUP_CONV + 128
Z_OFF = N_GROUPS * GROUP_COLS
ZX_COLS = Z_OFF + D_INNER
COL_BLK = 512
PROJ_BLK = ZX_COLS // 4
W_IN_SHARD = IN_PROJ_DIM // N_DEV

ADAM_LR = 0.001
ADAM_B1 = 0.9
ADAM_B2 = 0.999
ADAM_EPS = 1e-08
ADAM_WD = 0.01
ADAM_STEP = 10

VMEM_LIMIT_V7X = 56 * 1024 * 1024
MID_STEP_PERCENT = 85
TN_TOKENS = 512
TN_ACC_BYTES = 16 * 1024 * 1024
MATMUL_TOKENS = 1024

CONV_SHARD = CONV_DIM // N_DEV
SMALL_ROWS = 8

_NN = (((1,), (0,)), ((), ()))
_NT = (((1,), (1,)), ((), ()))
_TN = (((0,), (0,)), ((), ()))


def _cp(sem):
    return pltpu.CompilerParams(dimension_semantics=sem, vmem_limit_bytes=VMEM_LIMIT_V7X)


_ANY = pl.BlockSpec(memory_space=pl.ANY)


def _place():
    return lax.axis_index("x"), lax.axis_index("y"), lax.axis_index("c")


class _Carried:
    def __init__(self, ins, outs, sems, start, finish, mid=None, mid_percent=None):
        self.ins, self.outs, self.sems = list(ins), list(outs), list(sems)
        self.start, self.mid, self.finish, self.mid_percent = start, mid, finish, mid_percent


def _pcall(body, *, name, grid, in_specs, out_specs, out_shape, sem, args, scratch_shapes=(), carried=None,
           aliases=None):
    in_specs, out_specs, out_shape, scratch = list(in_specs), list(out_specs), list(out_shape), list(scratch_shapes)
    common = dict(name=name, grid=grid, input_output_aliases=aliases or {}, compiler_params=_cp(sem))
    if carried is None:
        res = pl.pallas_call(body, in_specs=in_specs, out_specs=out_specs, out_shape=out_shape,
                             scratch_shapes=scratch, **common)(*args)
        return list(res), []
    n_in, n_out, n_scr = len(in_specs), len(out_specs), len(scratch)
    ci, co = len(carried.ins), len(carried.outs)

    def wrapped(*refs):
        ins, cins = refs[:n_in], refs[n_in:n_in + ci]
        p = n_in + ci
        outs, couts = refs[p:p + n_out], refs[p + n_out:p + n_out + co]
        p += n_out + co
        scr, csems = refs[p:p + n_scr], refs[p + n_scr:]
        ids = [pl.program_id(a) for a in range(len(grid))]
        first = functools.reduce(jnp.logical_and, [i == 0 for i in ids])
        last = functools.reduce(jnp.logical_and, [i == g - 1 for i, g in zip(ids, grid)])

        @pl.when(first)
        def _():
            carried.start(cins, couts, csems)

        if carried.mid is not None:
            step, steps = 0, 1
            for i, g in zip(ids, grid):
                step, steps = step * g + i, steps * g

            @pl.when(step == min(steps - 1, (steps * carried.mid_percent) // 100))
            def _():
                carried.mid(cins, couts, csems)

        body(*ins, *outs, *scr)

        @pl.when(last)
        def _():
            carried.finish(cins, couts, csems)

    res = pl.pallas_call(wrapped, in_specs=in_specs + [_ANY] * ci, out_specs=out_specs + [_ANY] * co,
                         out_shape=out_shape + carried.outs, scratch_shapes=scratch + carried.sems,
                         **common)(*args, *carried.ins)
    return list(res[:n_out]), list(res[n_out:])


def _peers(x, y, c):
    out = []
    for rel in range(1, N_DEV):
        dx, dy, dc = (rel >> 2) & 1, (rel >> 1) & 1, rel & 1
        out.append((x + dx - 2 * x * dx, y + dy - 2 * y * dy, c + dc - 2 * c * dc))
    return out


def _direct_exchange(srcs, layout, out_shapes, scatter, src_rows=None):
    n = len(srcs)

    def copies(ins, outs, sems):
        send, recv, loc = sems
        x, y, c = _place()
        me = 4 * x + 2 * y + c
        out, arrive, local = [], [], []
        for i in range(n):
            j, off = layout[i]
            first, rows = (0, srcs[i].shape[-2]) if src_rows is None else src_rows[i]

            def piece(k):
                return ins[i].at[k, pl.ds(first, rows)] if scatter else ins[i]

            for r, peer in enumerate(_peers(x, y, c)):
                pidx = 4 * peer[0] + 2 * peer[1] + peer[2]
                kw = dict(send_sem=send.at[7 * i + r], recv_sem=recv.at[7 * i + r], device_id=peer, device_id_type=MESH)
                out.append(pltpu.make_async_remote_copy(
                    src_ref=piece(pidx), dst_ref=outs[j].at[me, pl.ds(off, rows)], **kw))
                arrive.append(pltpu.make_async_remote_copy(
                    src_ref=piece(pidx), dst_ref=outs[j].at[pidx, pl.ds(off, rows)], **kw))
            local.append(pltpu.make_async_copy(piece(me), outs[j].at[me, pl.ds(off, rows)], loc.at[i]))
        return out, arrive, local

    def start(ins, outs, sems):
        out, _, local = copies(ins, outs, sems)
        for cp in local + out:
            cp.start()

    def finish(ins, outs, sems):
        out, arrive, local = copies(ins, outs, sems)
        for cp in arrive:
            cp.wait_recv()
        for cp in out:
            cp.wait_send()
        for cp in local:
            cp.wait()

    return _Carried(srcs, [jax.ShapeDtypeStruct((N_DEV,) + tuple(s), d) for s, d in out_shapes],
                    [pltpu.SemaphoreType.DMA((7 * n,)), pltpu.SemaphoreType.DMA((7 * n,)),
                     pltpu.SemaphoreType.DMA((n,))], start, finish)


def _two_level_gather(shards, mid_percent=MID_STEP_PERCENT):
    n = len(shards)

    def copies(ins, outs, sems):
        send, recv, loc = sems
        x, y, c = _place()
        me, sibling = (x, y, c), (x, y, 1 - c)
        chips = [(1 - x, y), (x, 1 - y), (1 - x, 1 - y)]

        def win(i, place):
            return outs[i].at[4 * place[0] + 2 * place[1] + place[2]]

        def copy(i, k, block, to, src=None):
            return pltpu.make_async_remote_copy(
                src_ref=win(i, block) if src is None else src, dst_ref=win(i, block),
                send_sem=send.at[7 * i + k], recv_sem=recv.at[7 * i + k], device_id=to, device_id_type=MESH)

        own, passed, ici_in, d2d_in, local = [], [], [], [], []
        for i in range(n):
            own += [copy(i, 0, me, sibling, src=ins[i])]
            own += [copy(i, 1 + j, me, (*chip, c), src=ins[i]) for j, chip in enumerate(chips)]
            passed += [copy(i, 4 + j, (*chip, c), sibling) for j, chip in enumerate(chips)]
            ici_in += [copy(i, 1 + j, (*chip, c), me) for j, chip in enumerate(chips)]
            d2d_in += [copy(i, 0, sibling, me)] + [copy(i, 4 + j, (*chip, 1 - c), me) for j, chip in enumerate(chips)]
            local.append(pltpu.make_async_copy(ins[i], win(i, me), loc.at[i]))
        return own, passed, ici_in, d2d_in, local

    def start(ins, outs, sems):
        own, _, _, _, local = copies(ins, outs, sems)
        for cp in local + own:
            cp.start()

    def mid(ins, outs, sems):
        _, passed, ici_in, _, _ = copies(ins, outs, sems)
        for arrived, onward in zip(ici_in, passed):
            arrived.wait_recv()
            onward.start()

    def finish(ins, outs, sems):
        own, passed, _, d2d_in, local = copies(ins, outs, sems)
        for cp in d2d_in:
            cp.wait_recv()
        for cp in own + passed:
            cp.wait_send()
        for cp in local:
            cp.wait()

    return _Carried(shards, [jax.ShapeDtypeStruct((N_DEV,) + tuple(s.shape), s.dtype) for s in shards],
                    [pltpu.SemaphoreType.DMA((7 * n,)), pltpu.SemaphoreType.DMA((7 * n,)),
                     pltpu.SemaphoreType.DMA((n,))], start, finish, mid, mid_percent)


def _run_exchange(carried, name):
    ci = len(carried.ins)

    def body(*refs):
        ins, outs, sems = refs[:ci], refs[ci:ci + len(carried.outs)], refs[ci + len(carried.outs):]
        carried.start(ins, outs, sems)
        if carried.mid is not None:
            carried.mid(ins, outs, sems)
        carried.finish(ins, outs, sems)

    return list(pl.pallas_call(body, name=name, in_specs=[_ANY] * ci, out_specs=[_ANY] * len(carried.outs),
                               out_shape=carried.outs, scratch_shapes=carried.sems)(*carried.ins))


def _dg(a, b, dn):
    return lax.dot_general(a.astype(BF16), b.astype(BF16), dn, preferred_element_type=F32)


@jax.custom_vjp
def mm_nn(a, b):
    return _dg(a, b, _NN)


@jax.custom_vjp
def mm_nt(a, b):
    return _dg(a, b, _NT)


@jax.custom_vjp
def mm_tn(a, b):
    return _dg(a, b, _TN)


mm_nn.defvjp(lambda a, b: (_dg(a, b, _NN), (a, b)), lambda r, ct: (mm_nt(ct, r[1]), mm_tn(r[0], ct)))
mm_nt.defvjp(lambda a, b: (_dg(a, b, _NT), (a, b)), lambda r, ct: (mm_nn(ct, r[1]), mm_tn(ct, r[0])))
mm_tn.defvjp(lambda a, b: (_dg(a, b, _TN), (a, b)), lambda r, ct: (mm_nt(r[1], ct), mm_nn(r[0], ct)))


def _split3(x):
    p1 = x.astype(BF16)
    r1 = x - p1.astype(F32)
    p2 = r1.astype(BF16)
    r2 = r1 - p2.astype(F32)
    return p1, p2, r2.astype(BF16)


def _exact01(x, c, dn, const_left):
    acc = None
    for p in reversed(_split3(x)):
        t = (lax.dot_general(c, p, dn, preferred_element_type=F32) if const_left
             else lax.dot_general(p, c, dn, preferred_element_type=F32))
        acc = t if acc is None else acc + t
    return acc


def _make_cmm(dn, const_left, bwd_name):
    @jax.custom_vjp
    def f(x, c):
        return _exact01(x, c, dn, const_left)

    def fwd(x, c):
        return _exact01(x, c, dn, const_left), c

    def bwd(c, ct):
        return _CMM[bwd_name](ct, c), jnp.zeros_like(c)

    f.defvjp(fwd, bwd)
    return f


_CMM = {}
_CMM["xc"] = _make_cmm(_NN, False, "xct")
_CMM["xct"] = _make_cmm(_NT, False, "xc")
_CMM["cx"] = _make_cmm(_NN, True, "ctx")
_CMM["ctx"] = _make_cmm(_TN, True, "cx")


def _sigmoid(x):
    return 0.5 * jnp.tanh(0.5 * x) + 0.5


@jax.custom_vjp
def _silu(x):
    return x * _sigmoid(x)


def _silu_fwd(x):
    return _silu(x), x


def _silu_bwd(x, ct):
    s = _sigmoid(x)
    return (ct * (s * (1.0 + x * (1.0 - s))),)


_silu.defvjp(_silu_fwd, _silu_bwd)


def _log1p_pos(e):
    u = 1.0 + e
    d = u - 1.0
    return jnp.where(d == 0.0, e, jnp.log(u) * (e / jnp.where(d == 0.0, 1.0, d)))


@jax.custom_vjp
def _softplus(x):
    return jnp.maximum(x, 0.0) + _log1p_pos(jnp.exp(-jnp.abs(x)))


def _softplus_fwd(x):
    return _softplus(x), x


def _softplus_bwd(x, ct):
    return (ct * _sigmoid(x),)


_softplus.defvjp(_softplus_fwd, _softplus_bwd)


CONV_HALO = 8


def _make_shift(j):
    @jax.custom_vjp
    def f(ext):
        return pltpu.roll(ext, j, 0)[CONV_HALO:, :]

    def fwd(ext):
        return f(ext), None

    def bwd(_, ct):
        pad = jnp.concatenate([jnp.zeros((CONV_HALO, ct.shape[1]), ct.dtype), ct], axis=0)
        return (pltpu.roll(pad, CONV_HALO + CHUNK - j, 0),)

    f.defvjp(fwd, bwd)
    return f


_SHIFT = {j: _make_shift(j) for j in (1, 2, 3)}


@jax.custom_vjp
def _swap_halves(x):
    return pltpu.roll(x, HEAD_DIM, 1)


_swap_halves.defvjp(lambda x: (_swap_halves(x), None), lambda _, ct: (pltpu.roll(ct, HEAD_DIM, 1),))


def _rms_fwd(x, g):
    r = lax.rsqrt(jnp.mean(x * x, axis=-1, keepdims=True) + RMS_EPS)
    n = x * r
    return n * g, n, r


def _rms_bwd(dy, n, r, g):
    dn = dy * g
    dx = r * (dn - n * jnp.mean(dn * n, axis=-1, keepdims=True))
    dg = jnp.sum(dy * n, axis=0, keepdims=True)
    return dx, dg


def _one(cond):
    return jnp.where(cond, 1.0, 0.0)


def _pool_tile(xe, g, ws, b, scale, tile, tt):
    r = lax.rsqrt(jnp.mean(xe * xe, axis=-1, keepdims=True) + RMS_EPS)
    hn = xe * r * g
    row_e = lax.broadcasted_iota(jnp.int32, (tt + POOL_HALO, POOL_GROUP), 0)
    keep = _one(jnp.logical_or(row_e >= POOL_HALO, tile > 0))
    rr = lax.broadcasted_iota(jnp.int32, (tt, tt + POOL_HALO), 0)
    qq = lax.broadcasted_iota(jnp.int32, (tt, tt + POOL_HALO), 1)
    dd = qq - rr
    tpos = tile * tt + lax.broadcasted_iota(jnp.int32, (tt, POOL_GROUP), 0)
    outs = []
    for gi, w in enumerate(POOL_WINDOWS):
        hg = hn[:, gi * POOL_GROUP:(gi + 1) * POOL_GROUP] * keep
        band = _one(jnp.logical_and(dd >= POOL_HALO - w + 1, dd <= POOL_HALO)).astype(BF16)
        cnt = jnp.minimum(tpos + 1, w).astype(F32)
        pooled = _CMM["cx"](hg, band) / cnt
        mixed = pooled - hg[POOL_HALO:, :]
        outs.append(mm_nn(mixed, ws[gi]))
    out = (jnp.concatenate(outs, axis=1) + b) * scale
    return xe[POOL_HALO:, :] + out


def _pool_specs(tt, nt, rev):
    per = tt // POOL_HALO
    t_of = (lambda i: nt - 1 - i) if rev else (lambda i: i)
    main = pl.BlockSpec((tt, D_MODEL), lambda i: (t_of(i), 0))
    halo = pl.BlockSpec((POOL_HALO, D_MODEL), lambda i: (jnp.maximum(t_of(i) * per - 1, 0), 0))
    vec = pl.BlockSpec((1, D_MODEL), lambda i: (0, 0))
    wsp = pl.BlockSpec((N_DEV, 4 * POOL_SHARD, POOL_GROUP), lambda i: (0, 0, 0))
    return main, halo, vec, wsp


def _pool_weights(w_ref):
    return tuple(
        jnp.concatenate([w_ref[k, gi * POOL_SHARD:(gi + 1) * POOL_SHARD, :] for k in range(N_DEV)], axis=0).astype(F32)
        for gi in range(4))


def _pool_fwd(x, g, w, b, scale, carried=None):
    t = x.shape[0]
    tt = min(t, 256)
    nt = t // tt
    main, halo, vec, wsp = _pool_specs(tt, nt, False)

    def body(xm_ref, xh_ref, g_ref, w_ref, b_ref, s_ref, o_ref):
        i = pl.program_id(0)
        xe = jnp.concatenate([xh_ref[...], xm_ref[...]], axis=0)
        o_ref[...] = _pool_tile(xe, g_ref[...], _pool_weights(w_ref), b_ref[...], s_ref[...], i, tt)

    return _pcall(
        body, name="pool_fwd", grid=(nt,),
        in_specs=[main, halo, vec, wsp, vec, vec], out_specs=[main],
        out_shape=[jax.ShapeDtypeStruct((t, D_MODEL), F32)],
        sem=("arbitrary",), args=(x, x, g, w, b, scale), carried=carried)


def _pool_bwd(x, dh, g, w, b, scale, carried=None):
    t = x.shape[0]
    tt = min(t, 256)
    nt = t // tt
    main, halo, vec, wsp = _pool_specs(tt, nt, True)

    def body(xm_ref, xh_ref, dh_ref, g_ref, w_ref, b_ref, s_ref,
             dx_ref, dw_ref, db_ref, ds_ref, dg_ref, carry, dw_acc):
        i = pl.program_id(0)
        tile = nt - 1 - i

        @pl.when(i == 0)
        def _():
            carry[...] = jnp.zeros_like(carry)
            dw_acc[...] = jnp.zeros_like(dw_acc)
            db_ref[...] = jnp.zeros_like(db_ref)
            ds_ref[...] = jnp.zeros_like(ds_ref)
            dg_ref[...] = jnp.zeros_like(dg_ref)

        xe = jnp.concatenate([xh_ref[...], xm_ref[...]], axis=0)
        _, vjp = jax.vjp(lambda a, gg, ww, bb, ss: _pool_tile(a, gg, ww, bb, ss, tile, tt),
                         xe, g_ref[...], _pool_weights(w_ref), b_ref[...], s_ref[...])
        dxe, dgv, dws, dbv, dsv = vjp(dh_ref[...])
        dx_ref[...] = dxe[POOL_HALO:, :]
        dx_ref[tt - POOL_HALO:tt, :] += carry[...]
        carry[...] = dxe[:POOL_HALO, :]
        for gi in range(4):
            dw_acc[gi] += dws[gi]
        db_ref[...] += dbv
        ds_ref[...] += dsv
        dg_ref[...] += dgv

        @pl.when(i == nt - 1)
        def _():
            for k in range(N_DEV):
                for gi in range(4):
                    dw_ref[k, gi * POOL_SHARD:(gi + 1) * POOL_SHARD, :] = dw_acc[gi, k * POOL_SHARD:(k + 1) * POOL_SHARD, :]

    return _pcall(
        body, name="pool_bwd", grid=(nt,),
        in_specs=[main, halo, main, vec, wsp, vec, vec],
        out_specs=[main, wsp, vec, vec, vec],
        out_shape=[jax.ShapeDtypeStruct((t, D_MODEL), F32),
                   jax.ShapeDtypeStruct((N_DEV, 4 * POOL_SHARD, POOL_GROUP), F32),
                   jax.ShapeDtypeStruct((1, D_MODEL), F32),
                   jax.ShapeDtypeStruct((1, D_MODEL), F32),
                   jax.ShapeDtypeStruct((1, D_MODEL), F32)],
        scratch_shapes=[pltpu.VMEM((POOL_HALO, D_MODEL), F32), pltpu.VMEM((4, POOL_GROUP, POOL_GROUP), F32)],
        sem=("arbitrary",), args=(x, x, dh, g, w, b, scale), carried=carried)


def _mlp_weight_specs():
    fb = D_FF // N_DEV
    return (pl.BlockSpec((None, D_MODEL, fb), lambda i, k: (k, 0, 0)),
            pl.BlockSpec((None, fb, D_MODEL), lambda i, k: (k, 0, 0)))


def _mlp_fwd(h, g, w1g, w2g, name, carried=None):
    t = h.shape[0]
    tt = min(t, MATMUL_TOKENS)
    nk, fb = N_DEV, D_FF // N_DEV
    w1_spec, w2_spec = _mlp_weight_specs()

    def body(h_ref, g_ref, w1_ref, w2_ref, o_ref, u_ref, hm_ref, hm_s, acc_s):
        k = pl.program_id(1)

        @pl.when(k == 0)
        def _():
            xv = h_ref[...]
            y, _, _ = _rms_fwd(xv, g_ref[...])
            hb = y.astype(BF16)
            hm_s[...] = hb
            hm_ref[...] = hb
            acc_s[...] = xv

        a = jnp.dot(hm_s[...], w1_ref[...], preferred_element_type=F32)
        u = jnp.maximum(a, 0.0)
        u_ref[...] = u.astype(BF16)
        acc_s[...] += jnp.dot((u * u).astype(BF16), w2_ref[...], preferred_element_type=F32)

        @pl.when(k == nk - 1)
        def _():
            o_ref[...] = acc_s[...]

    return _pcall(
        body, name=name, grid=(t // tt, nk),
        in_specs=[pl.BlockSpec((tt, D_MODEL), lambda i, k: (i, 0)),
                  pl.BlockSpec((1, D_MODEL), lambda i, k: (0, 0)),
                  w1_spec, w2_spec],
        out_specs=[pl.BlockSpec((tt, D_MODEL), lambda i, k: (i, 0)),
                   pl.BlockSpec((tt, fb), lambda i, k: (i, k)),
                   pl.BlockSpec((tt, D_MODEL), lambda i, k: (i, 0))],
        out_shape=[jax.ShapeDtypeStruct((t, D_MODEL), F32),
                   jax.ShapeDtypeStruct((t, nk * fb), BF16),
                   jax.ShapeDtypeStruct((t, D_MODEL), BF16)],
        scratch_shapes=[pltpu.VMEM((tt, D_MODEL), BF16), pltpu.VMEM((tt, D_MODEL), F32)],
        sem=("arbitrary", "arbitrary"), args=(h, g, w1g, w2g), carried=carried)


def _mlp_bwd(dh, dhb, h, g, u, w1g, w2g, name, carried=None):
    t = h.shape[0]
    tt = min(t, MATMUL_TOKENS)
    nk, fb = N_DEV, D_FF // N_DEV
    w1_spec, w2_spec = _mlp_weight_specs()

    def body(dh_ref, dhb_ref, h_ref, g_ref, u_ref, w1_ref, w2_ref,
             dhin_ref, dhinb_ref, da_ref, dg_ref, acc_s):
        i = pl.program_id(0)
        k = pl.program_id(1)

        @pl.when(jnp.logical_and(i == 0, k == 0))
        def _():
            dg_ref[...] = jnp.zeros_like(dg_ref)

        @pl.when(k == 0)
        def _():
            acc_s[...] = jnp.zeros_like(acc_s)

        dv = lax.dot_general(dhb_ref[...], w2_ref[...], _NT, preferred_element_type=F32)
        dab = (dv * u_ref[...].astype(F32)).astype(BF16)
        da_ref[...] = dab
        acc_s[...] += lax.dot_general(dab, w1_ref[...], _NT, preferred_element_type=F32)

        @pl.when(k == nk - 1)
        def _():
            gv = g_ref[...]
            _, n, r = _rms_fwd(h_ref[...], gv)
            dx, dg = _rms_bwd(2.0 * acc_s[...], n, r, gv)
            dhin = dh_ref[...] + dx
            dhin_ref[...] = dhin
            dhinb_ref[...] = dhin.astype(BF16)
            dg_ref[...] += dg

    tile = pl.BlockSpec((tt, D_MODEL), lambda i, k: (i, 0))
    return _pcall(
        body, name=name, grid=(t // tt, nk),
        in_specs=[tile, tile, tile, pl.BlockSpec((1, D_MODEL), lambda i, k: (0, 0)),
                  pl.BlockSpec((tt, fb), lambda i, k: (i, k)), w1_spec, w2_spec],
        out_specs=[tile, tile, pl.BlockSpec((tt, fb), lambda i, k: (i, k)),
                   pl.BlockSpec((1, D_MODEL), lambda i, k: (0, 0))],
        out_shape=[jax.ShapeDtypeStruct((t, D_MODEL), F32),
                   jax.ShapeDtypeStruct((t, D_MODEL), BF16),
                   jax.ShapeDtypeStruct((t, nk * fb), BF16),
                   jax.ShapeDtypeStruct((1, D_MODEL), F32)],
        scratch_shapes=[pltpu.VMEM((tt, D_MODEL), F32)],
        sem=("arbitrary", "arbitrary"), args=(dh, dhb, h, g, u, w1g, w2g), carried=carried)


def _matmul_tn(a, b, name, square_a=False, col_blocked=False, carried=None, scale=None):
    t, k1 = a.shape
    k2 = b.shape[1]
    tt = min(t, TN_TOKENS)
    nt = t // tt
    wc = k2 if k1 * k2 * 4 <= TN_ACC_BYTES else k2 // 2
    nb = wc // COL_BLK

    def body(a_ref, b_ref, o_ref, acc):
        s = pl.program_id(1)

        @pl.when(s == 0)
        def _():
            acc[...] = jnp.zeros_like(acc)

        av = a_ref[...]
        if square_a:
            af = av.astype(F32)
            av = (af * af).astype(BF16)
        acc[...] += lax.dot_general(av, b_ref[...], _TN, preferred_element_type=F32)

        @pl.when(s == nt - 1)
        def _():
            def done(v):
                return (v if scale is None else scale * v).astype(o_ref.dtype)

            if col_blocked:
                for k in range(nb):
                    o_ref[k] = done(acc[:, k * COL_BLK:(k + 1) * COL_BLK])
            else:
                o_ref[...] = done(acc[...])

    if col_blocked:
        out_shape = jax.ShapeDtypeStruct((k2 // COL_BLK, k1, COL_BLK), BF16)
        out_spec = pl.BlockSpec((nb, k1, COL_BLK), lambda j, s: (j, 0, 0))
    else:
        out_shape = jax.ShapeDtypeStruct((k1, k2), BF16)
        out_spec = pl.BlockSpec((k1, wc), lambda j, s: (0, j))
    outs, landed = _pcall(
        body, name=name, grid=(k2 // wc, nt),
        in_specs=[pl.BlockSpec((tt, k1), lambda j, s: (s, 0)),
                  pl.BlockSpec((tt, wc), lambda j, s: (s, j))],
        out_specs=[out_spec], out_shape=[out_shape],
        scratch_shapes=[pltpu.VMEM((k1, wc), F32)],
        sem=("arbitrary", "arbitrary"), args=(a, b), carried=carried)
    return (outs[0], landed) if carried is not None else outs[0]


def _norm_matmul(h, g, w, carried=None):
    t = h.shape[0]
    tt = min(t, MATMUL_TOKENS)
    n = w.shape[1]

    def body(h_ref, g_ref, w_ref, o_ref, hn_ref, hn_s):
        @pl.when(pl.program_id(1) == 0)
        def _():
            y, _, _ = _rms_fwd(h_ref[...], g_ref[...])
            hb = y.astype(BF16)
            hn_s[...] = hb
            hn_ref[...] = hb

        o_ref[...] = jnp.dot(hn_s[...], w_ref[...], preferred_element_type=F32)

    return _pcall(
        body, name="ssm_in_proj", grid=(t // tt, n // PROJ_BLK),
        in_specs=[pl.BlockSpec((tt, D_MODEL), lambda i, j: (i, 0)),
                  pl.BlockSpec((1, D_MODEL), lambda i, j: (0, 0)),
                  pl.BlockSpec((D_MODEL, PROJ_BLK), lambda i, j: (0, j))],
        out_specs=[pl.BlockSpec((tt, PROJ_BLK), lambda i, j: (i, j)),
                   pl.BlockSpec((tt, D_MODEL), lambda i, j: (i, 0))],
        out_shape=[jax.ShapeDtypeStruct((t, n), F32), jax.ShapeDtypeStruct((t, D_MODEL), BF16)],
        scratch_shapes=[pltpu.VMEM((tt, D_MODEL), BF16)],
        sem=("arbitrary", "arbitrary"), args=(h, g, w), carried=carried)


def _in_proj_bwd(dzx, w, h, g, dh_next, carried=None):
    t = h.shape[0]
    tt = min(t, MATMUL_TOKENS)
    n = w.shape[1]
    nj = n // PROJ_BLK

    def body(dz_ref, w_ref, h_ref, g_ref, dn_ref, dh_ref, dhb_ref, dg_ref, acc):
        i = pl.program_id(0)
        j = pl.program_id(1)

        @pl.when(jnp.logical_and(i == 0, j == 0))
        def _():
            dg_ref[...] = jnp.zeros_like(dg_ref)

        @pl.when(j == 0)
        def _():
            acc[...] = jnp.zeros_like(acc)

        acc[...] += lax.dot_general(dz_ref[...], w_ref[...], _NT, preferred_element_type=F32)

        @pl.when(j == nj - 1)
        def _():
            gv = g_ref[...]
            _, nn, r = _rms_fwd(h_ref[...], gv)
            dx, dg = _rms_bwd(acc[...], nn, r, gv)
            dh = dn_ref[...] + dx
            dh_ref[...] = dh
            dhb_ref[...] = dh.astype(BF16)
            dg_ref[...] += dg

    tile = pl.BlockSpec((tt, D_MODEL), lambda i, j: (i, 0))
    return _pcall(
        body, name="ssm_in_proj_bwd", grid=(t // tt, nj),
        in_specs=[pl.BlockSpec((tt, PROJ_BLK), lambda i, j: (i, j)),
                  pl.BlockSpec((D_MODEL, PROJ_BLK), lambda i, j: (0, j)),
                  tile, pl.BlockSpec((1, D_MODEL), lambda i, j: (0, 0)), tile],
        out_specs=[tile, tile, pl.BlockSpec((1, D_MODEL), lambda i, j: (0, 0))],
        out_shape=[jax.ShapeDtypeStruct((t, D_MODEL), F32), jax.ShapeDtypeStruct((t, D_MODEL), BF16),
                   jax.ShapeDtypeStruct((1, D_MODEL), F32)],
        scratch_shapes=[pltpu.VMEM((tt, D_MODEL), F32)],
        sem=("arbitrary", "arbitrary"), args=(dzx, w, h, g, dh_next), carried=carried)


def _ssd_consts():
    lane = lax.broadcasted_iota(jnp.int32, (CHUNK, CHUNK), 1)
    row = lax.broadcasted_iota(jnp.int32, (CHUNK, CHUNK), 0)
    causal = lane <= row
    tri = _one(causal).astype(BF16)
    er = lax.broadcasted_iota(jnp.int32, (CHUNK, GROUP_X), 0)
    ec = lax.broadcasted_iota(jnp.int32, (CHUNK, GROUP_X), 1)
    expand = _one(jnp.right_shift(ec, 6) == er).astype(BF16)
    return dict(causal=causal, tri=tri, expand=expand, lo=lane < HEAD_DIM)


def _conv_silu(cur, prev, w, b):
    ext = jnp.concatenate([prev, cur], axis=0)
    acc = cur * w[3] + b
    for j in (1, 2, 3):
        acc = acc + _SHIFT[j](ext) * w[3 - j]
    return _silu(acc)


def _ssd_chunk(raw, rawp, ht, cw, cb_, dtb, alog, dsk, k):
    act = _conv_silu(raw[:, :GROUP_CONV], rawp[:, :GROUP_CONV], cw, cb_)
    xs = act[:, :GROUP_X]
    bm = act[:, GROUP_X:GROUP_X + D_STATE]
    cm = act[:, GROUP_X + D_STATE:]
    dt = _softplus(raw[:, GROUP_CONV:] + dtb)
    a = -jnp.exp(alog)
    xc = _CMM["xc"]

    def lanes(rowv):
        return jnp.sum(xc(jnp.broadcast_to(rowv, (16, CHUNK)), k["expand"]), axis=0, keepdims=True) * (1.0 / 16.0)

    dt_e = xc(dt, k["expand"])
    adt_e = dt_e * lanes(a)
    acs_e = _CMM["cx"](adt_e, k["tri"])
    tot_e = jnp.sum(adt_e, axis=0, keepdims=True)
    gmat = mm_nt(cm, bm)
    xdt = xs * dt_e
    ys = []
    for j in range(HEADS_PER_GROUP // 2):
        pair = acs_e[:, j * CHUNK:(j + 1) * CHUNK]
        swapped = _swap_halves(pair)
        ms = []
        for cb in (jnp.where(k["lo"], pair, swapped), jnp.where(k["lo"], swapped, pair)):
            seg = cb - cb.T
            ms.append(gmat * jnp.exp(jnp.where(k["causal"], seg, -jnp.inf)))
        xp = xdt[:, j * CHUNK:(j + 1) * CHUNK]
        rhs = jnp.concatenate([jnp.where(k["lo"], xp, 0.0), jnp.where(k["lo"], 0.0, xp)], axis=0)
        ys.append(mm_nn(jnp.concatenate(ms, axis=1), rhs))
    y_diag = jnp.concatenate(ys, axis=1)
    y_off = jnp.exp(acs_e) * mm_nn(cm, ht)
    h_new = jnp.exp(tot_e) * ht + mm_tn(bm, xdt * jnp.exp(tot_e - acs_e))
    return y_diag + y_off + lanes(dsk) * xs, h_new


def _ssd_in_specs(nc, rev):
    c_of = (lambda c: nc - 1 - c) if rev else (lambda c: c)
    per = CHUNK // CONV_HALO
    zx = [pl.BlockSpec((CHUNK, GROUP_COLS), lambda g, c: (c_of(c), g)),
          pl.BlockSpec((CONV_HALO, GROUP_COLS), lambda g, c: (jnp.maximum(c_of(c) * per - 1, 0), g))]
    conv = [pl.BlockSpec((4, GROUP_CONV), lambda g, c: (0, g)), pl.BlockSpec((1, GROUP_CONV), lambda g, c: (0, g))]
    head = [pl.BlockSpec((None, 1, 128), lambda g, c: (g, 0, 0))] * 3
    return zx + conv + head, c_of


def _load_chunk_args(refs, has_prev):
    raw, rawp, cw, cb_, dtb, alog, dsk = refs
    return (raw[...], rawp[...] * has_prev, tuple(cw[pl.ds(i, 1), :] for i in range(4)), cb_[...],
            dtb[...], alog[...], dsk[...])


def _ssd_fwd(zx, conv_w, conv_b, dtb, alog, dsk, carried=None):
    t = zx.shape[0]
    nc = t // CHUNK
    in_specs, _ = _ssd_in_specs(nc, False)

    def body(*refs):
        ins, (y_ref, hs_ref, ht) = refs[:7], refs[7:]
        c = pl.program_id(1)

        @pl.when(c == 0)
        def _():
            ht[...] = jnp.zeros_like(ht)

        a = _load_chunk_args(ins, _one(c > 0))
        h_in = ht[...]
        y, h_new = _ssd_chunk(*a[:2], h_in, *a[2:], _ssd_consts())
        y_ref[...] = y
        hs_ref[...] = h_in
        ht[...] = h_new

    return _pcall(
        body, name="ssd_fwd", grid=(N_GROUPS, nc),
        in_specs=in_specs,
        out_specs=[pl.BlockSpec((CHUNK, GROUP_X), lambda g, c: (c, g)),
                   pl.BlockSpec((None, None, D_STATE, GROUP_X), lambda g, c: (g, c, 0, 0))],
        out_shape=[jax.ShapeDtypeStruct((t, D_INNER), F32),
                   jax.ShapeDtypeStruct((N_GROUPS, nc, D_STATE, GROUP_X), F32)],
        scratch_shapes=[pltpu.VMEM((D_STATE, GROUP_X), F32)],
        sem=("arbitrary", "arbitrary"), args=(zx, zx, conv_w, conv_b, dtb, alog, dsk), carried=carried)


def _ssd_bwd(zx, conv_w, conv_b, dtb, alog, dsk, hs, dy, dzx, carried=None):
    t = zx.shape[0]
    nc = t // CHUNK
    in_specs, c_of = _ssd_in_specs(nc, True)
    n_in = 10

    def body(*refs):
        ins, hs_ref, dy_ref = refs[:7], refs[7], refs[8]
        (draw_ref, dcw, dcb, ddtb, dalog, ddsk, dht, carry) = refs[n_in:]
        cc = pl.program_id(1)
        accs = (dcw, dcb, ddtb, dalog, ddsk)

        @pl.when(cc == 0)
        def _():
            for r in (dht, carry) + accs:
                r[...] = jnp.zeros_like(r)

        has_prev = _one(c_of(cc) > 0)
        a = _load_chunk_args(ins, has_prev)
        k = _ssd_consts()
        fn = lambda *args: _ssd_chunk(*args, k)
        _, vjp = jax.vjp(fn, *a[:2], hs_ref[...], *a[2:])
        graw, grawp, ght, gcw, gcb, gdtb, galog, gdsk = vjp((dy_ref[...], dht[...]))
        tail = jnp.concatenate([jnp.zeros((CHUNK - CONV_HALO, GROUP_COLS), F32), carry[...]], axis=0)
        draw_ref[...] = (graw + tail).astype(BF16)
        carry[...] = grawp * has_prev
        dht[...] = ght
        for i in range(4):
            dcw[pl.ds(i, 1), :] += gcw[i]
        for ref, val in ((dcb, gcb), (ddtb, gdtb), (dalog, galog), (ddsk, gdsk)):
            ref[...] += val

    head_out = pl.BlockSpec((None, 1, 128), lambda g, c: (g, 0, 0))
    sds = jax.ShapeDtypeStruct
    return _pcall(
        body, name="ssd_bwd", grid=(N_GROUPS, nc),
        in_specs=in_specs + [
            pl.BlockSpec((None, None, D_STATE, GROUP_X), lambda g, c: (g, c_of(c), 0, 0)),
            pl.BlockSpec((CHUNK, GROUP_X), lambda g, c: (c_of(c), g)),
            _ANY],
        out_specs=[pl.BlockSpec((CHUNK, GROUP_COLS), lambda g, c: (c_of(c), g)),
                   pl.BlockSpec((4, GROUP_CONV), lambda g, c: (0, g)),
                   pl.BlockSpec((1, GROUP_CONV), lambda g, c: (0, g)),
                   head_out, head_out, head_out],
        out_shape=[sds((t, ZX_COLS), BF16), sds((4, N_GROUPS * GROUP_CONV), F32), sds((1, N_GROUPS * GROUP_CONV), F32),
                   sds((N_GROUPS, 1, 128), F32), sds((N_GROUPS, 1, 128), F32), sds((N_GROUPS, 1, 128), F32)],
        scratch_shapes=[pltpu.VMEM((D_STATE, GROUP_X), F32), pltpu.VMEM((CONV_HALO, GROUP_COLS), F32)],
        sem=("arbitrary", "arbitrary"), args=(zx, zx, conv_w, conv_b, dtb, alog, dsk, hs, dy, dzx),
        aliases={9: 0}, carried=carried)


def _gate_norm(y, zs, ng):
    outs = []
    for k in range(N_GROUPS):
        s = y[:, k * GROUP_X:(k + 1) * GROUP_X] * _silu(zs[k])
        outs.append(s * lax.rsqrt(jnp.mean(s * s, axis=-1, keepdims=True) + RMS_EPS))
    return jnp.concatenate(outs, axis=1) * ng


def _z_specs(tt):
    first = Z_OFF // GROUP_X
    return [pl.BlockSpec((tt, GROUP_X), functools.partial(lambda k, i: (i, first + k), k)) for k in range(N_GROUPS)]


def _ssm_out_fwd(y, zx, ng, w_out, h):
    t = h.shape[0]
    tt = min(t, 512)

    def body(y_ref, z0, z1, z2, z3, ng_ref, w_ref, h_ref, o_ref):
        yn = _gate_norm(y_ref[...], (z0[...], z1[...], z2[...], z3[...]), ng_ref[...])
        o_ref[...] = h_ref[...] + jnp.dot(yn.astype(BF16), w_ref[...], preferred_element_type=F32)

    return pl.pallas_call(
        body, name="ssm_out_fwd", grid=(t // tt,),
        in_specs=[pl.BlockSpec((tt, D_INNER), lambda i: (i, 0))] + _z_specs(tt) + [
            pl.BlockSpec((1, D_INNER), lambda i: (0, 0)),
            pl.BlockSpec((D_INNER, D_MODEL), lambda i: (0, 0)),
            pl.BlockSpec((tt, D_MODEL), lambda i: (i, 0))],
        out_specs=pl.BlockSpec((tt, D_MODEL), lambda i: (i, 0)),
        out_shape=jax.ShapeDtypeStruct((t, D_MODEL), F32),
        compiler_params=_cp(("arbitrary",)),
    )(y, zx, zx, zx, zx, ng, w_out, h)


def _gate_norm_group(y, z, ng):
    s = y * _silu(z)
    return s * lax.rsqrt(jnp.mean(s * s, axis=-1, keepdims=True) + RMS_EPS) * ng


def _ssm_out_bwd(dhb, y, zx, ng, w_out):
    t = dhb.shape[0]
    tt = min(t, MATMUL_TOKENS)
    first = Z_OFF // GROUP_X

    def body(dh_ref, y_ref, z_ref, ng_ref, w_ref, dy_ref, dzx_ref, yn_ref, dng_ref):
        @pl.when(pl.program_id(1) == 0)
        def _():
            dng_ref[...] = jnp.zeros_like(dng_ref)

        dyn = lax.dot_general(dh_ref[...], w_ref[...], _NT, preferred_element_type=F32)
        yn, vjp = jax.vjp(_gate_norm_group, y_ref[...], z_ref[...], ng_ref[...])
        dy, dz, dng = vjp(dyn)
        dy_ref[...] = dy
        dzx_ref[...] = dz.astype(BF16)
        yn_ref[...] = yn.astype(BF16)
        dng_ref[...] += dng

    grp = pl.BlockSpec((tt, GROUP_X), lambda k, i: (i, k))
    zgrp = pl.BlockSpec((tt, GROUP_X), lambda k, i: (i, first + k))
    gain = pl.BlockSpec((1, GROUP_X), lambda k, i: (0, k))
    return pl.pallas_call(
        body, name="ssm_out_bwd", grid=(N_GROUPS, t // tt),
        in_specs=[pl.BlockSpec((tt, D_MODEL), lambda k, i: (i, 0)), grp, zgrp, gain,
                  pl.BlockSpec((GROUP_X, D_MODEL), lambda k, i: (k, 0))],
        out_specs=[grp, zgrp, grp, gain],
        out_shape=[jax.ShapeDtypeStruct((t, D_INNER), F32), jax.ShapeDtypeStruct((t, ZX_COLS), BF16),
                   jax.ShapeDtypeStruct((t, D_INNER), BF16), jax.ShapeDtypeStruct((1, D_INNER), F32)],
        compiler_params=_cp(("arbitrary", "arbitrary")),
    )(dhb, y, zx, ng, w_out)


def _final(h, g, tgt):
    t = h.shape[0]
    tt = min(t, 512)
    nt = t // tt

    def body(h_ref, g_ref, t_ref, dh_ref, dhb_ref, loss_ref, dg_ref, lacc):
        i = pl.program_id(0)

        @pl.when(i == 0)
        def _():
            dg_ref[...] = jnp.zeros_like(dg_ref)
            lacc[...] = jnp.zeros_like(lacc)

        gv = g_ref[...]
        y, n, r = _rms_fwd(h_ref[...], gv)
        err = y - t_ref[...]
        lacc[...] += jnp.sum(err * err, axis=0, keepdims=True)
        dx, dg = _rms_bwd(err * (1.0 / D_MODEL), n, r, gv)
        dh_ref[...] = dx
        dhb_ref[...] = dx.astype(BF16)
        dg_ref[...] += dg

        @pl.when(i == nt - 1)
        def _():
            loss_ref[...] = jnp.zeros_like(loss_ref) + (0.5 / D_MODEL) * jnp.sum(lacc[...])

    tile = pl.BlockSpec((tt, D_MODEL), lambda i: (i, 0))
    vec = pl.BlockSpec((1, D_MODEL), lambda i: (0, 0))
    return pl.pallas_call(
        body, name="final_loss", grid=(nt,),
        in_specs=[tile, vec, tile],
        out_specs=[tile, tile, pl.BlockSpec((1, 128), lambda i: (0, 0)), vec],
        out_shape=[jax.ShapeDtypeStruct((t, D_MODEL), F32), jax.ShapeDtypeStruct((t, D_MODEL), BF16),
                   jax.ShapeDtypeStruct((1, 128), F32), jax.ShapeDtypeStruct((1, D_MODEL), F32)],
        scratch_shapes=[pltpu.VMEM((1, D_MODEL), F32)],
        compiler_params=_cp(("arbitrary",)),
    )(h, g, tgt)


def _adamw_reduced_parts(w, lands, m, v, name):
    rows, cols = w.shape
    br = 128
    nl = lands[0].shape[0]
    starts, blocks = [], []
    for land in lands:
        starts.append(sum(blocks))
        blocks.append(land.shape[1] // br)

    def body(w_ref, *refs):
        l_refs, (m_ref, v_ref, g_ref, d_ref, m2_ref, v2_ref) = refs[:len(lands)], refs[len(lands):]
        i = pl.program_id(0)
        gv = None
        for ref, first in zip(l_refs, starts):
            acc = ref[0].astype(F32)
            for q in range(1, nl):
                acc = acc + ref[q].astype(F32)
            gv = acc if gv is None else jnp.where(i >= first, acc, gv)
        g_ref[...] = gv
        d_ref[...], m2_ref[...], v2_ref[...] = _adamw_math(w_ref[...], gv, m_ref[...], v_ref[...])

    spec = pl.BlockSpec((br, cols), lambda i: (i, 0))
    land_specs = [pl.BlockSpec((nl, br, cols), functools.partial(
        lambda first, nb, i: (0, jnp.clip(i - first, 0, nb - 1), 0), first, nb)) for first, nb in zip(starts, blocks)]
    out = jax.ShapeDtypeStruct((rows, cols), F32)
    return pl.pallas_call(
        body, name=name, grid=(rows // br,),
        in_specs=[spec] + land_specs + [spec, spec], out_specs=[spec] * 4, out_shape=[out] * 4,
        compiler_params=_cp(("arbitrary",)),
    )(w, *lands, m, v)


def _adamw_reduced_layers(w, lands, m, v, name):
    _, rows, cols = w.shape
    br = rows if rows <= 256 else 256
    nb = rows // br
    nl = lands[0].shape[0]

    def body(w_ref, l0_ref, l1_ref, m_ref, v_ref, g_ref, d_ref, m2_ref, v2_ref):
        def total(ref):
            acc = ref[0].astype(F32)
            for q in range(1, nl):
                acc = acc + ref[q].astype(F32)
            return acc

        gv = jnp.where(pl.program_id(0) == 0, total(l0_ref), total(l1_ref))
        g_ref[...] = gv
        d_ref[...], m2_ref[...], v2_ref[...] = _adamw_math(w_ref[...], gv, m_ref[...], v_ref[...])

    spec = pl.BlockSpec((None, br, cols), lambda l, i: (l, i, 0))
    land0 = pl.BlockSpec((nl, br, cols), lambda l, i: (0, jnp.where(l == 0, i, nb - 1), 0))
    land1 = pl.BlockSpec((nl, br, cols), lambda l, i: (0, jnp.where(l == 1, i, 0), 0))
    out = jax.ShapeDtypeStruct(w.shape, F32)
    return pl.pallas_call(
        body, name=name, grid=(2, nb),
        in_specs=[spec, land0, land1, spec, spec], out_specs=[spec] * 4, out_shape=[out] * 4,
        compiler_params=_cp(("arbitrary", "arbitrary")),
    )(w, lands[0], lands[1], m, v)


def _all_reduce_small(sp):
    rows, n = sp.shape

    def body(x_ref, o_ref, land, send_sems, recv_sems):
        x, y, c = _place()
        me = 4 * x + 2 * y + c
        land[me] = x_ref[...]
        cps = []
        for rel in range(1, N_DEV):
            dx, dy, dc = (rel >> 2) & 1, (rel >> 1) & 1, rel & 1
            px = x + dx - 2 * x * dx
            py = y + dy - 2 * y * dy
            pc = c + dc - 2 * c * dc
            peer = 4 * px + 2 * py + pc
            cps.append((pltpu.make_async_remote_copy(
                src_ref=x_ref, dst_ref=land.at[me], send_sem=send_sems.at[rel - 1], recv_sem=recv_sems.at[rel - 1],
                device_id=(px, py, pc), device_id_type=MESH),
                pltpu.make_async_remote_copy(
                src_ref=x_ref, dst_ref=land.at[peer], send_sem=send_sems.at[rel - 1], recv_sem=recv_sems.at[rel - 1],
                device_id=(px, py, pc), device_id_type=MESH)))
        for cp, _ in cps:
            cp.start()
        for _, arr in cps:
            arr.wait_recv()
        for cp, _ in cps:
            cp.wait_send()
        acc = land[0]
        for k in range(1, N_DEV):
            acc = acc + land[k]
        o_ref[...] = acc

    vm = pl.BlockSpec(memory_space=pltpu.VMEM)
    return pl.pallas_call(
        body, name="all_reduce_small",
        out_shape=jax.ShapeDtypeStruct((rows, n), F32),
        in_specs=[vm], out_specs=vm,
        scratch_shapes=[pltpu.VMEM((N_DEV, rows, n), F32),
                        pltpu.SemaphoreType.DMA((N_DEV - 1,)), pltpu.SemaphoreType.DMA((N_DEV - 1,))],
    )(sp)


def _adamw_math(wv, gv, mv, vv):
    m2 = ADAM_B1 * mv + (1.0 - ADAM_B1) * gv
    v2 = ADAM_B2 * vv + (1.0 - ADAM_B2) * (gv * gv)
    m_hat = m2 / (1.0 - ADAM_B1 ** ADAM_STEP)
    v_hat = v2 / (1.0 - ADAM_B2 ** ADAM_STEP)
    return -ADAM_LR * (m_hat / (jnp.sqrt(v_hat) + ADAM_EPS) + ADAM_WD * wv), m2, v2


def _adamw(w, g, m, v, name):
    rows, cols = w.shape
    br = rows if rows <= 256 else 256

    def body(w_ref, g_ref, m_ref, v_ref, d_ref, m2_ref, v2_ref):
        d_ref[...], m2_ref[...], v2_ref[...] = _adamw_math(w_ref[...], g_ref[...], m_ref[...], v_ref[...])

    spec = pl.BlockSpec((br, cols), lambda i: (i, 0))
    out = jax.ShapeDtypeStruct((rows, cols), F32)
    return pl.pallas_call(
        body, name=name, grid=(rows // br,),
        in_specs=[spec] * 4, out_specs=[spec] * 3, out_shape=[out] * 3,
        compiler_params=_cp(("arbitrary",)),
    )(w, g, m, v)


def _adamw_reduced(w, land, m, v, name):
    rows, cols = w.shape
    br = rows if rows <= 256 else 256
    nl = land.shape[0]

    def body(w_ref, l_ref, m_ref, v_ref, g_ref, d_ref, m2_ref, v2_ref):
        gv = l_ref[0].astype(F32)
        for q in range(1, nl):
            gv = gv + l_ref[q].astype(F32)
        g_ref[...] = gv
        d_ref[...], m2_ref[...], v2_ref[...] = _adamw_math(w_ref[...], gv, m_ref[...], v_ref[...])

    spec = pl.BlockSpec((br, cols), lambda i: (i, 0))
    out = jax.ShapeDtypeStruct((rows, cols), F32)
    return pl.pallas_call(
        body, name=name, grid=(rows // br,),
        in_specs=[spec, pl.BlockSpec((nl, br, cols), lambda i: (0, i, 0)), spec, spec],
        out_specs=[spec] * 4, out_shape=[out] * 4,
        compiler_params=_cp(("arbitrary",)),
    )(w, land, m, v)


def _zx_source_col(col):
    blk = jnp.right_shift(col, 7)
    lane = jnp.bitwise_and(col, 127)
    per = GROUP_COLS // 128
    grp = jnp.where(blk >= per, 1, 0) + jnp.where(blk >= 2 * per, 1, 0) + jnp.where(blk >= 3 * per, 1, 0)
    o = blk - per * grp
    x_col = D_INNER + GROUP_X * grp + 128 * o + lane
    b_col = 2 * D_INNER + D_STATE * grp + lane
    c_col = 2 * D_INNER + N_GROUPS * D_STATE + D_STATE * grp + lane
    dt_col = jnp.where(lane < HEADS_PER_GROUP, D_INNER + CONV_DIM + HEADS_PER_GROUP * grp + lane, -1)
    src = jnp.where(o < 4, x_col, jnp.where(o == 4, b_col, jnp.where(o == 5, c_col, dt_col)))
    return jnp.where(col >= Z_OFF, col - Z_OFF, src)


def _zx_source_col_py(col):
    if col >= Z_OFF:
        return col - Z_OFF
    grp, o = divmod(col, GROUP_COLS)
    if o < GROUP_X:
        return D_INNER + GROUP_X * grp + o
    if o < GROUP_X + D_STATE:
        return 2 * D_INNER + D_STATE * grp + (o - GROUP_X)
    if o < GROUP_CONV:
        return 2 * D_INNER + N_GROUPS * D_STATE + D_STATE * grp + (o - GROUP_X - D_STATE)
    h = o - GROUP_CONV
    return D_INNER + CONV_DIM + HEADS_PER_GROUP * grp + h if h < HEADS_PER_GROUP else -1


def _overlap_tables():
    nblk = ZX_COLS // COL_BLK
    src = [_zx_source_col_py(c) for c in range(ZX_COLS)]
    fwd = [sorted({s // W_IN_SHARD for s in src[COL_BLK * j:COL_BLK * (j + 1)] if s >= 0}) for j in range(nblk)]
    dst = {s: c for c, s in enumerate(src) if s >= 0}
    bwd = [sorted({dst[s] // COL_BLK for s in range(W_IN_SHARD * k, W_IN_SHARD * (k + 1))}) for k in range(N_DEV)]

    def flat(rows):
        width = max(len(r) for r in rows)
        idx = [r + [r[-1]] * (width - len(r)) for r in rows]
        val = [[1] * len(r) + [0] * (width - len(r)) for r in rows]
        return (jnp.asarray(sum(idx, []), jnp.int32), jnp.asarray(sum(val, []), jnp.int32), width)

    return flat(fwd), flat(bwd)


def _w_in_to_zx(w_in_g):
    (tab, val, width), _ = _overlap_tables()
    nblk = ZX_COLS // COL_BLK

    def body(tab_ref, val_ref, w_ref, o_ref, acc):
        j = pl.program_id(0)
        s = pl.program_id(1)

        @pl.when(s == 0)
        def _():
            acc[...] = jnp.zeros_like(acc)

        @pl.when(val_ref[j * width + s] == 1)
        def _():
            k = tab_ref[j * width + s]
            col = COL_BLK * j + lax.broadcasted_iota(jnp.int32, (8, COL_BLK), 1)
            src = jnp.broadcast_to(_zx_source_col(col)[0:1, :], (W_IN_SHARD, COL_BLK))
            row = W_IN_SHARD * k + lax.broadcasted_iota(jnp.int32, (W_IN_SHARD, COL_BLK), 0)
            place = _one(src == row).astype(BF16)
            acc[...] += jnp.dot(w_ref[...], place, preferred_element_type=F32)

        @pl.when(s == width - 1)
        def _():
            o_ref[...] = acc[...].astype(BF16)

    return pl.pallas_call(
        body, name="w_in_to_zx",
        grid_spec=pltpu.PrefetchScalarGridSpec(
            num_scalar_prefetch=2, grid=(nblk, width),
            in_specs=[pl.BlockSpec((None, D_MODEL, W_IN_SHARD), lambda j, s, tab, val: (tab[j * width + s], 0, 0))],
            out_specs=pl.BlockSpec((D_MODEL, COL_BLK), lambda j, s, tab, val: (0, j)),
            scratch_shapes=[pltpu.VMEM((D_MODEL, COL_BLK), F32)]),
        out_shape=jax.ShapeDtypeStruct((D_MODEL, ZX_COLS), BF16),
        compiler_params=_cp(("arbitrary", "arbitrary")),
    )(tab, val, w_in_g)


def _zx_to_w_in(d_wzx):
    _, (tab, val, width) = _overlap_tables()

    def body(tab_ref, val_ref, d_ref, o_ref, acc):
        k = pl.program_id(0)
        s = pl.program_id(1)

        @pl.when(s == 0)
        def _():
            acc[...] = jnp.zeros_like(acc)

        @pl.when(val_ref[k * width + s] == 1)
        def _():
            j = tab_ref[k * width + s]
            col = COL_BLK * j + lax.broadcasted_iota(jnp.int32, (COL_BLK, 128), 0)
            src = jnp.broadcast_to(_zx_source_col(col)[:, 0:1], (COL_BLK, W_IN_SHARD))
            row = W_IN_SHARD * k + lax.broadcasted_iota(jnp.int32, (COL_BLK, W_IN_SHARD), 1)
            place = _one(src == row).astype(BF16)
            acc[...] += jnp.dot(d_ref[...], place, preferred_element_type=F32)

        @pl.when(s == width - 1)
        def _():
            o_ref[...] = acc[...].astype(BF16)

    return pl.pallas_call(
        body, name="zx_to_w_in",
        grid_spec=pltpu.PrefetchScalarGridSpec(
            num_scalar_prefetch=2, grid=(N_DEV, width),
            in_specs=[pl.BlockSpec((D_MODEL, COL_BLK), lambda k, s, tab, val: (0, tab[k * width + s]))],
            out_specs=pl.BlockSpec((None, D_MODEL, W_IN_SHARD), lambda k, s, tab, val: (k, 0, 0)),
            scratch_shapes=[pltpu.VMEM((D_MODEL, W_IN_SHARD), F32)]),
        out_shape=jax.ShapeDtypeStruct((N_DEV, D_MODEL, W_IN_SHARD), BF16),
        compiler_params=_cp(("arbitrary", "arbitrary")),
    )(tab, val, d_wzx)


def _group_conv_cols(a):
    rows = a.shape[0]
    x = a[:, :D_INNER].reshape(rows, N_GROUPS, GROUP_X)
    b = a[:, D_INNER:D_INNER + N_GROUPS * D_STATE].reshape(rows, N_GROUPS, D_STATE)
    c = a[:, D_INNER + N_GROUPS * D_STATE:].reshape(rows, N_GROUPS, D_STATE)
    return jnp.concatenate([x, b, c], axis=2).reshape(rows, N_GROUPS * GROUP_CONV)


def _ungroup_conv_cols(a):
    rows = a.shape[0]
    a3 = a.reshape(rows, N_GROUPS, GROUP_CONV)
    return jnp.concatenate([a3[:, :, :GROUP_X].reshape(rows, D_INNER),
                            a3[:, :, GROUP_X:GROUP_X + D_STATE].reshape(rows, N_GROUPS * D_STATE),
                            a3[:, :, GROUP_X + D_STATE:].reshape(rows, N_GROUPS * D_STATE)], axis=1)


def _small_shard(conv_w, conv_b, norm_g):
    ng = jnp.pad(norm_g.reshape(1, -1), ((0, 0), (0, CONV_SHARD - norm_g.shape[-1])))
    return jnp.concatenate([conv_w.reshape(4, CONV_SHARD), conv_b.reshape(1, CONV_SHARD), ng,
                            jnp.zeros((SMALL_ROWS - 6, CONV_SHARD), F32)], axis=0)


def _small_unshard(a):
    return a[0:4].reshape(1, 4, CONV_SHARD), a[4:5], a[5:6, :D_INNER // N_DEV]


def _heads_of(a):
    return a[:, :, :HEADS_PER_GROUP].reshape(1, N_HEADS)


def _head_params(p):
    return jnp.pad(p.reshape(N_GROUPS, 1, HEADS_PER_GROUP), ((0, 0), (0, 0), (0, 128 - HEADS_PER_GROUP)))


def _update(w, land, m, v, name):
    shp = w.shape
    to2 = lambda a: a.reshape(-1, shp[-1])
    return tuple(o.reshape(shp) for o in _adamw_reduced(to2(w), land, to2(m), to2(v), name))


def kernel(x, norm_mix_g, norm_mlp_g, pool_w, pool_b, pool_scale, ssm_w_in, ssm_conv_w, ssm_conv_b, ssm_dt_bias, ssm_a_log, ssm_d, ssm_norm_g, ssm_w_out, mlp_w1, mlp_w2, final_g, loss_target, m_norm_mix_g, m_norm_mlp_g, m_pool_w, m_pool_b, m_pool_scale, m_ssm_w_in, m_ssm_conv_w, m_ssm_conv_b, m_ssm_dt_bias, m_ssm_a_log, m_ssm_d, m_ssm_norm_g, m_ssm_w_out, m_mlp_w1, m_mlp_w2, m_final_g, v_norm_mix_g, v_norm_mlp_g, v_pool_w, v_pool_b, v_pool_scale, v_ssm_w_in, v_ssm_conv_w, v_ssm_conv_b, v_ssm_dt_bias, v_ssm_a_log, v_ssm_d, v_ssm_norm_g, v_ssm_w_out, v_mlp_w1, v_mlp_w2, v_final_g):
    x2 = x[0]
    tgt = loss_target[0]
    gm0, gm1 = norm_mix_g[0:1], norm_mix_g[1:2]
    gl0, gl1 = norm_mlp_g[0:1], norm_mlp_g[1:2]
    gfin = final_g.reshape(1, D_MODEL)

    fb = D_FF // N_DEV

    def bf(a):
        return a.astype(BF16)

    def gather_of(shards):
        return _direct_exchange(shards, [(i, 0) for i in range(len(shards))],
                                [(s.shape, s.dtype) for s in shards], scatter=False)

    def scatter_of(parts, rows=None):
        shapes = [((p.shape[1] if rows is None else rows[1], p.shape[2]), p.dtype) for p in parts]
        return _direct_exchange(parts, [(i, 0) for i in range(len(parts))], shapes, scatter=True,
                                src_rows=None if rows is None else [rows] * len(parts))

    w_pool, small_g = _run_exchange(_two_level_gather(
        [bf(pool_w.reshape(4 * POOL_SHARD, POOL_GROUP)), _small_shard(ssm_conv_w, ssm_conv_b, ssm_norm_g)]),
        "gather_first")
    conv_w = _group_conv_cols(small_g[:, 0:4].transpose(1, 0, 2).reshape(4, CONV_DIM))
    conv_b = _group_conv_cols(small_g[:, 4].reshape(1, CONV_DIM))
    ssm_ng = small_g[:, 5, :D_INNER // N_DEV].reshape(1, D_INNER)
    dtb, alog, dsk = _head_params(ssm_dt_bias), _head_params(ssm_a_log), _head_params(ssm_d)

    (h1,), (w1g0, w2g0) = _pool_fwd(x2, gm0, w_pool, pool_b, pool_scale,
                                    carried=_two_level_gather([bf(mlp_w1[0]), bf(mlp_w2[0])], mid_percent=100))
    (h2, u0, hm0), (w_in_g,) = _mlp_fwd(h1, gl0, w1g0, w2g0, "mlp0_fwd",
                                        carried=_two_level_gather([bf(ssm_w_in[0])]))
    w_zx = _w_in_to_zx(w_in_g)
    (zx, hn1), (w_out_g,) = _norm_matmul(h2, gm1, w_zx, carried=gather_of([bf(ssm_w_out[0])]))
    (y_ssd, states), (w1g1, w2g1) = _ssd_fwd(zx, conv_w, conv_b, dtb, alog, dsk,
                                             carried=_two_level_gather([bf(mlp_w1[1]), bf(mlp_w2[1])]))
    w_out = w_out_g.reshape(D_INNER, D_MODEL)
    h3 = _ssm_out_fwd(y_ssd, zx, ssm_ng, w_out, h2)
    (h4, u1, hm1), _ = _mlp_fwd(h3, gl1, w1g1, w2g1, "mlp1_fwd")
    dh4, dh4b, loss_row, d_gfin = _final(h4, gfin, tgt)

    (dh3, dh3b, da1, d_gl1), _ = _mlp_bwd(dh4, dh4b, h3, gl1, u1, w1g1, w2g1, "mlp1_bwd")
    d_w1_1 = _matmul_tn(hm1, da1, "mlp1_dw1", col_blocked=True, scale=2.0)
    d_w2_1 = _matmul_tn(u1, dh4b, "mlp1_dw2", square_a=True).reshape(N_DEV, fb, D_MODEL)
    dy_ssd, dzx, yn, d_ng = _ssm_out_bwd(dh3b, y_ssd, zx, ssm_ng, w_out)
    d_wout = _matmul_tn(yn, dh3b, "ssm_dw_out").reshape(N_DEV, D_INNER // N_DEV, D_MODEL)
    (dzx, d_cw, d_cb, d_dtb, d_alog, d_dsk), (l_w1_1, l_w2_1, l_wout) = _ssd_bwd(
        zx, conv_w, conv_b, dtb, alog, dsk, states, dy_ssd, dzx, carried=scatter_of([d_w1_1, d_w2_1, d_wout]))
    d_w_in = _zx_to_w_in(_matmul_tn(hn1, dzx, "ssm_dw_in"))
    most = 5 * D_MODEL // 8
    (dh2, dh2b, d_gm1), (l_w_in_a,) = _in_proj_bwd(dzx, w_zx, h2, gm1, dh3, carried=scatter_of([d_w_in], (0, most)))
    d_w2_0, (l_w_in_b,) = _matmul_tn(u0, dh2b, "mlp0_dw2", square_a=True,
                                     carried=scatter_of([d_w_in], (most, D_MODEL - most)))
    d_w2_0 = d_w2_0.reshape(N_DEV, fb, D_MODEL)
    (dh1, _, da0, d_gl0), (l_w2_0,) = _mlp_bwd(dh2, dh2b, h1, gl0, u0, w1g0, w2g0, "mlp0_bwd",
                                           carried=scatter_of([d_w2_0]))
    d_w1_0 = _matmul_tn(hm0, da0, "mlp0_dw1", col_blocked=True, scale=2.0)
    (dx, d_pool, d_pb, d_ps, d_gm0), (l_w1_0,) = _pool_bwd(x2, dh1, gm0, w_pool, pool_b, pool_scale,
                                                          carried=scatter_of([d_w1_0]))

    d_conv_w = _ungroup_conv_cols(d_cw).reshape(4, N_DEV, CONV_SHARD).transpose(1, 0, 2)
    d_conv_b = _ungroup_conv_cols(d_cb).reshape(N_DEV, 1, CONV_SHARD)
    d_gain = jnp.pad(d_ng.reshape(N_DEV, 1, D_INNER // N_DEV), ((0, 0), (0, 0), (0, CONV_SHARD - D_INNER // N_DEV)))
    d_small = jnp.concatenate([d_conv_w, d_conv_b, d_gain,
                               jnp.zeros((N_DEV, SMALL_ROWS - 6, CONV_SHARD), F32)], axis=1)
    l_pool, l_small = _run_exchange(scatter_of([bf(d_pool), d_small]), "reduce_scatter_tail")

    heads = jnp.concatenate([_heads_of(a) for a in (d_dtb, d_alog, d_dsk)] + [loss_row[:, 0:1]], axis=1)
    sp = jnp.concatenate([d_gm0, d_gm1, d_gl0, d_gl1, d_pb, d_ps, d_gfin,
                          jnp.pad(heads, ((0, 0), (0, D_MODEL - 3 * N_HEADS - 1)))], axis=0)
    sg = _all_reduce_small(sp)

    g_norm_mix = sg[0:2]
    g_norm_mlp = sg[2:4]
    g_pool_b, g_pool_scale = sg[4:5], sg[5:6]
    g_final = sg[6]
    g_dtb, g_alog, g_dsk = sg[7:8, 0:32], sg[7:8, 32:64], sg[7:8, 64:96]

    def rep_pack(nm, nl, pb, ps, fg, db, al, dk):
        hd = jnp.pad(jnp.concatenate([db, al, dk], axis=1), ((0, 0), (0, D_MODEL - 3 * N_HEADS)))
        return jnp.concatenate([nm, nl, pb, ps, fg.reshape(1, D_MODEL), hd], axis=0)

    rep = [rep_pack(*t) for t in (
        (norm_mix_g, norm_mlp_g, pool_b, pool_scale, final_g, ssm_dt_bias, ssm_a_log, ssm_d),
        (g_norm_mix, g_norm_mlp, g_pool_b, g_pool_scale, g_final, g_dtb, g_alog, g_dsk),
        (m_norm_mix_g, m_norm_mlp_g, m_pool_b, m_pool_scale, m_final_g, m_ssm_dt_bias, m_ssm_a_log, m_ssm_d),
        (v_norm_mix_g, v_norm_mlp_g, v_pool_b, v_pool_scale, v_final_g, v_ssm_dt_bias, v_ssm_a_log, v_ssm_d))]
    rep_out = _adamw(*rep, "adamw_replicated")

    def rep_unpack(a):
        return (a[0:2], a[2:4], a[4:5], a[5:6], a[6], a[7:8, 0:32], a[7:8, 32:64], a[7:8, 64:96])

    sm_out = _adamw_reduced(_small_shard(ssm_conv_w, ssm_conv_b, ssm_norm_g), l_small,
                            _small_shard(m_ssm_conv_w, m_ssm_conv_b, m_ssm_norm_g),
                            _small_shard(v_ssm_conv_w, v_ssm_conv_b, v_ssm_norm_g), "adamw_small_shards")

    big = {
        "pool_w": _update(pool_w, l_pool, m_pool_w, v_pool_w, "adamw_pool_w"),
        "ssm_w_in": tuple(o.reshape(ssm_w_in.shape) for o in _adamw_reduced_parts(
            ssm_w_in[0], (l_w_in_a, l_w_in_b), m_ssm_w_in[0], v_ssm_w_in[0], "adamw_w_in")),
        "ssm_w_out": _update(ssm_w_out, l_wout, m_ssm_w_out, v_ssm_w_out, "adamw_w_out"),
        "mlp_w1": _adamw_reduced_layers(mlp_w1, (l_w1_0, l_w1_1), m_mlp_w1, v_mlp_w1, "adamw_w1"),
        "mlp_w2": _adamw_reduced_layers(mlp_w2, (l_w2_0, l_w2_1), m_mlp_w2, v_mlp_w2, "adamw_w2"),
    }
    rep_all = (rep[1],) + tuple(rep_out)

    def ordered(kind):
        nm, nl, pb, ps, fg, db, al, dk = rep_unpack(rep_all[kind])
        cw, cb, ng = _small_unshard(sm_out[kind])
        return [nm, nl, big["pool_w"][kind], pb, ps, big["ssm_w_in"][kind], cw, cb, db, al, dk, ng,
                big["ssm_w_out"][kind], big["mlp_w1"][kind], big["mlp_w2"][kind], fg]

    loss = sg[7, 3 * N_HEADS]
    return (loss, dx[None], *ordered(0), *ordered(1), *ordered(2), *ordered(3))
```

```python
import functools

import jax
import jax.numpy as jnp
from jax import lax
from jax.experimental import pallas as pl
from jax.experimental.pallas import tpu as pltpu

F32 = jnp.float32
BF16 = jnp.bfloat16
MESH = pl.DeviceIdType.MESH

D_MODEL = 1024
RMS_EPS = 1e-5
POOL_WINDOWS = (2, 4, 8, 16)
POOL_GROUP = 256
POOL_HALO = 16
POOL_SHARD = POOL_GROUP // 8
D_INNER = 2048
HEAD_DIM = 64
N_HEADS = 32
N_GROUPS = 4
HEADS_PER_GROUP = 8
D_STATE = 128
CHUNK = 128
CONV_DIM = 3072
IN_PROJ_DIM = 5152
D_FF = 4096
N_DEV = 8
GROUP_X = HEADS_PER_GROUP * HEAD_DIM
GROUP_CONV = GROUP_X + 2 * D_STATE
GROUP_COLS = GROUP_CONV + 128
Z_OFF = N_GROUPS * GROUP_COLS
ZX_COLS = Z_OFF + D_INNER
COL_BLK = 512
PROJ_BLK = ZX_COLS // 4
W_IN_SHARD = IN_PROJ_DIM // N_DEV

ADAM_LR = 0.001
ADAM_B1 = 0.9
ADAM_B2 = 0.999
ADAM_EPS = 1e-08
ADAM_WD = 0.01
ADAM_STEP = 10

VMEM_LIMIT_V7X = 56 * 1024 * 1024
MID_STEP_PERCENT = 85
TN_TOKENS = 512
TN_ACC_BYTES = 16 * 1024 * 1024
MATMUL_TOKENS = 1024

CONV_SHARD = CONV_DIM // N_DEV
SMALL_ROWS = 8

_NN = (((1,), (0,)), ((), ()))
_NT = (((1,), (1,)), ((), ()))
_TN = (((0,), (0,)), ((), ()))


def _cp(sem):
    return pltpu.CompilerParams(dimension_semantics=sem, vmem_limit_bytes=VMEM_LIMIT_V7X)


_ANY = pl.BlockSpec(memory_space=pl.ANY)


def _place():
    return lax.axis_index("x"), lax.axis_index("y"), lax.axis_index("c")


class _Carried:
    def __init__(self, ins, outs, sems, start, finish, mid=None, mid_percent=None):
        self.ins, self.outs, self.sems = list(ins), list(outs), list(sems)
        self.start, self.mid, self.finish, self.mid_percent = start, mid, finish, mid_percent


def _pcall(body, *, name, grid, in_specs, out_specs, out_shape, sem, args, scratch_shapes=(), carried=None,
           aliases=None):
    in_specs, out_specs, out_shape, scratch = list(in_specs), list(out_specs), list(out_shape), list(scratch_shapes)
    common = dict(name=name, grid=grid, input_output_aliases=aliases or {}, compiler_params=_cp(sem))
    if carried is None:
        res = pl.pallas_call(body, in_specs=in_specs, out_specs=out_specs, out_shape=out_shape,
                             scratch_shapes=scratch, **common)(*args)
        return list(res), []
    n_in, n_out, n_scr = len(in_specs), len(out_specs), len(scratch)
    ci, co = len(carried.ins), len(carried.outs)

    def wrapped(*refs):
        ins, cins = refs[:n_in], refs[n_in:n_in + ci]
        p = n_in + ci
        outs, couts = refs[p:p + n_out], refs[p + n_out:p + n_out + co]
        p += n_out + co
        scr, csems = refs[p:p + n_scr], refs[p + n_scr:]
        ids = [pl.program_id(a) for a in range(len(grid))]
        first = functools.reduce(jnp.logical_and, [i == 0 for i in ids])
        last = functools.reduce(jnp.logical_and, [i == g - 1 for i, g in zip(ids, grid)])

        @pl.when(first)
        def _():
            carried.start(cins, couts, csems)

        if carried.mid is not None:
            step, steps = 0, 1
            for i, g in zip(ids, grid):
                step, steps = step * g + i, steps * g

            @pl.when(step == min(steps - 1, (steps * carried.mid_percent) // 100))
            def _():
                carried.mid(cins, couts, csems)

        body(*ins, *outs, *scr)

        @pl.when(last)
        def _():
            carried.finish(cins, couts, csems)

    res = pl.pallas_call(wrapped, in_specs=in_specs + [_ANY] * ci, out_specs=out_specs + [_ANY] * co,
                         out_shape=out_shape + carried.outs, scratch_shapes=scratch + carried.sems,
                         **common)(*args, *carried.ins)
    return list(res[:n_out]), list(res[n_out:])


def _peers(x, y, c):
    out = []
    for rel in range(1, N_DEV):
        dx, dy, dc = (rel >> 2) & 1, (rel >> 1) & 1, rel & 1
        out.append((x + dx - 2 * x * dx, y + dy - 2 * y * dy, c + dc - 2 * c * dc))
    return out


def _direct_exchange(srcs, layout, out_shapes, scatter, src_rows=None):
    n = len(srcs)

    def copies(ins, outs, sems):
        send, recv, loc = sems
        x, y, c = _place()
        me = 4 * x + 2 * y + c
        out, arrive, local = [], [], []
        for i in range(n):
            j, off = layout[i]
            first, rows = (0, srcs[i].shape[-2]) if src_rows is None else src_rows[i]

            def piece(k):
                return ins[i].at[k, pl.ds(first, rows)] if scatter else ins[i]

            for r, peer in enumerate(_peers(x, y, c)):
                pidx = 4 * peer[0] + 2 * peer[1] + peer[2]
                kw = dict(send_sem=send.at[7 * i + r], recv_sem=recv.at[7 * i + r], device_id=peer, device_id_type=MESH)
                out.append(pltpu.make_async_remote_copy(
                    src_ref=piece(pidx), dst_ref=outs[j].at[me, pl.ds(off, rows)], **kw))
                arrive.append(pltpu.make_async_remote_copy(
                    src_ref=piece(pidx), dst_ref=outs[j].at[pidx, pl.ds(off, rows)], **kw))
            local.append(pltpu.make_async_copy(piece(me), outs[j].at[me, pl.ds(off, rows)], loc.at[i]))
        return out, arrive, local

    def start(ins, outs, sems):
        out, _, local = copies(ins, outs, sems)
        for cp in local + out:
            cp.start()

    def finish(ins, outs, sems):
        out, arrive, local = copies(ins, outs, sems)
        for cp in arrive:
            cp.wait_recv()
        for cp in out:
            cp.wait_send()
        for cp in local:
            cp.wait()

    return _Carried(srcs, [jax.ShapeDtypeStruct((N_DEV,) + tuple(s), d) for s, d in out_shapes],
                    [pltpu.SemaphoreType.DMA((7 * n,)), pltpu.SemaphoreType.DMA((7 * n,)),
                     pltpu.SemaphoreType.DMA((n,))], start, finish)


def _two_level_gather(shards, mid_percent=MID_STEP_PERCENT):
    n = len(shards)

    def copies(ins, outs, sems):
        send, recv, loc = sems
        x, y, c = _place()
        me, sibling = (x, y, c), (x, y, 1 - c)
        chips = [(1 - x, y), (x, 1 - y), (1 - x, 1 - y)]

        def win(i, place):
            return outs[i].at[4 * place[0] + 2 * place[1] + place[2]]

        def copy(i, k, block, to, src=None):
            return pltpu.make_async_remote_copy(
                src_ref=win(i, block) if src is None else src, dst_ref=win(i, block),
                send_sem=send.at[7 * i + k], recv_sem=recv.at[7 * i + k], device_id=to, device_id_type=MESH)

        own, passed, ici_in, d2d_in, local = [], [], [], [], []
        for i in range(n):
            own += [copy(i, 0, me, sibling, src=ins[i])]
            own += [copy(i, 1 + j, me, (*chip, c), src=ins[i]) for j, chip in enumerate(chips)]
            passed += [copy(i, 4 + j, (*chip, c), sibling) for j, chip in enumerate(chips)]
            ici_in += [copy(i, 1 + j, (*chip, c), me) for j, chip in enumerate(chips)]
            d2d_in += [copy(i, 0, sibling, me)] + [copy(i, 4 + j, (*chip, 1 - c), me) for j, chip in enumerate(chips)]
            local.append(pltpu.make_async_copy(ins[i], win(i, me), loc.at[i]))
        return own, passed, ici_in, d2d_in, local

    def start(ins, outs, sems):
        own, _, _, _, local = copies(ins, outs, sems)
        for cp in local + own:
            cp.start()

    def mid(ins, outs, sems):
        _, passed, ici_in, _, _ = copies(ins, outs, sems)
        for arrived, onward in zip(ici_in, passed):
            arrived.wait_recv()
            onward.start()

    def finish(ins, outs, sems):
        own, passed, _, d2d_in, local = copies(ins, outs, sems)
        for cp in d2d_in:
            cp.wait_recv()
        for cp in own + passed:
            cp.wait_send()
        for cp in local:
            cp.wait()

    return _Carried(shards, [jax.ShapeDtypeStruct((N_DEV,) + tuple(s.shape), s.dtype) for s in shards],
                    [pltpu.SemaphoreType.DMA((7 * n,)), pltpu.SemaphoreType.DMA((7 * n,)),
                     pltpu.SemaphoreType.DMA((n,))], start, finish, mid, mid_percent)


def _run_exchange(carried, name):
    ci = len(carried.ins)

    def body(*refs):
        ins, outs, sems = refs[:ci], refs[ci:ci + len(carried.outs)], refs[ci + len(carried.outs):]
        carried.start(ins, outs, sems)
        if carried.mid is not None:
            carried.mid(ins, outs, sems)
        carried.finish(ins, outs, sems)

    return list(pl.pallas_call(body, name=name, in_specs=[_ANY] * ci, out_specs=[_ANY] * len(carried.outs),
                               out_shape=carried.outs, scratch_shapes=carried.sems)(*carried.ins))


def _dg(a, b, dn):
    return lax.dot_general(a.astype(BF16), b.astype(BF16), dn, preferred_element_type=F32)


@jax.custom_vjp
def mm_nn(a, b):
    return _dg(a, b, _NN)


@jax.custom_vjp
def mm_nt(a, b):
    return _dg(a, b, _NT)


@jax.custom_vjp
def mm_tn(a, b):
    return _dg(a, b, _TN)


mm_nn.defvjp(lambda a, b: (_dg(a, b, _NN), (a, b)), lambda r, ct: (mm_nt(ct, r[1]), mm_tn(r[0], ct)))
mm_nt.defvjp(lambda a, b: (_dg(a, b, _NT), (a, b)), lambda r, ct: (mm_nn(ct, r[1]), mm_tn(ct, r[0])))
mm_tn.defvjp(lambda a, b: (_dg(a, b, _TN), (a, b)), lambda r, ct: (mm_nt(r[1], ct), mm_nn(r[0], ct)))


def _split3(x):
    p1 = x.astype(BF16)
    r1 = x - p1.astype(F32)
    p2 = r1.astype(BF16)
    r2 = r1 - p2.astype(F32)
    return p1, p2, r2.astype(BF16)


def _exact01(x, c, dn, const_left):
    acc = None
    for p in reversed(_split3(x)):
        t = (lax.dot_general(c, p, dn, preferred_element_type=F32) if const_left
             else lax.dot_general(p, c, dn, preferred_element_type=F32))
        acc = t if acc is None else acc + t
    return acc


def _make_cmm(dn, const_left, bwd_name):
    @jax.custom_vjp
    def f(x, c):
        return _exact01(x, c, dn, const_left)

    def fwd(x, c):
        return _exact01(x, c, dn, const_left), c

    def bwd(c, ct):
        return _CMM[bwd_name](ct, c), jnp.zeros_like(c)

    f.defvjp(fwd, bwd)
    return f


_CMM = {}
_CMM["xc"] = _make_cmm(_NN, False, "xct")
_CMM["xct"] = _make_cmm(_NT, False, "xc")
_CMM["cx"] = _make_cmm(_NN, True, "ctx")
_CMM["ctx"] = _make_cmm(_TN, True, "cx")


def _sigmoid(x):
    return 0.5 * jnp.tanh(0.5 * x) + 0.5


@jax.custom_vjp
def _silu(x):
    return x * _sigmoid(x)


def _silu_fwd(x):
    return _silu(x), x


def _silu_bwd(x, ct):
    s = _sigmoid(x)
    return (ct * (s * (1.0 + x * (1.0 - s))),)


_silu.defvjp(_silu_fwd, _silu_bwd)


def _log1p_pos(e):
    u = 1.0 + e
    d = u - 1.0
    return jnp.where(d == 0.0, e, jnp.log(u) * (e / jnp.where(d == 0.0, 1.0, d)))


@jax.custom_vjp
def _softplus(x):
    return jnp.maximum(x, 0.0) + _log1p_pos(jnp.exp(-jnp.abs(x)))


def _softplus_fwd(x):
    return _softplus(x), x


def _softplus_bwd(x, ct):
    return (ct * _sigmoid(x),)


_softplus.defvjp(_softplus_fwd, _softplus_bwd)


CONV_HALO = 8


def _make_shift(j):
    @jax.custom_vjp
    def f(ext):
        return pltpu.roll(ext, j, 0)[CONV_HALO:, :]

    def fwd(ext):
        return f(ext), None

    def bwd(_, ct):
        pad = jnp.concatenate([jnp.zeros((CONV_HALO, ct.shape[1]), ct.dtype), ct], axis=0)
        return (pltpu.roll(pad, CONV_HALO + CHUNK - j, 0),)

    f.defvjp(fwd, bwd)
    return f


_SHIFT = {j: _make_shift(j) for j in (1, 2, 3)}


@jax.custom_vjp
def _swap_halves(x):
    return pltpu.roll(x, HEAD_DIM, 1)


_swap_halves.defvjp(lambda x: (_swap_halves(x), None), lambda _, ct: (pltpu.roll(ct, HEAD_DIM, 1),))


def _rms_fwd(x, g):
    r = lax.rsqrt(jnp.mean(x * x, axis=-1, keepdims=True) + RMS_EPS)
    n = x * r
    return n * g, n, r


def _rms_bwd(dy, n, r, g):
    dn = dy * g
    dx = r * (dn - n * jnp.mean(dn * n, axis=-1, keepdims=True))
    dg = jnp.sum(dy * n, axis=0, keepdims=True)
    return dx, dg


def _one(cond):
    return jnp.where(cond, 1.0, 0.0)


def _pool_tile(xe, g, ws, b, scale, tile, tt):
    r = lax.rsqrt(jnp.mean(xe * xe, axis=-1, keepdims=True) + RMS_EPS)
    hn = xe * r * g
    row_e = lax.broadcasted_iota(jnp.int32, (tt + POOL_HALO, POOL_GROUP), 0)
    keep = _one(jnp.logical_or(row_e >= POOL_HALO, tile > 0))
    rr = lax.broadcasted_iota(jnp.int32, (tt, tt + POOL_HALO), 0)
    qq = lax.broadcasted_iota(jnp.int32, (tt, tt + POOL_HALO), 1)
    dd = qq - rr
    tpos = tile * tt + lax.broadcasted_iota(jnp.int32, (tt, POOL_GROUP), 0)
    outs = []
    for gi, w in enumerate(POOL_WINDOWS):
        hg = hn[:, gi * POOL_GROUP:(gi + 1) * POOL_GROUP] * keep
        band = _one(jnp.logical_and(dd >= POOL_HALO - w + 1, dd <= POOL_HALO)).astype(BF16)
        cnt = jnp.minimum(tpos + 1, w).astype(F32)
        pooled = _CMM["cx"](hg, band) / cnt
        mixed = pooled - hg[POOL_HALO:, :]
        outs.append(mm_nn(mixed, ws[gi]))
    out = (jnp.concatenate(outs, axis=1) + b) * scale
    return xe[POOL_HALO:, :] + out


def _pool_specs(tt, nt, rev):
    per = tt // POOL_HALO
    t_of = (lambda i: nt - 1 - i) if rev else (lambda i: i)
    main = pl.BlockSpec((tt, D_MODEL), lambda i: (t_of(i), 0))
    halo = pl.BlockSpec((POOL_HALO, D_MODEL), lambda i: (jnp.maximum(t_of(i) * per - 1, 0), 0))
    vec = pl.BlockSpec((1, D_MODEL), lambda i: (0, 0))
    wsp = pl.BlockSpec((N_DEV, 4 * POOL_SHARD, POOL_GROUP), lambda i: (0, 0, 0))
    return main, halo, vec, wsp


def _pool_weights(w_ref):
    return tuple(
        jnp.concatenate([w_ref[k, gi * POOL_SHARD:(gi + 1) * POOL_SHARD, :] for k in range(N_DEV)], axis=0).astype(F32)
        for gi in range(4))


def _pool_fwd(x, g, w, b, scale, carried=None):
    t = x.shape[0]
    tt = min(t, 256)
    nt = t // tt
    main, halo, vec, wsp = _pool_specs(tt, nt, False)

    def body(xm_ref, xh_ref, g_ref, w_ref, b_ref, s_ref, o_ref):
        i = pl.program_id(0)
        xe = jnp.concatenate([xh_ref[...], xm_ref[...]], axis=0)
        o_ref[...] = _pool_tile(xe, g_ref[...], _pool_weights(w_ref), b_ref[...], s_ref[...], i, tt)

    return _pcall(
        body, name="pool_fwd", grid=(nt,),
        in_specs=[main, halo, vec, wsp, vec, vec], out_specs=[main],
        out_shape=[jax.ShapeDtypeStruct((t, D_MODEL), F32)],
        sem=("arbitrary",), args=(x, x, g, w, b, scale), carried=carried)


def _pool_bwd(x, dh, g, w, b, scale, carried=None):
    t = x.shape[0]
    tt = min(t, 256)
    nt = t // tt
    main, halo, vec, wsp = _pool_specs(tt, nt, True)

    def body(xm_ref, xh_ref, dh_ref, g_ref, w_ref, b_ref, s_ref,
             dx_ref, dw_ref, db_ref, ds_ref, dg_ref, carry, dw_acc):
        i = pl.program_id(0)
        tile = nt - 1 - i

        @pl.when(i == 0)
        def _():
            carry[...] = jnp.zeros_like(carry)
            dw_acc[...] = jnp.zeros_like(dw_acc)
            db_ref[...] = jnp.zeros_like(db_ref)
            ds_ref[...] = jnp.zeros_like(ds_ref)
            dg_ref[...] = jnp.zeros_like(dg_ref)

        xe = jnp.concatenate([xh_ref[...], xm_ref[...]], axis=0)
        _, vjp = jax.vjp(lambda a, gg, ww, bb, ss: _pool_tile(a, gg, ww, bb, ss, tile, tt),
                         xe, g_ref[...], _pool_weights(w_ref), b_ref[...], s_ref[...])
        dxe, dgv, dws, dbv, dsv = vjp(dh_ref[...])
        dx_ref[...] = dxe[POOL_HALO:, :]
        dx_ref[tt - POOL_HALO:tt, :] += carry[...]
        carry[...] = dxe[:POOL_HALO, :]
        for gi in range(4):
            dw_acc[gi] += dws[gi]
        db_ref[...] += dbv
        ds_ref[...] += dsv
        dg_ref[...] += dgv

        @pl.when(i == nt - 1)
        def _():
            for k in range(N_DEV):
                for gi in range(4):
                    dw_ref[k, gi * POOL_SHARD:(gi + 1) * POOL_SHARD, :] = dw_acc[gi, k * POOL_SHARD:(k + 1) * POOL_SHARD, :]

    return _pcall(
        body, name="pool_bwd", grid=(nt,),
        in_specs=[main, halo, main, vec, wsp, vec, vec],
        out_specs=[main, wsp, vec, vec, vec],
        out_shape=[jax.ShapeDtypeStruct((t, D_MODEL), F32),
                   jax.ShapeDtypeStruct((N_DEV, 4 * POOL_SHARD, POOL_GROUP), F32),
                   jax.ShapeDtypeStruct((1, D_MODEL), F32),
                   jax.ShapeDtypeStruct((1, D_MODEL), F32),
                   jax.ShapeDtypeStruct((1, D_MODEL), F32)],
        scratch_shapes=[pltpu.VMEM((POOL_HALO, D_MODEL), F32), pltpu.VMEM((4, POOL_GROUP, POOL_GROUP), F32)],
        sem=("arbitrary",), args=(x, x, dh, g, w, b, scale), carried=carried)


def _mlp_weight_specs():
    fb = D_FF // N_DEV
    return (pl.BlockSpec((None, D_MODEL, fb), lambda i, k: (k, 0, 0)),
            pl.BlockSpec((None, fb, D_MODEL), lambda i, k: (k, 0, 0)))


def _mlp_fwd(h, g, w1g, w2g, name, carried=None):
    t = h.shape[0]
    tt = min(t, MATMUL_TOKENS)
    nk, fb = N_DEV, D_FF // N_DEV
    w1_spec, w2_spec = _mlp_weight_specs()

    def body(h_ref, g_ref, w1_ref, w2_ref, o_ref, u_ref, hm_ref, hm_s, acc_s):
        k = pl.program_id(1)

        @pl.when(k == 0)
        def _():
            xv = h_ref[...]
            y, _, _ = _rms_fwd(xv, g_ref[...])
            hb = y.astype(BF16)
            hm_s[...] = hb
            hm_ref[...] = hb
            acc_s[...] = xv

        a = jnp.dot(hm_s[...], w1_ref[...], preferred_element_type=F32)
        u = jnp.maximum(a, 0.0)
        u_ref[...] = u.astype(BF16)
        acc_s[...] += jnp.dot((u * u).astype(BF16), w2_ref[...], preferred_element_type=F32)

        @pl.when(k == nk - 1)
        def _():
            o_ref[...] = acc_s[...]

    return _pcall(
        body, name=name, grid=(t // tt, nk),
        in_specs=[pl.BlockSpec((tt, D_MODEL), lambda i, k: (i, 0)),
                  pl.BlockSpec((1, D_MODEL), lambda i, k: (0, 0)),
                  w1_spec, w2_spec],
        out_specs=[pl.BlockSpec((tt, D_MODEL), lambda i, k: (i, 0)),
                   pl.BlockSpec((tt, fb), lambda i, k: (i, k)),
                   pl.BlockSpec((tt, D_MODEL), lambda i, k: (i, 0))],
        out_shape=[jax.ShapeDtypeStruct((t, D_MODEL), F32),
                   jax.ShapeDtypeStruct((t, nk * fb), BF16),
                   jax.ShapeDtypeStruct((t, D_MODEL), BF16)],
        scratch_shapes=[pltpu.VMEM((tt, D_MODEL), BF16), pltpu.VMEM((tt, D_MODEL), F32)],
        sem=("arbitrary", "arbitrary"), args=(h, g, w1g, w2g), carried=carried)


def _mlp_bwd(dh, dhb, h, g, u, w1g, w2g, name, carried=None):
    t = h.shape[0]
    tt = min(t, MATMUL_TOKENS)
    nk, fb = N_DEV, D_FF // N_DEV
    w1_spec, w2_spec = _mlp_weight_specs()

    def body(dh_ref, dhb_ref, h_ref, g_ref, u_ref, w1_ref, w2_ref,
             dhin_ref, dhinb_ref, da_ref, dg_ref, acc_s):
        i = pl.program_id(0)
        k = pl.program_id(1)

        @pl.when(jnp.logical_and(i == 0, k == 0))
        def _():
            dg_ref[...] = jnp.zeros_like(dg_ref)

        @pl.when(k == 0)
        def _():
            acc_s[...] = jnp.zeros_like(acc_s)

        dv = lax.dot_general(dhb_ref[...], w2_ref[...], _NT, preferred_element_type=F32)
        dab = (dv * u_ref[...].astype(F32)).astype(BF16)
        da_ref[...] = dab
        acc_s[...] += lax.dot_general(dab, w1_ref[...], _NT, preferred_element_type=F32)

        @pl.when(k == nk - 1)
        def _():
            gv = g_ref[...]
            _, n, r = _rms_fwd(h_ref[...], gv)
            dx, dg = _rms_bwd(2.0 * acc_s[...], n, r, gv)
            dhin = dh_ref[...] + dx
            dhin_ref[...] = dhin
            dhinb_ref[...] = dhin.astype(BF16)
            dg_ref[...] += dg

    tile = pl.BlockSpec((tt, D_MODEL), lambda i, k: (i, 0))
    return _pcall(
        body, name=name, grid=(t // tt, nk),
        in_specs=[tile, tile, tile, pl.BlockSpec((1, D_MODEL), lambda i, k: (0, 0)),
                  pl.BlockSpec((tt, fb), lambda i, k: (i, k)), w1_spec, w2_spec],
        out_specs=[tile, tile, pl.BlockSpec((tt, fb), lambda i, k: (i, k)),
                   pl.BlockSpec((1, D_MODEL), lambda i, k: (0, 0))],
        out_shape=[jax.ShapeDtypeStruct((t, D_MODEL), F32),
                   jax.ShapeDtypeStruct((t, D_MODEL), BF16),
                   jax.ShapeDtypeStruct((t, nk * fb), BF16),
                   jax.ShapeDtypeStruct((1, D_MODEL), F32)],
        scratch_shapes=[pltpu.VMEM((tt, D_MODEL), F32)],
        sem=("arbitrary", "arbitrary"), args=(dh, dhb, h, g, u, w1g, w2g), carried=carried)


def _matmul_tn(a, b, name, square_a=False, col_blocked=False, carried=None, scale=None):
    t, k1 = a.shape
    k2 = b.shape[1]
    tt = min(t, TN_TOKENS)
    nt = t // tt
    wc = k2 if k1 * k2 * 4 <= TN_ACC_BYTES else k2 // 2
    nb = wc // COL_BLK

    def body(a_ref, b_ref, o_ref, acc):
        s = pl.program_id(1)

        @pl.when(s == 0)
        def _():
            acc[...] = jnp.zeros_like(acc)

        av = a_ref[...]
        if square_a:
            af = av.astype(F32)
            av = (af * af).astype(BF16)
        acc[...] += lax.dot_general(av, b_ref[...], _TN, preferred_element_type=F32)

        @pl.when(s == nt - 1)
        def _():
            def done(v):
                return (v if scale is None else scale * v).astype(o_ref.dtype)

            if col_blocked:
                for k in range(nb):
                    o_ref[k] = done(acc[:, k * COL_BLK:(k + 1) * COL_BLK])
            else:
                o_ref[...] = done(acc[...])

    if col_blocked:
        out_shape = jax.ShapeDtypeStruct((k2 // COL_BLK, k1, COL_BLK), BF16)
        out_spec = pl.BlockSpec((nb, k1, COL_BLK), lambda j, s: (j, 0, 0))
    else:
        out_shape = jax.ShapeDtypeStruct((k1, k2), BF16)
        out_spec = pl.BlockSpec((k1, wc), lambda j, s: (0, j))
    outs, landed = _pcall(
        body, name=name, grid=(k2 // wc, nt),
        in_specs=[pl.BlockSpec((tt, k1), lambda j, s: (s, 0)),
                  pl.BlockSpec((tt, wc), lambda j, s: (s, j))],
        out_specs=[out_spec], out_shape=[out_shape],
        scratch_shapes=[pltpu.VMEM((k1, wc), F32)],
        sem=("arbitrary", "arbitrary"), args=(a, b), carried=carried)
    return (outs[0], landed) if carried is not None else outs[0]


def _norm_matmul(h, g, w, carried=None):
    t = h.shape[0]
    tt = min(t, MATMUL_TOKENS)
    n = w.shape[1]

    def body(h_ref, g_ref, w_ref, o_ref, hn_ref, hn_s):
        @pl.when(pl.program_id(1) == 0)
        def _():
            y, _, _ = _rms_fwd(h_ref[...], g_ref[...])
            hb = y.astype(BF16)
            hn_s[...] = hb
            hn_ref[...] = hb

        o_ref[...] = jnp.dot(hn_s[...], w_ref[...], preferred_element_type=F32)

    return _pcall(
        body, name="ssm_in_proj", grid=(t // tt, n // PROJ_BLK),
        in_specs=[pl.BlockSpec((tt, D_MODEL), lambda i, j: (i, 0)),
                  pl.BlockSpec((1, D_MODEL), lambda i, j: (0, 0)),
                  pl.BlockSpec((D_MODEL, PROJ_BLK), lambda i, j: (0, j))],
        out_specs=[pl.BlockSpec((tt, PROJ_BLK), lambda i, j: (i, j)),
                   pl.BlockSpec((tt, D_MODEL), lambda i, j: (i, 0))],
        out_shape=[jax.ShapeDtypeStruct((t, n), F32), jax.ShapeDtypeStruct((t, D_MODEL), BF16)],
        scratch_shapes=[pltpu.VMEM((tt, D_MODEL), BF16)],
        sem=("arbitrary", "arbitrary"), args=(h, g, w), carried=carried)


def _in_proj_bwd(dzx, w, h, g, dh_next, carried=None):
    t = h.shape[0]
    tt = min(t, MATMUL_TOKENS)
    n = w.shape[1]
    nj = n // PROJ_BLK

    def body(dz_ref, w_ref, h_ref, g_ref, dn_ref, dh_ref, dhb_ref, dg_ref, acc):
        i = pl.program_id(0)
        j = pl.program_id(1)

        @pl.when(jnp.logical_and(i == 0, j == 0))
        def _():
            dg_ref[...] = jnp.zeros_like(dg_ref)

        @pl.when(j == 0)
        def _():
            acc[...] = jnp.zeros_like(acc)

        acc[...] += lax.dot_general(dz_ref[...], w_ref[...], _NT, preferred_element_type=F32)

        @pl.when(j == nj - 1)
        def _():
            gv = g_ref[...]
            _, nn, r = _rms_fwd(h_ref[...], gv)
            dx, dg = _rms_bwd(acc[...], nn, r, gv)
            dh = dn_ref[...] + dx
            dh_ref[...] = dh
            dhb_ref[...] = dh.astype(BF16)
            dg_ref[...] += dg

    tile = pl.BlockSpec((tt, D_MODEL), lambda i, j: (i, 0))
    return _pcall(
        body, name="ssm_in_proj_bwd", grid=(t // tt, nj),
        in_specs=[pl.BlockSpec((tt, PROJ_BLK), lambda i, j: (i, j)),
                  pl.BlockSpec((D_MODEL, PROJ_BLK), lambda i, j: (0, j)),
                  tile, pl.BlockSpec((1, D_MODEL), lambda i, j: (0, 0)), tile],
        out_specs=[tile, tile, pl.BlockSpec((1, D_MODEL), lambda i, j: (0, 0))],
        out_shape=[jax.ShapeDtypeStruct((t, D_MODEL), F32), jax.ShapeDtypeStruct((t, D_MODEL), BF16),
                   jax.ShapeDtypeStruct((1, D_MODEL), F32)],
        scratch_shapes=[pltpu.VMEM((tt, D_MODEL), F32)],
        sem=("arbitrary", "arbitrary"), args=(dzx, w, h, g, dh_next), carried=carried)


def _ssd_consts():
    lane = lax.broadcasted_iota(jnp.int32, (CHUNK, CHUNK), 1)
    row = lax.broadcasted_iota(jnp.int32, (CHUNK, CHUNK), 0)
    causal = lane <= row
    tri = _one(causal).astype(BF16)
    er = lax.broadcasted_iota(jnp.int32, (CHUNK, GROUP_X), 0)
    ec = lax.broadcasted_iota(jnp.int32, (CHUNK, GROUP_X), 1)
    expand = _one(jnp.right_shift(ec, 6) == er).astype(BF16)
    return dict(causal=causal, tri=tri, expand=expand, lo=lane < HEAD_DIM)


def _conv_silu(cur, prev, w, b):
    ext = jnp.concatenate([prev, cur], axis=0)
    acc = cur * w[3] + b
    for j in (1, 2, 3):
        acc = acc + _SHIFT[j](ext) * w[3 - j]
    return _silu(acc)


def _ssd_chunk(raw, rawp, ht, cw, cb_, dtb, alog, dsk, k):
    act = _conv_silu(raw[:, :GROUP_CONV], rawp[:, :GROUP_CONV], cw, cb_)
    xs = act[:, :GROUP_X]
    bm = act[:, GROUP_X:GROUP_X + D_STATE]
    cm = act[:, GROUP_X + D_STATE:]
    dt = _softplus(raw[:, GROUP_CONV:] + dtb)
    a = -jnp.exp(alog)
    xc = _CMM["xc"]

    def lanes(rowv):
        return jnp.sum(xc(jnp.broadcast_to(rowv, (16, CHUNK)), k["expand"]), axis=0, keepdims=True) * (1.0 / 16.0)

    dt_e = xc(dt, k["expand"])
    adt_e = dt_e * lanes(a)
    acs_e = _CMM["cx"](adt_e, k["tri"])
    tot_e = jnp.sum(adt_e, axis=0, keepdims=True)
    gmat = mm_nt(cm, bm)
    xdt = xs * dt_e
    ys = []
    for j in range(HEADS_PER_GROUP // 2):
        pair = acs_e[:, j * CHUNK:(j + 1) * CHUNK]
        swapped = _swap_halves(pair)
        ms = []
        for cb in (jnp.where(k["lo"], pair, swapped), jnp.where(k["lo"], swapped, pair)):
            seg = cb - cb.T
            ms.append(gmat * jnp.exp(jnp.where(k["causal"], seg, -jnp.inf)))
        xp = xdt[:, j * CHUNK:(j + 1) * CHUNK]
        rhs = jnp.concatenate([jnp.where(k["lo"], xp, 0.0), jnp.where(k["lo"], 0.0, xp)], axis=0)
        ys.append(mm_nn(jnp.concatenate(ms, axis=1), rhs))
    y_diag = jnp.concatenate(ys, axis=1)
    y_off = jnp.exp(acs_e) * mm_nn(cm, ht)
    h_new = jnp.exp(tot_e) * ht + mm_tn(bm, xdt * jnp.exp(tot_e - acs_e))
    return y_diag + y_off + lanes(dsk) * xs, h_new


def _ssd_in_specs(nc, rev):
    c_of = (lambda c: nc - 1 - c) if rev else (lambda c: c)
    per = CHUNK // CONV_HALO
    zx = [pl.BlockSpec((CHUNK, GROUP_COLS), lambda g, c: (c_of(c), g)),
          pl.BlockSpec((CONV_HALO, GROUP_COLS), lambda g, c: (jnp.maximum(c_of(c) * per - 1, 0), g))]
    conv = [pl.BlockSpec((4, GROUP_CONV), lambda g, c: (0, g)), pl.BlockSpec((1, GROUP_CONV), lambda g, c: (0, g))]
    head = [pl.BlockSpec((None, 1, 128), lambda g, c: (g, 0, 0))] * 3
    return zx + conv + head, c_of


def _load_chunk_args(refs, has_prev):
    raw, rawp, cw, cb_, dtb, alog, dsk = refs
    return (raw[...], rawp[...] * has_prev, tuple(cw[pl.ds(i, 1), :] for i in range(4)), cb_[...],
            dtb[...], alog[...], dsk[...])


def _ssd_fwd(zx, conv_w, conv_b, dtb, alog, dsk, carried=None):
    t = zx.shape[0]
    nc = t // CHUNK
    in_specs, _ = _ssd_in_specs(nc, False)

    def body(*refs):
        ins, (y_ref, hs_ref, ht) = refs[:7], refs[7:]
        c = pl.program_id(1)

        @pl.when(c == 0)
        def _():
            ht[...] = jnp.zeros_like(ht)

        a = _load_chunk_args(ins, _one(c > 0))
        h_in = ht[...]
        y, h_new = _ssd_chunk(*a[:2], h_in, *a[2:], _ssd_consts())
        y_ref[...] = y
        hs_ref[...] = h_in
        ht[...] = h_new

    return _pcall(
        body, name="ssd_fwd", grid=(N_GROUPS, nc),
        in_specs=in_specs,
        out_specs=[pl.BlockSpec((CHUNK, GROUP_X), lambda g, c: (c, g)),
                   pl.BlockSpec((None, None, D_STATE, GROUP_X), lambda g, c: (g, c, 0, 0))],
        out_shape=[jax.ShapeDtypeStruct((t, D_INNER), F32),
                   jax.ShapeDtypeStruct((N_GROUPS, nc, D_STATE, GROUP_X), F32)],
        scratch_shapes=[pltpu.VMEM((D_STATE, GROUP_X), F32)],
        sem=("arbitrary", "arbitrary"), args=(zx, zx, conv_w, conv_b, dtb, alog, dsk), carried=carried)


def _ssd_bwd(zx, conv_w, conv_b, dtb, alog, dsk, hs, dy, dzx, carried=None):
    t = zx.shape[0]
    nc = t // CHUNK
    in_specs, c_of = _ssd_in_specs(nc, True)
    n_in = 10

    def body(*refs):
        ins, hs_ref, dy_ref = refs[:7], refs[7], refs[8]
        (draw_ref, dcw, dcb, ddtb, dalog, ddsk, dht, carry) = refs[n_in:]
        cc = pl.program_id(1)
        accs = (dcw, dcb, ddtb, dalog, ddsk)

        @pl.when(cc == 0)
        def _():
            for r in (dht, carry) + accs:
                r[...] = jnp.zeros_like(r)

        has_prev = _one(c_of(cc) > 0)
        a = _load_chunk_args(ins, has_prev)
        k = _ssd_consts()
        fn = lambda *args: _ssd_chunk(*args, k)
        _, vjp = jax.vjp(fn, *a[:2], hs_ref[...], *a[2:])
        graw, grawp, ght, gcw, gcb, gdtb, galog, gdsk = vjp((dy_ref[...], dht[...]))
        tail = jnp.concatenate([jnp.zeros((CHUNK - CONV_HALO, GROUP_COLS), F32), carry[...]], axis=0)
        draw_ref[...] = (graw + tail).astype(BF16)
        carry[...] = grawp * has_prev
        dht[...] = ght
        for i in range(4):
            dcw[pl.ds(i, 1), :] += gcw[i]
        for ref, val in ((dcb, gcb), (ddtb, gdtb), (dalog, galog), (ddsk, gdsk)):
            ref[...] += val

    head_out = pl.BlockSpec((None, 1, 128), lambda g, c: (g, 0, 0))
    sds = jax.ShapeDtypeStruct
    return _pcall(
        body, name="ssd_bwd", grid=(N_GROUPS, nc),
        in_specs=in_specs + [
            pl.BlockSpec((None, None, D_STATE, GROUP_X), lambda g, c: (g, c_of(c), 0, 0)),
            pl.BlockSpec((CHUNK, GROUP_X), lambda g, c: (c_of(c), g)),
            _ANY],
        out_specs=[pl.BlockSpec((CHUNK, GROUP_COLS), lambda g, c: (c_of(c), g)),
                   pl.BlockSpec((4, GROUP_CONV), lambda g, c: (0, g)),
                   pl.BlockSpec((1, GROUP_CONV), lambda g, c: (0, g)),
                   head_out, head_out, head_out],
        out_shape=[sds((t, ZX_COLS), BF16), sds((4, N_GROUPS * GROUP_CONV), F32), sds((1, N_GROUPS * GROUP_CONV), F32),
                   sds((N_GROUPS, 1, 128), F32), sds((N_GROUPS, 1, 128), F32), sds((N_GROUPS, 1, 128), F32)],
        scratch_shapes=[pltpu.VMEM((D_STATE, GROUP_X), F32), pltpu.VMEM((CONV_HALO, GROUP_COLS), F32)],
        sem=("arbitrary", "arbitrary"), args=(zx, zx, conv_w, conv_b, dtb, alog, dsk, hs, dy, dzx),
        aliases={9: 0}, carried=carried)


def _gate_norm(y, zs, ng):
    outs = []
    for k in range(N_GROUPS):
        s = y[:, k * GROUP_X:(k + 1) * GROUP_X] * _silu(zs[k])
        outs.append(s * lax.rsqrt(jnp.mean(s * s, axis=-1, keepdims=True) + RMS_EPS))
    return jnp.concatenate(outs, axis=1) * ng


def _z_specs(tt):
    first = Z_OFF // GROUP_X
    return [pl.BlockSpec((tt, GROUP_X), functools.partial(lambda k, i: (i, first + k), k)) for k in range(N_GROUPS)]


def _ssm_out_fwd(y, zx, ng, w_out, h):
    t = h.shape[0]
    tt = min(t, 512)

    def body(y_ref, z0, z1, z2, z3, ng_ref, w_ref, h_ref, o_ref):
        yn = _gate_norm(y_ref[...], (z0[...], z1[...], z2[...], z3[...]), ng_ref[...])
        o_ref[...] = h_ref[...] + jnp.dot(yn.astype(BF16), w_ref[...], preferred_element_type=F32)

    return pl.pallas_call(
        body, name="ssm_out_fwd", grid=(t // tt,),
        in_specs=[pl.BlockSpec((tt, D_INNER), lambda i: (i, 0))] + _z_specs(tt) + [
            pl.BlockSpec((1, D_INNER), lambda i: (0, 0)),
            pl.BlockSpec((D_INNER, D_MODEL), lambda i: (0, 0)),
            pl.BlockSpec((tt, D_MODEL), lambda i: (i, 0))],
        out_specs=pl.BlockSpec((tt, D_MODEL), lambda i: (i, 0)),
        out_shape=jax.ShapeDtypeStruct((t, D_MODEL), F32),
        compiler_params=_cp(("arbitrary",)),
    )(y, zx, zx, zx, zx, ng, w_out, h)


def _gate_norm_group(y, z, ng):
    s = y * _silu(z)
    return s * lax.rsqrt(jnp.mean(s * s, axis=-1, keepdims=True) + RMS_EPS) * ng


def _ssm_out_bwd(dhb, y, zx, ng, w_out):
    t = dhb.shape[0]
    tt = min(t, MATMUL_TOKENS)
    first = Z_OFF // GROUP_X

    def body(dh_ref, y_ref, z_ref, ng_ref, w_ref, dy_ref, dzx_ref, yn_ref, dng_ref):
        @pl.when(pl.program_id(1) == 0)
        def _():
            dng_ref[...] = jnp.zeros_like(dng_ref)

        dyn = lax.dot_general(dh_ref[...], w_ref[...], _NT, preferred_element_type=F32)
        yn, vjp = jax.vjp(_gate_norm_group, y_ref[...], z_ref[...], ng_ref[...])
        dy, dz, dng = vjp(dyn)
        dy_ref[...] = dy
        dzx_ref[...] = dz.astype(BF16)
        yn_ref[...] = yn.astype(BF16)
        dng_ref[...] += dng

    grp = pl.BlockSpec((tt, GROUP_X), lambda k, i: (i, k))
    zgrp = pl.BlockSpec((tt, GROUP_X), lambda k, i: (i, first + k))
    gain = pl.BlockSpec((1, GROUP_X), lambda k, i: (0, k))
    return pl.pallas_call(
        body, name="ssm_out_bwd", grid=(N_GROUPS, t // tt),
        in_specs=[pl.BlockSpec((tt, D_MODEL), lambda k, i: (i, 0)), grp, zgrp, gain,
                  pl.BlockSpec((GROUP_X, D_MODEL), lambda k, i: (k, 0))],
        out_specs=[grp, zgrp, grp, gain],
        out_shape=[jax.ShapeDtypeStruct((t, D_INNER), F32), jax.ShapeDtypeStruct((t, ZX_COLS), BF16),
                   jax.ShapeDtypeStruct((t, D_INNER), BF16), jax.ShapeDtypeStruct((1, D_INNER), F32)],
        compiler_params=_cp(("arbitrary", "arbitrary")),
    )(dhb, y, zx, ng, w_out)


def _final(h, g, tgt):
    t = h.shape[0]
    tt = min(t, 512)
    nt = t // tt

    def body(h_ref, g_ref, t_ref, dh_ref, dhb_ref, loss_ref, dg_ref, lacc):
        i = pl.program_id(0)

        @pl.when(i == 0)
        def _():
            dg_ref[...] = jnp.zeros_like(dg_ref)
            lacc[...] = jnp.zeros_like(lacc)

        gv = g_ref[...]
        y, n, r = _rms_fwd(h_ref[...], gv)
        err = y - t_ref[...]
        lacc[...] += jnp.sum(err * err, axis=0, keepdims=True)
        dx, dg = _rms_bwd(err * (1.0 / D_MODEL), n, r, gv)
        dh_ref[...] = dx
        dhb_ref[...] = dx.astype(BF16)
        dg_ref[...] += dg

        @pl.when(i == nt - 1)
        def _():
            loss_ref[...] = jnp.zeros_like(loss_ref) + (0.5 / D_MODEL) * jnp.sum(lacc[...])

    tile = pl.BlockSpec((tt, D_MODEL), lambda i: (i, 0))
    vec = pl.BlockSpec((1, D_MODEL), lambda i: (0, 0))
    return pl.pallas_call(
        body, name="final_loss", grid=(nt,),
        in_specs=[tile, vec, tile],
        out_specs=[tile, tile, pl.BlockSpec((1, 128), lambda i: (0, 0)), vec],
        out_shape=[jax.ShapeDtypeStruct((t, D_MODEL), F32), jax.ShapeDtypeStruct((t, D_MODEL), BF16),
                   jax.ShapeDtypeStruct((1, 128), F32), jax.ShapeDtypeStruct((1, D_MODEL), F32)],
        scratch_shapes=[pltpu.VMEM((1, D_MODEL), F32)],
        compiler_params=_cp(("arbitrary",)),
    )(h, g, tgt)


def _adamw_reduced_parts(w, lands, m, v, name):
    rows, cols = w.shape
    br = 128
    nl = lands[0].shape[0]
    starts, blocks = [], []
    for land in lands:
        starts.append(sum(blocks))
        blocks.append(land.shape[1] // br)

    def body(w_ref, *refs):
        l_refs, (m_ref, v_ref, g_ref, d_ref, m2_ref, v2_ref) = refs[:len(lands)], refs[len(lands):]
        i = pl.program_id(0)
        gv = None
        for ref, first in zip(l_refs, starts):
            acc = ref[0].astype(F32)
            for q in range(1, nl):
                acc = acc + ref[q].astype(F32)
            gv = acc if gv is None else jnp.where(i >= first, acc, gv)
        g_ref[...] = gv
        d_ref[...], m2_ref[...], v2_ref[...] = _adamw_math(w_ref[...], gv, m_ref[...], v_ref[...])

    spec = pl.BlockSpec((br, cols), lambda i: (i, 0))
    land_specs = [pl.BlockSpec((nl, br, cols), functools.partial(
        lambda first, nb, i: (0, jnp.clip(i - first, 0, nb - 1), 0), first, nb)) for first, nb in zip(starts, blocks)]
    out = jax.ShapeDtypeStruct((rows, cols), F32)
    return pl.pallas_call(
        body, name=name, grid=(rows // br,),
        in_specs=[spec] + land_specs + [spec, spec], out_specs=[spec] * 4, out_shape=[out] * 4,
        compiler_params=_cp(("arbitrary",)),
    )(w, *lands, m, v)


def _adamw_reduced_layers(w, lands, m, v, name):
    _, rows, cols = w.shape
    br = rows if rows <= 256 else 256
    nb = rows // br
    nl = lands[0].shape[0]

    def body(w_ref, l0_ref, l1_ref, m_ref, v_ref, g_ref, d_ref, m2_ref, v2_ref):
        def total(ref):
            acc = ref[0].astype(F32)
            for q in range(1, nl):
                acc = acc + ref[q].astype(F32)
            return acc

        gv = jnp.where(pl.program_id(0) == 0, total(l0_ref), total(l1_ref))
        g_ref[...] = gv
        d_ref[...], m2_ref[...], v2_ref[...] = _adamw_math(w_ref[...], gv, m_ref[...], v_ref[...])

    spec = pl.BlockSpec((None, br, cols), lambda l, i: (l, i, 0))
    land0 = pl.BlockSpec((nl, br, cols), lambda l, i: (0, jnp.where(l == 0, i, nb - 1), 0))
    land1 = pl.BlockSpec((nl, br, cols), lambda l, i: (0, jnp.where(l == 1, i, 0), 0))
    out = jax.ShapeDtypeStruct(w.shape, F32)
    return pl.pallas_call(
        body, name=name, grid=(2, nb),
        in_specs=[spec, land0, land1, spec, spec], out_specs=[spec] * 4, out_shape=[out] * 4,
        compiler_params=_cp(("arbitrary", "arbitrary")),
    )(w, lands[0], lands[1], m, v)


def _all_reduce_small(sp, carried):
    rows, n = sp.shape
    ci, co = len(carried.ins), len(carried.outs)

    def body(*refs):
        x_ref, cins, o_ref, couts = refs[0], refs[1:1 + ci], refs[1 + ci], refs[2 + ci:2 + ci + co]
        land, send_sems, recv_sems = refs[2 + ci + co:5 + ci + co]
        csems = refs[5 + ci + co:]
        carried.start(cins, couts, csems)
        x, y, c = _place()
        me = 4 * x + 2 * y + c
        land[me] = x_ref[...]
        cps = []
        for rel in range(1, N_DEV):
            dx, dy, dc = (rel >> 2) & 1, (rel >> 1) & 1, rel & 1
            px = x + dx - 2 * x * dx
            py = y + dy - 2 * y * dy
            pc = c + dc - 2 * c * dc
            peer = 4 * px + 2 * py + pc
            cps.append((pltpu.make_async_remote_copy(
                src_ref=x_ref, dst_ref=land.at[me], send_sem=send_sems.at[rel - 1], recv_sem=recv_sems.at[rel - 1],
                device_id=(px, py, pc), device_id_type=MESH),
                pltpu.make_async_remote_copy(
                src_ref=x_ref, dst_ref=land.at[peer], send_sem=send_sems.at[rel - 1], recv_sem=recv_sems.at[rel - 1],
                device_id=(px, py, pc), device_id_type=MESH)))
        for cp, _ in cps:
            cp.start()
        for _, arr in cps:
            arr.wait_recv()
        for cp, _ in cps:
            cp.wait_send()
        acc = land[0]
        for k in range(1, N_DEV):
            acc = acc + land[k]
        o_ref[...] = acc
        carried.finish(cins, couts, csems)

    vm = pl.BlockSpec(memory_space=pltpu.VMEM)
    res = pl.pallas_call(
        body, name="all_reduce_small",
        out_shape=[jax.ShapeDtypeStruct((rows, n), F32)] + carried.outs,
        in_specs=[vm] + [_ANY] * ci, out_specs=[vm] + [_ANY] * co,
        scratch_shapes=[pltpu.VMEM((N_DEV, rows, n), F32),
                        pltpu.SemaphoreType.DMA((N_DEV - 1,)), pltpu.SemaphoreType.DMA((N_DEV - 1,))] + carried.sems,
    )(sp, *carried.ins)
    return res[0], list(res[1:])


def _adamw_math(wv, gv, mv, vv):
    m2 = ADAM_B1 * mv + (1.0 - ADAM_B1) * gv
    v2 = ADAM_B2 * vv + (1.0 - ADAM_B2) * (gv * gv)
    m_hat = m2 / (1.0 - ADAM_B1 ** ADAM_STEP)
    v_hat = v2 / (1.0 - ADAM_B2 ** ADAM_STEP)
    return -ADAM_LR * (m_hat / (jnp.sqrt(v_hat) + ADAM_EPS) + ADAM_WD * wv), m2, v2


def _adamw(w, g, m, v, name):
    rows, cols = w.shape
    br = rows if rows <= 256 else 256

    def body(w_ref, g_ref, m_ref, v_ref, d_ref, m2_ref, v2_ref):
        d_ref[...], m2_ref[...], v2_ref[...] = _adamw_math(w_ref[...], g_ref[...], m_ref[...], v_ref[...])

    spec = pl.BlockSpec((br, cols), lambda i: (i, 0))
    out = jax.ShapeDtypeStruct((rows, cols), F32)
    return pl.pallas_call(
        body, name=name, grid=(rows // br,),
        in_specs=[spec] * 4, out_specs=[spec] * 3, out_shape=[out] * 3,
        compiler_params=_cp(("arbitrary",)),
    )(w, g, m, v)


def _adamw_reduced(w, land, m, v, name):
    rows, cols = w.shape
    br = rows if rows <= 256 else 256
    nl = land.shape[0]

    def body(w_ref, l_ref, m_ref, v_ref, g_ref, d_ref, m2_ref, v2_ref):
        gv = l_ref[0].astype(F32)
        for q in range(1, nl):
            gv = gv + l_ref[q].astype(F32)
        g_ref[...] = gv
        d_ref[...], m2_ref[...], v2_ref[...] = _adamw_math(w_ref[...], gv, m_ref[...], v_ref[...])

    spec = pl.BlockSpec((br, cols), lambda i: (i, 0))
    out = jax.ShapeDtypeStruct((rows, cols), F32)
    return pl.pallas_call(
        body, name=name, grid=(rows // br,),
        in_specs=[spec, pl.BlockSpec((nl, br, cols), lambda i: (0, i, 0)), spec, spec],
        out_specs=[spec] * 4, out_shape=[out] * 4,
        compiler_params=_cp(("arbitrary",)),
    )(w, land, m, v)


def _zx_source_col(col):
    blk = jnp.right_shift(col, 7)
    lane = jnp.bitwise_and(col, 127)
    per = GROUP_COLS // 128
    grp = jnp.where(blk >= per, 1, 0) + jnp.where(blk >= 2 * per, 1, 0) + jnp.where(blk >= 3 * per, 1, 0)
    o = blk - per * grp
    x_col = D_INNER + GROUP_X * grp + 128 * o + lane
    b_col = 2 * D_INNER + D_STATE * grp + lane
    c_col = 2 * D_INNER + N_GROUPS * D_STATE + D_STATE * grp + lane
    dt_col = jnp.where(lane < HEADS_PER_GROUP, D_INNER + CONV_DIM + HEADS_PER_GROUP * grp + lane, -1)
    src = jnp.where(o < 4, x_col, jnp.where(o == 4, b_col, jnp.where(o == 5, c_col, dt_col)))
    return jnp.where(col >= Z_OFF, col - Z_OFF, src)


def _zx_source_col_py(col):
    if col >= Z_OFF:
        return col - Z_OFF
    grp, o = divmod(col, GROUP_COLS)
    if o < GROUP_X:
        return D_INNER + GROUP_X * grp + o
    if o < GROUP_X + D_STATE:
        return 2 * D_INNER + D_STATE * grp + (o - GROUP_X)
    if o < GROUP_CONV:
        return 2 * D_INNER + N_GROUPS * D_STATE + D_STATE * grp + (o - GROUP_X - D_STATE)
    h = o - GROUP_CONV
    return D_INNER + CONV_DIM + HEADS_PER_GROUP * grp + h if h < HEADS_PER_GROUP else -1


def _overlap_tables():
    nblk = ZX_COLS // COL_BLK
    src = [_zx_source_col_py(c) for c in range(ZX_COLS)]
    fwd = [sorted({s // W_IN_SHARD for s in src[COL_BLK * j:COL_BLK * (j + 1)] if s >= 0}) for j in range(nblk)]
    dst = {s: c for c, s in enumerate(src) if s >= 0}
    bwd = [sorted({dst[s] // COL_BLK for s in range(W_IN_SHARD * k, W_IN_SHARD * (k + 1))}) for k in range(N_DEV)]

    def flat(rows):
        width = max(len(r) for r in rows)
        idx = [r + [r[-1]] * (width - len(r)) for r in rows]
        val = [[1] * len(r) + [0] * (width - len(r)) for r in rows]
        return (jnp.asarray(sum(idx, []), jnp.int32), jnp.asarray(sum(val, []), jnp.int32), width)

    return flat(fwd), flat(bwd)


def _w_in_to_zx(w_in_g):
    (tab, val, width), _ = _overlap_tables()
    nblk = ZX_COLS // COL_BLK

    def body(tab_ref, val_ref, w_ref, o_ref, acc):
        j = pl.program_id(0)
        s = pl.program_id(1)

        @pl.when(s == 0)
        def _():
            acc[...] = jnp.zeros_like(acc)

        @pl.when(val_ref[j * width + s] == 1)
        def _():
            k = tab_ref[j * width + s]
            col = COL_BLK * j + lax.broadcasted_iota(jnp.int32, (8, COL_BLK), 1)
            src = jnp.broadcast_to(_zx_source_col(col)[0:1, :], (W_IN_SHARD, COL_BLK))
            row = W_IN_SHARD * k + lax.broadcasted_iota(jnp.int32, (W_IN_SHARD, COL_BLK), 0)
            place = _one(src == row).astype(BF16)
            acc[...] += jnp.dot(w_ref[...], place, preferred_element_type=F32)

        @pl.when(s == width - 1)
        def _():
            o_ref[...] = acc[...].astype(BF16)

    return pl.pallas_call(
        body, name="w_in_to_zx",
        grid_spec=pltpu.PrefetchScalarGridSpec(
            num_scalar_prefetch=2, grid=(nblk, width),
            in_specs=[pl.BlockSpec((None, D_MODEL, W_IN_SHARD), lambda j, s, tab, val: (tab[j * width + s], 0, 0))],
            out_specs=pl.BlockSpec((D_MODEL, COL_BLK), lambda j, s, tab, val: (0, j)),
            scratch_shapes=[pltpu.VMEM((D_MODEL, COL_BLK), F32)]),
        out_shape=jax.ShapeDtypeStruct((D_MODEL, ZX_COLS), BF16),
        compiler_params=_cp(("arbitrary", "arbitrary")),
    )(tab, val, w_in_g)


def _zx_to_w_in(d_wzx):
    _, (tab, val, width) = _overlap_tables()

    def body(tab_ref, val_ref, d_ref, o_ref, acc):
        k = pl.program_id(0)
        s = pl.program_id(1)

        @pl.when(s == 0)
        def _():
            acc[...] = jnp.zeros_like(acc)

        @pl.when(val_ref[k * width + s] == 1)
        def _():
            j = tab_ref[k * width + s]
            col = COL_BLK * j + lax.broadcasted_iota(jnp.int32, (COL_BLK, 128), 0)
            src = jnp.broadcast_to(_zx_source_col(col)[:, 0:1], (COL_BLK, W_IN_SHARD))
            row = W_IN_SHARD * k + lax.broadcasted_iota(jnp.int32, (COL_BLK, W_IN_SHARD), 1)
            place = _one(src == row).astype(BF16)
            acc[...] += jnp.dot(d_ref[...], place, preferred_element_type=F32)

        @pl.when(s == width - 1)
        def _():
            o_ref[...] = acc[...].astype(BF16)

    return pl.pallas_call(
        body, name="zx_to_w_in",
        grid_spec=pltpu.PrefetchScalarGridSpec(
            num_scalar_prefetch=2, grid=(N_DEV, width),
            in_specs=[pl.BlockSpec((D_MODEL, COL_BLK), lambda k, s, tab, val: (0, tab[k * width + s]))],
            out_specs=pl.BlockSpec((None, D_MODEL, W_IN_SHARD), lambda k, s, tab, val: (k, 0, 0)),
            scratch_shapes=[pltpu.VMEM((D_MODEL, W_IN_SHARD), F32)]),
        out_shape=jax.ShapeDtypeStruct((N_DEV, D_MODEL, W_IN_SHARD), BF16),
        compiler_params=_cp(("arbitrary", "arbitrary")),
    )(tab, val, d_wzx)


def _group_conv_cols(a):
    rows = a.shape[0]
    x = a[:, :D_INNER].reshape(rows, N_GROUPS, GROUP_X)
    b = a[:, D_INNER:D_INNER + N_GROUPS * D_STATE].reshape(rows, N_GROUPS, D_STATE)
    c = a[:, D_INNER + N_GROUPS * D_STATE:].reshape(rows, N_GROUPS, D_STATE)
    return jnp.concatenate([x, b, c], axis=2).reshape(rows, N_GROUPS * GROUP_CONV)


def _ungroup_conv_cols(a):
    rows = a.shape[0]
    a3 = a.reshape(rows, N_GROUPS, GROUP_CONV)
    return jnp.concatenate([a3[:, :, :GROUP_X].reshape(rows, D_INNER),
                            a3[:, :, GROUP_X:GROUP_X + D_STATE].reshape(rows, N_GROUPS * D_STATE),
                            a3[:, :, GROUP_X + D_STATE:].reshape(rows, N_GROUPS * D_STATE)], axis=1)


def _small_shard(conv_w, conv_b, norm_g):
    ng = jnp.pad(norm_g.reshape(1, -1), ((0, 0), (0, CONV_SHARD - norm_g.shape[-1])))
    return jnp.concatenate([conv_w.reshape(4, CONV_SHARD), conv_b.reshape(1, CONV_SHARD), ng,
                            jnp.zeros((SMALL_ROWS - 6, CONV_SHARD), F32)], axis=0)


def _small_unshard(a):
    return a[0:4].reshape(1, 4, CONV_SHARD), a[4:5], a[5:6, :D_INNER // N_DEV]


def _heads_of(a):
    return a[:, :, :HEADS_PER_GROUP].reshape(1, N_HEADS)


def _head_params(p):
    return jnp.pad(p.reshape(N_GROUPS, 1, HEADS_PER_GROUP), ((0, 0), (0, 0), (0, 128 - HEADS_PER_GROUP)))


def _update(w, land, m, v, name):
    shp = w.shape
    to2 = lambda a: a.reshape(-1, shp[-1])
    return tuple(o.reshape(shp) for o in _adamw_reduced(to2(w), land, to2(m), to2(v), name))


def kernel(x, norm_mix_g, norm_mlp_g, pool_w, pool_b, pool_scale, ssm_w_in, ssm_conv_w, ssm_conv_b, ssm_dt_bias, ssm_a_log, ssm_d, ssm_norm_g, ssm_w_out, mlp_w1, mlp_w2, final_g, loss_target, m_norm_mix_g, m_norm_mlp_g, m_pool_w, m_pool_b, m_pool_scale, m_ssm_w_in, m_ssm_conv_w, m_ssm_conv_b, m_ssm_dt_bias, m_ssm_a_log, m_ssm_d, m_ssm_norm_g, m_ssm_w_out, m_mlp_w1, m_mlp_w2, m_final_g, v_norm_mix_g, v_norm_mlp_g, v_pool_w, v_pool_b, v_pool_scale, v_ssm_w_in, v_ssm_conv_w, v_ssm_conv_b, v_ssm_dt_bias, v_ssm_a_log, v_ssm_d, v_ssm_norm_g, v_ssm_w_out, v_mlp_w1, v_mlp_w2, v_final_g):
    x2 = x[0]
    tgt = loss_target[0]
    gm0, gm1 = norm_mix_g[0:1], norm_mix_g[1:2]
    gl0, gl1 = norm_mlp_g[0:1], norm_mlp_g[1:2]
    gfin = final_g.reshape(1, D_MODEL)

    fb = D_FF // N_DEV

    def bf(a):
        return a.astype(BF16)

    def gather_of(shards):
        return _direct_exchange(shards, [(i, 0) for i in range(len(shards))],
                                [(s.shape, s.dtype) for s in shards], scatter=False)

    def scatter_of(parts, rows=None):
        shapes = [((p.shape[1] if rows is None else rows[1], p.shape[2]), p.dtype) for p in parts]
        return _direct_exchange(parts, [(i, 0) for i in range(len(parts))], shapes, scatter=True,
                                src_rows=None if rows is None else [rows] * len(parts))

    w_pool, small_g = _run_exchange(_two_level_gather(
        [bf(pool_w.reshape(4 * POOL_SHARD, POOL_GROUP)), _small_shard(ssm_conv_w, ssm_conv_b, ssm_norm_g)]),
        "gather_first")
    conv_w = _group_conv_cols(small_g[:, 0:4].transpose(1, 0, 2).reshape(4, CONV_DIM))
    conv_b = _group_conv_cols(small_g[:, 4].reshape(1, CONV_DIM))
    ssm_ng = small_g[:, 5, :D_INNER // N_DEV].reshape(1, D_INNER)
    dtb, alog, dsk = _head_params(ssm_dt_bias), _head_params(ssm_a_log), _head_params(ssm_d)

    (h1,), (w1g0, w2g0) = _pool_fwd(x2, gm0, w_pool, pool_b, pool_scale,
                                    carried=_two_level_gather([bf(mlp_w1[0]), bf(mlp_w2[0])], mid_percent=100))
    (h2, u0, hm0), (w_in_g,) = _mlp_fwd(h1, gl0, w1g0, w2g0, "mlp0_fwd",
                                        carried=_two_level_gather([bf(ssm_w_in[0])]))
    w_zx = _w_in_to_zx(w_in_g)
    (zx, hn1), (w_out_g,) = _norm_matmul(h2, gm1, w_zx, carried=gather_of([bf(ssm_w_out[0])]))
    (y_ssd, states), (w1g1, w2g1) = _ssd_fwd(zx, conv_w, conv_b, dtb, alog, dsk,
                                             carried=_two_level_gather([bf(mlp_w1[1]), bf(mlp_w2[1])]))
    w_out = w_out_g.reshape(D_INNER, D_MODEL)
    h3 = _ssm_out_fwd(y_ssd, zx, ssm_ng, w_out, h2)
    (h4, u1, hm1), _ = _mlp_fwd(h3, gl1, w1g1, w2g1, "mlp1_fwd")
    dh4, dh4b, loss_row, d_gfin = _final(h4, gfin, tgt)

    (dh3, dh3b, da1, d_gl1), _ = _mlp_bwd(dh4, dh4b, h3, gl1, u1, w1g1, w2g1, "mlp1_bwd")
    d_w1_1 = _matmul_tn(hm1, da1, "mlp1_dw1", col_blocked=True, scale=2.0)
    d_w2_1 = _matmul_tn(u1, dh4b, "mlp1_dw2", square_a=True).reshape(N_DEV, fb, D_MODEL)
    dy_ssd, dzx, yn, d_ng = _ssm_out_bwd(dh3b, y_ssd, zx, ssm_ng, w_out)
    d_wout = _matmul_tn(yn, dh3b, "ssm_dw_out").reshape(N_DEV, D_INNER // N_DEV, D_MODEL)
    (dzx, d_cw, d_cb, d_dtb, d_alog, d_dsk), (l_w1_1, l_w2_1, l_wout) = _ssd_bwd(
        zx, conv_w, conv_b, dtb, alog, dsk, states, dy_ssd, dzx, carried=scatter_of([d_w1_1, d_w2_1, d_wout]))
    d_w_in = _zx_to_w_in(_matmul_tn(hn1, dzx, "ssm_dw_in"))
    most = 5 * D_MODEL // 8
    (dh2, dh2b, d_gm1), (l_w_in_a,) = _in_proj_bwd(dzx, w_zx, h2, gm1, dh3, carried=scatter_of([d_w_in], (0, most)))
    d_w2_0, (l_w_in_b,) = _matmul_tn(u0, dh2b, "mlp0_dw2", square_a=True,
                                     carried=scatter_of([d_w_in], (most, D_MODEL - most)))
    d_w2_0 = d_w2_0.reshape(N_DEV, fb, D_MODEL)
    (dh1, _, da0, d_gl0), (l_w2_0,) = _mlp_bwd(dh2, dh2b, h1, gl0, u0, w1g0, w2g0, "mlp0_bwd",
                                           carried=scatter_of([d_w2_0]))
    d_w1_0 = _matmul_tn(hm0, da0, "mlp0_dw1", col_blocked=True, scale=2.0)
    (dx, d_pool, d_pb, d_ps, d_gm0), (l_w1_0,) = _pool_bwd(x2, dh1, gm0, w_pool, pool_b, pool_scale,
                                                          carried=scatter_of([d_w1_0]))

    d_conv_w = _ungroup_conv_cols(d_cw).reshape(4, N_DEV, CONV_SHARD).transpose(1, 0, 2)
    d_conv_b = _ungroup_conv_cols(d_cb).reshape(N_DEV, 1, CONV_SHARD)
    d_gain = jnp.pad(d_ng.reshape(N_DEV, 1, D_INNER // N_DEV), ((0, 0), (0, 0), (0, CONV_SHARD - D_INNER // N_DEV)))
    d_small = jnp.concatenate([d_conv_w, d_conv_b, d_gain,
                               jnp.zeros((N_DEV, SMALL_ROWS - 6, CONV_SHARD), F32)], axis=1)

    heads = jnp.concatenate([_heads_of(a) for a in (d_dtb, d_alog, d_dsk)] + [loss_row[:, 0:1]], axis=1)
    sp = jnp.concatenate([d_gm0, d_gm1, d_gl0, d_gl1, d_pb, d_ps, d_gfin,
                          jnp.pad(heads, ((0, 0), (0, D_MODEL - 3 * N_HEADS - 1)))], axis=0)
    sg, (l_pool, l_small) = _all_reduce_small(sp, scatter_of([bf(d_pool), d_small]))

    g_norm_mix = sg[0:2]
    g_norm_mlp = sg[2:4]
    g_pool_b, g_pool_scale = sg[4:5], sg[5:6]
    g_final = sg[6]
    g_dtb, g_alog, g_dsk = sg[7:8, 0:32], sg[7:8, 32:64], sg[7:8, 64:96]

    def rep_pack(nm, nl, pb, ps, fg, db, al, dk):
        hd = jnp.pad(jnp.concatenate([db, al, dk], axis=1), ((0, 0), (0, D_MODEL - 3 * N_HEADS)))
        return jnp.concatenate([nm, nl, pb, ps, fg.reshape(1, D_MODEL), hd], axis=0)

    rep = [rep_pack(*t) for t in (
        (norm_mix_g, norm_mlp_g, pool_b, pool_scale, final_g, ssm_dt_bias, ssm_a_log, ssm_d),
        (g_norm_mix, g_norm_mlp, g_pool_b, g_pool_scale, g_final, g_dtb, g_alog, g_dsk),
        (m_norm_mix_g, m_norm_mlp_g, m_pool_b, m_pool_scale, m_final_g, m_ssm_dt_bias, m_ssm_a_log, m_ssm_d),
        (v_norm_mix_g, v_norm_mlp_g, v_pool_b, v_pool_scale, v_final_g, v_ssm_dt_bias, v_ssm_a_log, v_ssm_d))]
    rep_out = _adamw(*rep, "adamw_replicated")

    def rep_unpack(a):
        return (a[0:2], a[2:4], a[4:5], a[5:6], a[6], a[7:8, 0:32], a[7:8, 32:64], a[7:8, 64:96])

    sm_out = _adamw_reduced(_small_shard(ssm_conv_w, ssm_conv_b, ssm_norm_g), l_small,
                            _small_shard(m_ssm_conv_w, m_ssm_conv_b, m_ssm_norm_g),
                            _small_shard(v_ssm_conv_w, v_ssm_conv_b, v_ssm_norm_g), "adamw_small_shards")

    big = {
        "pool_w": _update(pool_w, l_pool, m_pool_w, v_pool_w, "adamw_pool_w"),
        "ssm_w_in": tuple(o.reshape(ssm_w_in.shape) for o in _adamw_reduced_parts(
            ssm_w_in[0], (l_w_in_a, l_w_in_b), m_ssm_w_in[0], v_ssm_w_in[0], "adamw_w_in")),
        "ssm_w_out": _update(ssm_w_out, l_wout, m_ssm_w_out, v_ssm_w_out, "adamw_w_out"),
        "mlp_w1": _adamw_reduced_layers(mlp_w1, (l_w1_0, l_w1_1), m_mlp_w1, v_mlp_w1, "adamw_w1"),
        "mlp_w2": _adamw_reduced_layers(mlp_w2, (l_w2_0, l_w2_1), m_mlp_w2, v_mlp_w2, "adamw_w2"),
    }
    rep_all = (rep[1],) + tuple(rep_out)

    def ordered(kind):
        nm, nl, pb, ps, fg, db, al, dk = rep_unpack(rep_all[kind])
        cw, cb, ng = _small_unshard(sm_out[kind])
        return [nm, nl, big["pool_w"][kind], pb, ps, big["ssm_w_in"][kind], cw, cb, db, al, dk, ng,
                big["ssm_w_out"][kind], big["mlp_w1"][kind], big["mlp_w2"][kind], fg]

    loss = sg[7, 3 * N_HEADS]
    return (loss, dx[None], *ordered(0), *ordered(1), *ordered(2), *ordered(3))
```
